```python
import math
import jax
import jax.numpy as jnp
from jax import lax
import numpy as np

D_MODEL = 1024
BATCH = 8
SEQ = 4096
DEPTH = 2

EPS = 1e-6
PLE_DIM = 256
D_FF = 2816

CONV_CH = 1024
CONV_WIDTH = 31

SSM_HEADS = 16
SSM_HEAD_DIM = 64
SSM_INNER = SSM_HEADS * SSM_HEAD_DIM
SSM_GROUPS = 2
SSM_STATE = 128
SSM_CONV = 4
SSM_CHUNK = 128
SSM_XBC = SSM_INNER + 2 * SSM_GROUPS * SSM_STATE

HYB_IN = 2 * CONV_CH + SSM_INNER + SSM_XBC + SSM_HEADS
HYB_MIX = CONV_CH + SSM_INNER

ATT_HEADS = 16
ATT_KV_HEADS = 4
ATT_HEAD_DIM = 64
ATT_QKV = (ATT_HEADS + 2 * ATT_KV_HEADS) * ATT_HEAD_DIM
WINDOW = 128
ROPE_THETA = 10000.0

N_EVEN = (DEPTH + 1) // 2
N_ODD = DEPTH // 2

kernel_name = 'macaron_conv_ssd_swa_hybrid'


def rms_norm(x, g):
    xf = x.astype(jnp.float32)
    y = xf * lax.rsqrt(jnp.mean(xf * xf, axis=-1, keepdims=True) + EPS)
    return (y * g.astype(jnp.float32)).astype(x.dtype)


def layer_norm(x, g, b):
    xf = x.astype(jnp.float32)
    mu = jnp.mean(xf, axis=-1, keepdims=True)
    xc = xf - mu
    var = jnp.mean(xc * xc, axis=-1, keepdims=True)
    y = xc * lax.rsqrt(var + EPS) * g.astype(jnp.float32) + b.astype(jnp.float32)
    return y.astype(x.dtype)


def grouped_rms_norm(y, g):
    bsz, seqlen, ch = y.shape
    yg = y.reshape(bsz, seqlen, SSM_GROUPS, ch // SSM_GROUPS)
    yg = yg * lax.rsqrt(jnp.mean(yg * yg, axis=-1, keepdims=True) + EPS)
    return yg.reshape(bsz, seqlen, ch) * g.astype(jnp.float32)


def swiglu(x, w_in, w_out):
    gate, up = jnp.split(x @ w_in, 2, axis=-1)
    return (jax.nn.silu(gate) * up) @ w_out


def causal_depthwise_conv(x, w, b):
    width, ch = w.shape
    y = lax.conv_general_dilated(x, w[:, None, :], window_strides=(1,), padding=[(width - 1, 0)],
                                 dimension_numbers=('NWC', 'WIO', 'NWC'), feature_group_count=ch)
    return y + b


def segsum_exp(a):
    t = a.shape[-1]
    cs = jnp.cumsum(a, axis=-1)
    diff = cs[..., :, None] - cs[..., None, :]
    mask = jnp.tril(jnp.ones((t, t), dtype=bool))
    return jnp.exp(jnp.where(mask, diff, -jnp.inf))


def ssd_chunked(x, dt, a, b, c):
    bsz, seqlen = x.shape[:2]
    nc = seqlen // SSM_CHUNK
    r = SSM_HEADS // SSM_GROUPS
    ln = SSM_CHUNK
    xdt = (x * dt[..., None]).reshape(bsz, nc, ln, SSM_GROUPS, r, SSM_HEAD_DIM)
    adt = (dt * a).reshape(bsz, nc, ln, SSM_GROUPS, r).transpose(0, 3, 4, 1, 2)
    a_cs = jnp.cumsum(adt, axis=-1)
    bc = b.reshape(bsz, nc, ln, SSM_GROUPS, SSM_STATE)
    cc = c.reshape(bsz, nc, ln, SSM_GROUPS, SSM_STATE)
    cb = jnp.einsum('bclgn,bcsgn->bcgls', cc, bc)
    y_diag = jnp.einsum('bcgls,bgrcls,bcsgrp->bclgrp', cb, segsum_exp(adt), xdt)
    decay_to_end = jnp.exp(a_cs[..., -1:] - a_cs)
    states = jnp.einsum('bclgn,bgrcl,bclgrp->bcgrpn', bc, decay_to_end, xdt)
    chunk_decay = jnp.exp(a_cs[..., -1])

    def step(h, inp):
        s_c, d_c = inp
        return h * d_c[..., None, None] + s_c, h

    h0 = jnp.zeros((bsz, SSM_GROUPS, r, SSM_HEAD_DIM, SSM_STATE), jnp.float32)
    _, prev = lax.scan(step, h0, (jnp.moveaxis(states, 1, 0), jnp.moveaxis(chunk_decay, 3, 0)))
    y_off = jnp.einsum('bclgn,cbgrpn,bgrcl->bclgrp', cc, prev, jnp.exp(a_cs))
    return (y_diag + y_off).reshape(bsz, seqlen, SSM_HEADS, SSM_HEAD_DIM)


def conv_ssd_mixer(hn, w_in, cv_w, cv_b, cv_g, cv_beta, sc_w, sc_b, dt_bias, a_log, d_skip, ssm_norm, w_out):
    bsz, seqlen, _ = hn.shape
    f32 = jnp.float32
    cut = np.cumsum([CONV_CH, CONV_CH, SSM_INNER, SSM_XBC]).tolist()
    cv_val, cv_gate, z, xbc, dt_raw = jnp.split(hn @ w_in, cut, axis=-1)
    u = cv_val * jax.nn.sigmoid(cv_gate)
    u = causal_depthwise_conv(u, cv_w, cv_b)
    u = jax.nn.silu(layer_norm(u, cv_g, cv_beta))
    xbc = jax.nn.silu(causal_depthwise_conv(xbc, sc_w, sc_b))
    xs, bs, cs = jnp.split(xbc, [SSM_INNER, SSM_INNER + SSM_GROUPS * SSM_STATE], axis=-1)
    xs = xs.reshape(bsz, seqlen, SSM_HEADS, SSM_HEAD_DIM).astype(f32)
    dt = jax.nn.softplus(dt_raw.astype(f32) + dt_bias.astype(f32))
    a = -jnp.exp(a_log.astype(f32))
    y = ssd_chunked(xs, dt, a,
                    bs.reshape(bsz, seqlen, SSM_GROUPS, SSM_STATE).astype(f32),
                    cs.reshape(bsz, seqlen, SSM_GROUPS, SSM_STATE).astype(f32))
    y = y + d_skip.astype(f32)[:, None] * xs
    y = y.reshape(bsz, seqlen, SSM_INNER) * jax.nn.silu(z.astype(f32))
    y = grouped_rms_norm(y, ssm_norm).astype(hn.dtype)
    return jnp.concatenate([u, y], axis=-1) @ w_out


def rope_tables(seqlen):
    inv = ROPE_THETA ** (-jnp.arange(0, ATT_HEAD_DIM, 2, dtype=jnp.float32) / ATT_HEAD_DIM)
    ang = jnp.arange(seqlen, dtype=jnp.float32)[:, None] * inv[None, :]
    return jnp.cos(ang), jnp.sin(ang)


def apply_rope(t, cos, sin):
    half = t.shape[-1] // 2
    t1, t2 = t[..., :half], t[..., half:]
    c = cos[None, :, None, :].astype(t.dtype)
    s = sin[None, :, None, :].astype(t.dtype)
    return jnp.concatenate([t1 * c - t2 * s, t2 * c + t1 * s], axis=-1)


def swa_sink_attention(hn, w_qkv, b_qkv, sinks, w_o, b_o, cos, sin):
    bsz, seqlen, _ = hn.shape
    nb = seqlen // WINDOW
    grp = ATT_HEADS // ATT_KV_HEADS
    q, k, v = jnp.split(hn @ w_qkv + b_qkv,
                        [ATT_HEADS * ATT_HEAD_DIM, (ATT_HEADS + ATT_KV_HEADS) * ATT_HEAD_DIM], axis=-1)
    q = apply_rope(q.reshape(bsz, seqlen, ATT_HEADS, ATT_HEAD_DIM), cos, sin)
    k = apply_rope(k.reshape(bsz, seqlen, ATT_KV_HEADS, ATT_HEAD_DIM), cos, sin)
    v = v.reshape(bsz, seqlen, ATT_KV_HEADS, ATT_HEAD_DIM)
    qb = q.reshape(bsz, nb, WINDOW, ATT_KV_HEADS, grp, ATT_HEAD_DIM)

    def band(t):
        tp = jnp.pad(t, ((0, 0), (WINDOW, 0), (0, 0), (0, 0)))
        tp = tp.reshape(bsz, nb + 1, WINDOW, ATT_KV_HEADS, ATT_HEAD_DIM)
        return jnp.concatenate([tp[:, :-1], tp[:, 1:]], axis=2)

    kb, vb = band(k), band(v)
    logits = jnp.einsum('bnqkgd,bnskd->bnkgqs', qb, kb).astype(jnp.float32) * (ATT_HEAD_DIM ** -0.5)
    qpos = jnp.arange(nb)[:, None, None] * WINDOW + jnp.arange(WINDOW)[None, :, None]
    kpos = jnp.arange(nb)[:, None, None] * WINDOW - WINDOW + jnp.arange(2 * WINDOW)[None, None, :]
    rel = qpos - kpos
    valid = (kpos >= 0) & (rel >= 0) & (rel < WINDOW)
    logits = jnp.where(valid[None, :, None, None], logits, -jnp.inf)
    sink = sinks.astype(jnp.float32).reshape(ATT_KV_HEADS, grp)[None, None, :, :, None, None]
    m = jnp.maximum(jnp.max(logits, axis=-1, keepdims=True), sink)
    e = jnp.exp(logits - m)
    probs = e / (jnp.sum(e, axis=-1, keepdims=True) + jnp.exp(sink - m))
    o = jnp.einsum('bnkgqs,bnskd->bnqkgd', probs.astype(vb.dtype), vb)
    return o.reshape(bsz, seqlen, ATT_HEADS * ATT_HEAD_DIM) @ w_o + b_o


def _fwd_setup_inputs(seed: int = 0) -> dict:
    key = jax.random.key(seed)
    ks = iter(jax.random.split(key, 40))
    D = D_MODEL

    def nrm(shape, scale):
        return scale * jax.random.normal(next(ks), shape, jnp.float32)

    def gain(shape):
        return 1.0 + nrm(shape, 0.02)

    x = jax.random.normal(next(ks), (BATCH, SEQ, D), jnp.float32)
    p = jax.random.normal(next(ks), (DEPTH, BATCH, SEQ, PLE_DIM), jnp.float32)
    dt0 = jnp.exp(jax.random.uniform(next(ks), (N_EVEN, SSM_HEADS), jnp.float32,
                                     minval=math.log(1e-3), maxval=math.log(1e-1)))
    a_init = jax.random.uniform(next(ks), (N_EVEN, SSM_HEADS), jnp.float32, minval=1.0, maxval=16.0)
    return {
        'x': x,
        'p': p,
        'norm_ffn1': gain((DEPTH, D)),
        'ffn1_w_in': nrm((DEPTH, D, 2 * D_FF), D ** -0.5),
        'ffn1_w_out': nrm((DEPTH, D_FF, D), D_FF ** -0.5),
        'norm_mix': gain((DEPTH, D)),
        'norm_ffn2': gain((DEPTH, D)),
        'ffn2_w_in': nrm((DEPTH, D, 2 * D_FF), D ** -0.5),
        'ffn2_w_out': nrm((DEPTH, D_FF, D), D_FF ** -0.5),
        'ple_norm': gain((DEPTH, D)),
        'ple_gate_w': nrm((DEPTH, D, D), D ** -0.5),
        'ple_proj_w': nrm((DEPTH, PLE_DIM, D), PLE_DIM ** -0.5),
        'hyb_w_in': nrm((N_EVEN, D, HYB_IN), D ** -0.5),
        'conv_dw_w': nrm((N_EVEN, CONV_WIDTH, CONV_CH), CONV_WIDTH ** -0.5),
        'conv_dw_b': nrm((N_EVEN, CONV_CH), 0.02),
        'conv_ln_g': gain((N_EVEN, CONV_CH)),
        'conv_ln_b': nrm((N_EVEN, CONV_CH), 0.02),
        'ssm_conv_w': nrm((N_EVEN, SSM_CONV, SSM_XBC), SSM_CONV ** -0.5),
        'ssm_conv_b': nrm((N_EVEN, SSM_XBC), 0.02),
        'ssm_dt_bias': dt0 + jnp.log(-jnp.expm1(-dt0)),
        'ssm_a_log': jnp.log(a_init),
        'ssm_d': gain((N_EVEN, SSM_HEADS)),
        'ssm_norm': gain((N_EVEN, SSM_INNER)),
        'hyb_w_out': nrm((N_EVEN, HYB_MIX, D), HYB_MIX ** -0.5),
        'att_w_qkv': nrm((N_ODD, D, ATT_QKV), D ** -0.5),
        'att_b_qkv': nrm((N_ODD, ATT_QKV), 0.02),
        'att_sinks': nrm((N_ODD, ATT_HEADS), 0.5),
        'att_w_o': nrm((N_ODD, ATT_HEADS * ATT_HEAD_DIM, D), (ATT_HEADS * ATT_HEAD_DIM) ** -0.5),
        'att_b_o': nrm((N_ODD, D), 0.02),
        'final_norm': gain((D,)),
    }


def _fwd_reference(x, p, norm_ffn1, ffn1_w_in, ffn1_w_out, norm_mix, norm_ffn2, ffn2_w_in, ffn2_w_out,
              ple_norm, ple_gate_w, ple_proj_w, hyb_w_in, conv_dw_w, conv_dw_b, conv_ln_g, conv_ln_b,
              ssm_conv_w, ssm_conv_b, ssm_dt_bias, ssm_a_log, ssm_d, ssm_norm, hyb_w_out,
              att_w_qkv, att_b_qkv, att_sinks, att_w_o, att_b_o, final_norm):
    cos, sin = rope_tables(x.shape[1])
    h = x
    for i in range(DEPTH):
        j = i // 2
        h = h + 0.5 * swiglu(rms_norm(h, norm_ffn1[i]), ffn1_w_in[i], ffn1_w_out[i])
        hn = rms_norm(h, norm_mix[i])
        if i % 2 == 0:
            h = h + conv_ssd_mixer(hn, hyb_w_in[j], conv_dw_w[j], conv_dw_b[j], conv_ln_g[j], conv_ln_b[j],
                                   ssm_conv_w[j], ssm_conv_b[j], ssm_dt_bias[j], ssm_a_log[j], ssm_d[j],
                                   ssm_norm[j], hyb_w_out[j])
        else:
            h = h + swa_sink_attention(hn, att_w_qkv[j], att_b_qkv[j], att_sinks[j], att_w_o[j], att_b_o[j],
                                       cos, sin)
        h = h + 0.5 * swiglu(rms_norm(h, norm_ffn2[i]), ffn2_w_in[i], ffn2_w_out[i])
        gate = jax.nn.sigmoid(rms_norm(h, ple_norm[i]) @ ple_gate_w[i])
        h = h + gate * (p[i] @ ple_proj_w[i])
    return rms_norm(h, final_norm)


import jax as _jax
import jax.numpy as _jnp

TWIN_FORMAT = 'train_step'
FWD_PARAMS = ['x', 'p', 'norm_ffn1', 'ffn1_w_in', 'ffn1_w_out', 'norm_mix', 'norm_ffn2', 'ffn2_w_in', 'ffn2_w_out', 'ple_norm', 'ple_gate_w', 'ple_proj_w', 'hyb_w_in', 'conv_dw_w', 'conv_dw_b', 'conv_ln_g', 'conv_ln_b', 'ssm_conv_w', 'ssm_conv_b', 'ssm_dt_bias', 'ssm_a_log', 'ssm_d', 'ssm_norm', 'hyb_w_out', 'att_w_qkv', 'att_b_qkv', 'att_sinks', 'att_w_o', 'att_b_o', 'final_norm']
TWIN_WEIGHTS = ['norm_ffn1', 'ffn1_w_in', 'ffn1_w_out', 'norm_mix', 'norm_ffn2', 'ffn2_w_in', 'ffn2_w_out', 'ple_norm', 'ple_gate_w', 'ple_proj_w', 'hyb_w_in', 'conv_dw_w', 'conv_dw_b', 'conv_ln_g', 'conv_ln_b', 'ssm_conv_w', 'ssm_conv_b', 'ssm_dt_bias', 'ssm_a_log', 'ssm_d', 'ssm_norm', 'hyb_w_out', 'att_w_qkv', 'att_b_qkv', 'att_sinks', 'att_w_o', 'att_b_o', 'final_norm']
TWIN_DIFF_INPUT = 'x'
TWIN_INPUTS = ['x', 'p', 'norm_ffn1', 'ffn1_w_in', 'ffn1_w_out', 'norm_mix', 'norm_ffn2', 'ffn2_w_in', 'ffn2_w_out', 'ple_norm', 'ple_gate_w', 'ple_proj_w', 'hyb_w_in', 'conv_dw_w', 'conv_dw_b', 'conv_ln_g', 'conv_ln_b', 'ssm_conv_w', 'ssm_conv_b', 'ssm_dt_bias', 'ssm_a_log', 'ssm_d', 'ssm_norm', 'hyb_w_out', 'att_w_qkv', 'att_b_qkv', 'att_sinks', 'att_w_o', 'att_b_o', 'final_norm', 'loss_target', 'm_norm_ffn1', 'm_ffn1_w_in', 'm_ffn1_w_out', 'm_norm_mix', 'm_norm_ffn2', 'm_ffn2_w_in', 'm_ffn2_w_out', 'm_ple_norm', 'm_ple_gate_w', 'm_ple_proj_w', 'm_hyb_w_in', 'm_conv_dw_w', 'm_conv_dw_b', 'm_conv_ln_g', 'm_conv_ln_b', 'm_ssm_conv_w', 'm_ssm_conv_b', 'm_ssm_dt_bias', 'm_ssm_a_log', 'm_ssm_d', 'm_ssm_norm', 'm_hyb_w_out', 'm_att_w_qkv', 'm_att_b_qkv', 'm_att_sinks', 'm_att_w_o', 'm_att_b_o', 'm_final_norm', 'v_norm_ffn1', 'v_ffn1_w_in', 'v_ffn1_w_out', 'v_norm_mix', 'v_norm_ffn2', 'v_ffn2_w_in', 'v_ffn2_w_out', 'v_ple_norm', 'v_ple_gate_w', 'v_ple_proj_w', 'v_hyb_w_in', 'v_conv_dw_w', 'v_conv_dw_b', 'v_conv_ln_g', 'v_conv_ln_b', 'v_ssm_conv_w', 'v_ssm_conv_b', 'v_ssm_dt_bias', 'v_ssm_a_log', 'v_ssm_d', 'v_ssm_norm', 'v_hyb_w_out', 'v_att_w_qkv', 'v_att_b_qkv', 'v_att_sinks', 'v_att_w_o', 'v_att_b_o', 'v_final_norm']
TWIN_OUTPUTS = ['loss', 'grad_x', 'grad_norm_ffn1', 'grad_ffn1_w_in', 'grad_ffn1_w_out', 'grad_norm_mix', 'grad_norm_ffn2', 'grad_ffn2_w_in', 'grad_ffn2_w_out', 'grad_ple_norm', 'grad_ple_gate_w', 'grad_ple_proj_w', 'grad_hyb_w_in', 'grad_conv_dw_w', 'grad_conv_dw_b', 'grad_conv_ln_g', 'grad_conv_ln_b', 'grad_ssm_conv_w', 'grad_ssm_conv_b', 'grad_ssm_dt_bias', 'grad_ssm_a_log', 'grad_ssm_d', 'grad_ssm_norm', 'grad_hyb_w_out', 'grad_att_w_qkv', 'grad_att_b_qkv', 'grad_att_sinks', 'grad_att_w_o', 'grad_att_b_o', 'grad_final_norm', 'delta_norm_ffn1', 'delta_ffn1_w_in', 'delta_ffn1_w_out', 'delta_norm_mix', 'delta_norm_ffn2', 'delta_ffn2_w_in', 'delta_ffn2_w_out', 'delta_ple_norm', 'delta_ple_gate_w', 'delta_ple_proj_w', 'delta_hyb_w_in', 'delta_conv_dw_w', 'delta_conv_dw_b', 'delta_conv_ln_g', 'delta_conv_ln_b', 'delta_ssm_conv_w', 'delta_ssm_conv_b', 'delta_ssm_dt_bias', 'delta_ssm_a_log', 'delta_ssm_d', 'delta_ssm_norm', 'delta_hyb_w_out', 'delta_att_w_qkv', 'delta_att_b_qkv', 'delta_att_sinks', 'delta_att_w_o', 'delta_att_b_o', 'delta_final_norm', 'new_m_norm_ffn1', 'new_m_ffn1_w_in', 'new_m_ffn1_w_out', 'new_m_norm_mix', 'new_m_norm_ffn2', 'new_m_ffn2_w_in', 'new_m_ffn2_w_out', 'new_m_ple_norm', 'new_m_ple_gate_w', 'new_m_ple_proj_w', 'new_m_hyb_w_in', 'new_m_conv_dw_w', 'new_m_conv_dw_b', 'new_m_conv_ln_g', 'new_m_conv_ln_b', 'new_m_ssm_conv_w', 'new_m_ssm_conv_b', 'new_m_ssm_dt_bias', 'new_m_ssm_a_log', 'new_m_ssm_d', 'new_m_ssm_norm', 'new_m_hyb_w_out', 'new_m_att_w_qkv', 'new_m_att_b_qkv', 'new_m_att_sinks', 'new_m_att_w_o', 'new_m_att_b_o', 'new_m_final_norm', 'new_v_norm_ffn1', 'new_v_ffn1_w_in', 'new_v_ffn1_w_out', 'new_v_norm_mix', 'new_v_norm_ffn2', 'new_v_ffn2_w_in', 'new_v_ffn2_w_out', 'new_v_ple_norm', 'new_v_ple_gate_w', 'new_v_ple_proj_w', 'new_v_hyb_w_in', 'new_v_conv_dw_w', 'new_v_conv_dw_b', 'new_v_conv_ln_g', 'new_v_conv_ln_b', 'new_v_ssm_conv_w', 'new_v_ssm_conv_b', 'new_v_ssm_dt_bias', 'new_v_ssm_a_log', 'new_v_ssm_d', 'new_v_ssm_norm', 'new_v_hyb_w_out', 'new_v_att_w_qkv', 'new_v_att_b_qkv', 'new_v_att_sinks', 'new_v_att_w_o', 'new_v_att_b_o', 'new_v_final_norm']
TWIN_LEAF_KINDS = {'loss': 'loss', 'grad_x': 'grad_x', 'grad_norm_ffn1': 'grad_w', 'grad_ffn1_w_in': 'grad_w', 'grad_ffn1_w_out': 'grad_w', 'grad_norm_mix': 'grad_w', 'grad_norm_ffn2': 'grad_w', 'grad_ffn2_w_in': 'grad_w', 'grad_ffn2_w_out': 'grad_w', 'grad_ple_norm': 'grad_w', 'grad_ple_gate_w': 'grad_w', 'grad_ple_proj_w': 'grad_w', 'grad_hyb_w_in': 'grad_w', 'grad_conv_dw_w': 'grad_w', 'grad_conv_dw_b': 'grad_w', 'grad_conv_ln_g': 'grad_w', 'grad_conv_ln_b': 'grad_w', 'grad_ssm_conv_w': 'grad_w', 'grad_ssm_conv_b': 'grad_w', 'grad_ssm_dt_bias': 'grad_w', 'grad_ssm_a_log': 'grad_w', 'grad_ssm_d': 'grad_w', 'grad_ssm_norm': 'grad_w', 'grad_hyb_w_out': 'grad_w', 'grad_att_w_qkv': 'grad_w', 'grad_att_b_qkv': 'grad_w', 'grad_att_sinks': 'grad_w', 'grad_att_w_o': 'grad_w', 'grad_att_b_o': 'grad_w', 'grad_final_norm': 'grad_w', 'delta_norm_ffn1': 'delta_w', 'delta_ffn1_w_in': 'delta_w', 'delta_ffn1_w_out': 'delta_w', 'delta_norm_mix': 'delta_w', 'delta_norm_ffn2': 'delta_w', 'delta_ffn2_w_in': 'delta_w', 'delta_ffn2_w_out': 'delta_w', 'delta_ple_norm': 'delta_w', 'delta_ple_gate_w': 'delta_w', 'delta_ple_proj_w': 'delta_w', 'delta_hyb_w_in': 'delta_w', 'delta_conv_dw_w': 'delta_w', 'delta_conv_dw_b': 'delta_w', 'delta_conv_ln_g': 'delta_w', 'delta_conv_ln_b': 'delta_w', 'delta_ssm_conv_w': 'delta_w', 'delta_ssm_conv_b': 'delta_w', 'delta_ssm_dt_bias': 'delta_w', 'delta_ssm_a_log': 'delta_w', 'delta_ssm_d': 'delta_w', 'delta_ssm_norm': 'delta_w', 'delta_hyb_w_out': 'delta_w', 'delta_att_w_qkv': 'delta_w', 'delta_att_b_qkv': 'delta_w', 'delta_att_sinks': 'delta_w', 'delta_att_w_o': 'delta_w', 'delta_att_b_o': 'delta_w', 'delta_final_norm': 'delta_w', 'new_m_norm_ffn1': 'new_m', 'new_m_ffn1_w_in': 'new_m', 'new_m_ffn1_w_out': 'new_m', 'new_m_norm_mix': 'new_m', 'new_m_norm_ffn2': 'new_m', 'new_m_ffn2_w_in': 'new_m', 'new_m_ffn2_w_out': 'new_m', 'new_m_ple_norm': 'new_m', 'new_m_ple_gate_w': 'new_m', 'new_m_ple_proj_w': 'new_m', 'new_m_hyb_w_in': 'new_m', 'new_m_conv_dw_w': 'new_m', 'new_m_conv_dw_b': 'new_m', 'new_m_conv_ln_g': 'new_m', 'new_m_conv_ln_b': 'new_m', 'new_m_ssm_conv_w': 'new_m', 'new_m_ssm_conv_b': 'new_m', 'new_m_ssm_dt_bias': 'new_m', 'new_m_ssm_a_log': 'new_m', 'new_m_ssm_d': 'new_m', 'new_m_ssm_norm': 'new_m', 'new_m_hyb_w_out': 'new_m', 'new_m_att_w_qkv': 'new_m', 'new_m_att_b_qkv': 'new_m', 'new_m_att_sinks': 'new_m', 'new_m_att_w_o': 'new_m', 'new_m_att_b_o': 'new_m', 'new_m_final_norm': 'new_m', 'new_v_norm_ffn1': 'new_v', 'new_v_ffn1_w_in': 'new_v', 'new_v_ffn1_w_out': 'new_v', 'new_v_norm_mix': 'new_v', 'new_v_norm_ffn2': 'new_v', 'new_v_ffn2_w_in': 'new_v', 'new_v_ffn2_w_out': 'new_v', 'new_v_ple_norm': 'new_v', 'new_v_ple_gate_w': 'new_v', 'new_v_ple_proj_w': 'new_v', 'new_v_hyb_w_in': 'new_v', 'new_v_conv_dw_w': 'new_v', 'new_v_conv_dw_b': 'new_v', 'new_v_conv_ln_g': 'new_v', 'new_v_conv_ln_b': 'new_v', 'new_v_ssm_conv_w': 'new_v', 'new_v_ssm_conv_b': 'new_v', 'new_v_ssm_dt_bias': 'new_v', 'new_v_ssm_a_log': 'new_v', 'new_v_ssm_d': 'new_v', 'new_v_ssm_norm': 'new_v', 'new_v_hyb_w_out': 'new_v', 'new_v_att_w_qkv': 'new_v', 'new_v_att_b_qkv': 'new_v', 'new_v_att_sinks': 'new_v', 'new_v_att_w_o': 'new_v', 'new_v_att_b_o': 'new_v', 'new_v_final_norm': 'new_v'}


def _forward(args):
    return _fwd_reference(*[args[k] for k in FWD_PARAMS])


def _output_shape():
    out = _jax.eval_shape(lambda: _forward(_fwd_setup_inputs(0)))
    return out.shape, out.dtype

N_MICROBATCH = 1
ADAM_LR = 0.001
ADAM_B1 = 0.9
ADAM_B2 = 0.999
ADAM_EPS = 1e-08
ADAM_WD = 0.01
ADAM_STEP = 10
PER_EXAMPLE_BATCH_AXIS = {'x': 0, 'p': 1, 'loss_target': 0}
SHARED_INPUTS = []
_WEIGHT_DTYPES = {'norm_ffn1': _jnp.float32, 'ffn1_w_in': _jnp.float32, 'ffn1_w_out': _jnp.float32, 'norm_mix': _jnp.float32, 'norm_ffn2': _jnp.float32, 'ffn2_w_in': _jnp.float32, 'ffn2_w_out': _jnp.float32, 'ple_norm': _jnp.float32, 'ple_gate_w': _jnp.float32, 'ple_proj_w': _jnp.float32, 'hyb_w_in': _jnp.float32, 'conv_dw_w': _jnp.float32, 'conv_dw_b': _jnp.float32, 'conv_ln_g': _jnp.float32, 'conv_ln_b': _jnp.float32, 'ssm_conv_w': _jnp.float32, 'ssm_conv_b': _jnp.float32, 'ssm_dt_bias': _jnp.float32, 'ssm_a_log': _jnp.float32, 'ssm_d': _jnp.float32, 'ssm_norm': _jnp.float32, 'hyb_w_out': _jnp.float32, 'att_w_qkv': _jnp.float32, 'att_b_qkv': _jnp.float32, 'att_sinks': _jnp.float32, 'att_w_o': _jnp.float32, 'att_b_o': _jnp.float32, 'final_norm': _jnp.float32}
MOMENT_SCALE = {'norm_ffn1': 6.807211e-02, 'ffn1_w_in': 2.895128e-02, 'ffn1_w_out': 4.727759e-02, 'norm_mix': 1.122301e-01, 'norm_ffn2': 5.090342e-02, 'ffn2_w_in': 2.119882e-02, 'ffn2_w_out': 3.457534e-02, 'ple_norm': 2.462885e-02, 'ple_gate_w': 2.457980e-02, 'ple_proj_w': 6.214928e-02, 'hyb_w_in': 6.976733e-02, 'conv_dw_w': 5.646153e-02, 'conv_dw_b': 1.387470e-01, 'conv_ln_g': 7.053624e-02, 'conv_ln_b': 6.843645e-02, 'ssm_conv_w': 7.974142e-02, 'ssm_conv_b': 1.183781e-01, 'ssm_dt_bias': 2.705793e-01, 'ssm_a_log': 3.322848e-01, 'ssm_d': 3.776351e-01, 'ssm_norm': 9.328527e-02, 'hyb_w_out': 1.072230e-01, 'att_w_qkv': 3.273939e-02, 'att_b_qkv': 1.011332e-01, 'att_sinks': 1.220754e-02, 'att_w_o': 2.766379e-02, 'att_b_o': 1.302072e-01, 'final_norm': 3.201917e+01}


def _to_microbatches(a, axis):
    t = _jnp.moveaxis(a, axis, 0)
    t = t.reshape((N_MICROBATCH, t.shape[0] // N_MICROBATCH) + t.shape[1:])
    return _jnp.moveaxis(t, 1, axis + 1)


def setup_inputs(seed: int = 0) -> dict:
    inp = _fwd_setup_inputs(seed)
    key = _jax.random.fold_in(_jax.random.key(seed), 7919)
    shape, _ = _output_shape()
    out = dict(inp)
    out["loss_target"] = _jax.random.normal(_jax.random.fold_in(key, 0), shape, _jnp.float32)
    for i, name in enumerate(TWIN_WEIGHTS):
        w = inp[name].astype(_jnp.float32)
        if MOMENT_SCALE is None:
            s = _jnp.sqrt(_jnp.mean(_jnp.square(w)) + 1e-30)
        else:
            s = MOMENT_SCALE[name]
        km, kv = _jax.random.split(_jax.random.fold_in(key, i + 1))
        out[name] = w
        out["m_" + name] = s * _jax.random.normal(km, w.shape, _jnp.float32)
        out["v_" + name] = (s * s) * _jax.random.uniform(kv, w.shape, _jnp.float32, 0.5, 1.5)
    if N_MICROBATCH > 1:
        for name, axis in PER_EXAMPLE_BATCH_AXIS.items():
            out[name] = _to_microbatches(out[name], axis)
    return {'x': out['x'], 'p': out['p'], 'norm_ffn1': out['norm_ffn1'], 'ffn1_w_in': out['ffn1_w_in'], 'ffn1_w_out': out['ffn1_w_out'], 'norm_mix': out['norm_mix'], 'norm_ffn2': out['norm_ffn2'], 'ffn2_w_in': out['ffn2_w_in'], 'ffn2_w_out': out['ffn2_w_out'], 'ple_norm': out['ple_norm'], 'ple_gate_w': out['ple_gate_w'], 'ple_proj_w': out['ple_proj_w'], 'hyb_w_in': out['hyb_w_in'], 'conv_dw_w': out['conv_dw_w'], 'conv_dw_b': out['conv_dw_b'], 'conv_ln_g': out['conv_ln_g'], 'conv_ln_b': out['conv_ln_b'], 'ssm_conv_w': out['ssm_conv_w'], 'ssm_conv_b': out['ssm_conv_b'], 'ssm_dt_bias': out['ssm_dt_bias'], 'ssm_a_log': out['ssm_a_log'], 'ssm_d': out['ssm_d'], 'ssm_norm': out['ssm_norm'], 'hyb_w_out': out['hyb_w_out'], 'att_w_qkv': out['att_w_qkv'], 'att_b_qkv': out['att_b_qkv'], 'att_sinks': out['att_sinks'], 'att_w_o': out['att_w_o'], 'att_b_o': out['att_b_o'], 'final_norm': out['final_norm'], 'loss_target': out['loss_target'], 'm_norm_ffn1': out['m_norm_ffn1'], 'm_ffn1_w_in': out['m_ffn1_w_in'], 'm_ffn1_w_out': out['m_ffn1_w_out'], 'm_norm_mix': out['m_norm_mix'], 'm_norm_ffn2': out['m_norm_ffn2'], 'm_ffn2_w_in': out['m_ffn2_w_in'], 'm_ffn2_w_out': out['m_ffn2_w_out'], 'm_ple_norm': out['m_ple_norm'], 'm_ple_gate_w': out['m_ple_gate_w'], 'm_ple_proj_w': out['m_ple_proj_w'], 'm_hyb_w_in': out['m_hyb_w_in'], 'm_conv_dw_w': out['m_conv_dw_w'], 'm_conv_dw_b': out['m_conv_dw_b'], 'm_conv_ln_g': out['m_conv_ln_g'], 'm_conv_ln_b': out['m_conv_ln_b'], 'm_ssm_conv_w': out['m_ssm_conv_w'], 'm_ssm_conv_b': out['m_ssm_conv_b'], 'm_ssm_dt_bias': out['m_ssm_dt_bias'], 'm_ssm_a_log': out['m_ssm_a_log'], 'm_ssm_d': out['m_ssm_d'], 'm_ssm_norm': out['m_ssm_norm'], 'm_hyb_w_out': out['m_hyb_w_out'], 'm_att_w_qkv': out['m_att_w_qkv'], 'm_att_b_qkv': out['m_att_b_qkv'], 'm_att_sinks': out['m_att_sinks'], 'm_att_w_o': out['m_att_w_o'], 'm_att_b_o': out['m_att_b_o'], 'm_final_norm': out['m_final_norm'], 'v_norm_ffn1': out['v_norm_ffn1'], 'v_ffn1_w_in': out['v_ffn1_w_in'], 'v_ffn1_w_out': out['v_ffn1_w_out'], 'v_norm_mix': out['v_norm_mix'], 'v_norm_ffn2': out['v_norm_ffn2'], 'v_ffn2_w_in': out['v_ffn2_w_in'], 'v_ffn2_w_out': out['v_ffn2_w_out'], 'v_ple_norm': out['v_ple_norm'], 'v_ple_gate_w': out['v_ple_gate_w'], 'v_ple_proj_w': out['v_ple_proj_w'], 'v_hyb_w_in': out['v_hyb_w_in'], 'v_conv_dw_w': out['v_conv_dw_w'], 'v_conv_dw_b': out['v_conv_dw_b'], 'v_conv_ln_g': out['v_conv_ln_g'], 'v_conv_ln_b': out['v_conv_ln_b'], 'v_ssm_conv_w': out['v_ssm_conv_w'], 'v_ssm_conv_b': out['v_ssm_conv_b'], 'v_ssm_dt_bias': out['v_ssm_dt_bias'], 'v_ssm_a_log': out['v_ssm_a_log'], 'v_ssm_d': out['v_ssm_d'], 'v_ssm_norm': out['v_ssm_norm'], 'v_hyb_w_out': out['v_hyb_w_out'], 'v_att_w_qkv': out['v_att_w_qkv'], 'v_att_b_qkv': out['v_att_b_qkv'], 'v_att_sinks': out['v_att_sinks'], 'v_att_w_o': out['v_att_w_o'], 'v_att_b_o': out['v_att_b_o'], 'v_final_norm': out['v_final_norm']}


def _loss(weights, diff, rest, loss_target):
    with _jax.named_scope("forward"):
        args = {**rest, TWIN_DIFF_INPUT: diff, **{k: w.astype(_WEIGHT_DTYPES[k]) for k, w in weights.items()}}
        y = _forward(args)
    with _jax.named_scope("loss_head"):
        err = _jnp.square(y.astype(_jnp.float32) - loss_target)
        return 0.5 * _jnp.sum(_jnp.mean(err, axis=-1)) if err.ndim else 0.5 * err


def _adamw(w, g, m, v):
    m = ADAM_B1 * m + (1.0 - ADAM_B1) * g
    v = ADAM_B2 * v + (1.0 - ADAM_B2) * _jnp.square(g)
    m_hat = m / (1.0 - ADAM_B1 ** ADAM_STEP)
    v_hat = v / (1.0 - ADAM_B2 ** ADAM_STEP)
    delta = -ADAM_LR * (m_hat / (_jnp.sqrt(v_hat) + ADAM_EPS) + ADAM_WD * w)
    return delta, m, v


def reference(x, p, norm_ffn1, ffn1_w_in, ffn1_w_out, norm_mix, norm_ffn2, ffn2_w_in, ffn2_w_out, ple_norm, ple_gate_w, ple_proj_w, hyb_w_in, conv_dw_w, conv_dw_b, conv_ln_g, conv_ln_b, ssm_conv_w, ssm_conv_b, ssm_dt_bias, ssm_a_log, ssm_d, ssm_norm, hyb_w_out, att_w_qkv, att_b_qkv, att_sinks, att_w_o, att_b_o, final_norm, loss_target, m_norm_ffn1, m_ffn1_w_in, m_ffn1_w_out, m_norm_mix, m_norm_ffn2, m_ffn2_w_in, m_ffn2_w_out, m_ple_norm, m_ple_gate_w, m_ple_proj_w, m_hyb_w_in, m_conv_dw_w, m_conv_dw_b, m_conv_ln_g, m_conv_ln_b, m_ssm_conv_w, m_ssm_conv_b, m_ssm_dt_bias, m_ssm_a_log, m_ssm_d, m_ssm_norm, m_hyb_w_out, m_att_w_qkv, m_att_b_qkv, m_att_sinks, m_att_w_o, m_att_b_o, m_final_norm, v_norm_ffn1, v_ffn1_w_in, v_ffn1_w_out, v_norm_mix, v_norm_ffn2, v_ffn2_w_in, v_ffn2_w_out, v_ple_norm, v_ple_gate_w, v_ple_proj_w, v_hyb_w_in, v_conv_dw_w, v_conv_dw_b, v_conv_ln_g, v_conv_ln_b, v_ssm_conv_w, v_ssm_conv_b, v_ssm_dt_bias, v_ssm_a_log, v_ssm_d, v_ssm_norm, v_hyb_w_out, v_att_w_qkv, v_att_b_qkv, v_att_sinks, v_att_w_o, v_att_b_o, v_final_norm):
    given = dict(x=x, p=p, norm_ffn1=norm_ffn1, ffn1_w_in=ffn1_w_in, ffn1_w_out=ffn1_w_out, norm_mix=norm_mix, norm_ffn2=norm_ffn2, ffn2_w_in=ffn2_w_in, ffn2_w_out=ffn2_w_out, ple_norm=ple_norm, ple_gate_w=ple_gate_w, ple_proj_w=ple_proj_w, hyb_w_in=hyb_w_in, conv_dw_w=conv_dw_w, conv_dw_b=conv_dw_b, conv_ln_g=conv_ln_g, conv_ln_b=conv_ln_b, ssm_conv_w=ssm_conv_w, ssm_conv_b=ssm_conv_b, ssm_dt_bias=ssm_dt_bias, ssm_a_log=ssm_a_log, ssm_d=ssm_d, ssm_norm=ssm_norm, hyb_w_out=hyb_w_out, att_w_qkv=att_w_qkv, att_b_qkv=att_b_qkv, att_sinks=att_sinks, att_w_o=att_w_o, att_b_o=att_b_o, final_norm=final_norm, loss_target=loss_target, m_norm_ffn1=m_norm_ffn1, m_ffn1_w_in=m_ffn1_w_in, m_ffn1_w_out=m_ffn1_w_out, m_norm_mix=m_norm_mix, m_norm_ffn2=m_norm_ffn2, m_ffn2_w_in=m_ffn2_w_in, m_ffn2_w_out=m_ffn2_w_out, m_ple_norm=m_ple_norm, m_ple_gate_w=m_ple_gate_w, m_ple_proj_w=m_ple_proj_w, m_hyb_w_in=m_hyb_w_in, m_conv_dw_w=m_conv_dw_w, m_conv_dw_b=m_conv_dw_b, m_conv_ln_g=m_conv_ln_g, m_conv_ln_b=m_conv_ln_b, m_ssm_conv_w=m_ssm_conv_w, m_ssm_conv_b=m_ssm_conv_b, m_ssm_dt_bias=m_ssm_dt_bias, m_ssm_a_log=m_ssm_a_log, m_ssm_d=m_ssm_d, m_ssm_norm=m_ssm_norm, m_hyb_w_out=m_hyb_w_out, m_att_w_qkv=m_att_w_qkv, m_att_b_qkv=m_att_b_qkv, m_att_sinks=m_att_sinks, m_att_w_o=m_att_w_o, m_att_b_o=m_att_b_o, m_final_norm=m_final_norm, v_norm_ffn1=v_norm_ffn1, v_ffn1_w_in=v_ffn1_w_in, v_ffn1_w_out=v_ffn1_w_out, v_norm_mix=v_norm_mix, v_norm_ffn2=v_norm_ffn2, v_ffn2_w_in=v_ffn2_w_in, v_ffn2_w_out=v_ffn2_w_out, v_ple_norm=v_ple_norm, v_ple_gate_w=v_ple_gate_w, v_ple_proj_w=v_ple_proj_w, v_hyb_w_in=v_hyb_w_in, v_conv_dw_w=v_conv_dw_w, v_conv_dw_b=v_conv_dw_b, v_conv_ln_g=v_conv_ln_g, v_conv_ln_b=v_conv_ln_b, v_ssm_conv_w=v_ssm_conv_w, v_ssm_conv_b=v_ssm_conv_b, v_ssm_dt_bias=v_ssm_dt_bias, v_ssm_a_log=v_ssm_a_log, v_ssm_d=v_ssm_d, v_ssm_norm=v_ssm_norm, v_hyb_w_out=v_hyb_w_out, v_att_w_qkv=v_att_w_qkv, v_att_b_qkv=v_att_b_qkv, v_att_sinks=v_att_sinks, v_att_w_o=v_att_w_o, v_att_b_o=v_att_b_o, v_final_norm=v_final_norm)
    weights = {n: given[n] for n in TWIN_WEIGHTS}
    shared = {n: given[n] for n in SHARED_INPUTS}
    per_example = {n: given[n] for n in ['x', 'p']}
    grad_fn = _jax.value_and_grad(_loss, argnums=(0, 1))

    def one_microbatch(ex, loss_target):
        ex = dict(ex)
        diff = ex.pop(TWIN_DIFF_INPUT)
        return grad_fn(weights, diff, {**shared, **ex}, loss_target)

    if N_MICROBATCH == 1:
        loss, (grad_w, grad_x) = one_microbatch(per_example, given["loss_target"])
    else:
        def body(carry, xs):
            loss_sum, grad_sum = carry
            l_k, (gw_k, gx_k) = one_microbatch(xs[0], xs[1])
            with _jax.named_scope("update"):
                return (loss_sum + l_k, _jax.tree.map(_jnp.add, grad_sum, gw_k)), gx_k

        init = (_jnp.zeros((), _jnp.float32), _jax.tree.map(_jnp.zeros_like, weights))
        (loss, grad_w), grad_x = _jax.lax.scan(body, init, (per_example, given["loss_target"]))
    with _jax.named_scope("update"):
        delta_w, new_m, new_v = {}, {}, {}
        for n in TWIN_WEIGHTS:
            delta_w[n], new_m[n], new_v[n] = _adamw(weights[n], grad_w[n], given["m_" + n], given["v_" + n])
    return (loss, grad_x, *[grad_w[n] for n in TWIN_WEIGHTS], *[delta_w[n] for n in TWIN_WEIGHTS],
            *[new_m[n] for n in TWIN_WEIGHTS], *[new_v[n] for n in TWIN_WEIGHTS])
```

```python
import functools
import math

import numpy as np
import jax
import jax.numpy as jnp
from jax import lax
from jax.experimental import pallas as pl
from jax.experimental.pallas import tpu as pltpu

F32, BF16 = jnp.float32, jnp.bfloat16
HI = lax.Precision.HIGHEST
SDS = jax.ShapeDtypeStruct

N_DEV = 8
D = 1024
D_FF = 2816
FF_SHARD = 2 * D_FF // N_DEV
PLE_DIM = 256
EPS = 1e-6
CONV_W = 31
SSM_CONV = 4
SSM_HEADS = 16
SSM_XBC = 1536
CHUNK = 128
HYB_IN = 4624
HYB_PAD = 5120
DT_COL = 4608
N_PAIR = 8
ROPE_THETA = 10000.0
LANE = 128
VMEM_LIMIT = 56 * 1024 * 1024

ADAM_LR, ADAM_B1, ADAM_B2, ADAM_EPS, ADAM_WD, ADAM_STEP = 0.001, 0.9, 0.999, 1e-08, 0.01, 10


def _params(sem):
    return pltpu.CompilerParams(dimension_semantics=sem, vmem_limit_bytes=VMEM_LIMIT)


def _mm(a, b, *, ta=False, tb=False, reduce_j=False, out_dtypes=(F32,), tm=1024, tn=1024, tk=1024,
        epi=None, extras=(), rows=(), name):
    ja, jb = a.shape[0], b.shape[0]
    nj = max(ja, jb)
    jo = 1 if reduce_j else nj
    m, k = (a.shape[2], a.shape[1]) if ta else (a.shape[1], a.shape[2])
    n = b.shape[1] if tb else b.shape[2]
    assert (b.shape[2] if tb else b.shape[1]) == k and ja in (1, nj) and jb in (1, nj)
    tm, tn, tk = min(tm, m), min(tn, n), min(tk, k)
    assert m % tm == 0 and n % tn == 0 and k % tk == 0, (name, m, n, k, tm, tn, tk)
    nk = k // tk
    steps = nk * (nj if reduce_j else 1)
    ne, nr, no = len(extras), len(rows), len(out_dtypes)

    def a_map(i, c, j, kk):
        return (j if ja > 1 else 0, kk, i) if ta else (j if ja > 1 else 0, i, kk)

    def b_map(i, c, j, kk):
        return (j if jb > 1 else 0, c, kk) if tb else (j if jb > 1 else 0, kk, c)

    def o_map(i, c, j, kk):
        return (0 if reduce_j else j, i, c)

    dims = (((0 if ta else 1,), (1 if tb else 0,)), ((), ()))

    def body(a_ref, b_ref, *rest):
        ex, rw, outs = rest[:ne], rest[ne:ne + nr], rest[ne + nr:ne + nr + no]
        part = lax.dot_general(a_ref[...], b_ref[...], dims, preferred_element_type=F32)

        def finish(acc):
            res = epi(acc, *[e[...] for e in ex], *[r[...] for r in rw]) if epi else (acc,)
            for o, r in zip(outs, res):
                o[...] = r.astype(o.dtype)

        if steps == 1:
            finish(part)
            return
        acc_ref = rest[-1]
        kk = pl.program_id(3)
        step = pl.program_id(2) * nk + kk if reduce_j else kk

        @pl.when(step == 0)
        def _():
            acc_ref[...] = part

        @pl.when(step > 0)
        def _():
            acc_ref[...] += part

        @pl.when(step == steps - 1)
        def _():
            finish(acc_ref[...])

    o_spec = pl.BlockSpec((None, tm, tn), o_map)
    return pl.pallas_call(
        body, name=name, grid=(m // tm, n // tn, nj, nk),
        in_specs=[pl.BlockSpec((None, tk, tm) if ta else (None, tm, tk), a_map),
                  pl.BlockSpec((None, tn, tk) if tb else (None, tk, tn), b_map)]
        + [o_spec] * ne + [pl.BlockSpec((1, tn), lambda i, c, j, kk: (0, c))] * nr,
        out_specs=[o_spec] * no,
        out_shape=[SDS((jo, m, n), dt) for dt in out_dtypes],
        scratch_shapes=[pltpu.VMEM((tm, tn), F32)] if steps > 1 else [],
        compiler_params=_params(("parallel", "parallel", "arbitrary", "arbitrary")),
    )(a, b, *extras, *rows)


def _whole(p):
    return pl.BlockSpec(p.shape, lambda *_: (0,) * p.ndim)


def _rowop(fn, tiles, params, outs, *, grid, name):
    nin = len(tiles) + len(params)

    def body(*refs):
        res = fn(*[r[...].astype(F32) for r in refs[:nin]])
        for r, o in zip(refs[nin:], res):
            r[...] = o.astype(r.dtype)

    return pl.pallas_call(
        body, name=name, grid=grid,
        in_specs=[s for _, s in tiles] + [_whole(p) for p in params],
        out_specs=[s for _, _, s in outs], out_shape=[SDS(sh, dt) for sh, dt, _ in outs],
        compiler_params=_params(("parallel",) * len(grid)),
    )(*[t for t, _ in tiles], *params)


def _rowop_bwd(fn, tiles, params, cots, wrt, gouts, *, grid, name, adds=()):
    nt, npar, nc, na = len(tiles), len(params), len(cots), len(adds)
    nin = nt + npar
    flat = [i for grp in wrt for i in grp]
    n_gout = sum(len(dts) for _, dts, _ in gouts)

    def body(*refs):
        vals = [r[...].astype(F32) for r in refs[:nin]]
        cvals = [r[...].astype(F32) for r in refs[nin:nin + nc]]
        avals = [r[...].astype(F32) for r in refs[nin + nc:nin + nc + na]]
        orefs = refs[nin + nc + na:]
        diff_idx = flat + list(range(nt, nin))

        def f(*dv):
            full = list(vals)
            for i, v in zip(diff_idx, dv):
                full[i] = v
            return fn(*full)

        _, vjp = jax.vjp(f, *[vals[i] for i in diff_idx])
        grads = vjp(tuple(cvals))
        tile_g, par_g = list(grads[:len(flat)]), grads[len(flat):]
        group_g, at = [], 0
        for grp in wrt:
            members = tile_g[at:at + len(grp)]
            at += len(grp)
            group_g.append(members[0] if len(grp) == 1 else jnp.stack(members, axis=0))
        for av in avals:
            group_g[0] = group_g[0] + av
        o = 0
        for g, (_, dts, _) in zip(group_g, gouts):
            for _ in dts:
                orefs[o][...] = g.astype(orefs[o].dtype)
                o += 1
        first = functools.reduce(jnp.logical_and, [pl.program_id(ax) == 0 for ax in range(len(grid))])
        for r, g in zip(orefs[n_gout:], par_g):
            @pl.when(first)
            def _(r=r, g=g):
                r[...] = g

            @pl.when(jnp.logical_not(first))
            def _(r=r, g=g):
                r[...] += g

    out_specs, out_shape = [], []
    for sh, dts, spec in gouts:
        for dt in dts:
            out_specs.append(spec)
            out_shape.append(SDS(sh, dt))
    for p in params:
        out_specs.append(_whole(p))
        out_shape.append(SDS(p.shape, F32))
    return pl.pallas_call(
        body, name=name, grid=grid,
        in_specs=[s for _, s in tiles] + [_whole(p) for p in params] + [s for _, s in cots] + [s for _, s in adds],
        out_specs=out_specs, out_shape=out_shape,
        compiler_params=_params(("arbitrary",) * len(grid)),
    )(*[t for t, _ in tiles], *params, *[c for c, _ in cots], *[a for a, _ in adds])


def _tok(c, tm, col=0):
    return pl.BlockSpec((tm, c), lambda i, col=col: (i, col))


def _rms_fn(h, g):
    return (h * lax.rsqrt(jnp.mean(h * h, axis=-1, keepdims=True) + EPS) * g,)


def _swiglu_fn(gate, up):
    return (gate * jax.nn.sigmoid(gate) * up,)


def _lnswish_fn(u, g, b):
    mu = jnp.mean(u, axis=-1, keepdims=True)
    xc = u - mu
    y = xc * lax.rsqrt(jnp.mean(xc * xc, axis=-1, keepdims=True) + EPS) * g + b
    return (y * jax.nn.sigmoid(y),)


def _ple_fn(z, e):
    return (jax.nn.sigmoid(z) * e,)


def _rms(h, g, name, tm=512):
    t = h.shape[0]
    return _rowop(_rms_fn, [(h, _tok(D, tm))], [g], [((t, D), BF16, _tok(D, tm))], grid=(t // tm,), name=name)[0]


def _rms_bwd(h, g, dn, dres, name, tm=512):
    t = h.shape[0]
    return _rowop_bwd(_rms_fn, [(h, _tok(D, tm))], [g], [(dn, _tok(D, tm))], [(0,)], [((t, D), (F32, BF16), _tok(D, tm))],
                      grid=(t // tm,), name=name, adds=[(dres, _tok(D, tm))])


def _conv_geometry(width):
    pad = 32 if width > 8 else 8
    return pad, pad - (width - 1)


def _fill_shifts(xpad_ref, sh_ref, t, shifts):
    for r in shifts:
        sh_ref[r, :, :] = xpad_ref[pl.ds(r, t + 32), :]


def _dwconv(xs, w, b, *, width, glu, silu, cb, name):
    t = xs[0][0].shape[0]
    c = w.shape[1]
    pad, off = _conv_geometry(width)
    shifts = sorted({(k + off) % 8 for k in range(width)})
    ch = 32

    def body(*refs):
        x_refs, (w_ref, b_ref, o_ref, xpad_ref, sh_ref) = refs[:len(xs)], refs[len(xs):]
        u = x_refs[0][...] * jax.nn.sigmoid(x_refs[1][...]) if glu else x_refs[0][...]
        xpad_ref[pl.ds(0, pad), :] = jnp.zeros((pad, cb), F32)
        xpad_ref[pl.ds(pad, t), :] = u
        xpad_ref[pl.ds(pad + t, 40 - pad), :] = jnp.zeros((40 - pad, cb), F32)
        _fill_shifts(xpad_ref, sh_ref, t, shifts)

        def chunk(i, carry):
            t0 = pl.multiple_of(i * ch, ch)
            acc = jnp.broadcast_to(b_ref[...], (ch, cb))
            for k in range(width):
                q, r = divmod(k + off, 8)
                acc = acc + w_ref[pl.ds(k, 1), :] * sh_ref[r, pl.ds(t0 + 8 * q, ch), :]
            o_ref[pl.ds(t0, ch), :] = acc * jax.nn.sigmoid(acc) if silu else acc
            return carry

        lax.fori_loop(0, t // ch, chunk, 0)

    return pl.pallas_call(
        body, name=name, grid=(c // cb,),
        in_specs=[pl.BlockSpec((t, cb), lambda i, o=o: (0, o + i)) for _, o in xs]
        + [pl.BlockSpec((width, cb), lambda i: (0, i)), pl.BlockSpec((1, cb), lambda i: (0, i))],
        out_specs=pl.BlockSpec((t, cb), lambda i: (0, i)), out_shape=SDS((t, c), F32),
        scratch_shapes=[pltpu.VMEM((t + 40, cb), F32), pltpu.VMEM((8, t + 32, cb), F32)],
        compiler_params=_params(("parallel",)),
    )(*[x for x, _ in xs], w, b)


def _dwconv_bwd(xs, w, b, dy, *, width, glu, silu, cb, name):
    t = xs[0][0].shape[0]
    c = w.shape[1]
    pad, off = _conv_geometry(width)
    shifts = sorted({(k + off) % 8 for k in range(width)})
    shifts_t = sorted({mm % 8 for mm in range(width)})
    ch = 32
    nx = len(xs)

    def body(*refs):
        x_refs = refs[:nx]
        w_ref, b_ref, dy_ref = refs[nx:nx + 3]
        dx_refs = refs[nx + 3:nx + 3 + nx]
        dw_ref, db_ref, xpad_ref, sh_ref, dc_ref = refs[nx + 3 + nx:]
        u = x_refs[0][...] * jax.nn.sigmoid(x_refs[1][...]) if glu else x_refs[0][...]
        xpad_ref[pl.ds(0, pad), :] = jnp.zeros((pad, cb), F32)
        xpad_ref[pl.ds(pad, t), :] = u
        xpad_ref[pl.ds(pad + t, 40 - pad), :] = jnp.zeros((40 - pad, cb), F32)
        _fill_shifts(xpad_ref, sh_ref, t, shifts)

        if silu:
            def act_chunk(i, carry):
                t0 = pl.multiple_of(i * ch, ch)
                acc = jnp.broadcast_to(b_ref[...], (ch, cb))
                for k in range(width):
                    q, r = divmod(k + off, 8)
                    acc = acc + w_ref[pl.ds(k, 1), :] * sh_ref[r, pl.ds(t0 + 8 * q, ch), :]
                sg = jax.nn.sigmoid(acc)
                dc_ref[pl.ds(t0, ch), :] = dy_ref[pl.ds(t0, ch), :] * (sg * (1.0 + acc * (1.0 - sg)))
                return carry

            lax.fori_loop(0, t // ch, act_chunk, 0)
        else:
            dc_ref[...] = dy_ref[...]

        def dw_chunk(i, accs):
            t0 = pl.multiple_of(i * 8, 8)
            d = dc_ref[pl.ds(t0, 8), :]
            new = []
            for k in range(width):
                q, r = divmod(k + off, 8)
                new.append(accs[k] + d * sh_ref[r, pl.ds(t0 + 8 * q, 8), :])
            new.append(accs[width] + d)
            return tuple(new)

        accs = lax.fori_loop(0, t // 8, dw_chunk, tuple(jnp.zeros((8, cb), F32) for _ in range(width + 1)))
        for k in range(width):
            dw_ref[pl.ds(k, 1), :] = jnp.sum(accs[k], axis=0, keepdims=True)
        db_ref[...] = jnp.sum(accs[width], axis=0, keepdims=True)

        xpad_ref[pl.ds(0, t), :] = dc_ref[...]
        xpad_ref[pl.ds(t, 40), :] = jnp.zeros((40, cb), F32)
        _fill_shifts(xpad_ref, sh_ref, t, shifts_t)

        def dx_chunk(i, carry):
            t0 = pl.multiple_of(i * ch, ch)
            acc = jnp.zeros((ch, cb), F32)
            for mm in range(width):
                q, r = divmod(mm, 8)
                acc = acc + w_ref[pl.ds(width - 1 - mm, 1), :] * sh_ref[r, pl.ds(t0 + 8 * q, ch), :]
            if glu:
                val, gate = x_refs[0][pl.ds(t0, ch), :], x_refs[1][pl.ds(t0, ch), :]
                sg = jax.nn.sigmoid(gate)
                dx_refs[0][pl.ds(t0, ch), :] = (acc * sg).astype(BF16)
                dx_refs[1][pl.ds(t0, ch), :] = (acc * val * sg * (1.0 - sg)).astype(BF16)
            else:
                dx_refs[0][pl.ds(t0, ch), :] = acc.astype(BF16)
            return carry

        lax.fori_loop(0, t // ch, dx_chunk, 0)

    col = pl.BlockSpec((t, cb), lambda i: (0, i))
    return pl.pallas_call(
        body, name=name, grid=(c // cb,),
        in_specs=[pl.BlockSpec((t, cb), lambda i, o=o: (0, o + i)) for _, o in xs]
        + [pl.BlockSpec((width, cb), lambda i: (0, i)), pl.BlockSpec((1, cb), lambda i: (0, i)), col],
        out_specs=[col] * nx + [pl.BlockSpec((width, cb), lambda i: (0, i)), pl.BlockSpec((1, cb), lambda i: (0, i))],
        out_shape=[SDS((t, c), BF16)] * nx + [SDS((width, c), F32), SDS((1, c), F32)],
        scratch_shapes=[pltpu.VMEM((t + 40, cb), F32), pltpu.VMEM((8, t + 32, cb), F32), pltpu.VMEM((t, cb), F32)],
        compiler_params=_params(("parallel",)),
    )(*[x for x, _ in xs], w, b, dy)


_DIMS = {"nn": (((1,), (0,)), ((), ())), "nt": (((1,), (1,)), ((), ())), "tn": (((0,), (0,)), ((), ()))}


def _raw_dot(a, b, mode):
    return lax.dot_general(a.astype(BF16), b.astype(BF16), _DIMS[mode], preferred_element_type=F32)


@functools.partial(jax.custom_vjp, nondiff_argnums=(2,))
def _bdot(a, b, mode):
    return _raw_dot(a, b, mode)


def _bdot_fwd(a, b, mode):
    return _raw_dot(a, b, mode), (a, b)


def _bdot_bwd(mode, res, g):
    a, b = res
    if mode == "nn":
        return _raw_dot(g, b, "nt"), _raw_dot(a, g, "tn")
    if mode == "nt":
        return _raw_dot(g, b, "nn"), _raw_dot(g, a, "tn")
    return _raw_dot(b, g, "nt"), _raw_dot(a, g, "nn")


_bdot.defvjp(_bdot_fwd, _bdot_bwd)


def _iota(shape, axis):
    return lax.broadcasted_iota(jnp.int32, shape, axis)


def _half_masks():
    left = (_iota((1, LANE), 1) < 64).astype(F32)
    return left, 1.0 - left


def _ssd_chunk(state, xa, dtr, z, dtb, alog, dsk, ng):
    xs, bm, cm = xa[:, :D], xa[:, D:D + 256], xa[:, D + 256:]
    left, right = _half_masks()
    expand = (_iota((LANE, D), 1) // 64 == _iota((LANE, D), 0)).astype(F32)
    li, si = _iota((CHUNK, CHUNK), 0), _iota((CHUNK, CHUNK), 1)
    tril = li >= si
    dt16 = jax.nn.softplus(dtr + dtb)
    adt = dt16 * (-jnp.exp(alog))
    dtf = jnp.dot(dt16, expand, precision=HI)
    cs16 = jnp.dot(tril.astype(F32), adt, precision=HI)
    csf = jnp.dot(cs16, expand, precision=HI)
    totf = jnp.sum(jnp.dot(adt, expand, precision=HI), axis=0, keepdims=True)
    cst = cs16.T
    xdt = xs * dtf
    ys, new_state = [], []
    for g in range(2):
        bg, cg = bm[:, LANE * g:LANE * (g + 1)], cm[:, LANE * g:LANE * (g + 1)]
        cb = _bdot(cg, bg, "nt")
        for q in range(4):
            pr = 4 * g + q
            decay = []
            for h in (2 * pr, 2 * pr + 1):
                col = jnp.dot(cs16, (li == h).astype(F32), precision=HI)
                row = jnp.dot((si == h).astype(F32), cst, precision=HI)
                decay.append(cb * jnp.exp(jnp.where(tril, col - row, -jnp.inf)))
            xp = xdt[:, LANE * pr:LANE * (pr + 1)]
            y_diag = _bdot(jnp.concatenate(decay, axis=1), jnp.concatenate([xp * left, xp * right], axis=0), "nn")
            csb, tot = csf[:, LANE * pr:LANE * (pr + 1)], totf[:, LANE * pr:LANE * (pr + 1)]
            ys.append(y_diag + _bdot(cg, state[pr], "nn") * jnp.exp(csb))
            new_state.append(state[pr] * jnp.exp(tot) + _bdot(bg, xp * jnp.exp(tot - csb), "tn"))
    y = jnp.concatenate(ys, axis=1)
    y = y + jnp.dot(jnp.broadcast_to(dsk, (CHUNK, LANE)), expand, precision=HI) * xs
    y = y * (z * jax.nn.sigmoid(z))
    halves = []
    for g in range(2):
        yg = y[:, 512 * g:512 * (g + 1)]
        halves.append(yg * lax.rsqrt(jnp.mean(yg * yg, axis=-1, keepdims=True) + EPS))
    return jnp.concatenate(halves, axis=1) * ng, jnp.stack(new_state, axis=0)


def _ssd_specs(t, rev):
    nc = t // CHUNK
    ix = (lambda c: nc - 1 - c) if rev else (lambda c: c)
    return nc, ix


def _ssd_fwd(xa, proj, dtb, alog, dsk, ng, name):
    t = xa.shape[0]
    nc, ix = _ssd_specs(t, False)

    def body(xa_ref, dt_ref, z_ref, dtb_ref, alog_ref, dsk_ref, ng_ref, y_ref, st_ref, carry_ref):
        @pl.when(pl.program_id(0) == 0)
        def _():
            carry_ref[...] = jnp.zeros_like(carry_ref)

        st_ref[...] = carry_ref[...]
        y, new = _ssd_chunk(carry_ref[...], xa_ref[...], dt_ref[...], z_ref[...], dtb_ref[...], alog_ref[...],
                            dsk_ref[...], ng_ref[...])
        y_ref[...] = y.astype(BF16)
        carry_ref[...] = new

    small = [dtb, alog, dsk, ng]
    return pl.pallas_call(
        body, name=name, grid=(nc,),
        in_specs=[pl.BlockSpec((CHUNK, SSM_XBC), lambda c: (c, 0)),
                  pl.BlockSpec((CHUNK, LANE), lambda c: (c, DT_COL // LANE)),
                  pl.BlockSpec((CHUNK, D), lambda c: (c, 2))] + [_whole(p) for p in small],
        out_specs=[pl.BlockSpec((CHUNK, D), lambda c: (c, 0)), pl.BlockSpec((None, N_PAIR, LANE, LANE), lambda c: (c, 0, 0, 0))],
        out_shape=[SDS((t, D), BF16), SDS((nc, N_PAIR, LANE, LANE), F32)],
        scratch_shapes=[pltpu.VMEM((N_PAIR, LANE, LANE), F32)],
        compiler_params=_params(("arbitrary",)),
    )(xa, proj, proj, *small)


def _ssd_bwd(xa, proj, states, dy, dtb, alog, dsk, ng, name):
    t = xa.shape[0]
    nc, ix = _ssd_specs(t, True)

    def body(xa_ref, dt_ref, z_ref, st_ref, dy_ref, dtb_ref, alog_ref, dsk_ref, ng_ref,
             dxa_ref, ddt_ref, dz_ref, gdtb_ref, galog_ref, gdsk_ref, gng_ref, carry_ref):
        first = pl.program_id(0) == 0

        @pl.when(first)
        def _():
            carry_ref[...] = jnp.zeros_like(carry_ref)

        args = (st_ref[...], xa_ref[...], dt_ref[...], z_ref[...], dtb_ref[...], alog_ref[...], dsk_ref[...], ng_ref[...])
        _, vjp = jax.vjp(_ssd_chunk, *args)
        ds, dxa, ddt, dz, gdtb, galog, gdsk, gng = vjp((dy_ref[...], carry_ref[...]))
        carry_ref[...] = ds
        dxa_ref[...] = dxa
        ddt_ref[...] = ddt.astype(BF16)
        dz_ref[...] = dz.astype(BF16)
        for r, g in ((gdtb_ref, gdtb), (galog_ref, galog), (gdsk_ref, gdsk), (gng_ref, gng)):
            @pl.when(first)
            def _(r=r, g=g):
                r[...] = g

            @pl.when(jnp.logical_not(first))
            def _(r=r, g=g):
                r[...] += g

    small = [dtb, alog, dsk, ng]
    return pl.pallas_call(
        body, name=name, grid=(nc,),
        in_specs=[pl.BlockSpec((CHUNK, SSM_XBC), lambda c: (ix(c), 0)),
                  pl.BlockSpec((CHUNK, LANE), lambda c: (ix(c), DT_COL // LANE)),
                  pl.BlockSpec((CHUNK, D), lambda c: (ix(c), 2)),
                  pl.BlockSpec((None, N_PAIR, LANE, LANE), lambda c: (ix(c), 0, 0, 0)),
                  pl.BlockSpec((CHUNK, D), lambda c: (ix(c), 0))] + [_whole(p) for p in small],
        out_specs=[pl.BlockSpec((CHUNK, SSM_XBC), lambda c: (ix(c), 0)), pl.BlockSpec((CHUNK, LANE), lambda c: (ix(c), 0)),
                   pl.BlockSpec((CHUNK, D), lambda c: (ix(c), 0))] + [_whole(p) for p in small],
        out_shape=[SDS((t, SSM_XBC), F32), SDS((t, LANE), BF16), SDS((t, D), BF16)] + [SDS(p.shape, F32) for p in small],
        scratch_shapes=[pltpu.VMEM((N_PAIR, LANE, LANE), F32)],
        compiler_params=_params(("arbitrary",)),
    )(xa, proj, proj, states, dy, *small)


def _attn_block(q, kv_prev, kv_cur, cq, sq, ck, sk, sinks, rot, first_block):
    left, right = _half_masks()
    k2 = jnp.concatenate([kv_prev[:, :256], kv_cur[:, :256]], axis=0)
    v2 = jnp.concatenate([kv_prev[:, 256:], kv_cur[:, 256:]], axis=0)
    ri, ci = _iota((LANE, LANE), 0), _iota((LANE, LANE), 1)
    dup = [((ri < 64) & (ci % 64 == ri)).astype(BF16), ((ri >= 64) & (ci % 64 == ri - 64)).astype(BF16)]

    def rope(tt, c, s):
        return tt * c + jnp.dot(tt, rot, precision=HI) * s

    kd, vd = [], []
    for j in range(4):
        sl = slice(LANE * (j // 2), LANE * (j // 2 + 1))
        kd.append(_bdot(rope(k2[:, sl], ck, sk), dup[j % 2], "nn"))
        vd.append(_bdot(v2[:, sl], dup[j % 2], "nn"))
    qi, si = _iota((2 * CHUNK, 2 * CHUNK), 0) % CHUNK, _iota((2 * CHUNK, 2 * CHUNK), 1)
    valid = (si > qi) & (si <= qi + CHUNK) & jnp.logical_or(si >= CHUNK, jnp.logical_not(first_block))
    upper = _iota((2 * CHUNK, 1), 0) < CHUNK
    lanes = _iota((1, LANE), 1)
    outs = []
    for pr in range(N_PAIR):
        qr = rope(q[:, LANE * pr:LANE * (pr + 1)], cq, sq)
        lg = _bdot(jnp.concatenate([qr * left, qr * right], axis=0), kd[pr // 2], "nt") * 0.125
        lg = jnp.where(valid, lg, -jnp.inf)
        s1 = jnp.sum(jnp.where(lanes == 2 * pr, sinks, 0.0), axis=1, keepdims=True)
        s2 = jnp.sum(jnp.where(lanes == 2 * pr + 1, sinks, 0.0), axis=1, keepdims=True)
        sink = jnp.where(upper, s1, s2)
        mx = lax.stop_gradient(jnp.maximum(jnp.max(lg, axis=-1, keepdims=True), sink))
        e = jnp.exp(lg - mx)
        probs = e / (jnp.sum(e, axis=-1, keepdims=True) + jnp.exp(sink - mx))
        o2 = _bdot(probs, vd[pr // 2], "nn")
        outs.append(o2[:CHUNK] * left + o2[CHUNK:] * right)
    return jnp.concatenate(outs, axis=1)


def _attn_fwd(qkv, cos, sin, sinks, rot, name):
    t = qkv.shape[0]
    nb = t // CHUNK

    def body(q_ref, kvp_ref, kvc_ref, cq_ref, sq_ref, cp_ref, sp_ref, sinks_ref, rot_ref, o_ref):
        ck = jnp.concatenate([cp_ref[...], cq_ref[...]], axis=0)
        sk = jnp.concatenate([sp_ref[...], sq_ref[...]], axis=0)
        o_ref[...] = _attn_block(q_ref[...], kvp_ref[...], kvc_ref[...], cq_ref[...], sq_ref[...], ck, sk,
                                 sinks_ref[...], rot_ref[...], pl.program_id(0) == 0).astype(BF16)

    prev = lambda n: jnp.maximum(n - 1, 0)
    return pl.pallas_call(
        body, name=name, grid=(nb,),
        in_specs=[pl.BlockSpec((CHUNK, D), lambda n: (n, 0)),
                  pl.BlockSpec((CHUNK, 512), lambda n: (prev(n), 2)), pl.BlockSpec((CHUNK, 512), lambda n: (n, 2)),
                  pl.BlockSpec((CHUNK, LANE), lambda n: (n, 0)), pl.BlockSpec((CHUNK, LANE), lambda n: (n, 0)),
                  pl.BlockSpec((CHUNK, LANE), lambda n: (prev(n), 0)), pl.BlockSpec((CHUNK, LANE), lambda n: (prev(n), 0)),
                  _whole(sinks), _whole(rot)],
        out_specs=pl.BlockSpec((CHUNK, D), lambda n: (n, 0)), out_shape=SDS((t, D), BF16),
        compiler_params=_params(("parallel",)),
    )(qkv, qkv, qkv, cos, sin, cos, sin, sinks, rot)


def _attn_bwd(qkv, do, cos, sin, sinks, rot, name):
    t = qkv.shape[0]
    nb = t // CHUNK

    def body(q_ref, kvp_ref, kvc_ref, do_ref, cq_ref, sq_ref, cp_ref, sp_ref, sinks_ref, rot_ref,
             dq_ref, dkv_ref, dbq_ref, dbkv_ref, dsink_ref, carry_ref):
        n = pl.program_id(0)

        @pl.when(n == 0)
        def _():
            carry_ref[...] = jnp.zeros_like(carry_ref)
            dbq_ref[...] = jnp.zeros_like(dbq_ref)
            dbkv_ref[...] = jnp.zeros_like(dbkv_ref)
            dsink_ref[...] = jnp.zeros_like(dsink_ref)

        @pl.when(n < nb)
        def _():
            ck = jnp.concatenate([cp_ref[...], cq_ref[...]], axis=0)
            sk = jnp.concatenate([sp_ref[...], sq_ref[...]], axis=0)
            f = lambda q, kvp, kvc, s: _attn_block(q, kvp, kvc, cq_ref[...], sq_ref[...], ck, sk, s, rot_ref[...], n == 0)
            _, vjp = jax.vjp(f, q_ref[...], kvp_ref[...], kvc_ref[...], sinks_ref[...])
            dq, dkvp, dkvc, ds = vjp(do_ref[...].astype(F32))
            done = carry_ref[...] + dkvp
            dq_ref[...] = dq.astype(BF16)
            dkv_ref[...] = done.astype(BF16)
            dbq_ref[...] += jnp.sum(dq, axis=0, keepdims=True)
            dsink_ref[...] += ds
            carry_ref[...] = dkvc

            @pl.when(n > 0)
            def _():
                dbkv_ref[...] += jnp.sum(done, axis=0, keepdims=True)

        @pl.when(n == nb)
        def _():
            done = carry_ref[...]
            dkv_ref[...] = done.astype(BF16)
            dbkv_ref[...] += jnp.sum(done, axis=0, keepdims=True)

    cur = lambda n: jnp.minimum(n, nb - 1)
    prev = lambda n: jnp.maximum(jnp.minimum(n, nb - 1) - 1, 0)
    fin = lambda n: jnp.maximum(n - 1, 0)
    outs = pl.pallas_call(
        body, name=name, grid=(nb + 1,),
        in_specs=[pl.BlockSpec((CHUNK, D), lambda n: (cur(n), 0)),
                  pl.BlockSpec((CHUNK, 512), lambda n: (prev(n), 2)), pl.BlockSpec((CHUNK, 512), lambda n: (cur(n), 2)),
                  pl.BlockSpec((CHUNK, D), lambda n: (cur(n), 0)),
                  pl.BlockSpec((CHUNK, LANE), lambda n: (cur(n), 0)), pl.BlockSpec((CHUNK, LANE), lambda n: (cur(n), 0)),
                  pl.BlockSpec((CHUNK, LANE), lambda n: (prev(n), 0)), pl.BlockSpec((CHUNK, LANE), lambda n: (prev(n), 0)),
                  _whole(sinks), _whole(rot)],
        out_specs=[pl.BlockSpec((CHUNK, D), lambda n: (cur(n), 0)), pl.BlockSpec((CHUNK, 512), lambda n: (fin(n), 0)),
                   pl.BlockSpec((1, D), lambda n: (0, 0)), pl.BlockSpec((1, 512), lambda n: (0, 0)), _whole(sinks)],
        out_shape=[SDS((t, D), BF16), SDS((t, 512), BF16), SDS((1, D), F32), SDS((1, 512), F32), SDS(sinks.shape, F32)],
        scratch_shapes=[pltpu.VMEM((CHUNK, 512), F32)],
        compiler_params=_params(("arbitrary",)),
    )(qkv, qkv, qkv, do, cos, sin, cos, sin, sinks, rot)
    dq, dkv, dbq, dbkv, dsinks = outs
    return jnp.concatenate([dq, dkv], axis=1), jnp.concatenate([dbq, dbkv], axis=1), dsinks


def _loss_head(h, tgt, g, name, tm=512):
    t = h.shape[0]

    def body(h_ref, t_ref, g_ref, loss_ref, dh_ref, dhb_ref, dg_ref):
        def f(hv, gv):
            err = _rms_fn(hv, gv)[0] - t_ref[...]
            return 0.5 * jnp.sum(jnp.mean(err * err, axis=-1, keepdims=True), axis=0, keepdims=True)

        loss, vjp = jax.vjp(f, h_ref[...], g_ref[...])
        dh, dg = vjp(jnp.ones((1, 1), F32))
        dh_ref[...] = dh
        dhb_ref[...] = dh.astype(BF16)
        first = pl.program_id(0) == 0

        @pl.when(first)
        def _():
            loss_ref[...] = loss
            dg_ref[...] = dg

        @pl.when(jnp.logical_not(first))
        def _():
            loss_ref[...] += loss
            dg_ref[...] += dg

    return pl.pallas_call(
        body, name=name, grid=(t // tm,),
        in_specs=[_tok(D, tm), _tok(D, tm), _whole(g)],
        out_specs=[pl.BlockSpec((1, 1), lambda i: (0, 0)), _tok(D, tm), _tok(D, tm), _whole(g)],
        out_shape=[SDS((1, 1), F32), SDS((t, D), F32), SDS((t, D), BF16), SDS(g.shape, F32)],
        compiler_params=_params(("arbitrary",)),
    )(h, tgt, g)


def _res_half(acc, res):
    return (res + 0.5 * acc,)


def _res_full(acc, res):
    return (res + acc,)


def _half(acc):
    return (0.5 * acc,)


def _ffn_fwd(h, g, w_in, w_out, tag):
    t = h.shape[0]
    tm = 512
    n = _rms(h, g, f"{tag}_rms")
    pre = _mm(n[None], w_in, tn=FF_SHARD, name=f"{tag}_in")[0]
    half = lambda o: pl.BlockSpec((None, tm, FF_SHARD), lambda j, i, o=o: (j + o, i, 0))
    act = _rowop(_swiglu_fn, [(pre, half(0)), (pre, half(4))], [], [((4, t, FF_SHARD), BF16, half(0))],
                 grid=(4, t // tm), name=f"{tag}_act")[0]
    out = _mm(act, w_out, reduce_j=True, tk=FF_SHARD, epi=_res_half, extras=(h[None],), name=f"{tag}_out")[0][0]
    return out, (h, n, pre, act)


def _ffn_bwd(dh, dhb, saved, g, w_in, w_out, tag):
    h, n, pre, act = saved
    t = h.shape[0]
    tm = 512
    dact = _mm(dhb[None], w_out, tb=True, tn=FF_SHARD, epi=_half, name=f"{tag}_dact")[0]
    dw_out = _mm(act, dhb[None], ta=True, tm=FF_SHARD, epi=_half, name=f"{tag}_dwout")[0]
    half = lambda o: pl.BlockSpec((None, tm, FF_SHARD), lambda j, i, o=o: (j + o, i, 0))
    pair = pl.BlockSpec((2, None, tm, FF_SHARD), lambda j, i: (0, j, i, 0))
    dpre = _rowop_bwd(_swiglu_fn, [(pre, half(0)), (pre, half(4))], [], [(dact, half(0))], [(0, 1)],
                      [((2, 4, t, FF_SHARD), (BF16,), pair)], grid=(4, t // tm), name=f"{tag}_dswiglu")[0]
    dpre = dpre.reshape(N_DEV, t, FF_SHARD)
    dn = _mm(dpre, w_in, tb=True, reduce_j=True, tk=FF_SHARD, name=f"{tag}_dn")[0][0]
    dw_in = _mm(n[None], dpre, ta=True, tn=FF_SHARD, name=f"{tag}_dwin")[0]
    dh_in, dhb_in, dg = _rms_bwd(h, g, dn, dh, f"{tag}_drms")
    return dh_in, dhb_in, dg, dw_in, dw_out


def _ple_fwd(h, g, pb, w_gate, w_proj, tag):
    t = h.shape[0]
    tm = 512
    n = _rms(h, g, f"{tag}_rms")
    e = _mm(pb[None], w_proj[None], name=f"{tag}_proj")[0][0]
    z = _mm(n[None], w_gate[None], name=f"{tag}_gate")[0][0]
    out = _rowop(lambda zz, ee, hh: (hh + _ple_fn(zz, ee)[0],), [(z, _tok(D, tm)), (e, _tok(D, tm)), (h, _tok(D, tm))], [],
                 [((t, D), F32, _tok(D, tm))], grid=(t // tm,), name=f"{tag}_mix")[0]
    return out, (h, n, e, z)


def _ple_bwd(dh, dhb, saved, g, pb, w_gate, tag):
    h, n, e, z = saved
    t = h.shape[0]
    tm = 512
    dz, de = _rowop_bwd(_ple_fn, [(z, _tok(D, tm)), (e, _tok(D, tm))], [], [(dh, _tok(D, tm))], [(0,), (1,)],
                        [((t, D), (BF16,), _tok(D, tm)), ((t, D), (BF16,), _tok(D, tm))], grid=(t // tm,), name=f"{tag}_dmix")
    dw_proj = _mm(pb[None], de[None], ta=True, name=f"{tag}_dwproj")[0][0]
    dw_gate = _mm(n[None], dz[None], ta=True, name=f"{tag}_dwgate")[0][0]
    dn = _mm(dz[None], w_gate[None], tb=True, name=f"{tag}_dn")[0][0]
    dh_in, dhb_in, dg = _rms_bwd(h, g, dn, dh, f"{tag}_drms")
    return dh_in, dhb_in, dg, dw_gate, dw_proj


def _hyb_fwd(h, w, tag):
    t = h.shape[0]
    tm = 512
    hn = _rms(h, w["norm_mix"], f"{tag}_rms")
    proj = _mm(hn[None], w["hyb_in"][None], tn=512, name=f"{tag}_in")[0][0]
    u1 = _dwconv([(proj, 0), (proj, D // LANE)], w["conv_w"], w["conv_b"], width=CONV_W, glu=True, silu=False, cb=LANE,
                 name=f"{tag}_conv")
    u = _rowop(_lnswish_fn, [(u1, _tok(D, tm))], [w["ln_g"], w["ln_b"]], [((t, D), BF16, _tok(D, tm))], grid=(t // tm,),
               name=f"{tag}_ln")[0]
    xa = _dwconv([(proj, 3 * D // LANE)], w["sconv_w"], w["sconv_b"], width=SSM_CONV, glu=False, silu=True, cb=LANE,
                 name=f"{tag}_sconv")
    y, states = _ssd_fwd(xa, proj, w["dt_bias"], w["a_log"], w["d_skip"], w["ssm_norm"], f"{tag}_ssd")
    mixed = jnp.stack([u, y], axis=0)
    out = _mm(mixed, w["hyb_out"], reduce_j=True, epi=_res_full, extras=(h[None],), name=f"{tag}_out")[0][0]
    return out, (h, hn, proj, u1, xa, states, mixed)


def _hyb_bwd(dh, dhb, saved, w, tag):
    h, hn, proj, u1, xa, states, mixed = saved
    t = h.shape[0]
    tm = 512
    dmix = _mm(dhb[None], w["hyb_out"], tb=True, name=f"{tag}_dmix")[0]
    dw_out = _mm(mixed, dhb[None], ta=True, name=f"{tag}_dwout")[0]
    du1, dln_g, dln_b = _rowop_bwd(_lnswish_fn, [(u1, _tok(D, tm))], [w["ln_g"], w["ln_b"]], [(dmix[0], _tok(D, tm))], [(0,)],
                                   [((t, D), (F32,), _tok(D, tm))], grid=(t // tm,), name=f"{tag}_dln")
    dval, dgate, dconv_w, dconv_b = _dwconv_bwd([(proj, 0), (proj, D // LANE)], w["conv_w"], w["conv_b"], du1,
                                                width=CONV_W, glu=True, silu=False, cb=LANE, name=f"{tag}_dconv")
    dxa, ddt, dz, g_dtb, g_alog, g_dsk, g_ng = _ssd_bwd(xa, proj, states, dmix[1], w["dt_bias"], w["a_log"], w["d_skip"],
                                                         w["ssm_norm"], f"{tag}_dssd")
    dxbc, dsconv_w, dsconv_b = _dwconv_bwd([(proj, 3 * D // LANE)], w["sconv_w"], w["sconv_b"], dxa, width=SSM_CONV,
                                           glu=False, silu=True, cb=LANE, name=f"{tag}_dsconv")
    dproj = jnp.concatenate([dval, dgate, dz, dxbc, ddt, jnp.zeros((t, HYB_PAD - DT_COL - LANE), BF16)], axis=1)
    dhn = _mm(dproj[None], w["hyb_in"][None], tb=True, name=f"{tag}_dhn")[0][0]
    dw_in = _mm(hn[None], dproj[None], ta=True, tn=512, name=f"{tag}_dwin")[0][0]
    dh_in, dhb_in, dg = _rms_bwd(h, w["norm_mix"], dhn, dh, f"{tag}_drms")
    grads = dict(norm_mix=dg, hyb_in=dw_in, hyb_out=dw_out, conv_w=dconv_w, conv_b=dconv_b, ln_g=dln_g, ln_b=dln_b,
                 sconv_w=dsconv_w, sconv_b=dsconv_b, dt_bias=g_dtb, a_log=g_alog, d_skip=g_dsk, ssm_norm=g_ng)
    return dh_in, dhb_in, grads


def _bias_epi(acc, row):
    return (acc + row,)


def _res_bias_epi(acc, res, row):
    return (res + acc + row,)


def _att_fwd(h, w, tables, tag):
    cos, sin, rot = tables
    hn = _rms(h, w["norm_mix"], f"{tag}_rms")
    qkv = _mm(hn[None], w["qkv"][None], tn=512, epi=_bias_epi, rows=(w["b_qkv"],), name=f"{tag}_qkv")[0][0]
    o = _attn_fwd(qkv, cos, sin, w["sinks"], rot, f"{tag}_core")
    out = _mm(o[None], w["w_o"][None], epi=_res_bias_epi, extras=(h[None],), rows=(w["b_o"],), name=f"{tag}_out")[0][0]
    return out, (h, hn, qkv, o)


def _att_bwd(dh, dhb, saved, w, tables, tag):
    cos, sin, rot = tables
    h, hn, qkv, o = saved
    t = h.shape[0]
    tm = 512
    do = _mm(dhb[None], w["w_o"][None], tb=True, out_dtypes=(BF16,), name=f"{tag}_do")[0][0]
    dw_o = _mm(o[None], dhb[None], ta=True, name=f"{tag}_dwo")[0][0]
    db_o = _rowop_bwd(lambda xx, bb: (xx + bb,), [(dh, _tok(D, tm))], [w["b_o"]], [(dh, _tok(D, tm))], [], [],
                      grid=(t // tm,), name=f"{tag}_dbo")[0]
    dqkv, db_qkv, dsinks = _attn_bwd(qkv, do, cos, sin, w["sinks"], rot, f"{tag}_dcore")
    dhn = _mm(dqkv[None], w["qkv"][None], tb=True, tk=512, name=f"{tag}_dhn")[0][0]
    dw_qkv = _mm(hn[None], dqkv[None], ta=True, tn=512, name=f"{tag}_dwqkv")[0][0]
    dh_in, dhb_in, dg = _rms_bwd(h, w["norm_mix"], dhn, dh, f"{tag}_drms")
    grads = dict(norm_mix=dg, qkv=dw_qkv, b_qkv=db_qkv, sinks=dsinks, w_o=dw_o, b_o=db_o)
    return dh_in, dhb_in, grads


def _rope_tables(t):
    inv = ROPE_THETA ** (-jnp.arange(0, 64, 2, dtype=F32) / 64)
    ang = jnp.arange(t, dtype=F32)[:, None] * inv[None, :]
    cos, sin = jnp.tile(jnp.cos(ang), (1, 4)), jnp.tile(jnp.sin(ang), (1, 4))
    rot = np.zeros((LANE, LANE), np.float32)
    for j in range(LANE):
        if j % 64 < 32:
            rot[j + 32, j] = -1.0
        else:
            rot[j - 32, j] = 1.0
    return cos, sin, jnp.asarray(rot)


def _local_step(x, p, tgt, layers, final_norm):
    t = x.shape[0]
    tables = _rope_tables(t)
    pb = p.astype(BF16)
    h, saved = x, []
    for i, w in enumerate(layers):
        s = {}
        h, s["ffn1"] = _ffn_fwd(h, w["norm_ffn1"], w["ffn1_in"], w["ffn1_out"], f"l{i}_ffn1")
        if i % 2 == 0:
            h, s["mix"] = _hyb_fwd(h, w, f"l{i}_hyb")
        else:
            h, s["mix"] = _att_fwd(h, w, tables, f"l{i}_att")
        h, s["ffn2"] = _ffn_fwd(h, w["norm_ffn2"], w["ffn2_in"], w["ffn2_out"], f"l{i}_ffn2")
        h, s["ple"] = _ple_fwd(h, w["ple_norm"], pb[i], w["ple_gate"], w["ple_proj"], f"l{i}_ple")
        saved.append(s)
    loss, dh, dhb, d_final = _loss_head(h, tgt, final_norm, "loss_head")
    grads = [None] * len(layers)
    for i in reversed(range(len(layers))):
        w, s, g = layers[i], saved[i], {}
        dh, dhb, g["ple_norm"], g["ple_gate"], g["ple_proj"] = _ple_bwd(dh, dhb, s["ple"], w["ple_norm"], pb[i], w["ple_gate"],
                                                                        f"l{i}_ple")
        dh, dhb, g["norm_ffn2"], g["ffn2_in"], g["ffn2_out"] = _ffn_bwd(dh, dhb, s["ffn2"], w["norm_ffn2"], w["ffn2_in"],
                                                                        w["ffn2_out"], f"l{i}_ffn2")
        if i % 2 == 0:
            dh, dhb, gm = _hyb_bwd(dh, dhb, s["mix"], w, f"l{i}_hyb")
        else:
            dh, dhb, gm = _att_bwd(dh, dhb, s["mix"], w, tables, f"l{i}_att")
        g.update(gm)
        dh, dhb, g["norm_ffn1"], g["ffn1_in"], g["ffn1_out"] = _ffn_bwd(dh, dhb, s["ffn1"], w["norm_ffn1"], w["ffn1_in"],
                                                                        w["ffn1_out"], f"l{i}_ffn1")
        grads[i] = g
    return loss[0, 0], dh, grads, d_final


def _cols(g):
    full = jnp.moveaxis(g, 0, -2)
    return full.reshape(*full.shape[:-2], N_DEV * g.shape[-1])


def _uncols(full):
    split = full.reshape(*full.shape[:-1], N_DEV, full.shape[-1] // N_DEV)
    return jnp.moveaxis(split, -2, 0)


def _lane_pad(v):
    return jnp.pad(v, ((0, 0), (0, LANE - v.shape[1])))


def _build_layers(gw, gs, rep):
    layers = []
    for i in range(2):
        w = {}
        for f in ("ffn1", "ffn2"):
            w[f"norm_{f}"] = rep[f"norm_{f}"][i][None]
            w[f"{f}_in"] = gw[f"{f}_w_in"][:, i]
            w[f"{f}_out"] = gw[f"{f}_w_out"][:, i].reshape(4, FF_SHARD, D)
        w["norm_mix"] = rep["norm_mix"][i][None]
        w["ple_norm"] = rep["ple_norm"][i][None]
        w["ple_gate"] = gw["ple_gate_w"][:, i].reshape(D, D)
        w["ple_proj"] = _cols(gw["ple_proj_w"][:, i])
        if i == 0:
            w["hyb_in"] = jnp.pad(_cols(gw["hyb_w_in"][:, 0]), ((0, 0), (0, HYB_PAD - HYB_IN)))
            w["hyb_out"] = gw["hyb_w_out"][:, 0].reshape(2, D, D)
            w["conv_w"] = _cols(gs["conv_dw_w"][:, 0])
            w["sconv_w"] = _cols(gs["ssm_conv_w"][:, 0])
            w["conv_b"], w["ln_g"], w["ln_b"] = rep["conv_dw_b"], rep["conv_ln_g"], rep["conv_ln_b"]
            w["sconv_b"], w["ssm_norm"] = rep["ssm_conv_b"], rep["ssm_norm"]
            w["dt_bias"], w["a_log"], w["d_skip"] = (_lane_pad(rep[k]) for k in ("ssm_dt_bias", "ssm_a_log", "ssm_d"))
        else:
            w["qkv"] = _cols(gw["att_w_qkv"][:, 0])
            w["w_o"] = gw["att_w_o"][:, 0].reshape(D, D)
            w["b_qkv"] = gs["att_b_qkv"][:, 0].reshape(1, -1)
            w["b_o"] = gs["att_b_o"][:, 0].reshape(1, -1)
            w["sinks"] = _lane_pad(rep["att_sinks"])
        layers.append(w)
    return layers


def _collect_grads(grads, d_final):
    g0, g1 = grads
    big, small = {}, {}
    for f in ("ffn1", "ffn2"):
        big[f"{f}_w_in"] = jnp.stack([g[f"{f}_in"] for g in grads], axis=1)
        big[f"{f}_w_out"] = jnp.stack([g[f"{f}_out"].reshape(N_DEV, D_FF // N_DEV, D) for g in grads], axis=1)
        small[f"norm_{f}"] = jnp.concatenate([g[f"norm_{f}"] for g in grads], axis=0)
    big["ple_gate_w"] = jnp.stack([g["ple_gate"].reshape(N_DEV, D // N_DEV, D) for g in grads], axis=1)
    big["ple_proj_w"] = jnp.stack([_uncols(g["ple_proj"]) for g in grads], axis=1)
    big["hyb_w_in"] = _uncols(g0["hyb_in"][:, :HYB_IN])[:, None]
    big["hyb_w_out"] = g0["hyb_out"].reshape(N_DEV, 2 * D // N_DEV, D)[:, None]
    big["att_w_qkv"] = _uncols(g1["qkv"])[:, None]
    big["att_w_o"] = g1["w_o"].reshape(N_DEV, D // N_DEV, D)[:, None]
    small["norm_mix"] = jnp.concatenate([g["norm_mix"] for g in grads], axis=0)
    small["ple_norm"] = jnp.concatenate([g["ple_norm"] for g in grads], axis=0)
    small["conv_dw_w"] = g0["conv_w"][None]
    small["conv_dw_b"], small["conv_ln_g"], small["conv_ln_b"] = g0["conv_b"], g0["ln_g"], g0["ln_b"]
    small["ssm_conv_w"] = g0["sconv_w"][None]
    small["ssm_conv_b"], small["ssm_norm"] = g0["sconv_b"], g0["ssm_norm"]
    small["ssm_dt_bias"], small["ssm_a_log"], small["ssm_d"] = (g0[k][:, :SSM_HEADS] for k in ("dt_bias", "a_log", "d_skip"))
    small["att_b_qkv"], small["att_b_o"] = g1["b_qkv"], g1["b_o"]
    small["att_sinks"] = g1["sinks"][:, :SSM_HEADS]
    small["final_norm"] = d_final[0]
    return big, small


MESH = pl.DeviceIdType.MESH


def _place():
    return lax.axis_index("x"), lax.axis_index("y"), lax.axis_index("c")


def _all_gather(block, space, name):
    def body(x_ref, out_ref, send_sems, recv_sems, local_sem):
        x, y, c = _place()
        me, sibling = (x, y, c), (x, y, 1 - c)
        chips = [(1 - x, y), (x, 1 - y), (1 - x, 1 - y)]

        def slot(px, py, pc):
            return out_ref.at[4 * px + 2 * py + pc]

        def copy(k, blk, to, src=None):
            return pltpu.make_async_remote_copy(src_ref=slot(*blk) if src is None else src, dst_ref=slot(*blk),
                                                send_sem=send_sems.at[k], recv_sem=recv_sems.at[k], device_id=to,
                                                device_id_type=MESH)

        mine = pltpu.make_async_copy(x_ref, slot(*me), local_sem)
        mine.start()
        first = [copy(0, me, sibling, src=x_ref)] + [copy(1 + j, me, (*chip, c), src=x_ref) for j, chip in enumerate(chips)]
        for cp in first:
            cp.start()
        passed = [copy(4 + j, (*chip, c), sibling) for j, chip in enumerate(chips)]
        for j, chip in enumerate(chips):
            copy(1 + j, (*chip, c), me).wait_recv()
            passed[j].start()
        copy(0, sibling, me).wait_recv()
        for j, chip in enumerate(chips):
            copy(4 + j, (*chip, 1 - c), me).wait_recv()
        for cp in first + passed:
            cp.wait_send()
        mine.wait()

    return pl.pallas_call(
        body, name=name, out_shape=SDS((N_DEV,) + block.shape, block.dtype),
        in_specs=[pl.BlockSpec(memory_space=space)], out_specs=pl.BlockSpec(memory_space=space),
        scratch_shapes=[pltpu.SemaphoreType.DMA((7,)), pltpu.SemaphoreType.DMA((7,)), pltpu.SemaphoreType.DMA],
    )(block)


def _pair_exchange(parts, name):
    def body(p_ref, got_ref, send_sems, recv_sems):
        x, y, c = _place()
        copies = [pltpu.make_async_remote_copy(src_ref=p_ref.at[2 * q + (1 - c)], dst_ref=got_ref.at[q],
                                               send_sem=send_sems.at[q], recv_sem=recv_sems.at[q], device_id=(x, y, 1 - c),
                                               device_id_type=MESH) for q in range(4)]
        for cp in copies:
            cp.start()
        for cp in copies:
            cp.wait_recv()
        for cp in copies:
            cp.wait_send()

    hbm = pl.BlockSpec(memory_space=pltpu.HBM)
    return pl.pallas_call(
        body, name=name, out_shape=SDS((4,) + parts.shape[1:], parts.dtype), in_specs=[hbm], out_specs=hbm,
        scratch_shapes=[pltpu.SemaphoreType.DMA((4,)), pltpu.SemaphoreType.DMA((4,))],
    )(parts)


def _row_tile(r, cap=4608):
    return max(d for d in range(16, min(r, cap) + 1, 16) if r % d == 0)


def _pair_add(parts, got, core, name):
    _, r, cdim = parts.shape
    tr = _row_tile(r)

    def body(core_ref, p_ref, g_ref, o_ref):
        o_ref[...] = (p_ref[...].astype(F32) + g_ref[...].astype(F32)).astype(o_ref.dtype)

    return pl.pallas_call(
        body, name=name, out_shape=SDS((4, r, cdim), BF16),
        grid_spec=pltpu.PrefetchScalarGridSpec(
            num_scalar_prefetch=1, grid=(4, r // tr),
            in_specs=[pl.BlockSpec((None, tr, cdim), lambda q, i, core_ref: (2 * q + core_ref[0], i, 0)),
                      pl.BlockSpec((None, tr, cdim), lambda q, i, core_ref: (q, i, 0))],
            out_specs=pl.BlockSpec((None, tr, cdim), lambda q, i, core_ref: (q, i, 0))),
        compiler_params=_params(("parallel", "parallel")),
    )(core, parts, got)


def _chip_exchange(sums, name):
    def body(b_ref, out_ref, send_sems, recv_sems, local_sem):
        x, y, c = _place()
        own = 2 * x + y
        chips = [(1 - x, y), (x, 1 - y), (1 - x, 1 - y)]

        def copy(k, chip, src_slot, dst_slot):
            return pltpu.make_async_remote_copy(src_ref=b_ref.at[src_slot], dst_ref=out_ref.at[dst_slot],
                                                send_sem=send_sems.at[k], recv_sem=recv_sems.at[k], device_id=(*chip, c),
                                                device_id_type=MESH)

        mine = pltpu.make_async_copy(b_ref.at[own], out_ref.at[own], local_sem)
        mine.start()
        sends = [copy(k, chip, 2 * chip[0] + chip[1], own) for k, chip in enumerate(chips)]
        for cp in sends:
            cp.start()
        for k, chip in enumerate(chips):
            copy(k, chip, own, 2 * chip[0] + chip[1]).wait_recv()
        for cp in sends:
            cp.wait_send()
        mine.wait()

    hbm = pl.BlockSpec(memory_space=pltpu.HBM)
    return pl.pallas_call(
        body, name=name, out_shape=SDS(sums.shape, sums.dtype), in_specs=[hbm], out_specs=hbm,
        scratch_shapes=[pltpu.SemaphoreType.DMA((3,)), pltpu.SemaphoreType.DMA((3,)), pltpu.SemaphoreType.DMA],
    )(sums)


def _sum_slots(parts, name):
    nj, r, cdim = parts.shape
    tr = _row_tile(r)

    def body(p_ref, o_ref):
        acc = p_ref[0].astype(F32)
        for j in range(1, nj):
            acc = acc + p_ref[j].astype(F32)
        o_ref[...] = acc

    return pl.pallas_call(
        body, name=name, out_shape=SDS((r, cdim), F32), grid=(r // tr,),
        in_specs=[pl.BlockSpec((nj, tr, cdim), lambda i: (0, i, 0))], out_specs=pl.BlockSpec((tr, cdim), lambda i: (i, 0)),
        compiler_params=_params(("parallel",)),
    )(parts)


def _adamw(w, g, m, v, name):
    shape = w.shape
    cdim = shape[-1]
    w2, g2, m2, v2 = (a.reshape(-1, cdim) for a in (w, g, m, v))
    r = w2.shape[0]
    tr = next(d for d in (512, 352, 256, 128, 64, 32, 16, 8, r) if r % d == 0)

    def body(w_ref, g_ref, m_ref, v_ref, d_ref, nm_ref, nv_ref):
        gv = g_ref[...]
        nm = ADAM_B1 * m_ref[...] + (1.0 - ADAM_B1) * gv
        nv = ADAM_B2 * v_ref[...] + (1.0 - ADAM_B2) * (gv * gv)
        m_hat = nm / (1.0 - ADAM_B1 ** ADAM_STEP)
        v_hat = nv / (1.0 - ADAM_B2 ** ADAM_STEP)
        d_ref[...] = -ADAM_LR * (m_hat / (jnp.sqrt(v_hat) + ADAM_EPS) + ADAM_WD * w_ref[...])
        nm_ref[...] = nm
        nv_ref[...] = nv

    spec = pl.BlockSpec((tr, cdim), lambda i: (i, 0))
    outs = pl.pallas_call(
        body, name=name, grid=(r // tr,), in_specs=[spec] * 4, out_specs=[spec] * 3, out_shape=[SDS((r, cdim), F32)] * 3,
        compiler_params=_params(("parallel",)),
    )(w2, g2, m2, v2)
    return tuple(o.reshape(shape) for o in outs)


WEIGHTS = ("norm_ffn1", "ffn1_w_in", "ffn1_w_out", "norm_mix", "norm_ffn2", "ffn2_w_in", "ffn2_w_out", "ple_norm", "ple_gate_w",
           "ple_proj_w", "hyb_w_in", "conv_dw_w", "conv_dw_b", "conv_ln_g", "conv_ln_b", "ssm_conv_w", "ssm_conv_b", "ssm_dt_bias",
           "ssm_a_log", "ssm_d", "ssm_norm", "hyb_w_out", "att_w_qkv", "att_b_qkv", "att_sinks", "att_w_o", "att_b_o", "final_norm")
BIG = ("ffn1_w_in", "ffn1_w_out", "ffn2_w_in", "ffn2_w_out", "ple_gate_w", "ple_proj_w", "hyb_w_in", "hyb_w_out", "att_w_qkv",
       "att_w_o")
SMALL_SHARDED = {"conv_dw_w": 2, "ssm_conv_w": 2, "att_b_qkv": 1, "att_b_o": 1}
SMALL = tuple(n for n in WEIGHTS if n not in BIG)
PACK_ROWS = 16


def _pack(arrays, lead=0):
    pieces = []
    for a in arrays:
        flat = a.reshape(*a.shape[:lead], -1)
        size = flat.shape[-1]
        padded = -(-size // (PACK_ROWS * LANE)) * PACK_ROWS * LANE
        flat = jnp.pad(flat, [(0, 0)] * lead + [(0, padded - size)])
        pieces.append(flat.reshape(*a.shape[:lead], padded // LANE, LANE))
    return jnp.concatenate(pieces, axis=lead)


def _unpack(buf, shapes, lead=0):
    out, row = [], 0
    for shape in shapes:
        size = math.prod(shape)
        rows = -(-size // (PACK_ROWS * LANE)) * PACK_ROWS
        piece = lax.slice_in_dim(buf, row, row + rows, axis=lead)
        piece = piece.reshape(*buf.shape[:lead], rows * LANE)
        out.append(lax.slice_in_dim(piece, 0, size, axis=lead).reshape(*buf.shape[:lead], *shape))
        row += rows
    return out


def kernel(x, p, norm_ffn1, ffn1_w_in, ffn1_w_out, norm_mix, norm_ffn2, ffn2_w_in, ffn2_w_out, ple_norm, ple_gate_w, ple_proj_w, hyb_w_in, conv_dw_w, conv_dw_b, conv_ln_g, conv_ln_b, ssm_conv_w, ssm_conv_b, ssm_dt_bias, ssm_a_log, ssm_d, ssm_norm, hyb_w_out, att_w_qkv, att_b_qkv, att_sinks, att_w_o, att_b_o, final_norm, loss_target, m_norm_ffn1, m_ffn1_w_in, m_ffn1_w_out, m_norm_mix, m_norm_ffn2, m_ffn2_w_in, m_ffn2_w_out, m_ple_norm, m_ple_gate_w, m_ple_proj_w, m_hyb_w_in, m_conv_dw_w, m_conv_dw_b, m_conv_ln_g, m_conv_ln_b, m_ssm_conv_w, m_ssm_conv_b, m_ssm_dt_bias, m_ssm_a_log, m_ssm_d, m_ssm_norm, m_hyb_w_out, m_att_w_qkv, m_att_b_qkv, m_att_sinks, m_att_w_o, m_att_b_o, m_final_norm, v_norm_ffn1, v_ffn1_w_in, v_ffn1_w_out, v_norm_mix, v_norm_ffn2, v_ffn2_w_in, v_ffn2_w_out, v_ple_norm, v_ple_gate_w, v_ple_proj_w, v_hyb_w_in, v_conv_dw_w, v_conv_dw_b, v_conv_ln_g, v_conv_ln_b, v_ssm_conv_w, v_ssm_conv_b, v_ssm_dt_bias, v_ssm_a_log, v_ssm_d, v_ssm_norm, v_hyb_w_out, v_att_w_qkv, v_att_b_qkv, v_att_sinks, v_att_w_o, v_att_b_o, v_final_norm):
    args = (norm_ffn1, ffn1_w_in, ffn1_w_out, norm_mix, norm_ffn2, ffn2_w_in, ffn2_w_out, ple_norm, ple_gate_w, ple_proj_w, hyb_w_in, conv_dw_w, conv_dw_b, conv_ln_g, conv_ln_b, ssm_conv_w, ssm_conv_b, ssm_dt_bias, ssm_a_log, ssm_d, ssm_norm, hyb_w_out, att_w_qkv, att_b_qkv, att_sinks, att_w_o, att_b_o, final_norm)
    moments_m = (m_norm_ffn1, m_ffn1_w_in, m_ffn1_w_out, m_norm_mix, m_norm_ffn2, m_ffn2_w_in, m_ffn2_w_out, m_ple_norm, m_ple_gate_w, m_ple_proj_w, m_hyb_w_in, m_conv_dw_w, m_conv_dw_b, m_conv_ln_g, m_conv_ln_b, m_ssm_conv_w, m_ssm_conv_b, m_ssm_dt_bias, m_ssm_a_log, m_ssm_d, m_ssm_norm, m_hyb_w_out, m_att_w_qkv, m_att_b_qkv, m_att_sinks, m_att_w_o, m_att_b_o, m_final_norm)
    moments_v = (v_norm_ffn1, v_ffn1_w_in, v_ffn1_w_out, v_norm_mix, v_norm_ffn2, v_ffn2_w_in, v_ffn2_w_out, v_ple_norm, v_ple_gate_w, v_ple_proj_w, v_hyb_w_in, v_conv_dw_w, v_conv_dw_b, v_conv_ln_g, v_conv_ln_b, v_ssm_conv_w, v_ssm_conv_b, v_ssm_dt_bias, v_ssm_a_log, v_ssm_d, v_ssm_norm, v_hyb_w_out, v_att_w_qkv, v_att_b_qkv, v_att_sinks, v_att_w_o, v_att_b_o, v_final_norm)
    w = dict(zip(WEIGHTS, args))
    m = dict(zip(WEIGHTS, moments_m))
    v = dict(zip(WEIGHTS, moments_v))
    cx, cy, cc = _place()
    me = 4 * cx + 2 * cy + cc

    big_shapes = [w[n].shape for n in BIG]
    gathered = _all_gather(_pack([w[n].astype(BF16) for n in BIG]), pltpu.HBM, "gather_weights")
    gw = dict(zip(BIG, _unpack(gathered, big_shapes, lead=1)))
    ss_shapes = [w[n].shape for n in SMALL_SHARDED]
    gathered_small = _all_gather(_pack([w[n] for n in SMALL_SHARDED]), pltpu.VMEM, "gather_small_weights")
    gs = dict(zip(SMALL_SHARDED, _unpack(gathered_small, ss_shapes, lead=1)))
    rep = {n: w[n] for n in SMALL if n not in SMALL_SHARDED}

    layers = _build_layers(gw, gs, rep)
    loss, dx, grads, d_final = _local_step(x[0], p[:, 0], loss_target[0], layers, final_norm[None])
    big, small = _collect_grads(grads, d_final)
    loss = lax.psum(loss, ("x", "y", "c"))

    parts = _pack([big[n].astype(BF16) for n in BIG], lead=1)
    got = _pair_exchange(parts, "grads_pair_exchange")
    sums = _pair_add(parts, got, jnp.reshape(cc, (1,)).astype(jnp.int32), "grads_pair_add")
    by_chip = _chip_exchange(sums, "grads_chip_exchange")
    g = dict(zip(BIG, _unpack(_sum_slots(by_chip, "grads_sum"), big_shapes)))
    small_shapes = [small[n].shape for n in SMALL]
    all_small = _all_gather(_pack([small[n] for n in SMALL]), pltpu.VMEM, "gather_small_grads")
    small_full = dict(zip(SMALL, _unpack(_sum_slots(all_small, "small_grads_sum"), small_shapes)))
    for n in SMALL:
        g[n] = small_full[n]
        if n in SMALL_SHARDED:
            width = w[n].shape[SMALL_SHARDED[n]]
            g[n] = lax.dynamic_slice_in_dim(g[n], me * width, width, axis=SMALL_SHARDED[n])

    delta, new_m, new_v = {}, {}, {}
    for n in BIG:
        delta[n], new_m[n], new_v[n] = _adamw(w[n], g[n], m[n], v[n], f"adamw_{n}")
    packed = [_pack([d[n] for n in SMALL]) for d in (w, g, m, v)]
    shapes = [w[n].shape for n in SMALL]
    for d, buf in zip((delta, new_m, new_v), _adamw(*packed, "adamw_small")):
        d.update(zip(SMALL, _unpack(buf, shapes)))
    return (loss, dx[None], *[g[n] for n in WEIGHTS], *[delta[n] for n in WEIGHTS], *[new_m[n] for n in WEIGHTS],
            *[new_v[n] for n in WEIGHTS])
```

```python
import functools
import math

import numpy as np
import jax
import jax.numpy as jnp
from jax import lax
from jax.experimental import pallas as pl
from jax.experimental.pallas import tpu as pltpu

F32, BF16 = jnp.float32, jnp.bfloat16
HI = lax.Precision.HIGHEST
SDS = jax.ShapeDtypeStruct

N_DEV = 8
D = 1024
D_FF = 2816
FF_SHARD = 2 * D_FF // N_DEV
PLE_DIM = 256
EPS = 1e-6
CONV_W = 31
SSM_CONV = 4
SSM_HEADS = 16
SSM_XBC = 1536
CHUNK = 128
HYB_IN = 4624
HYB_PAD = 5120
DT_COL = 4608
N_PAIR = 8
ROPE_THETA = 10000.0
LANE = 128
VMEM_LIMIT = 56 * 1024 * 1024

ADAM_LR, ADAM_B1, ADAM_B2, ADAM_EPS, ADAM_WD, ADAM_STEP = 0.001, 0.9, 0.999, 1e-08, 0.01, 10


def _params(sem):
    return pltpu.CompilerParams(dimension_semantics=sem, vmem_limit_bytes=VMEM_LIMIT)


def _mm(a, b, *, ta=False, tb=False, reduce_j=False, out_dtypes=(F32,), tm=1024, tn=1024, tk=1024,
        epi=None, extras=(), rows=(), name):
    ja, jb = a.shape[0], b.shape[0]
    nj = max(ja, jb)
    jo = 1 if reduce_j else nj
    m, k = (a.shape[2], a.shape[1]) if ta else (a.shape[1], a.shape[2])
    n = b.shape[1] if tb else b.shape[2]
    assert (b.shape[2] if tb else b.shape[1]) == k and ja in (1, nj) and jb in (1, nj)
    tm, tn, tk = min(tm, m), min(tn, n), min(tk, k)
    assert m % tm == 0 and n % tn == 0 and k % tk == 0, (name, m, n, k, tm, tn, tk)
    nk = k // tk
    steps = nk * (nj if reduce_j else 1)
    ne, nr, no = len(extras), len(rows), len(out_dtypes)

    def a_map(i, c, j, kk):
        return (j if ja > 1 else 0, kk, i) if ta else (j if ja > 1 else 0, i, kk)

    def b_map(i, c, j, kk):
        return (j if jb > 1 else 0, c, kk) if tb else (j if jb > 1 else 0, kk, c)

    def o_map(i, c, j, kk):
        return (0 if reduce_j else j, i, c)

    dims = (((0 if ta else 1,), (1 if tb else 0,)), ((), ()))

    def body(a_ref, b_ref, *rest):
        ex, rw, outs = rest[:ne], rest[ne:ne + nr], rest[ne + nr:ne + nr + no]
        part = lax.dot_general(a_ref[...], b_ref[...], dims, preferred_element_type=F32)

        def finish(acc):
            res = epi(acc, *[e[...] for e in ex], *[r[...] for r in rw]) if epi else (acc,)
            for o, r in zip(outs, res):
                o[...] = r.astype(o.dtype)

        if steps == 1:
            finish(part)
            return
        acc_ref = rest[-1]
        kk = pl.program_id(3)
        step = pl.program_id(2) * nk + kk if reduce_j else kk

        @pl.when(step == 0)
        def _():
            acc_ref[...] = part

        @pl.when(step > 0)
        def _():
            acc_ref[...] += part

        @pl.when(step == steps - 1)
        def _():
            finish(acc_ref[...])

    o_spec = pl.BlockSpec((None, tm, tn), o_map)
    return pl.pallas_call(
        body, name=name, grid=(m // tm, n // tn, nj, nk),
        in_specs=[pl.BlockSpec((None, tk, tm) if ta else (None, tm, tk), a_map),
                  pl.BlockSpec((None, tn, tk) if tb else (None, tk, tn), b_map)]
        + [o_spec] * ne + [pl.BlockSpec((1, tn), lambda i, c, j, kk: (0, c))] * nr,
        out_specs=[o_spec] * no,
        out_shape=[SDS((jo, m, n), dt) for dt in out_dtypes],
        scratch_shapes=[pltpu.VMEM((tm, tn), F32)] if steps > 1 else [],
        compiler_params=_params(("parallel", "parallel", "arbitrary", "arbitrary")),
    )(a, b, *extras, *rows)


def _whole(p):
    return pl.BlockSpec(p.shape, lambda *_: (0,) * p.ndim)


def _rowop(fn, tiles, params, outs, *, grid, name):
    nin = len(tiles) + len(params)

    def body(*refs):
        res = fn(*[r[...].astype(F32) for r in refs[:nin]])
        for r, o in zip(refs[nin:], res):
            r[...] = o.astype(r.dtype)

    return pl.pallas_call(
        body, name=name, grid=grid,
        in_specs=[s for _, s in tiles] + [_whole(p) for p in params],
        out_specs=[s for _, _, s in outs], out_shape=[SDS(sh, dt) for sh, dt, _ in outs],
        compiler_params=_params(("parallel",) * len(grid)),
    )(*[t for t, _ in tiles], *params)


def _rowop_bwd(fn, tiles, params, cots, wrt, gouts, *, grid, name, adds=()):
    nt, npar, nc, na = len(tiles), len(params), len(cots), len(adds)
    nin = nt + npar
    flat = [i for grp in wrt for i in grp]
    n_gout = sum(len(dts) for _, dts, _ in gouts)

    def body(*refs):
        vals = [r[...].astype(F32) for r in refs[:nin]]
        cvals = [r[...].astype(F32) for r in refs[nin:nin + nc]]
        avals = [r[...].astype(F32) for r in refs[nin + nc:nin + nc + na]]
        orefs = refs[nin + nc + na:]
        diff_idx = flat + list(range(nt, nin))

        def f(*dv):
            full = list(vals)
            for i, v in zip(diff_idx, dv):
                full[i] = v
            return fn(*full)

        _, vjp = jax.vjp(f, *[vals[i] for i in diff_idx])
        grads = vjp(tuple(cvals))
        tile_g, par_g = list(grads[:len(flat)]), grads[len(flat):]
        group_g, at = [], 0
        for grp in wrt:
            members = tile_g[at:at + len(grp)]
            at += len(grp)
            group_g.append(members[0] if len(grp) == 1 else jnp.stack(members, axis=0))
        for av in avals:
            group_g[0] = group_g[0] + av
        o = 0
        for g, (_, dts, _) in zip(group_g, gouts):
            for _ in dts:
                orefs[o][...] = g.astype(orefs[o].dtype)
                o += 1
        first = functools.reduce(jnp.logical_and, [pl.program_id(ax) == 0 for ax in range(len(grid))])
        for r, g in zip(orefs[n_gout:], par_g):
            @pl.when(first)
            def _(r=r, g=g):
                r[...] = g

            @pl.when(jnp.logical_not(first))
            def _(r=r, g=g):
                r[...] += g

    out_specs, out_shape = [], []
    for sh, dts, spec in gouts:
        for dt in dts:
            out_specs.append(spec)
            out_shape.append(SDS(sh, dt))
    for p in params:
        out_specs.append(_whole(p))
        out_shape.append(SDS(p.shape, F32))
    return pl.pallas_call(
        body, name=name, grid=grid,
        in_specs=[s for _, s in tiles] + [_whole(p) for p in params] + [s for _, s in cots] + [s for _, s in adds],
        out_specs=out_specs, out_shape=out_shape,
        compiler_params=_params(("arbitrary",) * len(grid)),
    )(*[t for t, _ in tiles], *params, *[c for c, _ in cots], *[a for a, _ in adds])


def _tok(c, tm, col=0):
    return pl.BlockSpec((tm, c), lambda i, col=col: (i, col))


def _rms_fn(h, g):
    return (h * lax.rsqrt(jnp.mean(h * h, axis=-1, keepdims=True) + EPS) * g,)


def _lnswish_fn(u, g, b):
    mu = jnp.mean(u, axis=-1, keepdims=True)
    xc = u - mu
    y = xc * lax.rsqrt(jnp.mean(xc * xc, axis=-1, keepdims=True) + EPS) * g + b
    return (y * jax.nn.sigmoid(y),)


def _ple_fn(z, e):
    return (jax.nn.sigmoid(z) * e,)


def _rms(h, g, name, tm=512):
    t = h.shape[0]
    return _rowop(_rms_fn, [(h, _tok(D, tm))], [g], [((t, D), BF16, _tok(D, tm))], grid=(t // tm,), name=name)[0]


def _rms_bwd(h, g, dn, dres, name, tm=512):
    t = h.shape[0]
    return _rowop_bwd(_rms_fn, [(h, _tok(D, tm))], [g], [(dn, _tok(D, tm))], [(0,)], [((t, D), (F32, BF16), _tok(D, tm))],
                      grid=(t // tm,), name=name, adds=[(dres, _tok(D, tm))])


def _conv_geometry(width):
    pad = 32 if width > 8 else 8
    return pad, pad - (width - 1)


def _fill_shifts(xpad_ref, sh_ref, t, shifts):
    for r in shifts:
        sh_ref[r, :, :] = xpad_ref[pl.ds(r, t + 32), :]


def _dwconv(xs, w, b, *, width, glu, silu, cb, name):
    t = xs[0][0].shape[0]
    c = w.shape[1]
    pad, off = _conv_geometry(width)
    shifts = sorted({(k + off) % 8 for k in range(width)})
    ch = 32

    def body(*refs):
        x_refs, (w_ref, b_ref, o_ref, xpad_ref, sh_ref) = refs[:len(xs)], refs[len(xs):]
        u = x_refs[0][...] * jax.nn.sigmoid(x_refs[1][...]) if glu else x_refs[0][...]
        xpad_ref[pl.ds(0, pad), :] = jnp.zeros((pad, cb), F32)
        xpad_ref[pl.ds(pad, t), :] = u
        xpad_ref[pl.ds(pad + t, 40 - pad), :] = jnp.zeros((40 - pad, cb), F32)
        _fill_shifts(xpad_ref, sh_ref, t, shifts)

        def chunk(i, carry):
            t0 = pl.multiple_of(i * ch, ch)
            acc = jnp.broadcast_to(b_ref[...], (ch, cb))
            for k in range(width):
                q, r = divmod(k + off, 8)
                acc = acc + w_ref[pl.ds(k, 1), :] * sh_ref[r, pl.ds(t0 + 8 * q, ch), :]
            o_ref[pl.ds(t0, ch), :] = acc * jax.nn.sigmoid(acc) if silu else acc
            return carry

        lax.fori_loop(0, t // ch, chunk, 0)

    return pl.pallas_call(
        body, name=name, grid=(c // cb,),
        in_specs=[pl.BlockSpec((t, cb), lambda i, o=o: (0, o + i)) for _, o in xs]
        + [pl.BlockSpec((width, cb), lambda i: (0, i)), pl.BlockSpec((1, cb), lambda i: (0, i))],
        out_specs=pl.BlockSpec((t, cb), lambda i: (0, i)), out_shape=SDS((t, c), F32),
        scratch_shapes=[pltpu.VMEM((t + 40, cb), F32), pltpu.VMEM((8, t + 32, cb), F32)],
        compiler_params=_params(("parallel",)),
    )(*[x for x, _ in xs], w, b)


def _dwconv_bwd(xs, w, b, dy, *, width, glu, silu, cb, name):
    t = xs[0][0].shape[0]
    c = w.shape[1]
    pad, off = _conv_geometry(width)
    shifts = sorted({(k + off) % 8 for k in range(width)})
    shifts_t = sorted({mm % 8 for mm in range(width)})
    ch = 32
    nx = len(xs)

    def body(*refs):
        x_refs = refs[:nx]
        w_ref, b_ref, dy_ref = refs[nx:nx + 3]
        dx_refs = refs[nx + 3:nx + 3 + nx]
        dw_ref, db_ref, xpad_ref, sh_ref, dc_ref = refs[nx + 3 + nx:]
        u = x_refs[0][...] * jax.nn.sigmoid(x_refs[1][...]) if glu else x_refs[0][...]
        xpad_ref[pl.ds(0, pad), :] = jnp.zeros((pad, cb), F32)
        xpad_ref[pl.ds(pad, t), :] = u
        xpad_ref[pl.ds(pad + t, 40 - pad), :] = jnp.zeros((40 - pad, cb), F32)
        _fill_shifts(xpad_ref, sh_ref, t, shifts)

        if silu:
            def act_chunk(i, carry):
                t0 = pl.multiple_of(i * ch, ch)
                acc = jnp.broadcast_to(b_ref[...], (ch, cb))
                for k in range(width):
                    q, r = divmod(k + off, 8)
                    acc = acc + w_ref[pl.ds(k, 1), :] * sh_ref[r, pl.ds(t0 + 8 * q, ch), :]
                sg = jax.nn.sigmoid(acc)
                dc_ref[pl.ds(t0, ch), :] = dy_ref[pl.ds(t0, ch), :] * (sg * (1.0 + acc * (1.0 - sg)))
                return carry

            lax.fori_loop(0, t // ch, act_chunk, 0)
        else:
            dc_ref[...] = dy_ref[...]

        def dw_chunk(i, accs):
            t0 = pl.multiple_of(i * 8, 8)
            d = dc_ref[pl.ds(t0, 8), :]
            new = []
            for k in range(width):
                q, r = divmod(k + off, 8)
                new.append(accs[k] + d * sh_ref[r, pl.ds(t0 + 8 * q, 8), :])
            new.append(accs[width] + d)
            return tuple(new)

        accs = lax.fori_loop(0, t // 8, dw_chunk, tuple(jnp.zeros((8, cb), F32) for _ in range(width + 1)))
        for k in range(width):
            dw_ref[pl.ds(k, 1), :] = jnp.sum(accs[k], axis=0, keepdims=True)
        db_ref[...] = jnp.sum(accs[width], axis=0, keepdims=True)

        xpad_ref[pl.ds(0, t), :] = dc_ref[...]
        xpad_ref[pl.ds(t, 40), :] = jnp.zeros((40, cb), F32)
        _fill_shifts(xpad_ref, sh_ref, t, shifts_t)

        def dx_chunk(i, carry):
            t0 = pl.multiple_of(i * ch, ch)
            acc = jnp.zeros((ch, cb), F32)
            for mm in range(width):
                q, r = divmod(mm, 8)
                acc = acc + w_ref[pl.ds(width - 1 - mm, 1), :] * sh_ref[r, pl.ds(t0 + 8 * q, ch), :]
            if glu:
                val, gate = x_refs[0][pl.ds(t0, ch), :], x_refs[1][pl.ds(t0, ch), :]
                sg = jax.nn.sigmoid(gate)
                dx_refs[0][pl.ds(t0, ch), :] = (acc * sg).astype(BF16)
                dx_refs[1][pl.ds(t0, ch), :] = (acc * val * sg * (1.0 - sg)).astype(BF16)
            else:
                dx_refs[0][pl.ds(t0, ch), :] = acc.astype(BF16)
            return carry

        lax.fori_loop(0, t // ch, dx_chunk, 0)

    col = pl.BlockSpec((t, cb), lambda i: (0, i))
    return pl.pallas_call(
        body, name=name, grid=(c // cb,),
        in_specs=[pl.BlockSpec((t, cb), lambda i, o=o: (0, o + i)) for _, o in xs]
        + [pl.BlockSpec((width, cb), lambda i: (0, i)), pl.BlockSpec((1, cb), lambda i: (0, i)), col],
        out_specs=[col] * nx + [pl.BlockSpec((width, cb), lambda i: (0, i)), pl.BlockSpec((1, cb), lambda i: (0, i))],
        out_shape=[SDS((t, c), BF16)] * nx + [SDS((width, c), F32), SDS((1, c), F32)],
        scratch_shapes=[pltpu.VMEM((t + 40, cb), F32), pltpu.VMEM((8, t + 32, cb), F32), pltpu.VMEM((t, cb), F32)],
        compiler_params=_params(("parallel",)),
    )(*[x for x, _ in xs], w, b, dy)


_DIMS = {"nn": (((1,), (0,)), ((), ())), "nt": (((1,), (1,)), ((), ())), "tn": (((0,), (0,)), ((), ()))}


def _raw_dot(a, b, mode):
    return lax.dot_general(a.astype(BF16), b.astype(BF16), _DIMS[mode], preferred_element_type=F32)


@functools.partial(jax.custom_vjp, nondiff_argnums=(2,))
def _bdot(a, b, mode):
    return _raw_dot(a, b, mode)


def _bdot_fwd(a, b, mode):
    return _raw_dot(a, b, mode), (a, b)


def _bdot_bwd(mode, res, g):
    a, b = res
    if mode == "nn":
        return _raw_dot(g, b, "nt"), _raw_dot(a, g, "tn")
    if mode == "nt":
        return _raw_dot(g, b, "nn"), _raw_dot(g, a, "tn")
    return _raw_dot(b, g, "nt"), _raw_dot(a, g, "nn")


_bdot.defvjp(_bdot_fwd, _bdot_bwd)


def _iota(shape, axis):
    return lax.broadcasted_iota(jnp.int32, shape, axis)


def _half_masks():
    left = (_iota((1, LANE), 1) < 64).astype(F32)
    return left, 1.0 - left


def _ssd_chunk(state, xa, dtr, z, dtb, alog, dsk, ng):
    xs, bm, cm = xa[:, :D], xa[:, D:D + 256], xa[:, D + 256:]
    left, right = _half_masks()
    expand = (_iota((LANE, D), 1) // 64 == _iota((LANE, D), 0)).astype(F32)
    li, si = _iota((CHUNK, CHUNK), 0), _iota((CHUNK, CHUNK), 1)
    tril = li >= si
    dt16 = jax.nn.softplus(dtr + dtb)
    adt = dt16 * (-jnp.exp(alog))
    dtf = jnp.dot(dt16, expand, precision=HI)
    cs16 = jnp.dot(tril.astype(F32), adt, precision=HI)
    csf = jnp.dot(cs16, expand, precision=HI)
    totf = jnp.sum(jnp.dot(adt, expand, precision=HI), axis=0, keepdims=True)
    cst = cs16.T
    xdt = xs * dtf
    ys, new_state = [], []
    for g in range(2):
        bg, cg = bm[:, LANE * g:LANE * (g + 1)], cm[:, LANE * g:LANE * (g + 1)]
        cb = _bdot(cg, bg, "nt")
        for q in range(4):
            pr = 4 * g + q
            decay = []
            for h in (2 * pr, 2 * pr + 1):
                col = jnp.sum(jnp.where(si == h, cs16, 0.0), axis=1, keepdims=True)
                row = jnp.sum(jnp.where(li == h, cst, 0.0), axis=0, keepdims=True)
                decay.append(cb * jnp.exp(jnp.where(tril, col - row, -jnp.inf)))
            xp = xdt[:, LANE * pr:LANE * (pr + 1)]
            y_diag = _bdot(jnp.concatenate(decay, axis=1), jnp.concatenate([xp * left, xp * right], axis=0), "nn")
            csb, tot = csf[:, LANE * pr:LANE * (pr + 1)], totf[:, LANE * pr:LANE * (pr + 1)]
            ys.append(y_diag + _bdot(cg, state[pr], "nn") * jnp.exp(csb))
            new_state.append(state[pr] * jnp.exp(tot) + _bdot(bg, xp * jnp.exp(tot - csb), "tn"))
    y = jnp.concatenate(ys, axis=1)
    y = y + jnp.dot(jnp.broadcast_to(dsk, (CHUNK, LANE)), expand, precision=HI) * xs
    y = y * (z * jax.nn.sigmoid(z))
    halves = []
    for g in range(2):
        yg = y[:, 512 * g:512 * (g + 1)]
        halves.append(yg * lax.rsqrt(jnp.mean(yg * yg, axis=-1, keepdims=True) + EPS))
    return jnp.concatenate(halves, axis=1) * ng, jnp.stack(new_state, axis=0)


def _ssd_specs(t, rev):
    nc = t // CHUNK
    ix = (lambda c: nc - 1 - c) if rev else (lambda c: c)
    return nc, ix


def _ssd_fwd(xa, proj, dtb, alog, dsk, ng, name):
    t = xa.shape[0]
    nc, ix = _ssd_specs(t, False)

    def body(xa_ref, dt_ref, z_ref, dtb_ref, alog_ref, dsk_ref, ng_ref, y_ref, st_ref, carry_ref):
        @pl.when(pl.program_id(0) == 0)
        def _():
            carry_ref[...] = jnp.zeros_like(carry_ref)

        st_ref[...] = carry_ref[...]
        y, new = _ssd_chunk(carry_ref[...], xa_ref[...], dt_ref[...], z_ref[...], dtb_ref[...], alog_ref[...],
                            dsk_ref[...], ng_ref[...])
        y_ref[...] = y.astype(BF16)
        carry_ref[...] = new

    small = [dtb, alog, dsk, ng]
    return pl.pallas_call(
        body, name=name, grid=(nc,),
        in_specs=[pl.BlockSpec((CHUNK, SSM_XBC), lambda c: (c, 0)),
                  pl.BlockSpec((CHUNK, LANE), lambda c: (c, DT_COL // LANE)),
                  pl.BlockSpec((CHUNK, D), lambda c: (c, 2))] + [_whole(p) for p in small],
        out_specs=[pl.BlockSpec((CHUNK, D), lambda c: (c, 0)), pl.BlockSpec((None, N_PAIR, LANE, LANE), lambda c: (c, 0, 0, 0))],
        out_shape=[SDS((t, D), BF16), SDS((nc, N_PAIR, LANE, LANE), F32)],
        scratch_shapes=[pltpu.VMEM((N_PAIR, LANE, LANE), F32)],
        compiler_params=_params(("arbitrary",)),
    )(xa, proj, proj, *small)


def _ssd_bwd(xa, proj, states, dy, dtb, alog, dsk, ng, name):
    t = xa.shape[0]
    nc, ix = _ssd_specs(t, True)

    def body(xa_ref, dt_ref, z_ref, st_ref, dy_ref, dtb_ref, alog_ref, dsk_ref, ng_ref,
             dxa_ref, ddt_ref, dz_ref, gdtb_ref, galog_ref, gdsk_ref, gng_ref, carry_ref):
        first = pl.program_id(0) == 0

        @pl.when(first)
        def _():
            carry_ref[...] = jnp.zeros_like(carry_ref)

        args = (st_ref[...], xa_ref[...], dt_ref[...], z_ref[...], dtb_ref[...], alog_ref[...], dsk_ref[...], ng_ref[...])
        _, vjp = jax.vjp(_ssd_chunk, *args)
        ds, dxa, ddt, dz, gdtb, galog, gdsk, gng = vjp((dy_ref[...], carry_ref[...]))
        carry_ref[...] = ds
        dxa_ref[...] = dxa
        ddt_ref[...] = ddt.astype(BF16)
        dz_ref[...] = dz.astype(BF16)
        for r, g in ((gdtb_ref, gdtb), (galog_ref, galog), (gdsk_ref, gdsk), (gng_ref, gng)):
            @pl.when(first)
            def _(r=r, g=g):
                r[...] = g

            @pl.when(jnp.logical_not(first))
            def _(r=r, g=g):
                r[...] += g

    small = [dtb, alog, dsk, ng]
    return pl.pallas_call(
        body, name=name, grid=(nc,),
        in_specs=[pl.BlockSpec((CHUNK, SSM_XBC), lambda c: (ix(c), 0)),
                  pl.BlockSpec((CHUNK, LANE), lambda c: (ix(c), DT_COL // LANE)),
                  pl.BlockSpec((CHUNK, D), lambda c: (ix(c), 2)),
                  pl.BlockSpec((None, N_PAIR, LANE, LANE), lambda c: (ix(c), 0, 0, 0)),
                  pl.BlockSpec((CHUNK, D), lambda c: (ix(c), 0))] + [_whole(p) for p in small],
        out_specs=[pl.BlockSpec((CHUNK, SSM_XBC), lambda c: (ix(c), 0)), pl.BlockSpec((CHUNK, LANE), lambda c: (ix(c), 0)),
                   pl.BlockSpec((CHUNK, D), lambda c: (ix(c), 0))] + [_whole(p) for p in small],
        out_shape=[SDS((t, SSM_XBC), F32), SDS((t, LANE), BF16), SDS((t, D), BF16)] + [SDS(p.shape, F32) for p in small],
        scratch_shapes=[pltpu.VMEM((N_PAIR, LANE, LANE), F32)],
        compiler_params=_params(("arbitrary",)),
    )(xa, proj, proj, states, dy, *small)


def _attn_block(q, kv_prev, kv_cur, cq, sq, ck, sk, sinks, rot, first_block):
    left, right = _half_masks()
    k2 = jnp.concatenate([kv_prev[:, :256], kv_cur[:, :256]], axis=0)
    v2 = jnp.concatenate([kv_prev[:, 256:], kv_cur[:, 256:]], axis=0)
    ri, ci = _iota((LANE, LANE), 0), _iota((LANE, LANE), 1)
    dup = [((ri < 64) & (ci % 64 == ri)).astype(BF16), ((ri >= 64) & (ci % 64 == ri - 64)).astype(BF16)]

    def rope(tt, c, s):
        return tt * c + jnp.dot(tt, rot, precision=HI) * s

    kd, vd = [], []
    for j in range(4):
        sl = slice(LANE * (j // 2), LANE * (j // 2 + 1))
        kd.append(_bdot(rope(k2[:, sl], ck, sk), dup[j % 2], "nn"))
        vd.append(_bdot(v2[:, sl], dup[j % 2], "nn"))
    qi, si = _iota((2 * CHUNK, 2 * CHUNK), 0) % CHUNK, _iota((2 * CHUNK, 2 * CHUNK), 1)
    valid = (si > qi) & (si <= qi + CHUNK) & jnp.logical_or(si >= CHUNK, jnp.logical_not(first_block))
    upper = _iota((2 * CHUNK, 1), 0) < CHUNK
    lanes = _iota((1, LANE), 1)
    outs = []
    for pr in range(N_PAIR):
        qr = rope(q[:, LANE * pr:LANE * (pr + 1)], cq, sq)
        lg = _bdot(jnp.concatenate([qr * left, qr * right], axis=0), kd[pr // 2], "nt") * 0.125
        lg = jnp.where(valid, lg, -jnp.inf)
        s1 = jnp.sum(jnp.where(lanes == 2 * pr, sinks, 0.0), axis=1, keepdims=True)
        s2 = jnp.sum(jnp.where(lanes == 2 * pr + 1, sinks, 0.0), axis=1, keepdims=True)
        sink = jnp.where(upper, s1, s2)
        mx = lax.stop_gradient(jnp.maximum(jnp.max(lg, axis=-1, keepdims=True), sink))
        e = jnp.exp(lg - mx)
        probs = e / (jnp.sum(e, axis=-1, keepdims=True) + jnp.exp(sink - mx))
        o2 = _bdot(probs, vd[pr // 2], "nn")
        outs.append(o2[:CHUNK] * left + o2[CHUNK:] * right)
    return jnp.concatenate(outs, axis=1)


def _attn_fwd(qkv, cos, sin, sinks, rot, name):
    t = qkv.shape[0]
    nb = t // CHUNK

    def body(q_ref, kvp_ref, kvc_ref, cq_ref, sq_ref, cp_ref, sp_ref, sinks_ref, rot_ref, o_ref):
        ck = jnp.concatenate([cp_ref[...], cq_ref[...]], axis=0)
        sk = jnp.concatenate([sp_ref[...], sq_ref[...]], axis=0)
        o_ref[...] = _attn_block(q_ref[...], kvp_ref[...], kvc_ref[...], cq_ref[...], sq_ref[...], ck, sk,
                                 sinks_ref[...], rot_ref[...], pl.program_id(0) == 0).astype(BF16)

    prev = lambda n: jnp.maximum(n - 1, 0)
    return pl.pallas_call(
        body, name=name, grid=(nb,),
        in_specs=[pl.BlockSpec((CHUNK, D), lambda n: (n, 0)),
                  pl.BlockSpec((CHUNK, 512), lambda n: (prev(n), 2)), pl.BlockSpec((CHUNK, 512), lambda n: (n, 2)),
                  pl.BlockSpec((CHUNK, LANE), lambda n: (n, 0)), pl.BlockSpec((CHUNK, LANE), lambda n: (n, 0)),
                  pl.BlockSpec((CHUNK, LANE), lambda n: (prev(n), 0)), pl.BlockSpec((CHUNK, LANE), lambda n: (prev(n), 0)),
                  _whole(sinks), _whole(rot)],
        out_specs=pl.BlockSpec((CHUNK, D), lambda n: (n, 0)), out_shape=SDS((t, D), BF16),
        compiler_params=_params(("parallel",)),
    )(qkv, qkv, qkv, cos, sin, cos, sin, sinks, rot)


def _attn_bwd(qkv, do, cos, sin, sinks, rot, name):
    t = qkv.shape[0]
    nb = t // CHUNK

    def body(q_ref, kvp_ref, kvc_ref, do_ref, cq_ref, sq_ref, cp_ref, sp_ref, sinks_ref, rot_ref,
             dq_ref, dkv_ref, dbq_ref, dbkv_ref, dsink_ref, carry_ref):
        n = pl.program_id(0)

        @pl.when(n == 0)
        def _():
            carry_ref[...] = jnp.zeros_like(carry_ref)
            dbq_ref[...] = jnp.zeros_like(dbq_ref)
            dbkv_ref[...] = jnp.zeros_like(dbkv_ref)
            dsink_ref[...] = jnp.zeros_like(dsink_ref)

        @pl.when(n < nb)
        def _():
            ck = jnp.concatenate([cp_ref[...], cq_ref[...]], axis=0)
            sk = jnp.concatenate([sp_ref[...], sq_ref[...]], axis=0)
            f = lambda q, kvp, kvc, s: _attn_block(q, kvp, kvc, cq_ref[...], sq_ref[...], ck, sk, s, rot_ref[...], n == 0)
            _, vjp = jax.vjp(f, q_ref[...], kvp_ref[...], kvc_ref[...], sinks_ref[...])
            dq, dkvp, dkvc, ds = vjp(do_ref[...].astype(F32))
            done = carry_ref[...] + dkvp
            dq_ref[...] = dq.astype(BF16)
            dkv_ref[...] = done.astype(BF16)
            dbq_ref[...] += jnp.sum(dq, axis=0, keepdims=True)
            dsink_ref[...] += ds
            carry_ref[...] = dkvc

            @pl.when(n > 0)
            def _():
                dbkv_ref[...] += jnp.sum(done, axis=0, keepdims=True)

        @pl.when(n == nb)
        def _():
            done = carry_ref[...]
            dkv_ref[...] = done.astype(BF16)
            dbkv_ref[...] += jnp.sum(done, axis=0, keepdims=True)

    cur = lambda n: jnp.minimum(n, nb - 1)
    prev = lambda n: jnp.maximum(jnp.minimum(n, nb - 1) - 1, 0)
    fin = lambda n: jnp.maximum(n - 1, 0)
    outs = pl.pallas_call(
        body, name=name, grid=(nb + 1,),
        in_specs=[pl.BlockSpec((CHUNK, D), lambda n: (cur(n), 0)),
                  pl.BlockSpec((CHUNK, 512), lambda n: (prev(n), 2)), pl.BlockSpec((CHUNK, 512), lambda n: (cur(n), 2)),
                  pl.BlockSpec((CHUNK, D), lambda n: (cur(n), 0)),
                  pl.BlockSpec((CHUNK, LANE), lambda n: (cur(n), 0)), pl.BlockSpec((CHUNK, LANE), lambda n: (cur(n), 0)),
                  pl.BlockSpec((CHUNK, LANE), lambda n: (prev(n), 0)), pl.BlockSpec((CHUNK, LANE), lambda n: (prev(n), 0)),
                  _whole(sinks), _whole(rot)],
        out_specs=[pl.BlockSpec((CHUNK, D), lambda n: (cur(n), 0)), pl.BlockSpec((CHUNK, 512), lambda n: (fin(n), 0)),
                   pl.BlockSpec((1, D), lambda n: (0, 0)), pl.BlockSpec((1, 512), lambda n: (0, 0)), _whole(sinks)],
        out_shape=[SDS((t, D), BF16), SDS((t, 512), BF16), SDS((1, D), F32), SDS((1, 512), F32), SDS(sinks.shape, F32)],
        scratch_shapes=[pltpu.VMEM((CHUNK, 512), F32)],
        compiler_params=_params(("arbitrary",)),
    )(qkv, qkv, qkv, do, cos, sin, cos, sin, sinks, rot)
    dq, dkv, dbq, dbkv, dsinks = outs
    return jnp.concatenate([dq, dkv], axis=1), jnp.concatenate([dbq, dbkv], axis=1), dsinks


def _loss_head(h, tgt, g, name, tm=512):
    t = h.shape[0]

    def body(h_ref, t_ref, g_ref, loss_ref, dh_ref, dhb_ref, dg_ref):
        def f(hv, gv):
            err = _rms_fn(hv, gv)[0] - t_ref[...]
            return 0.5 * jnp.sum(jnp.mean(err * err, axis=-1, keepdims=True), axis=0, keepdims=True)

        loss, vjp = jax.vjp(f, h_ref[...], g_ref[...])
        dh, dg = vjp(jnp.ones((1, 1), F32))
        dh_ref[...] = dh
        dhb_ref[...] = dh.astype(BF16)
        first = pl.program_id(0) == 0

        @pl.when(first)
        def _():
            loss_ref[...] = loss
            dg_ref[...] = dg

        @pl.when(jnp.logical_not(first))
        def _():
            loss_ref[...] += loss
            dg_ref[...] += dg

    return pl.pallas_call(
        body, name=name, grid=(t // tm,),
        in_specs=[_tok(D, tm), _tok(D, tm), _whole(g)],
        out_specs=[pl.BlockSpec((1, 1), lambda i: (0, 0)), _tok(D, tm), _tok(D, tm), _whole(g)],
        out_shape=[SDS((1, 1), F32), SDS((t, D), F32), SDS((t, D), BF16), SDS(g.shape, F32)],
        compiler_params=_params(("arbitrary",)),
    )(h, tgt, g)


def _res_half(acc, res):
    return (res + 0.5 * acc,)


def _res_full(acc, res):
    return (res + acc,)


def _half(acc):
    return (0.5 * acc,)


def _ffn_in(n, w_in, name, tm=1024):
    t = n.shape[0]
    tm = min(tm, t)

    def body(n_ref, w_ref, pre_ref, act_ref):
        a = n_ref[...]
        gate = jnp.dot(a, w_ref[0], preferred_element_type=F32)
        up = jnp.dot(a, w_ref[1], preferred_element_type=F32)
        pre_ref[0] = gate
        pre_ref[1] = up
        act_ref[...] = (gate * jax.nn.sigmoid(gate) * up).astype(BF16)

    pair = pl.BlockSpec((2, None, tm, FF_SHARD), lambda i, j: (0, j, i, 0))
    return pl.pallas_call(
        body, name=name, grid=(t // tm, 4),
        in_specs=[pl.BlockSpec((tm, D), lambda i, j: (i, 0)), pl.BlockSpec((2, None, D, FF_SHARD), lambda i, j: (0, j, 0, 0))],
        out_specs=[pair, pl.BlockSpec((None, tm, FF_SHARD), lambda i, j: (j, i, 0))],
        out_shape=[SDS((2, 4, t, FF_SHARD), F32), SDS((4, t, FF_SHARD), BF16)],
        compiler_params=_params(("parallel", "parallel")),
    )(n, w_in.reshape(2, 4, D, FF_SHARD))


def _ffn_dact(dhb, w_out, pre, name, tm=1024):
    t = dhb.shape[0]
    tm = min(tm, t)

    def body(d_ref, w_ref, pre_ref, o_ref):
        dact = 0.5 * lax.dot_general(d_ref[...], w_ref[...], _DIMS["nt"], preferred_element_type=F32)
        gate, up = pre_ref[0], pre_ref[1]
        sg = jax.nn.sigmoid(gate)
        o_ref[0] = (dact * up * (sg * (1.0 + gate * (1.0 - sg)))).astype(BF16)
        o_ref[1] = (dact * (gate * sg)).astype(BF16)

    pair = pl.BlockSpec((2, None, tm, FF_SHARD), lambda i, j: (0, j, i, 0))
    return pl.pallas_call(
        body, name=name, grid=(t // tm, 4),
        in_specs=[pl.BlockSpec((tm, D), lambda i, j: (i, 0)), pl.BlockSpec((None, FF_SHARD, D), lambda i, j: (j, 0, 0)), pair],
        out_specs=pair, out_shape=SDS((2, 4, t, FF_SHARD), BF16),
        compiler_params=_params(("parallel", "parallel")),
    )(dhb, w_out, pre)


def _ffn_fwd(h, g, w_in, w_out, tag):
    n = _rms(h, g, f"{tag}_rms")
    pre, act = _ffn_in(n, w_in, f"{tag}_in")
    out = _mm(act, w_out, reduce_j=True, tk=FF_SHARD, epi=_res_half, extras=(h[None],), name=f"{tag}_out")[0][0]
    return out, (h, n, pre, act)


def _ffn_bwd(dh, dhb, saved, g, w_in, w_out, tag):
    h, n, pre, act = saved
    t = h.shape[0]
    dpre = _ffn_dact(dhb, w_out, pre, f"{tag}_dact").reshape(N_DEV, t, FF_SHARD)
    dw_out = _mm(act, dhb[None], ta=True, tm=FF_SHARD, epi=_half, out_dtypes=(BF16,), name=f"{tag}_dwout")[0]
    dn = _mm(dpre, w_in, tb=True, reduce_j=True, tk=FF_SHARD, name=f"{tag}_dn")[0][0]
    dw_in = _mm(n[None], dpre, ta=True, tn=FF_SHARD, out_dtypes=(BF16,), name=f"{tag}_dwin")[0]
    dh_in, dhb_in, dg = _rms_bwd(h, g, dn, dh, f"{tag}_drms")
    return dh_in, dhb_in, dg, dw_in, dw_out


def _ple_fwd(h, g, pb, w_gate, w_proj, tag):
    t = h.shape[0]
    tm = 512
    n = _rms(h, g, f"{tag}_rms")
    e = _mm(pb[None], w_proj[None], name=f"{tag}_proj")[0][0]
    z = _mm(n[None], w_gate[None], name=f"{tag}_gate")[0][0]
    out = _rowop(lambda zz, ee, hh: (hh + _ple_fn(zz, ee)[0],), [(z, _tok(D, tm)), (e, _tok(D, tm)), (h, _tok(D, tm))], [],
                 [((t, D), F32, _tok(D, tm))], grid=(t // tm,), name=f"{tag}_mix")[0]
    return out, (h, n, e, z)


def _ple_bwd(dh, dhb, saved, g, pb, w_gate, tag):
    h, n, e, z = saved
    t = h.shape[0]
    tm = 512
    dz, de = _rowop_bwd(_ple_fn, [(z, _tok(D, tm)), (e, _tok(D, tm))], [], [(dh, _tok(D, tm))], [(0,), (1,)],
                        [((t, D), (BF16,), _tok(D, tm)), ((t, D), (BF16,), _tok(D, tm))], grid=(t // tm,), name=f"{tag}_dmix")
    dw_proj = _mm(pb[None], de[None], ta=True, out_dtypes=(BF16,), name=f"{tag}_dwproj")[0][0]
    dw_gate = _mm(n[None], dz[None], ta=True, out_dtypes=(BF16,), name=f"{tag}_dwgate")[0][0]
    dn = _mm(dz[None], w_gate[None], tb=True, name=f"{tag}_dn")[0][0]
    dh_in, dhb_in, dg = _rms_bwd(h, g, dn, dh, f"{tag}_drms")
    return dh_in, dhb_in, dg, dw_gate, dw_proj


def _hyb_fwd(h, w, tag):
    t = h.shape[0]
    tm = 512
    hn = _rms(h, w["norm_mix"], f"{tag}_rms")
    proj = _mm(hn[None], w["hyb_in"][None], tn=512, name=f"{tag}_in")[0][0]
    u1 = _dwconv([(proj, 0), (proj, D // LANE)], w["conv_w"], w["conv_b"], width=CONV_W, glu=True, silu=False, cb=LANE,
                 name=f"{tag}_conv")
    u = _rowop(_lnswish_fn, [(u1, _tok(D, tm))], [w["ln_g"], w["ln_b"]], [((t, D), BF16, _tok(D, tm))], grid=(t // tm,),
               name=f"{tag}_ln")[0]
    xa = _dwconv([(proj, 3 * D // LANE)], w["sconv_w"], w["sconv_b"], width=SSM_CONV, glu=False, silu=True, cb=LANE,
                 name=f"{tag}_sconv")
    y, states = _ssd_fwd(xa, proj, w["dt_bias"], w["a_log"], w["d_skip"], w["ssm_norm"], f"{tag}_ssd")
    mixed = jnp.stack([u, y], axis=0)
    out = _mm(mixed, w["hyb_out"], reduce_j=True, epi=_res_full, extras=(h[None],), name=f"{tag}_out")[0][0]
    return out, (h, hn, proj, u1, xa, states, mixed)


def _hyb_bwd(dh, dhb, saved, w, tag):
    h, hn, proj, u1, xa, states, mixed = saved
    t = h.shape[0]
    tm = 512
    dmix = _mm(dhb[None], w["hyb_out"], tb=True, name=f"{tag}_dmix")[0]
    dw_out = _mm(mixed, dhb[None], ta=True, out_dtypes=(BF16,), name=f"{tag}_dwout")[0]
    du1, dln_g, dln_b = _rowop_bwd(_lnswish_fn, [(u1, _tok(D, tm))], [w["ln_g"], w["ln_b"]], [(dmix[0], _tok(D, tm))], [(0,)],
                                   [((t, D), (F32,), _tok(D, tm))], grid=(t // tm,), name=f"{tag}_dln")
    dval, dgate, dconv_w, dconv_b = _dwconv_bwd([(proj, 0), (proj, D // LANE)], w["conv_w"], w["conv_b"], du1,
                                                width=CONV_W, glu=True, silu=False, cb=LANE, name=f"{tag}_dconv")
    dxa, ddt, dz, g_dtb, g_alog, g_dsk, g_ng = _ssd_bwd(xa, proj, states, dmix[1], w["dt_bias"], w["a_log"], w["d_skip"],
                                                         w["ssm_norm"], f"{tag}_dssd")
    dxbc, dsconv_w, dsconv_b = _dwconv_bwd([(proj, 3 * D // LANE)], w["sconv_w"], w["sconv_b"], dxa, width=SSM_CONV,
                                           glu=False, silu=True, cb=LANE, name=f"{tag}_dsconv")
    dproj = jnp.concatenate([dval, dgate, dz, dxbc, ddt, jnp.zeros((t, HYB_PAD - DT_COL - LANE), BF16)], axis=1)
    dhn = _mm(dproj[None], w["hyb_in"][None], tb=True, name=f"{tag}_dhn")[0][0]
    dw_in = _mm(hn[None], dproj[None], ta=True, tn=512, out_dtypes=(BF16,), name=f"{tag}_dwin")[0][0]
    dh_in, dhb_in, dg = _rms_bwd(h, w["norm_mix"], dhn, dh, f"{tag}_drms")
    grads = dict(norm_mix=dg, hyb_in=dw_in, hyb_out=dw_out, conv_w=dconv_w, conv_b=dconv_b, ln_g=dln_g, ln_b=dln_b,
                 sconv_w=dsconv_w, sconv_b=dsconv_b, dt_bias=g_dtb, a_log=g_alog, d_skip=g_dsk, ssm_norm=g_ng)
    return dh_in, dhb_in, grads


def _bias_epi(acc, row):
    return (acc + row,)


def _res_bias_epi(acc, res, row):
    return (res + acc + row,)


def _att_fwd(h, w, tables, tag):
    cos, sin, rot = tables
    hn = _rms(h, w["norm_mix"], f"{tag}_rms")
    qkv = _mm(hn[None], w["qkv"][None], tn=512, epi=_bias_epi, rows=(w["b_qkv"],), name=f"{tag}_qkv")[0][0]
    o = _attn_fwd(qkv, cos, sin, w["sinks"], rot, f"{tag}_core")
    out = _mm(o[None], w["w_o"][None], epi=_res_bias_epi, extras=(h[None],), rows=(w["b_o"],), name=f"{tag}_out")[0][0]
    return out, (h, hn, qkv, o)


def _att_bwd(dh, dhb, saved, w, tables, tag):
    cos, sin, rot = tables
    h, hn, qkv, o = saved
    t = h.shape[0]
    tm = 512
    do = _mm(dhb[None], w["w_o"][None], tb=True, out_dtypes=(BF16,), name=f"{tag}_do")[0][0]
    dw_o = _mm(o[None], dhb[None], ta=True, out_dtypes=(BF16,), name=f"{tag}_dwo")[0][0]
    db_o = _rowop_bwd(lambda xx, bb: (xx + bb,), [(dh, _tok(D, tm))], [w["b_o"]], [(dh, _tok(D, tm))], [], [],
                      grid=(t // tm,), name=f"{tag}_dbo")[0]
    dqkv, db_qkv, dsinks = _attn_bwd(qkv, do, cos, sin, w["sinks"], rot, f"{tag}_dcore")
    dhn = _mm(dqkv[None], w["qkv"][None], tb=True, tk=512, name=f"{tag}_dhn")[0][0]
    dw_qkv = _mm(hn[None], dqkv[None], ta=True, tn=512, out_dtypes=(BF16,), name=f"{tag}_dwqkv")[0][0]
    dh_in, dhb_in, dg = _rms_bwd(h, w["norm_mix"], dhn, dh, f"{tag}_drms")
    grads = dict(norm_mix=dg, qkv=dw_qkv, b_qkv=db_qkv, sinks=dsinks, w_o=dw_o, b_o=db_o)
    return dh_in, dhb_in, grads


def _rope_tables(t):
    inv = ROPE_THETA ** (-jnp.arange(0, 64, 2, dtype=F32) / 64)
    ang = jnp.arange(t, dtype=F32)[:, None] * inv[None, :]
    cos, sin = jnp.tile(jnp.cos(ang), (1, 4)), jnp.tile(jnp.sin(ang), (1, 4))
    rot = np.zeros((LANE, LANE), np.float32)
    for j in range(LANE):
        if j % 64 < 32:
            rot[j + 32, j] = -1.0
        else:
            rot[j - 32, j] = 1.0
    return cos, sin, jnp.asarray(rot)


def _local_step(x, p, tgt, layers, final_norm):
    t = x.shape[0]
    tables = _rope_tables(t)
    pb = p.astype(BF16)
    h, saved = x, []
    for i, w in enumerate(layers):
        s = {}
        h, s["ffn1"] = _ffn_fwd(h, w["norm_ffn1"], w["ffn1_in"], w["ffn1_out"], f"l{i}_ffn1")
        if i % 2 == 0:
            h, s["mix"] = _hyb_fwd(h, w, f"l{i}_hyb")
        else:
            h, s["mix"] = _att_fwd(h, w, tables, f"l{i}_att")
        h, s["ffn2"] = _ffn_fwd(h, w["norm_ffn2"], w["ffn2_in"], w["ffn2_out"], f"l{i}_ffn2")
        h, s["ple"] = _ple_fwd(h, w["ple_norm"], pb[i], w["ple_gate"], w["ple_proj"], f"l{i}_ple")
        saved.append(s)
    loss, dh, dhb, d_final = _loss_head(h, tgt, final_norm, "loss_head")
    grads = [None] * len(layers)
    for i in reversed(range(len(layers))):
        w, s, g = layers[i], saved[i], {}
        dh, dhb, g["ple_norm"], g["ple_gate"], g["ple_proj"] = _ple_bwd(dh, dhb, s["ple"], w["ple_norm"], pb[i], w["ple_gate"],
                                                                        f"l{i}_ple")
        dh, dhb, g["norm_ffn2"], g["ffn2_in"], g["ffn2_out"] = _ffn_bwd(dh, dhb, s["ffn2"], w["norm_ffn2"], w["ffn2_in"],
                                                                        w["ffn2_out"], f"l{i}_ffn2")
        if i % 2 == 0:
            dh, dhb, gm = _hyb_bwd(dh, dhb, s["mix"], w, f"l{i}_hyb")
        else:
            dh, dhb, gm = _att_bwd(dh, dhb, s["mix"], w, tables, f"l{i}_att")
        g.update(gm)
        dh, dhb, g["norm_ffn1"], g["ffn1_in"], g["ffn1_out"] = _ffn_bwd(dh, dhb, s["ffn1"], w["norm_ffn1"], w["ffn1_in"],
                                                                        w["ffn1_out"], f"l{i}_ffn1")
        grads[i] = g
    return loss[0, 0], dh, grads, d_final


def _cols(g):
    full = jnp.moveaxis(g, 0, -2)
    return full.reshape(*full.shape[:-2], N_DEV * g.shape[-1])


def _uncols(full):
    split = full.reshape(*full.shape[:-1], N_DEV, full.shape[-1] // N_DEV)
    return jnp.moveaxis(split, -2, 0)


def _lane_pad(v):
    return jnp.pad(v, ((0, 0), (0, LANE - v.shape[1])))


def _build_layers(gw, gs, rep):
    layers = []
    for i in range(2):
        w = {}
        for f in ("ffn1", "ffn2"):
            w[f"norm_{f}"] = rep[f"norm_{f}"][i][None]
            w[f"{f}_in"] = gw[f"{f}_w_in", i]
            w[f"{f}_out"] = gw[f"{f}_w_out", i].reshape(4, FF_SHARD, D)
        w["norm_mix"] = rep["norm_mix"][i][None]
        w["ple_norm"] = rep["ple_norm"][i][None]
        w["ple_gate"] = gw["ple_gate_w", i].reshape(D, D)
        w["ple_proj"] = _cols(gw["ple_proj_w", i])
        if i == 0:
            w["hyb_in"] = jnp.pad(_cols(gw["hyb_w_in", 0]), ((0, 0), (0, HYB_PAD - HYB_IN)))
            w["hyb_out"] = gw["hyb_w_out", 0].reshape(2, D, D)
            w["conv_w"] = _cols(gs["conv_dw_w"][:, 0])
            w["sconv_w"] = _cols(gs["ssm_conv_w"][:, 0])
            w["conv_b"], w["ln_g"], w["ln_b"] = rep["conv_dw_b"], rep["conv_ln_g"], rep["conv_ln_b"]
            w["sconv_b"], w["ssm_norm"] = rep["ssm_conv_b"], rep["ssm_norm"]
            w["dt_bias"], w["a_log"], w["d_skip"] = (_lane_pad(rep[k]) for k in ("ssm_dt_bias", "ssm_a_log", "ssm_d"))
        else:
            w["qkv"] = _cols(gw["att_w_qkv", 0])
            w["w_o"] = gw["att_w_o", 0].reshape(D, D)
            w["b_qkv"] = gs["att_b_qkv"][:, 0].reshape(1, -1)
            w["b_o"] = gs["att_b_o"][:, 0].reshape(1, -1)
            w["sinks"] = _lane_pad(rep["att_sinks"])
        layers.append(w)
    return layers


def _collect_grads(grads, d_final):
    g0, g1 = grads
    big, small = {}, {}
    for i, g in enumerate(grads):
        for f in ("ffn1", "ffn2"):
            big[f"{f}_w_in", i] = g[f"{f}_in"]
            big[f"{f}_w_out", i] = g[f"{f}_out"].reshape(N_DEV, D_FF // N_DEV, D)
        big["ple_gate_w", i] = g["ple_gate"].reshape(N_DEV, D // N_DEV, D)
        big["ple_proj_w", i] = _uncols(g["ple_proj"])
    for f in ("ffn1", "ffn2"):
        small[f"norm_{f}"] = jnp.concatenate([g[f"norm_{f}"] for g in grads], axis=0)
    big["hyb_w_in", 0] = _uncols(g0["hyb_in"][:, :HYB_IN])
    big["hyb_w_out", 0] = g0["hyb_out"].reshape(N_DEV, 2 * D // N_DEV, D)
    big["att_w_qkv", 0] = _uncols(g1["qkv"])
    big["att_w_o", 0] = g1["w_o"].reshape(N_DEV, D // N_DEV, D)
    small["norm_mix"] = jnp.concatenate([g["norm_mix"] for g in grads], axis=0)
    small["ple_norm"] = jnp.concatenate([g["ple_norm"] for g in grads], axis=0)
    small["conv_dw_w"] = g0["conv_w"][None]
    small["conv_dw_b"], small["conv_ln_g"], small["conv_ln_b"] = g0["conv_b"], g0["ln_g"], g0["ln_b"]
    small["ssm_conv_w"] = g0["sconv_w"][None]
    small["ssm_conv_b"], small["ssm_norm"] = g0["sconv_b"], g0["ssm_norm"]
    small["ssm_dt_bias"], small["ssm_a_log"], small["ssm_d"] = (g0[k][:, :SSM_HEADS] for k in ("dt_bias", "a_log", "d_skip"))
    small["att_b_qkv"], small["att_b_o"] = g1["b_qkv"], g1["b_o"]
    small["att_sinks"] = g1["sinks"][:, :SSM_HEADS]
    small["final_norm"] = d_final[0]
    return big, small


MESH = pl.DeviceIdType.MESH


def _place():
    return lax.axis_index("x"), lax.axis_index("y"), lax.axis_index("c")


def _all_gather(blocks, space, name):
    nb = len(blocks)

    def body(*refs):
        x_refs, out_refs, (send_sems, recv_sems, local_sem) = refs[:nb], refs[nb:2 * nb], refs[2 * nb:]
        x, y, c = _place()
        me, sibling = (x, y, c), (x, y, 1 - c)
        chips = [(1 - x, y), (x, 1 - y), (1 - x, 1 - y)]

        def copies(k, blk, to, own=False):
            idx = 4 * blk[0] + 2 * blk[1] + blk[2]
            return [pltpu.make_async_remote_copy(src_ref=x_ref if own else out_ref.at[idx], dst_ref=out_ref.at[idx],
                                                 send_sem=send_sems.at[k, b], recv_sem=recv_sems.at[k, b], device_id=to,
                                                 device_id_type=MESH) for b, (x_ref, out_ref) in enumerate(zip(x_refs, out_refs))]

        mine = [pltpu.make_async_copy(x_ref, out_ref.at[4 * x + 2 * y + c], local_sem.at[b])
                for b, (x_ref, out_ref) in enumerate(zip(x_refs, out_refs))]
        first = copies(0, me, sibling, own=True)
        for j, chip in enumerate(chips):
            first += copies(1 + j, me, (*chip, c), own=True)
        for cp in mine + first:
            cp.start()
        passed = []
        for j, chip in enumerate(chips):
            for cp in copies(1 + j, (*chip, c), me):
                cp.wait_recv()
            onward = copies(4 + j, (*chip, c), sibling)
            for cp in onward:
                cp.start()
            passed += onward
        for cp in copies(0, sibling, me):
            cp.wait_recv()
        for j, chip in enumerate(chips):
            for cp in copies(4 + j, (*chip, 1 - c), me):
                cp.wait_recv()
        for cp in first + passed:
            cp.wait_send()
        for cp in mine:
            cp.wait()

    spec = pl.BlockSpec(memory_space=space)
    return pl.pallas_call(
        body, name=name, out_shape=[SDS((N_DEV,) + b.shape, b.dtype) for b in blocks],
        in_specs=[spec] * nb, out_specs=[spec] * nb,
        scratch_shapes=[pltpu.SemaphoreType.DMA((7, nb)), pltpu.SemaphoreType.DMA((7, nb)), pltpu.SemaphoreType.DMA((nb,))],
    )(*blocks)


def _pair_exchange(parts, name):
    nb = len(parts)

    def body(*refs):
        p_refs, got_refs, (send_sems, recv_sems) = refs[:nb], refs[nb:2 * nb], refs[2 * nb:]
        x, y, c = _place()
        copies = [pltpu.make_async_remote_copy(src_ref=p_ref.at[2 * q + (1 - c)], dst_ref=got_ref.at[q],
                                               send_sem=send_sems.at[q, b], recv_sem=recv_sems.at[q, b], device_id=(x, y, 1 - c),
                                               device_id_type=MESH)
                  for q in range(4) for b, (p_ref, got_ref) in enumerate(zip(p_refs, got_refs))]
        for cp in copies:
            cp.start()
        for cp in copies:
            cp.wait_recv()
        for cp in copies:
            cp.wait_send()

    hbm = pl.BlockSpec(memory_space=pltpu.HBM)
    return pl.pallas_call(
        body, name=name, out_shape=[SDS((4,) + p.shape[1:], p.dtype) for p in parts], in_specs=[hbm] * nb, out_specs=[hbm] * nb,
        scratch_shapes=[pltpu.SemaphoreType.DMA((4, nb)), pltpu.SemaphoreType.DMA((4, nb))],
    )(*parts)


def _row_tile(r, cap=4608):
    return max(d for d in range(16, min(r, cap) + 1, 16) if r % d == 0)


def _pair_add(parts, got, core, name):
    _, r, cdim = parts.shape
    tr = _row_tile(r)

    def body(core_ref, p_ref, g_ref, o_ref):
        o_ref[...] = (p_ref[...].astype(F32) + g_ref[...].astype(F32)).astype(o_ref.dtype)

    return pl.pallas_call(
        body, name=name, out_shape=SDS((4, r, cdim), BF16),
        grid_spec=pltpu.PrefetchScalarGridSpec(
            num_scalar_prefetch=1, grid=(4, r // tr),
            in_specs=[pl.BlockSpec((None, tr, cdim), lambda q, i, core_ref: (2 * q + core_ref[0], i, 0)),
                      pl.BlockSpec((None, tr, cdim), lambda q, i, core_ref: (q, i, 0))],
            out_specs=pl.BlockSpec((None, tr, cdim), lambda q, i, core_ref: (q, i, 0))),
        compiler_params=_params(("parallel", "parallel")),
    )(core, parts, got)


def _chip_exchange(sums, name):
    nb = len(sums)

    def body(*refs):
        b_refs, out_refs, (send_sems, recv_sems, local_sem) = refs[:nb], refs[nb:2 * nb], refs[2 * nb:]
        x, y, c = _place()
        own = 2 * x + y
        chips = [(1 - x, y), (x, 1 - y), (1 - x, 1 - y)]

        def copies(k, chip, src_slot, dst_slot):
            return [pltpu.make_async_remote_copy(src_ref=b_ref.at[src_slot], dst_ref=out_ref.at[dst_slot],
                                                 send_sem=send_sems.at[k, b], recv_sem=recv_sems.at[k, b], device_id=(*chip, c),
                                                 device_id_type=MESH) for b, (b_ref, out_ref) in enumerate(zip(b_refs, out_refs))]

        mine = [pltpu.make_async_copy(b_ref.at[own], out_ref.at[own], local_sem.at[b])
                for b, (b_ref, out_ref) in enumerate(zip(b_refs, out_refs))]
        sends = []
        for k, chip in enumerate(chips):
            sends += copies(k, chip, 2 * chip[0] + chip[1], own)
        for cp in mine + sends:
            cp.start()
        for k, chip in enumerate(chips):
            for cp in copies(k, chip, own, 2 * chip[0] + chip[1]):
                cp.wait_recv()
        for cp in sends:
            cp.wait_send()
        for cp in mine:
            cp.wait()

    hbm = pl.BlockSpec(memory_space=pltpu.HBM)
    return pl.pallas_call(
        body, name=name, out_shape=[SDS(s.shape, s.dtype) for s in sums], in_specs=[hbm] * nb, out_specs=[hbm] * nb,
        scratch_shapes=[pltpu.SemaphoreType.DMA((3, nb)), pltpu.SemaphoreType.DMA((3, nb)), pltpu.SemaphoreType.DMA((nb,))],
    )(*sums)


def _sum_slots(parts, name):
    nj, r, cdim = parts.shape
    tr = _row_tile(r)

    def body(p_ref, o_ref):
        acc = p_ref[0].astype(F32)
        for j in range(1, nj):
            acc = acc + p_ref[j].astype(F32)
        o_ref[...] = acc

    return pl.pallas_call(
        body, name=name, out_shape=SDS((r, cdim), F32), grid=(r // tr,),
        in_specs=[pl.BlockSpec((nj, tr, cdim), lambda i: (0, i, 0))], out_specs=pl.BlockSpec((tr, cdim), lambda i: (i, 0)),
        compiler_params=_params(("parallel",)),
    )(parts)


def _adamw_update(wv, gv, mv, vv):
    nm = ADAM_B1 * mv + (1.0 - ADAM_B1) * gv
    nv = ADAM_B2 * vv + (1.0 - ADAM_B2) * (gv * gv)
    m_hat = nm / (1.0 - ADAM_B1 ** ADAM_STEP)
    v_hat = nv / (1.0 - ADAM_B2 ** ADAM_STEP)
    return -ADAM_LR * (m_hat / (jnp.sqrt(v_hat) + ADAM_EPS) + ADAM_WD * wv), nm, nv


def _adamw_summed(w, m, v, by_chip, name):
    nl, r, cdim = w.shape
    tr = _row_tile(r, 512)
    nblk = r // tr

    def body(*refs):
        chip_refs, (w_ref, m_ref, v_ref, g_ref, d_ref, nm_ref, nv_ref) = refs[:nl], refs[nl:]
        layer = pl.program_id(0)
        gv = None
        for ll, c_ref in enumerate(chip_refs):
            s = c_ref[0].astype(F32)
            for q in range(1, 4):
                s = s + c_ref[q].astype(F32)
            gv = s if gv is None else jnp.where(layer == ll, s, gv)
        g_ref[...] = gv
        d_ref[...], nm_ref[...], nv_ref[...] = _adamw_update(w_ref[...], gv, m_ref[...], v_ref[...])

    def chip_map(ll):
        return lambda l, i: (0, jnp.where(l == ll, i, jnp.where(l > ll, nblk - 1, 0)), 0)

    spec = pl.BlockSpec((None, tr, cdim), lambda l, i: (l, i, 0))
    return pl.pallas_call(
        body, name=name, grid=(nl, nblk),
        in_specs=[pl.BlockSpec((4, tr, cdim), chip_map(ll)) for ll in range(nl)] + [spec] * 3,
        out_specs=[spec] * 4, out_shape=[SDS((nl, r, cdim), F32)] * 4,
        compiler_params=_params(("arbitrary", "arbitrary")),
    )(*by_chip, w, m, v)


def _adamw(w, g, m, v, name):
    shape = w.shape
    cdim = shape[-1]
    w2, g2, m2, v2 = (a.reshape(-1, cdim) for a in (w, g, m, v))
    r = w2.shape[0]
    tr = next(d for d in (512, 352, 256, 128, 64, 32, 16, 8, r) if r % d == 0)

    def body(w_ref, g_ref, m_ref, v_ref, d_ref, nm_ref, nv_ref):
        d_ref[...], nm_ref[...], nv_ref[...] = _adamw_update(w_ref[...], g_ref[...], m_ref[...], v_ref[...])

    spec = pl.BlockSpec((tr, cdim), lambda i: (i, 0))
    outs = pl.pallas_call(
        body, name=name, grid=(r // tr,), in_specs=[spec] * 4, out_specs=[spec] * 3, out_shape=[SDS((r, cdim), F32)] * 3,
        compiler_params=_params(("parallel",)),
    )(w2, g2, m2, v2)
    return tuple(o.reshape(shape) for o in outs)


WEIGHTS = ("norm_ffn1", "ffn1_w_in", "ffn1_w_out", "norm_mix", "norm_ffn2", "ffn2_w_in", "ffn2_w_out", "ple_norm", "ple_gate_w",
           "ple_proj_w", "hyb_w_in", "conv_dw_w", "conv_dw_b", "conv_ln_g", "conv_ln_b", "ssm_conv_w", "ssm_conv_b", "ssm_dt_bias",
           "ssm_a_log", "ssm_d", "ssm_norm", "hyb_w_out", "att_w_qkv", "att_b_qkv", "att_sinks", "att_w_o", "att_b_o", "final_norm")
BIG = ("ffn1_w_in", "ffn1_w_out", "ffn2_w_in", "ffn2_w_out", "ple_gate_w", "ple_proj_w", "hyb_w_in", "hyb_w_out", "att_w_qkv",
       "att_w_o")
SMALL_SHARDED = {"conv_dw_w": 2, "ssm_conv_w": 2, "att_b_qkv": 1, "att_b_o": 1}
SMALL = tuple(n for n in WEIGHTS if n not in BIG)
PACK_ROWS = 16


def _pack(arrays, lead=0):
    pieces = []
    for a in arrays:
        flat = a.reshape(*a.shape[:lead], -1)
        size = flat.shape[-1]
        padded = -(-size // (PACK_ROWS * LANE)) * PACK_ROWS * LANE
        flat = jnp.pad(flat, [(0, 0)] * lead + [(0, padded - size)])
        pieces.append(flat.reshape(*a.shape[:lead], padded // LANE, LANE))
    return jnp.concatenate(pieces, axis=lead)


def _unpack(buf, shapes, lead=0):
    out, row = [], 0
    for shape in shapes:
        size = math.prod(shape)
        rows = -(-size // (PACK_ROWS * LANE)) * PACK_ROWS
        piece = lax.slice_in_dim(buf, row, row + rows, axis=lead)
        piece = piece.reshape(*buf.shape[:lead], rows * LANE)
        out.append(lax.slice_in_dim(piece, 0, size, axis=lead).reshape(*buf.shape[:lead], *shape))
        row += rows
    return out


def kernel(x, p, norm_ffn1, ffn1_w_in, ffn1_w_out, norm_mix, norm_ffn2, ffn2_w_in, ffn2_w_out, ple_norm, ple_gate_w, ple_proj_w, hyb_w_in, conv_dw_w, conv_dw_b, conv_ln_g, conv_ln_b, ssm_conv_w, ssm_conv_b, ssm_dt_bias, ssm_a_log, ssm_d, ssm_norm, hyb_w_out, att_w_qkv, att_b_qkv, att_sinks, att_w_o, att_b_o, final_norm, loss_target, m_norm_ffn1, m_ffn1_w_in, m_ffn1_w_out, m_norm_mix, m_norm_ffn2, m_ffn2_w_in, m_ffn2_w_out, m_ple_norm, m_ple_gate_w, m_ple_proj_w, m_hyb_w_in, m_conv_dw_w, m_conv_dw_b, m_conv_ln_g, m_conv_ln_b, m_ssm_conv_w, m_ssm_conv_b, m_ssm_dt_bias, m_ssm_a_log, m_ssm_d, m_ssm_norm, m_hyb_w_out, m_att_w_qkv, m_att_b_qkv, m_att_sinks, m_att_w_o, m_att_b_o, m_final_norm, v_norm_ffn1, v_ffn1_w_in, v_ffn1_w_out, v_norm_mix, v_norm_ffn2, v_ffn2_w_in, v_ffn2_w_out, v_ple_norm, v_ple_gate_w, v_ple_proj_w, v_hyb_w_in, v_conv_dw_w, v_conv_dw_b, v_conv_ln_g, v_conv_ln_b, v_ssm_conv_w, v_ssm_conv_b, v_ssm_dt_bias, v_ssm_a_log, v_ssm_d, v_ssm_norm, v_hyb_w_out, v_att_w_qkv, v_att_b_qkv, v_att_sinks, v_att_w_o, v_att_b_o, v_final_norm):
    args = (norm_ffn1, ffn1_w_in, ffn1_w_out, norm_mix, norm_ffn2, ffn2_w_in, ffn2_w_out, ple_norm, ple_gate_w, ple_proj_w, hyb_w_in, conv_dw_w, conv_dw_b, conv_ln_g, conv_ln_b, ssm_conv_w, ssm_conv_b, ssm_dt_bias, ssm_a_log, ssm_d, ssm_norm, hyb_w_out, att_w_qkv, att_b_qkv, att_sinks, att_w_o, att_b_o, final_norm)
    moments_m = (m_norm_ffn1, m_ffn1_w_in, m_ffn1_w_out, m_norm_mix, m_norm_ffn2, m_ffn2_w_in, m_ffn2_w_out, m_ple_norm, m_ple_gate_w, m_ple_proj_w, m_hyb_w_in, m_conv_dw_w, m_conv_dw_b, m_conv_ln_g, m_conv_ln_b, m_ssm_conv_w, m_ssm_conv_b, m_ssm_dt_bias, m_ssm_a_log, m_ssm_d, m_ssm_norm, m_hyb_w_out, m_att_w_qkv, m_att_b_qkv, m_att_sinks, m_att_w_o, m_att_b_o, m_final_norm)
    moments_v = (v_norm_ffn1, v_ffn1_w_in, v_ffn1_w_out, v_norm_mix, v_norm_ffn2, v_ffn2_w_in, v_ffn2_w_out, v_ple_norm, v_ple_gate_w, v_ple_proj_w, v_hyb_w_in, v_conv_dw_w, v_conv_dw_b, v_conv_ln_g, v_conv_ln_b, v_ssm_conv_w, v_ssm_conv_b, v_ssm_dt_bias, v_ssm_a_log, v_ssm_d, v_ssm_norm, v_hyb_w_out, v_att_w_qkv, v_att_b_qkv, v_att_sinks, v_att_w_o, v_att_b_o, v_final_norm)
    w = dict(zip(WEIGHTS, args))
    m = dict(zip(WEIGHTS, moments_m))
    v = dict(zip(WEIGHTS, moments_v))
    cx, cy, cc = _place()
    me = 4 * cx + 2 * cy + cc

    keys = [(n, i) for n in BIG for i in range(w[n].shape[0])]
    gathered = _all_gather([w[n][i].astype(BF16) for n, i in keys], pltpu.HBM, "gather_weights")
    gw = dict(zip(keys, gathered))
    ss_shapes = [w[n].shape for n in SMALL_SHARDED]
    gathered_small = _all_gather([_pack([w[n] for n in SMALL_SHARDED])], pltpu.VMEM, "gather_small_weights")[0]
    gs = dict(zip(SMALL_SHARDED, _unpack(gathered_small, ss_shapes, lead=1)))
    rep = {n: w[n] for n in SMALL if n not in SMALL_SHARDED}

    layers = _build_layers(gw, gs, rep)
    loss, dx, grads, d_final = _local_step(x[0], p[:, 0], loss_target[0], layers, final_norm[None])
    big, small = _collect_grads(grads, d_final)
    loss = lax.psum(loss, ("x", "y", "c"))

    parts = [big[key] for key in keys]
    got = _pair_exchange(parts, "grads_pair_exchange")
    core = jnp.reshape(cc, (1,)).astype(jnp.int32)
    sums = [_pair_add(pt, gt, core, f"grads_pair_add_{n}_{i}") for pt, gt, (n, i) in zip(parts, got, keys)]
    by_chip = dict(zip(keys, _chip_exchange(sums, "grads_chip_exchange")))
    small_shapes = [small[n].shape for n in SMALL]
    all_small = _all_gather([_pack([small[n] for n in SMALL])], pltpu.VMEM, "gather_small_grads")[0]
    g = dict(zip(SMALL, _unpack(_sum_slots(all_small, "small_grads_sum"), small_shapes)))
    for n, axis in SMALL_SHARDED.items():
        g[n] = lax.dynamic_slice_in_dim(g[n], me * w[n].shape[axis], w[n].shape[axis], axis=axis)

    delta, new_m, new_v = {}, {}, {}
    for n in BIG:
        g[n], delta[n], new_m[n], new_v[n] = _adamw_summed(w[n], m[n], v[n], [by_chip[n, i] for i in range(w[n].shape[0])],
                                                           f"adamw_{n}")
    packed = [_pack([d[n] for n in SMALL]) for d in (w, g, m, v)]
    shapes = [w[n].shape for n in SMALL]
    for d, buf in zip((delta, new_m, new_v), _adamw(*packed, "adamw_small")):
        d.update(zip(SMALL, _unpack(buf, shapes)))
    return (loss, dx[None], *[g[n] for n in WEIGHTS], *[delta[n] for n in WEIGHTS], *[new_m[n] for n in WEIGHTS],
            *[new_v[n] for n in WEIGHTS])
```

```python
import functools
import math

import numpy as np
import jax
import jax.numpy as jnp
from jax import lax
from jax.experimental import pallas as pl
from jax.experimental.pallas import tpu as pltpu

F32, BF16 = jnp.float32, jnp.bfloat16
HI = lax.Precision.HIGHEST
SDS = jax.ShapeDtypeStruct

N_DEV = 8
D = 1024
D_FF = 2816
FF_SHARD = 2 * D_FF // N_DEV
PLE_DIM = 256
EPS = 1e-6
CONV_W = 31
SSM_CONV = 4
SSM_HEADS = 16
SSM_XBC = 1536
CHUNK = 128
HYB_IN = 4624
HYB_PAD = 5120
DT_COL = 4608
N_PAIR = 8
ROPE_THETA = 10000.0
LANE = 128
VMEM_LIMIT = 56 * 1024 * 1024

ADAM_LR, ADAM_B1, ADAM_B2, ADAM_EPS, ADAM_WD, ADAM_STEP = 0.001, 0.9, 0.999, 1e-08, 0.01, 10


def _params(sem):
    return pltpu.CompilerParams(dimension_semantics=sem, vmem_limit_bytes=VMEM_LIMIT)


def _mm(a, b, *, ta=False, tb=False, reduce_j=False, out_dtypes=(F32,), tm=1024, tn=1024, tk=1024,
        epi=None, extras=(), rows=(), name):
    ja, jb = a.shape[0], b.shape[0]
    nj = max(ja, jb)
    jo = 1 if reduce_j else nj
    m, k = (a.shape[2], a.shape[1]) if ta else (a.shape[1], a.shape[2])
    n = b.shape[1] if tb else b.shape[2]
    assert (b.shape[2] if tb else b.shape[1]) == k and ja in (1, nj) and jb in (1, nj)
    tm, tn, tk = min(tm, m), min(tn, n), min(tk, k)
    assert m % tm == 0 and n % tn == 0 and k % tk == 0, (name, m, n, k, tm, tn, tk)
    nk = k // tk
    steps = nk * (nj if reduce_j else 1)
    ne, nr, no = len(extras), len(rows), len(out_dtypes)

    def a_map(i, c, j, kk):
        return (j if ja > 1 else 0, kk, i) if ta else (j if ja > 1 else 0, i, kk)

    def b_map(i, c, j, kk):
        return (j if jb > 1 else 0, c, kk) if tb else (j if jb > 1 else 0, kk, c)

    def o_map(i, c, j, kk):
        return (0 if reduce_j else j, i, c)

    dims = (((0 if ta else 1,), (1 if tb else 0,)), ((), ()))

    def body(a_ref, b_ref, *rest):
        ex, rw, outs = rest[:ne], rest[ne:ne + nr], rest[ne + nr:ne + nr + no]
        part = lax.dot_general(a_ref[...], b_ref[...], dims, preferred_element_type=F32)

        def finish(acc):
            res = epi(acc, *[e[...] for e in ex], *[r[...] for r in rw]) if epi else (acc,)
            for o, r in zip(outs, res):
                o[...] = r.astype(o.dtype)

        if steps == 1:
            finish(part)
            return
        acc_ref = rest[-1]
        kk = pl.program_id(3)
        step = pl.program_id(2) * nk + kk if reduce_j else kk

        @pl.when(step == 0)
        def _():
            acc_ref[...] = part

        @pl.when(step > 0)
        def _():
            acc_ref[...] += part

        @pl.when(step == steps - 1)
        def _():
            finish(acc_ref[...])

    o_spec = pl.BlockSpec((None, tm, tn), o_map)
    return pl.pallas_call(
        body, name=name, grid=(m // tm, n // tn, nj, nk),
        in_specs=[pl.BlockSpec((None, tk, tm) if ta else (None, tm, tk), a_map),
                  pl.BlockSpec((None, tn, tk) if tb else (None, tk, tn), b_map)]
        + [o_spec] * ne + [pl.BlockSpec((1, tn), lambda i, c, j, kk: (0, c))] * nr,
        out_specs=[o_spec] * no,
        out_shape=[SDS((jo, m, n), dt) for dt in out_dtypes],
        scratch_shapes=[pltpu.VMEM((tm, tn), F32)] if steps > 1 else [],
        compiler_params=_params(("parallel", "parallel", "arbitrary", "arbitrary")),
    )(a, b, *extras, *rows)


def _whole(p):
    return pl.BlockSpec(p.shape, lambda *_: (0,) * p.ndim)


ANY_SPEC = pl.BlockSpec(memory_space=pl.ANY)


def _rowop(fn, tiles, params, outs, *, grid, name, deps=()):
    nin = len(tiles) + len(params)

    def body(*refs):
        res = fn(*[r[...].astype(F32) for r in refs[:nin]])
        for r, o in zip(refs[nin + len(deps):], res):
            r[...] = o.astype(r.dtype)

    return pl.pallas_call(
        body, name=name, grid=grid,
        in_specs=[s for _, s in tiles] + [_whole(p) for p in params] + [ANY_SPEC] * len(deps),
        out_specs=[s for _, _, s in outs], out_shape=[SDS(sh, dt) for sh, dt, _ in outs],
        compiler_params=_params(("parallel",) * len(grid)),
    )(*[t for t, _ in tiles], *params, *deps)


def _rowop_bwd(fn, tiles, params, cots, wrt, gouts, *, grid, name, adds=(), deps=()):
    nt, npar, nc, na = len(tiles), len(params), len(cots), len(adds)
    nin = nt + npar
    flat = [i for grp in wrt for i in grp]
    n_gout = sum(len(dts) for _, dts, _ in gouts)

    def body(*refs):
        vals = [r[...].astype(F32) for r in refs[:nin]]
        cvals = [r[...].astype(F32) for r in refs[nin:nin + nc]]
        avals = [r[...].astype(F32) for r in refs[nin + nc:nin + nc + na]]
        orefs = refs[nin + nc + na + len(deps):]
        diff_idx = flat + list(range(nt, nin))

        def f(*dv):
            full = list(vals)
            for i, v in zip(diff_idx, dv):
                full[i] = v
            return fn(*full)

        _, vjp = jax.vjp(f, *[vals[i] for i in diff_idx])
        grads = vjp(tuple(cvals))
        tile_g, par_g = list(grads[:len(flat)]), grads[len(flat):]
        group_g, at = [], 0
        for grp in wrt:
            members = tile_g[at:at + len(grp)]
            at += len(grp)
            group_g.append(members[0] if len(grp) == 1 else jnp.stack(members, axis=0))
        for av in avals:
            group_g[0] = group_g[0] + av
        o = 0
        for g, (_, dts, _) in zip(group_g, gouts):
            for _ in dts:
                orefs[o][...] = g.astype(orefs[o].dtype)
                o += 1
        first = functools.reduce(jnp.logical_and, [pl.program_id(ax) == 0 for ax in range(len(grid))])
        for r, g in zip(orefs[n_gout:], par_g):
            @pl.when(first)
            def _(r=r, g=g):
                r[...] = g

            @pl.when(jnp.logical_not(first))
            def _(r=r, g=g):
                r[...] += g

    out_specs, out_shape = [], []
    for sh, dts, spec in gouts:
        for dt in dts:
            out_specs.append(spec)
            out_shape.append(SDS(sh, dt))
    for p in params:
        out_specs.append(_whole(p))
        out_shape.append(SDS(p.shape, F32))
    return pl.pallas_call(
        body, name=name, grid=grid,
        in_specs=[s for _, s in tiles] + [_whole(p) for p in params] + [s for _, s in cots] + [s for _, s in adds]
        + [ANY_SPEC] * len(deps),
        out_specs=out_specs, out_shape=out_shape,
        compiler_params=_params(("arbitrary",) * len(grid)),
    )(*[t for t, _ in tiles], *params, *[c for c, _ in cots], *[a for a, _ in adds], *deps)


def _tok(c, tm, col=0):
    return pl.BlockSpec((tm, c), lambda i, col=col: (i, col))


def _rms_fn(h, g):
    return (h * lax.rsqrt(jnp.mean(h * h, axis=-1, keepdims=True) + EPS) * g,)


def _lnswish_fn(u, g, b):
    mu = jnp.mean(u, axis=-1, keepdims=True)
    xc = u - mu
    y = xc * lax.rsqrt(jnp.mean(xc * xc, axis=-1, keepdims=True) + EPS) * g + b
    return (y * jax.nn.sigmoid(y),)


def _ple_fn(z, e):
    return (jax.nn.sigmoid(z) * e,)


def _rms(h, g, name, tm=512, deps=()):
    t = h.shape[0]
    return _rowop(_rms_fn, [(h, _tok(D, tm))], [g], [((t, D), BF16, _tok(D, tm))], grid=(t // tm,), name=name, deps=deps)[0]


def _rms_bwd(h, g, dn, dres, name, tm=512):
    t = h.shape[0]
    return _rowop_bwd(_rms_fn, [(h, _tok(D, tm))], [g], [(dn, _tok(D, tm))], [(0,)], [((t, D), (F32, BF16), _tok(D, tm))],
                      grid=(t // tm,), name=name, adds=[(dres, _tok(D, tm))])


def _conv_geometry(width):
    pad = 32 if width > 8 else 8
    return pad, pad - (width - 1)


def _fill_shifts(xpad_ref, sh_ref, t, shifts):
    for r in shifts:
        sh_ref[r, :, :] = xpad_ref[pl.ds(r, t + 32), :]


def _dwconv(xs, w, b, *, width, glu, silu, cb, name):
    t = xs[0][0].shape[0]
    c = w.shape[1]
    pad, off = _conv_geometry(width)
    shifts = sorted({(k + off) % 8 for k in range(width)})
    ch = 32

    def body(*refs):
        x_refs, (w_ref, b_ref, o_ref, xpad_ref, sh_ref) = refs[:len(xs)], refs[len(xs):]
        u = x_refs[0][...] * jax.nn.sigmoid(x_refs[1][...]) if glu else x_refs[0][...]
        xpad_ref[pl.ds(0, pad), :] = jnp.zeros((pad, cb), F32)
        xpad_ref[pl.ds(pad, t), :] = u
        xpad_ref[pl.ds(pad + t, 40 - pad), :] = jnp.zeros((40 - pad, cb), F32)
        _fill_shifts(xpad_ref, sh_ref, t, shifts)

        def chunk(i, carry):
            t0 = pl.multiple_of(i * ch, ch)
            acc = jnp.broadcast_to(b_ref[...], (ch, cb))
            for k in range(width):
                q, r = divmod(k + off, 8)
                acc = acc + w_ref[pl.ds(k, 1), :] * sh_ref[r, pl.ds(t0 + 8 * q, ch), :]
            o_ref[pl.ds(t0, ch), :] = acc * jax.nn.sigmoid(acc) if silu else acc
            return carry

        lax.fori_loop(0, t // ch, chunk, 0)

    return pl.pallas_call(
        body, name=name, grid=(c // cb,),
        in_specs=[pl.BlockSpec((t, cb), lambda i, o=o: (0, o + i)) for _, o in xs]
        + [pl.BlockSpec((width, cb), lambda i: (0, i)), pl.BlockSpec((1, cb), lambda i: (0, i))],
        out_specs=pl.BlockSpec((t, cb), lambda i: (0, i)), out_shape=SDS((t, c), F32),
        scratch_shapes=[pltpu.VMEM((t + 40, cb), F32), pltpu.VMEM((8, t + 32, cb), F32)],
        compiler_params=_params(("parallel",)),
    )(*[x for x, _ in xs], w, b)


def _dwconv_bwd(xs, w, b, dy, *, width, glu, silu, cb, name):
    t = xs[0][0].shape[0]
    c = w.shape[1]
    pad, off = _conv_geometry(width)
    shifts = sorted({(k + off) % 8 for k in range(width)})
    shifts_t = sorted({mm % 8 for mm in range(width)})
    ch = 32
    nx = len(xs)

    def body(*refs):
        x_refs = refs[:nx]
        w_ref, b_ref, dy_ref = refs[nx:nx + 3]
        dx_refs = refs[nx + 3:nx + 3 + nx]
        dw_ref, db_ref, xpad_ref, sh_ref, dc_ref = refs[nx + 3 + nx:]
        u = x_refs[0][...] * jax.nn.sigmoid(x_refs[1][...]) if glu else x_refs[0][...]
        xpad_ref[pl.ds(0, pad), :] = jnp.zeros((pad, cb), F32)
        xpad_ref[pl.ds(pad, t), :] = u
        xpad_ref[pl.ds(pad + t, 40 - pad), :] = jnp.zeros((40 - pad, cb), F32)
        _fill_shifts(xpad_ref, sh_ref, t, shifts)

        if silu:
            def act_chunk(i, carry):
                t0 = pl.multiple_of(i * ch, ch)
                acc = jnp.broadcast_to(b_ref[...], (ch, cb))
                for k in range(width):
                    q, r = divmod(k + off, 8)
                    acc = acc + w_ref[pl.ds(k, 1), :] * sh_ref[r, pl.ds(t0 + 8 * q, ch), :]
                sg = jax.nn.sigmoid(acc)
                dc_ref[pl.ds(t0, ch), :] = dy_ref[pl.ds(t0, ch), :] * (sg * (1.0 + acc * (1.0 - sg)))
                return carry

            lax.fori_loop(0, t // ch, act_chunk, 0)
        else:
            dc_ref[...] = dy_ref[...]

        def dw_chunk(i, accs):
            t0 = pl.multiple_of(i * 8, 8)
            d = dc_ref[pl.ds(t0, 8), :]
            new = []
            for k in range(width):
                q, r = divmod(k + off, 8)
                new.append(accs[k] + d * sh_ref[r, pl.ds(t0 + 8 * q, 8), :])
            new.append(accs[width] + d)
            return tuple(new)

        accs = lax.fori_loop(0, t // 8, dw_chunk, tuple(jnp.zeros((8, cb), F32) for _ in range(width + 1)))
        for k in range(width):
            dw_ref[pl.ds(k, 1), :] = jnp.sum(accs[k], axis=0, keepdims=True)
        db_ref[...] = jnp.sum(accs[width], axis=0, keepdims=True)

        xpad_ref[pl.ds(0, t), :] = dc_ref[...]
        xpad_ref[pl.ds(t, 40), :] = jnp.zeros((40, cb), F32)
        _fill_shifts(xpad_ref, sh_ref, t, shifts_t)

        def dx_chunk(i, carry):
            t0 = pl.multiple_of(i * ch, ch)
            acc = jnp.zeros((ch, cb), F32)
            for mm in range(width):
                q, r = divmod(mm, 8)
                acc = acc + w_ref[pl.ds(width - 1 - mm, 1), :] * sh_ref[r, pl.ds(t0 + 8 * q, ch), :]
            if glu:
                val, gate = x_refs[0][pl.ds(t0, ch), :], x_refs[1][pl.ds(t0, ch), :]
                sg = jax.nn.sigmoid(gate)
                dx_refs[0][pl.ds(t0, ch), :] = (acc * sg).astype(BF16)
                dx_refs[1][pl.ds(t0, ch), :] = (acc * val * sg * (1.0 - sg)).astype(BF16)
            else:
                dx_refs[0][pl.ds(t0, ch), :] = acc.astype(BF16)
            return carry

        lax.fori_loop(0, t // ch, dx_chunk, 0)

    col = pl.BlockSpec((t, cb), lambda i: (0, i))
    return pl.pallas_call(
        body, name=name, grid=(c // cb,),
        in_specs=[pl.BlockSpec((t, cb), lambda i, o=o: (0, o + i)) for _, o in xs]
        + [pl.BlockSpec((width, cb), lambda i: (0, i)), pl.BlockSpec((1, cb), lambda i: (0, i)), col],
        out_specs=[col] * nx + [pl.BlockSpec((width, cb), lambda i: (0, i)), pl.BlockSpec((1, cb), lambda i: (0, i))],
        out_shape=[SDS((t, c), BF16)] * nx + [SDS((width, c), F32), SDS((1, c), F32)],
        scratch_shapes=[pltpu.VMEM((t + 40, cb), F32), pltpu.VMEM((8, t + 32, cb), F32), pltpu.VMEM((t, cb), F32)],
        compiler_params=_params(("parallel",)),
    )(*[x for x, _ in xs], w, b, dy)


_DIMS = {"nn": (((1,), (0,)), ((), ())), "nt": (((1,), (1,)), ((), ())), "tn": (((0,), (0,)), ((), ()))}


def _raw_dot(a, b, mode):
    return lax.dot_general(a.astype(BF16), b.astype(BF16), _DIMS[mode], preferred_element_type=F32)


@functools.partial(jax.custom_vjp, nondiff_argnums=(2,))
def _bdot(a, b, mode):
    return _raw_dot(a, b, mode)


def _bdot_fwd(a, b, mode):
    return _raw_dot(a, b, mode), (a, b)


def _bdot_bwd(mode, res, g):
    a, b = res
    if mode == "nn":
        return _raw_dot(g, b, "nt"), _raw_dot(a, g, "tn")
    if mode == "nt":
        return _raw_dot(g, b, "nn"), _raw_dot(g, a, "tn")
    return _raw_dot(b, g, "nt"), _raw_dot(a, g, "nn")


_bdot.defvjp(_bdot_fwd, _bdot_bwd)


def _iota(shape, axis):
    return lax.broadcasted_iota(jnp.int32, shape, axis)


def _half_masks():
    left = (_iota((1, LANE), 1) < 64).astype(F32)
    return left, 1.0 - left


def _ssd_chunk(state, xa, dtr, z, dtb, alog, dsk, ng):
    xs, bm, cm = xa[:, :D], xa[:, D:D + 256], xa[:, D + 256:]
    left, right = _half_masks()
    expand = (_iota((LANE, D), 1) // 64 == _iota((LANE, D), 0)).astype(F32)
    li, si = _iota((CHUNK, CHUNK), 0), _iota((CHUNK, CHUNK), 1)
    tril = li >= si
    dt16 = jax.nn.softplus(dtr + dtb)
    adt = dt16 * (-jnp.exp(alog))
    dtf = jnp.dot(dt16, expand, precision=HI)
    cs16 = jnp.dot(tril.astype(F32), adt, precision=HI)
    csf = jnp.dot(cs16, expand, precision=HI)
    totf = jnp.sum(jnp.dot(adt, expand, precision=HI), axis=0, keepdims=True)
    cst = cs16.T
    xdt = xs * dtf
    ys, new_state = [], []
    for g in range(2):
        bg, cg = bm[:, LANE * g:LANE * (g + 1)], cm[:, LANE * g:LANE * (g + 1)]
        cb = _bdot(cg, bg, "nt")
        for q in range(4):
            pr = 4 * g + q
            decay = []
            for h in (2 * pr, 2 * pr + 1):
                col = jnp.sum(jnp.where(si == h, cs16, 0.0), axis=1, keepdims=True)
                row = jnp.sum(jnp.where(li == h, cst, 0.0), axis=0, keepdims=True)
                decay.append(cb * jnp.exp(jnp.where(tril, col - row, -jnp.inf)))
            xp = xdt[:, LANE * pr:LANE * (pr + 1)]
            y_diag = _bdot(jnp.concatenate(decay, axis=1), jnp.concatenate([xp * left, xp * right], axis=0), "nn")
            csb, tot = csf[:, LANE * pr:LANE * (pr + 1)], totf[:, LANE * pr:LANE * (pr + 1)]
            ys.append(y_diag + _bdot(cg, state[pr], "nn") * jnp.exp(csb))
            new_state.append(state[pr] * jnp.exp(tot) + _bdot(bg, xp * jnp.exp(tot - csb), "tn"))
    y = jnp.concatenate(ys, axis=1)
    y = y + jnp.dot(jnp.broadcast_to(dsk, (CHUNK, LANE)), expand, precision=HI) * xs
    y = y * (z * jax.nn.sigmoid(z))
    halves = []
    for g in range(2):
        yg = y[:, 512 * g:512 * (g + 1)]
        halves.append(yg * lax.rsqrt(jnp.mean(yg * yg, axis=-1, keepdims=True) + EPS))
    return jnp.concatenate(halves, axis=1) * ng, jnp.stack(new_state, axis=0)


def _ssd_specs(t, rev):
    nc = t // CHUNK
    ix = (lambda c: nc - 1 - c) if rev else (lambda c: c)
    return nc, ix


def _ssd_fwd(xa, proj, dtb, alog, dsk, ng, name):
    t = xa.shape[0]
    nc, ix = _ssd_specs(t, False)

    def body(xa_ref, dt_ref, z_ref, dtb_ref, alog_ref, dsk_ref, ng_ref, y_ref, st_ref, carry_ref):
        @pl.when(pl.program_id(0) == 0)
        def _():
            carry_ref[...] = jnp.zeros_like(carry_ref)

        st_ref[...] = carry_ref[...]
        y, new = _ssd_chunk(carry_ref[...], xa_ref[...], dt_ref[...], z_ref[...], dtb_ref[...], alog_ref[...],
                            dsk_ref[...], ng_ref[...])
        y_ref[...] = y.astype(BF16)
        carry_ref[...] = new

    small = [dtb, alog, dsk, ng]
    return pl.pallas_call(
        body, name=name, grid=(nc,),
        in_specs=[pl.BlockSpec((CHUNK, SSM_XBC), lambda c: (c, 0)),
                  pl.BlockSpec((CHUNK, LANE), lambda c: (c, DT_COL // LANE)),
                  pl.BlockSpec((CHUNK, D), lambda c: (c, 2))] + [_whole(p) for p in small],
        out_specs=[pl.BlockSpec((CHUNK, D), lambda c: (c, 0)), pl.BlockSpec((None, N_PAIR, LANE, LANE), lambda c: (c, 0, 0, 0))],
        out_shape=[SDS((t, D), BF16), SDS((nc, N_PAIR, LANE, LANE), F32)],
        scratch_shapes=[pltpu.VMEM((N_PAIR, LANE, LANE), F32)],
        compiler_params=_params(("arbitrary",)),
    )(xa, proj, proj, *small)


def _ssd_bwd(xa, proj, states, dy, dtb, alog, dsk, ng, name):
    t = xa.shape[0]
    nc, ix = _ssd_specs(t, True)

    def body(xa_ref, dt_ref, z_ref, st_ref, dy_ref, dtb_ref, alog_ref, dsk_ref, ng_ref,
             dxa_ref, ddt_ref, dz_ref, gdtb_ref, galog_ref, gdsk_ref, gng_ref, carry_ref):
        first = pl.program_id(0) == 0

        @pl.when(first)
        def _():
            carry_ref[...] = jnp.zeros_like(carry_ref)

        args = (st_ref[...], xa_ref[...], dt_ref[...], z_ref[...], dtb_ref[...], alog_ref[...], dsk_ref[...], ng_ref[...])
        _, vjp = jax.vjp(_ssd_chunk, *args)
        ds, dxa, ddt, dz, gdtb, galog, gdsk, gng = vjp((dy_ref[...], carry_ref[...]))
        carry_ref[...] = ds
        dxa_ref[...] = dxa
        ddt_ref[...] = ddt.astype(BF16)
        dz_ref[...] = dz.astype(BF16)
        for r, g in ((gdtb_ref, gdtb), (galog_ref, galog), (gdsk_ref, gdsk), (gng_ref, gng)):
            @pl.when(first)
            def _(r=r, g=g):
                r[...] = g

            @pl.when(jnp.logical_not(first))
            def _(r=r, g=g):
                r[...] += g

    small = [dtb, alog, dsk, ng]
    return pl.pallas_call(
        body, name=name, grid=(nc,),
        in_specs=[pl.BlockSpec((CHUNK, SSM_XBC), lambda c: (ix(c), 0)),
                  pl.BlockSpec((CHUNK, LANE), lambda c: (ix(c), DT_COL // LANE)),
                  pl.BlockSpec((CHUNK, D), lambda c: (ix(c), 2)),
                  pl.BlockSpec((None, N_PAIR, LANE, LANE), lambda c: (ix(c), 0, 0, 0)),
                  pl.BlockSpec((CHUNK, D), lambda c: (ix(c), 0))] + [_whole(p) for p in small],
        out_specs=[pl.BlockSpec((CHUNK, SSM_XBC), lambda c: (ix(c), 0)), pl.BlockSpec((CHUNK, LANE), lambda c: (ix(c), 0)),
                   pl.BlockSpec((CHUNK, D), lambda c: (ix(c), 0))] + [_whole(p) for p in small],
        out_shape=[SDS((t, SSM_XBC), F32), SDS((t, LANE), BF16), SDS((t, D), BF16)] + [SDS(p.shape, F32) for p in small],
        scratch_shapes=[pltpu.VMEM((N_PAIR, LANE, LANE), F32)],
        compiler_params=_params(("arbitrary",)),
    )(xa, proj, proj, states, dy, *small)


def _attn_block(q, kv_prev, kv_cur, cq, sq, ck, sk, sinks, rot, first_block):
    left, right = _half_masks()
    k2 = jnp.concatenate([kv_prev[:, :256], kv_cur[:, :256]], axis=0)
    v2 = jnp.concatenate([kv_prev[:, 256:], kv_cur[:, 256:]], axis=0)
    ri, ci = _iota((LANE, LANE), 0), _iota((LANE, LANE), 1)
    dup = [((ri < 64) & (ci % 64 == ri)).astype(BF16), ((ri >= 64) & (ci % 64 == ri - 64)).astype(BF16)]

    def rope(tt, c, s):
        return tt * c + jnp.dot(tt, rot, precision=HI) * s

    kd, vd = [], []
    for j in range(4):
        sl = slice(LANE * (j // 2), LANE * (j // 2 + 1))
        kd.append(_bdot(rope(k2[:, sl], ck, sk), dup[j % 2], "nn"))
        vd.append(_bdot(v2[:, sl], dup[j % 2], "nn"))
    qi, si = _iota((2 * CHUNK, 2 * CHUNK), 0) % CHUNK, _iota((2 * CHUNK, 2 * CHUNK), 1)
    valid = (si > qi) & (si <= qi + CHUNK) & jnp.logical_or(si >= CHUNK, jnp.logical_not(first_block))
    upper = _iota((2 * CHUNK, 1), 0) < CHUNK
    lanes = _iota((1, LANE), 1)
    outs = []
    for pr in range(N_PAIR):
        qr = rope(q[:, LANE * pr:LANE * (pr + 1)], cq, sq)
        lg = _bdot(jnp.concatenate([qr * left, qr * right], axis=0), kd[pr // 2], "nt") * 0.125
        lg = jnp.where(valid, lg, -jnp.inf)
        s1 = jnp.sum(jnp.where(lanes == 2 * pr, sinks, 0.0), axis=1, keepdims=True)
        s2 = jnp.sum(jnp.where(lanes == 2 * pr + 1, sinks, 0.0), axis=1, keepdims=True)
        sink = jnp.where(upper, s1, s2)
        mx = lax.stop_gradient(jnp.maximum(jnp.max(lg, axis=-1, keepdims=True), sink))
        e = jnp.exp(lg - mx)
        probs = e / (jnp.sum(e, axis=-1, keepdims=True) + jnp.exp(sink - mx))
        o2 = _bdot(probs, vd[pr // 2], "nn")
        outs.append(o2[:CHUNK] * left + o2[CHUNK:] * right)
    return jnp.concatenate(outs, axis=1)


def _attn_fwd(qkv, cos, sin, sinks, rot, name):
    t = qkv.shape[0]
    nb = t // CHUNK

    def body(q_ref, kvp_ref, kvc_ref, cq_ref, sq_ref, cp_ref, sp_ref, sinks_ref, rot_ref, o_ref):
        ck = jnp.concatenate([cp_ref[...], cq_ref[...]], axis=0)
        sk = jnp.concatenate([sp_ref[...], sq_ref[...]], axis=0)
        o_ref[...] = _attn_block(q_ref[...], kvp_ref[...], kvc_ref[...], cq_ref[...], sq_ref[...], ck, sk,
                                 sinks_ref[...], rot_ref[...], pl.program_id(0) == 0).astype(BF16)

    prev = lambda n: jnp.maximum(n - 1, 0)
    return pl.pallas_call(
        body, name=name, grid=(nb,),
        in_specs=[pl.BlockSpec((CHUNK, D), lambda n: (n, 0)),
                  pl.BlockSpec((CHUNK, 512), lambda n: (prev(n), 2)), pl.BlockSpec((CHUNK, 512), lambda n: (n, 2)),
                  pl.BlockSpec((CHUNK, LANE), lambda n: (n, 0)), pl.BlockSpec((CHUNK, LANE), lambda n: (n, 0)),
                  pl.BlockSpec((CHUNK, LANE), lambda n: (prev(n), 0)), pl.BlockSpec((CHUNK, LANE), lambda n: (prev(n), 0)),
                  _whole(sinks), _whole(rot)],
        out_specs=pl.BlockSpec((CHUNK, D), lambda n: (n, 0)), out_shape=SDS((t, D), BF16),
        compiler_params=_params(("parallel",)),
    )(qkv, qkv, qkv, cos, sin, cos, sin, sinks, rot)


def _attn_bwd(qkv, do, cos, sin, sinks, rot, name):
    t = qkv.shape[0]
    nb = t // CHUNK

    def body(q_ref, kvp_ref, kvc_ref, do_ref, cq_ref, sq_ref, cp_ref, sp_ref, sinks_ref, rot_ref,
             dq_ref, dkv_ref, dbq_ref, dbkv_ref, dsink_ref, carry_ref):
        n = pl.program_id(0)

        @pl.when(n == 0)
        def _():
            carry_ref[...] = jnp.zeros_like(carry_ref)
            dbq_ref[...] = jnp.zeros_like(dbq_ref)
            dbkv_ref[...] = jnp.zeros_like(dbkv_ref)
            dsink_ref[...] = jnp.zeros_like(dsink_ref)

        @pl.when(n < nb)
        def _():
            ck = jnp.concatenate([cp_ref[...], cq_ref[...]], axis=0)
            sk = jnp.concatenate([sp_ref[...], sq_ref[...]], axis=0)
            f = lambda q, kvp, kvc, s: _attn_block(q, kvp, kvc, cq_ref[...], sq_ref[...], ck, sk, s, rot_ref[...], n == 0)
            _, vjp = jax.vjp(f, q_ref[...], kvp_ref[...], kvc_ref[...], sinks_ref[...])
            dq, dkvp, dkvc, ds = vjp(do_ref[...].astype(F32))
            done = carry_ref[...] + dkvp
            dq_ref[...] = dq.astype(BF16)
            dkv_ref[...] = done.astype(BF16)
            dbq_ref[...] += jnp.sum(dq, axis=0, keepdims=True)
            dsink_ref[...] += ds
            carry_ref[...] = dkvc

            @pl.when(n > 0)
            def _():
                dbkv_ref[...] += jnp.sum(done, axis=0, keepdims=True)

        @pl.when(n == nb)
        def _():
            done = carry_ref[...]
            dkv_ref[...] = done.astype(BF16)
            dbkv_ref[...] += jnp.sum(done, axis=0, keepdims=True)

    cur = lambda n: jnp.minimum(n, nb - 1)
    prev = lambda n: jnp.maximum(jnp.minimum(n, nb - 1) - 1, 0)
    fin = lambda n: jnp.maximum(n - 1, 0)
    outs = pl.pallas_call(
        body, name=name, grid=(nb + 1,),
        in_specs=[pl.BlockSpec((CHUNK, D), lambda n: (cur(n), 0)),
                  pl.BlockSpec((CHUNK, 512), lambda n: (prev(n), 2)), pl.BlockSpec((CHUNK, 512), lambda n: (cur(n), 2)),
                  pl.BlockSpec((CHUNK, D), lambda n: (cur(n), 0)),
                  pl.BlockSpec((CHUNK, LANE), lambda n: (cur(n), 0)), pl.BlockSpec((CHUNK, LANE), lambda n: (cur(n), 0)),
                  pl.BlockSpec((CHUNK, LANE), lambda n: (prev(n), 0)), pl.BlockSpec((CHUNK, LANE), lambda n: (prev(n), 0)),
                  _whole(sinks), _whole(rot)],
        out_specs=[pl.BlockSpec((CHUNK, D), lambda n: (cur(n), 0)), pl.BlockSpec((CHUNK, 512), lambda n: (fin(n), 0)),
                   pl.BlockSpec((1, D), lambda n: (0, 0)), pl.BlockSpec((1, 512), lambda n: (0, 0)), _whole(sinks)],
        out_shape=[SDS((t, D), BF16), SDS((t, 512), BF16), SDS((1, D), F32), SDS((1, 512), F32), SDS(sinks.shape, F32)],
        scratch_shapes=[pltpu.VMEM((CHUNK, 512), F32)],
        compiler_params=_params(("arbitrary",)),
    )(qkv, qkv, qkv, do, cos, sin, cos, sin, sinks, rot)
    dq, dkv, dbq, dbkv, dsinks = outs
    return jnp.concatenate([dq, dkv], axis=1), jnp.concatenate([dbq, dbkv], axis=1), dsinks


def _loss_head(h, tgt, g, name, tm=512):
    t = h.shape[0]

    def body(h_ref, t_ref, g_ref, loss_ref, dh_ref, dhb_ref, dg_ref):
        def f(hv, gv):
            err = _rms_fn(hv, gv)[0] - t_ref[...]
            return 0.5 * jnp.sum(jnp.mean(err * err, axis=-1, keepdims=True), axis=0, keepdims=True)

        loss, vjp = jax.vjp(f, h_ref[...], g_ref[...])
        dh, dg = vjp(jnp.ones((1, 1), F32))
        dh_ref[...] = dh
        dhb_ref[...] = dh.astype(BF16)
        first = pl.program_id(0) == 0

        @pl.when(first)
        def _():
            loss_ref[...] = loss
            dg_ref[...] = dg

        @pl.when(jnp.logical_not(first))
        def _():
            loss_ref[...] += loss
            dg_ref[...] += dg

    return pl.pallas_call(
        body, name=name, grid=(t // tm,),
        in_specs=[_tok(D, tm), _tok(D, tm), _whole(g)],
        out_specs=[pl.BlockSpec((1, 1), lambda i: (0, 0)), _tok(D, tm), _tok(D, tm), _whole(g)],
        out_shape=[SDS((1, 1), F32), SDS((t, D), F32), SDS((t, D), BF16), SDS(g.shape, F32)],
        compiler_params=_params(("arbitrary",)),
    )(h, tgt, g)


def _res_half(acc, res):
    return (res + 0.5 * acc,)


def _res_full(acc, res):
    return (res + acc,)


def _half(acc):
    return (0.5 * acc,)


def _ffn_in(n, w_in, name, tm=1024):
    t = n.shape[0]
    tm = min(tm, t)

    def body(n_ref, w_ref, pre_ref, act_ref):
        a = n_ref[...]
        gate = jnp.dot(a, w_ref[0], preferred_element_type=F32)
        up = jnp.dot(a, w_ref[1], preferred_element_type=F32)
        pre_ref[0] = gate
        pre_ref[1] = up
        act_ref[...] = (gate * jax.nn.sigmoid(gate) * up).astype(BF16)

    pair = pl.BlockSpec((2, None, tm, FF_SHARD), lambda i, j: (0, j, i, 0))
    return pl.pallas_call(
        body, name=name, grid=(t // tm, 4),
        in_specs=[pl.BlockSpec((tm, D), lambda i, j: (i, 0)), pl.BlockSpec((2, None, D, FF_SHARD), lambda i, j: (0, j, 0, 0))],
        out_specs=[pair, pl.BlockSpec((None, tm, FF_SHARD), lambda i, j: (j, i, 0))],
        out_shape=[SDS((2, 4, t, FF_SHARD), F32), SDS((4, t, FF_SHARD), BF16)],
        compiler_params=_params(("parallel", "parallel")),
    )(n, w_in.reshape(2, 4, D, FF_SHARD))


def _ffn_dact(dhb, w_out, pre, name, tm=1024, deps=()):
    t = dhb.shape[0]
    tm = min(tm, t)

    def body(d_ref, w_ref, pre_ref, *rest):
        o_ref = rest[-1]
        dact = 0.5 * lax.dot_general(d_ref[...], w_ref[...], _DIMS["nt"], preferred_element_type=F32)
        gate, up = pre_ref[0], pre_ref[1]
        sg = jax.nn.sigmoid(gate)
        o_ref[0] = (dact * up * (sg * (1.0 + gate * (1.0 - sg)))).astype(BF16)
        o_ref[1] = (dact * (gate * sg)).astype(BF16)

    pair = pl.BlockSpec((2, None, tm, FF_SHARD), lambda i, j: (0, j, i, 0))
    return pl.pallas_call(
        body, name=name, grid=(t // tm, 4),
        in_specs=[pl.BlockSpec((tm, D), lambda i, j: (i, 0)), pl.BlockSpec((None, FF_SHARD, D), lambda i, j: (j, 0, 0)), pair]
        + [ANY_SPEC] * len(deps),
        out_specs=pair, out_shape=SDS((2, 4, t, FF_SHARD), BF16),
        compiler_params=_params(("parallel", "parallel")),
    )(dhb, w_out, pre, *deps)


def _ffn_fwd(h, g, w_in, w_out, tag, deps=()):
    n = _rms(h, g, f"{tag}_rms", deps=deps)
    pre, act = _ffn_in(n, w_in, f"{tag}_in")
    out = _mm(act, w_out, reduce_j=True, tk=FF_SHARD, epi=_res_half, extras=(h[None],), name=f"{tag}_out")[0][0]
    return out, (h, n, pre, act)


def _ffn_bwd(dh, dhb, saved, g, w_in, w_out, tag, deps=()):
    h, n, pre, act = saved
    t = h.shape[0]
    dpre = _ffn_dact(dhb, w_out, pre, f"{tag}_dact", deps=deps).reshape(N_DEV, t, FF_SHARD)
    dw_out = _mm(act, dhb[None], ta=True, tm=FF_SHARD, epi=_half, out_dtypes=(BF16,), name=f"{tag}_dwout")[0]
    dn = _mm(dpre, w_in, tb=True, reduce_j=True, tk=FF_SHARD, name=f"{tag}_dn")[0][0]
    dw_in = _mm(n[None], dpre, ta=True, tn=FF_SHARD, out_dtypes=(BF16,), name=f"{tag}_dwin")[0]
    dh_in, dhb_in, dg = _rms_bwd(h, g, dn, dh, f"{tag}_drms")
    return dh_in, dhb_in, dg, dw_in, dw_out


def _ple_fwd(h, g, pb, w_gate, w_proj, tag):
    t = h.shape[0]
    tm = 512
    n = _rms(h, g, f"{tag}_rms")
    e = _mm(pb[None], w_proj[None], name=f"{tag}_proj")[0][0]
    z = _mm(n[None], w_gate[None], name=f"{tag}_gate")[0][0]
    out = _rowop(lambda zz, ee, hh: (hh + _ple_fn(zz, ee)[0],), [(z, _tok(D, tm)), (e, _tok(D, tm)), (h, _tok(D, tm))], [],
                 [((t, D), F32, _tok(D, tm))], grid=(t // tm,), name=f"{tag}_mix")[0]
    return out, (h, n, e, z)


def _ple_bwd(dh, dhb, saved, g, pb, w_gate, tag, deps=()):
    h, n, e, z = saved
    t = h.shape[0]
    tm = 512
    dz, de = _rowop_bwd(_ple_fn, [(z, _tok(D, tm)), (e, _tok(D, tm))], [], [(dh, _tok(D, tm))], [(0,), (1,)],
                        [((t, D), (BF16,), _tok(D, tm)), ((t, D), (BF16,), _tok(D, tm))], grid=(t // tm,), name=f"{tag}_dmix",
                        deps=deps)
    dw_proj = _mm(pb[None], de[None], ta=True, out_dtypes=(BF16,), name=f"{tag}_dwproj")[0][0]
    dw_gate = _mm(n[None], dz[None], ta=True, out_dtypes=(BF16,), name=f"{tag}_dwgate")[0][0]
    dn = _mm(dz[None], w_gate[None], tb=True, name=f"{tag}_dn")[0][0]
    dh_in, dhb_in, dg = _rms_bwd(h, g, dn, dh, f"{tag}_drms")
    return dh_in, dhb_in, dg, dw_gate, dw_proj


def _hyb_fwd(h, w, tag):
    t = h.shape[0]
    tm = 512
    hn = _rms(h, w["norm_mix"], f"{tag}_rms")
    proj = _mm(hn[None], w["hyb_in"][None], tn=512, name=f"{tag}_in")[0][0]
    u1 = _dwconv([(proj, 0), (proj, D // LANE)], w["conv_w"], w["conv_b"], width=CONV_W, glu=True, silu=False, cb=LANE,
                 name=f"{tag}_conv")
    u = _rowop(_lnswish_fn, [(u1, _tok(D, tm))], [w["ln_g"], w["ln_b"]], [((t, D), BF16, _tok(D, tm))], grid=(t // tm,),
               name=f"{tag}_ln")[0]
    xa = _dwconv([(proj, 3 * D // LANE)], w["sconv_w"], w["sconv_b"], width=SSM_CONV, glu=False, silu=True, cb=LANE,
                 name=f"{tag}_sconv")
    y, states = _ssd_fwd(xa, proj, w["dt_bias"], w["a_log"], w["d_skip"], w["ssm_norm"], f"{tag}_ssd")
    mixed = jnp.stack([u, y], axis=0)
    out = _mm(mixed, w["hyb_out"], reduce_j=True, epi=_res_full, extras=(h[None],), name=f"{tag}_out")[0][0]
    return out, (h, hn, proj, u1, xa, states, mixed)


def _hyb_bwd(dh, dhb, saved, w, tag):
    h, hn, proj, u1, xa, states, mixed = saved
    t = h.shape[0]
    tm = 512
    dmix = _mm(dhb[None], w["hyb_out"], tb=True, name=f"{tag}_dmix")[0]
    dw_out = _mm(mixed, dhb[None], ta=True, out_dtypes=(BF16,), name=f"{tag}_dwout")[0]
    du1, dln_g, dln_b = _rowop_bwd(_lnswish_fn, [(u1, _tok(D, tm))], [w["ln_g"], w["ln_b"]], [(dmix[0], _tok(D, tm))], [(0,)],
                                   [((t, D), (F32,), _tok(D, tm))], grid=(t // tm,), name=f"{tag}_dln")
    dval, dgate, dconv_w, dconv_b = _dwconv_bwd([(proj, 0), (proj, D // LANE)], w["conv_w"], w["conv_b"], du1,
                                                width=CONV_W, glu=True, silu=False, cb=LANE, name=f"{tag}_dconv")
    dxa, ddt, dz, g_dtb, g_alog, g_dsk, g_ng = _ssd_bwd(xa, proj, states, dmix[1], w["dt_bias"], w["a_log"], w["d_skip"],
                                                         w["ssm_norm"], f"{tag}_dssd")
    dxbc, dsconv_w, dsconv_b = _dwconv_bwd([(proj, 3 * D // LANE)], w["sconv_w"], w["sconv_b"], dxa, width=SSM_CONV,
                                           glu=False, silu=True, cb=LANE, name=f"{tag}_dsconv")
    dproj = jnp.concatenate([dval, dgate, dz, dxbc, ddt, jnp.zeros((t, HYB_PAD - DT_COL - LANE), BF16)], axis=1)
    dhn = _mm(dproj[None], w["hyb_in"][None], tb=True, name=f"{tag}_dhn")[0][0]
    dw_in = _mm(hn[None], dproj[None], ta=True, tn=512, out_dtypes=(BF16,), name=f"{tag}_dwin")[0][0]
    dh_in, dhb_in, dg = _rms_bwd(h, w["norm_mix"], dhn, dh, f"{tag}_drms")
    grads = dict(norm_mix=dg, hyb_in=dw_in, hyb_out=dw_out, conv_w=dconv_w, conv_b=dconv_b, ln_g=dln_g, ln_b=dln_b,
                 sconv_w=dsconv_w, sconv_b=dsconv_b, dt_bias=g_dtb, a_log=g_alog, d_skip=g_dsk, ssm_norm=g_ng)
    return dh_in, dhb_in, grads


def _bias_epi(acc, row):
    return (acc + row,)


def _res_bias_epi(acc, res, row):
    return (res + acc + row,)


def _att_fwd(h, w, tables, tag):
    cos, sin, rot = tables
    hn = _rms(h, w["norm_mix"], f"{tag}_rms")
    qkv = _mm(hn[None], w["qkv"][None], tn=512, epi=_bias_epi, rows=(w["b_qkv"],), name=f"{tag}_qkv")[0][0]
    o = _attn_fwd(qkv, cos, sin, w["sinks"], rot, f"{tag}_core")
    out = _mm(o[None], w["w_o"][None], epi=_res_bias_epi, extras=(h[None],), rows=(w["b_o"],), name=f"{tag}_out")[0][0]
    return out, (h, hn, qkv, o)


def _att_bwd(dh, dhb, saved, w, tables, tag):
    cos, sin, rot = tables
    h, hn, qkv, o = saved
    t = h.shape[0]
    tm = 512
    do = _mm(dhb[None], w["w_o"][None], tb=True, out_dtypes=(BF16,), name=f"{tag}_do")[0][0]
    dw_o = _mm(o[None], dhb[None], ta=True, out_dtypes=(BF16,), name=f"{tag}_dwo")[0][0]
    db_o = _rowop_bwd(lambda xx, bb: (xx + bb,), [(dh, _tok(D, tm))], [w["b_o"]], [(dh, _tok(D, tm))], [], [],
                      grid=(t // tm,), name=f"{tag}_dbo")[0]
    dqkv, db_qkv, dsinks = _attn_bwd(qkv, do, cos, sin, w["sinks"], rot, f"{tag}_dcore")
    dhn = _mm(dqkv[None], w["qkv"][None], tb=True, tk=512, name=f"{tag}_dhn")[0][0]
    dw_qkv = _mm(hn[None], dqkv[None], ta=True, tn=512, out_dtypes=(BF16,), name=f"{tag}_dwqkv")[0][0]
    dh_in, dhb_in, dg = _rms_bwd(h, w["norm_mix"], dhn, dh, f"{tag}_drms")
    grads = dict(norm_mix=dg, qkv=dw_qkv, b_qkv=db_qkv, sinks=dsinks, w_o=dw_o, b_o=db_o)
    return dh_in, dhb_in, grads


def _rope_tables(t):
    inv = ROPE_THETA ** (-jnp.arange(0, 64, 2, dtype=F32) / 64)
    ang = jnp.arange(t, dtype=F32)[:, None] * inv[None, :]
    cos, sin = jnp.tile(jnp.cos(ang), (1, 4)), jnp.tile(jnp.sin(ang), (1, 4))
    rot = np.zeros((LANE, LANE), np.float32)
    for j in range(LANE):
        if j % 64 < 32:
            rot[j + 32, j] = -1.0
        else:
            rot[j - 32, j] = 1.0
    return cos, sin, jnp.asarray(rot)


def _local_step(x, p, tgt, layers, final_norm):
    tables = _rope_tables(x.shape[0])
    pb = p.astype(BF16)
    h, saved = x, []
    for i, w in enumerate(layers):
        h, s = _layer_fwd(i, h, w, pb[i], tables)
        saved.append(s)
    loss, dh, dhb, d_final = _loss_head(h, tgt, final_norm, "loss_head")
    grads = [None] * len(layers)
    for i in reversed(range(len(layers))):
        dh, dhb, head = _layer_bwd_head(i, dh, dhb, saved[i], layers[i], pb[i])
        dh, dhb, tail = _layer_bwd_tail(i, dh, dhb, saved[i], layers[i], tables)
        grads[i] = {**head, **tail}
    return loss[0, 0], dh, grads, d_final


def _layer_fwd(i, h, w, pb, tables, deps=()):
    s = {}
    h, s["ffn1"] = _ffn_fwd(h, w["norm_ffn1"], w["ffn1_in"], w["ffn1_out"], f"l{i}_ffn1", deps=deps)
    if i % 2 == 0:
        h, s["mix"] = _hyb_fwd(h, w, f"l{i}_hyb")
    else:
        h, s["mix"] = _att_fwd(h, w, tables, f"l{i}_att")
    h, s["ffn2"] = _ffn_fwd(h, w["norm_ffn2"], w["ffn2_in"], w["ffn2_out"], f"l{i}_ffn2")
    h, s["ple"] = _ple_fwd(h, w["ple_norm"], pb, w["ple_gate"], w["ple_proj"], f"l{i}_ple")
    return h, s


def _layer_bwd_head(i, dh, dhb, s, w, pb, deps=()):
    g = {}
    dh, dhb, g["ple_norm"], g["ple_gate"], g["ple_proj"] = _ple_bwd(dh, dhb, s["ple"], w["ple_norm"], pb, w["ple_gate"],
                                                                    f"l{i}_ple", deps=deps)
    return dh, dhb, g


def _layer_bwd_tail(i, dh, dhb, s, w, tables, deps=()):
    g = {}
    dh, dhb, g["norm_ffn2"], g["ffn2_in"], g["ffn2_out"] = _ffn_bwd(dh, dhb, s["ffn2"], w["norm_ffn2"], w["ffn2_in"],
                                                                    w["ffn2_out"], f"l{i}_ffn2", deps=deps)
    if i % 2 == 0:
        dh, dhb, gm = _hyb_bwd(dh, dhb, s["mix"], w, f"l{i}_hyb")
    else:
        dh, dhb, gm = _att_bwd(dh, dhb, s["mix"], w, tables, f"l{i}_att")
    g.update(gm)
    dh, dhb, g["norm_ffn1"], g["ffn1_in"], g["ffn1_out"] = _ffn_bwd(dh, dhb, s["ffn1"], w["norm_ffn1"], w["ffn1_in"],
                                                                    w["ffn1_out"], f"l{i}_ffn1")
    return dh, dhb, g


def _cols(g):
    full = jnp.moveaxis(g, 0, -2)
    return full.reshape(*full.shape[:-2], N_DEV * g.shape[-1])


def _uncols(full):
    split = full.reshape(*full.shape[:-1], N_DEV, full.shape[-1] // N_DEV)
    return jnp.moveaxis(split, -2, 0)


def _lane_pad(v):
    return jnp.pad(v, ((0, 0), (0, LANE - v.shape[1])))


def _build_layers(gw, gs, rep):
    return [_build_layer(i, gw, gs, rep) for i in range(2)]


def _build_layer(i, gw, gs, rep):
    w = {}
    for f in ("ffn1", "ffn2"):
        w[f"norm_{f}"] = rep[f"norm_{f}"][i][None]
        w[f"{f}_in"] = gw[f"{f}_w_in", i]
        w[f"{f}_out"] = gw[f"{f}_w_out", i].reshape(4, FF_SHARD, D)
    w["norm_mix"] = rep["norm_mix"][i][None]
    w["ple_norm"] = rep["ple_norm"][i][None]
    w["ple_gate"] = gw["ple_gate_w", i].reshape(D, D)
    w["ple_proj"] = _cols(gw["ple_proj_w", i])
    if i == 0:
        w["hyb_in"] = jnp.pad(_cols(gw["hyb_w_in", 0]), ((0, 0), (0, HYB_PAD - HYB_IN)))
        w["hyb_out"] = gw["hyb_w_out", 0].reshape(2, D, D)
        w["conv_w"] = _cols(gs["conv_dw_w"][:, 0])
        w["sconv_w"] = _cols(gs["ssm_conv_w"][:, 0])
        w["conv_b"], w["ln_g"], w["ln_b"] = rep["conv_dw_b"], rep["conv_ln_g"], rep["conv_ln_b"]
        w["sconv_b"], w["ssm_norm"] = rep["ssm_conv_b"], rep["ssm_norm"]
        w["dt_bias"], w["a_log"], w["d_skip"] = (_lane_pad(rep[k]) for k in ("ssm_dt_bias", "ssm_a_log", "ssm_d"))
    else:
        w["qkv"] = _cols(gw["att_w_qkv", 0])
        w["w_o"] = gw["att_w_o", 0].reshape(D, D)
        w["b_qkv"] = gs["att_b_qkv"][:, 0].reshape(1, -1)
        w["b_o"] = gs["att_b_o"][:, 0].reshape(1, -1)
        w["sinks"] = _lane_pad(rep["att_sinks"])
    return w


def _big_grads(i, g):
    big = {}
    for f in ("ffn1", "ffn2"):
        big[f"{f}_w_in", i] = g[f"{f}_in"]
        big[f"{f}_w_out", i] = g[f"{f}_out"].reshape(N_DEV, D_FF // N_DEV, D)
    big["ple_gate_w", i] = g["ple_gate"].reshape(N_DEV, D // N_DEV, D)
    big["ple_proj_w", i] = _uncols(g["ple_proj"])
    if i == 0:
        big["hyb_w_in", 0] = _uncols(g["hyb_in"][:, :HYB_IN])
        big["hyb_w_out", 0] = g["hyb_out"].reshape(N_DEV, 2 * D // N_DEV, D)
    else:
        big["att_w_qkv", 0] = _uncols(g["qkv"])
        big["att_w_o", 0] = g["w_o"].reshape(N_DEV, D // N_DEV, D)
    return big


def _collect_grads(grads, d_final):
    g0, g1 = grads
    big, small = {**_big_grads(0, g0), **_big_grads(1, g1)}, {}
    for f in ("ffn1", "ffn2"):
        small[f"norm_{f}"] = jnp.concatenate([g[f"norm_{f}"] for g in grads], axis=0)
    small["norm_mix"] = jnp.concatenate([g["norm_mix"] for g in grads], axis=0)
    small["ple_norm"] = jnp.concatenate([g["ple_norm"] for g in grads], axis=0)
    small["conv_dw_w"] = g0["conv_w"][None]
    small["conv_dw_b"], small["conv_ln_g"], small["conv_ln_b"] = g0["conv_b"], g0["ln_g"], g0["ln_b"]
    small["ssm_conv_w"] = g0["sconv_w"][None]
    small["ssm_conv_b"], small["ssm_norm"] = g0["sconv_b"], g0["ssm_norm"]
    small["ssm_dt_bias"], small["ssm_a_log"], small["ssm_d"] = (g0[k][:, :SSM_HEADS] for k in ("dt_bias", "a_log", "d_skip"))
    small["att_b_qkv"], small["att_b_o"] = g1["b_qkv"], g1["b_o"]
    small["att_sinks"] = g1["sinks"][:, :SSM_HEADS]
    small["final_norm"] = d_final[0]
    return big, small


MESH = pl.DeviceIdType.MESH


def _place():
    return lax.axis_index("x"), lax.axis_index("y"), lax.axis_index("c")


def _all_gather(blocks, space, name):
    nb = len(blocks)

    def body(*refs):
        x_refs, out_refs, (send_sems, recv_sems, local_sem) = refs[:nb], refs[nb:2 * nb], refs[2 * nb:]
        x, y, c = _place()
        me, sibling = (x, y, c), (x, y, 1 - c)
        chips = [(1 - x, y), (x, 1 - y), (1 - x, 1 - y)]

        def copies(k, blk, to, own=False):
            idx = 4 * blk[0] + 2 * blk[1] + blk[2]
            return [pltpu.make_async_remote_copy(src_ref=x_ref if own else out_ref.at[idx], dst_ref=out_ref.at[idx],
                                                 send_sem=send_sems.at[k, b], recv_sem=recv_sems.at[k, b], device_id=to,
                                                 device_id_type=MESH) for b, (x_ref, out_ref) in enumerate(zip(x_refs, out_refs))]

        mine = [pltpu.make_async_copy(x_ref, out_ref.at[4 * x + 2 * y + c], local_sem.at[b])
                for b, (x_ref, out_ref) in enumerate(zip(x_refs, out_refs))]
        first = copies(0, me, sibling, own=True)
        for j, chip in enumerate(chips):
            first += copies(1 + j, me, (*chip, c), own=True)
        for cp in mine + first:
            cp.start()
        passed = []
        for j, chip in enumerate(chips):
            for cp in copies(1 + j, (*chip, c), me):
                cp.wait_recv()
            onward = copies(4 + j, (*chip, c), sibling)
            for cp in onward:
                cp.start()
            passed += onward
        for cp in copies(0, sibling, me):
            cp.wait_recv()
        for j, chip in enumerate(chips):
            for cp in copies(4 + j, (*chip, 1 - c), me):
                cp.wait_recv()
        for cp in first + passed:
            cp.wait_send()
        for cp in mine:
            cp.wait()

    spec = pl.BlockSpec(memory_space=space)
    return pl.pallas_call(
        body, name=name, out_shape=[SDS((N_DEV,) + b.shape, b.dtype) for b in blocks],
        in_specs=[spec] * nb, out_specs=[spec] * nb,
        scratch_shapes=[pltpu.SemaphoreType.DMA((7, nb)), pltpu.SemaphoreType.DMA((7, nb)), pltpu.SemaphoreType.DMA((nb,))],
    )(*blocks)


def _pair_exchange(parts, name):
    nb = len(parts)

    def body(*refs):
        p_refs, got_refs, (send_sems, recv_sems) = refs[:nb], refs[nb:2 * nb], refs[2 * nb:]
        x, y, c = _place()
        copies = [pltpu.make_async_remote_copy(src_ref=p_ref.at[2 * q + (1 - c)], dst_ref=got_ref.at[q],
                                               send_sem=send_sems.at[q, b], recv_sem=recv_sems.at[q, b], device_id=(x, y, 1 - c),
                                               device_id_type=MESH)
                  for q in range(4) for b, (p_ref, got_ref) in enumerate(zip(p_refs, got_refs))]
        for cp in copies:
            cp.start()
        for cp in copies:
            cp.wait_recv()
        for cp in copies:
            cp.wait_send()

    hbm = pl.BlockSpec(memory_space=pltpu.HBM)
    return pl.pallas_call(
        body, name=name, out_shape=[SDS((4,) + p.shape[1:], p.dtype) for p in parts], in_specs=[hbm] * nb, out_specs=[hbm] * nb,
        scratch_shapes=[pltpu.SemaphoreType.DMA((4, nb)), pltpu.SemaphoreType.DMA((4, nb))],
    )(*parts)


HBM_SPEC = pl.BlockSpec(memory_space=pltpu.HBM)
SEM_SPEC = pl.BlockSpec(memory_space=pltpu.SEMAPHORE)
EFFECT = pltpu.SideEffectType.DATAFLOW_SIDE_EFFECTING


def _plan_descriptors(plan, srcs, lands, send_sems, recv_sems, local_sems, arriving):
    remote, local = plan(*_place())
    pick = lambda ref, slot: ref if slot is None else ref.at[slot]
    rem = [pltpu.make_async_remote_copy(src_ref=pick(srcs[si], ss), dst_ref=lands[li].at[rs if arriving else ds],
                                        send_sem=send_sems.at[k], recv_sem=recv_sems.at[k], device_id=dev, device_id_type=MESH)
           for k, (si, ss, li, ds, dev, rs) in enumerate(remote)]
    loc = [pltpu.make_async_copy(pick(srcs[si], ss), lands[li].at[ds], local_sems.at[k])
           for k, (si, ss, li, ds) in enumerate(local)]
    return rem, loc


def _plan_counts(plan):
    remote, local = plan(0, 0, 0)
    return len(remote), max(len(local), 1)


def _exchange_start(srcs, land_shapes, plan, name):
    ns, nl = len(srcs), len(land_shapes)
    n_remote, n_local = _plan_counts(plan)
    lands = [pltpu.with_memory_space_constraint(lax.empty(s.shape, s.dtype), pltpu.HBM) for s in land_shapes]
    srcs = [pltpu.with_memory_space_constraint(s, pltpu.HBM) for s in srcs]

    def body(*refs):
        src_refs, land_refs = refs[:ns], refs[ns:ns + nl]
        send_sems, recv_sems, local_sems = refs[ns + nl:ns + nl + 3]
        token = refs[-1]
        rem, loc = _plan_descriptors(plan, src_refs, land_refs, send_sems, recv_sems, local_sems, arriving=False)
        for cp in loc + rem:
            cp.start()
        token[...] = jnp.zeros_like(token)

    outs = pl.pallas_call(
        body, name=name,
        out_shape=[pltpu.SemaphoreType.DMA((n_remote,)), pltpu.SemaphoreType.DMA((n_remote,)), pltpu.SemaphoreType.DMA((n_local,))]
        + [pltpu.HBM(a.shape, a.dtype) for a in srcs + lands] + [SDS((8, LANE), F32)],
        in_specs=[HBM_SPEC] * (ns + nl),
        out_specs=[SEM_SPEC] * 3 + [HBM_SPEC] * (ns + nl) + [pl.BlockSpec(memory_space=pltpu.VMEM)],
        input_output_aliases={i: 3 + i for i in range(ns + nl)},
        compiler_params=pltpu.CompilerParams(has_side_effects=EFFECT),
    )(*srcs, *lands)
    return (outs[:3], outs[3:3 + ns], outs[3 + ns:3 + ns + nl]), outs[-1]


def _exchange_wait(state, after, plan, name):
    sems, srcs, lands = state
    ns, nl = len(srcs), len(lands)

    def body(*refs):
        src_refs, land_refs = refs[:ns], refs[ns:ns + nl]
        send_sems, recv_sems, local_sems = refs[ns + nl:ns + nl + 3]
        rem, loc = _plan_descriptors(plan, src_refs, land_refs, send_sems, recv_sems, local_sems, arriving=True)
        for cp in rem:
            cp.wait_send()
            cp.wait_recv()
        for cp in loc:
            cp.wait()

    outs = pl.pallas_call(
        body, name=name, out_shape=[pltpu.HBM(a.shape, a.dtype) for a in list(srcs) + list(lands)],
        in_specs=[HBM_SPEC] * (ns + nl) + [SEM_SPEC] * 3 + [ANY_SPEC], out_specs=[HBM_SPEC] * (ns + nl),
        input_output_aliases={i: i for i in range(ns + nl)},
        compiler_params=pltpu.CompilerParams(has_side_effects=EFFECT),
    )(*srcs, *lands, *sems, after)
    return outs[:ns], outs[ns:]


def _gather_plan(nb):
    def plan(x, y, c):
        me = 4 * x + 2 * y + c
        remote = []
        for b in range(nb):
            for r in range(1, N_DEV):
                tx, ty, tc = (1 - x if r & 4 else x), (1 - y if r & 2 else y), (1 - c if r & 1 else c)
                remote.append((b, None, b, me, (tx, ty, tc), 4 * tx + 2 * ty + tc))
        return remote, [(b, None, b, me) for b in range(nb)]
    return plan


def _pair_plan(nb):
    def plan(x, y, c):
        return [(b, 2 * q + (1 - c), b, q, (x, y, 1 - c), q) for b in range(nb) for q in range(4)], []
    return plan


def _chip_plan(nb):
    def plan(x, y, c):
        own = 2 * x + y
        chips = [(1 - x, y), (x, 1 - y), (1 - x, 1 - y)]
        remote = [(b, 2 * cx + cy, b, own, (cx, cy, c), 2 * cx + cy) for b in range(nb) for cx, cy in chips]
        return remote, [(b, own, b, own) for b in range(nb)]
    return plan


def _row_tile(r, cap=4608):
    return max(d for d in range(16, min(r, cap) + 1, 16) if r % d == 0)


def _pair_add(parts, got, core, name):
    _, r, cdim = parts.shape
    tr = _row_tile(r)

    def body(core_ref, p_ref, g_ref, o_ref):
        o_ref[...] = (p_ref[...].astype(F32) + g_ref[...].astype(F32)).astype(o_ref.dtype)

    return pl.pallas_call(
        body, name=name, out_shape=SDS((4, r, cdim), BF16),
        grid_spec=pltpu.PrefetchScalarGridSpec(
            num_scalar_prefetch=1, grid=(4, r // tr),
            in_specs=[pl.BlockSpec((None, tr, cdim), lambda q, i, core_ref: (2 * q + core_ref[0], i, 0)),
                      pl.BlockSpec((None, tr, cdim), lambda q, i, core_ref: (q, i, 0))],
            out_specs=pl.BlockSpec((None, tr, cdim), lambda q, i, core_ref: (q, i, 0))),
        compiler_params=_params(("parallel", "parallel")),
    )(core, parts, got)


def _chip_exchange(sums, name):
    nb = len(sums)

    def body(*refs):
        b_refs, out_refs, (send_sems, recv_sems, local_sem) = refs[:nb], refs[nb:2 * nb], refs[2 * nb:]
        x, y, c = _place()
        own = 2 * x + y
        chips = [(1 - x, y), (x, 1 - y), (1 - x, 1 - y)]

        def copies(k, chip, src_slot, dst_slot):
            return [pltpu.make_async_remote_copy(src_ref=b_ref.at[src_slot], dst_ref=out_ref.at[dst_slot],
                                                 send_sem=send_sems.at[k, b], recv_sem=recv_sems.at[k, b], device_id=(*chip, c),
                                                 device_id_type=MESH) for b, (b_ref, out_ref) in enumerate(zip(b_refs, out_refs))]

        mine = [pltpu.make_async_copy(b_ref.at[own], out_ref.at[own], local_sem.at[b])
                for b, (b_ref, out_ref) in enumerate(zip(b_refs, out_refs))]
        sends = []
        for k, chip in enumerate(chips):
            sends += copies(k, chip, 2 * chip[0] + chip[1], own)
        for cp in mine + sends:
            cp.start()
        for k, chip in enumerate(chips):
            for cp in copies(k, chip, own, 2 * chip[0] + chip[1]):
                cp.wait_recv()
        for cp in sends:
            cp.wait_send()
        for cp in mine:
            cp.wait()

    hbm = pl.BlockSpec(memory_space=pltpu.HBM)
    return pl.pallas_call(
        body, name=name, out_shape=[SDS(s.shape, s.dtype) for s in sums], in_specs=[hbm] * nb, out_specs=[hbm] * nb,
        scratch_shapes=[pltpu.SemaphoreType.DMA((3, nb)), pltpu.SemaphoreType.DMA((3, nb)), pltpu.SemaphoreType.DMA((nb,))],
    )(*sums)


def _sum_slots(parts, name):
    nj, r, cdim = parts.shape
    tr = _row_tile(r)

    def body(p_ref, o_ref):
        acc = p_ref[0].astype(F32)
        for j in range(1, nj):
            acc = acc + p_ref[j].astype(F32)
        o_ref[...] = acc

    return pl.pallas_call(
        body, name=name, out_shape=SDS((r, cdim), F32), grid=(r // tr,),
        in_specs=[pl.BlockSpec((nj, tr, cdim), lambda i: (0, i, 0))], out_specs=pl.BlockSpec((tr, cdim), lambda i: (i, 0)),
        compiler_params=_params(("parallel",)),
    )(parts)


def _adamw_update(wv, gv, mv, vv):
    nm = ADAM_B1 * mv + (1.0 - ADAM_B1) * gv
    nv = ADAM_B2 * vv + (1.0 - ADAM_B2) * (gv * gv)
    m_hat = nm / (1.0 - ADAM_B1 ** ADAM_STEP)
    v_hat = nv / (1.0 - ADAM_B2 ** ADAM_STEP)
    return -ADAM_LR * (m_hat / (jnp.sqrt(v_hat) + ADAM_EPS) + ADAM_WD * wv), nm, nv


def _adamw_summed(w, m, v, by_chip, name):
    nl, r, cdim = w.shape
    tr = _row_tile(r, 512)
    nblk = r // tr

    def body(*refs):
        chip_refs, (w_ref, m_ref, v_ref, g_ref, d_ref, nm_ref, nv_ref) = refs[:nl], refs[nl:]
        layer = pl.program_id(0)
        gv = None
        for ll, c_ref in enumerate(chip_refs):
            s = c_ref[0].astype(F32)
            for q in range(1, 4):
                s = s + c_ref[q].astype(F32)
            gv = s if gv is None else jnp.where(layer == ll, s, gv)
        g_ref[...] = gv
        d_ref[...], nm_ref[...], nv_ref[...] = _adamw_update(w_ref[...], gv, m_ref[...], v_ref[...])

    def chip_map(ll):
        return lambda l, i: (0, jnp.where(l == ll, i, jnp.where(l > ll, nblk - 1, 0)), 0)

    spec = pl.BlockSpec((None, tr, cdim), lambda l, i: (l, i, 0))
    return pl.pallas_call(
        body, name=name, grid=(nl, nblk),
        in_specs=[pl.BlockSpec((4, tr, cdim), chip_map(ll)) for ll in range(nl)] + [spec] * 3,
        out_specs=[spec] * 4, out_shape=[SDS((nl, r, cdim), F32)] * 4,
        compiler_params=_params(("arbitrary", "arbitrary")),
    )(*by_chip, w, m, v)


def _adamw(w, g, m, v, name):
    shape = w.shape
    cdim = shape[-1]
    w2, g2, m2, v2 = (a.reshape(-1, cdim) for a in (w, g, m, v))
    r = w2.shape[0]
    tr = next(d for d in (512, 352, 256, 128, 64, 32, 16, 8, r) if r % d == 0)

    def body(w_ref, g_ref, m_ref, v_ref, d_ref, nm_ref, nv_ref):
        d_ref[...], nm_ref[...], nv_ref[...] = _adamw_update(w_ref[...], g_ref[...], m_ref[...], v_ref[...])

    spec = pl.BlockSpec((tr, cdim), lambda i: (i, 0))
    outs = pl.pallas_call(
        body, name=name, grid=(r // tr,), in_specs=[spec] * 4, out_specs=[spec] * 3, out_shape=[SDS((r, cdim), F32)] * 3,
        compiler_params=_params(("parallel",)),
    )(w2, g2, m2, v2)
    return tuple(o.reshape(shape) for o in outs)


WEIGHTS = ("norm_ffn1", "ffn1_w_in", "ffn1_w_out", "norm_mix", "norm_ffn2", "ffn2_w_in", "ffn2_w_out", "ple_norm", "ple_gate_w",
           "ple_proj_w", "hyb_w_in", "conv_dw_w", "conv_dw_b", "conv_ln_g", "conv_ln_b", "ssm_conv_w", "ssm_conv_b", "ssm_dt_bias",
           "ssm_a_log", "ssm_d", "ssm_norm", "hyb_w_out", "att_w_qkv", "att_b_qkv", "att_sinks", "att_w_o", "att_b_o", "final_norm")
BIG = ("ffn1_w_in", "ffn1_w_out", "ffn2_w_in", "ffn2_w_out", "ple_gate_w", "ple_proj_w", "hyb_w_in", "hyb_w_out", "att_w_qkv",
       "att_w_o")
SMALL_SHARDED = {"conv_dw_w": 2, "ssm_conv_w": 2, "att_b_qkv": 1, "att_b_o": 1}
SMALL = tuple(n for n in WEIGHTS if n not in BIG)
PACK_ROWS = 16


def _pack(arrays, lead=0):
    pieces = []
    for a in arrays:
        flat = a.reshape(*a.shape[:lead], -1)
        size = flat.shape[-1]
        padded = -(-size // (PACK_ROWS * LANE)) * PACK_ROWS * LANE
        flat = jnp.pad(flat, [(0, 0)] * lead + [(0, padded - size)])
        pieces.append(flat.reshape(*a.shape[:lead], padded // LANE, LANE))
    return jnp.concatenate(pieces, axis=lead)


def _unpack(buf, shapes, lead=0):
    out, row = [], 0
    for shape in shapes:
        size = math.prod(shape)
        rows = -(-size // (PACK_ROWS * LANE)) * PACK_ROWS
        piece = lax.slice_in_dim(buf, row, row + rows, axis=lead)
        piece = piece.reshape(*buf.shape[:lead], rows * LANE)
        out.append(lax.slice_in_dim(piece, 0, size, axis=lead).reshape(*buf.shape[:lead], *shape))
        row += rows
    return out


def kernel(x, p, norm_ffn1, ffn1_w_in, ffn1_w_out, norm_mix, norm_ffn2, ffn2_w_in, ffn2_w_out, ple_norm, ple_gate_w, ple_proj_w, hyb_w_in, conv_dw_w, conv_dw_b, conv_ln_g, conv_ln_b, ssm_conv_w, ssm_conv_b, ssm_dt_bias, ssm_a_log, ssm_d, ssm_norm, hyb_w_out, att_w_qkv, att_b_qkv, att_sinks, att_w_o, att_b_o, final_norm, loss_target, m_norm_ffn1, m_ffn1_w_in, m_ffn1_w_out, m_norm_mix, m_norm_ffn2, m_ffn2_w_in, m_ffn2_w_out, m_ple_norm, m_ple_gate_w, m_ple_proj_w, m_hyb_w_in, m_conv_dw_w, m_conv_dw_b, m_conv_ln_g, m_conv_ln_b, m_ssm_conv_w, m_ssm_conv_b, m_ssm_dt_bias, m_ssm_a_log, m_ssm_d, m_ssm_norm, m_hyb_w_out, m_att_w_qkv, m_att_b_qkv, m_att_sinks, m_att_w_o, m_att_b_o, m_final_norm, v_norm_ffn1, v_ffn1_w_in, v_ffn1_w_out, v_norm_mix, v_norm_ffn2, v_ffn2_w_in, v_ffn2_w_out, v_ple_norm, v_ple_gate_w, v_ple_proj_w, v_hyb_w_in, v_conv_dw_w, v_conv_dw_b, v_conv_ln_g, v_conv_ln_b, v_ssm_conv_w, v_ssm_conv_b, v_ssm_dt_bias, v_ssm_a_log, v_ssm_d, v_ssm_norm, v_hyb_w_out, v_att_w_qkv, v_att_b_qkv, v_att_sinks, v_att_w_o, v_att_b_o, v_final_norm):
    args = (norm_ffn1, ffn1_w_in, ffn1_w_out, norm_mix, norm_ffn2, ffn2_w_in, ffn2_w_out, ple_norm, ple_gate_w, ple_proj_w, hyb_w_in, conv_dw_w, conv_dw_b, conv_ln_g, conv_ln_b, ssm_conv_w, ssm_conv_b, ssm_dt_bias, ssm_a_log, ssm_d, ssm_norm, hyb_w_out, att_w_qkv, att_b_qkv, att_sinks, att_w_o, att_b_o, final_norm)
    moments_m = (m_norm_ffn1, m_ffn1_w_in, m_ffn1_w_out, m_norm_mix, m_norm_ffn2, m_ffn2_w_in, m_ffn2_w_out, m_ple_norm, m_ple_gate_w, m_ple_proj_w, m_hyb_w_in, m_conv_dw_w, m_conv_dw_b, m_conv_ln_g, m_conv_ln_b, m_ssm_conv_w, m_ssm_conv_b, m_ssm_dt_bias, m_ssm_a_log, m_ssm_d, m_ssm_norm, m_hyb_w_out, m_att_w_qkv, m_att_b_qkv, m_att_sinks, m_att_w_o, m_att_b_o, m_final_norm)
    moments_v = (v_norm_ffn1, v_ffn1_w_in, v_ffn1_w_out, v_norm_mix, v_norm_ffn2, v_ffn2_w_in, v_ffn2_w_out, v_ple_norm, v_ple_gate_w, v_ple_proj_w, v_hyb_w_in, v_conv_dw_w, v_conv_dw_b, v_conv_ln_g, v_conv_ln_b, v_ssm_conv_w, v_ssm_conv_b, v_ssm_dt_bias, v_ssm_a_log, v_ssm_d, v_ssm_norm, v_hyb_w_out, v_att_w_qkv, v_att_b_qkv, v_att_sinks, v_att_w_o, v_att_b_o, v_final_norm)
    w = dict(zip(WEIGHTS, args))
    m = dict(zip(WEIGHTS, moments_m))
    v = dict(zip(WEIGHTS, moments_v))
    cx, cy, cc = _place()
    me = 4 * cx + 2 * cy + cc

    core = jnp.reshape(cc, (1,)).astype(jnp.int32)
    layer_of = lambda n, i: 1 if n.startswith("att_") else i
    keys = [[(n, i) for n in BIG for i in range(w[n].shape[0]) if layer_of(n, i) == layer] for layer in range(2)]

    gw = dict(zip(keys[0], _all_gather([w[n][i].astype(BF16) for n, i in keys[0]], pltpu.HBM, "gather_weights_l0")))
    blocks1 = [w[n][i].astype(BF16) for n, i in keys[1]]
    gather1, gather1_token = _exchange_start(blocks1, [SDS((N_DEV,) + b.shape, BF16) for b in blocks1],
                                             _gather_plan(len(blocks1)), "gather_weights_l1_start")
    ss_shapes = [w[n].shape for n in SMALL_SHARDED]
    gathered_small = _all_gather([_pack([w[n] for n in SMALL_SHARDED])], pltpu.VMEM, "gather_small_weights")[0]
    gs = dict(zip(SMALL_SHARDED, _unpack(gathered_small, ss_shapes, lead=1)))
    rep = {n: w[n] for n in SMALL if n not in SMALL_SHARDED}

    tables = _rope_tables(x.shape[1])
    pb = p[:, 0].astype(BF16)
    layer0 = _build_layer(0, gw, gs, rep)
    h, saved0 = _layer_fwd(0, x[0], layer0, pb[0], tables, deps=(gather1_token,))
    gw.update(zip(keys[1], _exchange_wait(gather1, h, _gather_plan(len(blocks1)), "gather_weights_l1_wait")[1]))
    layer1 = _build_layer(1, gw, gs, rep)
    h, saved1 = _layer_fwd(1, h, layer1, pb[1], tables)
    loss, dh, dhb, d_final = _loss_head(h, loss_target[0], final_norm[None], "loss_head")
    loss = lax.psum(loss[0, 0], ("x", "y", "c"))

    dh, dhb, head1 = _layer_bwd_head(1, dh, dhb, saved1, layer1, pb[1])
    dh, dhb, tail1 = _layer_bwd_tail(1, dh, dhb, saved1, layer1, tables)
    grads1 = {**head1, **tail1}
    big1 = _big_grads(1, grads1)
    nb1 = len(keys[1])
    pair1, pair1_token = _exchange_start([big1[key] for key in keys[1]], [SDS((4,) + big1[key].shape[1:], BF16) for key in keys[1]],
                                         _pair_plan(nb1), "grads_l1_pair_start")
    dh, dhb, head0 = _layer_bwd_head(0, dh, dhb, saved0, layer0, pb[0], deps=(pair1_token,))
    parts1, got1 = _exchange_wait(pair1, dh, _pair_plan(nb1), "grads_l1_pair_wait")
    sums1 = [_pair_add(pt, gt, core, f"grads_pair_add_{n}_{i}") for pt, gt, (n, i) in zip(parts1, got1, keys[1])]
    chip1, chip1_token = _exchange_start(sums1, [SDS(s.shape, BF16) for s in sums1], _chip_plan(nb1), "grads_l1_chip_start")
    dx, dhb, tail0 = _layer_bwd_tail(0, dh, dhb, saved0, layer0, tables, deps=(chip1_token,))
    by_chip = dict(zip(keys[1], _exchange_wait(chip1, dx, _chip_plan(nb1), "grads_l1_chip_wait")[1]))
    grads0 = {**head0, **tail0}
    big0 = _big_grads(0, grads0)
    parts0 = [big0[key] for key in keys[0]]
    got0 = _pair_exchange(parts0, "grads_l0_pair_exchange")
    sums0 = [_pair_add(pt, gt, core, f"grads_pair_add_{n}_{i}") for pt, gt, (n, i) in zip(parts0, got0, keys[0])]
    by_chip.update(zip(keys[0], _chip_exchange(sums0, "grads_l0_chip_exchange")))
    _, small = _collect_grads([grads0, grads1], d_final)
    small_shapes = [small[n].shape for n in SMALL]
    all_small = _all_gather([_pack([small[n] for n in SMALL])], pltpu.VMEM, "gather_small_grads")[0]
    g = dict(zip(SMALL, _unpack(_sum_slots(all_small, "small_grads_sum"), small_shapes)))
    for n, axis in SMALL_SHARDED.items():
        g[n] = lax.dynamic_slice_in_dim(g[n], me * w[n].shape[axis], w[n].shape[axis], axis=axis)

    delta, new_m, new_v = {}, {}, {}
    for n in BIG:
        g[n], delta[n], new_m[n], new_v[n] = _adamw_summed(w[n], m[n], v[n], [by_chip[n, i] for i in range(w[n].shape[0])],
                                                           f"adamw_{n}")
    packed = [_pack([d[n] for n in SMALL]) for d in (w, g, m, v)]
    shapes = [w[n].shape for n in SMALL]
    for d, buf in zip((delta, new_m, new_v), _adamw(*packed, "adamw_small")):
        d.update(zip(SMALL, _unpack(buf, shapes)))
    return (loss, dx[None], *[g[n] for n in WEIGHTS], *[delta[n] for n in WEIGHTS], *[new_m[n] for n in WEIGHTS],
            *[new_v[n] for n in WEIGHTS])
```

```python
import functools
import math

import numpy as np
import jax
import jax.numpy as jnp
from jax import lax
from jax.experimental import pallas as pl
from jax.experimental.pallas import tpu as pltpu

F32, BF16 = jnp.float32, jnp.bfloat16
HI = lax.Precision.HIGHEST
SDS = jax.ShapeDtypeStruct

N_DEV = 8
D = 1024
D_FF = 2816
FF_SHARD = 2 * D_FF // N_DEV
PLE_DIM = 256
EPS = 1e-6
CONV_W = 31
SSM_CONV = 4
SSM_HEADS = 16
SSM_XBC = 1536
CHUNK = 128
HYB_IN = 4624
HYB_PAD = 5120
DT_COL = 4608
N_PAIR = 8
ROPE_THETA = 10000.0
LANE = 128
VMEM_LIMIT = 56 * 1024 * 1024

ADAM_LR, ADAM_B1, ADAM_B2, ADAM_EPS, ADAM_WD, ADAM_STEP = 0.001, 0.9, 0.999, 1e-08, 0.01, 10


def _params(sem):
    return pltpu.CompilerParams(dimension_semantics=sem, vmem_limit_bytes=VMEM_LIMIT)


def _mm(a, b, *, ta=False, tb=False, reduce_j=False, out_dtypes=(F32,), tm=1024, tn=1024, tk=1024,
        epi=None, extras=(), rows=(), deps=(), sums=0, name):
    ja, jb = a.shape[0], b.shape[0]
    nj = max(ja, jb)
    jo = 1 if reduce_j else nj
    m, k = (a.shape[2], a.shape[1]) if ta else (a.shape[1], a.shape[2])
    n = b.shape[1] if tb else b.shape[2]
    assert (b.shape[2] if tb else b.shape[1]) == k and ja in (1, nj) and jb in (1, nj)
    tm, tn, tk = min(tm, m), min(tn, n), min(tk, k)
    assert m % tm == 0 and n % tn == 0 and k % tk == 0, (name, m, n, k, tm, tn, tk)
    assert not sums or (tn == n and (reduce_j or nj == 1))
    nk = k // tk
    steps = nk * (nj if reduce_j else 1)
    ne, nr, no = len(extras), len(rows), len(out_dtypes)

    def a_map(i, c, j, kk):
        return (j if ja > 1 else 0, kk, i) if ta else (j if ja > 1 else 0, i, kk)

    def b_map(i, c, j, kk):
        return (j if jb > 1 else 0, c, kk) if tb else (j if jb > 1 else 0, kk, c)

    def o_map(i, c, j, kk):
        return (0 if reduce_j else j, i, c)

    dims = (((0 if ta else 1,), (1 if tb else 0,)), ((), ()))

    def body(a_ref, b_ref, *rest):
        ex, rw = rest[:ne], rest[ne:ne + nr]
        outs = rest[ne + nr + len(deps):ne + nr + len(deps) + no]
        sum_refs = rest[ne + nr + len(deps) + no:ne + nr + len(deps) + no + sums]
        part = lax.dot_general(a_ref[...], b_ref[...], dims, preferred_element_type=F32)
        first_tile = pl.program_id(0) == 0

        def finish(acc):
            res = epi(acc, *[e[...] for e in ex], *[r[...] for r in rw]) if epi else (acc,)
            for o, r in zip(outs, res):
                o[...] = r.astype(o.dtype)
            for s_ref, r in zip(sum_refs, res[no:]):
                @pl.when(first_tile)
                def _(s_ref=s_ref, r=r):
                    s_ref[...] = r

                @pl.when(jnp.logical_not(first_tile))
                def _(s_ref=s_ref, r=r):
                    s_ref[...] += r

        if steps == 1:
            finish(part)
            return
        acc_ref = rest[-1]
        kk = pl.program_id(3)
        step = pl.program_id(2) * nk + kk if reduce_j else kk

        @pl.when(step == 0)
        def _():
            acc_ref[...] = part

        @pl.when(step > 0)
        def _():
            acc_ref[...] += part

        @pl.when(step == steps - 1)
        def _():
            finish(acc_ref[...])

    o_spec = pl.BlockSpec((None, tm, tn), o_map)
    row_spec = pl.BlockSpec((1, tn), lambda i, c, j, kk: (0, c))
    return pl.pallas_call(
        body, name=name, grid=(m // tm, n // tn, nj, nk),
        in_specs=[pl.BlockSpec((None, tk, tm) if ta else (None, tm, tk), a_map),
                  pl.BlockSpec((None, tn, tk) if tb else (None, tk, tn), b_map)]
        + [o_spec] * ne + [row_spec] * nr + [ANY_SPEC] * len(deps),
        out_specs=[o_spec] * no + [row_spec] * sums,
        out_shape=[SDS((jo, m, n), dt) for dt in out_dtypes] + [SDS((1, n), F32)] * sums,
        scratch_shapes=[pltpu.VMEM((tm, tn), F32)] if steps > 1 else [],
        compiler_params=_params(("arbitrary" if sums else "parallel", "parallel", "arbitrary", "arbitrary")),
    )(a, b, *extras, *rows, *deps)


def _whole(p):
    return pl.BlockSpec(p.shape, lambda *_: (0,) * p.ndim)


ANY_SPEC = pl.BlockSpec(memory_space=pl.ANY)


def _rowop(fn, tiles, params, outs, *, grid, name, deps=()):
    nin = len(tiles) + len(params)

    def body(*refs):
        res = fn(*[r[...].astype(F32) for r in refs[:nin]])
        for r, o in zip(refs[nin + len(deps):], res):
            r[...] = o.astype(r.dtype)

    return pl.pallas_call(
        body, name=name, grid=grid,
        in_specs=[s for _, s in tiles] + [_whole(p) for p in params] + [ANY_SPEC] * len(deps),
        out_specs=[s for _, _, s in outs], out_shape=[SDS(sh, dt) for sh, dt, _ in outs],
        compiler_params=_params(("parallel",) * len(grid)),
    )(*[t for t, _ in tiles], *params, *deps)


def _rowop_bwd(fn, tiles, params, cots, wrt, gouts, *, grid, name, adds=(), deps=()):
    nt, npar, nc, na = len(tiles), len(params), len(cots), len(adds)
    nin = nt + npar
    flat = [i for grp in wrt for i in grp]
    n_gout = sum(len(dts) for _, dts, _ in gouts)

    def body(*refs):
        vals = [r[...].astype(F32) for r in refs[:nin]]
        cvals = [r[...].astype(F32) for r in refs[nin:nin + nc]]
        avals = [r[...].astype(F32) for r in refs[nin + nc:nin + nc + na]]
        orefs = refs[nin + nc + na + len(deps):]
        diff_idx = flat + list(range(nt, nin))

        def f(*dv):
            full = list(vals)
            for i, v in zip(diff_idx, dv):
                full[i] = v
            return fn(*full)

        _, vjp = jax.vjp(f, *[vals[i] for i in diff_idx])
        grads = vjp(tuple(cvals))
        tile_g, par_g = list(grads[:len(flat)]), grads[len(flat):]
        group_g, at = [], 0
        for grp in wrt:
            members = tile_g[at:at + len(grp)]
            at += len(grp)
            group_g.append(members[0] if len(grp) == 1 else jnp.stack(members, axis=0))
        for av in avals:
            group_g[0] = group_g[0] + av
        o = 0
        for g, (_, dts, _) in zip(group_g, gouts):
            for _ in dts:
                orefs[o][...] = g.astype(orefs[o].dtype)
                o += 1
        first = functools.reduce(jnp.logical_and, [pl.program_id(ax) == 0 for ax in range(len(grid))])
        for r, g in zip(orefs[n_gout:], par_g):
            @pl.when(first)
            def _(r=r, g=g):
                r[...] = g

            @pl.when(jnp.logical_not(first))
            def _(r=r, g=g):
                r[...] += g

    out_specs, out_shape = [], []
    for sh, dts, spec in gouts:
        for dt in dts:
            out_specs.append(spec)
            out_shape.append(SDS(sh, dt))
    for p in params:
        out_specs.append(_whole(p))
        out_shape.append(SDS(p.shape, F32))
    return pl.pallas_call(
        body, name=name, grid=grid,
        in_specs=[s for _, s in tiles] + [_whole(p) for p in params] + [s for _, s in cots] + [s for _, s in adds]
        + [ANY_SPEC] * len(deps),
        out_specs=out_specs, out_shape=out_shape,
        compiler_params=_params(("arbitrary",) * len(grid)),
    )(*[t for t, _ in tiles], *params, *[c for c, _ in cots], *[a for a, _ in adds], *deps)


def _tok(c, tm, col=0):
    return pl.BlockSpec((tm, c), lambda i, col=col: (i, col))


def _rms_fn(h, g):
    return (h * lax.rsqrt(jnp.mean(h * h, axis=-1, keepdims=True) + EPS) * g,)


def _lnswish_fn(u, g, b):
    mu = jnp.mean(u, axis=-1, keepdims=True)
    xc = u - mu
    y = xc * lax.rsqrt(jnp.mean(xc * xc, axis=-1, keepdims=True) + EPS) * g + b
    return (y * jax.nn.sigmoid(y),)


def _ple_fn(z, e):
    return (jax.nn.sigmoid(z) * e,)


def _rms(h, g, name, tm=512, deps=()):
    t = h.shape[0]
    return _rowop(_rms_fn, [(h, _tok(D, tm))], [g], [((t, D), BF16, _tok(D, tm))], grid=(t // tm,), name=name, deps=deps)[0]


def _drms_epi(dn, h, dres, g):
    _, vjp = jax.vjp(_rms_fn, h, g)
    dh, dg = vjp((dn,))
    dh = dh + dres
    return dh, dh, dg


def _mm_drms(a, b, h, g, dres, name, tk):
    dh, dhb, dg = _mm(a, b, tb=True, reduce_j=a.shape[0] > 1, tm=512, tk=tk, epi=_drms_epi, extras=(h[None], dres[None]),
                      rows=(g,), out_dtypes=(F32, BF16), sums=1, name=name)
    return dh[0], dhb[0], dg


def _conv_geometry(width):
    pad = 32 if width > 8 else 8
    return pad, pad - (width - 1)


def _fill_shifts(xpad_ref, sh_ref, t, shifts):
    for r in shifts:
        sh_ref[r, :, :] = xpad_ref[pl.ds(r, t + 32), :]


def _dwconv(xs, w, b, *, width, glu, silu, cb, name):
    t = xs[0][0].shape[0]
    c = w.shape[1]
    pad, off = _conv_geometry(width)
    shifts = sorted({(k + off) % 8 for k in range(width)})
    ch = 32

    def body(*refs):
        x_refs, (w_ref, b_ref, o_ref, xpad_ref, sh_ref) = refs[:len(xs)], refs[len(xs):]
        u = x_refs[0][...] * jax.nn.sigmoid(x_refs[1][...]) if glu else x_refs[0][...]
        xpad_ref[pl.ds(0, pad), :] = jnp.zeros((pad, cb), F32)
        xpad_ref[pl.ds(pad, t), :] = u
        xpad_ref[pl.ds(pad + t, 40 - pad), :] = jnp.zeros((40 - pad, cb), F32)
        _fill_shifts(xpad_ref, sh_ref, t, shifts)

        def chunk(i, carry):
            t0 = pl.multiple_of(i * ch, ch)
            acc = jnp.broadcast_to(b_ref[...], (ch, cb))
            for k in range(width):
                q, r = divmod(k + off, 8)
                acc = acc + w_ref[pl.ds(k, 1), :] * sh_ref[r, pl.ds(t0 + 8 * q, ch), :]
            o_ref[pl.ds(t0, ch), :] = acc * jax.nn.sigmoid(acc) if silu else acc
            return carry

        lax.fori_loop(0, t // ch, chunk, 0)

    return pl.pallas_call(
        body, name=name, grid=(c // cb,),
        in_specs=[pl.BlockSpec((t, cb), lambda i, o=o: (0, o + i)) for _, o in xs]
        + [pl.BlockSpec((width, cb), lambda i: (0, i)), pl.BlockSpec((1, cb), lambda i: (0, i))],
        out_specs=pl.BlockSpec((t, cb), lambda i: (0, i)), out_shape=SDS((t, c), F32),
        scratch_shapes=[pltpu.VMEM((t + 40, cb), F32), pltpu.VMEM((8, t + 32, cb), F32)],
        compiler_params=_params(("parallel",)),
    )(*[x for x, _ in xs], w, b)


def _dwconv_bwd(xs, w, b, dy, *, width, glu, silu, cb, name):
    t = xs[0][0].shape[0]
    c = w.shape[1]
    pad, off = _conv_geometry(width)
    shifts = sorted({(k + off) % 8 for k in range(width)})
    shifts_t = sorted({mm % 8 for mm in range(width)})
    ch = 32
    nx = len(xs)

    def body(*refs):
        x_refs = refs[:nx]
        w_ref, b_ref, dy_ref = refs[nx:nx + 3]
        dx_refs = refs[nx + 3:nx + 3 + nx]
        dw_ref, db_ref, xpad_ref, sh_ref, dc_ref = refs[nx + 3 + nx:]
        u = x_refs[0][...] * jax.nn.sigmoid(x_refs[1][...]) if glu else x_refs[0][...]
        xpad_ref[pl.ds(0, pad), :] = jnp.zeros((pad, cb), F32)
        xpad_ref[pl.ds(pad, t), :] = u
        xpad_ref[pl.ds(pad + t, 40 - pad), :] = jnp.zeros((40 - pad, cb), F32)
        _fill_shifts(xpad_ref, sh_ref, t, shifts)

        if silu:
            def act_chunk(i, carry):
                t0 = pl.multiple_of(i * ch, ch)
                acc = jnp.broadcast_to(b_ref[...], (ch, cb))
                for k in range(width):
                    q, r = divmod(k + off, 8)
                    acc = acc + w_ref[pl.ds(k, 1), :] * sh_ref[r, pl.ds(t0 + 8 * q, ch), :]
                sg = jax.nn.sigmoid(acc)
                dc_ref[pl.ds(t0, ch), :] = dy_ref[pl.ds(t0, ch), :] * (sg * (1.0 + acc * (1.0 - sg)))
                return carry

            lax.fori_loop(0, t // ch, act_chunk, 0)
        else:
            dc_ref[...] = dy_ref[...]

        def dw_chunk(i, accs):
            t0 = pl.multiple_of(i * 8, 8)
            d = dc_ref[pl.ds(t0, 8), :]
            new = []
            for k in range(width):
                q, r = divmod(k + off, 8)
                new.append(accs[k] + d * sh_ref[r, pl.ds(t0 + 8 * q, 8), :])
            new.append(accs[width] + d)
            return tuple(new)

        accs = lax.fori_loop(0, t // 8, dw_chunk, tuple(jnp.zeros((8, cb), F32) for _ in range(width + 1)))
        for k in range(width):
            dw_ref[pl.ds(k, 1), :] = jnp.sum(accs[k], axis=0, keepdims=True)
        db_ref[...] = jnp.sum(accs[width], axis=0, keepdims=True)

        xpad_ref[pl.ds(0, t), :] = dc_ref[...]
        xpad_ref[pl.ds(t, 40), :] = jnp.zeros((40, cb), F32)
        _fill_shifts(xpad_ref, sh_ref, t, shifts_t)

        def dx_chunk(i, carry):
            t0 = pl.multiple_of(i * ch, ch)
            acc = jnp.zeros((ch, cb), F32)
            for mm in range(width):
                q, r = divmod(mm, 8)
                acc = acc + w_ref[pl.ds(width - 1 - mm, 1), :] * sh_ref[r, pl.ds(t0 + 8 * q, ch), :]
            if glu:
                val, gate = x_refs[0][pl.ds(t0, ch), :], x_refs[1][pl.ds(t0, ch), :]
                sg = jax.nn.sigmoid(gate)
                dx_refs[0][pl.ds(t0, ch), :] = (acc * sg).astype(BF16)
                dx_refs[1][pl.ds(t0, ch), :] = (acc * val * sg * (1.0 - sg)).astype(BF16)
            else:
                dx_refs[0][pl.ds(t0, ch), :] = acc.astype(BF16)
            return carry

        lax.fori_loop(0, t // ch, dx_chunk, 0)

    col = pl.BlockSpec((t, cb), lambda i: (0, i))
    return pl.pallas_call(
        body, name=name, grid=(c // cb,),
        in_specs=[pl.BlockSpec((t, cb), lambda i, o=o: (0, o + i)) for _, o in xs]
        + [pl.BlockSpec((width, cb), lambda i: (0, i)), pl.BlockSpec((1, cb), lambda i: (0, i)), col],
        out_specs=[col] * nx + [pl.BlockSpec((width, cb), lambda i: (0, i)), pl.BlockSpec((1, cb), lambda i: (0, i))],
        out_shape=[SDS((t, c), BF16)] * nx + [SDS((width, c), F32), SDS((1, c), F32)],
        scratch_shapes=[pltpu.VMEM((t + 40, cb), F32), pltpu.VMEM((8, t + 32, cb), F32), pltpu.VMEM((t, cb), F32)],
        compiler_params=_params(("parallel",)),
    )(*[x for x, _ in xs], w, b, dy)


_DIMS = {"nn": (((1,), (0,)), ((), ())), "nt": (((1,), (1,)), ((), ())), "tn": (((0,), (0,)), ((), ()))}


def _raw_dot(a, b, mode):
    return lax.dot_general(a.astype(BF16), b.astype(BF16), _DIMS[mode], preferred_element_type=F32)


@functools.partial(jax.custom_vjp, nondiff_argnums=(2,))
def _bdot(a, b, mode):
    return _raw_dot(a, b, mode)


def _bdot_fwd(a, b, mode):
    return _raw_dot(a, b, mode), (a, b)


def _bdot_bwd(mode, res, g):
    a, b = res
    if mode == "nn":
        return _raw_dot(g, b, "nt"), _raw_dot(a, g, "tn")
    if mode == "nt":
        return _raw_dot(g, b, "nn"), _raw_dot(g, a, "tn")
    return _raw_dot(b, g, "nt"), _raw_dot(a, g, "nn")


_bdot.defvjp(_bdot_fwd, _bdot_bwd)


def _iota(shape, axis):
    return lax.broadcasted_iota(jnp.int32, shape, axis)


def _half_masks():
    left = (_iota((1, LANE), 1) < 64).astype(F32)
    return left, 1.0 - left


def _ssd_chunk(state, xa, dtr, z, dtb, alog, dsk, ng):
    xs, bm, cm = xa[:, :D], xa[:, D:D + 256], xa[:, D + 256:]
    left, right = _half_masks()
    expand = (_iota((LANE, D), 1) // 64 == _iota((LANE, D), 0)).astype(F32)
    li, si = _iota((CHUNK, CHUNK), 0), _iota((CHUNK, CHUNK), 1)
    tril = li >= si
    dt16 = jax.nn.softplus(dtr + dtb)
    adt = dt16 * (-jnp.exp(alog))
    dtf = jnp.dot(dt16, expand, precision=HI)
    cs16 = jnp.dot(tril.astype(F32), adt, precision=HI)
    csf = jnp.dot(cs16, expand, precision=HI)
    totf = jnp.sum(jnp.dot(adt, expand, precision=HI), axis=0, keepdims=True)
    cst = cs16.T
    xdt = xs * dtf
    ys, new_state = [], []
    for g in range(2):
        bg, cg = bm[:, LANE * g:LANE * (g + 1)], cm[:, LANE * g:LANE * (g + 1)]
        cb = _bdot(cg, bg, "nt")
        for q in range(4):
            pr = 4 * g + q
            decay = []
            for h in (2 * pr, 2 * pr + 1):
                col = jnp.sum(jnp.where(si == h, cs16, 0.0), axis=1, keepdims=True)
                row = jnp.sum(jnp.where(li == h, cst, 0.0), axis=0, keepdims=True)
                decay.append(cb * jnp.exp(jnp.where(tril, col - row, -jnp.inf)))
            xp = xdt[:, LANE * pr:LANE * (pr + 1)]
            y_diag = _bdot(jnp.concatenate(decay, axis=1), jnp.concatenate([xp * left, xp * right], axis=0), "nn")
            csb, tot = csf[:, LANE * pr:LANE * (pr + 1)], totf[:, LANE * pr:LANE * (pr + 1)]
            ys.append(y_diag + _bdot(cg, state[pr], "nn") * jnp.exp(csb))
            new_state.append(state[pr] * jnp.exp(tot) + _bdot(bg, xp * jnp.exp(tot - csb), "tn"))
    y = jnp.concatenate(ys, axis=1)
    y = y + jnp.dot(jnp.broadcast_to(dsk, (CHUNK, LANE)), expand, precision=HI) * xs
    y = y * (z * jax.nn.sigmoid(z))
    halves = []
    for g in range(2):
        yg = y[:, 512 * g:512 * (g + 1)]
        halves.append(yg * lax.rsqrt(jnp.mean(yg * yg, axis=-1, keepdims=True) + EPS))
    return jnp.concatenate(halves, axis=1) * ng, jnp.stack(new_state, axis=0)


def _ssd_specs(t, rev):
    nc = t // CHUNK
    ix = (lambda c: nc - 1 - c) if rev else (lambda c: c)
    return nc, ix


def _ssd_fwd(xa, proj, dtb, alog, dsk, ng, name):
    t = xa.shape[0]
    nc, ix = _ssd_specs(t, False)

    def body(xa_ref, dt_ref, z_ref, dtb_ref, alog_ref, dsk_ref, ng_ref, y_ref, st_ref, carry_ref):
        @pl.when(pl.program_id(0) == 0)
        def _():
            carry_ref[...] = jnp.zeros_like(carry_ref)

        st_ref[...] = carry_ref[...]
        y, new = _ssd_chunk(carry_ref[...], xa_ref[...], dt_ref[...], z_ref[...], dtb_ref[...], alog_ref[...],
                            dsk_ref[...], ng_ref[...])
        y_ref[...] = y.astype(BF16)
        carry_ref[...] = new

    small = [dtb, alog, dsk, ng]
    return pl.pallas_call(
        body, name=name, grid=(nc,),
        in_specs=[pl.BlockSpec((CHUNK, SSM_XBC), lambda c: (c, 0)),
                  pl.BlockSpec((CHUNK, LANE), lambda c: (c, DT_COL // LANE)),
                  pl.BlockSpec((CHUNK, D), lambda c: (c, 2))] + [_whole(p) for p in small],
        out_specs=[pl.BlockSpec((CHUNK, D), lambda c: (c, 0)), pl.BlockSpec((None, N_PAIR, LANE, LANE), lambda c: (c, 0, 0, 0))],
        out_shape=[SDS((t, D), BF16), SDS((nc, N_PAIR, LANE, LANE), F32)],
        scratch_shapes=[pltpu.VMEM((N_PAIR, LANE, LANE), F32)],
        compiler_params=_params(("arbitrary",)),
    )(xa, proj, proj, *small)


def _ssd_bwd(xa, proj, states, dy, dtb, alog, dsk, ng, name):
    t = xa.shape[0]
    nc, ix = _ssd_specs(t, True)

    def body(xa_ref, dt_ref, z_ref, st_ref, dy_ref, dtb_ref, alog_ref, dsk_ref, ng_ref,
             dxa_ref, ddt_ref, dz_ref, gdtb_ref, galog_ref, gdsk_ref, gng_ref, carry_ref):
        first = pl.program_id(0) == 0

        @pl.when(first)
        def _():
            carry_ref[...] = jnp.zeros_like(carry_ref)

        args = (st_ref[...], xa_ref[...], dt_ref[...], z_ref[...], dtb_ref[...], alog_ref[...], dsk_ref[...], ng_ref[...])
        _, vjp = jax.vjp(_ssd_chunk, *args)
        ds, dxa, ddt, dz, gdtb, galog, gdsk, gng = vjp((dy_ref[...], carry_ref[...]))
        carry_ref[...] = ds
        dxa_ref[...] = dxa
        ddt_ref[...] = ddt.astype(BF16)
        dz_ref[...] = dz.astype(BF16)
        for r, g in ((gdtb_ref, gdtb), (galog_ref, galog), (gdsk_ref, gdsk), (gng_ref, gng)):
            @pl.when(first)
            def _(r=r, g=g):
                r[...] = g

            @pl.when(jnp.logical_not(first))
            def _(r=r, g=g):
                r[...] += g

    small = [dtb, alog, dsk, ng]
    return pl.pallas_call(
        body, name=name, grid=(nc,),
        in_specs=[pl.BlockSpec((CHUNK, SSM_XBC), lambda c: (ix(c), 0)),
                  pl.BlockSpec((CHUNK, LANE), lambda c: (ix(c), DT_COL // LANE)),
                  pl.BlockSpec((CHUNK, D), lambda c: (ix(c), 2)),
                  pl.BlockSpec((None, N_PAIR, LANE, LANE), lambda c: (ix(c), 0, 0, 0)),
                  pl.BlockSpec((CHUNK, D), lambda c: (ix(c), 0))] + [_whole(p) for p in small],
        out_specs=[pl.BlockSpec((CHUNK, SSM_XBC), lambda c: (ix(c), 0)), pl.BlockSpec((CHUNK, LANE), lambda c: (ix(c), 0)),
                   pl.BlockSpec((CHUNK, D), lambda c: (ix(c), 0))] + [_whole(p) for p in small],
        out_shape=[SDS((t, SSM_XBC), F32), SDS((t, LANE), BF16), SDS((t, D), BF16)] + [SDS(p.shape, F32) for p in small],
        scratch_shapes=[pltpu.VMEM((N_PAIR, LANE, LANE), F32)],
        compiler_params=_params(("arbitrary",)),
    )(xa, proj, proj, states, dy, *small)


def _attn_block(q, kv_prev, kv_cur, cq, sq, ck, sk, sinks, rot, first_block):
    left, right = _half_masks()
    k2 = jnp.concatenate([kv_prev[:, :256], kv_cur[:, :256]], axis=0)
    v2 = jnp.concatenate([kv_prev[:, 256:], kv_cur[:, 256:]], axis=0)
    ri, ci = _iota((LANE, LANE), 0), _iota((LANE, LANE), 1)
    dup = [((ri < 64) & (ci % 64 == ri)).astype(BF16), ((ri >= 64) & (ci % 64 == ri - 64)).astype(BF16)]

    def rope(tt, c, s):
        return tt * c + jnp.dot(tt, rot, precision=HI) * s

    kd, vd = [], []
    for j in range(4):
        sl = slice(LANE * (j // 2), LANE * (j // 2 + 1))
        kd.append(_bdot(rope(k2[:, sl], ck, sk), dup[j % 2], "nn"))
        vd.append(_bdot(v2[:, sl], dup[j % 2], "nn"))
    qi, si = _iota((2 * CHUNK, 2 * CHUNK), 0) % CHUNK, _iota((2 * CHUNK, 2 * CHUNK), 1)
    valid = (si > qi) & (si <= qi + CHUNK) & jnp.logical_or(si >= CHUNK, jnp.logical_not(first_block))
    upper = _iota((2 * CHUNK, 1), 0) < CHUNK
    lanes = _iota((1, LANE), 1)
    outs = []
    for pr in range(N_PAIR):
        qr = rope(q[:, LANE * pr:LANE * (pr + 1)], cq, sq)
        lg = _bdot(jnp.concatenate([qr * left, qr * right], axis=0), kd[pr // 2], "nt") * 0.125
        lg = jnp.where(valid, lg, -jnp.inf)
        s1 = jnp.sum(jnp.where(lanes == 2 * pr, sinks, 0.0), axis=1, keepdims=True)
        s2 = jnp.sum(jnp.where(lanes == 2 * pr + 1, sinks, 0.0), axis=1, keepdims=True)
        sink = jnp.where(upper, s1, s2)
        mx = lax.stop_gradient(jnp.maximum(jnp.max(lg, axis=-1, keepdims=True), sink))
        e = jnp.exp(lg - mx)
        probs = e / (jnp.sum(e, axis=-1, keepdims=True) + jnp.exp(sink - mx))
        o2 = _bdot(probs, vd[pr // 2], "nn")
        outs.append(o2[:CHUNK] * left + o2[CHUNK:] * right)
    return jnp.concatenate(outs, axis=1)


def _attn_fwd(qkv, cos, sin, sinks, rot, name):
    t = qkv.shape[0]
    nb = t // CHUNK

    def body(q_ref, kvp_ref, kvc_ref, cq_ref, sq_ref, cp_ref, sp_ref, sinks_ref, rot_ref, o_ref):
        ck = jnp.concatenate([cp_ref[...], cq_ref[...]], axis=0)
        sk = jnp.concatenate([sp_ref[...], sq_ref[...]], axis=0)
        o_ref[...] = _attn_block(q_ref[...], kvp_ref[...], kvc_ref[...], cq_ref[...], sq_ref[...], ck, sk,
                                 sinks_ref[...], rot_ref[...], pl.program_id(0) == 0).astype(BF16)

    prev = lambda n: jnp.maximum(n - 1, 0)
    return pl.pallas_call(
        body, name=name, grid=(nb,),
        in_specs=[pl.BlockSpec((CHUNK, D), lambda n: (n, 0)),
                  pl.BlockSpec((CHUNK, 512), lambda n: (prev(n), 2)), pl.BlockSpec((CHUNK, 512), lambda n: (n, 2)),
                  pl.BlockSpec((CHUNK, LANE), lambda n: (n, 0)), pl.BlockSpec((CHUNK, LANE), lambda n: (n, 0)),
                  pl.BlockSpec((CHUNK, LANE), lambda n: (prev(n), 0)), pl.BlockSpec((CHUNK, LANE), lambda n: (prev(n), 0)),
                  _whole(sinks), _whole(rot)],
        out_specs=pl.BlockSpec((CHUNK, D), lambda n: (n, 0)), out_shape=SDS((t, D), BF16),
        compiler_params=_params(("parallel",)),
    )(qkv, qkv, qkv, cos, sin, cos, sin, sinks, rot)


def _attn_bwd(qkv, do, cos, sin, sinks, rot, name):
    t = qkv.shape[0]
    nb = t // CHUNK

    def body(q_ref, kvp_ref, kvc_ref, do_ref, cq_ref, sq_ref, cp_ref, sp_ref, sinks_ref, rot_ref,
             dq_ref, dkv_ref, dbq_ref, dbkv_ref, dsink_ref, carry_ref):
        n = pl.program_id(0)

        @pl.when(n == 0)
        def _():
            carry_ref[...] = jnp.zeros_like(carry_ref)
            dbq_ref[...] = jnp.zeros_like(dbq_ref)
            dbkv_ref[...] = jnp.zeros_like(dbkv_ref)
            dsink_ref[...] = jnp.zeros_like(dsink_ref)

        @pl.when(n < nb)
        def _():
            ck = jnp.concatenate([cp_ref[...], cq_ref[...]], axis=0)
            sk = jnp.concatenate([sp_ref[...], sq_ref[...]], axis=0)
            f = lambda q, kvp, kvc, s: _attn_block(q, kvp, kvc, cq_ref[...], sq_ref[...], ck, sk, s, rot_ref[...], n == 0)
            _, vjp = jax.vjp(f, q_ref[...], kvp_ref[...], kvc_ref[...], sinks_ref[...])
            dq, dkvp, dkvc, ds = vjp(do_ref[...].astype(F32))
            done = carry_ref[...] + dkvp
            dq_ref[...] = dq.astype(BF16)
            dkv_ref[...] = done.astype(BF16)
            dbq_ref[...] += jnp.sum(dq, axis=0, keepdims=True)
            dsink_ref[...] += ds
            carry_ref[...] = dkvc

            @pl.when(n > 0)
            def _():
                dbkv_ref[...] += jnp.sum(done, axis=0, keepdims=True)

        @pl.when(n == nb)
        def _():
            done = carry_ref[...]
            dkv_ref[...] = done.astype(BF16)
            dbkv_ref[...] += jnp.sum(done, axis=0, keepdims=True)

    cur = lambda n: jnp.minimum(n, nb - 1)
    prev = lambda n: jnp.maximum(jnp.minimum(n, nb - 1) - 1, 0)
    fin = lambda n: jnp.maximum(n - 1, 0)
    outs = pl.pallas_call(
        body, name=name, grid=(nb + 1,),
        in_specs=[pl.BlockSpec((CHUNK, D), lambda n: (cur(n), 0)),
                  pl.BlockSpec((CHUNK, 512), lambda n: (prev(n), 2)), pl.BlockSpec((CHUNK, 512), lambda n: (cur(n), 2)),
                  pl.BlockSpec((CHUNK, D), lambda n: (cur(n), 0)),
                  pl.BlockSpec((CHUNK, LANE), lambda n: (cur(n), 0)), pl.BlockSpec((CHUNK, LANE), lambda n: (cur(n), 0)),
                  pl.BlockSpec((CHUNK, LANE), lambda n: (prev(n), 0)), pl.BlockSpec((CHUNK, LANE), lambda n: (prev(n), 0)),
                  _whole(sinks), _whole(rot)],
        out_specs=[pl.BlockSpec((CHUNK, D), lambda n: (cur(n), 0)), pl.BlockSpec((CHUNK, 512), lambda n: (fin(n), 0)),
                   pl.BlockSpec((1, D), lambda n: (0, 0)), pl.BlockSpec((1, 512), lambda n: (0, 0)), _whole(sinks)],
        out_shape=[SDS((t, D), BF16), SDS((t, 512), BF16), SDS((1, D), F32), SDS((1, 512), F32), SDS(sinks.shape, F32)],
        scratch_shapes=[pltpu.VMEM((CHUNK, 512), F32)],
        compiler_params=_params(("arbitrary",)),
    )(qkv, qkv, qkv, do, cos, sin, cos, sin, sinks, rot)
    dq, dkv, dbq, dbkv, dsinks = outs
    return jnp.concatenate([dq, dkv], axis=1), jnp.concatenate([dbq, dbkv], axis=1), dsinks


def _loss_head(h, tgt, g, name, tm=512):
    t = h.shape[0]

    def body(h_ref, t_ref, g_ref, loss_ref, dh_ref, dhb_ref, dg_ref):
        def f(hv, gv):
            err = _rms_fn(hv, gv)[0] - t_ref[...]
            return 0.5 * jnp.sum(jnp.mean(err * err, axis=-1, keepdims=True), axis=0, keepdims=True)

        loss, vjp = jax.vjp(f, h_ref[...], g_ref[...])
        dh, dg = vjp(jnp.ones((1, 1), F32))
        dh_ref[...] = dh
        dhb_ref[...] = dh.astype(BF16)
        first = pl.program_id(0) == 0

        @pl.when(first)
        def _():
            loss_ref[...] = loss
            dg_ref[...] = dg

        @pl.when(jnp.logical_not(first))
        def _():
            loss_ref[...] += loss
            dg_ref[...] += dg

    return pl.pallas_call(
        body, name=name, grid=(t // tm,),
        in_specs=[_tok(D, tm), _tok(D, tm), _whole(g)],
        out_specs=[pl.BlockSpec((1, 1), lambda i: (0, 0)), _tok(D, tm), _tok(D, tm), _whole(g)],
        out_shape=[SDS((1, 1), F32), SDS((t, D), F32), SDS((t, D), BF16), SDS(g.shape, F32)],
        compiler_params=_params(("arbitrary",)),
    )(h, tgt, g)


def _res_half(acc, res):
    return (res + 0.5 * acc,)


def _res_full(acc, res):
    return (res + acc,)


def _half(acc):
    return (0.5 * acc,)


def _ffn_in(n, w_in, name, tm=1024):
    t = n.shape[0]
    tm = min(tm, t)

    def body(n_ref, w_ref, pre_ref, act_ref):
        a = n_ref[...]
        gate = jnp.dot(a, w_ref[0], preferred_element_type=F32)
        up = jnp.dot(a, w_ref[1], preferred_element_type=F32)
        pre_ref[0] = gate
        pre_ref[1] = up
        act_ref[...] = (gate * jax.nn.sigmoid(gate) * up).astype(BF16)

    pair = pl.BlockSpec((2, None, tm, FF_SHARD), lambda i, j: (0, j, i, 0))
    return pl.pallas_call(
        body, name=name, grid=(t // tm, 4),
        in_specs=[pl.BlockSpec((tm, D), lambda i, j: (i, 0)), pl.BlockSpec((2, None, D, FF_SHARD), lambda i, j: (0, j, 0, 0))],
        out_specs=[pair, pl.BlockSpec((None, tm, FF_SHARD), lambda i, j: (j, i, 0))],
        out_shape=[SDS((2, 4, t, FF_SHARD), F32), SDS((4, t, FF_SHARD), BF16)],
        compiler_params=_params(("parallel", "parallel")),
    )(n, w_in.reshape(2, 4, D, FF_SHARD))


def _ffn_dact(dhb, w_out, pre, name, tm=1024, deps=()):
    t = dhb.shape[0]
    tm = min(tm, t)

    def body(d_ref, w_ref, pre_ref, *rest):
        o_ref = rest[-1]
        dact = 0.5 * lax.dot_general(d_ref[...], w_ref[...], _DIMS["nt"], preferred_element_type=F32)
        gate, up = pre_ref[0], pre_ref[1]
        sg = jax.nn.sigmoid(gate)
        o_ref[0] = (dact * up * (sg * (1.0 + gate * (1.0 - sg)))).astype(BF16)
        o_ref[1] = (dact * (gate * sg)).astype(BF16)

    pair = pl.BlockSpec((2, None, tm, FF_SHARD), lambda i, j: (0, j, i, 0))
    return pl.pallas_call(
        body, name=name, grid=(t // tm, 4),
        in_specs=[pl.BlockSpec((tm, D), lambda i, j: (i, 0)), pl.BlockSpec((None, FF_SHARD, D), lambda i, j: (j, 0, 0)), pair]
        + [ANY_SPEC] * len(deps),
        out_specs=pair, out_shape=SDS((2, 4, t, FF_SHARD), BF16),
        compiler_params=_params(("parallel", "parallel")),
    )(dhb, w_out, pre, *deps)


def _ffn_fwd(h, g, w_in, w_out, tag, deps=()):
    n = _rms(h, g, f"{tag}_rms", deps=deps)
    pre, act = _ffn_in(n, w_in, f"{tag}_in")
    out = _mm(act, w_out, reduce_j=True, tk=FF_SHARD, epi=_res_half, extras=(h[None],), name=f"{tag}_out")[0][0]
    return out, (h, n, pre, act)


def _ffn_bwd(dh, dhb, saved, g, w_in, w_out, tag, deps=(), hook=None):
    h, n, pre, act = saved
    t = h.shape[0]
    dpre = _ffn_dact(dhb, w_out, pre, f"{tag}_dact", deps=deps).reshape(N_DEV, t, FF_SHARD)
    dw_out = _mm(act, dhb[None], ta=True, tm=FF_SHARD, epi=_half, out_dtypes=(BF16,), deps=hook(dpre) if hook else (),
                 name=f"{tag}_dwout")[0]
    dh_in, dhb_in, dg = _mm_drms(dpre, w_in, h, g, dh, f"{tag}_dn", FF_SHARD)
    dw_in = _mm(n[None], dpre, ta=True, tn=FF_SHARD, out_dtypes=(BF16,), name=f"{tag}_dwin")[0]
    return dh_in, dhb_in, dg, dw_in, dw_out


def _ple_fwd(h, g, pb, w_gate, w_proj, tag):
    t = h.shape[0]
    tm = 512
    n = _rms(h, g, f"{tag}_rms")
    e = _mm(pb[None], w_proj[None], name=f"{tag}_proj")[0][0]
    z = _mm(n[None], w_gate[None], name=f"{tag}_gate")[0][0]
    out = _rowop(lambda zz, ee, hh: (hh + _ple_fn(zz, ee)[0],), [(z, _tok(D, tm)), (e, _tok(D, tm)), (h, _tok(D, tm))], [],
                 [((t, D), F32, _tok(D, tm))], grid=(t // tm,), name=f"{tag}_mix")[0]
    return out, (h, n, e, z)


def _ple_bwd(dh, dhb, saved, g, pb, w_gate, tag, deps=()):
    h, n, e, z = saved
    t = h.shape[0]
    tm = 512
    dz, de = _rowop_bwd(_ple_fn, [(z, _tok(D, tm)), (e, _tok(D, tm))], [], [(dh, _tok(D, tm))], [(0,), (1,)],
                        [((t, D), (BF16,), _tok(D, tm)), ((t, D), (BF16,), _tok(D, tm))], grid=(t // tm,), name=f"{tag}_dmix",
                        deps=deps)
    dw_proj = _mm(pb[None], de[None], ta=True, out_dtypes=(BF16,), name=f"{tag}_dwproj")[0][0]
    dw_gate = _mm(n[None], dz[None], ta=True, out_dtypes=(BF16,), name=f"{tag}_dwgate")[0][0]
    dh_in, dhb_in, dg = _mm_drms(dz[None], w_gate[None], h, g, dh, f"{tag}_dn", 1024)
    return dh_in, dhb_in, dg, dw_gate, dw_proj


def _hyb_fwd(h, w, tag):
    t = h.shape[0]
    tm = 512
    hn = _rms(h, w["norm_mix"], f"{tag}_rms")
    proj = _mm(hn[None], w["hyb_in"][None], tn=512, name=f"{tag}_in")[0][0]
    u1 = _dwconv([(proj, 0), (proj, D // LANE)], w["conv_w"], w["conv_b"], width=CONV_W, glu=True, silu=False, cb=LANE,
                 name=f"{tag}_conv")
    u = _rowop(_lnswish_fn, [(u1, _tok(D, tm))], [w["ln_g"], w["ln_b"]], [((t, D), BF16, _tok(D, tm))], grid=(t // tm,),
               name=f"{tag}_ln")[0]
    xa = _dwconv([(proj, 3 * D // LANE)], w["sconv_w"], w["sconv_b"], width=SSM_CONV, glu=False, silu=True, cb=LANE,
                 name=f"{tag}_sconv")
    y, states = _ssd_fwd(xa, proj, w["dt_bias"], w["a_log"], w["d_skip"], w["ssm_norm"], f"{tag}_ssd")
    mixed = jnp.stack([u, y], axis=0)
    out = _mm(mixed, w["hyb_out"], reduce_j=True, epi=_res_full, extras=(h[None],), name=f"{tag}_out")[0][0]
    return out, (h, hn, proj, u1, xa, states, mixed)


def _hyb_bwd(dh, dhb, saved, w, tag):
    h, hn, proj, u1, xa, states, mixed = saved
    t = h.shape[0]
    tm = 512
    dmix = _mm(dhb[None], w["hyb_out"], tb=True, name=f"{tag}_dmix")[0]
    dw_out = _mm(mixed, dhb[None], ta=True, out_dtypes=(BF16,), name=f"{tag}_dwout")[0]
    du1, dln_g, dln_b = _rowop_bwd(_lnswish_fn, [(u1, _tok(D, tm))], [w["ln_g"], w["ln_b"]], [(dmix[0], _tok(D, tm))], [(0,)],
                                   [((t, D), (F32,), _tok(D, tm))], grid=(t // tm,), name=f"{tag}_dln")
    dval, dgate, dconv_w, dconv_b = _dwconv_bwd([(proj, 0), (proj, D // LANE)], w["conv_w"], w["conv_b"], du1,
                                                width=CONV_W, glu=True, silu=False, cb=LANE, name=f"{tag}_dconv")
    dxa, ddt, dz, g_dtb, g_alog, g_dsk, g_ng = _ssd_bwd(xa, proj, states, dmix[1], w["dt_bias"], w["a_log"], w["d_skip"],
                                                         w["ssm_norm"], f"{tag}_dssd")
    dxbc, dsconv_w, dsconv_b = _dwconv_bwd([(proj, 3 * D // LANE)], w["sconv_w"], w["sconv_b"], dxa, width=SSM_CONV,
                                           glu=False, silu=True, cb=LANE, name=f"{tag}_dsconv")
    dproj = jnp.concatenate([dval, dgate, dz, dxbc, ddt, jnp.zeros((t, HYB_PAD - DT_COL - LANE), BF16)], axis=1)
    dh_in, dhb_in, dg = _mm_drms(dproj[None], w["hyb_in"][None], h, w["norm_mix"], dh, f"{tag}_dhn", 1024)
    dw_in = _mm(hn[None], dproj[None], ta=True, tn=512, out_dtypes=(BF16,), name=f"{tag}_dwin")[0][0]
    grads = dict(norm_mix=dg, hyb_in=dw_in, hyb_out=dw_out, conv_w=dconv_w, conv_b=dconv_b, ln_g=dln_g, ln_b=dln_b,
                 sconv_w=dsconv_w, sconv_b=dsconv_b, dt_bias=g_dtb, a_log=g_alog, d_skip=g_dsk, ssm_norm=g_ng)
    return dh_in, dhb_in, grads


def _bias_epi(acc, row):
    return (acc + row,)


def _res_bias_epi(acc, res, row):
    return (res + acc + row,)


def _att_fwd(h, w, tables, tag):
    cos, sin, rot = tables
    hn = _rms(h, w["norm_mix"], f"{tag}_rms")
    qkv = _mm(hn[None], w["qkv"][None], tn=512, epi=_bias_epi, rows=(w["b_qkv"],), name=f"{tag}_qkv")[0][0]
    o = _attn_fwd(qkv, cos, sin, w["sinks"], rot, f"{tag}_core")
    out = _mm(o[None], w["w_o"][None], epi=_res_bias_epi, extras=(h[None],), rows=(w["b_o"],), name=f"{tag}_out")[0][0]
    return out, (h, hn, qkv, o)


def _att_bwd(dh, dhb, saved, w, tables, tag):
    cos, sin, rot = tables
    h, hn, qkv, o = saved
    t = h.shape[0]
    tm = 512
    do = _mm(dhb[None], w["w_o"][None], tb=True, out_dtypes=(BF16,), name=f"{tag}_do")[0][0]
    dw_o = _mm(o[None], dhb[None], ta=True, out_dtypes=(BF16,), name=f"{tag}_dwo")[0][0]
    db_o = _rowop_bwd(lambda xx, bb: (xx + bb,), [(dh, _tok(D, tm))], [w["b_o"]], [(dh, _tok(D, tm))], [], [],
                      grid=(t // tm,), name=f"{tag}_dbo")[0]
    dqkv, db_qkv, dsinks = _attn_bwd(qkv, do, cos, sin, w["sinks"], rot, f"{tag}_dcore")
    dh_in, dhb_in, dg = _mm_drms(dqkv[None], w["qkv"][None], h, w["norm_mix"], dh, f"{tag}_dhn", 512)
    dw_qkv = _mm(hn[None], dqkv[None], ta=True, tn=512, out_dtypes=(BF16,), name=f"{tag}_dwqkv")[0][0]
    grads = dict(norm_mix=dg, qkv=dw_qkv, b_qkv=db_qkv, sinks=dsinks, w_o=dw_o, b_o=db_o)
    return dh_in, dhb_in, grads


def _rope_tables(t):
    inv = ROPE_THETA ** (-jnp.arange(0, 64, 2, dtype=F32) / 64)
    ang = jnp.arange(t, dtype=F32)[:, None] * inv[None, :]
    cos, sin = jnp.tile(jnp.cos(ang), (1, 4)), jnp.tile(jnp.sin(ang), (1, 4))
    rot = np.zeros((LANE, LANE), np.float32)
    for j in range(LANE):
        if j % 64 < 32:
            rot[j + 32, j] = -1.0
        else:
            rot[j - 32, j] = 1.0
    return cos, sin, jnp.asarray(rot)


def _local_step(x, p, tgt, layers, final_norm):
    tables = _rope_tables(x.shape[0])
    pb = p.astype(BF16)
    h, saved = x, []
    for i, w in enumerate(layers):
        h, s = _layer_fwd(i, h, w, pb[i], tables)
        saved.append(s)
    loss, dh, dhb, d_final = _loss_head(h, tgt, final_norm, "loss_head")
    grads = [None] * len(layers)
    for i in reversed(range(len(layers))):
        dh, dhb, head = _layer_bwd_head(i, dh, dhb, saved[i], layers[i], pb[i])
        dh, dhb, tail = _layer_bwd_tail(i, dh, dhb, saved[i], layers[i], tables)
        grads[i] = {**head, **tail}
    return loss[0, 0], dh, grads, d_final


def _layer_fwd(i, h, w, pb, tables, deps=()):
    s = {}
    h, s["ffn1"] = _ffn_fwd(h, w["norm_ffn1"], w["ffn1_in"], w["ffn1_out"], f"l{i}_ffn1", deps=deps)
    if i % 2 == 0:
        h, s["mix"] = _hyb_fwd(h, w, f"l{i}_hyb")
    else:
        h, s["mix"] = _att_fwd(h, w, tables, f"l{i}_att")
    h, s["ffn2"] = _ffn_fwd(h, w["norm_ffn2"], w["ffn2_in"], w["ffn2_out"], f"l{i}_ffn2")
    h, s["ple"] = _ple_fwd(h, w["ple_norm"], pb, w["ple_gate"], w["ple_proj"], f"l{i}_ple")
    return h, s


def _layer_bwd_head(i, dh, dhb, s, w, pb, deps=()):
    g = {}
    dh, dhb, g["ple_norm"], g["ple_gate"], g["ple_proj"] = _ple_bwd(dh, dhb, s["ple"], w["ple_norm"], pb, w["ple_gate"],
                                                                    f"l{i}_ple", deps=deps)
    return dh, dhb, g


def _layer_bwd_tail(i, dh, dhb, s, w, tables, deps=()):
    g = {}
    dh, dhb, g["norm_ffn2"], g["ffn2_in"], g["ffn2_out"] = _ffn_bwd(dh, dhb, s["ffn2"], w["norm_ffn2"], w["ffn2_in"],
                                                                    w["ffn2_out"], f"l{i}_ffn2", deps=deps)
    if i % 2 == 0:
        dh, dhb, gm = _hyb_bwd(dh, dhb, s["mix"], w, f"l{i}_hyb")
    else:
        dh, dhb, gm = _att_bwd(dh, dhb, s["mix"], w, tables, f"l{i}_att")
    g.update(gm)
    dh, dhb, g["norm_ffn1"], g["ffn1_in"], g["ffn1_out"] = _ffn_bwd(dh, dhb, s["ffn1"], w["norm_ffn1"], w["ffn1_in"],
                                                                    w["ffn1_out"], f"l{i}_ffn1")
    return dh, dhb, g


def _cols(g):
    full = jnp.moveaxis(g, 0, -2)
    return full.reshape(*full.shape[:-2], N_DEV * g.shape[-1])


def _uncols(full):
    split = full.reshape(*full.shape[:-1], N_DEV, full.shape[-1] // N_DEV)
    return jnp.moveaxis(split, -2, 0)


def _lane_pad(v):
    return jnp.pad(v, ((0, 0), (0, LANE - v.shape[1])))


def _build_layers(gw, gs, rep):
    return [_build_layer(i, gw, gs, rep) for i in range(2)]


def _build_layer(i, gw, gs, rep, parts=("ffn1", "mix", "ffn2", "ple")):
    w = {}
    for f in ("ffn1", "ffn2"):
        if f in parts:
            w[f"norm_{f}"] = rep[f"norm_{f}"][i][None]
            w[f"{f}_in"] = gw[f"{f}_w_in", i]
            w[f"{f}_out"] = gw[f"{f}_w_out", i].reshape(4, FF_SHARD, D)
    if "ple" in parts:
        w["ple_norm"] = rep["ple_norm"][i][None]
        w["ple_gate"] = gw["ple_gate_w", i].reshape(D, D)
        w["ple_proj"] = _cols(gw["ple_proj_w", i])
    if "mix" not in parts:
        return w
    w["norm_mix"] = rep["norm_mix"][i][None]
    if i == 0:
        w["hyb_in"] = jnp.pad(_cols(gw["hyb_w_in", 0]), ((0, 0), (0, HYB_PAD - HYB_IN)))
        w["hyb_out"] = gw["hyb_w_out", 0].reshape(2, D, D)
        w["conv_w"] = _cols(gs["conv_dw_w"][:, 0])
        w["sconv_w"] = _cols(gs["ssm_conv_w"][:, 0])
        w["conv_b"], w["ln_g"], w["ln_b"] = rep["conv_dw_b"], rep["conv_ln_g"], rep["conv_ln_b"]
        w["sconv_b"], w["ssm_norm"] = rep["ssm_conv_b"], rep["ssm_norm"]
        w["dt_bias"], w["a_log"], w["d_skip"] = (_lane_pad(rep[k]) for k in ("ssm_dt_bias", "ssm_a_log", "ssm_d"))
    else:
        w["qkv"] = _cols(gw["att_w_qkv", 0])
        w["w_o"] = gw["att_w_o", 0].reshape(D, D)
        w["b_qkv"] = gs["att_b_qkv"][:, 0].reshape(1, -1)
        w["b_o"] = gs["att_b_o"][:, 0].reshape(1, -1)
        w["sinks"] = _lane_pad(rep["att_sinks"])
    return w


def _big_grads(i, g):
    big = {}
    for f in ("ffn1", "ffn2"):
        if f"{f}_in" in g:
            big[f"{f}_w_in", i] = g[f"{f}_in"]
            big[f"{f}_w_out", i] = g[f"{f}_out"].reshape(N_DEV, D_FF // N_DEV, D)
    if "ple_gate" in g:
        big["ple_gate_w", i] = g["ple_gate"].reshape(N_DEV, D // N_DEV, D)
        big["ple_proj_w", i] = _uncols(g["ple_proj"])
    if "hyb_in" in g:
        big["hyb_w_in", 0] = _uncols(g["hyb_in"][:, :HYB_IN])
        big["hyb_w_out", 0] = g["hyb_out"].reshape(N_DEV, 2 * D // N_DEV, D)
    if "qkv" in g:
        big["att_w_qkv", 0] = _uncols(g["qkv"])
        big["att_w_o", 0] = g["w_o"].reshape(N_DEV, D // N_DEV, D)
    return big


def _collect_grads(grads, d_final):
    g0, g1 = grads
    big, small = {**_big_grads(0, g0), **_big_grads(1, g1)}, {}
    for f in ("ffn1", "ffn2"):
        small[f"norm_{f}"] = jnp.concatenate([g[f"norm_{f}"] for g in grads], axis=0)
    small["norm_mix"] = jnp.concatenate([g["norm_mix"] for g in grads], axis=0)
    small["ple_norm"] = jnp.concatenate([g["ple_norm"] for g in grads], axis=0)
    small["conv_dw_w"] = g0["conv_w"][None]
    small["conv_dw_b"], small["conv_ln_g"], small["conv_ln_b"] = g0["conv_b"], g0["ln_g"], g0["ln_b"]
    small["ssm_conv_w"] = g0["sconv_w"][None]
    small["ssm_conv_b"], small["ssm_norm"] = g0["sconv_b"], g0["ssm_norm"]
    small["ssm_dt_bias"], small["ssm_a_log"], small["ssm_d"] = (g0[k][:, :SSM_HEADS] for k in ("dt_bias", "a_log", "d_skip"))
    small["att_b_qkv"], small["att_b_o"] = g1["b_qkv"], g1["b_o"]
    small["att_sinks"] = g1["sinks"][:, :SSM_HEADS]
    small["final_norm"] = d_final[0]
    return big, small


MESH = pl.DeviceIdType.MESH


def _place():
    return lax.axis_index("x"), lax.axis_index("y"), lax.axis_index("c")


def _all_gather(blocks, space, name):
    nb = len(blocks)

    def body(*refs):
        x_refs, out_refs, (send_sems, recv_sems, local_sem) = refs[:nb], refs[nb:2 * nb], refs[2 * nb:]
        x, y, c = _place()
        me, sibling = (x, y, c), (x, y, 1 - c)
        chips = [(1 - x, y), (x, 1 - y), (1 - x, 1 - y)]

        def copies(k, blk, to, own=False):
            idx = 4 * blk[0] + 2 * blk[1] + blk[2]
            return [pltpu.make_async_remote_copy(src_ref=x_ref if own else out_ref.at[idx], dst_ref=out_ref.at[idx],
                                                 send_sem=send_sems.at[k, b], recv_sem=recv_sems.at[k, b], device_id=to,
                                                 device_id_type=MESH) for b, (x_ref, out_ref) in enumerate(zip(x_refs, out_refs))]

        mine = [pltpu.make_async_copy(x_ref, out_ref.at[4 * x + 2 * y + c], local_sem.at[b])
                for b, (x_ref, out_ref) in enumerate(zip(x_refs, out_refs))]
        first = copies(0, me, sibling, own=True)
        for j, chip in enumerate(chips):
            first += copies(1 + j, me, (*chip, c), own=True)
        for cp in mine + first:
            cp.start()
        passed = []
        for j, chip in enumerate(chips):
            for cp in copies(1 + j, (*chip, c), me):
                cp.wait_recv()
            onward = copies(4 + j, (*chip, c), sibling)
            for cp in onward:
                cp.start()
            passed += onward
        for cp in copies(0, sibling, me):
            cp.wait_recv()
        for j, chip in enumerate(chips):
            for cp in copies(4 + j, (*chip, 1 - c), me):
                cp.wait_recv()
        for cp in first + passed:
            cp.wait_send()
        for cp in mine:
            cp.wait()

    spec = pl.BlockSpec(memory_space=space)
    return pl.pallas_call(
        body, name=name, out_shape=[SDS((N_DEV,) + b.shape, b.dtype) for b in blocks],
        in_specs=[spec] * nb, out_specs=[spec] * nb,
        scratch_shapes=[pltpu.SemaphoreType.DMA((7, nb)), pltpu.SemaphoreType.DMA((7, nb)), pltpu.SemaphoreType.DMA((nb,))],
    )(*blocks)


def _pair_exchange(parts, name):
    nb = len(parts)

    def body(*refs):
        p_refs, got_refs, (send_sems, recv_sems) = refs[:nb], refs[nb:2 * nb], refs[2 * nb:]
        x, y, c = _place()
        copies = [pltpu.make_async_remote_copy(src_ref=p_ref.at[2 * q + (1 - c)], dst_ref=got_ref.at[q],
                                               send_sem=send_sems.at[q, b], recv_sem=recv_sems.at[q, b], device_id=(x, y, 1 - c),
                                               device_id_type=MESH)
                  for q in range(4) for b, (p_ref, got_ref) in enumerate(zip(p_refs, got_refs))]
        for cp in copies:
            cp.start()
        for cp in copies:
            cp.wait_recv()
        for cp in copies:
            cp.wait_send()

    hbm = pl.BlockSpec(memory_space=pltpu.HBM)
    return pl.pallas_call(
        body, name=name, out_shape=[SDS((4,) + p.shape[1:], p.dtype) for p in parts], in_specs=[hbm] * nb, out_specs=[hbm] * nb,
        scratch_shapes=[pltpu.SemaphoreType.DMA((4, nb)), pltpu.SemaphoreType.DMA((4, nb))],
    )(*parts)


HBM_SPEC = pl.BlockSpec(memory_space=pltpu.HBM)
SEM_SPEC = pl.BlockSpec(memory_space=pltpu.SEMAPHORE)
EFFECT = pltpu.SideEffectType.DATAFLOW_SIDE_EFFECTING


def _plan_descriptors(plan, srcs, lands, send_sems, recv_sems, local_sems, arriving):
    remote, local = plan(*_place())
    pick = lambda ref, slot: ref if slot is None else ref.at[slot]
    rem = [pltpu.make_async_remote_copy(src_ref=pick(srcs[si], ss), dst_ref=lands[li].at[rs if arriving else ds],
                                        send_sem=send_sems.at[k], recv_sem=recv_sems.at[k], device_id=dev, device_id_type=MESH)
           for k, (si, ss, li, ds, dev, rs) in enumerate(remote)]
    loc = [pltpu.make_async_copy(pick(srcs[si], ss), lands[li].at[ds], local_sems.at[k])
           for k, (si, ss, li, ds) in enumerate(local)]
    return rem, loc


def _plan_counts(plan):
    remote, local = plan(0, 0, 0)
    return len(remote), max(len(local), 1)


def _exchange_start(srcs, land_shapes, plan, name):
    ns, nl = len(srcs), len(land_shapes)
    n_remote, n_local = _plan_counts(plan)
    lands = [pltpu.with_memory_space_constraint(lax.empty(s.shape, s.dtype), pltpu.HBM) for s in land_shapes]
    srcs = [pltpu.with_memory_space_constraint(s, pltpu.HBM) for s in srcs]

    def body(*refs):
        src_refs, land_refs = refs[:ns], refs[ns:ns + nl]
        send_sems, recv_sems, local_sems = refs[ns + nl:ns + nl + 3]
        token = refs[-1]
        rem, loc = _plan_descriptors(plan, src_refs, land_refs, send_sems, recv_sems, local_sems, arriving=False)
        for cp in loc + rem:
            cp.start()
        token[...] = jnp.zeros_like(token)

    outs = pl.pallas_call(
        body, name=name,
        out_shape=[pltpu.SemaphoreType.DMA((n_remote,)), pltpu.SemaphoreType.DMA((n_remote,)), pltpu.SemaphoreType.DMA((n_local,))]
        + [pltpu.HBM(a.shape, a.dtype) for a in srcs + lands] + [SDS((8, LANE), F32)],
        in_specs=[HBM_SPEC] * (ns + nl),
        out_specs=[SEM_SPEC] * 3 + [HBM_SPEC] * (ns + nl) + [pl.BlockSpec(memory_space=pltpu.VMEM)],
        input_output_aliases={i: 3 + i for i in range(ns + nl)},
        compiler_params=pltpu.CompilerParams(has_side_effects=EFFECT),
    )(*srcs, *lands)
    return (outs[:3], outs[3:3 + ns], outs[3 + ns:3 + ns + nl]), outs[-1]


def _exchange_wait(state, after, plan, name):
    sems, srcs, lands = state
    ns, nl = len(srcs), len(lands)

    def body(*refs):
        src_refs, land_refs = refs[:ns], refs[ns:ns + nl]
        send_sems, recv_sems, local_sems = refs[ns + nl:ns + nl + 3]
        rem, loc = _plan_descriptors(plan, src_refs, land_refs, send_sems, recv_sems, local_sems, arriving=True)
        for cp in rem:
            cp.wait_send()
            cp.wait_recv()
        for cp in loc:
            cp.wait()

    outs = pl.pallas_call(
        body, name=name, out_shape=[pltpu.HBM(a.shape, a.dtype) for a in list(srcs) + list(lands)],
        in_specs=[HBM_SPEC] * (ns + nl) + [SEM_SPEC] * 3 + [ANY_SPEC], out_specs=[HBM_SPEC] * (ns + nl),
        input_output_aliases={i: i for i in range(ns + nl)},
        compiler_params=pltpu.CompilerParams(has_side_effects=EFFECT),
    )(*srcs, *lands, *sems, after)
    return outs[:ns], outs[ns:]


def _gather_plan(nb):
    def plan(x, y, c):
        me = 4 * x + 2 * y + c
        remote = []
        for b in range(nb):
            for r in range(1, N_DEV):
                tx, ty, tc = (1 - x if r & 4 else x), (1 - y if r & 2 else y), (1 - c if r & 1 else c)
                remote.append((b, None, b, me, (tx, ty, tc), 4 * tx + 2 * ty + tc))
        return remote, [(b, None, b, me) for b in range(nb)]
    return plan


def _pair_plan(nb):
    def plan(x, y, c):
        return [(b, 2 * q + (1 - c), b, q, (x, y, 1 - c), q) for b in range(nb) for q in range(4)], []
    return plan


def _chip_plan(nb):
    def plan(x, y, c):
        own = 2 * x + y
        chips = [(1 - x, y), (x, 1 - y), (1 - x, 1 - y)]
        remote = [(b, 2 * cx + cy, b, own, (cx, cy, c), 2 * cx + cy) for b in range(nb) for cx, cy in chips]
        return remote, [(b, own, b, own) for b in range(nb)]
    return plan


def _row_tile(r, cap=4608):
    return max(d for d in range(16, min(r, cap) + 1, 16) if r % d == 0)


def _pair_add(parts, got, core, name):
    _, r, cdim = parts.shape
    tr = _row_tile(r)

    def body(core_ref, p_ref, g_ref, o_ref):
        o_ref[...] = (p_ref[...].astype(F32) + g_ref[...].astype(F32)).astype(o_ref.dtype)

    return pl.pallas_call(
        body, name=name, out_shape=SDS((4, r, cdim), BF16),
        grid_spec=pltpu.PrefetchScalarGridSpec(
            num_scalar_prefetch=1, grid=(4, r // tr),
            in_specs=[pl.BlockSpec((None, tr, cdim), lambda q, i, core_ref: (2 * q + core_ref[0], i, 0)),
                      pl.BlockSpec((None, tr, cdim), lambda q, i, core_ref: (q, i, 0))],
            out_specs=pl.BlockSpec((None, tr, cdim), lambda q, i, core_ref: (q, i, 0))),
        compiler_params=_params(("parallel", "parallel")),
    )(core, parts, got)


def _chip_exchange(sums, name):
    nb = len(sums)

    def body(*refs):
        b_refs, out_refs, (send_sems, recv_sems, local_sem) = refs[:nb], refs[nb:2 * nb], refs[2 * nb:]
        x, y, c = _place()
        own = 2 * x + y
        chips = [(1 - x, y), (x, 1 - y), (1 - x, 1 - y)]

        def copies(k, chip, src_slot, dst_slot):
            return [pltpu.make_async_remote_copy(src_ref=b_ref.at[src_slot], dst_ref=out_ref.at[dst_slot],
                                                 send_sem=send_sems.at[k, b], recv_sem=recv_sems.at[k, b], device_id=(*chip, c),
                                                 device_id_type=MESH) for b, (b_ref, out_ref) in enumerate(zip(b_refs, out_refs))]

        mine = [pltpu.make_async_copy(b_ref.at[own], out_ref.at[own], local_sem.at[b])
                for b, (b_ref, out_ref) in enumerate(zip(b_refs, out_refs))]
        sends = []
        for k, chip in enumerate(chips):
            sends += copies(k, chip, 2 * chip[0] + chip[1], own)
        for cp in mine + sends:
            cp.start()
        for k, chip in enumerate(chips):
            for cp in copies(k, chip, own, 2 * chip[0] + chip[1]):
                cp.wait_recv()
        for cp in sends:
            cp.wait_send()
        for cp in mine:
            cp.wait()

    hbm = pl.BlockSpec(memory_space=pltpu.HBM)
    return pl.pallas_call(
        body, name=name, out_shape=[SDS(s.shape, s.dtype) for s in sums], in_specs=[hbm] * nb, out_specs=[hbm] * nb,
        scratch_shapes=[pltpu.SemaphoreType.DMA((3, nb)), pltpu.SemaphoreType.DMA((3, nb)), pltpu.SemaphoreType.DMA((nb,))],
    )(*sums)


def _sum_slots(parts, name):
    nj, r, cdim = parts.shape
    tr = _row_tile(r)

    def body(p_ref, o_ref):
        acc = p_ref[0].astype(F32)
        for j in range(1, nj):
            acc = acc + p_ref[j].astype(F32)
        o_ref[...] = acc

    return pl.pallas_call(
        body, name=name, out_shape=SDS((r, cdim), F32), grid=(r // tr,),
        in_specs=[pl.BlockSpec((nj, tr, cdim), lambda i: (0, i, 0))], out_specs=pl.BlockSpec((tr, cdim), lambda i: (i, 0)),
        compiler_params=_params(("parallel",)),
    )(parts)


def _adamw_update(wv, gv, mv, vv):
    nm = ADAM_B1 * mv + (1.0 - ADAM_B1) * gv
    nv = ADAM_B2 * vv + (1.0 - ADAM_B2) * (gv * gv)
    m_hat = nm / (1.0 - ADAM_B1 ** ADAM_STEP)
    v_hat = nv / (1.0 - ADAM_B2 ** ADAM_STEP)
    return -ADAM_LR * (m_hat / (jnp.sqrt(v_hat) + ADAM_EPS) + ADAM_WD * wv), nm, nv


def _adamw_summed(w, m, v, by_chip, name):
    nl, r, cdim = w.shape
    tr = _row_tile(r, 512)
    nblk = r // tr

    def body(*refs):
        chip_refs, (w_ref, m_ref, v_ref, g_ref, d_ref, nm_ref, nv_ref) = refs[:nl], refs[nl:]
        layer = pl.program_id(0)
        gv = None
        for ll, c_ref in enumerate(chip_refs):
            s = c_ref[0].astype(F32)
            for q in range(1, 4):
                s = s + c_ref[q].astype(F32)
            gv = s if gv is None else jnp.where(layer == ll, s, gv)
        g_ref[...] = gv
        d_ref[...], nm_ref[...], nv_ref[...] = _adamw_update(w_ref[...], gv, m_ref[...], v_ref[...])

    def chip_map(ll):
        return lambda l, i: (0, jnp.where(l == ll, i, jnp.where(l > ll, nblk - 1, 0)), 0)

    spec = pl.BlockSpec((None, tr, cdim), lambda l, i: (l, i, 0))
    return pl.pallas_call(
        body, name=name, grid=(nl, nblk),
        in_specs=[pl.BlockSpec((4, tr, cdim), chip_map(ll)) for ll in range(nl)] + [spec] * 3,
        out_specs=[spec] * 4, out_shape=[SDS((nl, r, cdim), F32)] * 4,
        compiler_params=_params(("arbitrary", "arbitrary")),
    )(*by_chip, w, m, v)


def _adamw(w, g, m, v, name):
    shape = w.shape
    cdim = shape[-1]
    w2, g2, m2, v2 = (a.reshape(-1, cdim) for a in (w, g, m, v))
    r = w2.shape[0]
    tr = next(d for d in (512, 352, 256, 128, 64, 32, 16, 8, r) if r % d == 0)

    def body(w_ref, g_ref, m_ref, v_ref, d_ref, nm_ref, nv_ref):
        d_ref[...], nm_ref[...], nv_ref[...] = _adamw_update(w_ref[...], g_ref[...], m_ref[...], v_ref[...])

    spec = pl.BlockSpec((tr, cdim), lambda i: (i, 0))
    outs = pl.pallas_call(
        body, name=name, grid=(r // tr,), in_specs=[spec] * 4, out_specs=[spec] * 3, out_shape=[SDS((r, cdim), F32)] * 3,
        compiler_params=_params(("parallel",)),
    )(w2, g2, m2, v2)
    return tuple(o.reshape(shape) for o in outs)


WEIGHTS = ("norm_ffn1", "ffn1_w_in", "ffn1_w_out", "norm_mix", "norm_ffn2", "ffn2_w_in", "ffn2_w_out", "ple_norm", "ple_gate_w",
           "ple_proj_w", "hyb_w_in", "conv_dw_w", "conv_dw_b", "conv_ln_g", "conv_ln_b", "ssm_conv_w", "ssm_conv_b", "ssm_dt_bias",
           "ssm_a_log", "ssm_d", "ssm_norm", "hyb_w_out", "att_w_qkv", "att_b_qkv", "att_sinks", "att_w_o", "att_b_o", "final_norm")
BIG = ("ffn1_w_in", "ffn1_w_out", "ffn2_w_in", "ffn2_w_out", "ple_gate_w", "ple_proj_w", "hyb_w_in", "hyb_w_out", "att_w_qkv",
       "att_w_o")
SMALL_SHARDED = {"conv_dw_w": 2, "ssm_conv_w": 2, "att_b_qkv": 1, "att_b_o": 1}
SMALL = tuple(n for n in WEIGHTS if n not in BIG)
PACK_ROWS = 16


def _pack(arrays, lead=0):
    pieces = []
    for a in arrays:
        flat = a.reshape(*a.shape[:lead], -1)
        size = flat.shape[-1]
        padded = -(-size // (PACK_ROWS * LANE)) * PACK_ROWS * LANE
        flat = jnp.pad(flat, [(0, 0)] * lead + [(0, padded - size)])
        pieces.append(flat.reshape(*a.shape[:lead], padded // LANE, LANE))
    return jnp.concatenate(pieces, axis=lead)


def _unpack(buf, shapes, lead=0):
    out, row = [], 0
    for shape in shapes:
        size = math.prod(shape)
        rows = -(-size // (PACK_ROWS * LANE)) * PACK_ROWS
        piece = lax.slice_in_dim(buf, row, row + rows, axis=lead)
        piece = piece.reshape(*buf.shape[:lead], rows * LANE)
        out.append(lax.slice_in_dim(piece, 0, size, axis=lead).reshape(*buf.shape[:lead], *shape))
        row += rows
    return out


def kernel(x, p, norm_ffn1, ffn1_w_in, ffn1_w_out, norm_mix, norm_ffn2, ffn2_w_in, ffn2_w_out, ple_norm, ple_gate_w, ple_proj_w, hyb_w_in, conv_dw_w, conv_dw_b, conv_ln_g, conv_ln_b, ssm_conv_w, ssm_conv_b, ssm_dt_bias, ssm_a_log, ssm_d, ssm_norm, hyb_w_out, att_w_qkv, att_b_qkv, att_sinks, att_w_o, att_b_o, final_norm, loss_target, m_norm_ffn1, m_ffn1_w_in, m_ffn1_w_out, m_norm_mix, m_norm_ffn2, m_ffn2_w_in, m_ffn2_w_out, m_ple_norm, m_ple_gate_w, m_ple_proj_w, m_hyb_w_in, m_conv_dw_w, m_conv_dw_b, m_conv_ln_g, m_conv_ln_b, m_ssm_conv_w, m_ssm_conv_b, m_ssm_dt_bias, m_ssm_a_log, m_ssm_d, m_ssm_norm, m_hyb_w_out, m_att_w_qkv, m_att_b_qkv, m_att_sinks, m_att_w_o, m_att_b_o, m_final_norm, v_norm_ffn1, v_ffn1_w_in, v_ffn1_w_out, v_norm_mix, v_norm_ffn2, v_ffn2_w_in, v_ffn2_w_out, v_ple_norm, v_ple_gate_w, v_ple_proj_w, v_hyb_w_in, v_conv_dw_w, v_conv_dw_b, v_conv_ln_g, v_conv_ln_b, v_ssm_conv_w, v_ssm_conv_b, v_ssm_dt_bias, v_ssm_a_log, v_ssm_d, v_ssm_norm, v_hyb_w_out, v_att_w_qkv, v_att_b_qkv, v_att_sinks, v_att_w_o, v_att_b_o, v_final_norm):
    args = (norm_ffn1, ffn1_w_in, ffn1_w_out, norm_mix, norm_ffn2, ffn2_w_in, ffn2_w_out, ple_norm, ple_gate_w, ple_proj_w, hyb_w_in, conv_dw_w, conv_dw_b, conv_ln_g, conv_ln_b, ssm_conv_w, ssm_conv_b, ssm_dt_bias, ssm_a_log, ssm_d, ssm_norm, hyb_w_out, att_w_qkv, att_b_qkv, att_sinks, att_w_o, att_b_o, final_norm)
    moments_m = (m_norm_ffn1, m_ffn1_w_in, m_ffn1_w_out, m_norm_mix, m_norm_ffn2, m_ffn2_w_in, m_ffn2_w_out, m_ple_norm, m_ple_gate_w, m_ple_proj_w, m_hyb_w_in, m_conv_dw_w, m_conv_dw_b, m_conv_ln_g, m_conv_ln_b, m_ssm_conv_w, m_ssm_conv_b, m_ssm_dt_bias, m_ssm_a_log, m_ssm_d, m_ssm_norm, m_hyb_w_out, m_att_w_qkv, m_att_b_qkv, m_att_sinks, m_att_w_o, m_att_b_o, m_final_norm)
    moments_v = (v_norm_ffn1, v_ffn1_w_in, v_ffn1_w_out, v_norm_mix, v_norm_ffn2, v_ffn2_w_in, v_ffn2_w_out, v_ple_norm, v_ple_gate_w, v_ple_proj_w, v_hyb_w_in, v_conv_dw_w, v_conv_dw_b, v_conv_ln_g, v_conv_ln_b, v_ssm_conv_w, v_ssm_conv_b, v_ssm_dt_bias, v_ssm_a_log, v_ssm_d, v_ssm_norm, v_hyb_w_out, v_att_w_qkv, v_att_b_qkv, v_att_sinks, v_att_w_o, v_att_b_o, v_final_norm)
    w = dict(zip(WEIGHTS, args))
    m = dict(zip(WEIGHTS, moments_m))
    v = dict(zip(WEIGHTS, moments_v))
    cx, cy, cc = _place()
    me = 4 * cx + 2 * cy + cc

    core = jnp.reshape(cc, (1,)).astype(jnp.int32)
    layer_of = lambda n, i: 1 if n.startswith("att_") else i
    keys = [[(n, i) for n in BIG for i in range(w[n].shape[0]) if layer_of(n, i) == layer] for layer in range(2)]

    first = [key for key in keys[0] if key[0].startswith("ffn1")]
    mixer = [key for key in keys[0] if key[0].startswith("hyb")]
    rest0 = [key for key in keys[0] if key not in first + mixer]
    gw, by_chip = {}, {}

    def gather_later(group, name):
        blocks = [w[n][i].astype(BF16) for n, i in group]
        plan = _gather_plan(len(blocks))
        state, token = _exchange_start(blocks, [SDS((N_DEV,) + b.shape, BF16) for b in blocks], plan, f"{name}_start")
        return token, lambda after: gw.update(zip(group, _exchange_wait(state, after, plan, f"{name}_wait")[1]))

    def reduce_later(group, big, name):
        pair_plan, chip_plan = _pair_plan(len(group)), _chip_plan(len(group))
        parts = [big[key] for key in group]
        pair, token = _exchange_start(parts, [SDS((4,) + pt.shape[1:], BF16) for pt in parts], pair_plan, f"{name}_pair_start")
        stage = {}

        def middle(after):
            thru, got = _exchange_wait(pair, after, pair_plan, f"{name}_pair_wait")
            sums = [_pair_add(pt, gt, core, f"grads_pair_add_{n}_{i}") for pt, gt, (n, i) in zip(thru, got, group)]
            stage["chip"], chip_token = _exchange_start(sums, [SDS(s.shape, BF16) for s in sums], chip_plan, f"{name}_chip_start")
            return chip_token

        def finish(after):
            by_chip.update(zip(group, _exchange_wait(stage["chip"], after, chip_plan, f"{name}_chip_wait")[1]))

        return token, middle, finish

    gw.update(zip(first, _all_gather([w[n][i].astype(BF16) for n, i in first], pltpu.HBM, "gather_weights_first")))
    mixer_token, mixer_arrived = gather_later(mixer, "gather_weights_mixer")
    rest0_token, rest0_arrived = gather_later(rest0, "gather_weights_rest")
    layer1_token, layer1_arrived = gather_later(keys[1], "gather_weights_l1")
    ss_shapes = [w[n].shape for n in SMALL_SHARDED]
    gathered_small = _all_gather([_pack([w[n] for n in SMALL_SHARDED])], pltpu.VMEM, "gather_small_weights")[0]
    gs = dict(zip(SMALL_SHARDED, _unpack(gathered_small, ss_shapes, lead=1)))
    rep = {n: w[n] for n in SMALL if n not in SMALL_SHARDED}

    tables = _rope_tables(x.shape[1])
    pb = p[:, 0].astype(BF16)
    w0, s0 = _build_layer(0, gw, gs, rep, parts=("ffn1",)), {}
    h, s0["ffn1"] = _ffn_fwd(x[0], w0["norm_ffn1"], w0["ffn1_in"], w0["ffn1_out"], "l0_ffn1",
                             deps=(mixer_token, rest0_token, layer1_token))
    mixer_arrived(h)
    w0.update(_build_layer(0, gw, gs, rep, parts=("mix",)))
    h, s0["mix"] = _hyb_fwd(h, w0, "l0_hyb")
    rest0_arrived(h)
    w0.update(_build_layer(0, gw, gs, rep, parts=("ffn2", "ple")))
    h, s0["ffn2"] = _ffn_fwd(h, w0["norm_ffn2"], w0["ffn2_in"], w0["ffn2_out"], "l0_ffn2")
    h, s0["ple"] = _ple_fwd(h, w0["ple_norm"], pb[0], w0["ple_gate"], w0["ple_proj"], "l0_ple")
    layer1_arrived(h)
    w1 = _build_layer(1, gw, gs, rep)
    h, s1 = _layer_fwd(1, h, w1, pb[1], tables)
    loss, dh, dhb, d_final = _loss_head(h, loss_target[0], final_norm[None], "loss_head")
    loss = lax.psum(loss[0, 0], ("x", "y", "c"))

    dh, dhb, head1 = _layer_bwd_head(1, dh, dhb, s1, w1, pb[1])
    dh, dhb, tail1 = _layer_bwd_tail(1, dh, dhb, s1, w1, tables)
    grads1 = {**head1, **tail1}
    l1_token, l1_middle, l1_finish = reduce_later(keys[1], _big_grads(1, grads1), "grads_l1")
    dh, dhb, grads0 = _layer_bwd_head(0, dh, dhb, s0, w0, pb[0], deps=(l1_token,))
    dh, dhb, grads0["norm_ffn2"], grads0["ffn2_in"], grads0["ffn2_out"] = _ffn_bwd(
        dh, dhb, s0["ffn2"], w0["norm_ffn2"], w0["ffn2_in"], w0["ffn2_out"], "l0_ffn2", deps=(l1_middle(dh),))
    dh, dhb, mixer_grads = _hyb_bwd(dh, dhb, s0["mix"], w0, "l0_hyb")
    grads0.update(mixer_grads)
    l0_token, l0_middle, l0_finish = reduce_later(mixer + rest0, _big_grads(0, grads0), "grads_l0")
    dx, dhb, grads0["norm_ffn1"], grads0["ffn1_in"], grads0["ffn1_out"] = _ffn_bwd(
        dh, dhb, s0["ffn1"], w0["norm_ffn1"], w0["ffn1_in"], w0["ffn1_out"], "l0_ffn1", deps=(l0_token,),
        hook=lambda dpre: (l0_middle(dpre),))
    l1_finish(dx)
    l0_finish(dx)
    big0 = _big_grads(0, grads0)
    parts0 = [big0[key] for key in first]
    got0 = _pair_exchange(parts0, "grads_first_pair_exchange")
    sums0 = [_pair_add(pt, gt, core, f"grads_pair_add_{n}_{i}") for pt, gt, (n, i) in zip(parts0, got0, first)]
    by_chip.update(zip(first, _chip_exchange(sums0, "grads_first_chip_exchange")))
    _, small = _collect_grads([grads0, grads1], d_final)
    small_shapes = [small[n].shape for n in SMALL]
    all_small = _all_gather([_pack([small[n] for n in SMALL])], pltpu.VMEM, "gather_small_grads")[0]
    g = dict(zip(SMALL, _unpack(_sum_slots(all_small, "small_grads_sum"), small_shapes)))
    for n, axis in SMALL_SHARDED.items():
        g[n] = lax.dynamic_slice_in_dim(g[n], me * w[n].shape[axis], w[n].shape[axis], axis=axis)

    delta, new_m, new_v = {}, {}, {}
    for n in BIG:
        g[n], delta[n], new_m[n], new_v[n] = _adamw_summed(w[n], m[n], v[n], [by_chip[n, i] for i in range(w[n].shape[0])],
                                                           f"adamw_{n}")
    packed = [_pack([d[n] for n in SMALL]) for d in (w, g, m, v)]
    shapes = [w[n].shape for n in SMALL]
    for d, buf in zip((delta, new_m, new_v), _adamw(*packed, "adamw_small")):
        d.update(zip(SMALL, _unpack(buf, shapes)))
    return (loss, dx[None], *[g[n] for n in WEIGHTS], *[delta[n] for n in WEIGHTS], *[new_m[n] for n in WEIGHTS],
            *[new_v[n] for n in WEIGHTS])
```

```python
import functools
import math

import numpy as np
import jax
import jax.numpy as jnp
from jax import lax
from jax.experimental import pallas as pl
from jax.experimental.pallas import tpu as pltpu

F32, BF16 = jnp.float32, jnp.bfloat16
HI = lax.Precision.HIGHEST
SDS = jax.ShapeDtypeStruct

N_DEV = 8
D = 1024
D_FF = 2816
FF_SHARD = 2 * D_FF // N_DEV
PLE_DIM = 256
EPS = 1e-6
CONV_W = 31
SSM_CONV = 4
SSM_HEADS = 16
SSM_XBC = 1536
CHUNK = 128
HYB_IN = 4624
HYB_PAD = 5120
DT_COL = 4608
N_PAIR = 8
ROPE_THETA = 10000.0
LANE = 128
VMEM_LIMIT = 56 * 1024 * 1024

ADAM_LR, ADAM_B1, ADAM_B2, ADAM_EPS, ADAM_WD, ADAM_STEP = 0.001, 0.9, 0.999, 1e-08, 0.01, 10


def _params(sem):
    return pltpu.CompilerParams(dimension_semantics=sem, vmem_limit_bytes=VMEM_LIMIT)


_CHAIN = []


def _restart_chain():
    _CHAIN.clear()


def _pallas(body, *, in_specs, **kw):
    def run(*args):
        n, dep = len(args), list(_CHAIN)

        def chained(*refs):
            return body(*refs[:n], *refs[n + len(dep):])

        outs = pl.pallas_call(chained, in_specs=list(in_specs) + [pl.BlockSpec(memory_space=pl.ANY)] * len(dep), **kw)(*args, *dep)
        _CHAIN[:] = [outs[-1] if isinstance(outs, (list, tuple)) else outs]
        return outs

    return run


def _mm(a, b, *, ta=False, tb=False, reduce_j=False, out_dtypes=(F32,), tm=1024, tn=1024, tk=1024,
        epi=None, extras=(), rows=(), deps=(), sums=0, name):
    ja, jb = a.shape[0], b.shape[0]
    nj = max(ja, jb)
    jo = 1 if reduce_j else nj
    m, k = (a.shape[2], a.shape[1]) if ta else (a.shape[1], a.shape[2])
    n = b.shape[1] if tb else b.shape[2]
    assert (b.shape[2] if tb else b.shape[1]) == k and ja in (1, nj) and jb in (1, nj)
    tm, tn, tk = min(tm, m), min(tn, n), min(tk, k)
    assert m % tm == 0 and n % tn == 0 and k % tk == 0, (name, m, n, k, tm, tn, tk)
    assert not sums or (tn == n and (reduce_j or nj == 1))
    nk = k // tk
    steps = nk * (nj if reduce_j else 1)
    ne, nr, no = len(extras), len(rows), len(out_dtypes)

    def a_map(i, c, j, kk):
        return (j if ja > 1 else 0, kk, i) if ta else (j if ja > 1 else 0, i, kk)

    def b_map(i, c, j, kk):
        return (j if jb > 1 else 0, c, kk) if tb else (j if jb > 1 else 0, kk, c)

    def o_map(i, c, j, kk):
        return (0 if reduce_j else j, i, c)

    dims = (((0 if ta else 1,), (1 if tb else 0,)), ((), ()))

    def body(a_ref, b_ref, *rest):
        ex, rw = rest[:ne], rest[ne:ne + nr]
        outs = rest[ne + nr + len(deps):ne + nr + len(deps) + no]
        sum_refs = rest[ne + nr + len(deps) + no:ne + nr + len(deps) + no + sums]
        part = lax.dot_general(a_ref[...], b_ref[...], dims, preferred_element_type=F32)
        first_tile = pl.program_id(0) == 0

        def finish(acc):
            res = epi(acc, *[e[...] for e in ex], *[r[...] for r in rw]) if epi else (acc,)
            for o, r in zip(outs, res):
                o[...] = r.astype(o.dtype)
            for s_ref, r in zip(sum_refs, res[no:]):
                @pl.when(first_tile)
                def _(s_ref=s_ref, r=r):
                    s_ref[...] = r

                @pl.when(jnp.logical_not(first_tile))
                def _(s_ref=s_ref, r=r):
                    s_ref[...] += r

        if steps == 1:
            finish(part)
            return
        acc_ref = rest[-1]
        kk = pl.program_id(3)
        step = pl.program_id(2) * nk + kk if reduce_j else kk

        @pl.when(step == 0)
        def _():
            acc_ref[...] = part

        @pl.when(step > 0)
        def _():
            acc_ref[...] += part

        @pl.when(step == steps - 1)
        def _():
            finish(acc_ref[...])

    o_spec = pl.BlockSpec((None, tm, tn), o_map)
    row_spec = pl.BlockSpec((1, tn), lambda i, c, j, kk: (0, c))
    return _pallas(
        body, name=name, grid=(m // tm, n // tn, nj, nk),
        in_specs=[pl.BlockSpec((None, tk, tm) if ta else (None, tm, tk), a_map),
                  pl.BlockSpec((None, tn, tk) if tb else (None, tk, tn), b_map)]
        + [o_spec] * ne + [row_spec] * nr + [ANY_SPEC] * len(deps),
        out_specs=[o_spec] * no + [row_spec] * sums,
        out_shape=[SDS((jo, m, n), dt) for dt in out_dtypes] + [SDS((1, n), F32)] * sums,
        scratch_shapes=[pltpu.VMEM((tm, tn), F32)] if steps > 1 else [],
        compiler_params=_params(("arbitrary" if sums else "parallel", "parallel", "arbitrary", "arbitrary")),
    )(a, b, *extras, *rows, *deps)


def _whole(p):
    return pl.BlockSpec(p.shape, lambda *_: (0,) * p.ndim)


ANY_SPEC = pl.BlockSpec(memory_space=pl.ANY)


def _rowop(fn, tiles, params, outs, *, grid, name, deps=()):
    nin = len(tiles) + len(params)

    def body(*refs):
        res = fn(*[r[...].astype(F32) for r in refs[:nin]])
        for r, o in zip(refs[nin + len(deps):], res):
            r[...] = o.astype(r.dtype)

    return _pallas(
        body, name=name, grid=grid,
        in_specs=[s for _, s in tiles] + [_whole(p) for p in params] + [ANY_SPEC] * len(deps),
        out_specs=[s for _, _, s in outs], out_shape=[SDS(sh, dt) for sh, dt, _ in outs],
        compiler_params=_params(("parallel",) * len(grid)),
    )(*[t for t, _ in tiles], *params, *deps)


def _rowop_bwd(fn, tiles, params, cots, wrt, gouts, *, grid, name, adds=(), deps=()):
    nt, npar, nc, na = len(tiles), len(params), len(cots), len(adds)
    nin = nt + npar
    flat = [i for grp in wrt for i in grp]
    n_gout = sum(len(dts) for _, dts, _ in gouts)

    def body(*refs):
        vals = [r[...].astype(F32) for r in refs[:nin]]
        cvals = [r[...].astype(F32) for r in refs[nin:nin + nc]]
        avals = [r[...].astype(F32) for r in refs[nin + nc:nin + nc + na]]
        orefs = refs[nin + nc + na + len(deps):]
        diff_idx = flat + list(range(nt, nin))

        def f(*dv):
            full = list(vals)
            for i, v in zip(diff_idx, dv):
                full[i] = v
            return fn(*full)

        _, vjp = jax.vjp(f, *[vals[i] for i in diff_idx])
        grads = vjp(tuple(cvals))
        tile_g, par_g = list(grads[:len(flat)]), grads[len(flat):]
        group_g, at = [], 0
        for grp in wrt:
            members = tile_g[at:at + len(grp)]
            at += len(grp)
            group_g.append(members[0] if len(grp) == 1 else jnp.stack(members, axis=0))
        for av in avals:
            group_g[0] = group_g[0] + av
        o = 0
        for g, (_, dts, _) in zip(group_g, gouts):
            for _ in dts:
                orefs[o][...] = g.astype(orefs[o].dtype)
                o += 1
        first = functools.reduce(jnp.logical_and, [pl.program_id(ax) == 0 for ax in range(len(grid))])
        for r, g in zip(orefs[n_gout:], par_g):
            @pl.when(first)
            def _(r=r, g=g):
                r[...] = g

            @pl.when(jnp.logical_not(first))
            def _(r=r, g=g):
                r[...] += g

    out_specs, out_shape = [], []
    for sh, dts, spec in gouts:
        for dt in dts:
            out_specs.append(spec)
            out_shape.append(SDS(sh, dt))
    for p in params:
        out_specs.append(_whole(p))
        out_shape.append(SDS(p.shape, F32))
    return _pallas(
        body, name=name, grid=grid,
        in_specs=[s for _, s in tiles] + [_whole(p) for p in params] + [s for _, s in cots] + [s for _, s in adds]
        + [ANY_SPEC] * len(deps),
        out_specs=out_specs, out_shape=out_shape,
        compiler_params=_params(("arbitrary",) * len(grid)),
    )(*[t for t, _ in tiles], *params, *[c for c, _ in cots], *[a for a, _ in adds], *deps)


def _tok(c, tm, col=0):
    return pl.BlockSpec((tm, c), lambda i, col=col: (i, col))


def _rms_fn(h, g):
    return (h * lax.rsqrt(jnp.mean(h * h, axis=-1, keepdims=True) + EPS) * g,)


def _lnswish_fn(u, g, b):
    mu = jnp.mean(u, axis=-1, keepdims=True)
    xc = u - mu
    y = xc * lax.rsqrt(jnp.mean(xc * xc, axis=-1, keepdims=True) + EPS) * g + b
    return (y * jax.nn.sigmoid(y),)


def _ple_fn(z, e):
    return (jax.nn.sigmoid(z) * e,)


def _rms(h, g, name, tm=512, deps=()):
    t = h.shape[0]
    return _rowop(_rms_fn, [(h, _tok(D, tm))], [g], [((t, D), BF16, _tok(D, tm))], grid=(t // tm,), name=name, deps=deps)[0]


def _drms_epi(dn, h, dres, g):
    _, vjp = jax.vjp(_rms_fn, h, g)
    dh, dg = vjp((dn,))
    dh = dh + dres
    return dh, dh, dg


def _mm_drms(a, b, h, g, dres, name, tk):
    dh, dhb, dg = _mm(a, b, tb=True, reduce_j=a.shape[0] > 1, tm=512, tk=tk, epi=_drms_epi, extras=(h[None], dres[None]),
                      rows=(g,), out_dtypes=(F32, BF16), sums=1, name=name)
    return dh[0], dhb[0], dg


def _conv_geometry(width):
    pad = 32 if width > 8 else 8
    return pad, pad - (width - 1)


def _fill_shifts(xpad_ref, sh_ref, t, shifts):
    for r in shifts:
        sh_ref[r, :, :] = xpad_ref[pl.ds(r, t + 32), :]


def _dwconv(xs, w, b, *, width, glu, silu, cb, name):
    t = xs[0][0].shape[0]
    c = w.shape[1]
    pad, off = _conv_geometry(width)
    shifts = sorted({(k + off) % 8 for k in range(width)})
    ch = 32

    def body(*refs):
        x_refs, (w_ref, b_ref, o_ref, xpad_ref, sh_ref) = refs[:len(xs)], refs[len(xs):]
        u = x_refs[0][...] * jax.nn.sigmoid(x_refs[1][...]) if glu else x_refs[0][...]
        xpad_ref[pl.ds(0, pad), :] = jnp.zeros((pad, cb), F32)
        xpad_ref[pl.ds(pad, t), :] = u
        xpad_ref[pl.ds(pad + t, 40 - pad), :] = jnp.zeros((40 - pad, cb), F32)
        _fill_shifts(xpad_ref, sh_ref, t, shifts)

        def chunk(i, carry):
            t0 = pl.multiple_of(i * ch, ch)
            acc = jnp.broadcast_to(b_ref[...], (ch, cb))
            for k in range(width):
                q, r = divmod(k + off, 8)
                acc = acc + w_ref[pl.ds(k, 1), :] * sh_ref[r, pl.ds(t0 + 8 * q, ch), :]
            o_ref[pl.ds(t0, ch), :] = acc * jax.nn.sigmoid(acc) if silu else acc
            return carry

        lax.fori_loop(0, t // ch, chunk, 0)

    return _pallas(
        body, name=name, grid=(c // cb,),
        in_specs=[pl.BlockSpec((t, cb), lambda i, o=o: (0, o + i)) for _, o in xs]
        + [pl.BlockSpec((width, cb), lambda i: (0, i)), pl.BlockSpec((1, cb), lambda i: (0, i))],
        out_specs=pl.BlockSpec((t, cb), lambda i: (0, i)), out_shape=SDS((t, c), F32),
        scratch_shapes=[pltpu.VMEM((t + 40, cb), F32), pltpu.VMEM((8, t + 32, cb), F32)],
        compiler_params=_params(("parallel",)),
    )(*[x for x, _ in xs], w, b)


def _dwconv_bwd(xs, w, b, dy, *, width, glu, silu, cb, name):
    t = xs[0][0].shape[0]
    c = w.shape[1]
    pad, off = _conv_geometry(width)
    shifts = sorted({(k + off) % 8 for k in range(width)})
    shifts_t = sorted({mm % 8 for mm in range(width)})
    ch = 32
    nx = len(xs)

    def body(*refs):
        x_refs = refs[:nx]
        w_ref, b_ref, dy_ref = refs[nx:nx + 3]
        dx_refs = refs[nx + 3:nx + 3 + nx]
        dw_ref, db_ref, xpad_ref, sh_ref, dc_ref = refs[nx + 3 + nx:]
        u = x_refs[0][...] * jax.nn.sigmoid(x_refs[1][...]) if glu else x_refs[0][...]
        xpad_ref[pl.ds(0, pad), :] = jnp.zeros((pad, cb), F32)
        xpad_ref[pl.ds(pad, t), :] = u
        xpad_ref[pl.ds(pad + t, 40 - pad), :] = jnp.zeros((40 - pad, cb), F32)
        _fill_shifts(xpad_ref, sh_ref, t, shifts)

        if silu:
            def act_chunk(i, carry):
                t0 = pl.multiple_of(i * ch, ch)
                acc = jnp.broadcast_to(b_ref[...], (ch, cb))
                for k in range(width):
                    q, r = divmod(k + off, 8)
                    acc = acc + w_ref[pl.ds(k, 1), :] * sh_ref[r, pl.ds(t0 + 8 * q, ch), :]
                sg = jax.nn.sigmoid(acc)
                dc_ref[pl.ds(t0, ch), :] = dy_ref[pl.ds(t0, ch), :] * (sg * (1.0 + acc * (1.0 - sg)))
                return carry

            lax.fori_loop(0, t // ch, act_chunk, 0)
        else:
            dc_ref[...] = dy_ref[...]

        def dw_chunk(i, accs):
            t0 = pl.multiple_of(i * 8, 8)
            d = dc_ref[pl.ds(t0, 8), :]
            new = []
            for k in range(width):
                q, r = divmod(k + off, 8)
                new.append(accs[k] + d * sh_ref[r, pl.ds(t0 + 8 * q, 8), :])
            new.append(accs[width] + d)
            return tuple(new)

        accs = lax.fori_loop(0, t // 8, dw_chunk, tuple(jnp.zeros((8, cb), F32) for _ in range(width + 1)))
        for k in range(width):
            dw_ref[pl.ds(k, 1), :] = jnp.sum(accs[k], axis=0, keepdims=True)
        db_ref[...] = jnp.sum(accs[width], axis=0, keepdims=True)

        xpad_ref[pl.ds(0, t), :] = dc_ref[...]
        xpad_ref[pl.ds(t, 40), :] = jnp.zeros((40, cb), F32)
        _fill_shifts(xpad_ref, sh_ref, t, shifts_t)

        def dx_chunk(i, carry):
            t0 = pl.multiple_of(i * ch, ch)
            acc = jnp.zeros((ch, cb), F32)
            for mm in range(width):
                q, r = divmod(mm, 8)
                acc = acc + w_ref[pl.ds(width - 1 - mm, 1), :] * sh_ref[r, pl.ds(t0 + 8 * q, ch), :]
            if glu:
                val, gate = x_refs[0][pl.ds(t0, ch), :], x_refs[1][pl.ds(t0, ch), :]
                sg = jax.nn.sigmoid(gate)
                dx_refs[0][pl.ds(t0, ch), :] = (acc * sg).astype(BF16)
                dx_refs[1][pl.ds(t0, ch), :] = (acc * val * sg * (1.0 - sg)).astype(BF16)
            else:
                dx_refs[0][pl.ds(t0, ch), :] = acc.astype(BF16)
            return carry

        lax.fori_loop(0, t // ch, dx_chunk, 0)

    col = pl.BlockSpec((t, cb), lambda i: (0, i))
    return _pallas(
        body, name=name, grid=(c // cb,),
        in_specs=[pl.BlockSpec((t, cb), lambda i, o=o: (0, o + i)) for _, o in xs]
        + [pl.BlockSpec((width, cb), lambda i: (0, i)), pl.BlockSpec((1, cb), lambda i: (0, i)), col],
        out_specs=[col] * nx + [pl.BlockSpec((width, cb), lambda i: (0, i)), pl.BlockSpec((1, cb), lambda i: (0, i))],
        out_shape=[SDS((t, c), BF16)] * nx + [SDS((width, c), F32), SDS((1, c), F32)],
        scratch_shapes=[pltpu.VMEM((t + 40, cb), F32), pltpu.VMEM((8, t + 32, cb), F32), pltpu.VMEM((t, cb), F32)],
        compiler_params=_params(("parallel",)),
    )(*[x for x, _ in xs], w, b, dy)


_DIMS = {"nn": (((1,), (0,)), ((), ())), "nt": (((1,), (1,)), ((), ())), "tn": (((0,), (0,)), ((), ()))}


def _raw_dot(a, b, mode):
    return lax.dot_general(a.astype(BF16), b.astype(BF16), _DIMS[mode], preferred_element_type=F32)


@functools.partial(jax.custom_vjp, nondiff_argnums=(2,))
def _bdot(a, b, mode):
    return _raw_dot(a, b, mode)


def _bdot_fwd(a, b, mode):
    return _raw_dot(a, b, mode), (a, b)


def _bdot_bwd(mode, res, g):
    a, b = res
    if mode == "nn":
        return _raw_dot(g, b, "nt"), _raw_dot(a, g, "tn")
    if mode == "nt":
        return _raw_dot(g, b, "nn"), _raw_dot(g, a, "tn")
    return _raw_dot(b, g, "nt"), _raw_dot(a, g, "nn")


_bdot.defvjp(_bdot_fwd, _bdot_bwd)


def _iota(shape, axis):
    return lax.broadcasted_iota(jnp.int32, shape, axis)


def _half_masks():
    left = (_iota((1, LANE), 1) < 64).astype(F32)
    return left, 1.0 - left


def _ssd_chunk(state, xa, dtr, z, dtb, alog, dsk, ng):
    xs, bm, cm = xa[:, :D], xa[:, D:D + 256], xa[:, D + 256:]
    left, right = _half_masks()
    expand = (_iota((LANE, D), 1) // 64 == _iota((LANE, D), 0)).astype(F32)
    li, si = _iota((CHUNK, CHUNK), 0), _iota((CHUNK, CHUNK), 1)
    tril = li >= si
    dt16 = jax.nn.softplus(dtr + dtb)
    adt = dt16 * (-jnp.exp(alog))
    dtf = jnp.dot(dt16, expand, precision=HI)
    cs16 = jnp.dot(tril.astype(F32), adt, precision=HI)
    csf = jnp.dot(cs16, expand, precision=HI)
    totf = jnp.sum(jnp.dot(adt, expand, precision=HI), axis=0, keepdims=True)
    cst = cs16.T
    xdt = xs * dtf
    ys, new_state = [], []
    for g in range(2):
        bg, cg = bm[:, LANE * g:LANE * (g + 1)], cm[:, LANE * g:LANE * (g + 1)]
        cb = _bdot(cg, bg, "nt")
        for q in range(4):
            pr = 4 * g + q
            decay = []
            for h in (2 * pr, 2 * pr + 1):
                col = jnp.sum(jnp.where(si == h, cs16, 0.0), axis=1, keepdims=True)
                row = jnp.sum(jnp.where(li == h, cst, 0.0), axis=0, keepdims=True)
                decay.append(cb * jnp.exp(jnp.where(tril, col - row, -jnp.inf)))
            xp = xdt[:, LANE * pr:LANE * (pr + 1)]
            y_diag = _bdot(jnp.concatenate(decay, axis=1), jnp.concatenate([xp * left, xp * right], axis=0), "nn")
            csb, tot = csf[:, LANE * pr:LANE * (pr + 1)], totf[:, LANE * pr:LANE * (pr + 1)]
            ys.append(y_diag + _bdot(cg, state[pr], "nn") * jnp.exp(csb))
            new_state.append(state[pr] * jnp.exp(tot) + _bdot(bg, xp * jnp.exp(tot - csb), "tn"))
    y = jnp.concatenate(ys, axis=1)
    y = y + jnp.dot(jnp.broadcast_to(dsk, (CHUNK, LANE)), expand, precision=HI) * xs
    y = y * (z * jax.nn.sigmoid(z))
    halves = []
    for g in range(2):
        yg = y[:, 512 * g:512 * (g + 1)]
        halves.append(yg * lax.rsqrt(jnp.mean(yg * yg, axis=-1, keepdims=True) + EPS))
    return jnp.concatenate(halves, axis=1) * ng, jnp.stack(new_state, axis=0)


def _ssd_specs(t, rev):
    nc = t // CHUNK
    ix = (lambda c: nc - 1 - c) if rev else (lambda c: c)
    return nc, ix


def _ssd_fwd(xa, proj, dtb, alog, dsk, ng, name):
    t = xa.shape[0]
    nc, ix = _ssd_specs(t, False)

    def body(xa_ref, dt_ref, z_ref, dtb_ref, alog_ref, dsk_ref, ng_ref, y_ref, st_ref, carry_ref):
        @pl.when(pl.program_id(0) == 0)
        def _():
            carry_ref[...] = jnp.zeros_like(carry_ref)

        st_ref[...] = carry_ref[...]
        y, new = _ssd_chunk(carry_ref[...], xa_ref[...], dt_ref[...], z_ref[...], dtb_ref[...], alog_ref[...],
                            dsk_ref[...], ng_ref[...])
        y_ref[...] = y.astype(BF16)
        carry_ref[...] = new

    small = [dtb, alog, dsk, ng]
    return _pallas(
        body, name=name, grid=(nc,),
        in_specs=[pl.BlockSpec((CHUNK, SSM_XBC), lambda c: (c, 0)),
                  pl.BlockSpec((CHUNK, LANE), lambda c: (c, DT_COL // LANE)),
                  pl.BlockSpec((CHUNK, D), lambda c: (c, 2))] + [_whole(p) for p in small],
        out_specs=[pl.BlockSpec((CHUNK, D), lambda c: (c, 0)), pl.BlockSpec((None, N_PAIR, LANE, LANE), lambda c: (c, 0, 0, 0))],
        out_shape=[SDS((t, D), BF16), SDS((nc, N_PAIR, LANE, LANE), F32)],
        scratch_shapes=[pltpu.VMEM((N_PAIR, LANE, LANE), F32)],
        compiler_params=_params(("arbitrary",)),
    )(xa, proj, proj, *small)


def _ssd_bwd(xa, proj, states, dy, dtb, alog, dsk, ng, name):
    t = xa.shape[0]
    nc, ix = _ssd_specs(t, True)

    def body(xa_ref, dt_ref, z_ref, st_ref, dy_ref, dtb_ref, alog_ref, dsk_ref, ng_ref,
             dxa_ref, ddt_ref, dz_ref, gdtb_ref, galog_ref, gdsk_ref, gng_ref, carry_ref):
        first = pl.program_id(0) == 0

        @pl.when(first)
        def _():
            carry_ref[...] = jnp.zeros_like(carry_ref)

        args = (st_ref[...], xa_ref[...], dt_ref[...], z_ref[...], dtb_ref[...], alog_ref[...], dsk_ref[...], ng_ref[...])
        _, vjp = jax.vjp(_ssd_chunk, *args)
        ds, dxa, ddt, dz, gdtb, galog, gdsk, gng = vjp((dy_ref[...], carry_ref[...]))
        carry_ref[...] = ds
        dxa_ref[...] = dxa
        ddt_ref[...] = ddt.astype(BF16)
        dz_ref[...] = dz.astype(BF16)
        for r, g in ((gdtb_ref, gdtb), (galog_ref, galog), (gdsk_ref, gdsk), (gng_ref, gng)):
            @pl.when(first)
            def _(r=r, g=g):
                r[...] = g

            @pl.when(jnp.logical_not(first))
            def _(r=r, g=g):
                r[...] += g

    small = [dtb, alog, dsk, ng]
    return _pallas(
        body, name=name, grid=(nc,),
        in_specs=[pl.BlockSpec((CHUNK, SSM_XBC), lambda c: (ix(c), 0)),
                  pl.BlockSpec((CHUNK, LANE), lambda c: (ix(c), DT_COL // LANE)),
                  pl.BlockSpec((CHUNK, D), lambda c: (ix(c), 2)),
                  pl.BlockSpec((None, N_PAIR, LANE, LANE), lambda c: (ix(c), 0, 0, 0)),
                  pl.BlockSpec((CHUNK, D), lambda c: (ix(c), 0))] + [_whole(p) for p in small],
        out_specs=[pl.BlockSpec((CHUNK, SSM_XBC), lambda c: (ix(c), 0)), pl.BlockSpec((CHUNK, LANE), lambda c: (ix(c), 0)),
                   pl.BlockSpec((CHUNK, D), lambda c: (ix(c), 0))] + [_whole(p) for p in small],
        out_shape=[SDS((t, SSM_XBC), F32), SDS((t, LANE), BF16), SDS((t, D), BF16)] + [SDS(p.shape, F32) for p in small],
        scratch_shapes=[pltpu.VMEM((N_PAIR, LANE, LANE), F32)],
        compiler_params=_params(("arbitrary",)),
    )(xa, proj, proj, states, dy, *small)


def _attn_block(q, kv_prev, kv_cur, cq, sq, ck, sk, sinks, rot, first_block):
    left, right = _half_masks()
    k2 = jnp.concatenate([kv_prev[:, :256], kv_cur[:, :256]], axis=0)
    v2 = jnp.concatenate([kv_prev[:, 256:], kv_cur[:, 256:]], axis=0)
    ri, ci = _iota((LANE, LANE), 0), _iota((LANE, LANE), 1)
    dup = [((ri < 64) & (ci % 64 == ri)).astype(BF16), ((ri >= 64) & (ci % 64 == ri - 64)).astype(BF16)]

    def rope(tt, c, s):
        return tt * c + jnp.dot(tt, rot, precision=HI) * s

    kd, vd = [], []
    for j in range(4):
        sl = slice(LANE * (j // 2), LANE * (j // 2 + 1))
        kd.append(_bdot(rope(k2[:, sl], ck, sk), dup[j % 2], "nn"))
        vd.append(_bdot(v2[:, sl], dup[j % 2], "nn"))
    qi, si = _iota((2 * CHUNK, 2 * CHUNK), 0) % CHUNK, _iota((2 * CHUNK, 2 * CHUNK), 1)
    valid = (si > qi) & (si <= qi + CHUNK) & jnp.logical_or(si >= CHUNK, jnp.logical_not(first_block))
    upper = _iota((2 * CHUNK, 1), 0) < CHUNK
    lanes = _iota((1, LANE), 1)
    outs = []
    for pr in range(N_PAIR):
        qr = rope(q[:, LANE * pr:LANE * (pr + 1)], cq, sq)
        lg = _bdot(jnp.concatenate([qr * left, qr * right], axis=0), kd[pr // 2], "nt") * 0.125
        lg = jnp.where(valid, lg, -jnp.inf)
        s1 = jnp.sum(jnp.where(lanes == 2 * pr, sinks, 0.0), axis=1, keepdims=True)
        s2 = jnp.sum(jnp.where(lanes == 2 * pr + 1, sinks, 0.0), axis=1, keepdims=True)
        sink = jnp.where(upper, s1, s2)
        mx = lax.stop_gradient(jnp.maximum(jnp.max(lg, axis=-1, keepdims=True), sink))
        e = jnp.exp(lg - mx)
        probs = e / (jnp.sum(e, axis=-1, keepdims=True) + jnp.exp(sink - mx))
        o2 = _bdot(probs, vd[pr // 2], "nn")
        outs.append(o2[:CHUNK] * left + o2[CHUNK:] * right)
    return jnp.concatenate(outs, axis=1)


def _attn_fwd(qkv, cos, sin, sinks, rot, name):
    t = qkv.shape[0]
    nb = t // CHUNK

    def body(q_ref, kvp_ref, kvc_ref, cq_ref, sq_ref, cp_ref, sp_ref, sinks_ref, rot_ref, o_ref):
        ck = jnp.concatenate([cp_ref[...], cq_ref[...]], axis=0)
        sk = jnp.concatenate([sp_ref[...], sq_ref[...]], axis=0)
        o_ref[...] = _attn_block(q_ref[...], kvp_ref[...], kvc_ref[...], cq_ref[...], sq_ref[...], ck, sk,
                                 sinks_ref[...], rot_ref[...], pl.program_id(0) == 0).astype(BF16)

    prev = lambda n: jnp.maximum(n - 1, 0)
    return _pallas(
        body, name=name, grid=(nb,),
        in_specs=[pl.BlockSpec((CHUNK, D), lambda n: (n, 0)),
                  pl.BlockSpec((CHUNK, 512), lambda n: (prev(n), 2)), pl.BlockSpec((CHUNK, 512), lambda n: (n, 2)),
                  pl.BlockSpec((CHUNK, LANE), lambda n: (n, 0)), pl.BlockSpec((CHUNK, LANE), lambda n: (n, 0)),
                  pl.BlockSpec((CHUNK, LANE), lambda n: (prev(n), 0)), pl.BlockSpec((CHUNK, LANE), lambda n: (prev(n), 0)),
                  _whole(sinks), _whole(rot)],
        out_specs=pl.BlockSpec((CHUNK, D), lambda n: (n, 0)), out_shape=SDS((t, D), BF16),
        compiler_params=_params(("parallel",)),
    )(qkv, qkv, qkv, cos, sin, cos, sin, sinks, rot)


def _attn_bwd(qkv, do, cos, sin, sinks, rot, name):
    t = qkv.shape[0]
    nb = t // CHUNK

    def body(q_ref, kvp_ref, kvc_ref, do_ref, cq_ref, sq_ref, cp_ref, sp_ref, sinks_ref, rot_ref,
             dq_ref, dkv_ref, dbq_ref, dbkv_ref, dsink_ref, carry_ref):
        n = pl.program_id(0)

        @pl.when(n == 0)
        def _():
            carry_ref[...] = jnp.zeros_like(carry_ref)
            dbq_ref[...] = jnp.zeros_like(dbq_ref)
            dbkv_ref[...] = jnp.zeros_like(dbkv_ref)
            dsink_ref[...] = jnp.zeros_like(dsink_ref)

        @pl.when(n < nb)
        def _():
            ck = jnp.concatenate([cp_ref[...], cq_ref[...]], axis=0)
            sk = jnp.concatenate([sp_ref[...], sq_ref[...]], axis=0)
            f = lambda q, kvp, kvc, s: _attn_block(q, kvp, kvc, cq_ref[...], sq_ref[...], ck, sk, s, rot_ref[...], n == 0)
            _, vjp = jax.vjp(f, q_ref[...], kvp_ref[...], kvc_ref[...], sinks_ref[...])
            dq, dkvp, dkvc, ds = vjp(do_ref[...].astype(F32))
            done = carry_ref[...] + dkvp
            dq_ref[...] = dq.astype(BF16)
            dkv_ref[...] = done.astype(BF16)
            dbq_ref[...] += jnp.sum(dq, axis=0, keepdims=True)
            dsink_ref[...] += ds
            carry_ref[...] = dkvc

            @pl.when(n > 0)
            def _():
                dbkv_ref[...] += jnp.sum(done, axis=0, keepdims=True)

        @pl.when(n == nb)
        def _():
            done = carry_ref[...]
            dkv_ref[...] = done.astype(BF16)
            dbkv_ref[...] += jnp.sum(done, axis=0, keepdims=True)

    cur = lambda n: jnp.minimum(n, nb - 1)
    prev = lambda n: jnp.maximum(jnp.minimum(n, nb - 1) - 1, 0)
    fin = lambda n: jnp.maximum(n - 1, 0)
    outs = _pallas(
        body, name=name, grid=(nb + 1,),
        in_specs=[pl.BlockSpec((CHUNK, D), lambda n: (cur(n), 0)),
                  pl.BlockSpec((CHUNK, 512), lambda n: (prev(n), 2)), pl.BlockSpec((CHUNK, 512), lambda n: (cur(n), 2)),
                  pl.BlockSpec((CHUNK, D), lambda n: (cur(n), 0)),
                  pl.BlockSpec((CHUNK, LANE), lambda n: (cur(n), 0)), pl.BlockSpec((CHUNK, LANE), lambda n: (cur(n), 0)),
                  pl.BlockSpec((CHUNK, LANE), lambda n: (prev(n), 0)), pl.BlockSpec((CHUNK, LANE), lambda n: (prev(n), 0)),
                  _whole(sinks), _whole(rot)],
        out_specs=[pl.BlockSpec((CHUNK, D), lambda n: (cur(n), 0)), pl.BlockSpec((CHUNK, 512), lambda n: (fin(n), 0)),
                   pl.BlockSpec((1, D), lambda n: (0, 0)), pl.BlockSpec((1, 512), lambda n: (0, 0)), _whole(sinks)],
        out_shape=[SDS((t, D), BF16), SDS((t, 512), BF16), SDS((1, D), F32), SDS((1, 512), F32), SDS(sinks.shape, F32)],
        scratch_shapes=[pltpu.VMEM((CHUNK, 512), F32)],
        compiler_params=_params(("arbitrary",)),
    )(qkv, qkv, qkv, do, cos, sin, cos, sin, sinks, rot)
    dq, dkv, dbq, dbkv, dsinks = outs
    return jnp.concatenate([dq, dkv], axis=1), jnp.concatenate([dbq, dbkv], axis=1), dsinks


def _loss_head(h, tgt, g, name, tm=512):
    t = h.shape[0]

    def body(h_ref, t_ref, g_ref, loss_ref, dh_ref, dhb_ref, dg_ref):
        def f(hv, gv):
            err = _rms_fn(hv, gv)[0] - t_ref[...]
            return 0.5 * jnp.sum(jnp.mean(err * err, axis=-1, keepdims=True), axis=0, keepdims=True)

        loss, vjp = jax.vjp(f, h_ref[...], g_ref[...])
        dh, dg = vjp(jnp.ones((1, 1), F32))
        dh_ref[...] = dh
        dhb_ref[...] = dh.astype(BF16)
        first = pl.program_id(0) == 0

        @pl.when(first)
        def _():
            loss_ref[...] = loss
            dg_ref[...] = dg

        @pl.when(jnp.logical_not(first))
        def _():
            loss_ref[...] += loss
            dg_ref[...] += dg

    return _pallas(
        body, name=name, grid=(t // tm,),
        in_specs=[_tok(D, tm), _tok(D, tm), _whole(g)],
        out_specs=[pl.BlockSpec((1, 1), lambda i: (0, 0)), _tok(D, tm), _tok(D, tm), _whole(g)],
        out_shape=[SDS((1, 1), F32), SDS((t, D), F32), SDS((t, D), BF16), SDS(g.shape, F32)],
        compiler_params=_params(("arbitrary",)),
    )(h, tgt, g)


def _res_half(acc, res):
    return (res + 0.5 * acc,)


def _res_full(acc, res):
    return (res + acc,)


def _half(acc):
    return (0.5 * acc,)


def _ffn_in(n, w_in, name, tm=1024):
    t = n.shape[0]
    tm = min(tm, t)

    def body(n_ref, w_ref, pre_ref, act_ref):
        a = n_ref[...]
        gate = jnp.dot(a, w_ref[0], preferred_element_type=F32)
        up = jnp.dot(a, w_ref[1], preferred_element_type=F32)
        pre_ref[0] = gate
        pre_ref[1] = up
        act_ref[...] = (gate * jax.nn.sigmoid(gate) * up).astype(BF16)

    pair = pl.BlockSpec((2, None, tm, FF_SHARD), lambda i, j: (0, j, i, 0))
    return _pallas(
        body, name=name, grid=(t // tm, 4),
        in_specs=[pl.BlockSpec((tm, D), lambda i, j: (i, 0)), pl.BlockSpec((2, None, D, FF_SHARD), lambda i, j: (0, j, 0, 0))],
        out_specs=[pair, pl.BlockSpec((None, tm, FF_SHARD), lambda i, j: (j, i, 0))],
        out_shape=[SDS((2, 4, t, FF_SHARD), F32), SDS((4, t, FF_SHARD), BF16)],
        compiler_params=_params(("parallel", "parallel")),
    )(n, w_in.reshape(2, 4, D, FF_SHARD))


def _ffn_dact(dhb, w_out, pre, name, tm=1024, deps=()):
    t = dhb.shape[0]
    tm = min(tm, t)

    def body(d_ref, w_ref, pre_ref, *rest):
        o_ref = rest[-1]
        dact = 0.5 * lax.dot_general(d_ref[...], w_ref[...], _DIMS["nt"], preferred_element_type=F32)
        gate, up = pre_ref[0], pre_ref[1]
        sg = jax.nn.sigmoid(gate)
        o_ref[0] = (dact * up * (sg * (1.0 + gate * (1.0 - sg)))).astype(BF16)
        o_ref[1] = (dact * (gate * sg)).astype(BF16)

    pair = pl.BlockSpec((2, None, tm, FF_SHARD), lambda i, j: (0, j, i, 0))
    return _pallas(
        body, name=name, grid=(t // tm, 4),
        in_specs=[pl.BlockSpec((tm, D), lambda i, j: (i, 0)), pl.BlockSpec((None, FF_SHARD, D), lambda i, j: (j, 0, 0)), pair]
        + [ANY_SPEC] * len(deps),
        out_specs=pair, out_shape=SDS((2, 4, t, FF_SHARD), BF16),
        compiler_params=_params(("parallel", "parallel")),
    )(dhb, w_out, pre, *deps)


def _ffn_fwd(h, g, w_in, w_out, tag, deps=()):
    n = _rms(h, g, f"{tag}_rms", deps=deps)
    pre, act = _ffn_in(n, w_in, f"{tag}_in")
    out = _mm(act, w_out, reduce_j=True, tk=FF_SHARD, epi=_res_half, extras=(h[None],), name=f"{tag}_out")[0][0]
    return out, (h, n, pre, act)


def _ffn_bwd(dh, dhb, saved, g, w_in, w_out, tag, deps=(), hook=None):
    h, n, pre, act = saved
    t = h.shape[0]
    dpre = _ffn_dact(dhb, w_out, pre, f"{tag}_dact", deps=deps).reshape(N_DEV, t, FF_SHARD)
    dw_out = _mm(act, dhb[None], ta=True, tm=FF_SHARD, epi=_half, out_dtypes=(BF16,), deps=hook(dpre) if hook else (),
                 name=f"{tag}_dwout")[0]
    dh_in, dhb_in, dg = _mm_drms(dpre, w_in, h, g, dh, f"{tag}_dn", FF_SHARD)
    dw_in = _mm(n[None], dpre, ta=True, tn=FF_SHARD, out_dtypes=(BF16,), name=f"{tag}_dwin")[0]
    return dh_in, dhb_in, dg, dw_in, dw_out


def _ple_fwd(h, g, pb, w_gate, w_proj, tag):
    t = h.shape[0]
    tm = 512
    n = _rms(h, g, f"{tag}_rms")
    e = _mm(pb[None], w_proj[None], name=f"{tag}_proj")[0][0]
    z = _mm(n[None], w_gate[None], name=f"{tag}_gate")[0][0]
    out = _rowop(lambda zz, ee, hh: (hh + _ple_fn(zz, ee)[0],), [(z, _tok(D, tm)), (e, _tok(D, tm)), (h, _tok(D, tm))], [],
                 [((t, D), F32, _tok(D, tm))], grid=(t // tm,), name=f"{tag}_mix")[0]
    return out, (h, n, e, z)


def _ple_bwd(dh, dhb, saved, g, pb, w_gate, tag, deps=()):
    h, n, e, z = saved
    t = h.shape[0]
    tm = 512
    dz, de = _rowop_bwd(_ple_fn, [(z, _tok(D, tm)), (e, _tok(D, tm))], [], [(dh, _tok(D, tm))], [(0,), (1,)],
                        [((t, D), (BF16,), _tok(D, tm)), ((t, D), (BF16,), _tok(D, tm))], grid=(t // tm,), name=f"{tag}_dmix",
                        deps=deps)
    dw_proj = _mm(pb[None], de[None], ta=True, out_dtypes=(BF16,), name=f"{tag}_dwproj")[0][0]
    dw_gate = _mm(n[None], dz[None], ta=True, out_dtypes=(BF16,), name=f"{tag}_dwgate")[0][0]
    dh_in, dhb_in, dg = _mm_drms(dz[None], w_gate[None], h, g, dh, f"{tag}_dn", 1024)
    return dh_in, dhb_in, dg, dw_gate, dw_proj


def _hyb_fwd(h, w, tag):
    t = h.shape[0]
    tm = 512
    hn = _rms(h, w["norm_mix"], f"{tag}_rms")
    proj = _mm(hn[None], w["hyb_in"][None], tn=512, name=f"{tag}_in")[0][0]
    u1 = _dwconv([(proj, 0), (proj, D // LANE)], w["conv_w"], w["conv_b"], width=CONV_W, glu=True, silu=False, cb=LANE,
                 name=f"{tag}_conv")
    u = _rowop(_lnswish_fn, [(u1, _tok(D, tm))], [w["ln_g"], w["ln_b"]], [((t, D), BF16, _tok(D, tm))], grid=(t // tm,),
               name=f"{tag}_ln")[0]
    xa = _dwconv([(proj, 3 * D // LANE)], w["sconv_w"], w["sconv_b"], width=SSM_CONV, glu=False, silu=True, cb=LANE,
                 name=f"{tag}_sconv")
    y, states = _ssd_fwd(xa, proj, w["dt_bias"], w["a_log"], w["d_skip"], w["ssm_norm"], f"{tag}_ssd")
    mixed = jnp.stack([u, y], axis=0)
    out = _mm(mixed, w["hyb_out"], reduce_j=True, epi=_res_full, extras=(h[None],), name=f"{tag}_out")[0][0]
    return out, (h, hn, proj, u1, xa, states, mixed)


def _hyb_bwd(dh, dhb, saved, w, tag):
    h, hn, proj, u1, xa, states, mixed = saved
    t = h.shape[0]
    tm = 512
    dmix = _mm(dhb[None], w["hyb_out"], tb=True, name=f"{tag}_dmix")[0]
    dw_out = _mm(mixed, dhb[None], ta=True, out_dtypes=(BF16,), name=f"{tag}_dwout")[0]
    du1, dln_g, dln_b = _rowop_bwd(_lnswish_fn, [(u1, _tok(D, tm))], [w["ln_g"], w["ln_b"]], [(dmix[0], _tok(D, tm))], [(0,)],
                                   [((t, D), (F32,), _tok(D, tm))], grid=(t // tm,), name=f"{tag}_dln")
    dval, dgate, dconv_w, dconv_b = _dwconv_bwd([(proj, 0), (proj, D // LANE)], w["conv_w"], w["conv_b"], du1,
                                                width=CONV_W, glu=True, silu=False, cb=LANE, name=f"{tag}_dconv")
    dxa, ddt, dz, g_dtb, g_alog, g_dsk, g_ng = _ssd_bwd(xa, proj, states, dmix[1], w["dt_bias"], w["a_log"], w["d_skip"],
                                                         w["ssm_norm"], f"{tag}_dssd")
    dxbc, dsconv_w, dsconv_b = _dwconv_bwd([(proj, 3 * D // LANE)], w["sconv_w"], w["sconv_b"], dxa, width=SSM_CONV,
                                           glu=False, silu=True, cb=LANE, name=f"{tag}_dsconv")
    dproj = jnp.concatenate([dval, dgate, dz, dxbc, ddt, jnp.zeros((t, HYB_PAD - DT_COL - LANE), BF16)], axis=1)
    dh_in, dhb_in, dg = _mm_drms(dproj[None], w["hyb_in"][None], h, w["norm_mix"], dh, f"{tag}_dhn", 1024)
    dw_in = _mm(hn[None], dproj[None], ta=True, tn=512, out_dtypes=(BF16,), name=f"{tag}_dwin")[0][0]
    grads = dict(norm_mix=dg, hyb_in=dw_in, hyb_out=dw_out, conv_w=dconv_w, conv_b=dconv_b, ln_g=dln_g, ln_b=dln_b,
                 sconv_w=dsconv_w, sconv_b=dsconv_b, dt_bias=g_dtb, a_log=g_alog, d_skip=g_dsk, ssm_norm=g_ng)
    return dh_in, dhb_in, grads


def _bias_epi(acc, row):
    return (acc + row,)


def _res_bias_epi(acc, res, row):
    return (res + acc + row,)


def _att_fwd(h, w, tables, tag):
    cos, sin, rot = tables
    hn = _rms(h, w["norm_mix"], f"{tag}_rms")
    qkv = _mm(hn[None], w["qkv"][None], tn=512, epi=_bias_epi, rows=(w["b_qkv"],), name=f"{tag}_qkv")[0][0]
    o = _attn_fwd(qkv, cos, sin, w["sinks"], rot, f"{tag}_core")
    out = _mm(o[None], w["w_o"][None], epi=_res_bias_epi, extras=(h[None],), rows=(w["b_o"],), name=f"{tag}_out")[0][0]
    return out, (h, hn, qkv, o)


def _att_bwd(dh, dhb, saved, w, tables, tag):
    cos, sin, rot = tables
    h, hn, qkv, o = saved
    t = h.shape[0]
    tm = 512
    do = _mm(dhb[None], w["w_o"][None], tb=True, out_dtypes=(BF16,), name=f"{tag}_do")[0][0]
    dw_o = _mm(o[None], dhb[None], ta=True, out_dtypes=(BF16,), name=f"{tag}_dwo")[0][0]
    db_o = _rowop_bwd(lambda xx, bb: (xx + bb,), [(dh, _tok(D, tm))], [w["b_o"]], [(dh, _tok(D, tm))], [], [],
                      grid=(t // tm,), name=f"{tag}_dbo")[0]
    dqkv, db_qkv, dsinks = _attn_bwd(qkv, do, cos, sin, w["sinks"], rot, f"{tag}_dcore")
    dh_in, dhb_in, dg = _mm_drms(dqkv[None], w["qkv"][None], h, w["norm_mix"], dh, f"{tag}_dhn", 512)
    dw_qkv = _mm(hn[None], dqkv[None], ta=True, tn=512, out_dtypes=(BF16,), name=f"{tag}_dwqkv")[0][0]
    grads = dict(norm_mix=dg, qkv=dw_qkv, b_qkv=db_qkv, sinks=dsinks, w_o=dw_o, b_o=db_o)
    return dh_in, dhb_in, grads


def _rope_tables(t):
    inv = ROPE_THETA ** (-jnp.arange(0, 64, 2, dtype=F32) / 64)
    ang = jnp.arange(t, dtype=F32)[:, None] * inv[None, :]
    cos, sin = jnp.tile(jnp.cos(ang), (1, 4)), jnp.tile(jnp.sin(ang), (1, 4))
    rot = np.zeros((LANE, LANE), np.float32)
    for j in range(LANE):
        if j % 64 < 32:
            rot[j + 32, j] = -1.0
        else:
            rot[j - 32, j] = 1.0
    return cos, sin, jnp.asarray(rot)


def _local_step(x, p, tgt, layers, final_norm):
    _restart_chain()
    tables = _rope_tables(x.shape[0])
    pb = p.astype(BF16)
    h, saved = x, []
    for i, w in enumerate(layers):
        h, s = _layer_fwd(i, h, w, pb[i], tables)
        saved.append(s)
    loss, dh, dhb, d_final = _loss_head(h, tgt, final_norm, "loss_head")
    grads = [None] * len(layers)
    for i in reversed(range(len(layers))):
        dh, dhb, head = _layer_bwd_head(i, dh, dhb, saved[i], layers[i], pb[i])
        dh, dhb, tail = _layer_bwd_tail(i, dh, dhb, saved[i], layers[i], tables)
        grads[i] = {**head, **tail}
    return loss[0, 0], dh, grads, d_final


def _layer_fwd(i, h, w, pb, tables, deps=()):
    s = {}
    h, s["ffn1"] = _ffn_fwd(h, w["norm_ffn1"], w["ffn1_in"], w["ffn1_out"], f"l{i}_ffn1", deps=deps)
    if i % 2 == 0:
        h, s["mix"] = _hyb_fwd(h, w, f"l{i}_hyb")
    else:
        h, s["mix"] = _att_fwd(h, w, tables, f"l{i}_att")
    h, s["ffn2"] = _ffn_fwd(h, w["norm_ffn2"], w["ffn2_in"], w["ffn2_out"], f"l{i}_ffn2")
    h, s["ple"] = _ple_fwd(h, w["ple_norm"], pb, w["ple_gate"], w["ple_proj"], f"l{i}_ple")
    return h, s


def _layer_bwd_head(i, dh, dhb, s, w, pb, deps=()):
    g = {}
    dh, dhb, g["ple_norm"], g["ple_gate"], g["ple_proj"] = _ple_bwd(dh, dhb, s["ple"], w["ple_norm"], pb, w["ple_gate"],
                                                                    f"l{i}_ple", deps=deps)
    return dh, dhb, g


def _layer_bwd_tail(i, dh, dhb, s, w, tables, deps=()):
    g = {}
    dh, dhb, g["norm_ffn2"], g["ffn2_in"], g["ffn2_out"] = _ffn_bwd(dh, dhb, s["ffn2"], w["norm_ffn2"], w["ffn2_in"],
                                                                    w["ffn2_out"], f"l{i}_ffn2", deps=deps)
    if i % 2 == 0:
        dh, dhb, gm = _hyb_bwd(dh, dhb, s["mix"], w, f"l{i}_hyb")
    else:
        dh, dhb, gm = _att_bwd(dh, dhb, s["mix"], w, tables, f"l{i}_att")
    g.update(gm)
    dh, dhb, g["norm_ffn1"], g["ffn1_in"], g["ffn1_out"] = _ffn_bwd(dh, dhb, s["ffn1"], w["norm_ffn1"], w["ffn1_in"],
                                                                    w["ffn1_out"], f"l{i}_ffn1")
    return dh, dhb, g


def _cols(g):
    full = jnp.moveaxis(g, 0, -2)
    return full.reshape(*full.shape[:-2], N_DEV * g.shape[-1])


def _uncols(full):
    split = full.reshape(*full.shape[:-1], N_DEV, full.shape[-1] // N_DEV)
    return jnp.moveaxis(split, -2, 0)


def _lane_pad(v):
    return jnp.pad(v, ((0, 0), (0, LANE - v.shape[1])))


def _build_layers(gw, gs, rep):
    return [_build_layer(i, gw, gs, rep) for i in range(2)]


def _build_layer(i, gw, gs, rep, parts=("ffn1", "mix", "ffn2", "ple")):
    w = {}
    for f in ("ffn1", "ffn2"):
        if f in parts:
            w[f"norm_{f}"] = rep[f"norm_{f}"][i][None]
            w[f"{f}_in"] = gw[f"{f}_w_in", i]
            w[f"{f}_out"] = gw[f"{f}_w_out", i].reshape(4, FF_SHARD, D)
    if "ple" in parts:
        w["ple_norm"] = rep["ple_norm"][i][None]
        w["ple_gate"] = gw["ple_gate_w", i].reshape(D, D)
        w["ple_proj"] = _cols(gw["ple_proj_w", i])
    if "mix" not in parts:
        return w
    w["norm_mix"] = rep["norm_mix"][i][None]
    if i == 0:
        w["hyb_in"] = jnp.pad(_cols(gw["hyb_w_in", 0]), ((0, 0), (0, HYB_PAD - HYB_IN)))
        w["hyb_out"] = gw["hyb_w_out", 0].reshape(2, D, D)
        w["conv_w"] = _cols(gs["conv_dw_w"][:, 0])
        w["sconv_w"] = _cols(gs["ssm_conv_w"][:, 0])
        w["conv_b"], w["ln_g"], w["ln_b"] = rep["conv_dw_b"], rep["conv_ln_g"], rep["conv_ln_b"]
        w["sconv_b"], w["ssm_norm"] = rep["ssm_conv_b"], rep["ssm_norm"]
        w["dt_bias"], w["a_log"], w["d_skip"] = (_lane_pad(rep[k]) for k in ("ssm_dt_bias", "ssm_a_log", "ssm_d"))
    else:
        w["qkv"] = _cols(gw["att_w_qkv", 0])
        w["w_o"] = gw["att_w_o", 0].reshape(D, D)
        w["b_qkv"] = gs["att_b_qkv"][:, 0].reshape(1, -1)
        w["b_o"] = gs["att_b_o"][:, 0].reshape(1, -1)
        w["sinks"] = _lane_pad(rep["att_sinks"])
    return w


def _big_grads(i, g):
    big = {}
    for f in ("ffn1", "ffn2"):
        if f"{f}_in" in g:
            big[f"{f}_w_in", i] = g[f"{f}_in"]
            big[f"{f}_w_out", i] = g[f"{f}_out"].reshape(N_DEV, D_FF // N_DEV, D)
    if "ple_gate" in g:
        big["ple_gate_w", i] = g["ple_gate"].reshape(N_DEV, D // N_DEV, D)
        big["ple_proj_w", i] = _uncols(g["ple_proj"])
    if "hyb_in" in g:
        big["hyb_w_in", 0] = _uncols(g["hyb_in"][:, :HYB_IN])
        big["hyb_w_out", 0] = g["hyb_out"].reshape(N_DEV, 2 * D // N_DEV, D)
    if "qkv" in g:
        big["att_w_qkv", 0] = _uncols(g["qkv"])
        big["att_w_o", 0] = g["w_o"].reshape(N_DEV, D // N_DEV, D)
    return big


def _collect_grads(grads, d_final):
    g0, g1 = grads
    big, small = {**_big_grads(0, g0), **_big_grads(1, g1)}, {}
    for f in ("ffn1", "ffn2"):
        small[f"norm_{f}"] = jnp.concatenate([g[f"norm_{f}"] for g in grads], axis=0)
    small["norm_mix"] = jnp.concatenate([g["norm_mix"] for g in grads], axis=0)
    small["ple_norm"] = jnp.concatenate([g["ple_norm"] for g in grads], axis=0)
    small["conv_dw_w"] = g0["conv_w"][None]
    small["conv_dw_b"], small["conv_ln_g"], small["conv_ln_b"] = g0["conv_b"], g0["ln_g"], g0["ln_b"]
    small["ssm_conv_w"] = g0["sconv_w"][None]
    small["ssm_conv_b"], small["ssm_norm"] = g0["sconv_b"], g0["ssm_norm"]
    small["ssm_dt_bias"], small["ssm_a_log"], small["ssm_d"] = (g0[k][:, :SSM_HEADS] for k in ("dt_bias", "a_log", "d_skip"))
    small["att_b_qkv"], small["att_b_o"] = g1["b_qkv"], g1["b_o"]
    small["att_sinks"] = g1["sinks"][:, :SSM_HEADS]
    small["final_norm"] = d_final[0]
    return big, small


MESH = pl.DeviceIdType.MESH


def _place():
    return lax.axis_index("x"), lax.axis_index("y"), lax.axis_index("c")


def _all_gather(blocks, space, name):
    nb = len(blocks)

    def body(*refs):
        x_refs, out_refs, (send_sems, recv_sems, local_sem) = refs[:nb], refs[nb:2 * nb], refs[2 * nb:]
        x, y, c = _place()
        me, sibling = (x, y, c), (x, y, 1 - c)
        chips = [(1 - x, y), (x, 1 - y), (1 - x, 1 - y)]

        def copies(k, blk, to, own=False):
            idx = 4 * blk[0] + 2 * blk[1] + blk[2]
            return [pltpu.make_async_remote_copy(src_ref=x_ref if own else out_ref.at[idx], dst_ref=out_ref.at[idx],
                                                 send_sem=send_sems.at[k, b], recv_sem=recv_sems.at[k, b], device_id=to,
                                                 device_id_type=MESH) for b, (x_ref, out_ref) in enumerate(zip(x_refs, out_refs))]

        mine = [pltpu.make_async_copy(x_ref, out_ref.at[4 * x + 2 * y + c], local_sem.at[b])
                for b, (x_ref, out_ref) in enumerate(zip(x_refs, out_refs))]
        first = copies(0, me, sibling, own=True)
        for j, chip in enumerate(chips):
            first += copies(1 + j, me, (*chip, c), own=True)
        for cp in mine + first:
            cp.start()
        passed = []
        for j, chip in enumerate(chips):
            for cp in copies(1 + j, (*chip, c), me):
                cp.wait_recv()
            onward = copies(4 + j, (*chip, c), sibling)
            for cp in onward:
                cp.start()
            passed += onward
        for cp in copies(0, sibling, me):
            cp.wait_recv()
        for j, chip in enumerate(chips):
            for cp in copies(4 + j, (*chip, 1 - c), me):
                cp.wait_recv()
        for cp in first + passed:
            cp.wait_send()
        for cp in mine:
            cp.wait()

    spec = pl.BlockSpec(memory_space=space)
    return _pallas(
        body, name=name, out_shape=[SDS((N_DEV,) + b.shape, b.dtype) for b in blocks],
        in_specs=[spec] * nb, out_specs=[spec] * nb,
        scratch_shapes=[pltpu.SemaphoreType.DMA((7, nb)), pltpu.SemaphoreType.DMA((7, nb)), pltpu.SemaphoreType.DMA((nb,))],
    )(*blocks)


def _pair_exchange(parts, name):
    nb = len(parts)

    def body(*refs):
        p_refs, got_refs, (send_sems, recv_sems) = refs[:nb], refs[nb:2 * nb], refs[2 * nb:]
        x, y, c = _place()
        copies = [pltpu.make_async_remote_copy(src_ref=p_ref.at[2 * q + (1 - c)], dst_ref=got_ref.at[q],
                                               send_sem=send_sems.at[q, b], recv_sem=recv_sems.at[q, b], device_id=(x, y, 1 - c),
                                               device_id_type=MESH)
                  for q in range(4) for b, (p_ref, got_ref) in enumerate(zip(p_refs, got_refs))]
        for cp in copies:
            cp.start()
        for cp in copies:
            cp.wait_recv()
        for cp in copies:
            cp.wait_send()

    hbm = pl.BlockSpec(memory_space=pltpu.HBM)
    return _pallas(
        body, name=name, out_shape=[SDS((4,) + p.shape[1:], p.dtype) for p in parts], in_specs=[hbm] * nb, out_specs=[hbm] * nb,
        scratch_shapes=[pltpu.SemaphoreType.DMA((4, nb)), pltpu.SemaphoreType.DMA((4, nb))],
    )(*parts)


HBM_SPEC = pl.BlockSpec(memory_space=pltpu.HBM)
SEM_SPEC = pl.BlockSpec(memory_space=pltpu.SEMAPHORE)
EFFECT = pltpu.SideEffectType.DATAFLOW_SIDE_EFFECTING


def _plan_descriptors(plan, srcs, lands, send_sems, recv_sems, local_sems, arriving):
    remote, local = plan(*_place())
    pick = lambda ref, slot: ref if slot is None else ref.at[slot]
    rem = [pltpu.make_async_remote_copy(src_ref=pick(srcs[si], ss), dst_ref=lands[li].at[rs if arriving else ds],
                                        send_sem=send_sems.at[k], recv_sem=recv_sems.at[k], device_id=dev, device_id_type=MESH)
           for k, (si, ss, li, ds, dev, rs) in enumerate(remote)]
    loc = [pltpu.make_async_copy(pick(srcs[si], ss), lands[li].at[ds], local_sems.at[k])
           for k, (si, ss, li, ds) in enumerate(local)]
    return rem, loc


def _plan_counts(plan):
    remote, local = plan(0, 0, 0)
    return len(remote), max(len(local), 1)


def _exchange_start(srcs, land_shapes, plan, name):
    ns, nl = len(srcs), len(land_shapes)
    n_remote, n_local = _plan_counts(plan)
    lands = [pltpu.with_memory_space_constraint(lax.empty(s.shape, s.dtype), pltpu.HBM) for s in land_shapes]
    srcs = [pltpu.with_memory_space_constraint(s, pltpu.HBM) for s in srcs]

    def body(*refs):
        src_refs, land_refs = refs[:ns], refs[ns:ns + nl]
        send_sems, recv_sems, local_sems = refs[ns + nl:ns + nl + 3]
        token = refs[-1]
        rem, loc = _plan_descriptors(plan, src_refs, land_refs, send_sems, recv_sems, local_sems, arriving=False)
        for cp in loc + rem:
            cp.start()
        token[...] = jnp.zeros_like(token)

    outs = _pallas(
        body, name=name,
        out_shape=[pltpu.SemaphoreType.DMA((n_remote,)), pltpu.SemaphoreType.DMA((n_remote,)), pltpu.SemaphoreType.DMA((n_local,))]
        + [pltpu.HBM(a.shape, a.dtype) for a in srcs + lands] + [SDS((8, LANE), F32)],
        in_specs=[HBM_SPEC] * (ns + nl),
        out_specs=[SEM_SPEC] * 3 + [HBM_SPEC] * (ns + nl) + [pl.BlockSpec(memory_space=pltpu.VMEM)],
        input_output_aliases={i: 3 + i for i in range(ns + nl)},
        compiler_params=pltpu.CompilerParams(has_side_effects=EFFECT),
    )(*srcs, *lands)
    return (outs[:3], outs[3:3 + ns], outs[3 + ns:3 + ns + nl]), outs[-1]


def _exchange_wait(state, after, plan, name):
    sems, srcs, lands = state
    ns, nl = len(srcs), len(lands)

    def body(*refs):
        src_refs, land_refs = refs[:ns], refs[ns:ns + nl]
        send_sems, recv_sems, local_sems = refs[ns + nl:ns + nl + 3]
        rem, loc = _plan_descriptors(plan, src_refs, land_refs, send_sems, recv_sems, local_sems, arriving=True)
        for cp in rem:
            cp.wait_send()
            cp.wait_recv()
        for cp in loc:
            cp.wait()

    outs = _pallas(
        body, name=name, out_shape=[pltpu.HBM(a.shape, a.dtype) for a in list(srcs) + list(lands)],
        in_specs=[HBM_SPEC] * (ns + nl) + [SEM_SPEC] * 3 + [ANY_SPEC], out_specs=[HBM_SPEC] * (ns + nl),
        input_output_aliases={i: i for i in range(ns + nl)},
        compiler_params=pltpu.CompilerParams(has_side_effects=EFFECT),
    )(*srcs, *lands, *sems, after)
    return outs[:ns], outs[ns:]


def _gather_plan(nb):
    def plan(x, y, c):
        me = 4 * x + 2 * y + c
        remote = []
        for b in range(nb):
            for r in range(1, N_DEV):
                tx, ty, tc = (1 - x if r & 4 else x), (1 - y if r & 2 else y), (1 - c if r & 1 else c)
                remote.append((b, None, b, me, (tx, ty, tc), 4 * tx + 2 * ty + tc))
        return remote, [(b, None, b, me) for b in range(nb)]
    return plan


def _pair_plan(nb):
    def plan(x, y, c):
        return [(b, 2 * q + (1 - c), b, q, (x, y, 1 - c), q) for b in range(nb) for q in range(4)], []
    return plan


def _chip_plan(nb):
    def plan(x, y, c):
        own = 2 * x + y
        chips = [(1 - x, y), (x, 1 - y), (1 - x, 1 - y)]
        remote = [(b, 2 * cx + cy, b, own, (cx, cy, c), 2 * cx + cy) for b in range(nb) for cx, cy in chips]
        return remote, [(b, own, b, own) for b in range(nb)]
    return plan


def _row_tile(r, cap=4608):
    return max(d for d in range(16, min(r, cap) + 1, 16) if r % d == 0)


def _pair_add(parts, got, core, name):
    _, r, cdim = parts.shape
    tr = _row_tile(r)

    def body(core_ref, p_ref, g_ref, o_ref):
        o_ref[...] = (p_ref[...].astype(F32) + g_ref[...].astype(F32)).astype(o_ref.dtype)

    return pl.pallas_call(
        body, name=name, out_shape=SDS((4, r, cdim), BF16),
        grid_spec=pltpu.PrefetchScalarGridSpec(
            num_scalar_prefetch=1, grid=(4, r // tr),
            in_specs=[pl.BlockSpec((None, tr, cdim), lambda q, i, core_ref: (2 * q + core_ref[0], i, 0)),
                      pl.BlockSpec((None, tr, cdim), lambda q, i, core_ref: (q, i, 0))],
            out_specs=pl.BlockSpec((None, tr, cdim), lambda q, i, core_ref: (q, i, 0))),
        compiler_params=_params(("parallel", "parallel")),
    )(core, parts, got)


def _chip_exchange(sums, name):
    nb = len(sums)

    def body(*refs):
        b_refs, out_refs, (send_sems, recv_sems, local_sem) = refs[:nb], refs[nb:2 * nb], refs[2 * nb:]
        x, y, c = _place()
        own = 2 * x + y
        chips = [(1 - x, y), (x, 1 - y), (1 - x, 1 - y)]

        def copies(k, chip, src_slot, dst_slot):
            return [pltpu.make_async_remote_copy(src_ref=b_ref.at[src_slot], dst_ref=out_ref.at[dst_slot],
                                                 send_sem=send_sems.at[k, b], recv_sem=recv_sems.at[k, b], device_id=(*chip, c),
                                                 device_id_type=MESH) for b, (b_ref, out_ref) in enumerate(zip(b_refs, out_refs))]

        mine = [pltpu.make_async_copy(b_ref.at[own], out_ref.at[own], local_sem.at[b])
                for b, (b_ref, out_ref) in enumerate(zip(b_refs, out_refs))]
        sends = []
        for k, chip in enumerate(chips):
            sends += copies(k, chip, 2 * chip[0] + chip[1], own)
        for cp in mine + sends:
            cp.start()
        for k, chip in enumerate(chips):
            for cp in copies(k, chip, own, 2 * chip[0] + chip[1]):
                cp.wait_recv()
        for cp in sends:
            cp.wait_send()
        for cp in mine:
            cp.wait()

    hbm = pl.BlockSpec(memory_space=pltpu.HBM)
    return _pallas(
        body, name=name, out_shape=[SDS(s.shape, s.dtype) for s in sums], in_specs=[hbm] * nb, out_specs=[hbm] * nb,
        scratch_shapes=[pltpu.SemaphoreType.DMA((3, nb)), pltpu.SemaphoreType.DMA((3, nb)), pltpu.SemaphoreType.DMA((nb,))],
    )(*sums)


def _sum_slots(parts, name):
    nj, r, cdim = parts.shape
    tr = _row_tile(r)

    def body(p_ref, o_ref):
        acc = p_ref[0].astype(F32)
        for j in range(1, nj):
            acc = acc + p_ref[j].astype(F32)
        o_ref[...] = acc

    return _pallas(
        body, name=name, out_shape=SDS((r, cdim), F32), grid=(r // tr,),
        in_specs=[pl.BlockSpec((nj, tr, cdim), lambda i: (0, i, 0))], out_specs=pl.BlockSpec((tr, cdim), lambda i: (i, 0)),
        compiler_params=_params(("parallel",)),
    )(parts)


def _adamw_update(wv, gv, mv, vv):
    nm = ADAM_B1 * mv + (1.0 - ADAM_B1) * gv
    nv = ADAM_B2 * vv + (1.0 - ADAM_B2) * (gv * gv)
    m_hat = nm / (1.0 - ADAM_B1 ** ADAM_STEP)
    v_hat = nv / (1.0 - ADAM_B2 ** ADAM_STEP)
    return -ADAM_LR * (m_hat / (jnp.sqrt(v_hat) + ADAM_EPS) + ADAM_WD * wv), nm, nv


def _adamw_summed(w, m, v, by_chip, name):
    nl, r, cdim = w.shape
    tr = _row_tile(r, 512)
    nblk = r // tr

    def body(*refs):
        chip_refs, (w_ref, m_ref, v_ref, g_ref, d_ref, nm_ref, nv_ref) = refs[:nl], refs[nl:]
        layer = pl.program_id(0)
        gv = None
        for ll, c_ref in enumerate(chip_refs):
            s = c_ref[0].astype(F32)
            for q in range(1, 4):
                s = s + c_ref[q].astype(F32)
            gv = s if gv is None else jnp.where(layer == ll, s, gv)
        g_ref[...] = gv
        d_ref[...], nm_ref[...], nv_ref[...] = _adamw_update(w_ref[...], gv, m_ref[...], v_ref[...])

    def chip_map(ll):
        return lambda l, i: (0, jnp.where(l == ll, i, jnp.where(l > ll, nblk - 1, 0)), 0)

    spec = pl.BlockSpec((None, tr, cdim), lambda l, i: (l, i, 0))
    return _pallas(
        body, name=name, grid=(nl, nblk),
        in_specs=[pl.BlockSpec((4, tr, cdim), chip_map(ll)) for ll in range(nl)] + [spec] * 3,
        out_specs=[spec] * 4, out_shape=[SDS((nl, r, cdim), F32)] * 4,
        compiler_params=_params(("arbitrary", "arbitrary")),
    )(*by_chip, w, m, v)


def _adamw(w, g, m, v, name):
    shape = w.shape
    cdim = shape[-1]
    w2, g2, m2, v2 = (a.reshape(-1, cdim) for a in (w, g, m, v))
    r = w2.shape[0]
    tr = next(d for d in (512, 352, 256, 128, 64, 32, 16, 8, r) if r % d == 0)

    def body(w_ref, g_ref, m_ref, v_ref, d_ref, nm_ref, nv_ref):
        d_ref[...], nm_ref[...], nv_ref[...] = _adamw_update(w_ref[...], g_ref[...], m_ref[...], v_ref[...])

    spec = pl.BlockSpec((tr, cdim), lambda i: (i, 0))
    outs = _pallas(
        body, name=name, grid=(r // tr,), in_specs=[spec] * 4, out_specs=[spec] * 3, out_shape=[SDS((r, cdim), F32)] * 3,
        compiler_params=_params(("parallel",)),
    )(w2, g2, m2, v2)
    return tuple(o.reshape(shape) for o in outs)


WEIGHTS = ("norm_ffn1", "ffn1_w_in", "ffn1_w_out", "norm_mix", "norm_ffn2", "ffn2_w_in", "ffn2_w_out", "ple_norm", "ple_gate_w",
           "ple_proj_w", "hyb_w_in", "conv_dw_w", "conv_dw_b", "conv_ln_g", "conv_ln_b", "ssm_conv_w", "ssm_conv_b", "ssm_dt_bias",
           "ssm_a_log", "ssm_d", "ssm_norm", "hyb_w_out", "att_w_qkv", "att_b_qkv", "att_sinks", "att_w_o", "att_b_o", "final_norm")
BIG = ("ffn1_w_in", "ffn1_w_out", "ffn2_w_in", "ffn2_w_out", "ple_gate_w", "ple_proj_w", "hyb_w_in", "hyb_w_out", "att_w_qkv",
       "att_w_o")
SMALL_SHARDED = {"conv_dw_w": 2, "ssm_conv_w": 2, "att_b_qkv": 1, "att_b_o": 1}
SMALL = tuple(n for n in WEIGHTS if n not in BIG)
PACK_ROWS = 16


def _pack(arrays, lead=0):
    pieces = []
    for a in arrays:
        flat = a.reshape(*a.shape[:lead], -1)
        size = flat.shape[-1]
        padded = -(-size // (PACK_ROWS * LANE)) * PACK_ROWS * LANE
        flat = jnp.pad(flat, [(0, 0)] * lead + [(0, padded - size)])
        pieces.append(flat.reshape(*a.shape[:lead], padded // LANE, LANE))
    return jnp.concatenate(pieces, axis=lead)


def _unpack(buf, shapes, lead=0):
    out, row = [], 0
    for shape in shapes:
        size = math.prod(shape)
        rows = -(-size // (PACK_ROWS * LANE)) * PACK_ROWS
        piece = lax.slice_in_dim(buf, row, row + rows, axis=lead)
        piece = piece.reshape(*buf.shape[:lead], rows * LANE)
        out.append(lax.slice_in_dim(piece, 0, size, axis=lead).reshape(*buf.shape[:lead], *shape))
        row += rows
    return out


def kernel(x, p, norm_ffn1, ffn1_w_in, ffn1_w_out, norm_mix, norm_ffn2, ffn2_w_in, ffn2_w_out, ple_norm, ple_gate_w, ple_proj_w, hyb_w_in, conv_dw_w, conv_dw_b, conv_ln_g, conv_ln_b, ssm_conv_w, ssm_conv_b, ssm_dt_bias, ssm_a_log, ssm_d, ssm_norm, hyb_w_out, att_w_qkv, att_b_qkv, att_sinks, att_w_o, att_b_o, final_norm, loss_target, m_norm_ffn1, m_ffn1_w_in, m_ffn1_w_out, m_norm_mix, m_norm_ffn2, m_ffn2_w_in, m_ffn2_w_out, m_ple_norm, m_ple_gate_w, m_ple_proj_w, m_hyb_w_in, m_conv_dw_w, m_conv_dw_b, m_conv_ln_g, m_conv_ln_b, m_ssm_conv_w, m_ssm_conv_b, m_ssm_dt_bias, m_ssm_a_log, m_ssm_d, m_ssm_norm, m_hyb_w_out, m_att_w_qkv, m_att_b_qkv, m_att_sinks, m_att_w_o, m_att_b_o, m_final_norm, v_norm_ffn1, v_ffn1_w_in, v_ffn1_w_out, v_norm_mix, v_norm_ffn2, v_ffn2_w_in, v_ffn2_w_out, v_ple_norm, v_ple_gate_w, v_ple_proj_w, v_hyb_w_in, v_conv_dw_w, v_conv_dw_b, v_conv_ln_g, v_conv_ln_b, v_ssm_conv_w, v_ssm_conv_b, v_ssm_dt_bias, v_ssm_a_log, v_ssm_d, v_ssm_norm, v_hyb_w_out, v_att_w_qkv, v_att_b_qkv, v_att_sinks, v_att_w_o, v_att_b_o, v_final_norm):
    args = (norm_ffn1, ffn1_w_in, ffn1_w_out, norm_mix, norm_ffn2, ffn2_w_in, ffn2_w_out, ple_norm, ple_gate_w, ple_proj_w, hyb_w_in, conv_dw_w, conv_dw_b, conv_ln_g, conv_ln_b, ssm_conv_w, ssm_conv_b, ssm_dt_bias, ssm_a_log, ssm_d, ssm_norm, hyb_w_out, att_w_qkv, att_b_qkv, att_sinks, att_w_o, att_b_o, final_norm)
    moments_m = (m_norm_ffn1, m_ffn1_w_in, m_ffn1_w_out, m_norm_mix, m_norm_ffn2, m_ffn2_w_in, m_ffn2_w_out, m_ple_norm, m_ple_gate_w, m_ple_proj_w, m_hyb_w_in, m_conv_dw_w, m_conv_dw_b, m_conv_ln_g, m_conv_ln_b, m_ssm_conv_w, m_ssm_conv_b, m_ssm_dt_bias, m_ssm_a_log, m_ssm_d, m_ssm_norm, m_hyb_w_out, m_att_w_qkv, m_att_b_qkv, m_att_sinks, m_att_w_o, m_att_b_o, m_final_norm)
    moments_v = (v_norm_ffn1, v_ffn1_w_in, v_ffn1_w_out, v_norm_mix, v_norm_ffn2, v_ffn2_w_in, v_ffn2_w_out, v_ple_norm, v_ple_gate_w, v_ple_proj_w, v_hyb_w_in, v_conv_dw_w, v_conv_dw_b, v_conv_ln_g, v_conv_ln_b, v_ssm_conv_w, v_ssm_conv_b, v_ssm_dt_bias, v_ssm_a_log, v_ssm_d, v_ssm_norm, v_hyb_w_out, v_att_w_qkv, v_att_b_qkv, v_att_sinks, v_att_w_o, v_att_b_o, v_final_norm)
    w = dict(zip(WEIGHTS, args))
    m = dict(zip(WEIGHTS, moments_m))
    v = dict(zip(WEIGHTS, moments_v))
    cx, cy, cc = _place()
    me = 4 * cx + 2 * cy + cc

    core = jnp.reshape(cc, (1,)).astype(jnp.int32)
    layer_of = lambda n, i: 1 if n.startswith("att_") else i
    keys = [[(n, i) for n in BIG for i in range(w[n].shape[0]) if layer_of(n, i) == layer] for layer in range(2)]

    first = [key for key in keys[0] if key[0].startswith("ffn1")]
    mixer = [key for key in keys[0] if key[0].startswith("hyb")]
    rest0 = [key for key in keys[0] if key not in first + mixer]
    gw, by_chip = {}, {}

    def gather_later(group, name):
        blocks = [w[n][i].astype(BF16) for n, i in group]
        plan = _gather_plan(len(blocks))
        state, token = _exchange_start(blocks, [SDS((N_DEV,) + b.shape, BF16) for b in blocks], plan, f"{name}_start")
        return token, lambda after: gw.update(zip(group, _exchange_wait(state, after, plan, f"{name}_wait")[1]))

    def reduce_later(group, big, name):
        pair_plan, chip_plan = _pair_plan(len(group)), _chip_plan(len(group))
        parts = [big[key] for key in group]
        pair, token = _exchange_start(parts, [SDS((4,) + pt.shape[1:], BF16) for pt in parts], pair_plan, f"{name}_pair_start")
        stage = {}

        def middle(after):
            thru, got = _exchange_wait(pair, after, pair_plan, f"{name}_pair_wait")
            sums = [_pair_add(pt, gt, core, f"grads_pair_add_{n}_{i}") for pt, gt, (n, i) in zip(thru, got, group)]
            stage["chip"], chip_token = _exchange_start(sums, [SDS(s.shape, BF16) for s in sums], chip_plan, f"{name}_chip_start")
            return chip_token

        def finish(after):
            by_chip.update(zip(group, _exchange_wait(stage["chip"], after, chip_plan, f"{name}_chip_wait")[1]))

        return token, middle, finish

    _restart_chain()
    *gathered_first, gathered_small = _all_gather([w[n][i].astype(BF16) for n, i in first] + [_pack([w[n] for n in SMALL_SHARDED])],
                                                  pltpu.HBM, "gather_weights_first")
    gw.update(zip(first, gathered_first))
    mixer_token, mixer_arrived = gather_later(mixer, "gather_weights_mixer")
    rest0_token, rest0_arrived = gather_later(rest0, "gather_weights_rest")
    layer1_token, layer1_arrived = gather_later(keys[1], "gather_weights_l1")
    gs = dict(zip(SMALL_SHARDED, _unpack(gathered_small, [w[n].shape for n in SMALL_SHARDED], lead=1)))
    rep = {n: w[n] for n in SMALL if n not in SMALL_SHARDED}

    tables = _rope_tables(x.shape[1])
    pb = p[:, 0].astype(BF16)
    w0, s0 = _build_layer(0, gw, gs, rep, parts=("ffn1",)), {}
    h, s0["ffn1"] = _ffn_fwd(x[0], w0["norm_ffn1"], w0["ffn1_in"], w0["ffn1_out"], "l0_ffn1",
                             deps=(mixer_token, rest0_token, layer1_token))
    mixer_arrived(h)
    w0.update(_build_layer(0, gw, gs, rep, parts=("mix",)))
    h, s0["mix"] = _hyb_fwd(h, w0, "l0_hyb")
    rest0_arrived(h)
    w0.update(_build_layer(0, gw, gs, rep, parts=("ffn2", "ple")))
    h, s0["ffn2"] = _ffn_fwd(h, w0["norm_ffn2"], w0["ffn2_in"], w0["ffn2_out"], "l0_ffn2")
    h, s0["ple"] = _ple_fwd(h, w0["ple_norm"], pb[0], w0["ple_gate"], w0["ple_proj"], "l0_ple")
    layer1_arrived(h)
    w1 = _build_layer(1, gw, gs, rep)
    h, s1 = _layer_fwd(1, h, w1, pb[1], tables)
    loss, dh, dhb, d_final = _loss_head(h, loss_target[0], final_norm[None], "loss_head")
    loss = lax.psum(loss[0, 0], ("x", "y", "c"))

    dh, dhb, head1 = _layer_bwd_head(1, dh, dhb, s1, w1, pb[1])
    dh, dhb, tail1 = _layer_bwd_tail(1, dh, dhb, s1, w1, tables)
    grads1 = {**head1, **tail1}
    l1_token, l1_middle, l1_finish = reduce_later(keys[1], _big_grads(1, grads1), "grads_l1")
    dh, dhb, grads0 = _layer_bwd_head(0, dh, dhb, s0, w0, pb[0], deps=(l1_token,))
    dh, dhb, grads0["norm_ffn2"], grads0["ffn2_in"], grads0["ffn2_out"] = _ffn_bwd(
        dh, dhb, s0["ffn2"], w0["norm_ffn2"], w0["ffn2_in"], w0["ffn2_out"], "l0_ffn2", deps=(l1_middle(dh),))
    dh, dhb, mixer_grads = _hyb_bwd(dh, dhb, s0["mix"], w0, "l0_hyb")
    grads0.update(mixer_grads)
    l0_token, l0_middle, l0_finish = reduce_later(mixer + rest0, _big_grads(0, grads0), "grads_l0")
    dx, dhb, grads0["norm_ffn1"], grads0["ffn1_in"], grads0["ffn1_out"] = _ffn_bwd(
        dh, dhb, s0["ffn1"], w0["norm_ffn1"], w0["ffn1_in"], w0["ffn1_out"], "l0_ffn1", deps=(l0_token,),
        hook=lambda dpre: (l0_middle(dpre),))
    l1_finish(dx)
    l0_finish(dx)
    big0 = _big_grads(0, grads0)
    parts0 = [big0[key] for key in first]
    got0 = _pair_exchange(parts0, "grads_first_pair_exchange")
    sums0 = [_pair_add(pt, gt, core, f"grads_pair_add_{n}_{i}") for pt, gt, (n, i) in zip(parts0, got0, first)]
    by_chip.update(zip(first, _chip_exchange(sums0, "grads_first_chip_exchange")))
    _, small = _collect_grads([grads0, grads1], d_final)
    small_shapes = [small[n].shape for n in SMALL]
    all_small = _all_gather([_pack([small[n] for n in SMALL])], pltpu.VMEM, "gather_small_grads")[0]
    g = dict(zip(SMALL, _unpack(_sum_slots(all_small, "small_grads_sum"), small_shapes)))
    for n, axis in SMALL_SHARDED.items():
        g[n] = lax.dynamic_slice_in_dim(g[n], me * w[n].shape[axis], w[n].shape[axis], axis=axis)

    delta, new_m, new_v = {}, {}, {}
    for n in BIG:
        g[n], delta[n], new_m[n], new_v[n] = _adamw_summed(w[n], m[n], v[n], [by_chip[n, i] for i in range(w[n].shape[0])],
                                                           f"adamw_{n}")
    packed = [_pack([d[n] for n in SMALL]) for d in (w, g, m, v)]
    shapes = [w[n].shape for n in SMALL]
    for d, buf in zip((delta, new_m, new_v), _adamw(*packed, "adamw_small")):
        d.update(zip(SMALL, _unpack(buf, shapes)))
    return (loss, dx[None], *[g[n] for n in WEIGHTS], *[delta[n] for n in WEIGHTS], *[new_m[n] for n in WEIGHTS],
            *[new_v[n] for n in WEIGHTS])
```

```python
import functools
import math

import numpy as np
import jax
import jax.numpy as jnp
from jax import lax
from jax.experimental import pallas as pl
from jax.experimental.pallas import tpu as pltpu

F32, BF16 = jnp.float32, jnp.bfloat16
HI = lax.Precision.HIGHEST
SDS = jax.ShapeDtypeStruct

N_DEV = 8
D = 1024
D_FF = 2816
FF_SHARD = 2 * D_FF // N_DEV
PLE_DIM = 256
EPS = 1e-6
CONV_W = 31
SSM_CONV = 4
SSM_HEADS = 16
SSM_XBC = 1536
CHUNK = 128
HYB_IN = 4624
HYB_PAD = 5120
DT_COL = 4608
N_PAIR = 8
ROPE_THETA = 10000.0
LANE = 128
VMEM_LIMIT = 56 * 1024 * 1024

ADAM_LR, ADAM_B1, ADAM_B2, ADAM_EPS, ADAM_WD, ADAM_STEP = 0.001, 0.9, 0.999, 1e-08, 0.01, 10


def _params(sem):
    return pltpu.CompilerParams(dimension_semantics=sem, vmem_limit_bytes=VMEM_LIMIT)


_CHAIN = []


def _restart_chain():
    _CHAIN.clear()


def _pallas(body, *, in_specs, **kw):
    def run(*args):
        n, dep = len(args), list(_CHAIN)

        def chained(*refs):
            return body(*refs[:n], *refs[n + len(dep):])

        outs = pl.pallas_call(chained, in_specs=list(in_specs) + [pl.BlockSpec(memory_space=pl.ANY)] * len(dep), **kw)(*args, *dep)
        _CHAIN[:] = [outs[-1] if isinstance(outs, (list, tuple)) else outs]
        return outs

    return run


def _mm(a, b, *, ta=False, tb=False, reduce_j=False, out_dtypes=(F32,), tm=1024, tn=1024, tk=1024,
        epi=None, extras=(), rows=(), deps=(), sums=0, name):
    ja, jb = a.shape[0], b.shape[0]
    nj = max(ja, jb)
    jo = 1 if reduce_j else nj
    m, k = (a.shape[2], a.shape[1]) if ta else (a.shape[1], a.shape[2])
    n = b.shape[1] if tb else b.shape[2]
    assert (b.shape[2] if tb else b.shape[1]) == k and ja in (1, nj) and jb in (1, nj)
    tm, tn, tk = min(tm, m), min(tn, n), min(tk, k)
    assert m % tm == 0 and n % tn == 0 and k % tk == 0, (name, m, n, k, tm, tn, tk)
    assert not sums or (tn == n and (reduce_j or nj == 1))
    nk = k // tk
    steps = nk * (nj if reduce_j else 1)
    ne, nr, no = len(extras), len(rows), len(out_dtypes)

    def a_map(i, c, j, kk):
        return (j if ja > 1 else 0, kk, i) if ta else (j if ja > 1 else 0, i, kk)

    def b_map(i, c, j, kk):
        return (j if jb > 1 else 0, c, kk) if tb else (j if jb > 1 else 0, kk, c)

    def o_map(i, c, j, kk):
        return (0 if reduce_j else j, i, c)

    dims = (((0 if ta else 1,), (1 if tb else 0,)), ((), ()))

    def body(a_ref, b_ref, *rest):
        ex, rw = rest[:ne], rest[ne:ne + nr]
        outs = rest[ne + nr + len(deps):ne + nr + len(deps) + no]
        sum_refs = rest[ne + nr + len(deps) + no:ne + nr + len(deps) + no + sums]
        part = lax.dot_general(a_ref[...], b_ref[...], dims, preferred_element_type=F32)
        first_tile = pl.program_id(0) == 0

        def finish(acc):
            res = epi(acc, *[e[...] for e in ex], *[r[...] for r in rw]) if epi else (acc,)
            for o, r in zip(outs, res):
                o[...] = r.astype(o.dtype)
            for s_ref, r in zip(sum_refs, res[no:]):
                @pl.when(first_tile)
                def _(s_ref=s_ref, r=r):
                    s_ref[...] = r

                @pl.when(jnp.logical_not(first_tile))
                def _(s_ref=s_ref, r=r):
                    s_ref[...] += r

        if steps == 1:
            finish(part)
            return
        acc_ref = rest[-1]
        kk = pl.program_id(3)
        step = pl.program_id(2) * nk + kk if reduce_j else kk

        @pl.when(step == 0)
        def _():
            acc_ref[...] = part

        @pl.when(step > 0)
        def _():
            acc_ref[...] += part

        @pl.when(step == steps - 1)
        def _():
            finish(acc_ref[...])

    o_spec = pl.BlockSpec((None, tm, tn), o_map)
    row_spec = pl.BlockSpec((1, tn), lambda i, c, j, kk: (0, c))
    return _pallas(
        body, name=name, grid=(m // tm, n // tn, nj, nk),
        in_specs=[pl.BlockSpec((None, tk, tm) if ta else (None, tm, tk), a_map),
                  pl.BlockSpec((None, tn, tk) if tb else (None, tk, tn), b_map)]
        + [o_spec] * ne + [row_spec] * nr + [ANY_SPEC] * len(deps),
        out_specs=[o_spec] * no + [row_spec] * sums,
        out_shape=[SDS((jo, m, n), dt) for dt in out_dtypes] + [SDS((1, n), F32)] * sums,
        scratch_shapes=[pltpu.VMEM((tm, tn), F32)] if steps > 1 else [],
        compiler_params=_params(("arbitrary" if sums else "parallel", "parallel", "arbitrary", "arbitrary")),
    )(a, b, *extras, *rows, *deps)


def _whole(p):
    return pl.BlockSpec(p.shape, lambda *_: (0,) * p.ndim)


ANY_SPEC = pl.BlockSpec(memory_space=pl.ANY)


def _rowop(fn, tiles, params, outs, *, grid, name, deps=()):
    nin = len(tiles) + len(params)

    def body(*refs):
        res = fn(*[r[...].astype(F32) for r in refs[:nin]])
        for r, o in zip(refs[nin + len(deps):], res):
            r[...] = o.astype(r.dtype)

    return _pallas(
        body, name=name, grid=grid,
        in_specs=[s for _, s in tiles] + [_whole(p) for p in params] + [ANY_SPEC] * len(deps),
        out_specs=[s for _, _, s in outs], out_shape=[SDS(sh, dt) for sh, dt, _ in outs],
        compiler_params=_params(("parallel",) * len(grid)),
    )(*[t for t, _ in tiles], *params, *deps)


def _rowop_bwd(fn, tiles, params, cots, wrt, gouts, *, grid, name, adds=(), deps=()):
    nt, npar, nc, na = len(tiles), len(params), len(cots), len(adds)
    nin = nt + npar
    flat = [i for grp in wrt for i in grp]
    n_gout = sum(len(dts) for _, dts, _ in gouts)

    def body(*refs):
        vals = [r[...].astype(F32) for r in refs[:nin]]
        cvals = [r[...].astype(F32) for r in refs[nin:nin + nc]]
        avals = [r[...].astype(F32) for r in refs[nin + nc:nin + nc + na]]
        orefs = refs[nin + nc + na + len(deps):]
        diff_idx = flat + list(range(nt, nin))

        def f(*dv):
            full = list(vals)
            for i, v in zip(diff_idx, dv):
                full[i] = v
            return fn(*full)

        _, vjp = jax.vjp(f, *[vals[i] for i in diff_idx])
        grads = vjp(tuple(cvals))
        tile_g, par_g = list(grads[:len(flat)]), grads[len(flat):]
        group_g, at = [], 0
        for grp in wrt:
            members = tile_g[at:at + len(grp)]
            at += len(grp)
            group_g.append(members[0] if len(grp) == 1 else jnp.stack(members, axis=0))
        for av in avals:
            group_g[0] = group_g[0] + av
        o = 0
        for g, (_, dts, _) in zip(group_g, gouts):
            for _ in dts:
                orefs[o][...] = g.astype(orefs[o].dtype)
                o += 1
        first = functools.reduce(jnp.logical_and, [pl.program_id(ax) == 0 for ax in range(len(grid))])
        for r, g in zip(orefs[n_gout:], par_g):
            @pl.when(first)
            def _(r=r, g=g):
                r[...] = g

            @pl.when(jnp.logical_not(first))
            def _(r=r, g=g):
                r[...] += g

    out_specs, out_shape = [], []
    for sh, dts, spec in gouts:
        for dt in dts:
            out_specs.append(spec)
            out_shape.append(SDS(sh, dt))
    for p in params:
        out_specs.append(_whole(p))
        out_shape.append(SDS(p.shape, F32))
    return _pallas(
        body, name=name, grid=grid,
        in_specs=[s for _, s in tiles] + [_whole(p) for p in params] + [s for _, s in cots] + [s for _, s in adds]
        + [ANY_SPEC] * len(deps),
        out_specs=out_specs, out_shape=out_shape,
        compiler_params=_params(("arbitrary",) * len(grid)),
    )(*[t for t, _ in tiles], *params, *[c for c, _ in cots], *[a for a, _ in adds], *deps)


def _tok(c, tm, col=0):
    return pl.BlockSpec((tm, c), lambda i, col=col: (i, col))


def _rms_fn(h, g):
    return (h * lax.rsqrt(jnp.mean(h * h, axis=-1, keepdims=True) + EPS) * g,)


def _lnswish_fn(u, g, b):
    mu = jnp.mean(u, axis=-1, keepdims=True)
    xc = u - mu
    y = xc * lax.rsqrt(jnp.mean(xc * xc, axis=-1, keepdims=True) + EPS) * g + b
    return (y * jax.nn.sigmoid(y),)


def _ple_fn(z, e):
    return (jax.nn.sigmoid(z) * e,)


def _rms(h, g, name, tm=512, deps=()):
    t = h.shape[0]
    return _rowop(_rms_fn, [(h, _tok(D, tm))], [g], [((t, D), BF16, _tok(D, tm))], grid=(t // tm,), name=name, deps=deps)[0]


def _drms_epi(dn, h, dres, g):
    _, vjp = jax.vjp(_rms_fn, h, g)
    dh, dg = vjp((dn,))
    dh = dh + dres
    return dh, dh, dg


def _mm_drms(a, b, h, g, dres, name, tk, tb=True):
    dh, dhb, dg = _mm(a, b, tb=tb, reduce_j=a.shape[0] > 1, tm=512, tk=tk, epi=_drms_epi, extras=(h[None], dres[None]),
                      rows=(g,), out_dtypes=(F32, BF16), sums=1, name=name)
    return dh[0], dhb[0], dg


def _conv_geometry(width):
    pad = 32 if width > 8 else 8
    return pad, pad - (width - 1)


def _fill_shifts(xpad_ref, sh_ref, t, shifts):
    for r in shifts:
        sh_ref[r, :, :] = xpad_ref[pl.ds(r, t + 32), :]


def _dwconv(xs, w, b, *, width, glu, silu, cb, name):
    t = xs[0][0].shape[0]
    c = w.shape[1]
    pad, off = _conv_geometry(width)
    shifts = sorted({(k + off) % 8 for k in range(width)})
    ch = 32

    def body(*refs):
        x_refs, (w_ref, b_ref, o_ref, xpad_ref, sh_ref) = refs[:len(xs)], refs[len(xs):]
        u = x_refs[0][...] * jax.nn.sigmoid(x_refs[1][...]) if glu else x_refs[0][...]
        xpad_ref[pl.ds(0, pad), :] = jnp.zeros((pad, cb), F32)
        xpad_ref[pl.ds(pad, t), :] = u
        xpad_ref[pl.ds(pad + t, 40 - pad), :] = jnp.zeros((40 - pad, cb), F32)
        _fill_shifts(xpad_ref, sh_ref, t, shifts)

        def chunk(i, carry):
            t0 = pl.multiple_of(i * ch, ch)
            acc = jnp.broadcast_to(b_ref[...], (ch, cb))
            for k in range(width):
                q, r = divmod(k + off, 8)
                acc = acc + w_ref[pl.ds(k, 1), :] * sh_ref[r, pl.ds(t0 + 8 * q, ch), :]
            o_ref[pl.ds(t0, ch), :] = acc * jax.nn.sigmoid(acc) if silu else acc
            return carry

        lax.fori_loop(0, t // ch, chunk, 0)

    return _pallas(
        body, name=name, grid=(c // cb,),
        in_specs=[pl.BlockSpec((t, cb), lambda i, o=o: (0, o + i)) for _, o in xs]
        + [pl.BlockSpec((width, cb), lambda i: (0, i)), pl.BlockSpec((1, cb), lambda i: (0, i))],
        out_specs=pl.BlockSpec((t, cb), lambda i: (0, i)), out_shape=SDS((t, c), F32),
        scratch_shapes=[pltpu.VMEM((t + 40, cb), F32), pltpu.VMEM((8, t + 32, cb), F32)],
        compiler_params=_params(("parallel",)),
    )(*[x for x, _ in xs], w, b)


def _dwconv_bwd(xs, w, b, dy, *, width, glu, silu, cb, name):
    t = xs[0][0].shape[0]
    c = w.shape[1]
    pad, off = _conv_geometry(width)
    shifts = sorted({(k + off) % 8 for k in range(width)})
    shifts_t = sorted({mm % 8 for mm in range(width)})
    ch = 32
    nx = len(xs)

    def body(*refs):
        x_refs = refs[:nx]
        w_ref, b_ref, dy_ref = refs[nx:nx + 3]
        dx_refs = refs[nx + 3:nx + 3 + nx]
        dw_ref, db_ref, xpad_ref, sh_ref, dc_ref = refs[nx + 3 + nx:]
        u = x_refs[0][...] * jax.nn.sigmoid(x_refs[1][...]) if glu else x_refs[0][...]
        xpad_ref[pl.ds(0, pad), :] = jnp.zeros((pad, cb), F32)
        xpad_ref[pl.ds(pad, t), :] = u
        xpad_ref[pl.ds(pad + t, 40 - pad), :] = jnp.zeros((40 - pad, cb), F32)
        _fill_shifts(xpad_ref, sh_ref, t, shifts)

        if silu:
            def act_chunk(i, carry):
                t0 = pl.multiple_of(i * ch, ch)
                acc = jnp.broadcast_to(b_ref[...], (ch, cb))
                for k in range(width):
                    q, r = divmod(k + off, 8)
                    acc = acc + w_ref[pl.ds(k, 1), :] * sh_ref[r, pl.ds(t0 + 8 * q, ch), :]
                sg = jax.nn.sigmoid(acc)
                dc_ref[pl.ds(t0, ch), :] = dy_ref[pl.ds(t0, ch), :] * (sg * (1.0 + acc * (1.0 - sg)))
                return carry

            lax.fori_loop(0, t // ch, act_chunk, 0)
        else:
            dc_ref[...] = dy_ref[...]

        def dw_chunk(i, accs):
            t0 = pl.multiple_of(i * 8, 8)
            d = dc_ref[pl.ds(t0, 8), :]
            new = []
            for k in range(width):
                q, r = divmod(k + off, 8)
                new.append(accs[k] + d * sh_ref[r, pl.ds(t0 + 8 * q, 8), :])
            new.append(accs[width] + d)
            return tuple(new)

        accs = lax.fori_loop(0, t // 8, dw_chunk, tuple(jnp.zeros((8, cb), F32) for _ in range(width + 1)))
        for k in range(width):
            dw_ref[pl.ds(k, 1), :] = jnp.sum(accs[k], axis=0, keepdims=True)
        db_ref[...] = jnp.sum(accs[width], axis=0, keepdims=True)

        xpad_ref[pl.ds(0, t), :] = dc_ref[...]
        xpad_ref[pl.ds(t, 40), :] = jnp.zeros((40, cb), F32)
        _fill_shifts(xpad_ref, sh_ref, t, shifts_t)

        def dx_chunk(i, carry):
            t0 = pl.multiple_of(i * ch, ch)
            acc = jnp.zeros((ch, cb), F32)
            for mm in range(width):
                q, r = divmod(mm, 8)
                acc = acc + w_ref[pl.ds(width - 1 - mm, 1), :] * sh_ref[r, pl.ds(t0 + 8 * q, ch), :]
            if glu:
                val, gate = x_refs[0][pl.ds(t0, ch), :], x_refs[1][pl.ds(t0, ch), :]
                sg = jax.nn.sigmoid(gate)
                dx_refs[0][pl.ds(t0, ch), :] = (acc * sg).astype(BF16)
                dx_refs[1][pl.ds(t0, ch), :] = (acc * val * sg * (1.0 - sg)).astype(BF16)
            else:
                dx_refs[0][pl.ds(t0, ch), :] = acc.astype(BF16)
            return carry

        lax.fori_loop(0, t // ch, dx_chunk, 0)

    col = pl.BlockSpec((t, cb), lambda i: (0, i))
    return _pallas(
        body, name=name, grid=(c // cb,),
        in_specs=[pl.BlockSpec((t, cb), lambda i, o=o: (0, o + i)) for _, o in xs]
        + [pl.BlockSpec((width, cb), lambda i: (0, i)), pl.BlockSpec((1, cb), lambda i: (0, i)), col],
        out_specs=[col] * nx + [pl.BlockSpec((width, cb), lambda i: (0, i)), pl.BlockSpec((1, cb), lambda i: (0, i))],
        out_shape=[SDS((t, c), BF16)] * nx + [SDS((width, c), F32), SDS((1, c), F32)],
        scratch_shapes=[pltpu.VMEM((t + 40, cb), F32), pltpu.VMEM((8, t + 32, cb), F32), pltpu.VMEM((t, cb), F32)],
        compiler_params=_params(("parallel",)),
    )(*[x for x, _ in xs], w, b, dy)


_DIMS = {"nn": (((1,), (0,)), ((), ())), "nt": (((1,), (1,)), ((), ())), "tn": (((0,), (0,)), ((), ()))}


def _raw_dot(a, b, mode):
    return lax.dot_general(a.astype(BF16), b.astype(BF16), _DIMS[mode], preferred_element_type=F32)


@functools.partial(jax.custom_vjp, nondiff_argnums=(2,))
def _bdot(a, b, mode):
    return _raw_dot(a, b, mode)


def _bdot_fwd(a, b, mode):
    return _raw_dot(a, b, mode), (a, b)


def _bdot_bwd(mode, res, g):
    a, b = res
    if mode == "nn":
        return _raw_dot(g, b, "nt"), _raw_dot(a, g, "tn")
    if mode == "nt":
        return _raw_dot(g, b, "nn"), _raw_dot(g, a, "tn")
    return _raw_dot(b, g, "nt"), _raw_dot(a, g, "nn")


_bdot.defvjp(_bdot_fwd, _bdot_bwd)


def _iota(shape, axis):
    return lax.broadcasted_iota(jnp.int32, shape, axis)


def _half_masks():
    left = (_iota((1, LANE), 1) < 64).astype(F32)
    return left, 1.0 - left


def _ssd_chunk(state, xa, dtr, z, dtb, alog, dsk, ng):
    xs, bm, cm = xa[:, :D], xa[:, D:D + 256], xa[:, D + 256:]
    left, right = _half_masks()
    expand = (_iota((LANE, D), 1) // 64 == _iota((LANE, D), 0)).astype(F32)
    li, si = _iota((CHUNK, CHUNK), 0), _iota((CHUNK, CHUNK), 1)
    tril = li >= si
    dt16 = jax.nn.softplus(dtr + dtb)
    adt = dt16 * (-jnp.exp(alog))
    dtf = jnp.dot(dt16, expand, precision=HI)
    cs16 = jnp.dot(tril.astype(F32), adt, precision=HI)
    csf = jnp.dot(cs16, expand, precision=HI)
    totf = jnp.sum(jnp.dot(adt, expand, precision=HI), axis=0, keepdims=True)
    cst = cs16.T
    xdt = xs * dtf
    ys, new_state = [], []
    for g in range(2):
        bg, cg = bm[:, LANE * g:LANE * (g + 1)], cm[:, LANE * g:LANE * (g + 1)]
        cb = _bdot(cg, bg, "nt")
        for q in range(4):
            pr = 4 * g + q
            decay = []
            for h in (2 * pr, 2 * pr + 1):
                col = jnp.sum(jnp.where(si == h, cs16, 0.0), axis=1, keepdims=True)
                row = jnp.sum(jnp.where(li == h, cst, 0.0), axis=0, keepdims=True)
                decay.append(cb * jnp.exp(jnp.where(tril, col - row, -jnp.inf)))
            xp = xdt[:, LANE * pr:LANE * (pr + 1)]
            y_diag = _bdot(jnp.concatenate(decay, axis=1), jnp.concatenate([xp * left, xp * right], axis=0), "nn")
            csb, tot = csf[:, LANE * pr:LANE * (pr + 1)], totf[:, LANE * pr:LANE * (pr + 1)]
            ys.append(y_diag + _bdot(cg, state[pr], "nn") * jnp.exp(csb))
            new_state.append(state[pr] * jnp.exp(tot) + _bdot(bg, xp * jnp.exp(tot - csb), "tn"))
    y = jnp.concatenate(ys, axis=1)
    y = y + jnp.dot(jnp.broadcast_to(dsk, (CHUNK, LANE)), expand, precision=HI) * xs
    y = y * (z * jax.nn.sigmoid(z))
    halves = []
    for g in range(2):
        yg = y[:, 512 * g:512 * (g + 1)]
        halves.append(yg * lax.rsqrt(jnp.mean(yg * yg, axis=-1, keepdims=True) + EPS))
    return jnp.concatenate(halves, axis=1) * ng, jnp.stack(new_state, axis=0)


def _ssd_specs(t, rev):
    nc = t // CHUNK
    ix = (lambda c: nc - 1 - c) if rev else (lambda c: c)
    return nc, ix


def _ssd_fwd(xa, proj, dtb, alog, dsk, ng, name):
    t = xa.shape[0]
    nc, ix = _ssd_specs(t, False)

    def body(xa_ref, dt_ref, z_ref, dtb_ref, alog_ref, dsk_ref, ng_ref, y_ref, st_ref, carry_ref):
        @pl.when(pl.program_id(0) == 0)
        def _():
            carry_ref[...] = jnp.zeros_like(carry_ref)

        st_ref[...] = carry_ref[...]
        y, new = _ssd_chunk(carry_ref[...], xa_ref[...], dt_ref[...], z_ref[...], dtb_ref[...], alog_ref[...],
                            dsk_ref[...], ng_ref[...])
        y_ref[...] = y.astype(BF16)
        carry_ref[...] = new

    small = [dtb, alog, dsk, ng]
    return _pallas(
        body, name=name, grid=(nc,),
        in_specs=[pl.BlockSpec((CHUNK, SSM_XBC), lambda c: (c, 0)),
                  pl.BlockSpec((CHUNK, LANE), lambda c: (c, DT_COL // LANE)),
                  pl.BlockSpec((CHUNK, D), lambda c: (c, 2))] + [_whole(p) for p in small],
        out_specs=[pl.BlockSpec((CHUNK, D), lambda c: (c, 0)), pl.BlockSpec((None, N_PAIR, LANE, LANE), lambda c: (c, 0, 0, 0))],
        out_shape=[SDS((t, D), BF16), SDS((nc, N_PAIR, LANE, LANE), F32)],
        scratch_shapes=[pltpu.VMEM((N_PAIR, LANE, LANE), F32)],
        compiler_params=_params(("arbitrary",)),
    )(xa, proj, proj, *small)


def _ssd_bwd(xa, proj, states, dy, dtb, alog, dsk, ng, name):
    t = xa.shape[0]
    nc, ix = _ssd_specs(t, True)

    def body(xa_ref, dt_ref, z_ref, st_ref, dy_ref, dtb_ref, alog_ref, dsk_ref, ng_ref,
             dxa_ref, ddt_ref, dz_ref, gdtb_ref, galog_ref, gdsk_ref, gng_ref, carry_ref):
        first = pl.program_id(0) == 0

        @pl.when(first)
        def _():
            carry_ref[...] = jnp.zeros_like(carry_ref)

        args = (st_ref[...], xa_ref[...], dt_ref[...], z_ref[...], dtb_ref[...], alog_ref[...], dsk_ref[...], ng_ref[...])
        _, vjp = jax.vjp(_ssd_chunk, *args)
        ds, dxa, ddt, dz, gdtb, galog, gdsk, gng = vjp((dy_ref[...], carry_ref[...]))
        carry_ref[...] = ds
        dxa_ref[...] = dxa
        ddt_ref[...] = ddt.astype(BF16)
        dz_ref[...] = dz.astype(BF16)
        for r, g in ((gdtb_ref, gdtb), (galog_ref, galog), (gdsk_ref, gdsk), (gng_ref, gng)):
            @pl.when(first)
            def _(r=r, g=g):
                r[...] = g

            @pl.when(jnp.logical_not(first))
            def _(r=r, g=g):
                r[...] += g

    small = [dtb, alog, dsk, ng]
    return _pallas(
        body, name=name, grid=(nc,),
        in_specs=[pl.BlockSpec((CHUNK, SSM_XBC), lambda c: (ix(c), 0)),
                  pl.BlockSpec((CHUNK, LANE), lambda c: (ix(c), DT_COL // LANE)),
                  pl.BlockSpec((CHUNK, D), lambda c: (ix(c), 2)),
                  pl.BlockSpec((None, N_PAIR, LANE, LANE), lambda c: (ix(c), 0, 0, 0)),
                  pl.BlockSpec((CHUNK, D), lambda c: (ix(c), 0))] + [_whole(p) for p in small],
        out_specs=[pl.BlockSpec((CHUNK, SSM_XBC), lambda c: (ix(c), 0)), pl.BlockSpec((CHUNK, LANE), lambda c: (ix(c), 0)),
                   pl.BlockSpec((CHUNK, D), lambda c: (ix(c), 0))] + [_whole(p) for p in small],
        out_shape=[SDS((t, SSM_XBC), F32), SDS((t, LANE), BF16), SDS((t, D), BF16)] + [SDS(p.shape, F32) for p in small],
        scratch_shapes=[pltpu.VMEM((N_PAIR, LANE, LANE), F32)],
        compiler_params=_params(("arbitrary",)),
    )(xa, proj, proj, states, dy, *small)


def _attn_block(q, kv_prev, kv_cur, cq, sq, ck, sk, sinks, rot, first_block):
    left, right = _half_masks()
    k2 = jnp.concatenate([kv_prev[:, :256], kv_cur[:, :256]], axis=0)
    v2 = jnp.concatenate([kv_prev[:, 256:], kv_cur[:, 256:]], axis=0)
    ri, ci = _iota((LANE, LANE), 0), _iota((LANE, LANE), 1)
    dup = [((ri < 64) & (ci % 64 == ri)).astype(BF16), ((ri >= 64) & (ci % 64 == ri - 64)).astype(BF16)]

    def rope(tt, c, s):
        return tt * c + jnp.dot(tt, rot, precision=HI) * s

    kd, vd = [], []
    for j in range(4):
        sl = slice(LANE * (j // 2), LANE * (j // 2 + 1))
        kd.append(_bdot(rope(k2[:, sl], ck, sk), dup[j % 2], "nn"))
        vd.append(_bdot(v2[:, sl], dup[j % 2], "nn"))
    qi, si = _iota((2 * CHUNK, 2 * CHUNK), 0) % CHUNK, _iota((2 * CHUNK, 2 * CHUNK), 1)
    valid = (si > qi) & (si <= qi + CHUNK) & jnp.logical_or(si >= CHUNK, jnp.logical_not(first_block))
    upper = _iota((2 * CHUNK, 1), 0) < CHUNK
    lanes = _iota((1, LANE), 1)
    outs = []
    for pr in range(N_PAIR):
        qr = rope(q[:, LANE * pr:LANE * (pr + 1)], cq, sq)
        lg = _bdot(jnp.concatenate([qr * left, qr * right], axis=0), kd[pr // 2], "nt") * 0.125
        lg = jnp.where(valid, lg, -jnp.inf)
        s1 = jnp.sum(jnp.where(lanes == 2 * pr, sinks, 0.0), axis=1, keepdims=True)
        s2 = jnp.sum(jnp.where(lanes == 2 * pr + 1, sinks, 0.0), axis=1, keepdims=True)
        sink = jnp.where(upper, s1, s2)
        mx = lax.stop_gradient(jnp.maximum(jnp.max(lg, axis=-1, keepdims=True), sink))
        e = jnp.exp(lg - mx)
        probs = e / (jnp.sum(e, axis=-1, keepdims=True) + jnp.exp(sink - mx))
        o2 = _bdot(probs, vd[pr // 2], "nn")
        outs.append(o2[:CHUNK] * left + o2[CHUNK:] * right)
    return jnp.concatenate(outs, axis=1)


def _attn_fwd(qkv, cos, sin, sinks, rot, name):
    t = qkv.shape[0]
    nb = t // CHUNK

    def body(q_ref, kvp_ref, kvc_ref, cq_ref, sq_ref, cp_ref, sp_ref, sinks_ref, rot_ref, o_ref):
        ck = jnp.concatenate([cp_ref[...], cq_ref[...]], axis=0)
        sk = jnp.concatenate([sp_ref[...], sq_ref[...]], axis=0)
        o_ref[...] = _attn_block(q_ref[...], kvp_ref[...], kvc_ref[...], cq_ref[...], sq_ref[...], ck, sk,
                                 sinks_ref[...], rot_ref[...], pl.program_id(0) == 0).astype(BF16)

    prev = lambda n: jnp.maximum(n - 1, 0)
    return _pallas(
        body, name=name, grid=(nb,),
        in_specs=[pl.BlockSpec((CHUNK, D), lambda n: (n, 0)),
                  pl.BlockSpec((CHUNK, 512), lambda n: (prev(n), 2)), pl.BlockSpec((CHUNK, 512), lambda n: (n, 2)),
                  pl.BlockSpec((CHUNK, LANE), lambda n: (n, 0)), pl.BlockSpec((CHUNK, LANE), lambda n: (n, 0)),
                  pl.BlockSpec((CHUNK, LANE), lambda n: (prev(n), 0)), pl.BlockSpec((CHUNK, LANE), lambda n: (prev(n), 0)),
                  _whole(sinks), _whole(rot)],
        out_specs=pl.BlockSpec((CHUNK, D), lambda n: (n, 0)), out_shape=SDS((t, D), BF16),
        compiler_params=_params(("parallel",)),
    )(qkv, qkv, qkv, cos, sin, cos, sin, sinks, rot)


def _attn_bwd(qkv, do, cos, sin, sinks, rot, name):
    t = qkv.shape[0]
    nb = t // CHUNK

    def body(q_ref, kvp_ref, kvc_ref, do_ref, cq_ref, sq_ref, cp_ref, sp_ref, sinks_ref, rot_ref,
             dq_ref, dkv_ref, dbq_ref, dbkv_ref, dsink_ref, carry_ref):
        n = pl.program_id(0)

        @pl.when(n == 0)
        def _():
            carry_ref[...] = jnp.zeros_like(carry_ref)
            dbq_ref[...] = jnp.zeros_like(dbq_ref)
            dbkv_ref[...] = jnp.zeros_like(dbkv_ref)
            dsink_ref[...] = jnp.zeros_like(dsink_ref)

        @pl.when(n < nb)
        def _():
            ck = jnp.concatenate([cp_ref[...], cq_ref[...]], axis=0)
            sk = jnp.concatenate([sp_ref[...], sq_ref[...]], axis=0)
            f = lambda q, kvp, kvc, s: _attn_block(q, kvp, kvc, cq_ref[...], sq_ref[...], ck, sk, s, rot_ref[...], n == 0)
            _, vjp = jax.vjp(f, q_ref[...], kvp_ref[...], kvc_ref[...], sinks_ref[...])
            dq, dkvp, dkvc, ds = vjp(do_ref[...].astype(F32))
            done = carry_ref[...] + dkvp
            dq_ref[...] = dq.astype(BF16)
            dkv_ref[...] = done.astype(BF16)
            dbq_ref[...] += jnp.sum(dq, axis=0, keepdims=True)
            dsink_ref[...] += ds
            carry_ref[...] = dkvc

            @pl.when(n > 0)
            def _():
                dbkv_ref[...] += jnp.sum(done, axis=0, keepdims=True)

        @pl.when(n == nb)
        def _():
            done = carry_ref[...]
            dkv_ref[...] = done.astype(BF16)
            dbkv_ref[...] += jnp.sum(done, axis=0, keepdims=True)

    cur = lambda n: jnp.minimum(n, nb - 1)
    prev = lambda n: jnp.maximum(jnp.minimum(n, nb - 1) - 1, 0)
    fin = lambda n: jnp.maximum(n - 1, 0)
    outs = _pallas(
        body, name=name, grid=(nb + 1,),
        in_specs=[pl.BlockSpec((CHUNK, D), lambda n: (cur(n), 0)),
                  pl.BlockSpec((CHUNK, 512), lambda n: (prev(n), 2)), pl.BlockSpec((CHUNK, 512), lambda n: (cur(n), 2)),
                  pl.BlockSpec((CHUNK, D), lambda n: (cur(n), 0)),
                  pl.BlockSpec((CHUNK, LANE), lambda n: (cur(n), 0)), pl.BlockSpec((CHUNK, LANE), lambda n: (cur(n), 0)),
                  pl.BlockSpec((CHUNK, LANE), lambda n: (prev(n), 0)), pl.BlockSpec((CHUNK, LANE), lambda n: (prev(n), 0)),
                  _whole(sinks), _whole(rot)],
        out_specs=[pl.BlockSpec((CHUNK, D), lambda n: (cur(n), 0)), pl.BlockSpec((CHUNK, 512), lambda n: (fin(n), 0)),
                   pl.BlockSpec((1, D), lambda n: (0, 0)), pl.BlockSpec((1, 512), lambda n: (0, 0)), _whole(sinks)],
        out_shape=[SDS((t, D), BF16), SDS((t, 512), BF16), SDS((1, D), F32), SDS((1, 512), F32), SDS(sinks.shape, F32)],
        scratch_shapes=[pltpu.VMEM((CHUNK, 512), F32)],
        compiler_params=_params(("arbitrary",)),
    )(qkv, qkv, qkv, do, cos, sin, cos, sin, sinks, rot)
    dq, dkv, dbq, dbkv, dsinks = outs
    return jnp.concatenate([dq, dkv], axis=1), jnp.concatenate([dbq, dbkv], axis=1), dsinks


def _loss_head(h, tgt, g, name, tm=512):
    t = h.shape[0]

    def body(h_ref, t_ref, g_ref, loss_ref, dh_ref, dhb_ref, dg_ref):
        def f(hv, gv):
            err = _rms_fn(hv, gv)[0] - t_ref[...]
            return 0.5 * jnp.sum(jnp.mean(err * err, axis=-1, keepdims=True), axis=0, keepdims=True)

        loss, vjp = jax.vjp(f, h_ref[...], g_ref[...])
        dh, dg = vjp(jnp.ones((1, 1), F32))
        dh_ref[...] = dh
        dhb_ref[...] = dh.astype(BF16)
        first = pl.program_id(0) == 0

        @pl.when(first)
        def _():
            loss_ref[...] = loss
            dg_ref[...] = dg

        @pl.when(jnp.logical_not(first))
        def _():
            loss_ref[...] += loss
            dg_ref[...] += dg

    return _pallas(
        body, name=name, grid=(t // tm,),
        in_specs=[_tok(D, tm), _tok(D, tm), _whole(g)],
        out_specs=[pl.BlockSpec((1, 1), lambda i: (0, 0)), _tok(D, tm), _tok(D, tm), _whole(g)],
        out_shape=[SDS((1, 1), F32), SDS((t, D), F32), SDS((t, D), BF16), SDS(g.shape, F32)],
        compiler_params=_params(("arbitrary",)),
    )(h, tgt, g)


def _res_half(acc, res):
    return (res + 0.5 * acc,)


def _res_full(acc, res):
    return (res + acc,)


def _half(acc):
    return (0.5 * acc,)


def _ffn_in(n, w_in, name, tm=1024):
    t = n.shape[0]
    tm = min(tm, t)

    def body(n_ref, w_ref, pre_ref, act_ref):
        a = n_ref[...]
        gate = lax.dot_general(a, w_ref[0], _DIMS["nt"], preferred_element_type=F32)
        up = lax.dot_general(a, w_ref[1], _DIMS["nt"], preferred_element_type=F32)
        pre_ref[0] = gate.astype(BF16)
        pre_ref[1] = up.astype(BF16)
        act_ref[...] = (gate * jax.nn.sigmoid(gate) * up).astype(BF16)

    pair = pl.BlockSpec((2, None, tm, FF_SHARD), lambda i, j: (0, j, i, 0))
    return _pallas(
        body, name=name, grid=(t // tm, 4),
        in_specs=[pl.BlockSpec((tm, D), lambda i, j: (i, 0)), pl.BlockSpec((2, None, FF_SHARD, D), lambda i, j: (0, j, 0, 0))],
        out_specs=[pair, pl.BlockSpec((None, tm, FF_SHARD), lambda i, j: (j, i, 0))],
        out_shape=[SDS((2, 4, t, FF_SHARD), BF16), SDS((4, t, FF_SHARD), BF16)],
        compiler_params=_params(("parallel", "parallel")),
    )(n, w_in.reshape(2, 4, FF_SHARD, D))


def _ffn_dact(dhb, w_out, pre, name, tm=1024, deps=()):
    t = dhb.shape[0]
    tm = min(tm, t)

    def body(d_ref, w_ref, pre_ref, *rest):
        o_ref = rest[-1]
        dact = 0.5 * lax.dot_general(d_ref[...], w_ref[...], _DIMS["nt"], preferred_element_type=F32)
        gate, up = pre_ref[0].astype(F32), pre_ref[1].astype(F32)
        sg = jax.nn.sigmoid(gate)
        o_ref[0] = (dact * up * (sg * (1.0 + gate * (1.0 - sg)))).astype(BF16)
        o_ref[1] = (dact * (gate * sg)).astype(BF16)

    pair = pl.BlockSpec((2, None, tm, FF_SHARD), lambda i, j: (0, j, i, 0))
    return _pallas(
        body, name=name, grid=(t // tm, 4),
        in_specs=[pl.BlockSpec((tm, D), lambda i, j: (i, 0)), pl.BlockSpec((None, FF_SHARD, D), lambda i, j: (j, 0, 0)), pair]
        + [ANY_SPEC] * len(deps),
        out_specs=pair, out_shape=SDS((2, 4, t, FF_SHARD), BF16),
        compiler_params=_params(("parallel", "parallel")),
    )(dhb, w_out, pre, *deps)


def _ffn_fwd(h, g, w_in, w_out, tag, deps=()):
    n = _rms(h, g, f"{tag}_rms", deps=deps)
    pre, act = _ffn_in(n, w_in, f"{tag}_in")
    out = _mm(act, w_out, reduce_j=True, tk=FF_SHARD, epi=_res_half, extras=(h[None],), name=f"{tag}_out")[0][0]
    return out, (h, n, pre, act)


def _ffn_bwd(dh, dhb, saved, g, w_in, w_out, tag, deps=(), hook=None):
    h, n, pre, act = saved
    t = h.shape[0]
    dpre = _ffn_dact(dhb, w_out, pre, f"{tag}_dact", deps=deps).reshape(N_DEV, t, FF_SHARD)
    dw_out = _mm(act, dhb[None], ta=True, tm=FF_SHARD, epi=_half, out_dtypes=(BF16,), deps=hook(dpre) if hook else (),
                 name=f"{tag}_dwout")[0]
    dh_in, dhb_in, dg = _mm_drms(dpre, w_in, h, g, dh, f"{tag}_dn", FF_SHARD, tb=False)
    dw_in = _mm(dpre, n[None], ta=True, tm=FF_SHARD, out_dtypes=(BF16,), name=f"{tag}_dwin")[0]
    return dh_in, dhb_in, dg, dw_in, dw_out


def _ple_fwd(h, g, pb, w_gate, w_proj, tag):
    t = h.shape[0]
    tm = 512
    n = _rms(h, g, f"{tag}_rms")
    e = _mm(pb[None], w_proj[None], name=f"{tag}_proj")[0][0]
    z = _mm(n[None], w_gate[None], name=f"{tag}_gate")[0][0]
    out = _rowop(lambda zz, ee, hh: (hh + _ple_fn(zz, ee)[0],), [(z, _tok(D, tm)), (e, _tok(D, tm)), (h, _tok(D, tm))], [],
                 [((t, D), F32, _tok(D, tm))], grid=(t // tm,), name=f"{tag}_mix")[0]
    return out, (h, n, e, z)


def _ple_bwd(dh, dhb, saved, g, pb, w_gate, tag, deps=()):
    h, n, e, z = saved
    t = h.shape[0]
    tm = 512
    dz, de = _rowop_bwd(_ple_fn, [(z, _tok(D, tm)), (e, _tok(D, tm))], [], [(dh, _tok(D, tm))], [(0,), (1,)],
                        [((t, D), (BF16,), _tok(D, tm)), ((t, D), (BF16,), _tok(D, tm))], grid=(t // tm,), name=f"{tag}_dmix",
                        deps=deps)
    dw_proj = _mm(pb[None], de[None], ta=True, out_dtypes=(BF16,), name=f"{tag}_dwproj")[0][0]
    dw_gate = _mm(n[None], dz[None], ta=True, out_dtypes=(BF16,), name=f"{tag}_dwgate")[0][0]
    dh_in, dhb_in, dg = _mm_drms(dz[None], w_gate[None], h, g, dh, f"{tag}_dn", 1024)
    return dh_in, dhb_in, dg, dw_gate, dw_proj


def _hyb_fwd(h, w, tag):
    t = h.shape[0]
    tm = 512
    hn = _rms(h, w["norm_mix"], f"{tag}_rms")
    proj = _mm(hn[None], w["hyb_in"][None], tn=512, name=f"{tag}_in")[0][0]
    u1 = _dwconv([(proj, 0), (proj, D // LANE)], w["conv_w"], w["conv_b"], width=CONV_W, glu=True, silu=False, cb=LANE,
                 name=f"{tag}_conv")
    u = _rowop(_lnswish_fn, [(u1, _tok(D, tm))], [w["ln_g"], w["ln_b"]], [((t, D), BF16, _tok(D, tm))], grid=(t // tm,),
               name=f"{tag}_ln")[0]
    xa = _dwconv([(proj, 3 * D // LANE)], w["sconv_w"], w["sconv_b"], width=SSM_CONV, glu=False, silu=True, cb=LANE,
                 name=f"{tag}_sconv")
    y, states = _ssd_fwd(xa, proj, w["dt_bias"], w["a_log"], w["d_skip"], w["ssm_norm"], f"{tag}_ssd")
    mixed = jnp.stack([u, y], axis=0)
    out = _mm(mixed, w["hyb_out"], reduce_j=True, epi=_res_full, extras=(h[None],), name=f"{tag}_out")[0][0]
    return out, (h, hn, proj, u1, xa, states, mixed)


def _hyb_bwd(dh, dhb, saved, w, tag):
    h, hn, proj, u1, xa, states, mixed = saved
    t = h.shape[0]
    tm = 512
    dmix = _mm(dhb[None], w["hyb_out"], tb=True, name=f"{tag}_dmix")[0]
    dw_out = _mm(mixed, dhb[None], ta=True, out_dtypes=(BF16,), name=f"{tag}_dwout")[0]
    du1, dln_g, dln_b = _rowop_bwd(_lnswish_fn, [(u1, _tok(D, tm))], [w["ln_g"], w["ln_b"]], [(dmix[0], _tok(D, tm))], [(0,)],
                                   [((t, D), (F32,), _tok(D, tm))], grid=(t // tm,), name=f"{tag}_dln")
    dval, dgate, dconv_w, dconv_b = _dwconv_bwd([(proj, 0), (proj, D // LANE)], w["conv_w"], w["conv_b"], du1,
                                                width=CONV_W, glu=True, silu=False, cb=LANE, name=f"{tag}_dconv")
    dxa, ddt, dz, g_dtb, g_alog, g_dsk, g_ng = _ssd_bwd(xa, proj, states, dmix[1], w["dt_bias"], w["a_log"], w["d_skip"],
                                                         w["ssm_norm"], f"{tag}_dssd")
    dxbc, dsconv_w, dsconv_b = _dwconv_bwd([(proj, 3 * D // LANE)], w["sconv_w"], w["sconv_b"], dxa, width=SSM_CONV,
                                           glu=False, silu=True, cb=LANE, name=f"{tag}_dsconv")
    dproj = jnp.concatenate([dval, dgate, dz, dxbc, ddt, jnp.zeros((t, HYB_PAD - DT_COL - LANE), BF16)], axis=1)
    dh_in, dhb_in, dg = _mm_drms(dproj[None], w["hyb_in"][None], h, w["norm_mix"], dh, f"{tag}_dhn", 1024)
    dw_in = _mm(hn[None], dproj[None], ta=True, tn=512, out_dtypes=(BF16,), name=f"{tag}_dwin")[0][0]
    grads = dict(norm_mix=dg, hyb_in=dw_in, hyb_out=dw_out, conv_w=dconv_w, conv_b=dconv_b, ln_g=dln_g, ln_b=dln_b,
                 sconv_w=dsconv_w, sconv_b=dsconv_b, dt_bias=g_dtb, a_log=g_alog, d_skip=g_dsk, ssm_norm=g_ng)
    return dh_in, dhb_in, grads


def _bias_epi(acc, row):
    return (acc + row,)


def _res_bias_epi(acc, res, row):
    return (res + acc + row,)


def _att_fwd(h, w, tables, tag):
    cos, sin, rot = tables
    hn = _rms(h, w["norm_mix"], f"{tag}_rms")
    qkv = _mm(hn[None], w["qkv"][None], tb=True, tn=512, epi=_bias_epi, rows=(w["b_qkv"],), name=f"{tag}_qkv")[0][0]
    o = _attn_fwd(qkv, cos, sin, w["sinks"], rot, f"{tag}_core")
    out = _mm(o[None], w["w_o"][None], epi=_res_bias_epi, extras=(h[None],), rows=(w["b_o"],), name=f"{tag}_out")[0][0]
    return out, (h, hn, qkv, o)


def _att_bwd(dh, dhb, saved, w, tables, tag):
    cos, sin, rot = tables
    h, hn, qkv, o = saved
    t = h.shape[0]
    tm = 512
    do = _mm(dhb[None], w["w_o"][None], tb=True, out_dtypes=(BF16,), name=f"{tag}_do")[0][0]
    dw_o = _mm(o[None], dhb[None], ta=True, out_dtypes=(BF16,), name=f"{tag}_dwo")[0][0]
    db_o = _rowop_bwd(lambda xx, bb: (xx + bb,), [(dh, _tok(D, tm))], [w["b_o"]], [(dh, _tok(D, tm))], [], [],
                      grid=(t // tm,), name=f"{tag}_dbo")[0]
    dqkv, db_qkv, dsinks = _attn_bwd(qkv, do, cos, sin, w["sinks"], rot, f"{tag}_dcore")
    dh_in, dhb_in, dg = _mm_drms(dqkv[None], w["qkv"][None], h, w["norm_mix"], dh, f"{tag}_dhn", 512, tb=False)
    dw_qkv = _mm(dqkv[None], hn[None], ta=True, tm=512, out_dtypes=(BF16,), name=f"{tag}_dwqkv")[0][0]
    grads = dict(norm_mix=dg, qkv=dw_qkv, b_qkv=db_qkv, sinks=dsinks, w_o=dw_o, b_o=db_o)
    return dh_in, dhb_in, grads


def _rope_tables(t):
    inv = ROPE_THETA ** (-jnp.arange(0, 64, 2, dtype=F32) / 64)
    ang = jnp.arange(t, dtype=F32)[:, None] * inv[None, :]
    cos, sin = jnp.tile(jnp.cos(ang), (1, 4)), jnp.tile(jnp.sin(ang), (1, 4))
    rot = np.zeros((LANE, LANE), np.float32)
    for j in range(LANE):
        if j % 64 < 32:
            rot[j + 32, j] = -1.0
        else:
            rot[j - 32, j] = 1.0
    return cos, sin, jnp.asarray(rot)


def _local_step(x, p, tgt, layers, final_norm):
    _restart_chain()
    tables = _rope_tables(x.shape[0])
    pb = p.astype(BF16)
    h, saved = x, []
    for i, w in enumerate(layers):
        h, s = _layer_fwd(i, h, w, pb[i], tables)
        saved.append(s)
    loss, dh, dhb, d_final = _loss_head(h, tgt, final_norm, "loss_head")
    grads = [None] * len(layers)
    for i in reversed(range(len(layers))):
        dh, dhb, head = _layer_bwd_head(i, dh, dhb, saved[i], layers[i], pb[i])
        dh, dhb, tail = _layer_bwd_tail(i, dh, dhb, saved[i], layers[i], tables)
        grads[i] = {**head, **tail}
    return loss[0, 0], dh, grads, d_final


def _layer_fwd(i, h, w, pb, tables, deps=()):
    s = {}
    h, s["ffn1"] = _ffn_fwd(h, w["norm_ffn1"], w["ffn1_in"], w["ffn1_out"], f"l{i}_ffn1", deps=deps)
    if i % 2 == 0:
        h, s["mix"] = _hyb_fwd(h, w, f"l{i}_hyb")
    else:
        h, s["mix"] = _att_fwd(h, w, tables, f"l{i}_att")
    h, s["ffn2"] = _ffn_fwd(h, w["norm_ffn2"], w["ffn2_in"], w["ffn2_out"], f"l{i}_ffn2")
    h, s["ple"] = _ple_fwd(h, w["ple_norm"], pb, w["ple_gate"], w["ple_proj"], f"l{i}_ple")
    return h, s


def _layer_bwd_head(i, dh, dhb, s, w, pb, deps=()):
    g = {}
    dh, dhb, g["ple_norm"], g["ple_gate"], g["ple_proj"] = _ple_bwd(dh, dhb, s["ple"], w["ple_norm"], pb, w["ple_gate"],
                                                                    f"l{i}_ple", deps=deps)
    return dh, dhb, g


def _layer_bwd_tail(i, dh, dhb, s, w, tables, deps=()):
    g = {}
    dh, dhb, g["norm_ffn2"], g["ffn2_in"], g["ffn2_out"] = _ffn_bwd(dh, dhb, s["ffn2"], w["norm_ffn2"], w["ffn2_in"],
                                                                    w["ffn2_out"], f"l{i}_ffn2", deps=deps)
    if i % 2 == 0:
        dh, dhb, gm = _hyb_bwd(dh, dhb, s["mix"], w, f"l{i}_hyb")
    else:
        dh, dhb, gm = _att_bwd(dh, dhb, s["mix"], w, tables, f"l{i}_att")
    g.update(gm)
    dh, dhb, g["norm_ffn1"], g["ffn1_in"], g["ffn1_out"] = _ffn_bwd(dh, dhb, s["ffn1"], w["norm_ffn1"], w["ffn1_in"],
                                                                    w["ffn1_out"], f"l{i}_ffn1")
    return dh, dhb, g


def _cols(g):
    full = jnp.moveaxis(g, 0, -2)
    return full.reshape(*full.shape[:-2], N_DEV * g.shape[-1])


def _uncols(full):
    split = full.reshape(*full.shape[:-1], N_DEV, full.shape[-1] // N_DEV)
    return jnp.moveaxis(split, -2, 0)


def _lane_pad(v):
    return jnp.pad(v, ((0, 0), (0, LANE - v.shape[1])))


def _build_layers(gw, gs, rep):
    return [_build_layer(i, gw, gs, rep) for i in range(2)]


def _build_layer(i, gw, gs, rep, parts=("ffn1", "mix", "ffn2", "ple")):
    w = {}
    for f in ("ffn1", "ffn2"):
        if f in parts:
            w[f"norm_{f}"] = rep[f"norm_{f}"][i][None]
            w[f"{f}_in"] = gw[f"{f}_w_in", i]
            w[f"{f}_out"] = gw[f"{f}_w_out", i].reshape(4, FF_SHARD, D)
    if "ple" in parts:
        w["ple_norm"] = rep["ple_norm"][i][None]
        w["ple_gate"] = gw["ple_gate_w", i].reshape(D, D)
        w["ple_proj"] = _cols(gw["ple_proj_w", i])
    if "mix" not in parts:
        return w
    w["norm_mix"] = rep["norm_mix"][i][None]
    if i == 0:
        w["hyb_in"] = jnp.pad(_cols(gw["hyb_w_in", 0]), ((0, 0), (0, HYB_PAD - HYB_IN)))
        w["hyb_out"] = gw["hyb_w_out", 0].reshape(2, D, D)
        w["conv_w"] = _cols(gs["conv_dw_w"][:, 0])
        w["sconv_w"] = _cols(gs["ssm_conv_w"][:, 0])
        w["conv_b"], w["ln_g"], w["ln_b"] = rep["conv_dw_b"], rep["conv_ln_g"], rep["conv_ln_b"]
        w["sconv_b"], w["ssm_norm"] = rep["ssm_conv_b"], rep["ssm_norm"]
        w["dt_bias"], w["a_log"], w["d_skip"] = (_lane_pad(rep[k]) for k in ("ssm_dt_bias", "ssm_a_log", "ssm_d"))
    else:
        w["qkv"] = gw["att_w_qkv", 0].reshape(-1, D)
        w["w_o"] = gw["att_w_o", 0].reshape(D, D)
        w["b_qkv"] = gs["att_b_qkv"][:, 0].reshape(1, -1)
        w["b_o"] = gs["att_b_o"][:, 0].reshape(1, -1)
        w["sinks"] = _lane_pad(rep["att_sinks"])
    return w


def _big_grads(i, g):
    big = {}
    for f in ("ffn1", "ffn2"):
        if f"{f}_in" in g:
            big[f"{f}_w_in", i] = g[f"{f}_in"]
            big[f"{f}_w_out", i] = g[f"{f}_out"].reshape(N_DEV, D_FF // N_DEV, D)
    if "ple_gate" in g:
        big["ple_gate_w", i] = g["ple_gate"].reshape(N_DEV, D // N_DEV, D)
        big["ple_proj_w", i] = _uncols(g["ple_proj"])
    if "hyb_in" in g:
        big["hyb_w_in", 0] = _uncols(g["hyb_in"][:, :HYB_IN])
        big["hyb_w_out", 0] = g["hyb_out"].reshape(N_DEV, 2 * D // N_DEV, D)
    if "qkv" in g:
        big["att_w_qkv", 0] = g["qkv"].reshape(N_DEV, -1, D)
        big["att_w_o", 0] = g["w_o"].reshape(N_DEV, D // N_DEV, D)
    return big


def _collect_grads(grads, d_final):
    g0, g1 = grads
    big, small = {**_big_grads(0, g0), **_big_grads(1, g1)}, {}
    for f in ("ffn1", "ffn2"):
        small[f"norm_{f}"] = jnp.concatenate([g[f"norm_{f}"] for g in grads], axis=0)
    small["norm_mix"] = jnp.concatenate([g["norm_mix"] for g in grads], axis=0)
    small["ple_norm"] = jnp.concatenate([g["ple_norm"] for g in grads], axis=0)
    small["conv_dw_w"] = g0["conv_w"][None]
    small["conv_dw_b"], small["conv_ln_g"], small["conv_ln_b"] = g0["conv_b"], g0["ln_g"], g0["ln_b"]
    small["ssm_conv_w"] = g0["sconv_w"][None]
    small["ssm_conv_b"], small["ssm_norm"] = g0["sconv_b"], g0["ssm_norm"]
    small["ssm_dt_bias"], small["ssm_a_log"], small["ssm_d"] = (g0[k][:, :SSM_HEADS] for k in ("dt_bias", "a_log", "d_skip"))
    small["att_b_qkv"], small["att_b_o"] = g1["b_qkv"], g1["b_o"]
    small["att_sinks"] = g1["sinks"][:, :SSM_HEADS]
    small["final_norm"] = d_final[0]
    return big, small


MESH = pl.DeviceIdType.MESH


def _place():
    return lax.axis_index("x"), lax.axis_index("y"), lax.axis_index("c")


def _all_gather(blocks, space, name):
    nb = len(blocks)

    def body(*refs):
        x_refs, out_refs, (send_sems, recv_sems, local_sem) = refs[:nb], refs[nb:2 * nb], refs[2 * nb:]
        x, y, c = _place()
        me, sibling = (x, y, c), (x, y, 1 - c)
        chips = [(1 - x, y), (x, 1 - y), (1 - x, 1 - y)]

        def copies(k, blk, to, own=False):
            idx = 4 * blk[0] + 2 * blk[1] + blk[2]
            return [pltpu.make_async_remote_copy(src_ref=x_ref if own else out_ref.at[idx], dst_ref=out_ref.at[idx],
                                                 send_sem=send_sems.at[k, b], recv_sem=recv_sems.at[k, b], device_id=to,
                                                 device_id_type=MESH) for b, (x_ref, out_ref) in enumerate(zip(x_refs, out_refs))]

        mine = [pltpu.make_async_copy(x_ref, out_ref.at[4 * x + 2 * y + c], local_sem.at[b])
                for b, (x_ref, out_ref) in enumerate(zip(x_refs, out_refs))]
        first = copies(0, me, sibling, own=True)
        for j, chip in enumerate(chips):
            first += copies(1 + j, me, (*chip, c), own=True)
        for cp in mine + first:
            cp.start()
        passed = []
        for j, chip in enumerate(chips):
            for cp in copies(1 + j, (*chip, c), me):
                cp.wait_recv()
            onward = copies(4 + j, (*chip, c), sibling)
            for cp in onward:
                cp.start()
            passed += onward
        for cp in copies(0, sibling, me):
            cp.wait_recv()
        for j, chip in enumerate(chips):
            for cp in copies(4 + j, (*chip, 1 - c), me):
                cp.wait_recv()
        for cp in first + passed:
            cp.wait_send()
        for cp in mine:
            cp.wait()

    spec = pl.BlockSpec(memory_space=space)
    return _pallas(
        body, name=name, out_shape=[SDS((N_DEV,) + b.shape, b.dtype) for b in blocks],
        in_specs=[spec] * nb, out_specs=[spec] * nb,
        scratch_shapes=[pltpu.SemaphoreType.DMA((7, nb)), pltpu.SemaphoreType.DMA((7, nb)), pltpu.SemaphoreType.DMA((nb,))],
    )(*blocks)


def _pair_exchange(parts, name):
    nb = len(parts)

    def body(*refs):
        p_refs, got_refs, (send_sems, recv_sems) = refs[:nb], refs[nb:2 * nb], refs[2 * nb:]
        x, y, c = _place()
        copies = [pltpu.make_async_remote_copy(src_ref=p_ref.at[2 * q + (1 - c)], dst_ref=got_ref.at[q],
                                               send_sem=send_sems.at[q, b], recv_sem=recv_sems.at[q, b], device_id=(x, y, 1 - c),
                                               device_id_type=MESH)
                  for q in range(4) for b, (p_ref, got_ref) in enumerate(zip(p_refs, got_refs))]
        for cp in copies:
            cp.start()
        for cp in copies:
            cp.wait_recv()
        for cp in copies:
            cp.wait_send()

    hbm = pl.BlockSpec(memory_space=pltpu.HBM)
    return _pallas(
        body, name=name, out_shape=[SDS((4,) + p.shape[1:], p.dtype) for p in parts], in_specs=[hbm] * nb, out_specs=[hbm] * nb,
        scratch_shapes=[pltpu.SemaphoreType.DMA((4, nb)), pltpu.SemaphoreType.DMA((4, nb))],
    )(*parts)


HBM_SPEC = pl.BlockSpec(memory_space=pltpu.HBM)
SEM_SPEC = pl.BlockSpec(memory_space=pltpu.SEMAPHORE)
EFFECT = pltpu.SideEffectType.DATAFLOW_SIDE_EFFECTING


def _plan_descriptors(plan, srcs, lands, send_sems, recv_sems, local_sems, arriving):
    remote, local = plan(*_place())
    pick = lambda ref, slot: ref if slot is None else ref.at[slot]
    rem = [pltpu.make_async_remote_copy(src_ref=pick(srcs[si], ss), dst_ref=lands[li].at[rs if arriving else ds],
                                        send_sem=send_sems.at[k], recv_sem=recv_sems.at[k], device_id=dev, device_id_type=MESH)
           for k, (si, ss, li, ds, dev, rs) in enumerate(remote)]
    loc = [pltpu.make_async_copy(pick(srcs[si], ss), lands[li].at[ds], local_sems.at[k])
           for k, (si, ss, li, ds) in enumerate(local)]
    return rem, loc


def _plan_counts(plan):
    remote, local = plan(0, 0, 0)
    return len(remote), max(len(local), 1)


def _exchange_start(srcs, land_shapes, plan, name):
    ns, nl = len(srcs), len(land_shapes)
    n_remote, n_local = _plan_counts(plan)
    lands = [pltpu.with_memory_space_constraint(lax.empty(s.shape, s.dtype), pltpu.HBM) for s in land_shapes]
    srcs = [pltpu.with_memory_space_constraint(s, pltpu.HBM) for s in srcs]

    def body(*refs):
        src_refs, land_refs = refs[:ns], refs[ns:ns + nl]
        send_sems, recv_sems, local_sems = refs[ns + nl:ns + nl + 3]
        token = refs[-1]
        rem, loc = _plan_descriptors(plan, src_refs, land_refs, send_sems, recv_sems, local_sems, arriving=False)
        for cp in loc + rem:
            cp.start()
        token[...] = jnp.zeros_like(token)

    outs = _pallas(
        body, name=name,
        out_shape=[pltpu.SemaphoreType.DMA((n_remote,)), pltpu.SemaphoreType.DMA((n_remote,)), pltpu.SemaphoreType.DMA((n_local,))]
        + [pltpu.HBM(a.shape, a.dtype) for a in srcs + lands] + [SDS((8, LANE), F32)],
        in_specs=[HBM_SPEC] * (ns + nl),
        out_specs=[SEM_SPEC] * 3 + [HBM_SPEC] * (ns + nl) + [pl.BlockSpec(memory_space=pltpu.VMEM)],
        input_output_aliases={i: 3 + i for i in range(ns + nl)},
        compiler_params=pltpu.CompilerParams(has_side_effects=EFFECT),
    )(*srcs, *lands)
    return (outs[:3], outs[3:3 + ns], outs[3 + ns:3 + ns + nl]), outs[-1]


def _exchange_wait(state, after, plan, name):
    sems, srcs, lands = state
    ns, nl = len(srcs), len(lands)

    def body(*refs):
        src_refs, land_refs = refs[:ns], refs[ns:ns + nl]
        send_sems, recv_sems, local_sems = refs[ns + nl:ns + nl + 3]
        rem, loc = _plan_descriptors(plan, src_refs, land_refs, send_sems, recv_sems, local_sems, arriving=True)
        for cp in rem:
            cp.wait_send()
            cp.wait_recv()
        for cp in loc:
            cp.wait()

    outs = _pallas(
        body, name=name, out_shape=[pltpu.HBM(a.shape, a.dtype) for a in list(srcs) + list(lands)],
        in_specs=[HBM_SPEC] * (ns + nl) + [SEM_SPEC] * 3 + [ANY_SPEC], out_specs=[HBM_SPEC] * (ns + nl),
        input_output_aliases={i: i for i in range(ns + nl)},
        compiler_params=pltpu.CompilerParams(has_side_effects=EFFECT),
    )(*srcs, *lands, *sems, after)
    return outs[:ns], outs[ns:]


def _gather_plan(nb):
    def plan(x, y, c):
        me = 4 * x + 2 * y + c
        remote = []
        for b in range(nb):
            for r in range(1, N_DEV):
                tx, ty, tc = (1 - x if r & 4 else x), (1 - y if r & 2 else y), (1 - c if r & 1 else c)
                remote.append((b, None, b, me, (tx, ty, tc), 4 * tx + 2 * ty + tc))
        return remote, [(b, None, b, me) for b in range(nb)]
    return plan


def _pair_plan(nb):
    def plan(x, y, c):
        return [(b, 2 * q + (1 - c), b, q, (x, y, 1 - c), q) for b in range(nb) for q in range(4)], []
    return plan


def _chip_plan(nb):
    def plan(x, y, c):
        own = 2 * x + y
        chips = [(1 - x, y), (x, 1 - y), (1 - x, 1 - y)]
        remote = [(b, 2 * cx + cy, b, own, (cx, cy, c), 2 * cx + cy) for b in range(nb) for cx, cy in chips]
        return remote, [(b, own, b, own) for b in range(nb)]
    return plan


def _row_tile(r, cap=4608):
    return max(d for d in range(16, min(r, cap) + 1, 16) if r % d == 0)


def _pair_add(parts, got, core, name):
    _, r, cdim = parts.shape
    tr = _row_tile(r)

    def body(core_ref, p_ref, g_ref, o_ref):
        o_ref[...] = (p_ref[...].astype(F32) + g_ref[...].astype(F32)).astype(o_ref.dtype)

    return pl.pallas_call(
        body, name=name, out_shape=SDS((4, r, cdim), BF16),
        grid_spec=pltpu.PrefetchScalarGridSpec(
            num_scalar_prefetch=1, grid=(4, r // tr),
            in_specs=[pl.BlockSpec((None, tr, cdim), lambda q, i, core_ref: (2 * q + core_ref[0], i, 0)),
                      pl.BlockSpec((None, tr, cdim), lambda q, i, core_ref: (q, i, 0))],
            out_specs=pl.BlockSpec((None, tr, cdim), lambda q, i, core_ref: (q, i, 0))),
        compiler_params=_params(("parallel", "parallel")),
    )(core, parts, got)


def _chip_exchange(sums, name):
    nb = len(sums)

    def body(*refs):
        b_refs, out_refs, (send_sems, recv_sems, local_sem) = refs[:nb], refs[nb:2 * nb], refs[2 * nb:]
        x, y, c = _place()
        own = 2 * x + y
        chips = [(1 - x, y), (x, 1 - y), (1 - x, 1 - y)]

        def copies(k, chip, src_slot, dst_slot):
            return [pltpu.make_async_remote_copy(src_ref=b_ref.at[src_slot], dst_ref=out_ref.at[dst_slot],
                                                 send_sem=send_sems.at[k, b], recv_sem=recv_sems.at[k, b], device_id=(*chip, c),
                                                 device_id_type=MESH) for b, (b_ref, out_ref) in enumerate(zip(b_refs, out_refs))]

        mine = [pltpu.make_async_copy(b_ref.at[own], out_ref.at[own], local_sem.at[b])
                for b, (b_ref, out_ref) in enumerate(zip(b_refs, out_refs))]
        sends = []
        for k, chip in enumerate(chips):
            sends += copies(k, chip, 2 * chip[0] + chip[1], own)
        for cp in mine + sends:
            cp.start()
        for k, chip in enumerate(chips):
            for cp in copies(k, chip, own, 2 * chip[0] + chip[1]):
                cp.wait_recv()
        for cp in sends:
            cp.wait_send()
        for cp in mine:
            cp.wait()

    hbm = pl.BlockSpec(memory_space=pltpu.HBM)
    return _pallas(
        body, name=name, out_shape=[SDS(s.shape, s.dtype) for s in sums], in_specs=[hbm] * nb, out_specs=[hbm] * nb,
        scratch_shapes=[pltpu.SemaphoreType.DMA((3, nb)), pltpu.SemaphoreType.DMA((3, nb)), pltpu.SemaphoreType.DMA((nb,))],
    )(*sums)


def _sum_slots(parts, name):
    nj, r, cdim = parts.shape
    tr = _row_tile(r)

    def body(p_ref, o_ref):
        acc = p_ref[0].astype(F32)
        for j in range(1, nj):
            acc = acc + p_ref[j].astype(F32)
        o_ref[...] = acc

    return _pallas(
        body, name=name, out_shape=SDS((r, cdim), F32), grid=(r // tr,),
        in_specs=[pl.BlockSpec((nj, tr, cdim), lambda i: (0, i, 0))], out_specs=pl.BlockSpec((tr, cdim), lambda i: (i, 0)),
        compiler_params=_params(("parallel",)),
    )(parts)


def _adamw_update(wv, gv, mv, vv):
    nm = ADAM_B1 * mv + (1.0 - ADAM_B1) * gv
    nv = ADAM_B2 * vv + (1.0 - ADAM_B2) * (gv * gv)
    m_hat = nm / (1.0 - ADAM_B1 ** ADAM_STEP)
    v_hat = nv / (1.0 - ADAM_B2 ** ADAM_STEP)
    return -ADAM_LR * (m_hat / (jnp.sqrt(v_hat) + ADAM_EPS) + ADAM_WD * wv), nm, nv


def _adamw_summed(w, m, v, by_chip, name):
    nl, r, cdim = w.shape
    tr = _row_tile(r, 512)
    nblk = r // tr

    def body(*refs):
        chip_refs, (w_ref, m_ref, v_ref, g_ref, d_ref, nm_ref, nv_ref) = refs[:nl], refs[nl:]
        layer = pl.program_id(0)
        gv = None
        for ll, c_ref in enumerate(chip_refs):
            s = c_ref[0].astype(F32)
            for q in range(1, 4):
                s = s + c_ref[q].astype(F32)
            gv = s if gv is None else jnp.where(layer == ll, s, gv)
        g_ref[...] = gv
        d_ref[...], nm_ref[...], nv_ref[...] = _adamw_update(w_ref[...], gv, m_ref[...], v_ref[...])

    def chip_map(ll):
        return lambda l, i: (0, jnp.where(l == ll, i, jnp.where(l > ll, nblk - 1, 0)), 0)

    spec = pl.BlockSpec((None, tr, cdim), lambda l, i: (l, i, 0))
    return _pallas(
        body, name=name, grid=(nl, nblk),
        in_specs=[pl.BlockSpec((4, tr, cdim), chip_map(ll)) for ll in range(nl)] + [spec] * 3,
        out_specs=[spec] * 4, out_shape=[SDS((nl, r, cdim), F32)] * 4,
        compiler_params=_params(("arbitrary", "arbitrary")),
    )(*by_chip, w, m, v)


def _adamw(w, g, m, v, name):
    shape = w.shape
    cdim = shape[-1]
    w2, g2, m2, v2 = (a.reshape(-1, cdim) for a in (w, g, m, v))
    r = w2.shape[0]
    tr = next(d for d in (512, 352, 256, 128, 64, 32, 16, 8, r) if r % d == 0)

    def body(w_ref, g_ref, m_ref, v_ref, d_ref, nm_ref, nv_ref):
        d_ref[...], nm_ref[...], nv_ref[...] = _adamw_update(w_ref[...], g_ref[...], m_ref[...], v_ref[...])

    spec = pl.BlockSpec((tr, cdim), lambda i: (i, 0))
    outs = _pallas(
        body, name=name, grid=(r // tr,), in_specs=[spec] * 4, out_specs=[spec] * 3, out_shape=[SDS((r, cdim), F32)] * 3,
        compiler_params=_params(("parallel",)),
    )(w2, g2, m2, v2)
    return tuple(o.reshape(shape) for o in outs)


WEIGHTS = ("norm_ffn1", "ffn1_w_in", "ffn1_w_out", "norm_mix", "norm_ffn2", "ffn2_w_in", "ffn2_w_out", "ple_norm", "ple_gate_w",
           "ple_proj_w", "hyb_w_in", "conv_dw_w", "conv_dw_b", "conv_ln_g", "conv_ln_b", "ssm_conv_w", "ssm_conv_b", "ssm_dt_bias",
           "ssm_a_log", "ssm_d", "ssm_norm", "hyb_w_out", "att_w_qkv", "att_b_qkv", "att_sinks", "att_w_o", "att_b_o", "final_norm")
BIG = ("ffn1_w_in", "ffn1_w_out", "ffn2_w_in", "ffn2_w_out", "ple_gate_w", "ple_proj_w", "hyb_w_in", "hyb_w_out", "att_w_qkv",
       "att_w_o")
SMALL_SHARDED = {"conv_dw_w": 2, "ssm_conv_w": 2, "att_b_qkv": 1, "att_b_o": 1}
SMALL = tuple(n for n in WEIGHTS if n not in BIG)
TRANSPOSED = ("ffn1_w_in", "ffn2_w_in", "att_w_qkv")
PACK_ROWS = 16


def _pack(arrays, lead=0):
    pieces = []
    for a in arrays:
        flat = a.reshape(*a.shape[:lead], -1)
        size = flat.shape[-1]
        padded = -(-size // (PACK_ROWS * LANE)) * PACK_ROWS * LANE
        flat = jnp.pad(flat, [(0, 0)] * lead + [(0, padded - size)])
        pieces.append(flat.reshape(*a.shape[:lead], padded // LANE, LANE))
    return jnp.concatenate(pieces, axis=lead)


def _unpack(buf, shapes, lead=0):
    out, row = [], 0
    for shape in shapes:
        size = math.prod(shape)
        rows = -(-size // (PACK_ROWS * LANE)) * PACK_ROWS
        piece = lax.slice_in_dim(buf, row, row + rows, axis=lead)
        piece = piece.reshape(*buf.shape[:lead], rows * LANE)
        out.append(lax.slice_in_dim(piece, 0, size, axis=lead).reshape(*buf.shape[:lead], *shape))
        row += rows
    return out


def kernel(x, p, norm_ffn1, ffn1_w_in, ffn1_w_out, norm_mix, norm_ffn2, ffn2_w_in, ffn2_w_out, ple_norm, ple_gate_w, ple_proj_w, hyb_w_in, conv_dw_w, conv_dw_b, conv_ln_g, conv_ln_b, ssm_conv_w, ssm_conv_b, ssm_dt_bias, ssm_a_log, ssm_d, ssm_norm, hyb_w_out, att_w_qkv, att_b_qkv, att_sinks, att_w_o, att_b_o, final_norm, loss_target, m_norm_ffn1, m_ffn1_w_in, m_ffn1_w_out, m_norm_mix, m_norm_ffn2, m_ffn2_w_in, m_ffn2_w_out, m_ple_norm, m_ple_gate_w, m_ple_proj_w, m_hyb_w_in, m_conv_dw_w, m_conv_dw_b, m_conv_ln_g, m_conv_ln_b, m_ssm_conv_w, m_ssm_conv_b, m_ssm_dt_bias, m_ssm_a_log, m_ssm_d, m_ssm_norm, m_hyb_w_out, m_att_w_qkv, m_att_b_qkv, m_att_sinks, m_att_w_o, m_att_b_o, m_final_norm, v_norm_ffn1, v_ffn1_w_in, v_ffn1_w_out, v_norm_mix, v_norm_ffn2, v_ffn2_w_in, v_ffn2_w_out, v_ple_norm, v_ple_gate_w, v_ple_proj_w, v_hyb_w_in, v_conv_dw_w, v_conv_dw_b, v_conv_ln_g, v_conv_ln_b, v_ssm_conv_w, v_ssm_conv_b, v_ssm_dt_bias, v_ssm_a_log, v_ssm_d, v_ssm_norm, v_hyb_w_out, v_att_w_qkv, v_att_b_qkv, v_att_sinks, v_att_w_o, v_att_b_o, v_final_norm):
    args = (norm_ffn1, ffn1_w_in, ffn1_w_out, norm_mix, norm_ffn2, ffn2_w_in, ffn2_w_out, ple_norm, ple_gate_w, ple_proj_w, hyb_w_in, conv_dw_w, conv_dw_b, conv_ln_g, conv_ln_b, ssm_conv_w, ssm_conv_b, ssm_dt_bias, ssm_a_log, ssm_d, ssm_norm, hyb_w_out, att_w_qkv, att_b_qkv, att_sinks, att_w_o, att_b_o, final_norm)
    moments_m = (m_norm_ffn1, m_ffn1_w_in, m_ffn1_w_out, m_norm_mix, m_norm_ffn2, m_ffn2_w_in, m_ffn2_w_out, m_ple_norm, m_ple_gate_w, m_ple_proj_w, m_hyb_w_in, m_conv_dw_w, m_conv_dw_b, m_conv_ln_g, m_conv_ln_b, m_ssm_conv_w, m_ssm_conv_b, m_ssm_dt_bias, m_ssm_a_log, m_ssm_d, m_ssm_norm, m_hyb_w_out, m_att_w_qkv, m_att_b_qkv, m_att_sinks, m_att_w_o, m_att_b_o, m_final_norm)
    moments_v = (v_norm_ffn1, v_ffn1_w_in, v_ffn1_w_out, v_norm_mix, v_norm_ffn2, v_ffn2_w_in, v_ffn2_w_out, v_ple_norm, v_ple_gate_w, v_ple_proj_w, v_hyb_w_in, v_conv_dw_w, v_conv_dw_b, v_conv_ln_g, v_conv_ln_b, v_ssm_conv_w, v_ssm_conv_b, v_ssm_dt_bias, v_ssm_a_log, v_ssm_d, v_ssm_norm, v_hyb_w_out, v_att_w_qkv, v_att_b_qkv, v_att_sinks, v_att_w_o, v_att_b_o, v_final_norm)
    w = dict(zip(WEIGHTS, args))
    m = dict(zip(WEIGHTS, moments_m))
    v = dict(zip(WEIGHTS, moments_v))
    cx, cy, cc = _place()
    me = 4 * cx + 2 * cy + cc

    core = jnp.reshape(cc, (1,)).astype(jnp.int32)
    layer_of = lambda n, i: 1 if n.startswith("att_") else i
    keys = [[(n, i) for n in BIG for i in range(w[n].shape[0]) if layer_of(n, i) == layer] for layer in range(2)]

    first = [key for key in keys[0] if key[0].startswith("ffn1")]
    mixer = [key for key in keys[0] if key[0].startswith("hyb")]
    rest0 = [key for key in keys[0] if key not in first + mixer]
    gw, by_chip = {}, {}
    view = lambda a, n: jnp.swapaxes(a, 1, 2) if n in TRANSPOSED else a
    block = lambda n, i: view(w[n], n)[i].astype(BF16)

    def gather_later(group, name):
        blocks = [block(n, i) for n, i in group]
        plan = _gather_plan(len(blocks))
        state, token = _exchange_start(blocks, [SDS((N_DEV,) + b.shape, BF16) for b in blocks], plan, f"{name}_start")
        return token, lambda after: gw.update(zip(group, _exchange_wait(state, after, plan, f"{name}_wait")[1]))

    def reduce_later(group, big, name):
        pair_plan, chip_plan = _pair_plan(len(group)), _chip_plan(len(group))
        parts = [big[key] for key in group]
        pair, token = _exchange_start(parts, [SDS((4,) + pt.shape[1:], BF16) for pt in parts], pair_plan, f"{name}_pair_start")
        stage = {}

        def middle(after):
            thru, got = _exchange_wait(pair, after, pair_plan, f"{name}_pair_wait")
            sums = [_pair_add(pt, gt, core, f"grads_pair_add_{n}_{i}") for pt, gt, (n, i) in zip(thru, got, group)]
            stage["chip"], chip_token = _exchange_start(sums, [SDS(s.shape, BF16) for s in sums], chip_plan, f"{name}_chip_start")
            return chip_token

        def finish(after):
            by_chip.update(zip(group, _exchange_wait(stage["chip"], after, chip_plan, f"{name}_chip_wait")[1]))

        return token, middle, finish

    _restart_chain()
    *gathered_first, gathered_small = _all_gather([block(n, i) for n, i in first] + [_pack([w[n] for n in SMALL_SHARDED])],
                                                  pltpu.HBM, "gather_weights_first")
    gw.update(zip(first, gathered_first))
    mixer_token, mixer_arrived = gather_later(mixer, "gather_weights_mixer")
    rest0_token, rest0_arrived = gather_later(rest0, "gather_weights_rest")
    layer1_token, layer1_arrived = gather_later(keys[1], "gather_weights_l1")
    gs = dict(zip(SMALL_SHARDED, _unpack(gathered_small, [w[n].shape for n in SMALL_SHARDED], lead=1)))
    rep = {n: w[n] for n in SMALL if n not in SMALL_SHARDED}

    tables = _rope_tables(x.shape[1])
    pb = p[:, 0].astype(BF16)
    w0, s0 = _build_layer(0, gw, gs, rep, parts=("ffn1",)), {}
    h, s0["ffn1"] = _ffn_fwd(x[0], w0["norm_ffn1"], w0["ffn1_in"], w0["ffn1_out"], "l0_ffn1",
                             deps=(mixer_token, rest0_token, layer1_token))
    mixer_arrived(h)
    w0.update(_build_layer(0, gw, gs, rep, parts=("mix",)))
    h, s0["mix"] = _hyb_fwd(h, w0, "l0_hyb")
    rest0_arrived(h)
    w0.update(_build_layer(0, gw, gs, rep, parts=("ffn2", "ple")))
    h, s0["ffn2"] = _ffn_fwd(h, w0["norm_ffn2"], w0["ffn2_in"], w0["ffn2_out"], "l0_ffn2")
    h, s0["ple"] = _ple_fwd(h, w0["ple_norm"], pb[0], w0["ple_gate"], w0["ple_proj"], "l0_ple")
    layer1_arrived(h)
    w1 = _build_layer(1, gw, gs, rep)
    h, s1 = _layer_fwd(1, h, w1, pb[1], tables)
    loss, dh, dhb, d_final = _loss_head(h, loss_target[0], final_norm[None], "loss_head")
    loss = lax.psum(loss[0, 0], ("x", "y", "c"))

    dh, dhb, head1 = _layer_bwd_head(1, dh, dhb, s1, w1, pb[1])
    dh, dhb, tail1 = _layer_bwd_tail(1, dh, dhb, s1, w1, tables)
    grads1 = {**head1, **tail1}
    l1_token, l1_middle, l1_finish = reduce_later(keys[1], _big_grads(1, grads1), "grads_l1")
    dh, dhb, grads0 = _layer_bwd_head(0, dh, dhb, s0, w0, pb[0], deps=(l1_token,))
    dh, dhb, grads0["norm_ffn2"], grads0["ffn2_in"], grads0["ffn2_out"] = _ffn_bwd(
        dh, dhb, s0["ffn2"], w0["norm_ffn2"], w0["ffn2_in"], w0["ffn2_out"], "l0_ffn2", deps=(l1_middle(dh),))
    dh, dhb, mixer_grads = _hyb_bwd(dh, dhb, s0["mix"], w0, "l0_hyb")
    grads0.update(mixer_grads)
    l0_token, l0_middle, l0_finish = reduce_later(mixer + rest0, _big_grads(0, grads0), "grads_l0")
    dx, dhb, grads0["norm_ffn1"], grads0["ffn1_in"], grads0["ffn1_out"] = _ffn_bwd(
        dh, dhb, s0["ffn1"], w0["norm_ffn1"], w0["ffn1_in"], w0["ffn1_out"], "l0_ffn1", deps=(l0_token,),
        hook=lambda dpre: (l0_middle(dpre),))
    l1_finish(dx)
    l0_finish(dx)
    big0 = _big_grads(0, grads0)
    parts0 = [big0[key] for key in first]
    got0 = _pair_exchange(parts0, "grads_first_pair_exchange")
    sums0 = [_pair_add(pt, gt, core, f"grads_pair_add_{n}_{i}") for pt, gt, (n, i) in zip(parts0, got0, first)]
    by_chip.update(zip(first, _chip_exchange(sums0, "grads_first_chip_exchange")))
    _, small = _collect_grads([grads0, grads1], d_final)
    small_shapes = [small[n].shape for n in SMALL]
    all_small = _all_gather([_pack([small[n] for n in SMALL])], pltpu.VMEM, "gather_small_grads")[0]
    g = dict(zip(SMALL, _unpack(_sum_slots(all_small, "small_grads_sum"), small_shapes)))
    for n, axis in SMALL_SHARDED.items():
        g[n] = lax.dynamic_slice_in_dim(g[n], me * w[n].shape[axis], w[n].shape[axis], axis=axis)

    delta, new_m, new_v = {}, {}, {}
    for n in BIG:
        outs = _adamw_summed(view(w[n], n), view(m[n], n), view(v[n], n), [by_chip[n, i] for i in range(w[n].shape[0])],
                             f"adamw_{n}")
        g[n], delta[n], new_m[n], new_v[n] = (view(o, n) for o in outs)
    packed = [_pack([d[n] for n in SMALL]) for d in (w, g, m, v)]
    shapes = [w[n].shape for n in SMALL]
    for d, buf in zip((delta, new_m, new_v), _adamw(*packed, "adamw_small")):
        d.update(zip(SMALL, _unpack(buf, shapes)))
    return (loss, dx[None], *[g[n] for n in WEIGHTS], *[delta[n] for n in WEIGHTS], *[new_m[n] for n in WEIGHTS],
            *[new_v[n] for n in WEIGHTS])
```

```python
import functools
import math

import numpy as np
import jax
import jax.numpy as jnp
from jax import lax
from jax.experimental import pallas as pl
from jax.experimental.pallas import tpu as pltpu

F32, BF16 = jnp.float32, jnp.bfloat16
HI = lax.Precision.HIGHEST
SDS = jax.ShapeDtypeStruct

N_DEV = 8
D = 1024
D_FF = 2816
FF_SHARD = 2 * D_FF // N_DEV
PLE_DIM = 256
EPS = 1e-6
CONV_W = 31
SSM_CONV = 4
SSM_HEADS = 16
SSM_XBC = 1536
CHUNK = 128
HYB_IN = 4624
HYB_PAD = 5120
DT_COL = 4608
N_PAIR = 8
ROPE_THETA = 10000.0
LANE = 128
VMEM_LIMIT = 56 * 1024 * 1024

ADAM_LR, ADAM_B1, ADAM_B2, ADAM_EPS, ADAM_WD, ADAM_STEP = 0.001, 0.9, 0.999, 1e-08, 0.01, 10


def _params(sem):
    return pltpu.CompilerParams(dimension_semantics=sem, vmem_limit_bytes=VMEM_LIMIT)


_CHAIN = []


def _restart_chain():
    _CHAIN.clear()


def _pallas(body, *, in_specs, **kw):
    def run(*args):
        n, dep = len(args), list(_CHAIN)

        def chained(*refs):
            return body(*refs[:n], *refs[n + len(dep):])

        outs = pl.pallas_call(chained, in_specs=list(in_specs) + [pl.BlockSpec(memory_space=pl.ANY)] * len(dep), **kw)(*args, *dep)
        _CHAIN[:] = [outs[-1] if isinstance(outs, (list, tuple)) else outs]
        return outs

    return run


def _mm(a, b, *, ta=False, tb=False, reduce_j=False, out_dtypes=(F32,), tm=1024, tn=1024, tk=1024,
        epi=None, extras=(), rows=(), deps=(), sums=0, name):
    ja, jb = a.shape[0], b.shape[0]
    nj = max(ja, jb)
    jo = 1 if reduce_j else nj
    m, k = (a.shape[2], a.shape[1]) if ta else (a.shape[1], a.shape[2])
    n = b.shape[1] if tb else b.shape[2]
    assert (b.shape[2] if tb else b.shape[1]) == k and ja in (1, nj) and jb in (1, nj)
    tm, tn, tk = min(tm, m), min(tn, n), min(tk, k)
    assert m % tm == 0 and n % tn == 0 and k % tk == 0, (name, m, n, k, tm, tn, tk)
    assert not sums or (tn == n and (reduce_j or nj == 1))
    nk = k // tk
    steps = nk * (nj if reduce_j else 1)
    ne, nr, no = len(extras), len(rows), len(out_dtypes)

    def a_map(i, c, j, kk):
        return (j if ja > 1 else 0, kk, i) if ta else (j if ja > 1 else 0, i, kk)

    def b_map(i, c, j, kk):
        return (j if jb > 1 else 0, c, kk) if tb else (j if jb > 1 else 0, kk, c)

    def o_map(i, c, j, kk):
        return (0 if reduce_j else j, i, c)

    dims = (((0 if ta else 1,), (1 if tb else 0,)), ((), ()))

    def body(a_ref, b_ref, *rest):
        ex, rw = rest[:ne], rest[ne:ne + nr]
        outs = rest[ne + nr + len(deps):ne + nr + len(deps) + no]
        sum_refs = rest[ne + nr + len(deps) + no:ne + nr + len(deps) + no + sums]
        first_tile = pl.program_id(0) == 0

        def product():
            return lax.dot_general(a_ref[...], b_ref[...], dims, preferred_element_type=F32)

        def finish(acc):
            res = epi(acc, *[e[...] for e in ex], *[r[...] for r in rw]) if epi else (acc,)
            for o, r in zip(outs, res):
                o[...] = r.astype(o.dtype)
            for s_ref, r in zip(sum_refs, res[no:]):
                @pl.when(first_tile)
                def _(s_ref=s_ref, r=r):
                    s_ref[...] = r

                @pl.when(jnp.logical_not(first_tile))
                def _(s_ref=s_ref, r=r):
                    s_ref[...] += r

        if steps == 1:
            finish(product())
            return
        acc_ref = rest[-1]
        kk = pl.program_id(3)
        step = pl.program_id(2) * nk + kk if reduce_j else kk

        @pl.when(step == 0)
        def _():
            acc_ref[...] = product()

        @pl.when(jnp.logical_and(step > 0, step < steps - 1))
        def _():
            acc_ref[...] += product()

        @pl.when(step == steps - 1)
        def _():
            finish(acc_ref[...] + product())

    o_spec = pl.BlockSpec((None, tm, tn), o_map)
    row_spec = pl.BlockSpec((1, tn), lambda i, c, j, kk: (0, c))
    return _pallas(
        body, name=name, grid=(m // tm, n // tn, nj, nk),
        in_specs=[pl.BlockSpec((None, tk, tm) if ta else (None, tm, tk), a_map),
                  pl.BlockSpec((None, tn, tk) if tb else (None, tk, tn), b_map)]
        + [o_spec] * ne + [row_spec] * nr + [ANY_SPEC] * len(deps),
        out_specs=[o_spec] * no + [row_spec] * sums,
        out_shape=[SDS((jo, m, n), dt) for dt in out_dtypes] + [SDS((1, n), F32)] * sums,
        scratch_shapes=[pltpu.VMEM((tm, tn), F32)] if steps > 1 else [],
        compiler_params=_params(("arbitrary" if sums else "parallel", "parallel", "arbitrary", "arbitrary")),
    )(a, b, *extras, *rows, *deps)


def _whole(p):
    return pl.BlockSpec(p.shape, lambda *_: (0,) * p.ndim)


ANY_SPEC = pl.BlockSpec(memory_space=pl.ANY)


def _rowop(fn, tiles, params, outs, *, grid, name, deps=()):
    nin = len(tiles) + len(params)

    def body(*refs):
        res = fn(*[r[...].astype(F32) for r in refs[:nin]])
        for r, o in zip(refs[nin + len(deps):], res):
            r[...] = o.astype(r.dtype)

    return _pallas(
        body, name=name, grid=grid,
        in_specs=[s for _, s in tiles] + [_whole(p) for p in params] + [ANY_SPEC] * len(deps),
        out_specs=[s for _, _, s in outs], out_shape=[SDS(sh, dt) for sh, dt, _ in outs],
        compiler_params=_params(("parallel",) * len(grid)),
    )(*[t for t, _ in tiles], *params, *deps)


def _rowop_bwd(fn, tiles, params, cots, wrt, gouts, *, grid, name, adds=(), deps=()):
    nt, npar, nc, na = len(tiles), len(params), len(cots), len(adds)
    nin = nt + npar
    flat = [i for grp in wrt for i in grp]
    n_gout = sum(len(dts) for _, dts, _ in gouts)

    def body(*refs):
        vals = [r[...].astype(F32) for r in refs[:nin]]
        cvals = [r[...].astype(F32) for r in refs[nin:nin + nc]]
        avals = [r[...].astype(F32) for r in refs[nin + nc:nin + nc + na]]
        orefs = refs[nin + nc + na + len(deps):]
        diff_idx = flat + list(range(nt, nin))

        def f(*dv):
            full = list(vals)
            for i, v in zip(diff_idx, dv):
                full[i] = v
            return fn(*full)

        _, vjp = jax.vjp(f, *[vals[i] for i in diff_idx])
        grads = vjp(tuple(cvals))
        tile_g, par_g = list(grads[:len(flat)]), grads[len(flat):]
        group_g, at = [], 0
        for grp in wrt:
            members = tile_g[at:at + len(grp)]
            at += len(grp)
            group_g.append(members[0] if len(grp) == 1 else jnp.stack(members, axis=0))
        for av in avals:
            group_g[0] = group_g[0] + av
        o = 0
        for g, (_, dts, _) in zip(group_g, gouts):
            for _ in dts:
                orefs[o][...] = g.astype(orefs[o].dtype)
                o += 1
        first = functools.reduce(jnp.logical_and, [pl.program_id(ax) == 0 for ax in range(len(grid))])
        for r, g in zip(orefs[n_gout:], par_g):
            @pl.when(first)
            def _(r=r, g=g):
                r[...] = g

            @pl.when(jnp.logical_not(first))
            def _(r=r, g=g):
                r[...] += g

    out_specs, out_shape = [], []
    for sh, dts, spec in gouts:
        for dt in dts:
            out_specs.append(spec)
            out_shape.append(SDS(sh, dt))
    for p in params:
        out_specs.append(_whole(p))
        out_shape.append(SDS(p.shape, F32))
    return _pallas(
        body, name=name, grid=grid,
        in_specs=[s for _, s in tiles] + [_whole(p) for p in params] + [s for _, s in cots] + [s for _, s in adds]
        + [ANY_SPEC] * len(deps),
        out_specs=out_specs, out_shape=out_shape,
        compiler_params=_params(("arbitrary",) * len(grid)),
    )(*[t for t, _ in tiles], *params, *[c for c, _ in cots], *[a for a, _ in adds], *deps)


def _tok(c, tm, col=0):
    return pl.BlockSpec((tm, c), lambda i, col=col: (i, col))


def _rms_fn(h, g):
    return (h * lax.rsqrt(jnp.mean(h * h, axis=-1, keepdims=True) + EPS) * g,)


def _lnswish_fn(u, g, b):
    mu = jnp.mean(u, axis=-1, keepdims=True)
    xc = u - mu
    y = xc * lax.rsqrt(jnp.mean(xc * xc, axis=-1, keepdims=True) + EPS) * g + b
    return (y * jax.nn.sigmoid(y),)


def _ple_fn(z, e):
    return (jax.nn.sigmoid(z) * e,)


def _rms(h, g, name, tm=512, deps=()):
    t = h.shape[0]
    return _rowop(_rms_fn, [(h, _tok(D, tm))], [g], [((t, D), BF16, _tok(D, tm))], grid=(t // tm,), name=name, deps=deps)[0]


def _drms_epi(dn, h, dres, g):
    _, vjp = jax.vjp(_rms_fn, h, g)
    dh, dg = vjp((dn,))
    dh = dh + dres
    return dh, dh, dg


def _mm_drms(a, b, h, g, dres, name, tk, tb=True):
    dh, dhb, dg = _mm(a, b, tb=tb, reduce_j=a.shape[0] > 1, tm=512, tk=tk, epi=_drms_epi, extras=(h[None], dres[None]),
                      rows=(g,), out_dtypes=(F32, BF16), sums=1, name=name)
    return dh[0], dhb[0], dg


def _conv_geometry(width):
    pad = 32 if width > 8 else 8
    return pad, pad - (width - 1)


def _fill_shifts(xpad_ref, sh_ref, t, shifts):
    for r in shifts:
        sh_ref[r, :, :] = xpad_ref[pl.ds(r, t + 32), :]


def _dwconv(xs, w, b, *, width, glu, silu, cb, name):
    t = xs[0][0].shape[0]
    c = w.shape[1]
    pad, off = _conv_geometry(width)
    shifts = sorted({(k + off) % 8 for k in range(width)})
    ch = 32

    def body(*refs):
        x_refs, (w_ref, b_ref, o_ref, xpad_ref, sh_ref) = refs[:len(xs)], refs[len(xs):]
        u = x_refs[0][...] * jax.nn.sigmoid(x_refs[1][...]) if glu else x_refs[0][...]
        xpad_ref[pl.ds(0, pad), :] = jnp.zeros((pad, cb), F32)
        xpad_ref[pl.ds(pad, t), :] = u
        xpad_ref[pl.ds(pad + t, 40 - pad), :] = jnp.zeros((40 - pad, cb), F32)
        _fill_shifts(xpad_ref, sh_ref, t, shifts)

        def chunk(i, carry):
            t0 = pl.multiple_of(i * ch, ch)
            acc = jnp.broadcast_to(b_ref[...], (ch, cb))
            for k in range(width):
                q, r = divmod(k + off, 8)
                acc = acc + w_ref[pl.ds(k, 1), :] * sh_ref[r, pl.ds(t0 + 8 * q, ch), :]
            o_ref[pl.ds(t0, ch), :] = acc * jax.nn.sigmoid(acc) if silu else acc
            return carry

        lax.fori_loop(0, t // ch, chunk, 0)

    return _pallas(
        body, name=name, grid=(c // cb,),
        in_specs=[pl.BlockSpec((t, cb), lambda i, o=o: (0, o + i)) for _, o in xs]
        + [pl.BlockSpec((width, cb), lambda i: (0, i)), pl.BlockSpec((1, cb), lambda i: (0, i))],
        out_specs=pl.BlockSpec((t, cb), lambda i: (0, i)), out_shape=SDS((t, c), F32),
        scratch_shapes=[pltpu.VMEM((t + 40, cb), F32), pltpu.VMEM((8, t + 32, cb), F32)],
        compiler_params=_params(("parallel",)),
    )(*[x for x, _ in xs], w, b)


def _dwconv_bwd(xs, w, b, dy, *, width, glu, silu, cb, name):
    t = xs[0][0].shape[0]
    c = w.shape[1]
    pad, off = _conv_geometry(width)
    shifts = sorted({(k + off) % 8 for k in range(width)})
    shifts_t = sorted({mm % 8 for mm in range(width)})
    ch = 32
    nx = len(xs)

    def body(*refs):
        x_refs = refs[:nx]
        w_ref, b_ref, dy_ref = refs[nx:nx + 3]
        dx_refs = refs[nx + 3:nx + 3 + nx]
        dw_ref, db_ref, xpad_ref, sh_ref, dc_ref = refs[nx + 3 + nx:]
        u = x_refs[0][...] * jax.nn.sigmoid(x_refs[1][...]) if glu else x_refs[0][...]
        xpad_ref[pl.ds(0, pad), :] = jnp.zeros((pad, cb), F32)
        xpad_ref[pl.ds(pad, t), :] = u
        xpad_ref[pl.ds(pad + t, 40 - pad), :] = jnp.zeros((40 - pad, cb), F32)
        _fill_shifts(xpad_ref, sh_ref, t, shifts)

        if silu:
            def act_chunk(i, carry):
                t0 = pl.multiple_of(i * ch, ch)
                acc = jnp.broadcast_to(b_ref[...], (ch, cb))
                for k in range(width):
                    q, r = divmod(k + off, 8)
                    acc = acc + w_ref[pl.ds(k, 1), :] * sh_ref[r, pl.ds(t0 + 8 * q, ch), :]
                sg = jax.nn.sigmoid(acc)
                dc_ref[pl.ds(t0, ch), :] = dy_ref[pl.ds(t0, ch), :] * (sg * (1.0 + acc * (1.0 - sg)))
                return carry

            lax.fori_loop(0, t // ch, act_chunk, 0)
        else:
            dc_ref[...] = dy_ref[...]

        def dw_chunk(i, accs):
            t0 = pl.multiple_of(i * 8, 8)
            d = dc_ref[pl.ds(t0, 8), :]
            new = []
            for k in range(width):
                q, r = divmod(k + off, 8)
                new.append(accs[k] + d * sh_ref[r, pl.ds(t0 + 8 * q, 8), :])
            new.append(accs[width] + d)
            return tuple(new)

        accs = lax.fori_loop(0, t // 8, dw_chunk, tuple(jnp.zeros((8, cb), F32) for _ in range(width + 1)))
        for k in range(width):
            dw_ref[pl.ds(k, 1), :] = jnp.sum(accs[k], axis=0, keepdims=True)
        db_ref[...] = jnp.sum(accs[width], axis=0, keepdims=True)

        xpad_ref[pl.ds(0, t), :] = dc_ref[...]
        xpad_ref[pl.ds(t, 40), :] = jnp.zeros((40, cb), F32)
        _fill_shifts(xpad_ref, sh_ref, t, shifts_t)

        def dx_chunk(i, carry):
            t0 = pl.multiple_of(i * ch, ch)
            acc = jnp.zeros((ch, cb), F32)
            for mm in range(width):
                q, r = divmod(mm, 8)
                acc = acc + w_ref[pl.ds(width - 1 - mm, 1), :] * sh_ref[r, pl.ds(t0 + 8 * q, ch), :]
            if glu:
                val, gate = x_refs[0][pl.ds(t0, ch), :], x_refs[1][pl.ds(t0, ch), :]
                sg = jax.nn.sigmoid(gate)
                dx_refs[0][pl.ds(t0, ch), :] = (acc * sg).astype(BF16)
                dx_refs[1][pl.ds(t0, ch), :] = (acc * val * sg * (1.0 - sg)).astype(BF16)
            else:
                dx_refs[0][pl.ds(t0, ch), :] = acc.astype(BF16)
            return carry

        lax.fori_loop(0, t // ch, dx_chunk, 0)

    col = pl.BlockSpec((t, cb), lambda i: (0, i))
    return _pallas(
        body, name=name, grid=(c // cb,),
        in_specs=[pl.BlockSpec((t, cb), lambda i, o=o: (0, o + i)) for _, o in xs]
        + [pl.BlockSpec((width, cb), lambda i: (0, i)), pl.BlockSpec((1, cb), lambda i: (0, i)), col],
        out_specs=[col] * nx + [pl.BlockSpec((width, cb), lambda i: (0, i)), pl.BlockSpec((1, cb), lambda i: (0, i))],
        out_shape=[SDS((t, c), BF16)] * nx + [SDS((width, c), F32), SDS((1, c), F32)],
        scratch_shapes=[pltpu.VMEM((t + 40, cb), F32), pltpu.VMEM((8, t + 32, cb), F32), pltpu.VMEM((t, cb), F32)],
        compiler_params=_params(("parallel",)),
    )(*[x for x, _ in xs], w, b, dy)


_DIMS = {"nn": (((1,), (0,)), ((), ())), "nt": (((1,), (1,)), ((), ())), "tn": (((0,), (0,)), ((), ()))}


def _raw_dot(a, b, mode):
    return lax.dot_general(a.astype(BF16), b.astype(BF16), _DIMS[mode], preferred_element_type=F32)


@functools.partial(jax.custom_vjp, nondiff_argnums=(2,))
def _bdot(a, b, mode):
    return _raw_dot(a, b, mode)


def _bdot_fwd(a, b, mode):
    return _raw_dot(a, b, mode), (a, b)


def _bdot_bwd(mode, res, g):
    a, b = res
    if mode == "nn":
        return _raw_dot(g, b, "nt"), _raw_dot(a, g, "tn")
    if mode == "nt":
        return _raw_dot(g, b, "nn"), _raw_dot(g, a, "tn")
    return _raw_dot(b, g, "nt"), _raw_dot(a, g, "nn")


_bdot.defvjp(_bdot_fwd, _bdot_bwd)


def _iota(shape, axis):
    return lax.broadcasted_iota(jnp.int32, shape, axis)


def _half_masks():
    left = (_iota((1, LANE), 1) < 64).astype(F32)
    return left, 1.0 - left


def _ssd_chunk(state, xa, dtr, z, dtb, alog, dsk, ng):
    xs, bm, cm = xa[:, :D], xa[:, D:D + 256], xa[:, D + 256:]
    left, right = _half_masks()
    expand = (_iota((LANE, D), 1) // 64 == _iota((LANE, D), 0)).astype(F32)
    li, si = _iota((CHUNK, CHUNK), 0), _iota((CHUNK, CHUNK), 1)
    tril = li >= si
    dt16 = jax.nn.softplus(dtr + dtb)
    adt = dt16 * (-jnp.exp(alog))
    dtf = jnp.dot(dt16, expand, precision=HI)
    cs16 = jnp.dot(tril.astype(F32), adt, precision=HI)
    csf = jnp.dot(cs16, expand, precision=HI)
    totf = jnp.sum(jnp.dot(adt, expand, precision=HI), axis=0, keepdims=True)
    cst = cs16.T
    xdt = xs * dtf
    ys, new_state = [], []
    for g in range(2):
        bg, cg = bm[:, LANE * g:LANE * (g + 1)], cm[:, LANE * g:LANE * (g + 1)]
        cb = _bdot(cg, bg, "nt")
        for q in range(4):
            pr = 4 * g + q
            decay = []
            for h in (2 * pr, 2 * pr + 1):
                col = jnp.sum(jnp.where(si == h, cs16, 0.0), axis=1, keepdims=True)
                row = jnp.sum(jnp.where(li == h, cst, 0.0), axis=0, keepdims=True)
                decay.append(cb * jnp.exp(jnp.where(tril, col - row, -jnp.inf)))
            xp = xdt[:, LANE * pr:LANE * (pr + 1)]
            y_diag = _bdot(jnp.concatenate(decay, axis=1), jnp.concatenate([xp * left, xp * right], axis=0), "nn")
            csb, tot = csf[:, LANE * pr:LANE * (pr + 1)], totf[:, LANE * pr:LANE * (pr + 1)]
            ys.append(y_diag + _bdot(cg, state[pr], "nn") * jnp.exp(csb))
            new_state.append(state[pr] * jnp.exp(tot) + _bdot(bg, xp * jnp.exp(tot - csb), "tn"))
    y = jnp.concatenate(ys, axis=1)
    y = y + jnp.dot(jnp.broadcast_to(dsk, (CHUNK, LANE)), expand, precision=HI) * xs
    y = y * (z * jax.nn.sigmoid(z))
    halves = []
    for g in range(2):
        yg = y[:, 512 * g:512 * (g + 1)]
        halves.append(yg * lax.rsqrt(jnp.mean(yg * yg, axis=-1, keepdims=True) + EPS))
    return jnp.concatenate(halves, axis=1) * ng, jnp.stack(new_state, axis=0)


def _ssd_specs(t, rev):
    nc = t // CHUNK
    ix = (lambda c: nc - 1 - c) if rev else (lambda c: c)
    return nc, ix


def _ssd_fwd(xa, proj, dtb, alog, dsk, ng, name):
    t = xa.shape[0]
    nc, ix = _ssd_specs(t, False)

    def body(xa_ref, dt_ref, z_ref, dtb_ref, alog_ref, dsk_ref, ng_ref, y_ref, st_ref, carry_ref):
        @pl.when(pl.program_id(0) == 0)
        def _():
            carry_ref[...] = jnp.zeros_like(carry_ref)

        st_ref[...] = carry_ref[...]
        y, new = _ssd_chunk(carry_ref[...], xa_ref[...], dt_ref[...], z_ref[...], dtb_ref[...], alog_ref[...],
                            dsk_ref[...], ng_ref[...])
        y_ref[...] = y.astype(BF16)
        carry_ref[...] = new

    small = [dtb, alog, dsk, ng]
    return _pallas(
        body, name=name, grid=(nc,),
        in_specs=[pl.BlockSpec((CHUNK, SSM_XBC), lambda c: (c, 0)),
                  pl.BlockSpec((CHUNK, LANE), lambda c: (c, DT_COL // LANE)),
                  pl.BlockSpec((CHUNK, D), lambda c: (c, 2))] + [_whole(p) for p in small],
        out_specs=[pl.BlockSpec((CHUNK, D), lambda c: (c, 0)), pl.BlockSpec((None, N_PAIR, LANE, LANE), lambda c: (c, 0, 0, 0))],
        out_shape=[SDS((t, D), BF16), SDS((nc, N_PAIR, LANE, LANE), F32)],
        scratch_shapes=[pltpu.VMEM((N_PAIR, LANE, LANE), F32)],
        compiler_params=_params(("arbitrary",)),
    )(xa, proj, proj, *small)


def _ssd_bwd(xa, proj, states, dy, dtb, alog, dsk, ng, name):
    t = xa.shape[0]
    nc, ix = _ssd_specs(t, True)

    def body(xa_ref, dt_ref, z_ref, st_ref, dy_ref, dtb_ref, alog_ref, dsk_ref, ng_ref,
             dxa_ref, ddt_ref, dz_ref, gdtb_ref, galog_ref, gdsk_ref, gng_ref, carry_ref):
        first = pl.program_id(0) == 0

        @pl.when(first)
        def _():
            carry_ref[...] = jnp.zeros_like(carry_ref)

        args = (st_ref[...], xa_ref[...], dt_ref[...], z_ref[...], dtb_ref[...], alog_ref[...], dsk_ref[...], ng_ref[...])
        _, vjp = jax.vjp(_ssd_chunk, *args)
        ds, dxa, ddt, dz, gdtb, galog, gdsk, gng = vjp((dy_ref[...], carry_ref[...]))
        carry_ref[...] = ds
        dxa_ref[...] = dxa
        ddt_ref[...] = ddt.astype(BF16)
        dz_ref[...] = dz.astype(BF16)
        for r, g in ((gdtb_ref, gdtb), (galog_ref, galog), (gdsk_ref, gdsk), (gng_ref, gng)):
            @pl.when(first)
            def _(r=r, g=g):
                r[...] = g

            @pl.when(jnp.logical_not(first))
            def _(r=r, g=g):
                r[...] += g

    small = [dtb, alog, dsk, ng]
    return _pallas(
        body, name=name, grid=(nc,),
        in_specs=[pl.BlockSpec((CHUNK, SSM_XBC), lambda c: (ix(c), 0)),
                  pl.BlockSpec((CHUNK, LANE), lambda c: (ix(c), DT_COL // LANE)),
                  pl.BlockSpec((CHUNK, D), lambda c: (ix(c), 2)),
                  pl.BlockSpec((None, N_PAIR, LANE, LANE), lambda c: (ix(c), 0, 0, 0)),
                  pl.BlockSpec((CHUNK, D), lambda c: (ix(c), 0))] + [_whole(p) for p in small],
        out_specs=[pl.BlockSpec((CHUNK, SSM_XBC), lambda c: (ix(c), 0)), pl.BlockSpec((CHUNK, LANE), lambda c: (ix(c), 0)),
                   pl.BlockSpec((CHUNK, D), lambda c: (ix(c), 0))] + [_whole(p) for p in small],
        out_shape=[SDS((t, SSM_XBC), F32), SDS((t, LANE), BF16), SDS((t, D), BF16)] + [SDS(p.shape, F32) for p in small],
        scratch_shapes=[pltpu.VMEM((N_PAIR, LANE, LANE), F32)],
        compiler_params=_params(("arbitrary",)),
    )(xa, proj, proj, states, dy, *small)


def _attn_block(q, kv_prev, kv_cur, cq, sq, ck, sk, sinks, rot, first_block):
    left, right = _half_masks()
    k2 = jnp.concatenate([kv_prev[:, :256], kv_cur[:, :256]], axis=0)
    v2 = jnp.concatenate([kv_prev[:, 256:], kv_cur[:, 256:]], axis=0)
    ri, ci = _iota((LANE, LANE), 0), _iota((LANE, LANE), 1)
    dup = [((ri < 64) & (ci % 64 == ri)).astype(BF16), ((ri >= 64) & (ci % 64 == ri - 64)).astype(BF16)]

    def rope(tt, c, s):
        return tt * c + jnp.dot(tt, rot, precision=HI) * s

    kd, vd = [], []
    for j in range(4):
        sl = slice(LANE * (j // 2), LANE * (j // 2 + 1))
        kd.append(_bdot(rope(k2[:, sl], ck, sk), dup[j % 2], "nn"))
        vd.append(_bdot(v2[:, sl], dup[j % 2], "nn"))
    qi, si = _iota((2 * CHUNK, 2 * CHUNK), 0) % CHUNK, _iota((2 * CHUNK, 2 * CHUNK), 1)
    valid = (si > qi) & (si <= qi + CHUNK) & jnp.logical_or(si >= CHUNK, jnp.logical_not(first_block))
    upper = _iota((2 * CHUNK, 1), 0) < CHUNK
    lanes = _iota((1, LANE), 1)
    outs = []
    for pr in range(N_PAIR):
        qr = rope(q[:, LANE * pr:LANE * (pr + 1)], cq, sq)
        lg = _bdot(jnp.concatenate([qr * left, qr * right], axis=0), kd[pr // 2], "nt") * 0.125
        lg = jnp.where(valid, lg, -jnp.inf)
        s1 = jnp.sum(jnp.where(lanes == 2 * pr, sinks, 0.0), axis=1, keepdims=True)
        s2 = jnp.sum(jnp.where(lanes == 2 * pr + 1, sinks, 0.0), axis=1, keepdims=True)
        sink = jnp.where(upper, s1, s2)
        mx = lax.stop_gradient(jnp.maximum(jnp.max(lg, axis=-1, keepdims=True), sink))
        e = jnp.exp(lg - mx)
        probs = e / (jnp.sum(e, axis=-1, keepdims=True) + jnp.exp(sink - mx))
        o2 = _bdot(probs, vd[pr // 2], "nn")
        outs.append(o2[:CHUNK] * left + o2[CHUNK:] * right)
    return jnp.concatenate(outs, axis=1)


def _attn_fwd(qkv, cos, sin, sinks, rot, name):
    t = qkv.shape[0]
    nb = t // CHUNK

    def body(q_ref, kvp_ref, kvc_ref, cq_ref, sq_ref, cp_ref, sp_ref, sinks_ref, rot_ref, o_ref):
        ck = jnp.concatenate([cp_ref[...], cq_ref[...]], axis=0)
        sk = jnp.concatenate([sp_ref[...], sq_ref[...]], axis=0)
        o_ref[...] = _attn_block(q_ref[...], kvp_ref[...], kvc_ref[...], cq_ref[...], sq_ref[...], ck, sk,
                                 sinks_ref[...], rot_ref[...], pl.program_id(0) == 0).astype(BF16)

    prev = lambda n: jnp.maximum(n - 1, 0)
    return _pallas(
        body, name=name, grid=(nb,),
        in_specs=[pl.BlockSpec((CHUNK, D), lambda n: (n, 0)),
                  pl.BlockSpec((CHUNK, 512), lambda n: (prev(n), 2)), pl.BlockSpec((CHUNK, 512), lambda n: (n, 2)),
                  pl.BlockSpec((CHUNK, LANE), lambda n: (n, 0)), pl.BlockSpec((CHUNK, LANE), lambda n: (n, 0)),
                  pl.BlockSpec((CHUNK, LANE), lambda n: (prev(n), 0)), pl.BlockSpec((CHUNK, LANE), lambda n: (prev(n), 0)),
                  _whole(sinks), _whole(rot)],
        out_specs=pl.BlockSpec((CHUNK, D), lambda n: (n, 0)), out_shape=SDS((t, D), BF16),
        compiler_params=_params(("parallel",)),
    )(qkv, qkv, qkv, cos, sin, cos, sin, sinks, rot)


def _attn_bwd(qkv, do, cos, sin, sinks, rot, name):
    t = qkv.shape[0]
    nb = t // CHUNK

    def body(q_ref, kvp_ref, kvc_ref, do_ref, cq_ref, sq_ref, cp_ref, sp_ref, sinks_ref, rot_ref,
             dq_ref, dkv_ref, dbq_ref, dbkv_ref, dsink_ref, carry_ref):
        n = pl.program_id(0)

        @pl.when(n == 0)
        def _():
            carry_ref[...] = jnp.zeros_like(carry_ref)
            dbq_ref[...] = jnp.zeros_like(dbq_ref)
            dbkv_ref[...] = jnp.zeros_like(dbkv_ref)
            dsink_ref[...] = jnp.zeros_like(dsink_ref)

        @pl.when(n < nb)
        def _():
            ck = jnp.concatenate([cp_ref[...], cq_ref[...]], axis=0)
            sk = jnp.concatenate([sp_ref[...], sq_ref[...]], axis=0)
            f = lambda q, kvp, kvc, s: _attn_block(q, kvp, kvc, cq_ref[...], sq_ref[...], ck, sk, s, rot_ref[...], n == 0)
            _, vjp = jax.vjp(f, q_ref[...], kvp_ref[...], kvc_ref[...], sinks_ref[...])
            dq, dkvp, dkvc, ds = vjp(do_ref[...].astype(F32))
            done = carry_ref[...] + dkvp
            dq_ref[...] = dq.astype(BF16)
            dkv_ref[...] = done.astype(BF16)
            dbq_ref[...] += jnp.sum(dq, axis=0, keepdims=True)
            dsink_ref[...] += ds
            carry_ref[...] = dkvc

            @pl.when(n > 0)
            def _():
                dbkv_ref[...] += jnp.sum(done, axis=0, keepdims=True)

        @pl.when(n == nb)
        def _():
            done = carry_ref[...]
            dkv_ref[...] = done.astype(BF16)
            dbkv_ref[...] += jnp.sum(done, axis=0, keepdims=True)

    cur = lambda n: jnp.minimum(n, nb - 1)
    prev = lambda n: jnp.maximum(jnp.minimum(n, nb - 1) - 1, 0)
    fin = lambda n: jnp.maximum(n - 1, 0)
    outs = _pallas(
        body, name=name, grid=(nb + 1,),
        in_specs=[pl.BlockSpec((CHUNK, D), lambda n: (cur(n), 0)),
                  pl.BlockSpec((CHUNK, 512), lambda n: (prev(n), 2)), pl.BlockSpec((CHUNK, 512), lambda n: (cur(n), 2)),
                  pl.BlockSpec((CHUNK, D), lambda n: (cur(n), 0)),
                  pl.BlockSpec((CHUNK, LANE), lambda n: (cur(n), 0)), pl.BlockSpec((CHUNK, LANE), lambda n: (cur(n), 0)),
                  pl.BlockSpec((CHUNK, LANE), lambda n: (prev(n), 0)), pl.BlockSpec((CHUNK, LANE), lambda n: (prev(n), 0)),
                  _whole(sinks), _whole(rot)],
        out_specs=[pl.BlockSpec((CHUNK, D), lambda n: (cur(n), 0)), pl.BlockSpec((CHUNK, 512), lambda n: (fin(n), 0)),
                   pl.BlockSpec((1, D), lambda n: (0, 0)), pl.BlockSpec((1, 512), lambda n: (0, 0)), _whole(sinks)],
        out_shape=[SDS((t, D), BF16), SDS((t, 512), BF16), SDS((1, D), F32), SDS((1, 512), F32), SDS(sinks.shape, F32)],
        scratch_shapes=[pltpu.VMEM((CHUNK, 512), F32)],
        compiler_params=_params(("arbitrary",)),
    )(qkv, qkv, qkv, do, cos, sin, cos, sin, sinks, rot)
    dq, dkv, dbq, dbkv, dsinks = outs
    return jnp.concatenate([dq, dkv], axis=1), jnp.concatenate([dbq, dbkv], axis=1), dsinks


def _loss_head(h, tgt, g, name, tm=512):
    t = h.shape[0]

    def body(h_ref, t_ref, g_ref, loss_ref, dh_ref, dhb_ref, dg_ref):
        def f(hv, gv):
            err = _rms_fn(hv, gv)[0] - t_ref[...]
            return 0.5 * jnp.sum(jnp.mean(err * err, axis=-1, keepdims=True), axis=0, keepdims=True)

        loss, vjp = jax.vjp(f, h_ref[...], g_ref[...])
        dh, dg = vjp(jnp.ones((1, 1), F32))
        dh_ref[...] = dh
        dhb_ref[...] = dh.astype(BF16)
        first = pl.program_id(0) == 0

        @pl.when(first)
        def _():
            loss_ref[...] = loss
            dg_ref[...] = dg

        @pl.when(jnp.logical_not(first))
        def _():
            loss_ref[...] += loss
            dg_ref[...] += dg

    return _pallas(
        body, name=name, grid=(t // tm,),
        in_specs=[_tok(D, tm), _tok(D, tm), _whole(g)],
        out_specs=[pl.BlockSpec((1, 1), lambda i: (0, 0)), _tok(D, tm), _tok(D, tm), _whole(g)],
        out_shape=[SDS((1, 1), F32), SDS((t, D), F32), SDS((t, D), BF16), SDS(g.shape, F32)],
        compiler_params=_params(("arbitrary",)),
    )(h, tgt, g)


def _res_half(acc, res):
    return (res + 0.5 * acc,)


def _res_full(acc, res):
    return (res + acc,)


def _half(acc):
    return (0.5 * acc,)


def _ffn_in(n, w_in, name, tm=1024):
    t = n.shape[0]
    tm = min(tm, t)

    def body(n_ref, w_ref, pre_ref, act_ref):
        a = n_ref[...]
        gate = lax.dot_general(a, w_ref[0], _DIMS["nt"], preferred_element_type=F32)
        up = lax.dot_general(a, w_ref[1], _DIMS["nt"], preferred_element_type=F32)
        pre_ref[0] = gate.astype(BF16)
        pre_ref[1] = up.astype(BF16)
        act_ref[...] = (gate * jax.nn.sigmoid(gate) * up).astype(BF16)

    pair = pl.BlockSpec((2, None, tm, FF_SHARD), lambda i, j: (0, j, i, 0))
    return _pallas(
        body, name=name, grid=(t // tm, 4),
        in_specs=[pl.BlockSpec((tm, D), lambda i, j: (i, 0)), pl.BlockSpec((2, None, FF_SHARD, D), lambda i, j: (0, j, 0, 0))],
        out_specs=[pair, pl.BlockSpec((None, tm, FF_SHARD), lambda i, j: (j, i, 0))],
        out_shape=[SDS((2, 4, t, FF_SHARD), BF16), SDS((4, t, FF_SHARD), BF16)],
        compiler_params=_params(("parallel", "parallel")),
    )(n, w_in.reshape(2, 4, FF_SHARD, D))


def _ffn_dact(dhb, w_out, pre, name, tm=1024, deps=()):
    t = dhb.shape[0]
    tm = min(tm, t)

    def body(d_ref, w_ref, pre_ref, *rest):
        o_ref = rest[-1]
        dact = 0.5 * lax.dot_general(d_ref[...], w_ref[...], _DIMS["nt"], preferred_element_type=F32)
        gate, up = pre_ref[0].astype(F32), pre_ref[1].astype(F32)
        sg = jax.nn.sigmoid(gate)
        o_ref[0] = (dact * up * (sg * (1.0 + gate * (1.0 - sg)))).astype(BF16)
        o_ref[1] = (dact * (gate * sg)).astype(BF16)

    pair = pl.BlockSpec((2, None, tm, FF_SHARD), lambda i, j: (0, j, i, 0))
    return _pallas(
        body, name=name, grid=(t // tm, 4),
        in_specs=[pl.BlockSpec((tm, D), lambda i, j: (i, 0)), pl.BlockSpec((None, FF_SHARD, D), lambda i, j: (j, 0, 0)), pair]
        + [ANY_SPEC] * len(deps),
        out_specs=pair, out_shape=SDS((2, 4, t, FF_SHARD), BF16),
        compiler_params=_params(("parallel", "parallel")),
    )(dhb, w_out, pre, *deps)


def _ffn_fwd(h, g, w_in, w_out, tag, deps=()):
    n = _rms(h, g, f"{tag}_rms", deps=deps)
    pre, act = _ffn_in(n, w_in, f"{tag}_in")
    out = _mm(act, w_out, reduce_j=True, tk=FF_SHARD, epi=_res_half, extras=(h[None],), name=f"{tag}_out")[0][0]
    return out, (h, n, pre, act)


def _ffn_bwd(dh, dhb, saved, g, w_in, w_out, tag, deps=(), hook=None):
    h, n, pre, act = saved
    t = h.shape[0]
    dpre = _ffn_dact(dhb, w_out, pre, f"{tag}_dact", deps=deps).reshape(N_DEV, t, FF_SHARD)
    dw_out = _mm(act, dhb[None], ta=True, tm=FF_SHARD, epi=_half, out_dtypes=(BF16,), deps=hook(dpre) if hook else (),
                 name=f"{tag}_dwout")[0]
    dh_in, dhb_in, dg = _mm_drms(dpre, w_in, h, g, dh, f"{tag}_dn", FF_SHARD, tb=False)
    dw_in = _mm(dpre, n[None], ta=True, tm=FF_SHARD, out_dtypes=(BF16,), name=f"{tag}_dwin")[0]
    return dh_in, dhb_in, dg, dw_in, dw_out


def _ple_fwd(h, g, pb, w_gate, w_proj, tag):
    t = h.shape[0]
    tm = 512
    n = _rms(h, g, f"{tag}_rms")
    e = _mm(pb[None], w_proj[None], name=f"{tag}_proj")[0][0]
    z = _mm(n[None], w_gate[None], name=f"{tag}_gate")[0][0]
    out = _rowop(lambda zz, ee, hh: (hh + _ple_fn(zz, ee)[0],), [(z, _tok(D, tm)), (e, _tok(D, tm)), (h, _tok(D, tm))], [],
                 [((t, D), F32, _tok(D, tm))], grid=(t // tm,), name=f"{tag}_mix")[0]
    return out, (h, n, e, z)


def _ple_bwd(dh, dhb, saved, g, pb, w_gate, tag, deps=()):
    h, n, e, z = saved
    t = h.shape[0]
    tm = 512
    dz, de = _rowop_bwd(_ple_fn, [(z, _tok(D, tm)), (e, _tok(D, tm))], [], [(dh, _tok(D, tm))], [(0,), (1,)],
                        [((t, D), (BF16,), _tok(D, tm)), ((t, D), (BF16,), _tok(D, tm))], grid=(t // tm,), name=f"{tag}_dmix",
                        deps=deps)
    dw_proj = _mm(pb[None], de[None], ta=True, out_dtypes=(BF16,), name=f"{tag}_dwproj")[0][0]
    dw_gate = _mm(n[None], dz[None], ta=True, out_dtypes=(BF16,), name=f"{tag}_dwgate")[0][0]
    dh_in, dhb_in, dg = _mm_drms(dz[None], w_gate[None], h, g, dh, f"{tag}_dn", 1024)
    return dh_in, dhb_in, dg, dw_gate, dw_proj


def _hyb_fwd(h, w, tag):
    t = h.shape[0]
    tm = 512
    hn = _rms(h, w["norm_mix"], f"{tag}_rms")
    proj = _mm(hn[None], w["hyb_in"][None], tn=512, name=f"{tag}_in")[0][0]
    u1 = _dwconv([(proj, 0), (proj, D // LANE)], w["conv_w"], w["conv_b"], width=CONV_W, glu=True, silu=False, cb=LANE,
                 name=f"{tag}_conv")
    u = _rowop(_lnswish_fn, [(u1, _tok(D, tm))], [w["ln_g"], w["ln_b"]], [((t, D), BF16, _tok(D, tm))], grid=(t // tm,),
               name=f"{tag}_ln")[0]
    xa = _dwconv([(proj, 3 * D // LANE)], w["sconv_w"], w["sconv_b"], width=SSM_CONV, glu=False, silu=True, cb=LANE,
                 name=f"{tag}_sconv")
    y, states = _ssd_fwd(xa, proj, w["dt_bias"], w["a_log"], w["d_skip"], w["ssm_norm"], f"{tag}_ssd")
    mixed = jnp.stack([u, y], axis=0)
    out = _mm(mixed, w["hyb_out"], reduce_j=True, epi=_res_full, extras=(h[None],), name=f"{tag}_out")[0][0]
    return out, (h, hn, proj, u1, xa, states, mixed)


def _hyb_bwd(dh, dhb, saved, w, tag):
    h, hn, proj, u1, xa, states, mixed = saved
    t = h.shape[0]
    tm = 512
    dmix = _mm(dhb[None], w["hyb_out"], tb=True, name=f"{tag}_dmix")[0]
    dw_out = _mm(mixed, dhb[None], ta=True, out_dtypes=(BF16,), name=f"{tag}_dwout")[0]
    du1, dln_g, dln_b = _rowop_bwd(_lnswish_fn, [(u1, _tok(D, tm))], [w["ln_g"], w["ln_b"]], [(dmix[0], _tok(D, tm))], [(0,)],
                                   [((t, D), (F32,), _tok(D, tm))], grid=(t // tm,), name=f"{tag}_dln")
    dval, dgate, dconv_w, dconv_b = _dwconv_bwd([(proj, 0), (proj, D // LANE)], w["conv_w"], w["conv_b"], du1,
                                                width=CONV_W, glu=True, silu=False, cb=LANE, name=f"{tag}_dconv")
    dxa, ddt, dz, g_dtb, g_alog, g_dsk, g_ng = _ssd_bwd(xa, proj, states, dmix[1], w["dt_bias"], w["a_log"], w["d_skip"],
                                                         w["ssm_norm"], f"{tag}_dssd")
    dxbc, dsconv_w, dsconv_b = _dwconv_bwd([(proj, 3 * D // LANE)], w["sconv_w"], w["sconv_b"], dxa, width=SSM_CONV,
                                           glu=False, silu=True, cb=LANE, name=f"{tag}_dsconv")
    dproj = jnp.concatenate([dval, dgate, dz, dxbc, ddt, jnp.zeros((t, HYB_PAD - DT_COL - LANE), BF16)], axis=1)
    dh_in, dhb_in, dg = _mm_drms(dproj[None], w["hyb_in"][None], h, w["norm_mix"], dh, f"{tag}_dhn", 1024)
    dw_in = _mm(hn[None], dproj[None], ta=True, tn=512, out_dtypes=(BF16,), name=f"{tag}_dwin")[0][0]
    grads = dict(norm_mix=dg, hyb_in=dw_in, hyb_out=dw_out, conv_w=dconv_w, conv_b=dconv_b, ln_g=dln_g, ln_b=dln_b,
                 sconv_w=dsconv_w, sconv_b=dsconv_b, dt_bias=g_dtb, a_log=g_alog, d_skip=g_dsk, ssm_norm=g_ng)
    return dh_in, dhb_in, grads


def _bias_epi(acc, row):
    return (acc + row,)


def _res_bias_epi(acc, res, row):
    return (res + acc + row,)


def _att_fwd(h, w, tables, tag):
    cos, sin, rot = tables
    hn = _rms(h, w["norm_mix"], f"{tag}_rms")
    qkv = _mm(hn[None], w["qkv"][None], tb=True, tn=512, epi=_bias_epi, rows=(w["b_qkv"],), name=f"{tag}_qkv")[0][0]
    o = _attn_fwd(qkv, cos, sin, w["sinks"], rot, f"{tag}_core")
    out = _mm(o[None], w["w_o"][None], epi=_res_bias_epi, extras=(h[None],), rows=(w["b_o"],), name=f"{tag}_out")[0][0]
    return out, (h, hn, qkv, o)


def _att_bwd(dh, dhb, saved, w, tables, tag):
    cos, sin, rot = tables
    h, hn, qkv, o = saved
    t = h.shape[0]
    tm = 512
    do = _mm(dhb[None], w["w_o"][None], tb=True, out_dtypes=(BF16,), name=f"{tag}_do")[0][0]
    dw_o = _mm(o[None], dhb[None], ta=True, out_dtypes=(BF16,), name=f"{tag}_dwo")[0][0]
    db_o = _rowop_bwd(lambda xx, bb: (xx + bb,), [(dh, _tok(D, tm))], [w["b_o"]], [(dh, _tok(D, tm))], [], [],
                      grid=(t // tm,), name=f"{tag}_dbo")[0]
    dqkv, db_qkv, dsinks = _attn_bwd(qkv, do, cos, sin, w["sinks"], rot, f"{tag}_dcore")
    dh_in, dhb_in, dg = _mm_drms(dqkv[None], w["qkv"][None], h, w["norm_mix"], dh, f"{tag}_dhn", 512, tb=False)
    dw_qkv = _mm(dqkv[None], hn[None], ta=True, tm=512, out_dtypes=(BF16,), name=f"{tag}_dwqkv")[0][0]
    grads = dict(norm_mix=dg, qkv=dw_qkv, b_qkv=db_qkv, sinks=dsinks, w_o=dw_o, b_o=db_o)
    return dh_in, dhb_in, grads


def _rope_tables(t):
    inv = ROPE_THETA ** (-jnp.arange(0, 64, 2, dtype=F32) / 64)
    ang = jnp.arange(t, dtype=F32)[:, None] * inv[None, :]
    cos, sin = jnp.tile(jnp.cos(ang), (1, 4)), jnp.tile(jnp.sin(ang), (1, 4))
    rot = np.zeros((LANE, LANE), np.float32)
    for j in range(LANE):
        if j % 64 < 32:
            rot[j + 32, j] = -1.0
        else:
            rot[j - 32, j] = 1.0
    return cos, sin, jnp.asarray(rot)


def _local_step(x, p, tgt, layers, final_norm):
    _restart_chain()
    tables = _rope_tables(x.shape[0])
    pb = p.astype(BF16)
    h, saved = x, []
    for i, w in enumerate(layers):
        h, s = _layer_fwd(i, h, w, pb[i], tables)
        saved.append(s)
    loss, dh, dhb, d_final = _loss_head(h, tgt, final_norm, "loss_head")
    grads = [None] * len(layers)
    for i in reversed(range(len(layers))):
        dh, dhb, head = _layer_bwd_head(i, dh, dhb, saved[i], layers[i], pb[i])
        dh, dhb, tail = _layer_bwd_tail(i, dh, dhb, saved[i], layers[i], tables)
        grads[i] = {**head, **tail}
    return loss[0, 0], dh, grads, d_final


def _layer_fwd(i, h, w, pb, tables, deps=()):
    s = {}
    h, s["ffn1"] = _ffn_fwd(h, w["norm_ffn1"], w["ffn1_in"], w["ffn1_out"], f"l{i}_ffn1", deps=deps)
    if i % 2 == 0:
        h, s["mix"] = _hyb_fwd(h, w, f"l{i}_hyb")
    else:
        h, s["mix"] = _att_fwd(h, w, tables, f"l{i}_att")
    h, s["ffn2"] = _ffn_fwd(h, w["norm_ffn2"], w["ffn2_in"], w["ffn2_out"], f"l{i}_ffn2")
    h, s["ple"] = _ple_fwd(h, w["ple_norm"], pb, w["ple_gate"], w["ple_proj"], f"l{i}_ple")
    return h, s


def _layer_bwd_head(i, dh, dhb, s, w, pb, deps=()):
    g = {}
    dh, dhb, g["ple_norm"], g["ple_gate"], g["ple_proj"] = _ple_bwd(dh, dhb, s["ple"], w["ple_norm"], pb, w["ple_gate"],
                                                                    f"l{i}_ple", deps=deps)
    return dh, dhb, g


def _layer_bwd_tail(i, dh, dhb, s, w, tables, deps=()):
    g = {}
    dh, dhb, g["norm_ffn2"], g["ffn2_in"], g["ffn2_out"] = _ffn_bwd(dh, dhb, s["ffn2"], w["norm_ffn2"], w["ffn2_in"],
                                                                    w["ffn2_out"], f"l{i}_ffn2", deps=deps)
    if i % 2 == 0:
        dh, dhb, gm = _hyb_bwd(dh, dhb, s["mix"], w, f"l{i}_hyb")
    else:
        dh, dhb, gm = _att_bwd(dh, dhb, s["mix"], w, tables, f"l{i}_att")
    g.update(gm)
    dh, dhb, g["norm_ffn1"], g["ffn1_in"], g["ffn1_out"] = _ffn_bwd(dh, dhb, s["ffn1"], w["norm_ffn1"], w["ffn1_in"],
                                                                    w["ffn1_out"], f"l{i}_ffn1")
    return dh, dhb, g


def _cols(g):
    full = jnp.moveaxis(g, 0, -2)
    return full.reshape(*full.shape[:-2], N_DEV * g.shape[-1])


def _uncols(full):
    split = full.reshape(*full.shape[:-1], N_DEV, full.shape[-1] // N_DEV)
    return jnp.moveaxis(split, -2, 0)


def _lane_pad(v):
    return jnp.pad(v, ((0, 0), (0, LANE - v.shape[1])))


def _build_layers(gw, gs, rep):
    return [_build_layer(i, gw, gs, rep) for i in range(2)]


def _build_layer(i, gw, gs, rep, parts=("ffn1", "mix", "ffn2", "ple")):
    w = {}
    for f in ("ffn1", "ffn2"):
        if f in parts:
            w[f"norm_{f}"] = rep[f"norm_{f}"][i][None]
            w[f"{f}_in"] = gw[f"{f}_w_in", i]
            w[f"{f}_out"] = gw[f"{f}_w_out", i].reshape(4, FF_SHARD, D)
    if "ple" in parts:
        w["ple_norm"] = rep["ple_norm"][i][None]
        w["ple_gate"] = gw["ple_gate_w", i].reshape(D, D)
        w["ple_proj"] = _cols(gw["ple_proj_w", i])
    if "mix" not in parts:
        return w
    w["norm_mix"] = rep["norm_mix"][i][None]
    if i == 0:
        w["hyb_in"] = jnp.pad(_cols(gw["hyb_w_in", 0]), ((0, 0), (0, HYB_PAD - HYB_IN)))
        w["hyb_out"] = gw["hyb_w_out", 0].reshape(2, D, D)
        w["conv_w"] = _cols(gs["conv_dw_w"][:, 0])
        w["sconv_w"] = _cols(gs["ssm_conv_w"][:, 0])
        w["conv_b"], w["ln_g"], w["ln_b"] = rep["conv_dw_b"], rep["conv_ln_g"], rep["conv_ln_b"]
        w["sconv_b"], w["ssm_norm"] = rep["ssm_conv_b"], rep["ssm_norm"]
        w["dt_bias"], w["a_log"], w["d_skip"] = (_lane_pad(rep[k]) for k in ("ssm_dt_bias", "ssm_a_log", "ssm_d"))
    else:
        w["qkv"] = gw["att_w_qkv", 0].reshape(-1, D)
        w["w_o"] = gw["att_w_o", 0].reshape(D, D)
        w["b_qkv"] = gs["att_b_qkv"][:, 0].reshape(1, -1)
        w["b_o"] = gs["att_b_o"][:, 0].reshape(1, -1)
        w["sinks"] = _lane_pad(rep["att_sinks"])
    return w


def _big_grads(i, g):
    big = {}
    for f in ("ffn1", "ffn2"):
        if f"{f}_in" in g:
            big[f"{f}_w_in", i] = g[f"{f}_in"]
            big[f"{f}_w_out", i] = g[f"{f}_out"].reshape(N_DEV, D_FF // N_DEV, D)
    if "ple_gate" in g:
        big["ple_gate_w", i] = g["ple_gate"].reshape(N_DEV, D // N_DEV, D)
        big["ple_proj_w", i] = _uncols(g["ple_proj"])
    if "hyb_in" in g:
        big["hyb_w_in", 0] = _uncols(g["hyb_in"][:, :HYB_IN])
        big["hyb_w_out", 0] = g["hyb_out"].reshape(N_DEV, 2 * D // N_DEV, D)
    if "qkv" in g:
        big["att_w_qkv", 0] = g["qkv"].reshape(N_DEV, -1, D)
        big["att_w_o", 0] = g["w_o"].reshape(N_DEV, D // N_DEV, D)
    return big


def _collect_grads(grads, d_final):
    g0, g1 = grads
    big, small = {**_big_grads(0, g0), **_big_grads(1, g1)}, {}
    for f in ("ffn1", "ffn2"):
        small[f"norm_{f}"] = jnp.concatenate([g[f"norm_{f}"] for g in grads], axis=0)
    small["norm_mix"] = jnp.concatenate([g["norm_mix"] for g in grads], axis=0)
    small["ple_norm"] = jnp.concatenate([g["ple_norm"] for g in grads], axis=0)
    small["conv_dw_w"] = g0["conv_w"][None]
    small["conv_dw_b"], small["conv_ln_g"], small["conv_ln_b"] = g0["conv_b"], g0["ln_g"], g0["ln_b"]
    small["ssm_conv_w"] = g0["sconv_w"][None]
    small["ssm_conv_b"], small["ssm_norm"] = g0["sconv_b"], g0["ssm_norm"]
    small["ssm_dt_bias"], small["ssm_a_log"], small["ssm_d"] = (g0[k][:, :SSM_HEADS] for k in ("dt_bias", "a_log", "d_skip"))
    small["att_b_qkv"], small["att_b_o"] = g1["b_qkv"], g1["b_o"]
    small["att_sinks"] = g1["sinks"][:, :SSM_HEADS]
    small["final_norm"] = d_final[0]
    return big, small


MESH = pl.DeviceIdType.MESH


def _place():
    return lax.axis_index("x"), lax.axis_index("y"), lax.axis_index("c")


def _all_gather(blocks, space, name):
    nb = len(blocks)

    def body(*refs):
        x_refs, out_refs, (send_sems, recv_sems, local_sem) = refs[:nb], refs[nb:2 * nb], refs[2 * nb:]
        x, y, c = _place()
        me, sibling = (x, y, c), (x, y, 1 - c)
        chips = [(1 - x, y), (x, 1 - y), (1 - x, 1 - y)]

        def copies(k, blk, to, own=False):
            idx = 4 * blk[0] + 2 * blk[1] + blk[2]
            return [pltpu.make_async_remote_copy(src_ref=x_ref if own else out_ref.at[idx], dst_ref=out_ref.at[idx],
                                                 send_sem=send_sems.at[k, b], recv_sem=recv_sems.at[k, b], device_id=to,
                                                 device_id_type=MESH) for b, (x_ref, out_ref) in enumerate(zip(x_refs, out_refs))]

        mine = [pltpu.make_async_copy(x_ref, out_ref.at[4 * x + 2 * y + c], local_sem.at[b])
                for b, (x_ref, out_ref) in enumerate(zip(x_refs, out_refs))]
        first = copies(0, me, sibling, own=True)
        for j, chip in enumerate(chips):
            first += copies(1 + j, me, (*chip, c), own=True)
        for cp in mine + first:
            cp.start()
        passed = []
        for j, chip in enumerate(chips):
            for cp in copies(1 + j, (*chip, c), me):
                cp.wait_recv()
            onward = copies(4 + j, (*chip, c), sibling)
            for cp in onward:
                cp.start()
            passed += onward
        for cp in copies(0, sibling, me):
            cp.wait_recv()
        for j, chip in enumerate(chips):
            for cp in copies(4 + j, (*chip, 1 - c), me):
                cp.wait_recv()
        for cp in first + passed:
            cp.wait_send()
        for cp in mine:
            cp.wait()

    spec = pl.BlockSpec(memory_space=space)
    return _pallas(
        body, name=name, out_shape=[SDS((N_DEV,) + b.shape, b.dtype) for b in blocks],
        in_specs=[spec] * nb, out_specs=[spec] * nb,
        scratch_shapes=[pltpu.SemaphoreType.DMA((7, nb)), pltpu.SemaphoreType.DMA((7, nb)), pltpu.SemaphoreType.DMA((nb,))],
    )(*blocks)


def _pair_exchange(parts, name):
    nb = len(parts)

    def body(*refs):
        p_refs, got_refs, (send_sems, recv_sems) = refs[:nb], refs[nb:2 * nb], refs[2 * nb:]
        x, y, c = _place()
        copies = [pltpu.make_async_remote_copy(src_ref=p_ref.at[2 * q + (1 - c)], dst_ref=got_ref.at[q],
                                               send_sem=send_sems.at[q, b], recv_sem=recv_sems.at[q, b], device_id=(x, y, 1 - c),
                                               device_id_type=MESH)
                  for q in range(4) for b, (p_ref, got_ref) in enumerate(zip(p_refs, got_refs))]
        for cp in copies:
            cp.start()
        for cp in copies:
            cp.wait_recv()
        for cp in copies:
            cp.wait_send()

    hbm = pl.BlockSpec(memory_space=pltpu.HBM)
    return _pallas(
        body, name=name, out_shape=[SDS((4,) + p.shape[1:], p.dtype) for p in parts], in_specs=[hbm] * nb, out_specs=[hbm] * nb,
        scratch_shapes=[pltpu.SemaphoreType.DMA((4, nb)), pltpu.SemaphoreType.DMA((4, nb))],
    )(*parts)


HBM_SPEC = pl.BlockSpec(memory_space=pltpu.HBM)
SEM_SPEC = pl.BlockSpec(memory_space=pltpu.SEMAPHORE)
EFFECT = pltpu.SideEffectType.DATAFLOW_SIDE_EFFECTING


def _plan_descriptors(plan, srcs, lands, send_sems, recv_sems, local_sems, arriving):
    remote, local = plan(*_place())
    pick = lambda ref, slot: ref if slot is None else ref.at[slot]
    rem = [pltpu.make_async_remote_copy(src_ref=pick(srcs[si], ss), dst_ref=lands[li].at[rs if arriving else ds],
                                        send_sem=send_sems.at[k], recv_sem=recv_sems.at[k], device_id=dev, device_id_type=MESH)
           for k, (si, ss, li, ds, dev, rs) in enumerate(remote)]
    loc = [pltpu.make_async_copy(pick(srcs[si], ss), lands[li].at[ds], local_sems.at[k])
           for k, (si, ss, li, ds) in enumerate(local)]
    return rem, loc


def _plan_counts(plan):
    remote, local = plan(0, 0, 0)
    return len(remote), max(len(local), 1)


def _exchange_start(srcs, land_shapes, plan, name):
    ns, nl = len(srcs), len(land_shapes)
    n_remote, n_local = _plan_counts(plan)
    lands = [pltpu.with_memory_space_constraint(lax.empty(s.shape, s.dtype), pltpu.HBM) for s in land_shapes]
    srcs = [pltpu.with_memory_space_constraint(s, pltpu.HBM) for s in srcs]

    def body(*refs):
        src_refs, land_refs = refs[:ns], refs[ns:ns + nl]
        send_sems, recv_sems, local_sems = refs[ns + nl:ns + nl + 3]
        token = refs[-1]
        rem, loc = _plan_descriptors(plan, src_refs, land_refs, send_sems, recv_sems, local_sems, arriving=False)
        for cp in loc + rem:
            cp.start()
        token[...] = jnp.zeros_like(token)

    outs = _pallas(
        body, name=name,
        out_shape=[pltpu.SemaphoreType.DMA((n_remote,)), pltpu.SemaphoreType.DMA((n_remote,)), pltpu.SemaphoreType.DMA((n_local,))]
        + [pltpu.HBM(a.shape, a.dtype) for a in srcs + lands] + [SDS((8, LANE), F32)],
        in_specs=[HBM_SPEC] * (ns + nl),
        out_specs=[SEM_SPEC] * 3 + [HBM_SPEC] * (ns + nl) + [pl.BlockSpec(memory_space=pltpu.VMEM)],
        input_output_aliases={i: 3 + i for i in range(ns + nl)},
        compiler_params=pltpu.CompilerParams(has_side_effects=EFFECT),
    )(*srcs, *lands)
    return (outs[:3], outs[3:3 + ns], outs[3 + ns:3 + ns + nl]), outs[-1]


def _exchange_wait(state, after, plan, name):
    sems, srcs, lands = state
    ns, nl = len(srcs), len(lands)

    def body(*refs):
        src_refs, land_refs = refs[:ns], refs[ns:ns + nl]
        send_sems, recv_sems, local_sems = refs[ns + nl:ns + nl + 3]
        rem, loc = _plan_descriptors(plan, src_refs, land_refs, send_sems, recv_sems, local_sems, arriving=True)
        for cp in rem:
            cp.wait_send()
            cp.wait_recv()
        for cp in loc:
            cp.wait()

    outs = _pallas(
        body, name=name, out_shape=[pltpu.HBM(a.shape, a.dtype) for a in list(srcs) + list(lands)],
        in_specs=[HBM_SPEC] * (ns + nl) + [SEM_SPEC] * 3 + [ANY_SPEC], out_specs=[HBM_SPEC] * (ns + nl),
        input_output_aliases={i: i for i in range(ns + nl)},
        compiler_params=pltpu.CompilerParams(has_side_effects=EFFECT),
    )(*srcs, *lands, *sems, after)
    return outs[:ns], outs[ns:]


def _gather_plan(nb):
    def plan(x, y, c):
        me = 4 * x + 2 * y + c
        remote = []
        for b in range(nb):
            for r in range(1, N_DEV):
                tx, ty, tc = (1 - x if r & 4 else x), (1 - y if r & 2 else y), (1 - c if r & 1 else c)
                remote.append((b, None, b, me, (tx, ty, tc), 4 * tx + 2 * ty + tc))
        return remote, [(b, None, b, me) for b in range(nb)]
    return plan


def _pair_plan(nb):
    def plan(x, y, c):
        return [(b, 2 * q + (1 - c), b, q, (x, y, 1 - c), q) for b in range(nb) for q in range(4)], []
    return plan


def _chip_plan(nb):
    def plan(x, y, c):
        own = 2 * x + y
        chips = [(1 - x, y), (x, 1 - y), (1 - x, 1 - y)]
        remote = [(b, 2 * cx + cy, b, own, (cx, cy, c), 2 * cx + cy) for b in range(nb) for cx, cy in chips]
        return remote, [(b, own, b, own) for b in range(nb)]
    return plan


def _row_tile(r, cap=4608):
    return max(d for d in range(16, min(r, cap) + 1, 16) if r % d == 0)


def _pair_add(parts, got, core, name):
    _, r, cdim = parts.shape
    tr = _row_tile(r)

    def body(core_ref, p_ref, g_ref, o_ref):
        o_ref[...] = (p_ref[...].astype(F32) + g_ref[...].astype(F32)).astype(o_ref.dtype)

    return pl.pallas_call(
        body, name=name, out_shape=SDS((4, r, cdim), BF16),
        grid_spec=pltpu.PrefetchScalarGridSpec(
            num_scalar_prefetch=1, grid=(4, r // tr),
            in_specs=[pl.BlockSpec((None, tr, cdim), lambda q, i, core_ref: (2 * q + core_ref[0], i, 0)),
                      pl.BlockSpec((None, tr, cdim), lambda q, i, core_ref: (q, i, 0))],
            out_specs=pl.BlockSpec((None, tr, cdim), lambda q, i, core_ref: (q, i, 0))),
        compiler_params=_params(("parallel", "parallel")),
    )(core, parts, got)


def _chip_exchange(sums, name):
    nb = len(sums)

    def body(*refs):
        b_refs, out_refs, (send_sems, recv_sems, local_sem) = refs[:nb], refs[nb:2 * nb], refs[2 * nb:]
        x, y, c = _place()
        own = 2 * x + y
        chips = [(1 - x, y), (x, 1 - y), (1 - x, 1 - y)]

        def copies(k, chip, src_slot, dst_slot):
            return [pltpu.make_async_remote_copy(src_ref=b_ref.at[src_slot], dst_ref=out_ref.at[dst_slot],
                                                 send_sem=send_sems.at[k, b], recv_sem=recv_sems.at[k, b], device_id=(*chip, c),
                                                 device_id_type=MESH) for b, (b_ref, out_ref) in enumerate(zip(b_refs, out_refs))]

        mine = [pltpu.make_async_copy(b_ref.at[own], out_ref.at[own], local_sem.at[b])
                for b, (b_ref, out_ref) in enumerate(zip(b_refs, out_refs))]
        sends = []
        for k, chip in enumerate(chips):
            sends += copies(k, chip, 2 * chip[0] + chip[1], own)
        for cp in mine + sends:
            cp.start()
        for k, chip in enumerate(chips):
            for cp in copies(k, chip, own, 2 * chip[0] + chip[1]):
                cp.wait_recv()
        for cp in sends:
            cp.wait_send()
        for cp in mine:
            cp.wait()

    hbm = pl.BlockSpec(memory_space=pltpu.HBM)
    return _pallas(
        body, name=name, out_shape=[SDS(s.shape, s.dtype) for s in sums], in_specs=[hbm] * nb, out_specs=[hbm] * nb,
        scratch_shapes=[pltpu.SemaphoreType.DMA((3, nb)), pltpu.SemaphoreType.DMA((3, nb)), pltpu.SemaphoreType.DMA((nb,))],
    )(*sums)


def _sum_slots(parts, name):
    nj, r, cdim = parts.shape
    tr = _row_tile(r)

    def body(p_ref, o_ref):
        acc = p_ref[0].astype(F32)
        for j in range(1, nj):
            acc = acc + p_ref[j].astype(F32)
        o_ref[...] = acc

    return _pallas(
        body, name=name, out_shape=SDS((r, cdim), F32), grid=(r // tr,),
        in_specs=[pl.BlockSpec((nj, tr, cdim), lambda i: (0, i, 0))], out_specs=pl.BlockSpec((tr, cdim), lambda i: (i, 0)),
        compiler_params=_params(("parallel",)),
    )(parts)


def _adamw_update(wv, gv, mv, vv):
    nm = ADAM_B1 * mv + (1.0 - ADAM_B1) * gv
    nv = ADAM_B2 * vv + (1.0 - ADAM_B2) * (gv * gv)
    m_hat = nm / (1.0 - ADAM_B1 ** ADAM_STEP)
    v_hat = nv / (1.0 - ADAM_B2 ** ADAM_STEP)
    return -ADAM_LR * (m_hat / (jnp.sqrt(v_hat) + ADAM_EPS) + ADAM_WD * wv), nm, nv


def _adamw_summed(w, m, v, by_chip, name):
    nl, r, cdim = w.shape
    tr = _row_tile(r, 512)
    nblk = r // tr

    def body(*refs):
        chip_refs, (w_ref, m_ref, v_ref, g_ref, d_ref, nm_ref, nv_ref) = refs[:nl], refs[nl:]
        layer = pl.program_id(0)
        gv = None
        for ll, c_ref in enumerate(chip_refs):
            s = c_ref[0].astype(F32)
            for q in range(1, 4):
                s = s + c_ref[q].astype(F32)
            gv = s if gv is None else jnp.where(layer == ll, s, gv)
        g_ref[...] = gv
        d_ref[...], nm_ref[...], nv_ref[...] = _adamw_update(w_ref[...], gv, m_ref[...], v_ref[...])

    def chip_map(ll):
        return lambda l, i: (0, jnp.where(l == ll, i, jnp.where(l > ll, nblk - 1, 0)), 0)

    spec = pl.BlockSpec((None, tr, cdim), lambda l, i: (l, i, 0))
    return _pallas(
        body, name=name, grid=(nl, nblk),
        in_specs=[pl.BlockSpec((4, tr, cdim), chip_map(ll)) for ll in range(nl)] + [spec] * 3,
        out_specs=[spec] * 4, out_shape=[SDS((nl, r, cdim), F32)] * 4,
        compiler_params=_params(("arbitrary", "arbitrary")),
    )(*by_chip, w, m, v)


def _adamw(w, g, m, v, name):
    shape = w.shape
    cdim = shape[-1]
    w2, g2, m2, v2 = (a.reshape(-1, cdim) for a in (w, g, m, v))
    r = w2.shape[0]
    tr = next(d for d in (512, 352, 256, 128, 64, 32, 16, 8, r) if r % d == 0)

    def body(w_ref, g_ref, m_ref, v_ref, d_ref, nm_ref, nv_ref):
        d_ref[...], nm_ref[...], nv_ref[...] = _adamw_update(w_ref[...], g_ref[...], m_ref[...], v_ref[...])

    spec = pl.BlockSpec((tr, cdim), lambda i: (i, 0))
    outs = _pallas(
        body, name=name, grid=(r // tr,), in_specs=[spec] * 4, out_specs=[spec] * 3, out_shape=[SDS((r, cdim), F32)] * 3,
        compiler_params=_params(("parallel",)),
    )(w2, g2, m2, v2)
    return tuple(o.reshape(shape) for o in outs)


WEIGHTS = ("norm_ffn1", "ffn1_w_in", "ffn1_w_out", "norm_mix", "norm_ffn2", "ffn2_w_in", "ffn2_w_out", "ple_norm", "ple_gate_w",
           "ple_proj_w", "hyb_w_in", "conv_dw_w", "conv_dw_b", "conv_ln_g", "conv_ln_b", "ssm_conv_w", "ssm_conv_b", "ssm_dt_bias",
           "ssm_a_log", "ssm_d", "ssm_norm", "hyb_w_out", "att_w_qkv", "att_b_qkv", "att_sinks", "att_w_o", "att_b_o", "final_norm")
BIG = ("ffn1_w_in", "ffn1_w_out", "ffn2_w_in", "ffn2_w_out", "ple_gate_w", "ple_proj_w", "hyb_w_in", "hyb_w_out", "att_w_qkv",
       "att_w_o")
SMALL_SHARDED = {"conv_dw_w": 2, "ssm_conv_w": 2, "att_b_qkv": 1, "att_b_o": 1}
SMALL = tuple(n for n in WEIGHTS if n not in BIG)
TRANSPOSED = ("ffn1_w_in", "ffn2_w_in", "att_w_qkv")
PACK_ROWS = 16


def _pack(arrays, lead=0):
    pieces = []
    for a in arrays:
        flat = a.reshape(*a.shape[:lead], -1)
        size = flat.shape[-1]
        padded = -(-size // (PACK_ROWS * LANE)) * PACK_ROWS * LANE
        flat = jnp.pad(flat, [(0, 0)] * lead + [(0, padded - size)])
        pieces.append(flat.reshape(*a.shape[:lead], padded // LANE, LANE))
    return jnp.concatenate(pieces, axis=lead)


def _unpack(buf, shapes, lead=0):
    out, row = [], 0
    for shape in shapes:
        size = math.prod(shape)
        rows = -(-size // (PACK_ROWS * LANE)) * PACK_ROWS
        piece = lax.slice_in_dim(buf, row, row + rows, axis=lead)
        piece = piece.reshape(*buf.shape[:lead], rows * LANE)
        out.append(lax.slice_in_dim(piece, 0, size, axis=lead).reshape(*buf.shape[:lead], *shape))
        row += rows
    return out


def kernel(x, p, norm_ffn1, ffn1_w_in, ffn1_w_out, norm_mix, norm_ffn2, ffn2_w_in, ffn2_w_out, ple_norm, ple_gate_w, ple_proj_w, hyb_w_in, conv_dw_w, conv_dw_b, conv_ln_g, conv_ln_b, ssm_conv_w, ssm_conv_b, ssm_dt_bias, ssm_a_log, ssm_d, ssm_norm, hyb_w_out, att_w_qkv, att_b_qkv, att_sinks, att_w_o, att_b_o, final_norm, loss_target, m_norm_ffn1, m_ffn1_w_in, m_ffn1_w_out, m_norm_mix, m_norm_ffn2, m_ffn2_w_in, m_ffn2_w_out, m_ple_norm, m_ple_gate_w, m_ple_proj_w, m_hyb_w_in, m_conv_dw_w, m_conv_dw_b, m_conv_ln_g, m_conv_ln_b, m_ssm_conv_w, m_ssm_conv_b, m_ssm_dt_bias, m_ssm_a_log, m_ssm_d, m_ssm_norm, m_hyb_w_out, m_att_w_qkv, m_att_b_qkv, m_att_sinks, m_att_w_o, m_att_b_o, m_final_norm, v_norm_ffn1, v_ffn1_w_in, v_ffn1_w_out, v_norm_mix, v_norm_ffn2, v_ffn2_w_in, v_ffn2_w_out, v_ple_norm, v_ple_gate_w, v_ple_proj_w, v_hyb_w_in, v_conv_dw_w, v_conv_dw_b, v_conv_ln_g, v_conv_ln_b, v_ssm_conv_w, v_ssm_conv_b, v_ssm_dt_bias, v_ssm_a_log, v_ssm_d, v_ssm_norm, v_hyb_w_out, v_att_w_qkv, v_att_b_qkv, v_att_sinks, v_att_w_o, v_att_b_o, v_final_norm):
    args = (norm_ffn1, ffn1_w_in, ffn1_w_out, norm_mix, norm_ffn2, ffn2_w_in, ffn2_w_out, ple_norm, ple_gate_w, ple_proj_w, hyb_w_in, conv_dw_w, conv_dw_b, conv_ln_g, conv_ln_b, ssm_conv_w, ssm_conv_b, ssm_dt_bias, ssm_a_log, ssm_d, ssm_norm, hyb_w_out, att_w_qkv, att_b_qkv, att_sinks, att_w_o, att_b_o, final_norm)
    moments_m = (m_norm_ffn1, m_ffn1_w_in, m_ffn1_w_out, m_norm_mix, m_norm_ffn2, m_ffn2_w_in, m_ffn2_w_out, m_ple_norm, m_ple_gate_w, m_ple_proj_w, m_hyb_w_in, m_conv_dw_w, m_conv_dw_b, m_conv_ln_g, m_conv_ln_b, m_ssm_conv_w, m_ssm_conv_b, m_ssm_dt_bias, m_ssm_a_log, m_ssm_d, m_ssm_norm, m_hyb_w_out, m_att_w_qkv, m_att_b_qkv, m_att_sinks, m_att_w_o, m_att_b_o, m_final_norm)
    moments_v = (v_norm_ffn1, v_ffn1_w_in, v_ffn1_w_out, v_norm_mix, v_norm_ffn2, v_ffn2_w_in, v_ffn2_w_out, v_ple_norm, v_ple_gate_w, v_ple_proj_w, v_hyb_w_in, v_conv_dw_w, v_conv_dw_b, v_conv_ln_g, v_conv_ln_b, v_ssm_conv_w, v_ssm_conv_b, v_ssm_dt_bias, v_ssm_a_log, v_ssm_d, v_ssm_norm, v_hyb_w_out, v_att_w_qkv, v_att_b_qkv, v_att_sinks, v_att_w_o, v_att_b_o, v_final_norm)
    w = dict(zip(WEIGHTS, args))
    m = dict(zip(WEIGHTS, moments_m))
    v = dict(zip(WEIGHTS, moments_v))
    cx, cy, cc = _place()
    me = 4 * cx + 2 * cy + cc

    core = jnp.reshape(cc, (1,)).astype(jnp.int32)
    layer_of = lambda n, i: 1 if n.startswith("att_") else i
    keys = [[(n, i) for n in BIG for i in range(w[n].shape[0]) if layer_of(n, i) == layer] for layer in range(2)]

    first = [key for key in keys[0] if key[0].startswith("ffn1")]
    mixer = [key for key in keys[0] if key[0].startswith("hyb")]
    rest0 = [key for key in keys[0] if key not in first + mixer]
    gw, by_chip = {}, {}
    view = lambda a, n: jnp.swapaxes(a, 1, 2) if n in TRANSPOSED else a
    block = lambda n, i: view(w[n], n)[i].astype(BF16)

    def gather_later(group, name):
        blocks = [block(n, i) for n, i in group]
        plan = _gather_plan(len(blocks))
        state, token = _exchange_start(blocks, [SDS((N_DEV,) + b.shape, BF16) for b in blocks], plan, f"{name}_start")
        return token, lambda after: gw.update(zip(group, _exchange_wait(state, after, plan, f"{name}_wait")[1]))

    def reduce_later(group, big, name):
        pair_plan, chip_plan = _pair_plan(len(group)), _chip_plan(len(group))
        parts = [big[key] for key in group]
        pair, token = _exchange_start(parts, [SDS((4,) + pt.shape[1:], BF16) for pt in parts], pair_plan, f"{name}_pair_start")
        stage = {}

        def middle(after):
            thru, got = _exchange_wait(pair, after, pair_plan, f"{name}_pair_wait")
            sums = [_pair_add(pt, gt, core, f"grads_pair_add_{n}_{i}") for pt, gt, (n, i) in zip(thru, got, group)]
            stage["chip"], chip_token = _exchange_start(sums, [SDS(s.shape, BF16) for s in sums], chip_plan, f"{name}_chip_start")
            return chip_token

        def finish(after):
            by_chip.update(zip(group, _exchange_wait(stage["chip"], after, chip_plan, f"{name}_chip_wait")[1]))

        return token, middle, finish

    _restart_chain()
    *gathered_first, gathered_small = _all_gather([block(n, i) for n, i in first] + [_pack([w[n] for n in SMALL_SHARDED])],
                                                  pltpu.HBM, "gather_weights_first")
    gw.update(zip(first, gathered_first))
    mixer_token, mixer_arrived = gather_later(mixer, "gather_weights_mixer")
    rest0_token, rest0_arrived = gather_later(rest0, "gather_weights_rest")
    layer1_token, layer1_arrived = gather_later(keys[1], "gather_weights_l1")
    gs = dict(zip(SMALL_SHARDED, _unpack(gathered_small, [w[n].shape for n in SMALL_SHARDED], lead=1)))
    rep = {n: w[n] for n in SMALL if n not in SMALL_SHARDED}

    tables = _rope_tables(x.shape[1])
    pb = p[:, 0].astype(BF16)
    w0, s0 = _build_layer(0, gw, gs, rep, parts=("ffn1",)), {}
    h, s0["ffn1"] = _ffn_fwd(x[0], w0["norm_ffn1"], w0["ffn1_in"], w0["ffn1_out"], "l0_ffn1",
                             deps=(mixer_token, rest0_token, layer1_token))
    mixer_arrived(h)
    w0.update(_build_layer(0, gw, gs, rep, parts=("mix",)))
    h, s0["mix"] = _hyb_fwd(h, w0, "l0_hyb")
    rest0_arrived(h)
    w0.update(_build_layer(0, gw, gs, rep, parts=("ffn2", "ple")))
    h, s0["ffn2"] = _ffn_fwd(h, w0["norm_ffn2"], w0["ffn2_in"], w0["ffn2_out"], "l0_ffn2")
    h, s0["ple"] = _ple_fwd(h, w0["ple_norm"], pb[0], w0["ple_gate"], w0["ple_proj"], "l0_ple")
    layer1_arrived(h)
    w1 = _build_layer(1, gw, gs, rep)
    h, s1 = _layer_fwd(1, h, w1, pb[1], tables)
    loss, dh, dhb, d_final = _loss_head(h, loss_target[0], final_norm[None], "loss_head")
    loss = lax.psum(loss[0, 0], ("x", "y", "c"))

    dh, dhb, head1 = _layer_bwd_head(1, dh, dhb, s1, w1, pb[1])
    dh, dhb, tail1 = _layer_bwd_tail(1, dh, dhb, s1, w1, tables)
    grads1 = {**head1, **tail1}
    l1_token, l1_middle, l1_finish = reduce_later(keys[1], _big_grads(1, grads1), "grads_l1")
    dh, dhb, grads0 = _layer_bwd_head(0, dh, dhb, s0, w0, pb[0], deps=(l1_token,))
    dh, dhb, grads0["norm_ffn2"], grads0["ffn2_in"], grads0["ffn2_out"] = _ffn_bwd(
        dh, dhb, s0["ffn2"], w0["norm_ffn2"], w0["ffn2_in"], w0["ffn2_out"], "l0_ffn2", deps=(l1_middle(dh),))
    dh, dhb, mixer_grads = _hyb_bwd(dh, dhb, s0["mix"], w0, "l0_hyb")
    grads0.update(mixer_grads)
    l0_token, l0_middle, l0_finish = reduce_later(mixer + rest0, _big_grads(0, grads0), "grads_l0")
    dx, dhb, grads0["norm_ffn1"], grads0["ffn1_in"], grads0["ffn1_out"] = _ffn_bwd(
        dh, dhb, s0["ffn1"], w0["norm_ffn1"], w0["ffn1_in"], w0["ffn1_out"], "l0_ffn1", deps=(l0_token,),
        hook=lambda dpre: (l0_middle(dpre),))
    l1_finish(dx)
    l0_finish(dx)
    big0 = _big_grads(0, grads0)
    parts0 = [big0[key] for key in first]
    got0 = _pair_exchange(parts0, "grads_first_pair_exchange")
    sums0 = [_pair_add(pt, gt, core, f"grads_pair_add_{n}_{i}") for pt, gt, (n, i) in zip(parts0, got0, first)]
    by_chip.update(zip(first, _chip_exchange(sums0, "grads_first_chip_exchange")))
    _, small = _collect_grads([grads0, grads1], d_final)
    small_shapes = [small[n].shape for n in SMALL]
    all_small = _all_gather([_pack([small[n] for n in SMALL])], pltpu.VMEM, "gather_small_grads")[0]
    g = dict(zip(SMALL, _unpack(_sum_slots(all_small, "small_grads_sum"), small_shapes)))
    for n, axis in SMALL_SHARDED.items():
        g[n] = lax.dynamic_slice_in_dim(g[n], me * w[n].shape[axis], w[n].shape[axis], axis=axis)

    delta, new_m, new_v = {}, {}, {}
    for n in BIG:
        outs = _adamw_summed(view(w[n], n), view(m[n], n), view(v[n], n), [by_chip[n, i] for i in range(w[n].shape[0])],
                             f"adamw_{n}")
        g[n], delta[n], new_m[n], new_v[n] = (view(o, n) for o in outs)
    packed = [_pack([d[n] for n in SMALL]) for d in (w, g, m, v)]
    shapes = [w[n].shape for n in SMALL]
    for d, buf in zip((delta, new_m, new_v), _adamw(*packed, "adamw_small")):
        d.update(zip(SMALL, _unpack(buf, shapes)))
    return (loss, dx[None], *[g[n] for n in WEIGHTS], *[delta[n] for n in WEIGHTS], *[new_m[n] for n in WEIGHTS],
            *[new_v[n] for n in WEIGHTS])
```

```python
import functools
import math

import numpy as np
import jax
import jax.numpy as jnp
from jax import lax
from jax.experimental import pallas as pl
from jax.experimental.pallas import tpu as pltpu

F32, BF16 = jnp.float32, jnp.bfloat16
HI = lax.Precision.HIGHEST
SDS = jax.ShapeDtypeStruct

N_DEV = 8
D = 1024
D_FF = 2816
FF_SHARD = 2 * D_FF // N_DEV
PLE_DIM = 256
EPS = 1e-6
CONV_W = 31
SSM_CONV = 4
SSM_HEADS = 16
SSM_XBC = 1536
CHUNK = 128
HYB_IN = 4624
HYB_PAD = 5120
DT_COL = 4608
N_PAIR = 8
ROPE_THETA = 10000.0
LANE = 128
VMEM_LIMIT = 56 * 1024 * 1024

ADAM_LR, ADAM_B1, ADAM_B2, ADAM_EPS, ADAM_WD, ADAM_STEP = 0.001, 0.9, 0.999, 1e-08, 0.01, 10


def _params(sem):
    return pltpu.CompilerParams(dimension_semantics=sem, vmem_limit_bytes=VMEM_LIMIT)


_CHAIN = []


def _restart_chain():
    _CHAIN.clear()


def _pallas(body, *, in_specs, **kw):
    def run(*args):
        n, dep = len(args), list(_CHAIN)

        def chained(*refs):
            return body(*refs[:n], *refs[n + len(dep):])

        outs = pl.pallas_call(chained, in_specs=list(in_specs) + [pl.BlockSpec(memory_space=pl.ANY)] * len(dep), **kw)(*args, *dep)
        _CHAIN[:] = [outs[-1] if isinstance(outs, (list, tuple)) else outs]
        return outs

    return run


def _mm(a, b, *, ta=False, tb=False, reduce_j=False, out_dtypes=(F32,), tm=1024, tn=1024, tk=1024,
        epi=None, extras=(), rows=(), deps=(), sums=0, name):
    ja, jb = a.shape[0], b.shape[0]
    nj = max(ja, jb)
    jo = 1 if reduce_j else nj
    m, k = (a.shape[2], a.shape[1]) if ta else (a.shape[1], a.shape[2])
    n = b.shape[1] if tb else b.shape[2]
    assert (b.shape[2] if tb else b.shape[1]) == k and ja in (1, nj) and jb in (1, nj)
    tm, tn, tk = min(tm, m), min(tn, n), min(tk, k)
    assert m % tm == 0 and n % tn == 0 and k % tk == 0, (name, m, n, k, tm, tn, tk)
    assert not sums or (tn == n and (reduce_j or nj == 1))
    nk = k // tk
    steps = nk * (nj if reduce_j else 1)
    ne, nr, no = len(extras), len(rows), len(out_dtypes)

    def a_map(i, c, j, kk):
        return (j if ja > 1 else 0, kk, i) if ta else (j if ja > 1 else 0, i, kk)

    def b_map(i, c, j, kk):
        return (j if jb > 1 else 0, c, kk) if tb else (j if jb > 1 else 0, kk, c)

    def o_map(i, c, j, kk):
        return (0 if reduce_j else j, i, c)

    dims = (((0 if ta else 1,), (1 if tb else 0,)), ((), ()))

    def body(a_ref, b_ref, *rest):
        ex, rw = rest[:ne], rest[ne:ne + nr]
        outs = rest[ne + nr + len(deps):ne + nr + len(deps) + no]
        sum_refs = rest[ne + nr + len(deps) + no:ne + nr + len(deps) + no + sums]
        first_tile = pl.program_id(0) == 0

        def product():
            return lax.dot_general(a_ref[...], b_ref[...], dims, preferred_element_type=F32)

        def finish(acc):
            res = epi(acc, *[e[...] for e in ex], *[r[...] for r in rw]) if epi else (acc,)
            for o, r in zip(outs, res):
                o[...] = r.astype(o.dtype)
            for s_ref, r in zip(sum_refs, res[no:]):
                @pl.when(first_tile)
                def _(s_ref=s_ref, r=r):
                    s_ref[...] = r

                @pl.when(jnp.logical_not(first_tile))
                def _(s_ref=s_ref, r=r):
                    s_ref[...] += r

        if steps == 1:
            finish(product())
            return
        acc_ref = rest[-1]
        kk = pl.program_id(3)
        step = pl.program_id(2) * nk + kk if reduce_j else kk

        @pl.when(step == 0)
        def _():
            acc_ref[...] = product()

        @pl.when(jnp.logical_and(step > 0, step < steps - 1))
        def _():
            acc_ref[...] += product()

        @pl.when(step == steps - 1)
        def _():
            finish(acc_ref[...] + product())

    o_spec = pl.BlockSpec((None, tm, tn), o_map)
    row_spec = pl.BlockSpec((1, tn), lambda i, c, j, kk: (0, c))
    return _pallas(
        body, name=name, grid=(m // tm, n // tn, nj, nk),
        in_specs=[pl.BlockSpec((None, tk, tm) if ta else (None, tm, tk), a_map),
                  pl.BlockSpec((None, tn, tk) if tb else (None, tk, tn), b_map)]
        + [o_spec] * ne + [row_spec] * nr + [ANY_SPEC] * len(deps),
        out_specs=[o_spec] * no + [row_spec] * sums,
        out_shape=[SDS((jo, m, n), dt) for dt in out_dtypes] + [SDS((1, n), F32)] * sums,
        scratch_shapes=[pltpu.VMEM((tm, tn), F32)] if steps > 1 else [],
        compiler_params=_params(("arbitrary" if sums else "parallel", "parallel", "arbitrary", "arbitrary")),
    )(a, b, *extras, *rows, *deps)


def _whole(p):
    return pl.BlockSpec(p.shape, lambda *_: (0,) * p.ndim)


ANY_SPEC = pl.BlockSpec(memory_space=pl.ANY)


def _rowop(fn, tiles, params, outs, *, grid, name, deps=()):
    nin = len(tiles) + len(params)

    def body(*refs):
        res = fn(*[r[...].astype(F32) for r in refs[:nin]])
        for r, o in zip(refs[nin + len(deps):], res):
            r[...] = o.astype(r.dtype)

    return _pallas(
        body, name=name, grid=grid,
        in_specs=[s for _, s in tiles] + [_whole(p) for p in params] + [ANY_SPEC] * len(deps),
        out_specs=[s for _, _, s in outs], out_shape=[SDS(sh, dt) for sh, dt, _ in outs],
        compiler_params=_params(("parallel",) * len(grid)),
    )(*[t for t, _ in tiles], *params, *deps)


def _rowop_bwd(fn, tiles, params, cots, wrt, gouts, *, grid, name, adds=(), deps=()):
    nt, npar, nc, na = len(tiles), len(params), len(cots), len(adds)
    nin = nt + npar
    flat = [i for grp in wrt for i in grp]
    n_gout = sum(len(dts) for _, dts, _ in gouts)

    def body(*refs):
        vals = [r[...].astype(F32) for r in refs[:nin]]
        cvals = [r[...].astype(F32) for r in refs[nin:nin + nc]]
        avals = [r[...].astype(F32) for r in refs[nin + nc:nin + nc + na]]
        orefs = refs[nin + nc + na + len(deps):]
        diff_idx = flat + list(range(nt, nin))

        def f(*dv):
            full = list(vals)
            for i, v in zip(diff_idx, dv):
                full[i] = v
            return fn(*full)

        _, vjp = jax.vjp(f, *[vals[i] for i in diff_idx])
        grads = vjp(tuple(cvals))
        tile_g, par_g = list(grads[:len(flat)]), grads[len(flat):]
        group_g, at = [], 0
        for grp in wrt:
            members = tile_g[at:at + len(grp)]
            at += len(grp)
            group_g.append(members[0] if len(grp) == 1 else jnp.stack(members, axis=0))
        for av in avals:
            group_g[0] = group_g[0] + av
        o = 0
        for g, (_, dts, _) in zip(group_g, gouts):
            for _ in dts:
                orefs[o][...] = g.astype(orefs[o].dtype)
                o += 1
        first = functools.reduce(jnp.logical_and, [pl.program_id(ax) == 0 for ax in range(len(grid))])
        for r, g in zip(orefs[n_gout:], par_g):
            @pl.when(first)
            def _(r=r, g=g):
                r[...] = g

            @pl.when(jnp.logical_not(first))
            def _(r=r, g=g):
                r[...] += g

    out_specs, out_shape = [], []
    for sh, dts, spec in gouts:
        for dt in dts:
            out_specs.append(spec)
            out_shape.append(SDS(sh, dt))
    for p in params:
        out_specs.append(_whole(p))
        out_shape.append(SDS(p.shape, F32))
    return _pallas(
        body, name=name, grid=grid,
        in_specs=[s for _, s in tiles] + [_whole(p) for p in params] + [s for _, s in cots] + [s for _, s in adds]
        + [ANY_SPEC] * len(deps),
        out_specs=out_specs, out_shape=out_shape,
        compiler_params=_params(("arbitrary",) * len(grid)),
    )(*[t for t, _ in tiles], *params, *[c for c, _ in cots], *[a for a, _ in adds], *deps)


def _tok(c, tm, col=0):
    return pl.BlockSpec((tm, c), lambda i, col=col: (i, col))


def _rms_fn(h, g):
    return (h * lax.rsqrt(jnp.mean(h * h, axis=-1, keepdims=True) + EPS) * g,)


def _lnswish_fn(u, g, b):
    mu = jnp.mean(u, axis=-1, keepdims=True)
    xc = u - mu
    y = xc * lax.rsqrt(jnp.mean(xc * xc, axis=-1, keepdims=True) + EPS) * g + b
    return (y * jax.nn.sigmoid(y),)


def _ple_fn(z, e):
    return (jax.nn.sigmoid(z) * e,)


def _rms(h, g, name, tm=512, deps=()):
    t = h.shape[0]
    return _rowop(_rms_fn, [(h, _tok(D, tm))], [g], [((t, D), BF16, _tok(D, tm))], grid=(t // tm,), name=name, deps=deps)[0]


def _drms_epi(dn, h, dres, g):
    _, vjp = jax.vjp(_rms_fn, h, g)
    dh, dg = vjp((dn,))
    dh = dh + dres
    return dh, dh, dg


def _mm_drms(a, b, h, g, dres, name, tk, tb=True):
    dh, dhb, dg = _mm(a, b, tb=tb, reduce_j=a.shape[0] > 1, tm=512, tk=tk, epi=_drms_epi, extras=(h[None], dres[None]),
                      rows=(g,), out_dtypes=(F32, BF16), sums=1, name=name)
    return dh[0], dhb[0], dg


def _conv_geometry(width):
    pad = 32 if width > 8 else 8
    return pad, pad - (width - 1)


def _fill_shifts(xpad_ref, sh_ref, t, shifts):
    for r in shifts:
        sh_ref[r, :, :] = xpad_ref[pl.ds(r, t + 32), :]


def _dwconv(xs, w, b, *, width, glu, silu, cb, name):
    t = xs[0][0].shape[0]
    c = w.shape[1]
    pad, off = _conv_geometry(width)
    shifts = sorted({(k + off) % 8 for k in range(width)})
    ch = 32

    def body(*refs):
        x_refs, (w_ref, b_ref, o_ref, xpad_ref, sh_ref) = refs[:len(xs)], refs[len(xs):]
        u = x_refs[0][...] * jax.nn.sigmoid(x_refs[1][...]) if glu else x_refs[0][...]
        xpad_ref[pl.ds(0, pad), :] = jnp.zeros((pad, cb), F32)
        xpad_ref[pl.ds(pad, t), :] = u
        xpad_ref[pl.ds(pad + t, 40 - pad), :] = jnp.zeros((40 - pad, cb), F32)
        _fill_shifts(xpad_ref, sh_ref, t, shifts)

        def chunk(i, carry):
            t0 = pl.multiple_of(i * ch, ch)
            acc = jnp.broadcast_to(b_ref[...], (ch, cb))
            for k in range(width):
                q, r = divmod(k + off, 8)
                acc = acc + w_ref[pl.ds(k, 1), :] * sh_ref[r, pl.ds(t0 + 8 * q, ch), :]
            o_ref[pl.ds(t0, ch), :] = acc * jax.nn.sigmoid(acc) if silu else acc
            return carry

        lax.fori_loop(0, t // ch, chunk, 0)

    return _pallas(
        body, name=name, grid=(c // cb,),
        in_specs=[pl.BlockSpec((t, cb), lambda i, o=o: (0, o + i)) for _, o in xs]
        + [pl.BlockSpec((width, cb), lambda i: (0, i)), pl.BlockSpec((1, cb), lambda i: (0, i))],
        out_specs=pl.BlockSpec((t, cb), lambda i: (0, i)), out_shape=SDS((t, c), F32),
        scratch_shapes=[pltpu.VMEM((t + 40, cb), F32), pltpu.VMEM((8, t + 32, cb), F32)],
        compiler_params=_params(("parallel",)),
    )(*[x for x, _ in xs], w, b)


def _dwconv_bwd(xs, w, b, dy, *, width, glu, silu, cb, name):
    t = xs[0][0].shape[0]
    c = w.shape[1]
    pad, off = _conv_geometry(width)
    shifts = sorted({(k + off) % 8 for k in range(width)})
    shifts_t = sorted({mm % 8 for mm in range(width)})
    ch = 32
    nx = len(xs)

    def body(*refs):
        x_refs = refs[:nx]
        w_ref, b_ref, dy_ref = refs[nx:nx + 3]
        dx_refs = refs[nx + 3:nx + 3 + nx]
        dw_ref, db_ref, xpad_ref, sh_ref, dc_ref = refs[nx + 3 + nx:]
        u = x_refs[0][...] * jax.nn.sigmoid(x_refs[1][...]) if glu else x_refs[0][...]
        xpad_ref[pl.ds(0, pad), :] = jnp.zeros((pad, cb), F32)
        xpad_ref[pl.ds(pad, t), :] = u
        xpad_ref[pl.ds(pad + t, 40 - pad), :] = jnp.zeros((40 - pad, cb), F32)
        _fill_shifts(xpad_ref, sh_ref, t, shifts)

        if silu:
            def act_chunk(i, carry):
                t0 = pl.multiple_of(i * ch, ch)
                acc = jnp.broadcast_to(b_ref[...], (ch, cb))
                for k in range(width):
                    q, r = divmod(k + off, 8)
                    acc = acc + w_ref[pl.ds(k, 1), :] * sh_ref[r, pl.ds(t0 + 8 * q, ch), :]
                sg = jax.nn.sigmoid(acc)
                dc_ref[pl.ds(t0, ch), :] = dy_ref[pl.ds(t0, ch), :] * (sg * (1.0 + acc * (1.0 - sg)))
                return carry

            lax.fori_loop(0, t // ch, act_chunk, 0)
        else:
            dc_ref[...] = dy_ref[...]

        def dw_chunk(i, accs):
            t0 = pl.multiple_of(i * 8, 8)
            d = dc_ref[pl.ds(t0, 8), :]
            new = []
            for k in range(width):
                q, r = divmod(k + off, 8)
                new.append(accs[k] + d * sh_ref[r, pl.ds(t0 + 8 * q, 8), :])
            new.append(accs[width] + d)
            return tuple(new)

        accs = lax.fori_loop(0, t // 8, dw_chunk, tuple(jnp.zeros((8, cb), F32) for _ in range(width + 1)))
        for k in range(width):
            dw_ref[pl.ds(k, 1), :] = jnp.sum(accs[k], axis=0, keepdims=True)
        db_ref[...] = jnp.sum(accs[width], axis=0, keepdims=True)

        xpad_ref[pl.ds(0, t), :] = dc_ref[...]
        xpad_ref[pl.ds(t, 40), :] = jnp.zeros((40, cb), F32)
        _fill_shifts(xpad_ref, sh_ref, t, shifts_t)

        def dx_chunk(i, carry):
            t0 = pl.multiple_of(i * ch, ch)
            acc = jnp.zeros((ch, cb), F32)
            for mm in range(width):
                q, r = divmod(mm, 8)
                acc = acc + w_ref[pl.ds(width - 1 - mm, 1), :] * sh_ref[r, pl.ds(t0 + 8 * q, ch), :]
            if glu:
                val, gate = x_refs[0][pl.ds(t0, ch), :], x_refs[1][pl.ds(t0, ch), :]
                sg = jax.nn.sigmoid(gate)
                dx_refs[0][pl.ds(t0, ch), :] = (acc * sg).astype(BF16)
                dx_refs[1][pl.ds(t0, ch), :] = (acc * val * sg * (1.0 - sg)).astype(BF16)
            else:
                dx_refs[0][pl.ds(t0, ch), :] = acc.astype(BF16)
            return carry

        lax.fori_loop(0, t // ch, dx_chunk, 0)

    col = pl.BlockSpec((t, cb), lambda i: (0, i))
    return _pallas(
        body, name=name, grid=(c // cb,),
        in_specs=[pl.BlockSpec((t, cb), lambda i, o=o: (0, o + i)) for _, o in xs]
        + [pl.BlockSpec((width, cb), lambda i: (0, i)), pl.BlockSpec((1, cb), lambda i: (0, i)), col],
        out_specs=[col] * nx + [pl.BlockSpec((width, cb), lambda i: (0, i)), pl.BlockSpec((1, cb), lambda i: (0, i))],
        out_shape=[SDS((t, c), BF16)] * nx + [SDS((width, c), F32), SDS((1, c), F32)],
        scratch_shapes=[pltpu.VMEM((t + 40, cb), F32), pltpu.VMEM((8, t + 32, cb), F32), pltpu.VMEM((t, cb), F32)],
        compiler_params=_params(("parallel",)),
    )(*[x for x, _ in xs], w, b, dy)


_DIMS = {"nn": (((1,), (0,)), ((), ())), "nt": (((1,), (1,)), ((), ())), "tn": (((0,), (0,)), ((), ()))}


def _raw_dot(a, b, mode):
    return lax.dot_general(a.astype(BF16), b.astype(BF16), _DIMS[mode], preferred_element_type=F32)


@functools.partial(jax.custom_vjp, nondiff_argnums=(2,))
def _bdot(a, b, mode):
    return _raw_dot(a, b, mode)


def _bdot_fwd(a, b, mode):
    return _raw_dot(a, b, mode), (a, b)


def _bdot_bwd(mode, res, g):
    a, b = res
    if mode == "nn":
        return _raw_dot(g, b, "nt"), _raw_dot(a, g, "tn")
    if mode == "nt":
        return _raw_dot(g, b, "nn"), _raw_dot(g, a, "tn")
    return _raw_dot(b, g, "nt"), _raw_dot(a, g, "nn")


_bdot.defvjp(_bdot_fwd, _bdot_bwd)


def _iota(shape, axis):
    return lax.broadcasted_iota(jnp.int32, shape, axis)


def _half_masks():
    left = (_iota((1, LANE), 1) < 64).astype(F32)
    return left, 1.0 - left


def _split3(a):
    a1 = a.astype(BF16)
    r1 = a - a1.astype(F32)
    a2 = r1.astype(BF16)
    return a1, a2, (r1 - a2.astype(F32)).astype(BF16)


def _exact_dot(a, e, mode):
    return sum(lax.dot_general(piece, e, _DIMS[mode], preferred_element_type=F32) for piece in _split3(a))


@jax.custom_vjp
def _spread(a, e):
    return _exact_dot(a, e, "nn")


_spread.defvjp(lambda a, e: (_exact_dot(a, e, "nn"), e), lambda e, g: (_exact_dot(g, e, "nt"), jnp.zeros_like(e)))


@jax.custom_vjp
def _running_sum(tri, a):
    return sum(lax.dot_general(tri, piece, _DIMS["nn"], preferred_element_type=F32) for piece in _split3(a))


_running_sum.defvjp(
    lambda tri, a: (sum(lax.dot_general(tri, piece, _DIMS["nn"], preferred_element_type=F32) for piece in _split3(a)), tri),
    lambda tri, g: (jnp.zeros_like(tri), sum(lax.dot_general(tri, piece, _DIMS["tn"], preferred_element_type=F32)
                                             for piece in _split3(g))))


def _ssd_chunk(state, xa, dtr, z, dtb, alog, dskf, ng):
    xs, bm, cm = xa[:, :D], xa[:, D:D + 256], xa[:, D + 256:]
    left, right = _half_masks()
    expand = (_iota((LANE, D), 1) // 64 == _iota((LANE, D), 0)).astype(BF16)
    li, si = _iota((CHUNK, CHUNK), 0), _iota((CHUNK, CHUNK), 1)
    tril = li >= si
    dt16 = jax.nn.softplus(dtr + dtb)
    adt = dt16 * (-jnp.exp(alog))
    dtf = _spread(dt16, expand)
    cs16 = _running_sum(tril.astype(BF16), adt)
    csf = _spread(cs16, expand)
    totf = jnp.sum(jnp.where(_iota((CHUNK, D), 0) == CHUNK - 1, csf, 0.0), axis=0, keepdims=True)
    cst = cs16.T
    xdt = xs * dtf
    ys, new_state = [], []
    for g in range(2):
        bg, cg = bm[:, LANE * g:LANE * (g + 1)], cm[:, LANE * g:LANE * (g + 1)]
        cb = _bdot(cg, bg, "nt")
        for q in range(4):
            pr = 4 * g + q
            decay = []
            for h in (2 * pr, 2 * pr + 1):
                col = jnp.sum(jnp.where(si == h, cs16, 0.0), axis=1, keepdims=True)
                row = jnp.sum(jnp.where(li == h, cst, 0.0), axis=0, keepdims=True)
                decay.append(cb * jnp.exp(jnp.where(tril, col - row, -jnp.inf)))
            xp = xdt[:, LANE * pr:LANE * (pr + 1)]
            y_diag = _bdot(jnp.concatenate(decay, axis=1), jnp.concatenate([xp * left, xp * right], axis=0), "nn")
            csb, tot = csf[:, LANE * pr:LANE * (pr + 1)], totf[:, LANE * pr:LANE * (pr + 1)]
            ys.append(y_diag + _bdot(cg, state[pr], "nn") * jnp.exp(csb))
            new_state.append(state[pr] * jnp.exp(tot) + _bdot(bg, xp * jnp.exp(tot - csb), "tn"))
    y = jnp.concatenate(ys, axis=1)
    y = y + dskf * xs
    y = y * (z * jax.nn.sigmoid(z))
    halves = []
    for g in range(2):
        yg = y[:, 512 * g:512 * (g + 1)]
        halves.append(yg * lax.rsqrt(jnp.mean(yg * yg, axis=-1, keepdims=True) + EPS))
    return jnp.concatenate(halves, axis=1) * ng, jnp.stack(new_state, axis=0)


def _ssd_specs(t, rev):
    nc = t // CHUNK
    ix = (lambda c: nc - 1 - c) if rev else (lambda c: c)
    return nc, ix


def _ssd_fwd(xa, proj, dtb, alog, dsk, ng, name):
    t = xa.shape[0]
    nc, ix = _ssd_specs(t, False)

    def body(xa_ref, dt_ref, z_ref, dtb_ref, alog_ref, dsk_ref, ng_ref, y_ref, st_ref, carry_ref):
        @pl.when(pl.program_id(0) == 0)
        def _():
            carry_ref[...] = jnp.zeros_like(carry_ref)

        st_ref[...] = carry_ref[...]
        y, new = _ssd_chunk(carry_ref[...], xa_ref[...], dt_ref[...], z_ref[...], dtb_ref[...], alog_ref[...],
                            dsk_ref[...], ng_ref[...])
        y_ref[...] = y.astype(BF16)
        carry_ref[...] = new

    small = [dtb, alog, dsk, ng]
    return _pallas(
        body, name=name, grid=(nc,),
        in_specs=[pl.BlockSpec((CHUNK, SSM_XBC), lambda c: (c, 0)),
                  pl.BlockSpec((CHUNK, LANE), lambda c: (c, DT_COL // LANE)),
                  pl.BlockSpec((CHUNK, D), lambda c: (c, 2))] + [_whole(p) for p in small],
        out_specs=[pl.BlockSpec((CHUNK, D), lambda c: (c, 0)), pl.BlockSpec((None, N_PAIR, LANE, LANE), lambda c: (c, 0, 0, 0))],
        out_shape=[SDS((t, D), BF16), SDS((nc, N_PAIR, LANE, LANE), F32)],
        scratch_shapes=[pltpu.VMEM((N_PAIR, LANE, LANE), F32)],
        compiler_params=_params(("arbitrary",)),
    )(xa, proj, proj, *small)


def _ssd_bwd(xa, proj, states, dy, dtb, alog, dsk, ng, name):
    t = xa.shape[0]
    nc, ix = _ssd_specs(t, True)

    def body(xa_ref, dt_ref, z_ref, st_ref, dy_ref, dtb_ref, alog_ref, dsk_ref, ng_ref,
             dxa_ref, ddt_ref, dz_ref, gdtb_ref, galog_ref, gdsk_ref, gng_ref, carry_ref):
        first = pl.program_id(0) == 0

        @pl.when(first)
        def _():
            carry_ref[...] = jnp.zeros_like(carry_ref)

        args = (st_ref[...], xa_ref[...], dt_ref[...], z_ref[...], dtb_ref[...], alog_ref[...], dsk_ref[...], ng_ref[...])
        _, vjp = jax.vjp(_ssd_chunk, *args)
        ds, dxa, ddt, dz, gdtb, galog, gdsk, gng = vjp((dy_ref[...], carry_ref[...]))
        carry_ref[...] = ds
        dxa_ref[...] = dxa
        ddt_ref[...] = ddt.astype(BF16)
        dz_ref[...] = dz.astype(BF16)
        for r, g in ((gdtb_ref, gdtb), (galog_ref, galog), (gdsk_ref, gdsk), (gng_ref, gng)):
            @pl.when(first)
            def _(r=r, g=g):
                r[...] = g

            @pl.when(jnp.logical_not(first))
            def _(r=r, g=g):
                r[...] += g

    small = [dtb, alog, dsk, ng]
    return _pallas(
        body, name=name, grid=(nc,),
        in_specs=[pl.BlockSpec((CHUNK, SSM_XBC), lambda c: (ix(c), 0)),
                  pl.BlockSpec((CHUNK, LANE), lambda c: (ix(c), DT_COL // LANE)),
                  pl.BlockSpec((CHUNK, D), lambda c: (ix(c), 2)),
                  pl.BlockSpec((None, N_PAIR, LANE, LANE), lambda c: (ix(c), 0, 0, 0)),
                  pl.BlockSpec((CHUNK, D), lambda c: (ix(c), 0))] + [_whole(p) for p in small],
        out_specs=[pl.BlockSpec((CHUNK, SSM_XBC), lambda c: (ix(c), 0)), pl.BlockSpec((CHUNK, LANE), lambda c: (ix(c), 0)),
                   pl.BlockSpec((CHUNK, D), lambda c: (ix(c), 0))] + [_whole(p) for p in small],
        out_shape=[SDS((t, SSM_XBC), F32), SDS((t, LANE), BF16), SDS((t, D), BF16)] + [SDS(p.shape, F32) for p in small],
        scratch_shapes=[pltpu.VMEM((N_PAIR, LANE, LANE), F32)],
        compiler_params=_params(("arbitrary",)),
    )(xa, proj, proj, states, dy, *small)


def _attn_block(q, kv_prev, kv_cur, cq, sq, ck, sk, sinks, rot, first_block):
    left, right = _half_masks()
    k2 = jnp.concatenate([kv_prev[:, :256], kv_cur[:, :256]], axis=0)
    v2 = jnp.concatenate([kv_prev[:, 256:], kv_cur[:, 256:]], axis=0)
    ri, ci = _iota((LANE, LANE), 0), _iota((LANE, LANE), 1)
    dup = [((ri < 64) & (ci % 64 == ri)).astype(BF16), ((ri >= 64) & (ci % 64 == ri - 64)).astype(BF16)]

    def rope(tt, c, s):
        return tt * c + jnp.dot(tt, rot, precision=HI) * s

    kd, vd = [], []
    for j in range(4):
        sl = slice(LANE * (j // 2), LANE * (j // 2 + 1))
        kd.append(_bdot(rope(k2[:, sl], ck, sk), dup[j % 2], "nn"))
        vd.append(_bdot(v2[:, sl], dup[j % 2], "nn"))
    qi, si = _iota((2 * CHUNK, 2 * CHUNK), 0) % CHUNK, _iota((2 * CHUNK, 2 * CHUNK), 1)
    valid = (si > qi) & (si <= qi + CHUNK) & jnp.logical_or(si >= CHUNK, jnp.logical_not(first_block))
    upper = _iota((2 * CHUNK, 1), 0) < CHUNK
    lanes = _iota((1, LANE), 1)
    outs = []
    for pr in range(N_PAIR):
        qr = rope(q[:, LANE * pr:LANE * (pr + 1)], cq, sq)
        lg = _bdot(jnp.concatenate([qr * left, qr * right], axis=0), kd[pr // 2], "nt") * 0.125
        lg = jnp.where(valid, lg, -jnp.inf)
        s1 = jnp.sum(jnp.where(lanes == 2 * pr, sinks, 0.0), axis=1, keepdims=True)
        s2 = jnp.sum(jnp.where(lanes == 2 * pr + 1, sinks, 0.0), axis=1, keepdims=True)
        sink = jnp.where(upper, s1, s2)
        mx = lax.stop_gradient(jnp.maximum(jnp.max(lg, axis=-1, keepdims=True), sink))
        e = jnp.exp(lg - mx)
        probs = e / (jnp.sum(e, axis=-1, keepdims=True) + jnp.exp(sink - mx))
        o2 = _bdot(probs, vd[pr // 2], "nn")
        outs.append(o2[:CHUNK] * left + o2[CHUNK:] * right)
    return jnp.concatenate(outs, axis=1)


def _attn_fwd(qkv, cos, sin, sinks, rot, name):
    t = qkv.shape[0]
    nb = t // CHUNK

    def body(q_ref, kvp_ref, kvc_ref, cq_ref, sq_ref, cp_ref, sp_ref, sinks_ref, rot_ref, o_ref):
        ck = jnp.concatenate([cp_ref[...], cq_ref[...]], axis=0)
        sk = jnp.concatenate([sp_ref[...], sq_ref[...]], axis=0)
        o_ref[...] = _attn_block(q_ref[...], kvp_ref[...], kvc_ref[...], cq_ref[...], sq_ref[...], ck, sk,
                                 sinks_ref[...], rot_ref[...], pl.program_id(0) == 0).astype(BF16)

    prev = lambda n: jnp.maximum(n - 1, 0)
    return _pallas(
        body, name=name, grid=(nb,),
        in_specs=[pl.BlockSpec((CHUNK, D), lambda n: (n, 0)),
                  pl.BlockSpec((CHUNK, 512), lambda n: (prev(n), 2)), pl.BlockSpec((CHUNK, 512), lambda n: (n, 2)),
                  pl.BlockSpec((CHUNK, LANE), lambda n: (n, 0)), pl.BlockSpec((CHUNK, LANE), lambda n: (n, 0)),
                  pl.BlockSpec((CHUNK, LANE), lambda n: (prev(n), 0)), pl.BlockSpec((CHUNK, LANE), lambda n: (prev(n), 0)),
                  _whole(sinks), _whole(rot)],
        out_specs=pl.BlockSpec((CHUNK, D), lambda n: (n, 0)), out_shape=SDS((t, D), BF16),
        compiler_params=_params(("parallel",)),
    )(qkv, qkv, qkv, cos, sin, cos, sin, sinks, rot)


def _attn_bwd(qkv, do, cos, sin, sinks, rot, name):
    t = qkv.shape[0]
    nb = t // CHUNK

    def body(q_ref, kvp_ref, kvc_ref, do_ref, cq_ref, sq_ref, cp_ref, sp_ref, sinks_ref, rot_ref,
             dq_ref, dkv_ref, dbq_ref, dbkv_ref, dsink_ref, carry_ref):
        n = pl.program_id(0)

        @pl.when(n == 0)
        def _():
            carry_ref[...] = jnp.zeros_like(carry_ref)
            dbq_ref[...] = jnp.zeros_like(dbq_ref)
            dbkv_ref[...] = jnp.zeros_like(dbkv_ref)
            dsink_ref[...] = jnp.zeros_like(dsink_ref)

        @pl.when(n < nb)
        def _():
            ck = jnp.concatenate([cp_ref[...], cq_ref[...]], axis=0)
            sk = jnp.concatenate([sp_ref[...], sq_ref[...]], axis=0)
            f = lambda q, kvp, kvc, s: _attn_block(q, kvp, kvc, cq_ref[...], sq_ref[...], ck, sk, s, rot_ref[...], n == 0)
            _, vjp = jax.vjp(f, q_ref[...], kvp_ref[...], kvc_ref[...], sinks_ref[...])
            dq, dkvp, dkvc, ds = vjp(do_ref[...].astype(F32))
            done = carry_ref[...] + dkvp
            dq_ref[...] = dq.astype(BF16)
            dkv_ref[...] = done.astype(BF16)
            dbq_ref[...] += jnp.sum(dq, axis=0, keepdims=True)
            dsink_ref[...] += ds
            carry_ref[...] = dkvc

            @pl.when(n > 0)
            def _():
                dbkv_ref[...] += jnp.sum(done, axis=0, keepdims=True)

        @pl.when(n == nb)
        def _():
            done = carry_ref[...]
            dkv_ref[...] = done.astype(BF16)
            dbkv_ref[...] += jnp.sum(done, axis=0, keepdims=True)

    cur = lambda n: jnp.minimum(n, nb - 1)
    prev = lambda n: jnp.maximum(jnp.minimum(n, nb - 1) - 1, 0)
    fin = lambda n: jnp.maximum(n - 1, 0)
    outs = _pallas(
        body, name=name, grid=(nb + 1,),
        in_specs=[pl.BlockSpec((CHUNK, D), lambda n: (cur(n), 0)),
                  pl.BlockSpec((CHUNK, 512), lambda n: (prev(n), 2)), pl.BlockSpec((CHUNK, 512), lambda n: (cur(n), 2)),
                  pl.BlockSpec((CHUNK, D), lambda n: (cur(n), 0)),
                  pl.BlockSpec((CHUNK, LANE), lambda n: (cur(n), 0)), pl.BlockSpec((CHUNK, LANE), lambda n: (cur(n), 0)),
                  pl.BlockSpec((CHUNK, LANE), lambda n: (prev(n), 0)), pl.BlockSpec((CHUNK, LANE), lambda n: (prev(n), 0)),
                  _whole(sinks), _whole(rot)],
        out_specs=[pl.BlockSpec((CHUNK, D), lambda n: (cur(n), 0)), pl.BlockSpec((CHUNK, 512), lambda n: (fin(n), 0)),
                   pl.BlockSpec((1, D), lambda n: (0, 0)), pl.BlockSpec((1, 512), lambda n: (0, 0)), _whole(sinks)],
        out_shape=[SDS((t, D), BF16), SDS((t, 512), BF16), SDS((1, D), F32), SDS((1, 512), F32), SDS(sinks.shape, F32)],
        scratch_shapes=[pltpu.VMEM((CHUNK, 512), F32)],
        compiler_params=_params(("arbitrary",)),
    )(qkv, qkv, qkv, do, cos, sin, cos, sin, sinks, rot)
    dq, dkv, dbq, dbkv, dsinks = outs
    return jnp.concatenate([dq, dkv], axis=1), jnp.concatenate([dbq, dbkv], axis=1), dsinks


def _loss_head(h, tgt, g, name, tm=512):
    t = h.shape[0]

    def body(h_ref, t_ref, g_ref, loss_ref, dh_ref, dhb_ref, dg_ref):
        def f(hv, gv):
            err = _rms_fn(hv, gv)[0] - t_ref[...]
            return 0.5 * jnp.sum(jnp.mean(err * err, axis=-1, keepdims=True), axis=0, keepdims=True)

        loss, vjp = jax.vjp(f, h_ref[...], g_ref[...])
        dh, dg = vjp(jnp.ones((1, 1), F32))
        dh_ref[...] = dh
        dhb_ref[...] = dh.astype(BF16)
        first = pl.program_id(0) == 0

        @pl.when(first)
        def _():
            loss_ref[...] = loss
            dg_ref[...] = dg

        @pl.when(jnp.logical_not(first))
        def _():
            loss_ref[...] += loss
            dg_ref[...] += dg

    return _pallas(
        body, name=name, grid=(t // tm,),
        in_specs=[_tok(D, tm), _tok(D, tm), _whole(g)],
        out_specs=[pl.BlockSpec((1, 1), lambda i: (0, 0)), _tok(D, tm), _tok(D, tm), _whole(g)],
        out_shape=[SDS((1, 1), F32), SDS((t, D), F32), SDS((t, D), BF16), SDS(g.shape, F32)],
        compiler_params=_params(("arbitrary",)),
    )(h, tgt, g)


def _res_half(acc, res):
    return (res + 0.5 * acc,)


def _res_full(acc, res):
    return (res + acc,)


def _half(acc):
    return (0.5 * acc,)


def _ffn_in(n, w_in, name, tm=1024):
    t = n.shape[0]
    tm = min(tm, t)

    def body(n_ref, w_ref, pre_ref, act_ref):
        a = n_ref[...]
        gate = lax.dot_general(a, w_ref[0], _DIMS["nt"], preferred_element_type=F32)
        up = lax.dot_general(a, w_ref[1], _DIMS["nt"], preferred_element_type=F32)
        pre_ref[0] = gate.astype(BF16)
        pre_ref[1] = up.astype(BF16)
        act_ref[...] = (gate * jax.nn.sigmoid(gate) * up).astype(BF16)

    pair = pl.BlockSpec((2, None, tm, FF_SHARD), lambda i, j: (0, j, i, 0))
    return _pallas(
        body, name=name, grid=(t // tm, 4),
        in_specs=[pl.BlockSpec((tm, D), lambda i, j: (i, 0)), pl.BlockSpec((2, None, FF_SHARD, D), lambda i, j: (0, j, 0, 0))],
        out_specs=[pair, pl.BlockSpec((None, tm, FF_SHARD), lambda i, j: (j, i, 0))],
        out_shape=[SDS((2, 4, t, FF_SHARD), BF16), SDS((4, t, FF_SHARD), BF16)],
        compiler_params=_params(("parallel", "parallel")),
    )(n, w_in.reshape(2, 4, FF_SHARD, D))


def _ffn_dact(dhb, w_out, pre, name, tm=1024, deps=()):
    t = dhb.shape[0]
    tm = min(tm, t)

    def body(d_ref, w_ref, pre_ref, *rest):
        o_ref = rest[-1]
        dact = 0.5 * lax.dot_general(d_ref[...], w_ref[...], _DIMS["nt"], preferred_element_type=F32)
        gate, up = pre_ref[0].astype(F32), pre_ref[1].astype(F32)
        sg = jax.nn.sigmoid(gate)
        o_ref[0] = (dact * up * (sg * (1.0 + gate * (1.0 - sg)))).astype(BF16)
        o_ref[1] = (dact * (gate * sg)).astype(BF16)

    pair = pl.BlockSpec((2, None, tm, FF_SHARD), lambda i, j: (0, j, i, 0))
    return _pallas(
        body, name=name, grid=(t // tm, 4),
        in_specs=[pl.BlockSpec((tm, D), lambda i, j: (i, 0)), pl.BlockSpec((None, FF_SHARD, D), lambda i, j: (j, 0, 0)), pair]
        + [ANY_SPEC] * len(deps),
        out_specs=pair, out_shape=SDS((2, 4, t, FF_SHARD), BF16),
        compiler_params=_params(("parallel", "parallel")),
    )(dhb, w_out, pre, *deps)


def _ffn_fwd(h, g, w_in, w_out, tag, deps=()):
    n = _rms(h, g, f"{tag}_rms", deps=deps)
    pre, act = _ffn_in(n, w_in, f"{tag}_in")
    w_out = w_out() if callable(w_out) else w_out
    out = _mm(act, w_out, reduce_j=True, tk=FF_SHARD, epi=_res_half, extras=(h[None],), name=f"{tag}_out")[0][0]
    return out, (h, n, pre, act)


def _ffn_bwd(dh, dhb, saved, g, w_in, w_out, tag, deps=(), hook=None, weights_hook=None):
    h, n, pre, act = saved
    t = h.shape[0]
    dpre = _ffn_dact(dhb, w_out, pre, f"{tag}_dact", deps=deps).reshape(N_DEV, t, FF_SHARD)
    dw_out = _mm(act, dhb[None], ta=True, tm=FF_SHARD, epi=_half, out_dtypes=(BF16,), deps=hook(dpre) if hook else (),
                 name=f"{tag}_dwout")[0]
    dw_in = _mm(dpre, n[None], ta=True, tm=FF_SHARD, out_dtypes=(BF16,), name=f"{tag}_dwin")[0]
    if weights_hook:
        weights_hook(dw_in, dw_out)
    dh_in, dhb_in, dg = _mm_drms(dpre, w_in, h, g, dh, f"{tag}_dn", FF_SHARD, tb=False)
    return dh_in, dhb_in, dg, dw_in, dw_out


def _ple_fwd(h, g, pb, w_gate, w_proj, tag):
    t = h.shape[0]
    tm = 512
    n = _rms(h, g, f"{tag}_rms")
    e = _mm(pb[None], w_proj[None], name=f"{tag}_proj")[0][0]
    z = _mm(n[None], w_gate[None], name=f"{tag}_gate")[0][0]
    out = _rowop(lambda zz, ee, hh: (hh + _ple_fn(zz, ee)[0],), [(z, _tok(D, tm)), (e, _tok(D, tm)), (h, _tok(D, tm))], [],
                 [((t, D), F32, _tok(D, tm))], grid=(t // tm,), name=f"{tag}_mix")[0]
    return out, (h, n, e, z)


def _ple_bwd(dh, dhb, saved, g, pb, w_gate, tag, deps=()):
    h, n, e, z = saved
    t = h.shape[0]
    tm = 512
    dz, de = _rowop_bwd(_ple_fn, [(z, _tok(D, tm)), (e, _tok(D, tm))], [], [(dh, _tok(D, tm))], [(0,), (1,)],
                        [((t, D), (BF16,), _tok(D, tm)), ((t, D), (BF16,), _tok(D, tm))], grid=(t // tm,), name=f"{tag}_dmix",
                        deps=deps)
    dw_proj = _mm(pb[None], de[None], ta=True, out_dtypes=(BF16,), name=f"{tag}_dwproj")[0][0]
    dw_gate = _mm(n[None], dz[None], ta=True, out_dtypes=(BF16,), name=f"{tag}_dwgate")[0][0]
    dh_in, dhb_in, dg = _mm_drms(dz[None], w_gate[None], h, g, dh, f"{tag}_dn", 1024)
    return dh_in, dhb_in, dg, dw_gate, dw_proj


def _hyb_fwd(h, w, tag):
    t = h.shape[0]
    tm = 512
    hn = _rms(h, w["norm_mix"], f"{tag}_rms")
    proj = _mm(hn[None], w["hyb_in"][None], tn=512, name=f"{tag}_in")[0][0]
    u1 = _dwconv([(proj, 0), (proj, D // LANE)], w["conv_w"], w["conv_b"], width=CONV_W, glu=True, silu=False, cb=LANE,
                 name=f"{tag}_conv")
    u = _rowop(_lnswish_fn, [(u1, _tok(D, tm))], [w["ln_g"], w["ln_b"]], [((t, D), BF16, _tok(D, tm))], grid=(t // tm,),
               name=f"{tag}_ln")[0]
    xa = _dwconv([(proj, 3 * D // LANE)], w["sconv_w"], w["sconv_b"], width=SSM_CONV, glu=False, silu=True, cb=LANE,
                 name=f"{tag}_sconv")
    y, states = _ssd_fwd(xa, proj, w["dt_bias"], w["a_log"], w["d_skip"], w["ssm_norm"], f"{tag}_ssd")
    mixed = jnp.stack([u, y], axis=0)
    out = _mm(mixed, w["hyb_out"], reduce_j=True, epi=_res_full, extras=(h[None],), name=f"{tag}_out")[0][0]
    return out, (h, hn, proj, u1, xa, states, mixed)


def _hyb_bwd(dh, dhb, saved, w, tag):
    h, hn, proj, u1, xa, states, mixed = saved
    t = h.shape[0]
    tm = 512
    dmix = _mm(dhb[None], w["hyb_out"], tb=True, name=f"{tag}_dmix")[0]
    dw_out = _mm(mixed, dhb[None], ta=True, out_dtypes=(BF16,), name=f"{tag}_dwout")[0]
    du1, dln_g, dln_b = _rowop_bwd(_lnswish_fn, [(u1, _tok(D, tm))], [w["ln_g"], w["ln_b"]], [(dmix[0], _tok(D, tm))], [(0,)],
                                   [((t, D), (F32,), _tok(D, tm))], grid=(t // tm,), name=f"{tag}_dln")
    dval, dgate, dconv_w, dconv_b = _dwconv_bwd([(proj, 0), (proj, D // LANE)], w["conv_w"], w["conv_b"], du1,
                                                width=CONV_W, glu=True, silu=False, cb=LANE, name=f"{tag}_dconv")
    dxa, ddt, dz, g_dtb, g_alog, g_dsk, g_ng = _ssd_bwd(xa, proj, states, dmix[1], w["dt_bias"], w["a_log"], w["d_skip"],
                                                         w["ssm_norm"], f"{tag}_dssd")
    dxbc, dsconv_w, dsconv_b = _dwconv_bwd([(proj, 3 * D // LANE)], w["sconv_w"], w["sconv_b"], dxa, width=SSM_CONV,
                                           glu=False, silu=True, cb=LANE, name=f"{tag}_dsconv")
    dproj = jnp.concatenate([dval, dgate, dz, dxbc, ddt, jnp.zeros((t, HYB_PAD - DT_COL - LANE), BF16)], axis=1)
    dh_in, dhb_in, dg = _mm_drms(dproj[None], w["hyb_in"][None], h, w["norm_mix"], dh, f"{tag}_dhn", 1024)
    dw_in = _mm(hn[None], dproj[None], ta=True, tn=512, out_dtypes=(BF16,), name=f"{tag}_dwin")[0][0]
    grads = dict(norm_mix=dg, hyb_in=dw_in, hyb_out=dw_out, conv_w=dconv_w, conv_b=dconv_b, ln_g=dln_g, ln_b=dln_b,
                 sconv_w=dsconv_w, sconv_b=dsconv_b, dt_bias=g_dtb, a_log=g_alog, d_skip=g_dsk, ssm_norm=g_ng)
    return dh_in, dhb_in, grads


def _bias_epi(acc, row):
    return (acc + row,)


def _res_bias_epi(acc, res, row):
    return (res + acc + row,)


def _att_fwd(h, w, tables, tag):
    cos, sin, rot = tables
    hn = _rms(h, w["norm_mix"], f"{tag}_rms")
    qkv = _mm(hn[None], w["qkv"][None], tb=True, tn=512, epi=_bias_epi, rows=(w["b_qkv"],), name=f"{tag}_qkv")[0][0]
    o = _attn_fwd(qkv, cos, sin, w["sinks"], rot, f"{tag}_core")
    out = _mm(o[None], w["w_o"][None], epi=_res_bias_epi, extras=(h[None],), rows=(w["b_o"],), name=f"{tag}_out")[0][0]
    return out, (h, hn, qkv, o)


def _att_bwd(dh, dhb, saved, w, tables, tag):
    cos, sin, rot = tables
    h, hn, qkv, o = saved
    t = h.shape[0]
    tm = 512
    do = _mm(dhb[None], w["w_o"][None], tb=True, out_dtypes=(BF16,), name=f"{tag}_do")[0][0]
    dw_o = _mm(o[None], dhb[None], ta=True, out_dtypes=(BF16,), name=f"{tag}_dwo")[0][0]
    db_o = _rowop_bwd(lambda xx, bb: (xx + bb,), [(dh, _tok(D, tm))], [w["b_o"]], [(dh, _tok(D, tm))], [], [],
                      grid=(t // tm,), name=f"{tag}_dbo")[0]
    dqkv, db_qkv, dsinks = _attn_bwd(qkv, do, cos, sin, w["sinks"], rot, f"{tag}_dcore")
    dh_in, dhb_in, dg = _mm_drms(dqkv[None], w["qkv"][None], h, w["norm_mix"], dh, f"{tag}_dhn", 512, tb=False)
    dw_qkv = _mm(dqkv[None], hn[None], ta=True, tm=512, out_dtypes=(BF16,), name=f"{tag}_dwqkv")[0][0]
    grads = dict(norm_mix=dg, qkv=dw_qkv, b_qkv=db_qkv, sinks=dsinks, w_o=dw_o, b_o=db_o)
    return dh_in, dhb_in, grads


def _rope_tables(t):
    inv = ROPE_THETA ** (-jnp.arange(0, 64, 2, dtype=F32) / 64)
    ang = jnp.arange(t, dtype=F32)[:, None] * inv[None, :]
    cos, sin = jnp.tile(jnp.cos(ang), (1, 4)), jnp.tile(jnp.sin(ang), (1, 4))
    rot = np.zeros((LANE, LANE), np.float32)
    for j in range(LANE):
        if j % 64 < 32:
            rot[j + 32, j] = -1.0
        else:
            rot[j - 32, j] = 1.0
    return cos, sin, jnp.asarray(rot)


def _local_step(x, p, tgt, layers, final_norm):
    _restart_chain()
    tables = _rope_tables(x.shape[0])
    pb = p.astype(BF16)
    h, saved = x, []
    for i, w in enumerate(layers):
        h, s = _layer_fwd(i, h, w, pb[i], tables)
        saved.append(s)
    loss, dh, dhb, d_final = _loss_head(h, tgt, final_norm, "loss_head")
    grads = [None] * len(layers)
    for i in reversed(range(len(layers))):
        dh, dhb, head = _layer_bwd_head(i, dh, dhb, saved[i], layers[i], pb[i])
        dh, dhb, tail = _layer_bwd_tail(i, dh, dhb, saved[i], layers[i], tables)
        grads[i] = {**head, **tail}
    return loss[0, 0], dh, grads, d_final


def _layer_fwd(i, h, w, pb, tables, deps=()):
    s = {}
    h, s["ffn1"] = _ffn_fwd(h, w["norm_ffn1"], w["ffn1_in"], w["ffn1_out"], f"l{i}_ffn1", deps=deps)
    if i % 2 == 0:
        h, s["mix"] = _hyb_fwd(h, w, f"l{i}_hyb")
    else:
        h, s["mix"] = _att_fwd(h, w, tables, f"l{i}_att")
    h, s["ffn2"] = _ffn_fwd(h, w["norm_ffn2"], w["ffn2_in"], w["ffn2_out"], f"l{i}_ffn2")
    h, s["ple"] = _ple_fwd(h, w["ple_norm"], pb, w["ple_gate"], w["ple_proj"], f"l{i}_ple")
    return h, s


def _layer_bwd_head(i, dh, dhb, s, w, pb, deps=()):
    g = {}
    dh, dhb, g["ple_norm"], g["ple_gate"], g["ple_proj"] = _ple_bwd(dh, dhb, s["ple"], w["ple_norm"], pb, w["ple_gate"],
                                                                    f"l{i}_ple", deps=deps)
    return dh, dhb, g


def _layer_bwd_tail(i, dh, dhb, s, w, tables, deps=()):
    g = {}
    dh, dhb, g["norm_ffn2"], g["ffn2_in"], g["ffn2_out"] = _ffn_bwd(dh, dhb, s["ffn2"], w["norm_ffn2"], w["ffn2_in"],
                                                                    w["ffn2_out"], f"l{i}_ffn2", deps=deps)
    if i % 2 == 0:
        dh, dhb, gm = _hyb_bwd(dh, dhb, s["mix"], w, f"l{i}_hyb")
    else:
        dh, dhb, gm = _att_bwd(dh, dhb, s["mix"], w, tables, f"l{i}_att")
    g.update(gm)
    dh, dhb, g["norm_ffn1"], g["ffn1_in"], g["ffn1_out"] = _ffn_bwd(dh, dhb, s["ffn1"], w["norm_ffn1"], w["ffn1_in"],
                                                                    w["ffn1_out"], f"l{i}_ffn1")
    return dh, dhb, g


def _cols(g):
    full = jnp.moveaxis(g, 0, -2)
    return full.reshape(*full.shape[:-2], N_DEV * g.shape[-1])


def _uncols(full):
    split = full.reshape(*full.shape[:-1], N_DEV, full.shape[-1] // N_DEV)
    return jnp.moveaxis(split, -2, 0)


def _lane_pad(v):
    return jnp.pad(v, ((0, 0), (0, LANE - v.shape[1])))


def _build_layers(gw, gs, rep):
    return [_build_layer(i, gw, gs, rep) for i in range(2)]


def _build_layer(i, gw, gs, rep, parts=("ffn1", "mix", "ffn2", "ple")):
    w = {}
    for f in ("ffn1", "ffn2"):
        if f in parts:
            w[f"norm_{f}"] = rep[f"norm_{f}"][i][None]
            w[f"{f}_in"] = gw[f"{f}_w_in", i]
            w[f"{f}_out"] = gw[f"{f}_w_out", i].reshape(4, FF_SHARD, D)
    if "ple" in parts:
        w["ple_norm"] = rep["ple_norm"][i][None]
        w["ple_gate"] = gw["ple_gate_w", i].reshape(D, D)
        w["ple_proj"] = _cols(gw["ple_proj_w", i])
    if "mix" not in parts:
        return w
    w["norm_mix"] = rep["norm_mix"][i][None]
    if i == 0:
        w["hyb_in"] = jnp.pad(_cols(gw["hyb_w_in", 0]), ((0, 0), (0, HYB_PAD - HYB_IN)))
        w["hyb_out"] = gw["hyb_w_out", 0].reshape(2, D, D)
        w["conv_w"] = _cols(gs["conv_dw_w"][:, 0])
        w["sconv_w"] = _cols(gs["ssm_conv_w"][:, 0])
        w["conv_b"], w["ln_g"], w["ln_b"] = rep["conv_dw_b"], rep["conv_ln_g"], rep["conv_ln_b"]
        w["sconv_b"], w["ssm_norm"] = rep["ssm_conv_b"], rep["ssm_norm"]
        w["dt_bias"], w["a_log"] = _lane_pad(rep["ssm_dt_bias"]), _lane_pad(rep["ssm_a_log"])
        w["d_skip"] = jnp.repeat(rep["ssm_d"], D // SSM_HEADS, axis=1)
    else:
        w["qkv"] = gw["att_w_qkv", 0].reshape(-1, D)
        w["w_o"] = gw["att_w_o", 0].reshape(D, D)
        w["b_qkv"] = gs["att_b_qkv"][:, 0].reshape(1, -1)
        w["b_o"] = gs["att_b_o"][:, 0].reshape(1, -1)
        w["sinks"] = _lane_pad(rep["att_sinks"])
    return w


def _big_grads(i, g):
    big = {}
    for f in ("ffn1", "ffn2"):
        if f"{f}_in" in g:
            big[f"{f}_w_in", i] = g[f"{f}_in"]
            big[f"{f}_w_out", i] = g[f"{f}_out"].reshape(N_DEV, D_FF // N_DEV, D)
    if "ple_gate" in g:
        big["ple_gate_w", i] = g["ple_gate"].reshape(N_DEV, D // N_DEV, D)
        big["ple_proj_w", i] = _uncols(g["ple_proj"])
    if "hyb_in" in g:
        big["hyb_w_in", 0] = _uncols(g["hyb_in"][:, :HYB_IN])
        big["hyb_w_out", 0] = g["hyb_out"].reshape(N_DEV, 2 * D // N_DEV, D)
    if "qkv" in g:
        big["att_w_qkv", 0] = g["qkv"].reshape(N_DEV, -1, D)
        big["att_w_o", 0] = g["w_o"].reshape(N_DEV, D // N_DEV, D)
    return big


def _collect_grads(grads, d_final):
    g0, g1 = grads
    big, small = {**_big_grads(0, g0), **_big_grads(1, g1)}, {}
    for f in ("ffn1", "ffn2"):
        small[f"norm_{f}"] = jnp.concatenate([g[f"norm_{f}"] for g in grads], axis=0)
    small["norm_mix"] = jnp.concatenate([g["norm_mix"] for g in grads], axis=0)
    small["ple_norm"] = jnp.concatenate([g["ple_norm"] for g in grads], axis=0)
    small["conv_dw_w"] = g0["conv_w"][None]
    small["conv_dw_b"], small["conv_ln_g"], small["conv_ln_b"] = g0["conv_b"], g0["ln_g"], g0["ln_b"]
    small["ssm_conv_w"] = g0["sconv_w"][None]
    small["ssm_conv_b"], small["ssm_norm"] = g0["sconv_b"], g0["ssm_norm"]
    small["ssm_dt_bias"], small["ssm_a_log"] = g0["dt_bias"][:, :SSM_HEADS], g0["a_log"][:, :SSM_HEADS]
    small["ssm_d"] = g0["d_skip"].reshape(1, SSM_HEADS, D // SSM_HEADS).sum(axis=-1)
    small["att_b_qkv"], small["att_b_o"] = g1["b_qkv"], g1["b_o"]
    small["att_sinks"] = g1["sinks"][:, :SSM_HEADS]
    small["final_norm"] = d_final[0]
    return big, small


MESH = pl.DeviceIdType.MESH


def _place():
    return lax.axis_index("x"), lax.axis_index("y"), lax.axis_index("c")


def _all_gather(blocks, space, name):
    nb = len(blocks)

    def body(*refs):
        x_refs, out_refs, (send_sems, recv_sems, local_sem) = refs[:nb], refs[nb:2 * nb], refs[2 * nb:]
        x, y, c = _place()
        me, sibling = (x, y, c), (x, y, 1 - c)
        chips = [(1 - x, y), (x, 1 - y), (1 - x, 1 - y)]

        def copies(k, blk, to, own=False):
            idx = 4 * blk[0] + 2 * blk[1] + blk[2]
            return [pltpu.make_async_remote_copy(src_ref=x_ref if own else out_ref.at[idx], dst_ref=out_ref.at[idx],
                                                 send_sem=send_sems.at[k, b], recv_sem=recv_sems.at[k, b], device_id=to,
                                                 device_id_type=MESH) for b, (x_ref, out_ref) in enumerate(zip(x_refs, out_refs))]

        mine = [pltpu.make_async_copy(x_ref, out_ref.at[4 * x + 2 * y + c], local_sem.at[b])
                for b, (x_ref, out_ref) in enumerate(zip(x_refs, out_refs))]
        first = copies(0, me, sibling, own=True)
        for j, chip in enumerate(chips):
            first += copies(1 + j, me, (*chip, c), own=True)
        for cp in mine + first:
            cp.start()
        passed = []
        for j, chip in enumerate(chips):
            for cp in copies(1 + j, (*chip, c), me):
                cp.wait_recv()
            onward = copies(4 + j, (*chip, c), sibling)
            for cp in onward:
                cp.start()
            passed += onward
        for cp in copies(0, sibling, me):
            cp.wait_recv()
        for j, chip in enumerate(chips):
            for cp in copies(4 + j, (*chip, 1 - c), me):
                cp.wait_recv()
        for cp in first + passed:
            cp.wait_send()
        for cp in mine:
            cp.wait()

    spec = pl.BlockSpec(memory_space=space)
    return _pallas(
        body, name=name, out_shape=[SDS((N_DEV,) + b.shape, b.dtype) for b in blocks],
        in_specs=[spec] * nb, out_specs=[spec] * nb,
        scratch_shapes=[pltpu.SemaphoreType.DMA((7, nb)), pltpu.SemaphoreType.DMA((7, nb)), pltpu.SemaphoreType.DMA((nb,))],
    )(*blocks)


def _pair_exchange(parts, name):
    nb = len(parts)

    def body(*refs):
        p_refs, got_refs, (send_sems, recv_sems) = refs[:nb], refs[nb:2 * nb], refs[2 * nb:]
        x, y, c = _place()
        copies = [pltpu.make_async_remote_copy(src_ref=p_ref.at[2 * q + (1 - c)], dst_ref=got_ref.at[q],
                                               send_sem=send_sems.at[q, b], recv_sem=recv_sems.at[q, b], device_id=(x, y, 1 - c),
                                               device_id_type=MESH)
                  for q in range(4) for b, (p_ref, got_ref) in enumerate(zip(p_refs, got_refs))]
        for cp in copies:
            cp.start()
        for cp in copies:
            cp.wait_recv()
        for cp in copies:
            cp.wait_send()

    hbm = pl.BlockSpec(memory_space=pltpu.HBM)
    return _pallas(
        body, name=name, out_shape=[SDS((4,) + p.shape[1:], p.dtype) for p in parts], in_specs=[hbm] * nb, out_specs=[hbm] * nb,
        scratch_shapes=[pltpu.SemaphoreType.DMA((4, nb)), pltpu.SemaphoreType.DMA((4, nb))],
    )(*parts)


HBM_SPEC = pl.BlockSpec(memory_space=pltpu.HBM)
SEM_SPEC = pl.BlockSpec(memory_space=pltpu.SEMAPHORE)
EFFECT = pltpu.SideEffectType.DATAFLOW_SIDE_EFFECTING


def _plan_descriptors(plan, srcs, lands, send_sems, recv_sems, local_sems, arriving):
    remote, local = plan(*_place())
    pick = lambda ref, slot: ref if slot is None else ref.at[slot]
    rem = [pltpu.make_async_remote_copy(src_ref=pick(srcs[si], ss), dst_ref=lands[li].at[rs if arriving else ds],
                                        send_sem=send_sems.at[k], recv_sem=recv_sems.at[k], device_id=dev, device_id_type=MESH)
           for k, (si, ss, li, ds, dev, rs) in enumerate(remote)]
    loc = [pltpu.make_async_copy(pick(srcs[si], ss), lands[li].at[ds], local_sems.at[k])
           for k, (si, ss, li, ds) in enumerate(local)]
    return rem, loc


def _plan_counts(plan):
    remote, local = plan(0, 0, 0)
    return len(remote), max(len(local), 1)


def _exchange_start(srcs, land_shapes, plan, name):
    ns, nl = len(srcs), len(land_shapes)
    n_remote, n_local = _plan_counts(plan)
    lands = [pltpu.with_memory_space_constraint(lax.empty(s.shape, s.dtype), pltpu.HBM) for s in land_shapes]
    srcs = [pltpu.with_memory_space_constraint(s, pltpu.HBM) for s in srcs]

    def body(*refs):
        src_refs, land_refs = refs[:ns], refs[ns:ns + nl]
        send_sems, recv_sems, local_sems = refs[ns + nl:ns + nl + 3]
        token = refs[-1]
        rem, loc = _plan_descriptors(plan, src_refs, land_refs, send_sems, recv_sems, local_sems, arriving=False)
        for cp in loc + rem:
            cp.start()
        token[...] = jnp.zeros_like(token)

    outs = _pallas(
        body, name=name,
        out_shape=[pltpu.SemaphoreType.DMA((n_remote,)), pltpu.SemaphoreType.DMA((n_remote,)), pltpu.SemaphoreType.DMA((n_local,))]
        + [pltpu.HBM(a.shape, a.dtype) for a in srcs + lands] + [SDS((8, LANE), F32)],
        in_specs=[HBM_SPEC] * (ns + nl),
        out_specs=[SEM_SPEC] * 3 + [HBM_SPEC] * (ns + nl) + [pl.BlockSpec(memory_space=pltpu.VMEM)],
        input_output_aliases={i: 3 + i for i in range(ns + nl)},
        compiler_params=pltpu.CompilerParams(has_side_effects=EFFECT),
    )(*srcs, *lands)
    return (outs[:3], outs[3:3 + ns], outs[3 + ns:3 + ns + nl]), outs[-1]


def _exchange_wait(state, after, plan, name):
    sems, srcs, lands = state
    ns, nl = len(srcs), len(lands)

    def body(*refs):
        src_refs, land_refs = refs[:ns], refs[ns:ns + nl]
        send_sems, recv_sems, local_sems = refs[ns + nl:ns + nl + 3]
        rem, loc = _plan_descriptors(plan, src_refs, land_refs, send_sems, recv_sems, local_sems, arriving=True)
        for cp in rem:
            cp.wait_send()
            cp.wait_recv()
        for cp in loc:
            cp.wait()

    outs = _pallas(
        body, name=name, out_shape=[pltpu.HBM(a.shape, a.dtype) for a in list(srcs) + list(lands)],
        in_specs=[HBM_SPEC] * (ns + nl) + [SEM_SPEC] * 3 + [ANY_SPEC] * (after is not None), out_specs=[HBM_SPEC] * (ns + nl),
        input_output_aliases={i: i for i in range(ns + nl)},
        compiler_params=pltpu.CompilerParams(has_side_effects=EFFECT),
    )(*srcs, *lands, *sems, *([after] if after is not None else []))
    return outs[:ns], outs[ns:]


def _gather_plan(nb):
    def plan(x, y, c):
        me = 4 * x + 2 * y + c
        remote = []
        for b in range(nb):
            for r in range(1, N_DEV):
                tx, ty, tc = (1 - x if r & 4 else x), (1 - y if r & 2 else y), (1 - c if r & 1 else c)
                remote.append((b, None, b, me, (tx, ty, tc), 4 * tx + 2 * ty + tc))
        return remote, [(b, None, b, me) for b in range(nb)]
    return plan


def _pair_plan(nb):
    def plan(x, y, c):
        return [(b, 2 * q + (1 - c), b, q, (x, y, 1 - c), q) for b in range(nb) for q in range(4)], []
    return plan


def _chip_plan(nb):
    def plan(x, y, c):
        own = 2 * x + y
        chips = [(1 - x, y), (x, 1 - y), (1 - x, 1 - y)]
        remote = [(b, 2 * cx + cy, b, own, (cx, cy, c), 2 * cx + cy) for b in range(nb) for cx, cy in chips]
        return remote, [(b, own, b, own) for b in range(nb)]
    return plan


def _row_tile(r, cap=4608):
    return max(d for d in range(16, min(r, cap) + 1, 16) if r % d == 0)


def _pair_add(parts, got, core, name):
    _, r, cdim = parts.shape
    tr = _row_tile(r)

    def body(core_ref, p_ref, g_ref, o_ref):
        o_ref[...] = (p_ref[...].astype(F32) + g_ref[...].astype(F32)).astype(o_ref.dtype)

    return pl.pallas_call(
        body, name=name, out_shape=SDS((4, r, cdim), BF16),
        grid_spec=pltpu.PrefetchScalarGridSpec(
            num_scalar_prefetch=1, grid=(4, r // tr),
            in_specs=[pl.BlockSpec((None, tr, cdim), lambda q, i, core_ref: (2 * q + core_ref[0], i, 0)),
                      pl.BlockSpec((None, tr, cdim), lambda q, i, core_ref: (q, i, 0))],
            out_specs=pl.BlockSpec((None, tr, cdim), lambda q, i, core_ref: (q, i, 0))),
        compiler_params=_params(("parallel", "parallel")),
    )(core, parts, got)


def _chip_exchange(sums, name):
    nb = len(sums)

    def body(*refs):
        b_refs, out_refs, (send_sems, recv_sems, local_sem) = refs[:nb], refs[nb:2 * nb], refs[2 * nb:]
        x, y, c = _place()
        own = 2 * x + y
        chips = [(1 - x, y), (x, 1 - y), (1 - x, 1 - y)]

        def copies(k, chip, src_slot, dst_slot):
            return [pltpu.make_async_remote_copy(src_ref=b_ref.at[src_slot], dst_ref=out_ref.at[dst_slot],
                                                 send_sem=send_sems.at[k, b], recv_sem=recv_sems.at[k, b], device_id=(*chip, c),
                                                 device_id_type=MESH) for b, (b_ref, out_ref) in enumerate(zip(b_refs, out_refs))]

        mine = [pltpu.make_async_copy(b_ref.at[own], out_ref.at[own], local_sem.at[b])
                for b, (b_ref, out_ref) in enumerate(zip(b_refs, out_refs))]
        sends = []
        for k, chip in enumerate(chips):
            sends += copies(k, chip, 2 * chip[0] + chip[1], own)
        for cp in mine + sends:
            cp.start()
        for k, chip in enumerate(chips):
            for cp in copies(k, chip, own, 2 * chip[0] + chip[1]):
                cp.wait_recv()
        for cp in sends:
            cp.wait_send()
        for cp in mine:
            cp.wait()

    hbm = pl.BlockSpec(memory_space=pltpu.HBM)
    return _pallas(
        body, name=name, out_shape=[SDS(s.shape, s.dtype) for s in sums], in_specs=[hbm] * nb, out_specs=[hbm] * nb,
        scratch_shapes=[pltpu.SemaphoreType.DMA((3, nb)), pltpu.SemaphoreType.DMA((3, nb)), pltpu.SemaphoreType.DMA((nb,))],
    )(*sums)


def _sum_slots(parts, name):
    nj, r, cdim = parts.shape
    tr = _row_tile(r)

    def body(p_ref, o_ref):
        acc = p_ref[0].astype(F32)
        for j in range(1, nj):
            acc = acc + p_ref[j].astype(F32)
        o_ref[...] = acc

    return _pallas(
        body, name=name, out_shape=SDS((r, cdim), F32), grid=(r // tr,),
        in_specs=[pl.BlockSpec((nj, tr, cdim), lambda i: (0, i, 0))], out_specs=pl.BlockSpec((tr, cdim), lambda i: (i, 0)),
        compiler_params=_params(("parallel",)),
    )(parts)


def _adamw_update(wv, gv, mv, vv):
    nm = ADAM_B1 * mv + (1.0 - ADAM_B1) * gv
    nv = ADAM_B2 * vv + (1.0 - ADAM_B2) * (gv * gv)
    m_hat = nm / (1.0 - ADAM_B1 ** ADAM_STEP)
    v_hat = nv / (1.0 - ADAM_B2 ** ADAM_STEP)
    return -ADAM_LR * (m_hat / (jnp.sqrt(v_hat) + ADAM_EPS) + ADAM_WD * wv), nm, nv


def _adamw_summed(w, m, v, by_chip, name):
    nl, r, cdim = w.shape
    tr = _row_tile(r, 512)
    nblk = r // tr

    def body(*refs):
        chip_refs, (w_ref, m_ref, v_ref, g_ref, d_ref, nm_ref, nv_ref) = refs[:nl], refs[nl:]
        layer = pl.program_id(0)
        gv = None
        for ll, c_ref in enumerate(chip_refs):
            s = c_ref[0].astype(F32)
            for q in range(1, 4):
                s = s + c_ref[q].astype(F32)
            gv = s if gv is None else jnp.where(layer == ll, s, gv)
        g_ref[...] = gv
        d_ref[...], nm_ref[...], nv_ref[...] = _adamw_update(w_ref[...], gv, m_ref[...], v_ref[...])

    def chip_map(ll):
        return lambda l, i: (0, jnp.where(l == ll, i, jnp.where(l > ll, nblk - 1, 0)), 0)

    spec = pl.BlockSpec((None, tr, cdim), lambda l, i: (l, i, 0))
    return _pallas(
        body, name=name, grid=(nl, nblk),
        in_specs=[pl.BlockSpec((4, tr, cdim), chip_map(ll)) for ll in range(nl)] + [spec] * 3,
        out_specs=[spec] * 4, out_shape=[SDS((nl, r, cdim), F32)] * 4,
        compiler_params=_params(("arbitrary", "arbitrary")),
    )(*by_chip, w, m, v)


def _adamw(w, g, m, v, name):
    shape = w.shape
    cdim = shape[-1]
    w2, g2, m2, v2 = (a.reshape(-1, cdim) for a in (w, g, m, v))
    r = w2.shape[0]
    tr = next(d for d in (512, 352, 256, 128, 64, 32, 16, 8, r) if r % d == 0)

    def body(w_ref, g_ref, m_ref, v_ref, d_ref, nm_ref, nv_ref):
        d_ref[...], nm_ref[...], nv_ref[...] = _adamw_update(w_ref[...], g_ref[...], m_ref[...], v_ref[...])

    spec = pl.BlockSpec((tr, cdim), lambda i: (i, 0))
    outs = _pallas(
        body, name=name, grid=(r // tr,), in_specs=[spec] * 4, out_specs=[spec] * 3, out_shape=[SDS((r, cdim), F32)] * 3,
        compiler_params=_params(("parallel",)),
    )(w2, g2, m2, v2)
    return tuple(o.reshape(shape) for o in outs)


WEIGHTS = ("norm_ffn1", "ffn1_w_in", "ffn1_w_out", "norm_mix", "norm_ffn2", "ffn2_w_in", "ffn2_w_out", "ple_norm", "ple_gate_w",
           "ple_proj_w", "hyb_w_in", "conv_dw_w", "conv_dw_b", "conv_ln_g", "conv_ln_b", "ssm_conv_w", "ssm_conv_b", "ssm_dt_bias",
           "ssm_a_log", "ssm_d", "ssm_norm", "hyb_w_out", "att_w_qkv", "att_b_qkv", "att_sinks", "att_w_o", "att_b_o", "final_norm")
BIG = ("ffn1_w_in", "ffn1_w_out", "ffn2_w_in", "ffn2_w_out", "ple_gate_w", "ple_proj_w", "hyb_w_in", "hyb_w_out", "att_w_qkv",
       "att_w_o")
SMALL_SHARDED = {"conv_dw_w": 2, "ssm_conv_w": 2, "att_b_qkv": 1, "att_b_o": 1}
SMALL = tuple(n for n in WEIGHTS if n not in BIG)
TRANSPOSED = ("ffn1_w_in", "ffn2_w_in", "att_w_qkv")
PACK_ROWS = 16


def _pack(arrays, lead=0):
    pieces = []
    for a in arrays:
        flat = a.reshape(*a.shape[:lead], -1)
        size = flat.shape[-1]
        padded = -(-size // (PACK_ROWS * LANE)) * PACK_ROWS * LANE
        flat = jnp.pad(flat, [(0, 0)] * lead + [(0, padded - size)])
        pieces.append(flat.reshape(*a.shape[:lead], padded // LANE, LANE))
    return jnp.concatenate(pieces, axis=lead)


def _unpack(buf, shapes, lead=0):
    out, row = [], 0
    for shape in shapes:
        size = math.prod(shape)
        rows = -(-size // (PACK_ROWS * LANE)) * PACK_ROWS
        piece = lax.slice_in_dim(buf, row, row + rows, axis=lead)
        piece = piece.reshape(*buf.shape[:lead], rows * LANE)
        out.append(lax.slice_in_dim(piece, 0, size, axis=lead).reshape(*buf.shape[:lead], *shape))
        row += rows
    return out


def kernel(x, p, norm_ffn1, ffn1_w_in, ffn1_w_out, norm_mix, norm_ffn2, ffn2_w_in, ffn2_w_out, ple_norm, ple_gate_w, ple_proj_w, hyb_w_in, conv_dw_w, conv_dw_b, conv_ln_g, conv_ln_b, ssm_conv_w, ssm_conv_b, ssm_dt_bias, ssm_a_log, ssm_d, ssm_norm, hyb_w_out, att_w_qkv, att_b_qkv, att_sinks, att_w_o, att_b_o, final_norm, loss_target, m_norm_ffn1, m_ffn1_w_in, m_ffn1_w_out, m_norm_mix, m_norm_ffn2, m_ffn2_w_in, m_ffn2_w_out, m_ple_norm, m_ple_gate_w, m_ple_proj_w, m_hyb_w_in, m_conv_dw_w, m_conv_dw_b, m_conv_ln_g, m_conv_ln_b, m_ssm_conv_w, m_ssm_conv_b, m_ssm_dt_bias, m_ssm_a_log, m_ssm_d, m_ssm_norm, m_hyb_w_out, m_att_w_qkv, m_att_b_qkv, m_att_sinks, m_att_w_o, m_att_b_o, m_final_norm, v_norm_ffn1, v_ffn1_w_in, v_ffn1_w_out, v_norm_mix, v_norm_ffn2, v_ffn2_w_in, v_ffn2_w_out, v_ple_norm, v_ple_gate_w, v_ple_proj_w, v_hyb_w_in, v_conv_dw_w, v_conv_dw_b, v_conv_ln_g, v_conv_ln_b, v_ssm_conv_w, v_ssm_conv_b, v_ssm_dt_bias, v_ssm_a_log, v_ssm_d, v_ssm_norm, v_hyb_w_out, v_att_w_qkv, v_att_b_qkv, v_att_sinks, v_att_w_o, v_att_b_o, v_final_norm):
    args = (norm_ffn1, ffn1_w_in, ffn1_w_out, norm_mix, norm_ffn2, ffn2_w_in, ffn2_w_out, ple_norm, ple_gate_w, ple_proj_w, hyb_w_in, conv_dw_w, conv_dw_b, conv_ln_g, conv_ln_b, ssm_conv_w, ssm_conv_b, ssm_dt_bias, ssm_a_log, ssm_d, ssm_norm, hyb_w_out, att_w_qkv, att_b_qkv, att_sinks, att_w_o, att_b_o, final_norm)
    moments_m = (m_norm_ffn1, m_ffn1_w_in, m_ffn1_w_out, m_norm_mix, m_norm_ffn2, m_ffn2_w_in, m_ffn2_w_out, m_ple_norm, m_ple_gate_w, m_ple_proj_w, m_hyb_w_in, m_conv_dw_w, m_conv_dw_b, m_conv_ln_g, m_conv_ln_b, m_ssm_conv_w, m_ssm_conv_b, m_ssm_dt_bias, m_ssm_a_log, m_ssm_d, m_ssm_norm, m_hyb_w_out, m_att_w_qkv, m_att_b_qkv, m_att_sinks, m_att_w_o, m_att_b_o, m_final_norm)
    moments_v = (v_norm_ffn1, v_ffn1_w_in, v_ffn1_w_out, v_norm_mix, v_norm_ffn2, v_ffn2_w_in, v_ffn2_w_out, v_ple_norm, v_ple_gate_w, v_ple_proj_w, v_hyb_w_in, v_conv_dw_w, v_conv_dw_b, v_conv_ln_g, v_conv_ln_b, v_ssm_conv_w, v_ssm_conv_b, v_ssm_dt_bias, v_ssm_a_log, v_ssm_d, v_ssm_norm, v_hyb_w_out, v_att_w_qkv, v_att_b_qkv, v_att_sinks, v_att_w_o, v_att_b_o, v_final_norm)
    w = dict(zip(WEIGHTS, args))
    m = dict(zip(WEIGHTS, moments_m))
    v = dict(zip(WEIGHTS, moments_v))
    cx, cy, cc = _place()
    me = 4 * cx + 2 * cy + cc

    core = jnp.reshape(cc, (1,)).astype(jnp.int32)
    layer_of = lambda n, i: 1 if n.startswith("att_") else i
    keys = [[(n, i) for n in BIG for i in range(w[n].shape[0]) if layer_of(n, i) == layer] for layer in range(2)]

    first = [key for key in keys[0] if key[0].startswith("ffn1")]
    mixer = [key for key in keys[0] if key[0].startswith("hyb")]
    rest0 = [key for key in keys[0] if key not in first + mixer]
    gw, by_chip = {}, {}
    view = lambda a, n: jnp.swapaxes(a, 1, 2) if n in TRANSPOSED else a
    block = lambda n, i: view(w[n], n)[i].astype(BF16)

    def gather_later(group, name):
        blocks = [block(n, i) for n, i in group]
        plan = _gather_plan(len(blocks))
        state, token = _exchange_start(blocks, [SDS((N_DEV,) + b.shape, BF16) for b in blocks], plan, f"{name}_start")
        return token, lambda after: gw.update(zip(group, _exchange_wait(state, after, plan, f"{name}_wait")[1]))

    def reduce_later(group, big, name):
        pair_plan, chip_plan = _pair_plan(len(group)), _chip_plan(len(group))
        parts = [big[key] for key in group]
        pair, token = _exchange_start(parts, [SDS((4,) + pt.shape[1:], BF16) for pt in parts], pair_plan, f"{name}_pair_start")
        stage = {}

        def middle(after):
            thru, got = _exchange_wait(pair, after, pair_plan, f"{name}_pair_wait")
            sums = [_pair_add(pt, gt, core, f"grads_pair_add_{n}_{i}") for pt, gt, (n, i) in zip(thru, got, group)]
            stage["chip"], chip_token = _exchange_start(sums, [SDS(s.shape, BF16) for s in sums], chip_plan, f"{name}_chip_start")
            return chip_token

        def finish(after):
            by_chip.update(zip(group, _exchange_wait(stage["chip"], after, chip_plan, f"{name}_chip_wait")[1]))

        return token, middle, finish

    _restart_chain()
    gw["ffn1_w_in", 0], gathered_small = _all_gather([block("ffn1_w_in", 0), _pack([w[n] for n in SMALL_SHARDED])],
                                                      pltpu.HBM, "gather_weights_first")
    early_token, early_arrived = gather_later([("ffn1_w_out", 0)], "gather_weights_early")
    mixer_token, mixer_arrived = gather_later(mixer, "gather_weights_mixer")
    rest0_token, rest0_arrived = gather_later(rest0, "gather_weights_rest")
    layer1_token, layer1_arrived = gather_later(keys[1], "gather_weights_l1")
    gs = dict(zip(SMALL_SHARDED, _unpack(gathered_small, [w[n].shape for n in SMALL_SHARDED], lead=1)))
    rep = {n: w[n] for n in SMALL if n not in SMALL_SHARDED}

    tables = _rope_tables(x.shape[1])
    pb = p[:, 0].astype(BF16)
    w0, s0 = {}, {}

    def first_w_out():
        early_arrived(None)
        w0.update(_build_layer(0, gw, gs, rep, parts=("ffn1",)))
        return w0["ffn1_out"]

    h, s0["ffn1"] = _ffn_fwd(x[0], rep["norm_ffn1"][0][None], gw["ffn1_w_in", 0], first_w_out, "l0_ffn1")
    mixer_arrived(h)
    w0.update(_build_layer(0, gw, gs, rep, parts=("mix",)))
    h, s0["mix"] = _hyb_fwd(h, w0, "l0_hyb")
    rest0_arrived(h)
    w0.update(_build_layer(0, gw, gs, rep, parts=("ffn2", "ple")))
    h, s0["ffn2"] = _ffn_fwd(h, w0["norm_ffn2"], w0["ffn2_in"], w0["ffn2_out"], "l0_ffn2")
    h, s0["ple"] = _ple_fwd(h, w0["ple_norm"], pb[0], w0["ple_gate"], w0["ple_proj"], "l0_ple")
    layer1_arrived(h)
    w1 = _build_layer(1, gw, gs, rep)
    h, s1 = _layer_fwd(1, h, w1, pb[1], tables)
    loss, dh, dhb, d_final = _loss_head(h, loss_target[0], final_norm[None], "loss_head")
    loss = lax.psum(loss[0, 0], ("x", "y", "c"))

    dh, dhb, head1 = _layer_bwd_head(1, dh, dhb, s1, w1, pb[1])
    dh, dhb, tail1 = _layer_bwd_tail(1, dh, dhb, s1, w1, tables)
    grads1 = {**head1, **tail1}
    l1_token, l1_middle, l1_finish = reduce_later(keys[1], _big_grads(1, grads1), "grads_l1")
    dh, dhb, grads0 = _layer_bwd_head(0, dh, dhb, s0, w0, pb[0], deps=(l1_token,))
    dh, dhb, grads0["norm_ffn2"], grads0["ffn2_in"], grads0["ffn2_out"] = _ffn_bwd(
        dh, dhb, s0["ffn2"], w0["norm_ffn2"], w0["ffn2_in"], w0["ffn2_out"], "l0_ffn2", deps=(l1_middle(dh),))
    dh, dhb, mixer_grads = _hyb_bwd(dh, dhb, s0["mix"], w0, "l0_hyb")
    grads0.update(mixer_grads)
    l0_token, l0_middle, l0_finish = reduce_later(mixer + rest0, _big_grads(0, grads0), "grads_l0")
    last = {}

    def reduce_first(dw_in, dw_out):
        token, middle, last["finish"] = reduce_later(first, _big_grads(0, dict(ffn1_in=dw_in, ffn1_out=dw_out)), "grads_first")
        middle(token)

    dx, dhb, grads0["norm_ffn1"], grads0["ffn1_in"], grads0["ffn1_out"] = _ffn_bwd(
        dh, dhb, s0["ffn1"], w0["norm_ffn1"], w0["ffn1_in"], w0["ffn1_out"], "l0_ffn1", deps=(l0_token,),
        hook=lambda dpre: (l0_middle(dpre),), weights_hook=reduce_first)
    l1_finish(dx)
    l0_finish(dx)
    last["finish"](dx)
    _, small = _collect_grads([grads0, grads1], d_final)
    small_shapes = [small[n].shape for n in SMALL]
    all_small = _all_gather([_pack([small[n] for n in SMALL])], pltpu.VMEM, "gather_small_grads")[0]
    g = dict(zip(SMALL, _unpack(_sum_slots(all_small, "small_grads_sum"), small_shapes)))
    for n, axis in SMALL_SHARDED.items():
        g[n] = lax.dynamic_slice_in_dim(g[n], me * w[n].shape[axis], w[n].shape[axis], axis=axis)

    delta, new_m, new_v = {}, {}, {}
    for n in BIG:
        outs = _adamw_summed(view(w[n], n), view(m[n], n), view(v[n], n), [by_chip[n, i] for i in range(w[n].shape[0])],
                             f"adamw_{n}")
        g[n], delta[n], new_m[n], new_v[n] = (view(o, n) for o in outs)
    packed = [_pack([d[n] for n in SMALL]) for d in (w, g, m, v)]
    shapes = [w[n].shape for n in SMALL]
    for d, buf in zip((delta, new_m, new_v), _adamw(*packed, "adamw_small")):
        d.update(zip(SMALL, _unpack(buf, shapes)))
    return (loss, dx[None], *[g[n] for n in WEIGHTS], *[delta[n] for n in WEIGHTS], *[new_m[n] for n in WEIGHTS],
            *[new_v[n] for n in WEIGHTS])
```

```python
import functools
import math

import numpy as np
import jax
import jax.numpy as jnp
from jax import lax
from jax.experimental import pallas as pl
from jax.experimental.pallas import tpu as pltpu

F32, BF16 = jnp.float32, jnp.bfloat16
HI = lax.Precision.HIGHEST
SDS = jax.ShapeDtypeStruct

N_DEV = 8
D = 1024
D_FF = 2816
FF_SHARD = 2 * D_FF // N_DEV
PLE_DIM = 256
EPS = 1e-6
CONV_W = 31
SSM_CONV = 4
SSM_HEADS = 16
SSM_XBC = 1536
CHUNK = 128
HYB_IN = 4624
HYB_PAD = 5120
DT_COL = 4608
N_PAIR = 8
ROPE_THETA = 10000.0
LANE = 128
VMEM_LIMIT = 56 * 1024 * 1024

ADAM_LR, ADAM_B1, ADAM_B2, ADAM_EPS, ADAM_WD, ADAM_STEP = 0.001, 0.9, 0.999, 1e-08, 0.01, 10


def _params(sem):
    return pltpu.CompilerParams(dimension_semantics=sem, vmem_limit_bytes=VMEM_LIMIT)


_CHAIN = []


def _restart_chain():
    _CHAIN.clear()


def _pallas(body, *, in_specs, **kw):
    def run(*args):
        n, dep = len(args), list(_CHAIN)

        def chained(*refs):
            return body(*refs[:n], *refs[n + len(dep):])

        outs = pl.pallas_call(chained, in_specs=list(in_specs) + [pl.BlockSpec(memory_space=pl.ANY)] * len(dep), **kw)(*args, *dep)
        _CHAIN[:] = [outs[-1] if isinstance(outs, (list, tuple)) else outs]
        return outs

    return run


def _mm(a, b, *, ta=False, tb=False, reduce_j=False, out_dtypes=(F32,), tm=1024, tn=1024, tk=1024,
        epi=None, extras=(), rows=(), deps=(), sums=0, name):
    ja, jb = a.shape[0], b.shape[0]
    nj = max(ja, jb)
    jo = 1 if reduce_j else nj
    m, k = (a.shape[2], a.shape[1]) if ta else (a.shape[1], a.shape[2])
    n = b.shape[1] if tb else b.shape[2]
    assert (b.shape[2] if tb else b.shape[1]) == k and ja in (1, nj) and jb in (1, nj)
    tm, tn, tk = min(tm, m), min(tn, n), min(tk, k)
    assert m % tm == 0 and n % tn == 0 and k % tk == 0, (name, m, n, k, tm, tn, tk)
    assert not sums or (tn == n and (reduce_j or nj == 1))
    nk = k // tk
    steps = nk * (nj if reduce_j else 1)
    ne, nr, no = len(extras), len(rows), len(out_dtypes)

    def a_map(i, c, j, kk):
        return (j if ja > 1 else 0, kk, i) if ta else (j if ja > 1 else 0, i, kk)

    def b_map(i, c, j, kk):
        return (j if jb > 1 else 0, c, kk) if tb else (j if jb > 1 else 0, kk, c)

    def o_map(i, c, j, kk):
        return (0 if reduce_j else j, i, c)

    dims = (((0 if ta else 1,), (1 if tb else 0,)), ((), ()))

    def body(a_ref, b_ref, *rest):
        ex, rw = rest[:ne], rest[ne:ne + nr]
        outs = rest[ne + nr + len(deps):ne + nr + len(deps) + no]
        sum_refs = rest[ne + nr + len(deps) + no:ne + nr + len(deps) + no + sums]
        first_tile = pl.program_id(0) == 0

        def product():
            return lax.dot_general(a_ref[...], b_ref[...], dims, preferred_element_type=F32)

        def finish(acc):
            res = epi(acc, *[e[...] for e in ex], *[r[...] for r in rw]) if epi else (acc,)
            for o, r in zip(outs, res):
                o[...] = r.astype(o.dtype)
            for s_ref, r in zip(sum_refs, res[no:]):
                @pl.when(first_tile)
                def _(s_ref=s_ref, r=r):
                    s_ref[...] = r

                @pl.when(jnp.logical_not(first_tile))
                def _(s_ref=s_ref, r=r):
                    s_ref[...] += r

        if steps == 1:
            finish(product())
            return
        acc_ref = rest[-1]
        kk = pl.program_id(3)
        step = pl.program_id(2) * nk + kk if reduce_j else kk

        @pl.when(step == 0)
        def _():
            acc_ref[...] = product()

        @pl.when(jnp.logical_and(step > 0, step < steps - 1))
        def _():
            acc_ref[...] += product()

        @pl.when(step == steps - 1)
        def _():
            finish(acc_ref[...] + product())

    o_spec = pl.BlockSpec((None, tm, tn), o_map)
    row_spec = pl.BlockSpec((1, tn), lambda i, c, j, kk: (0, c))
    return _pallas(
        body, name=name, grid=(m // tm, n // tn, nj, nk),
        in_specs=[pl.BlockSpec((None, tk, tm) if ta else (None, tm, tk), a_map),
                  pl.BlockSpec((None, tn, tk) if tb else (None, tk, tn), b_map)]
        + [o_spec] * ne + [row_spec] * nr + [ANY_SPEC] * len(deps),
        out_specs=[o_spec] * no + [row_spec] * sums,
        out_shape=[SDS((jo, m, n), dt) for dt in out_dtypes] + [SDS((1, n), F32)] * sums,
        scratch_shapes=[pltpu.VMEM((tm, tn), F32)] if steps > 1 else [],
        compiler_params=_params(("arbitrary" if sums else "parallel", "parallel", "arbitrary", "arbitrary")),
    )(a, b, *extras, *rows, *deps)


def _whole(p):
    return pl.BlockSpec(p.shape, lambda *_: (0,) * p.ndim)


ANY_SPEC = pl.BlockSpec(memory_space=pl.ANY)


def _rowop(fn, tiles, params, outs, *, grid, name, deps=()):
    nin = len(tiles) + len(params)

    def body(*refs):
        res = fn(*[r[...].astype(F32) for r in refs[:nin]])
        for r, o in zip(refs[nin + len(deps):], res):
            r[...] = o.astype(r.dtype)

    return _pallas(
        body, name=name, grid=grid,
        in_specs=[s for _, s in tiles] + [_whole(p) for p in params] + [ANY_SPEC] * len(deps),
        out_specs=[s for _, _, s in outs], out_shape=[SDS(sh, dt) for sh, dt, _ in outs],
        compiler_params=_params(("parallel",) * len(grid)),
    )(*[t for t, _ in tiles], *params, *deps)


def _rowop_bwd(fn, tiles, params, cots, wrt, gouts, *, grid, name, adds=(), deps=()):
    nt, npar, nc, na = len(tiles), len(params), len(cots), len(adds)
    nin = nt + npar
    flat = [i for grp in wrt for i in grp]
    n_gout = sum(len(dts) for _, dts, _ in gouts)

    def body(*refs):
        vals = [r[...].astype(F32) for r in refs[:nin]]
        cvals = [r[...].astype(F32) for r in refs[nin:nin + nc]]
        avals = [r[...].astype(F32) for r in refs[nin + nc:nin + nc + na]]
        orefs = refs[nin + nc + na + len(deps):]
        diff_idx = flat + list(range(nt, nin))

        def f(*dv):
            full = list(vals)
            for i, v in zip(diff_idx, dv):
                full[i] = v
            return fn(*full)

        _, vjp = jax.vjp(f, *[vals[i] for i in diff_idx])
        grads = vjp(tuple(cvals))
        tile_g, par_g = list(grads[:len(flat)]), grads[len(flat):]
        group_g, at = [], 0
        for grp in wrt:
            members = tile_g[at:at + len(grp)]
            at += len(grp)
            group_g.append(members[0] if len(grp) == 1 else jnp.stack(members, axis=0))
        for av in avals:
            group_g[0] = group_g[0] + av
        o = 0
        for g, (_, dts, _) in zip(group_g, gouts):
            for _ in dts:
                orefs[o][...] = g.astype(orefs[o].dtype)
                o += 1
        first = functools.reduce(jnp.logical_and, [pl.program_id(ax) == 0 for ax in range(len(grid))])
        for r, g in zip(orefs[n_gout:], par_g):
            @pl.when(first)
            def _(r=r, g=g):
                r[...] = g

            @pl.when(jnp.logical_not(first))
            def _(r=r, g=g):
                r[...] += g

    out_specs, out_shape = [], []
    for sh, dts, spec in gouts:
        for dt in dts:
            out_specs.append(spec)
            out_shape.append(SDS(sh, dt))
    for p in params:
        out_specs.append(_whole(p))
        out_shape.append(SDS(p.shape, F32))
    return _pallas(
        body, name=name, grid=grid,
        in_specs=[s for _, s in tiles] + [_whole(p) for p in params] + [s for _, s in cots] + [s for _, s in adds]
        + [ANY_SPEC] * len(deps),
        out_specs=out_specs, out_shape=out_shape,
        compiler_params=_params(("arbitrary",) * len(grid)),
    )(*[t for t, _ in tiles], *params, *[c for c, _ in cots], *[a for a, _ in adds], *deps)


def _tok(c, tm, col=0):
    return pl.BlockSpec((tm, c), lambda i, col=col: (i, col))


def _rms_fn(h, g):
    return (h * lax.rsqrt(jnp.mean(h * h, axis=-1, keepdims=True) + EPS) * g,)


def _lnswish_fn(u, g, b):
    mu = jnp.mean(u, axis=-1, keepdims=True)
    xc = u - mu
    y = xc * lax.rsqrt(jnp.mean(xc * xc, axis=-1, keepdims=True) + EPS) * g + b
    return (y * jax.nn.sigmoid(y),)


def _ple_fn(z, e):
    return (jax.nn.sigmoid(z) * e,)


def _rms(h, g, name, tm=512, deps=()):
    t = h.shape[0]
    return _rowop(_rms_fn, [(h, _tok(D, tm))], [g], [((t, D), BF16, _tok(D, tm))], grid=(t // tm,), name=name, deps=deps)[0]


def _drms_epi(dn, h, dres, g):
    _, vjp = jax.vjp(_rms_fn, h, g)
    dh, dg = vjp((dn,))
    dh = dh + dres
    return dh, dh, dg


def _mm_drms(a, b, h, g, dres, name, tk, tb=True):
    dh, dhb, dg = _mm(a, b, tb=tb, reduce_j=a.shape[0] > 1, tm=512, tk=tk, epi=_drms_epi, extras=(h[None], dres[None]),
                      rows=(g,), out_dtypes=(F32, BF16), sums=1, name=name)
    return dh[0], dhb[0], dg


def _conv_geometry(width):
    pad = 32 if width > 8 else 8
    return pad, pad - (width - 1)


def _fill_shifts(xpad_ref, sh_ref, t, shifts):
    for r in shifts:
        sh_ref[r, :, :] = xpad_ref[pl.ds(r, t + 32), :]


def _dwconv(xs, w, b, *, width, glu, silu, cb, name):
    t = xs[0][0].shape[0]
    c = w.shape[1]
    pad, off = _conv_geometry(width)
    shifts = sorted({(k + off) % 8 for k in range(width)})
    ch = 32

    def body(*refs):
        x_refs, (w_ref, b_ref, o_ref, xpad_ref, sh_ref) = refs[:len(xs)], refs[len(xs):]
        u = x_refs[0][...] * jax.nn.sigmoid(x_refs[1][...]) if glu else x_refs[0][...]
        xpad_ref[pl.ds(0, pad), :] = jnp.zeros((pad, cb), F32)
        xpad_ref[pl.ds(pad, t), :] = u
        xpad_ref[pl.ds(pad + t, 40 - pad), :] = jnp.zeros((40 - pad, cb), F32)
        _fill_shifts(xpad_ref, sh_ref, t, shifts)

        def chunk(i, carry):
            t0 = pl.multiple_of(i * ch, ch)
            acc = jnp.broadcast_to(b_ref[...], (ch, cb))
            for k in range(width):
                q, r = divmod(k + off, 8)
                acc = acc + w_ref[pl.ds(k, 1), :] * sh_ref[r, pl.ds(t0 + 8 * q, ch), :]
            o_ref[pl.ds(t0, ch), :] = acc * jax.nn.sigmoid(acc) if silu else acc
            return carry

        lax.fori_loop(0, t // ch, chunk, 0)

    return _pallas(
        body, name=name, grid=(c // cb,),
        in_specs=[pl.BlockSpec((t, cb), lambda i, o=o: (0, o + i)) for _, o in xs]
        + [pl.BlockSpec((width, cb), lambda i: (0, i)), pl.BlockSpec((1, cb), lambda i: (0, i))],
        out_specs=pl.BlockSpec((t, cb), lambda i: (0, i)), out_shape=SDS((t, c), F32),
        scratch_shapes=[pltpu.VMEM((t + 40, cb), F32), pltpu.VMEM((8, t + 32, cb), F32)],
        compiler_params=_params(("parallel",)),
    )(*[x for x, _ in xs], w, b)


def _dwconv_bwd(xs, w, b, dy, *, width, glu, silu, cb, name):
    t = xs[0][0].shape[0]
    c = w.shape[1]
    pad, off = _conv_geometry(width)
    shifts = sorted({(k + off) % 8 for k in range(width)})
    shifts_t = sorted({mm % 8 for mm in range(width)})
    ch = 32
    nx = len(xs)

    def body(*refs):
        x_refs = refs[:nx]
        w_ref, b_ref, dy_ref = refs[nx:nx + 3]
        dx_refs = refs[nx + 3:nx + 3 + nx]
        dw_ref, db_ref, xpad_ref, sh_ref, dc_ref = refs[nx + 3 + nx:]
        u = x_refs[0][...] * jax.nn.sigmoid(x_refs[1][...]) if glu else x_refs[0][...]
        xpad_ref[pl.ds(0, pad), :] = jnp.zeros((pad, cb), F32)
        xpad_ref[pl.ds(pad, t), :] = u
        xpad_ref[pl.ds(pad + t, 40 - pad), :] = jnp.zeros((40 - pad, cb), F32)
        _fill_shifts(xpad_ref, sh_ref, t, shifts)

        if silu:
            def act_chunk(i, carry):
                t0 = pl.multiple_of(i * ch, ch)
                acc = jnp.broadcast_to(b_ref[...], (ch, cb))
                for k in range(width):
                    q, r = divmod(k + off, 8)
                    acc = acc + w_ref[pl.ds(k, 1), :] * sh_ref[r, pl.ds(t0 + 8 * q, ch), :]
                sg = jax.nn.sigmoid(acc)
                dc_ref[pl.ds(t0, ch), :] = dy_ref[pl.ds(t0, ch), :] * (sg * (1.0 + acc * (1.0 - sg)))
                return carry

            lax.fori_loop(0, t // ch, act_chunk, 0)
        else:
            dc_ref[...] = dy_ref[...]

        def dw_chunk(i, accs):
            t0 = pl.multiple_of(i * 8, 8)
            d = dc_ref[pl.ds(t0, 8), :]
            new = []
            for k in range(width):
                q, r = divmod(k + off, 8)
                new.append(accs[k] + d * sh_ref[r, pl.ds(t0 + 8 * q, 8), :])
            new.append(accs[width] + d)
            return tuple(new)

        accs = lax.fori_loop(0, t // 8, dw_chunk, tuple(jnp.zeros((8, cb), F32) for _ in range(width + 1)))
        for k in range(width):
            dw_ref[pl.ds(k, 1), :] = jnp.sum(accs[k], axis=0, keepdims=True)
        db_ref[...] = jnp.sum(accs[width], axis=0, keepdims=True)

        xpad_ref[pl.ds(0, t), :] = dc_ref[...]
        xpad_ref[pl.ds(t, 40), :] = jnp.zeros((40, cb), F32)
        _fill_shifts(xpad_ref, sh_ref, t, shifts_t)

        def dx_chunk(i, carry):
            t0 = pl.multiple_of(i * ch, ch)
            acc = jnp.zeros((ch, cb), F32)
            for mm in range(width):
                q, r = divmod(mm, 8)
                acc = acc + w_ref[pl.ds(width - 1 - mm, 1), :] * sh_ref[r, pl.ds(t0 + 8 * q, ch), :]
            if glu:
                val, gate = x_refs[0][pl.ds(t0, ch), :], x_refs[1][pl.ds(t0, ch), :]
                sg = jax.nn.sigmoid(gate)
                dx_refs[0][pl.ds(t0, ch), :] = (acc * sg).astype(BF16)
                dx_refs[1][pl.ds(t0, ch), :] = (acc * val * sg * (1.0 - sg)).astype(BF16)
            else:
                dx_refs[0][pl.ds(t0, ch), :] = acc.astype(BF16)
            return carry

        lax.fori_loop(0, t // ch, dx_chunk, 0)

    col = pl.BlockSpec((t, cb), lambda i: (0, i))
    return _pallas(
        body, name=name, grid=(c // cb,),
        in_specs=[pl.BlockSpec((t, cb), lambda i, o=o: (0, o + i)) for _, o in xs]
        + [pl.BlockSpec((width, cb), lambda i: (0, i)), pl.BlockSpec((1, cb), lambda i: (0, i)), col],
        out_specs=[col] * nx + [pl.BlockSpec((width, cb), lambda i: (0, i)), pl.BlockSpec((1, cb), lambda i: (0, i))],
        out_shape=[SDS((t, c), BF16)] * nx + [SDS((width, c), F32), SDS((1, c), F32)],
        scratch_shapes=[pltpu.VMEM((t + 40, cb), F32), pltpu.VMEM((8, t + 32, cb), F32), pltpu.VMEM((t, cb), F32)],
        compiler_params=_params(("parallel",)),
    )(*[x for x, _ in xs], w, b, dy)


_DIMS = {"nn": (((1,), (0,)), ((), ())), "nt": (((1,), (1,)), ((), ())), "tn": (((0,), (0,)), ((), ()))}


def _raw_dot(a, b, mode):
    return lax.dot_general(a.astype(BF16), b.astype(BF16), _DIMS[mode], preferred_element_type=F32)


@functools.partial(jax.custom_vjp, nondiff_argnums=(2,))
def _bdot(a, b, mode):
    return _raw_dot(a, b, mode)


def _bdot_fwd(a, b, mode):
    return _raw_dot(a, b, mode), (a, b)


def _bdot_bwd(mode, res, g):
    a, b = res
    if mode == "nn":
        return _raw_dot(g, b, "nt"), _raw_dot(a, g, "tn")
    if mode == "nt":
        return _raw_dot(g, b, "nn"), _raw_dot(g, a, "tn")
    return _raw_dot(b, g, "nt"), _raw_dot(a, g, "nn")


_bdot.defvjp(_bdot_fwd, _bdot_bwd)


def _iota(shape, axis):
    return lax.broadcasted_iota(jnp.int32, shape, axis)


def _half_masks():
    left = (_iota((1, LANE), 1) < 64).astype(F32)
    return left, 1.0 - left


def _split3(a):
    a1 = a.astype(BF16)
    r1 = a - a1.astype(F32)
    a2 = r1.astype(BF16)
    return a1, a2, (r1 - a2.astype(F32)).astype(BF16)


def _exact_dot(a, e, mode):
    return sum(lax.dot_general(piece, e, _DIMS[mode], preferred_element_type=F32) for piece in _split3(a))


@jax.custom_vjp
def _spread(a, e):
    return _exact_dot(a, e, "nn")


_spread.defvjp(lambda a, e: (_exact_dot(a, e, "nn"), e), lambda e, g: (_exact_dot(g, e, "nt"), jnp.zeros_like(e)))


@jax.custom_vjp
def _running_sum(tri, a):
    return sum(lax.dot_general(tri, piece, _DIMS["nn"], preferred_element_type=F32) for piece in _split3(a))


_running_sum.defvjp(
    lambda tri, a: (sum(lax.dot_general(tri, piece, _DIMS["nn"], preferred_element_type=F32) for piece in _split3(a)), tri),
    lambda tri, g: (jnp.zeros_like(tri), sum(lax.dot_general(tri, piece, _DIMS["tn"], preferred_element_type=F32)
                                             for piece in _split3(g))))


def _ssd_chunk(state, xa, dtr, z, dtb, alog, dskf, ng):
    xs, bm, cm = xa[:, :D], xa[:, D:D + 256], xa[:, D + 256:]
    left, right = _half_masks()
    expand = (_iota((LANE, D), 1) // 64 == _iota((LANE, D), 0)).astype(BF16)
    li, si = _iota((CHUNK, CHUNK), 0), _iota((CHUNK, CHUNK), 1)
    tril = li >= si
    dt16 = jax.nn.softplus(dtr + dtb)
    adt = dt16 * (-jnp.exp(alog))
    dtf = _spread(dt16, expand)
    cs16 = _running_sum(tril.astype(BF16), adt)
    csf = _spread(cs16, expand)
    totf = jnp.sum(jnp.where(_iota((CHUNK, D), 0) == CHUNK - 1, csf, 0.0), axis=0, keepdims=True)
    cst = cs16.T
    xdt = xs * dtf
    ys, new_state = [], []
    for g in range(2):
        bg, cg = bm[:, LANE * g:LANE * (g + 1)], cm[:, LANE * g:LANE * (g + 1)]
        cb = _bdot(cg, bg, "nt")
        for q in range(4):
            pr = 4 * g + q
            decay = []
            for h in (2 * pr, 2 * pr + 1):
                col = jnp.sum(jnp.where(si == h, cs16, 0.0), axis=1, keepdims=True)
                row = jnp.sum(jnp.where(li == h, cst, 0.0), axis=0, keepdims=True)
                decay.append(cb * jnp.exp(jnp.where(tril, col - row, -jnp.inf)))
            xp = xdt[:, LANE * pr:LANE * (pr + 1)]
            y_diag = _bdot(jnp.concatenate(decay, axis=1), jnp.concatenate([xp * left, xp * right], axis=0), "nn")
            csb, tot = csf[:, LANE * pr:LANE * (pr + 1)], totf[:, LANE * pr:LANE * (pr + 1)]
            ys.append(y_diag + _bdot(cg, state[pr], "nn") * jnp.exp(csb))
            new_state.append(state[pr] * jnp.exp(tot) + _bdot(bg, xp * jnp.exp(tot - csb), "tn"))
    y = jnp.concatenate(ys, axis=1)
    y = y + dskf * xs
    y = y * (z * jax.nn.sigmoid(z))
    halves = []
    for g in range(2):
        yg = y[:, 512 * g:512 * (g + 1)]
        halves.append(yg * lax.rsqrt(jnp.mean(yg * yg, axis=-1, keepdims=True) + EPS))
    return jnp.concatenate(halves, axis=1) * ng, jnp.stack(new_state, axis=0)


def _ssd_specs(t, rev):
    nc = t // CHUNK
    ix = (lambda c: nc - 1 - c) if rev else (lambda c: c)
    return nc, ix


def _ssd_fwd(xa, proj, dtb, alog, dsk, ng, name):
    t = xa.shape[0]
    nc, ix = _ssd_specs(t, False)

    def body(xa_ref, dt_ref, z_ref, dtb_ref, alog_ref, dsk_ref, ng_ref, y_ref, st_ref, carry_ref):
        @pl.when(pl.program_id(0) == 0)
        def _():
            carry_ref[...] = jnp.zeros_like(carry_ref)

        st_ref[...] = carry_ref[...]
        y, new = _ssd_chunk(carry_ref[...], xa_ref[...], dt_ref[...], z_ref[...], dtb_ref[...], alog_ref[...],
                            dsk_ref[...], ng_ref[...])
        y_ref[...] = y.astype(BF16)
        carry_ref[...] = new

    small = [dtb, alog, dsk, ng]
    return _pallas(
        body, name=name, grid=(nc,),
        in_specs=[pl.BlockSpec((CHUNK, SSM_XBC), lambda c: (c, 0)),
                  pl.BlockSpec((CHUNK, LANE), lambda c: (c, DT_COL // LANE)),
                  pl.BlockSpec((CHUNK, D), lambda c: (c, 2))] + [_whole(p) for p in small],
        out_specs=[pl.BlockSpec((CHUNK, D), lambda c: (c, 0)), pl.BlockSpec((None, N_PAIR, LANE, LANE), lambda c: (c, 0, 0, 0))],
        out_shape=[SDS((t, D), BF16), SDS((nc, N_PAIR, LANE, LANE), F32)],
        scratch_shapes=[pltpu.VMEM((N_PAIR, LANE, LANE), F32)],
        compiler_params=_params(("arbitrary",)),
    )(xa, proj, proj, *small)


def _ssd_bwd(xa, proj, states, dy, dtb, alog, dsk, ng, name):
    t = xa.shape[0]
    nc, ix = _ssd_specs(t, True)

    def body(xa_ref, dt_ref, z_ref, st_ref, dy_ref, dtb_ref, alog_ref, dsk_ref, ng_ref,
             dxa_ref, ddt_ref, dz_ref, gdtb_ref, galog_ref, gdsk_ref, gng_ref, carry_ref):
        first = pl.program_id(0) == 0

        @pl.when(first)
        def _():
            carry_ref[...] = jnp.zeros_like(carry_ref)

        args = (st_ref[...], xa_ref[...], dt_ref[...], z_ref[...], dtb_ref[...], alog_ref[...], dsk_ref[...], ng_ref[...])
        _, vjp = jax.vjp(_ssd_chunk, *args)
        ds, dxa, ddt, dz, gdtb, galog, gdsk, gng = vjp((dy_ref[...], carry_ref[...]))
        carry_ref[...] = ds
        dxa_ref[...] = dxa
        ddt_ref[...] = ddt.astype(BF16)
        dz_ref[...] = dz.astype(BF16)
        for r, g in ((gdtb_ref, gdtb), (galog_ref, galog), (gdsk_ref, gdsk), (gng_ref, gng)):
            @pl.when(first)
            def _(r=r, g=g):
                r[...] = g

            @pl.when(jnp.logical_not(first))
            def _(r=r, g=g):
                r[...] += g

    small = [dtb, alog, dsk, ng]
    return _pallas(
        body, name=name, grid=(nc,),
        in_specs=[pl.BlockSpec((CHUNK, SSM_XBC), lambda c: (ix(c), 0)),
                  pl.BlockSpec((CHUNK, LANE), lambda c: (ix(c), DT_COL // LANE)),
                  pl.BlockSpec((CHUNK, D), lambda c: (ix(c), 2)),
                  pl.BlockSpec((None, N_PAIR, LANE, LANE), lambda c: (ix(c), 0, 0, 0)),
                  pl.BlockSpec((CHUNK, D), lambda c: (ix(c), 0))] + [_whole(p) for p in small],
        out_specs=[pl.BlockSpec((CHUNK, SSM_XBC), lambda c: (ix(c), 0)), pl.BlockSpec((CHUNK, LANE), lambda c: (ix(c), 0)),
                   pl.BlockSpec((CHUNK, D), lambda c: (ix(c), 0))] + [_whole(p) for p in small],
        out_shape=[SDS((t, SSM_XBC), F32), SDS((t, LANE), BF16), SDS((t, D), BF16)] + [SDS(p.shape, F32) for p in small],
        scratch_shapes=[pltpu.VMEM((N_PAIR, LANE, LANE), F32)],
        compiler_params=_params(("arbitrary",)),
    )(xa, proj, proj, states, dy, *small)


def _attn_block(q, kv_prev, kv_cur, cq, sq, ck, sk, sinks, rot, first_block):
    left, right = _half_masks()
    k2 = jnp.concatenate([kv_prev[:, :256], kv_cur[:, :256]], axis=0)
    v2 = jnp.concatenate([kv_prev[:, 256:], kv_cur[:, 256:]], axis=0)
    ri, ci = _iota((LANE, LANE), 0), _iota((LANE, LANE), 1)
    dup = [((ri < 64) & (ci % 64 == ri)).astype(BF16), ((ri >= 64) & (ci % 64 == ri - 64)).astype(BF16)]

    rot16 = rot.astype(BF16)

    def rope(tt, c, s):
        return tt * c + _spread(tt, rot16) * s

    kd, vd = [], []
    for j in range(4):
        sl = slice(LANE * (j // 2), LANE * (j // 2 + 1))
        kd.append(_bdot(rope(k2[:, sl], ck, sk), dup[j % 2], "nn"))
        vd.append(_bdot(v2[:, sl], dup[j % 2], "nn"))
    qi, si = _iota((2 * CHUNK, 2 * CHUNK), 0) % CHUNK, _iota((2 * CHUNK, 2 * CHUNK), 1)
    valid = (si > qi) & (si <= qi + CHUNK) & jnp.logical_or(si >= CHUNK, jnp.logical_not(first_block))
    upper = _iota((2 * CHUNK, 1), 0) < CHUNK
    lanes = _iota((1, LANE), 1)
    outs = []
    for pr in range(N_PAIR):
        qr = rope(q[:, LANE * pr:LANE * (pr + 1)], cq, sq)
        lg = _bdot(jnp.concatenate([qr * left, qr * right], axis=0), kd[pr // 2], "nt") * 0.125
        lg = jnp.where(valid, lg, -jnp.inf)
        s1 = jnp.sum(jnp.where(lanes == 2 * pr, sinks, 0.0), axis=1, keepdims=True)
        s2 = jnp.sum(jnp.where(lanes == 2 * pr + 1, sinks, 0.0), axis=1, keepdims=True)
        sink = jnp.where(upper, s1, s2)
        mx = lax.stop_gradient(jnp.maximum(jnp.max(lg, axis=-1, keepdims=True), sink))
        e = jnp.exp(lg - mx)
        probs = e / (jnp.sum(e, axis=-1, keepdims=True) + jnp.exp(sink - mx))
        o2 = _bdot(probs, vd[pr // 2], "nn")
        outs.append(o2[:CHUNK] * left + o2[CHUNK:] * right)
    return jnp.concatenate(outs, axis=1)


def _attn_fwd(qkv, cos, sin, sinks, rot, name):
    t = qkv.shape[0]
    nb = t // CHUNK

    def body(q_ref, kvp_ref, kvc_ref, cq_ref, sq_ref, cp_ref, sp_ref, sinks_ref, rot_ref, o_ref):
        ck = jnp.concatenate([cp_ref[...], cq_ref[...]], axis=0)
        sk = jnp.concatenate([sp_ref[...], sq_ref[...]], axis=0)
        o_ref[...] = _attn_block(q_ref[...], kvp_ref[...], kvc_ref[...], cq_ref[...], sq_ref[...], ck, sk,
                                 sinks_ref[...], rot_ref[...], pl.program_id(0) == 0).astype(BF16)

    prev = lambda n: jnp.maximum(n - 1, 0)
    return _pallas(
        body, name=name, grid=(nb,),
        in_specs=[pl.BlockSpec((CHUNK, D), lambda n: (n, 0)),
                  pl.BlockSpec((CHUNK, 512), lambda n: (prev(n), 2)), pl.BlockSpec((CHUNK, 512), lambda n: (n, 2)),
                  pl.BlockSpec((CHUNK, LANE), lambda n: (n, 0)), pl.BlockSpec((CHUNK, LANE), lambda n: (n, 0)),
                  pl.BlockSpec((CHUNK, LANE), lambda n: (prev(n), 0)), pl.BlockSpec((CHUNK, LANE), lambda n: (prev(n), 0)),
                  _whole(sinks), _whole(rot)],
        out_specs=pl.BlockSpec((CHUNK, D), lambda n: (n, 0)), out_shape=SDS((t, D), BF16),
        compiler_params=_params(("parallel",)),
    )(qkv, qkv, qkv, cos, sin, cos, sin, sinks, rot)


def _attn_bwd(qkv, do, cos, sin, sinks, rot, name):
    t = qkv.shape[0]
    nb = t // CHUNK

    def body(q_ref, kvp_ref, kvc_ref, do_ref, cq_ref, sq_ref, cp_ref, sp_ref, sinks_ref, rot_ref,
             dq_ref, dkv_ref, dbq_ref, dbkv_ref, dsink_ref, carry_ref):
        n = pl.program_id(0)

        @pl.when(n == 0)
        def _():
            carry_ref[...] = jnp.zeros_like(carry_ref)
            dbq_ref[...] = jnp.zeros_like(dbq_ref)
            dbkv_ref[...] = jnp.zeros_like(dbkv_ref)
            dsink_ref[...] = jnp.zeros_like(dsink_ref)

        @pl.when(n < nb)
        def _():
            ck = jnp.concatenate([cp_ref[...], cq_ref[...]], axis=0)
            sk = jnp.concatenate([sp_ref[...], sq_ref[...]], axis=0)
            f = lambda q, kvp, kvc, s: _attn_block(q, kvp, kvc, cq_ref[...], sq_ref[...], ck, sk, s, rot_ref[...], n == 0)
            _, vjp = jax.vjp(f, q_ref[...], kvp_ref[...], kvc_ref[...], sinks_ref[...])
            dq, dkvp, dkvc, ds = vjp(do_ref[...].astype(F32))
            done = carry_ref[...] + dkvp
            dq_ref[...] = dq.astype(BF16)
            dkv_ref[...] = done.astype(BF16)
            dbq_ref[...] += jnp.sum(dq, axis=0, keepdims=True)
            dsink_ref[...] += ds
            carry_ref[...] = dkvc

            @pl.when(n > 0)
            def _():
                dbkv_ref[...] += jnp.sum(done, axis=0, keepdims=True)

        @pl.when(n == nb)
        def _():
            done = carry_ref[...]
            dkv_ref[...] = done.astype(BF16)
            dbkv_ref[...] += jnp.sum(done, axis=0, keepdims=True)

    cur = lambda n: jnp.minimum(n, nb - 1)
    prev = lambda n: jnp.maximum(jnp.minimum(n, nb - 1) - 1, 0)
    fin = lambda n: jnp.maximum(n - 1, 0)
    outs = _pallas(
        body, name=name, grid=(nb + 1,),
        in_specs=[pl.BlockSpec((CHUNK, D), lambda n: (cur(n), 0)),
                  pl.BlockSpec((CHUNK, 512), lambda n: (prev(n), 2)), pl.BlockSpec((CHUNK, 512), lambda n: (cur(n), 2)),
                  pl.BlockSpec((CHUNK, D), lambda n: (cur(n), 0)),
                  pl.BlockSpec((CHUNK, LANE), lambda n: (cur(n), 0)), pl.BlockSpec((CHUNK, LANE), lambda n: (cur(n), 0)),
                  pl.BlockSpec((CHUNK, LANE), lambda n: (prev(n), 0)), pl.BlockSpec((CHUNK, LANE), lambda n: (prev(n), 0)),
                  _whole(sinks), _whole(rot)],
        out_specs=[pl.BlockSpec((CHUNK, D), lambda n: (cur(n), 0)), pl.BlockSpec((CHUNK, 512), lambda n: (fin(n), 0)),
                   pl.BlockSpec((1, D), lambda n: (0, 0)), pl.BlockSpec((1, 512), lambda n: (0, 0)), _whole(sinks)],
        out_shape=[SDS((t, D), BF16), SDS((t, 512), BF16), SDS((1, D), F32), SDS((1, 512), F32), SDS(sinks.shape, F32)],
        scratch_shapes=[pltpu.VMEM((CHUNK, 512), F32)],
        compiler_params=_params(("arbitrary",)),
    )(qkv, qkv, qkv, do, cos, sin, cos, sin, sinks, rot)
    dq, dkv, dbq, dbkv, dsinks = outs
    return jnp.concatenate([dq, dkv], axis=1), jnp.concatenate([dbq, dbkv], axis=1), dsinks


def _loss_head(h, tgt, g, name, tm=512):
    t = h.shape[0]

    def body(h_ref, t_ref, g_ref, loss_ref, dh_ref, dhb_ref, dg_ref):
        def f(hv, gv):
            err = _rms_fn(hv, gv)[0] - t_ref[...]
            return 0.5 * jnp.sum(jnp.mean(err * err, axis=-1, keepdims=True), axis=0, keepdims=True)

        loss, vjp = jax.vjp(f, h_ref[...], g_ref[...])
        dh, dg = vjp(jnp.ones((1, 1), F32))
        dh_ref[...] = dh
        dhb_ref[...] = dh.astype(BF16)
        first = pl.program_id(0) == 0

        @pl.when(first)
        def _():
            loss_ref[...] = loss
            dg_ref[...] = dg

        @pl.when(jnp.logical_not(first))
        def _():
            loss_ref[...] += loss
            dg_ref[...] += dg

    return _pallas(
        body, name=name, grid=(t // tm,),
        in_specs=[_tok(D, tm), _tok(D, tm), _whole(g)],
        out_specs=[pl.BlockSpec((1, 1), lambda i: (0, 0)), _tok(D, tm), _tok(D, tm), _whole(g)],
        out_shape=[SDS((1, 1), F32), SDS((t, D), F32), SDS((t, D), BF16), SDS(g.shape, F32)],
        compiler_params=_params(("arbitrary",)),
    )(h, tgt, g)


def _res_half(acc, res):
    return (res + 0.5 * acc,)


def _res_full(acc, res):
    return (res + acc,)


def _half(acc):
    return (0.5 * acc,)


def _ffn_in(n, w_in, name, tm=1024):
    t = n.shape[0]
    tm = min(tm, t)

    def body(n_ref, w_ref, pre_ref, act_ref):
        a = n_ref[...]
        gate = lax.dot_general(a, w_ref[0], _DIMS["nt"], preferred_element_type=F32)
        up = lax.dot_general(a, w_ref[1], _DIMS["nt"], preferred_element_type=F32)
        pre_ref[0] = gate.astype(BF16)
        pre_ref[1] = up.astype(BF16)
        act_ref[...] = (gate * jax.nn.sigmoid(gate) * up).astype(BF16)

    pair = pl.BlockSpec((2, None, tm, FF_SHARD), lambda i, j: (0, j, i, 0))
    return _pallas(
        body, name=name, grid=(t // tm, 4),
        in_specs=[pl.BlockSpec((tm, D), lambda i, j: (i, 0)), pl.BlockSpec((2, None, FF_SHARD, D), lambda i, j: (0, j, 0, 0))],
        out_specs=[pair, pl.BlockSpec((None, tm, FF_SHARD), lambda i, j: (j, i, 0))],
        out_shape=[SDS((2, 4, t, FF_SHARD), BF16), SDS((4, t, FF_SHARD), BF16)],
        compiler_params=_params(("parallel", "parallel")),
    )(n, w_in.reshape(2, 4, FF_SHARD, D))


def _ffn_dact(dhb, w_out, pre, name, tm=1024, deps=()):
    t = dhb.shape[0]
    tm = min(tm, t)

    def body(d_ref, w_ref, pre_ref, *rest):
        o_ref = rest[-1]
        dact = 0.5 * lax.dot_general(d_ref[...], w_ref[...], _DIMS["nt"], preferred_element_type=F32)
        gate, up = pre_ref[0].astype(F32), pre_ref[1].astype(F32)
        sg = jax.nn.sigmoid(gate)
        o_ref[0] = (dact * up * (sg * (1.0 + gate * (1.0 - sg)))).astype(BF16)
        o_ref[1] = (dact * (gate * sg)).astype(BF16)

    pair = pl.BlockSpec((2, None, tm, FF_SHARD), lambda i, j: (0, j, i, 0))
    return _pallas(
        body, name=name, grid=(t // tm, 4),
        in_specs=[pl.BlockSpec((tm, D), lambda i, j: (i, 0)), pl.BlockSpec((None, FF_SHARD, D), lambda i, j: (j, 0, 0)), pair]
        + [ANY_SPEC] * len(deps),
        out_specs=pair, out_shape=SDS((2, 4, t, FF_SHARD), BF16),
        compiler_params=_params(("parallel", "parallel")),
    )(dhb, w_out, pre, *deps)


def _ffn_fwd(h, g, w_in, w_out, tag, deps=()):
    n = _rms(h, g, f"{tag}_rms", deps=deps)
    pre, act = _ffn_in(n, w_in, f"{tag}_in")
    w_out = w_out() if callable(w_out) else w_out
    out = _mm(act, w_out, reduce_j=True, tk=FF_SHARD, epi=_res_half, extras=(h[None],), name=f"{tag}_out")[0][0]
    return out, (h, n, pre, act)


def _ffn_bwd(dh, dhb, saved, g, w_in, w_out, tag, deps=(), hook=None, weights_hook=None):
    h, n, pre, act = saved
    t = h.shape[0]
    dpre = _ffn_dact(dhb, w_out, pre, f"{tag}_dact", deps=deps).reshape(N_DEV, t, FF_SHARD)
    dw_out = _mm(act, dhb[None], ta=True, tm=FF_SHARD, epi=_half, out_dtypes=(BF16,), deps=hook(dpre) if hook else (),
                 name=f"{tag}_dwout")[0]
    dw_in = _mm(dpre, n[None], ta=True, tm=FF_SHARD, out_dtypes=(BF16,), name=f"{tag}_dwin")[0]
    if weights_hook:
        weights_hook(dw_in, dw_out)
    dh_in, dhb_in, dg = _mm_drms(dpre, w_in, h, g, dh, f"{tag}_dn", FF_SHARD, tb=False)
    return dh_in, dhb_in, dg, dw_in, dw_out


def _ple_fwd(h, g, pb, w_gate, w_proj, tag):
    t = h.shape[0]
    tm = 512
    n = _rms(h, g, f"{tag}_rms")
    e = _mm(pb[None], w_proj[None], name=f"{tag}_proj")[0][0]
    z = _mm(n[None], w_gate[None], name=f"{tag}_gate")[0][0]
    out = _rowop(lambda zz, ee, hh: (hh + _ple_fn(zz, ee)[0],), [(z, _tok(D, tm)), (e, _tok(D, tm)), (h, _tok(D, tm))], [],
                 [((t, D), F32, _tok(D, tm))], grid=(t // tm,), name=f"{tag}_mix")[0]
    return out, (h, n, e, z)


def _ple_bwd(dh, dhb, saved, g, pb, w_gate, tag, deps=()):
    h, n, e, z = saved
    t = h.shape[0]
    tm = 512
    dz, de = _rowop_bwd(_ple_fn, [(z, _tok(D, tm)), (e, _tok(D, tm))], [], [(dh, _tok(D, tm))], [(0,), (1,)],
                        [((t, D), (BF16,), _tok(D, tm)), ((t, D), (BF16,), _tok(D, tm))], grid=(t // tm,), name=f"{tag}_dmix",
                        deps=deps)
    dw_proj = _mm(pb[None], de[None], ta=True, out_dtypes=(BF16,), name=f"{tag}_dwproj")[0][0]
    dw_gate = _mm(n[None], dz[None], ta=True, out_dtypes=(BF16,), name=f"{tag}_dwgate")[0][0]
    dh_in, dhb_in, dg = _mm_drms(dz[None], w_gate[None], h, g, dh, f"{tag}_dn", 1024)
    return dh_in, dhb_in, dg, dw_gate, dw_proj


def _hyb_fwd(h, w, tag, after_in=None):
    t = h.shape[0]
    tm = 512
    hn = _rms(h, w["norm_mix"], f"{tag}_rms")
    proj = _mm(hn[None], w["hyb_in"][None], tn=512, name=f"{tag}_in")[0][0]
    if after_in:
        after_in()
    u1 = _dwconv([(proj, 0), (proj, D // LANE)], w["conv_w"], w["conv_b"], width=CONV_W, glu=True, silu=False, cb=LANE,
                 name=f"{tag}_conv")
    u = _rowop(_lnswish_fn, [(u1, _tok(D, tm))], [w["ln_g"], w["ln_b"]], [((t, D), BF16, _tok(D, tm))], grid=(t // tm,),
               name=f"{tag}_ln")[0]
    xa = _dwconv([(proj, 3 * D // LANE)], w["sconv_w"], w["sconv_b"], width=SSM_CONV, glu=False, silu=True, cb=LANE,
                 name=f"{tag}_sconv")
    y, states = _ssd_fwd(xa, proj, w["dt_bias"], w["a_log"], w["d_skip"], w["ssm_norm"], f"{tag}_ssd")
    mixed = jnp.stack([u, y], axis=0)
    out = _mm(mixed, w["hyb_out"], reduce_j=True, epi=_res_full, extras=(h[None],), name=f"{tag}_out")[0][0]
    return out, (h, hn, proj, u1, xa, states, mixed)


def _hyb_bwd(dh, dhb, saved, w, tag):
    h, hn, proj, u1, xa, states, mixed = saved
    t = h.shape[0]
    tm = 512
    dmix = _mm(dhb[None], w["hyb_out"], tb=True, name=f"{tag}_dmix")[0]
    dw_out = _mm(mixed, dhb[None], ta=True, out_dtypes=(BF16,), name=f"{tag}_dwout")[0]
    du1, dln_g, dln_b = _rowop_bwd(_lnswish_fn, [(u1, _tok(D, tm))], [w["ln_g"], w["ln_b"]], [(dmix[0], _tok(D, tm))], [(0,)],
                                   [((t, D), (F32,), _tok(D, tm))], grid=(t // tm,), name=f"{tag}_dln")
    dval, dgate, dconv_w, dconv_b = _dwconv_bwd([(proj, 0), (proj, D // LANE)], w["conv_w"], w["conv_b"], du1,
                                                width=CONV_W, glu=True, silu=False, cb=LANE, name=f"{tag}_dconv")
    dxa, ddt, dz, g_dtb, g_alog, g_dsk, g_ng = _ssd_bwd(xa, proj, states, dmix[1], w["dt_bias"], w["a_log"], w["d_skip"],
                                                         w["ssm_norm"], f"{tag}_dssd")
    dxbc, dsconv_w, dsconv_b = _dwconv_bwd([(proj, 3 * D // LANE)], w["sconv_w"], w["sconv_b"], dxa, width=SSM_CONV,
                                           glu=False, silu=True, cb=LANE, name=f"{tag}_dsconv")
    dproj = jnp.concatenate([dval, dgate, dz, dxbc, ddt, jnp.zeros((t, HYB_PAD - DT_COL - LANE), BF16)], axis=1)
    dh_in, dhb_in, dg = _mm_drms(dproj[None], w["hyb_in"][None], h, w["norm_mix"], dh, f"{tag}_dhn", 1024)
    dw_in = _mm(hn[None], dproj[None], ta=True, tn=512, out_dtypes=(BF16,), name=f"{tag}_dwin")[0][0]
    grads = dict(norm_mix=dg, hyb_in=dw_in, hyb_out=dw_out, conv_w=dconv_w, conv_b=dconv_b, ln_g=dln_g, ln_b=dln_b,
                 sconv_w=dsconv_w, sconv_b=dsconv_b, dt_bias=g_dtb, a_log=g_alog, d_skip=g_dsk, ssm_norm=g_ng)
    return dh_in, dhb_in, grads


def _bias_epi(acc, row):
    return (acc + row,)


def _res_bias_epi(acc, res, row):
    return (res + acc + row,)


def _att_fwd(h, w, tables, tag):
    cos, sin, rot = tables
    hn = _rms(h, w["norm_mix"], f"{tag}_rms")
    qkv = _mm(hn[None], w["qkv"][None], tb=True, tn=512, epi=_bias_epi, rows=(w["b_qkv"],), name=f"{tag}_qkv")[0][0]
    o = _attn_fwd(qkv, cos, sin, w["sinks"], rot, f"{tag}_core")
    out = _mm(o[None], w["w_o"][None], epi=_res_bias_epi, extras=(h[None],), rows=(w["b_o"],), name=f"{tag}_out")[0][0]
    return out, (h, hn, qkv, o)


def _att_bwd(dh, dhb, saved, w, tables, tag):
    cos, sin, rot = tables
    h, hn, qkv, o = saved
    t = h.shape[0]
    tm = 512
    do = _mm(dhb[None], w["w_o"][None], tb=True, out_dtypes=(BF16,), name=f"{tag}_do")[0][0]
    dw_o = _mm(o[None], dhb[None], ta=True, out_dtypes=(BF16,), name=f"{tag}_dwo")[0][0]
    db_o = _rowop_bwd(lambda xx, bb: (xx + bb,), [(dh, _tok(D, tm))], [w["b_o"]], [(dh, _tok(D, tm))], [], [],
                      grid=(t // tm,), name=f"{tag}_dbo")[0]
    dqkv, db_qkv, dsinks = _attn_bwd(qkv, do, cos, sin, w["sinks"], rot, f"{tag}_dcore")
    dh_in, dhb_in, dg = _mm_drms(dqkv[None], w["qkv"][None], h, w["norm_mix"], dh, f"{tag}_dhn", 512, tb=False)
    dw_qkv = _mm(dqkv[None], hn[None], ta=True, tm=512, out_dtypes=(BF16,), name=f"{tag}_dwqkv")[0][0]
    grads = dict(norm_mix=dg, qkv=dw_qkv, b_qkv=db_qkv, sinks=dsinks, w_o=dw_o, b_o=db_o)
    return dh_in, dhb_in, grads


def _rope_tables(t):
    inv = ROPE_THETA ** (-jnp.arange(0, 64, 2, dtype=F32) / 64)
    ang = jnp.arange(t, dtype=F32)[:, None] * inv[None, :]
    cos, sin = jnp.tile(jnp.cos(ang), (1, 4)), jnp.tile(jnp.sin(ang), (1, 4))
    rot = np.zeros((LANE, LANE), np.float32)
    for j in range(LANE):
        if j % 64 < 32:
            rot[j + 32, j] = -1.0
        else:
            rot[j - 32, j] = 1.0
    return cos, sin, jnp.asarray(rot)


def _local_step(x, p, tgt, layers, final_norm):
    _restart_chain()
    tables = _rope_tables(x.shape[0])
    pb = p.astype(BF16)
    h, saved = x, []
    for i, w in enumerate(layers):
        h, s = _layer_fwd(i, h, w, pb[i], tables)
        saved.append(s)
    loss, dh, dhb, d_final = _loss_head(h, tgt, final_norm, "loss_head")
    grads = [None] * len(layers)
    for i in reversed(range(len(layers))):
        dh, dhb, head = _layer_bwd_head(i, dh, dhb, saved[i], layers[i], pb[i])
        dh, dhb, tail = _layer_bwd_tail(i, dh, dhb, saved[i], layers[i], tables)
        grads[i] = {**head, **tail}
    return loss[0, 0], dh, grads, d_final


def _layer_fwd(i, h, w, pb, tables, deps=()):
    s = {}
    h, s["ffn1"] = _ffn_fwd(h, w["norm_ffn1"], w["ffn1_in"], w["ffn1_out"], f"l{i}_ffn1", deps=deps)
    if i % 2 == 0:
        h, s["mix"] = _hyb_fwd(h, w, f"l{i}_hyb")
    else:
        h, s["mix"] = _att_fwd(h, w, tables, f"l{i}_att")
    h, s["ffn2"] = _ffn_fwd(h, w["norm_ffn2"], w["ffn2_in"], w["ffn2_out"], f"l{i}_ffn2")
    h, s["ple"] = _ple_fwd(h, w["ple_norm"], pb, w["ple_gate"], w["ple_proj"], f"l{i}_ple")
    return h, s


def _layer_bwd_head(i, dh, dhb, s, w, pb, deps=()):
    g = {}
    dh, dhb, g["ple_norm"], g["ple_gate"], g["ple_proj"] = _ple_bwd(dh, dhb, s["ple"], w["ple_norm"], pb, w["ple_gate"],
                                                                    f"l{i}_ple", deps=deps)
    return dh, dhb, g


def _layer_bwd_tail(i, dh, dhb, s, w, tables, deps=()):
    g = {}
    dh, dhb, g["norm_ffn2"], g["ffn2_in"], g["ffn2_out"] = _ffn_bwd(dh, dhb, s["ffn2"], w["norm_ffn2"], w["ffn2_in"],
                                                                    w["ffn2_out"], f"l{i}_ffn2", deps=deps)
    if i % 2 == 0:
        dh, dhb, gm = _hyb_bwd(dh, dhb, s["mix"], w, f"l{i}_hyb")
    else:
        dh, dhb, gm = _att_bwd(dh, dhb, s["mix"], w, tables, f"l{i}_att")
    g.update(gm)
    dh, dhb, g["norm_ffn1"], g["ffn1_in"], g["ffn1_out"] = _ffn_bwd(dh, dhb, s["ffn1"], w["norm_ffn1"], w["ffn1_in"],
                                                                    w["ffn1_out"], f"l{i}_ffn1")
    return dh, dhb, g


def _cols(g):
    full = jnp.moveaxis(g, 0, -2)
    return full.reshape(*full.shape[:-2], N_DEV * g.shape[-1])


def _uncols(full):
    split = full.reshape(*full.shape[:-1], N_DEV, full.shape[-1] // N_DEV)
    return jnp.moveaxis(split, -2, 0)


def _lane_pad(v):
    return jnp.pad(v, ((0, 0), (0, LANE - v.shape[1])))


def _build_layers(gw, gs, rep):
    return [_build_layer(i, gw, gs, rep) for i in range(2)]


def _build_layer(i, gw, gs, rep, parts=("ffn1", "mix", "ffn2", "ple")):
    w = {}
    for f in ("ffn1", "ffn2"):
        if f in parts:
            w[f"norm_{f}"] = rep[f"norm_{f}"][i][None]
            w[f"{f}_in"] = gw[f"{f}_w_in", i]
            w[f"{f}_out"] = gw[f"{f}_w_out", i].reshape(4, FF_SHARD, D)
    if "ple" in parts:
        w["ple_norm"] = rep["ple_norm"][i][None]
        w["ple_gate"] = gw["ple_gate_w", i].reshape(D, D)
        w["ple_proj"] = _cols(gw["ple_proj_w", i])
    if "mix" not in parts:
        return w
    w["norm_mix"] = rep["norm_mix"][i][None]
    if i == 0:
        w["hyb_in"] = jnp.pad(_cols(gw["hyb_w_in", 0]), ((0, 0), (0, HYB_PAD - HYB_IN)))
        w["hyb_out"] = gw["hyb_w_out", 0].reshape(2, D, D)
        w["conv_w"] = _cols(gs["conv_dw_w"][:, 0])
        w["sconv_w"] = _cols(gs["ssm_conv_w"][:, 0])
        w["conv_b"], w["ln_g"], w["ln_b"] = rep["conv_dw_b"], rep["conv_ln_g"], rep["conv_ln_b"]
        w["sconv_b"], w["ssm_norm"] = rep["ssm_conv_b"], rep["ssm_norm"]
        w["dt_bias"], w["a_log"] = _lane_pad(rep["ssm_dt_bias"]), _lane_pad(rep["ssm_a_log"])
        w["d_skip"] = jnp.repeat(rep["ssm_d"], D // SSM_HEADS, axis=1)
    else:
        w["qkv"] = gw["att_w_qkv", 0].reshape(-1, D)
        w["w_o"] = gw["att_w_o", 0].reshape(D, D)
        w["b_qkv"] = gs["att_b_qkv"][:, 0].reshape(1, -1)
        w["b_o"] = gs["att_b_o"][:, 0].reshape(1, -1)
        w["sinks"] = _lane_pad(rep["att_sinks"])
    return w


def _big_grads(i, g):
    big = {}
    for f in ("ffn1", "ffn2"):
        if f"{f}_in" in g:
            big[f"{f}_w_in", i] = g[f"{f}_in"]
            big[f"{f}_w_out", i] = g[f"{f}_out"].reshape(N_DEV, D_FF // N_DEV, D)
    if "ple_gate" in g:
        big["ple_gate_w", i] = g["ple_gate"].reshape(N_DEV, D // N_DEV, D)
        big["ple_proj_w", i] = _uncols(g["ple_proj"])
    if "hyb_in" in g:
        big["hyb_w_in", 0] = _uncols(g["hyb_in"][:, :HYB_IN])
        big["hyb_w_out", 0] = g["hyb_out"].reshape(N_DEV, 2 * D // N_DEV, D)
    if "qkv" in g:
        big["att_w_qkv", 0] = g["qkv"].reshape(N_DEV, -1, D)
        big["att_w_o", 0] = g["w_o"].reshape(N_DEV, D // N_DEV, D)
    return big


def _collect_grads(grads, d_final):
    g0, g1 = grads
    big, small = {**_big_grads(0, g0), **_big_grads(1, g1)}, {}
    for f in ("ffn1", "ffn2"):
        small[f"norm_{f}"] = jnp.concatenate([g[f"norm_{f}"] for g in grads], axis=0)
    small["norm_mix"] = jnp.concatenate([g["norm_mix"] for g in grads], axis=0)
    small["ple_norm"] = jnp.concatenate([g["ple_norm"] for g in grads], axis=0)
    small["conv_dw_w"] = g0["conv_w"][None]
    small["conv_dw_b"], small["conv_ln_g"], small["conv_ln_b"] = g0["conv_b"], g0["ln_g"], g0["ln_b"]
    small["ssm_conv_w"] = g0["sconv_w"][None]
    small["ssm_conv_b"], small["ssm_norm"] = g0["sconv_b"], g0["ssm_norm"]
    small["ssm_dt_bias"], small["ssm_a_log"] = g0["dt_bias"][:, :SSM_HEADS], g0["a_log"][:, :SSM_HEADS]
    small["ssm_d"] = g0["d_skip"].reshape(1, SSM_HEADS, D // SSM_HEADS).sum(axis=-1)
    small["att_b_qkv"], small["att_b_o"] = g1["b_qkv"], g1["b_o"]
    small["att_sinks"] = g1["sinks"][:, :SSM_HEADS]
    small["final_norm"] = d_final[0]
    return big, small


MESH = pl.DeviceIdType.MESH


def _place():
    return lax.axis_index("x"), lax.axis_index("y"), lax.axis_index("c")


def _all_gather(blocks, space, name):
    nb = len(blocks)

    def body(*refs):
        x_refs, out_refs, (send_sems, recv_sems, local_sem) = refs[:nb], refs[nb:2 * nb], refs[2 * nb:]
        x, y, c = _place()
        me, sibling = (x, y, c), (x, y, 1 - c)
        chips = [(1 - x, y), (x, 1 - y), (1 - x, 1 - y)]

        def copies(k, blk, to, own=False):
            idx = 4 * blk[0] + 2 * blk[1] + blk[2]
            return [pltpu.make_async_remote_copy(src_ref=x_ref if own else out_ref.at[idx], dst_ref=out_ref.at[idx],
                                                 send_sem=send_sems.at[k, b], recv_sem=recv_sems.at[k, b], device_id=to,
                                                 device_id_type=MESH) for b, (x_ref, out_ref) in enumerate(zip(x_refs, out_refs))]

        mine = [pltpu.make_async_copy(x_ref, out_ref.at[4 * x + 2 * y + c], local_sem.at[b])
                for b, (x_ref, out_ref) in enumerate(zip(x_refs, out_refs))]
        first = copies(0, me, sibling, own=True)
        for j, chip in enumerate(chips):
            first += copies(1 + j, me, (*chip, c), own=True)
        for cp in mine + first:
            cp.start()
        passed = []
        for j, chip in enumerate(chips):
            for cp in copies(1 + j, (*chip, c), me):
                cp.wait_recv()
            onward = copies(4 + j, (*chip, c), sibling)
            for cp in onward:
                cp.start()
            passed += onward
        for cp in copies(0, sibling, me):
            cp.wait_recv()
        for j, chip in enumerate(chips):
            for cp in copies(4 + j, (*chip, 1 - c), me):
                cp.wait_recv()
        for cp in first + passed:
            cp.wait_send()
        for cp in mine:
            cp.wait()

    spec = pl.BlockSpec(memory_space=space)
    return _pallas(
        body, name=name, out_shape=[SDS((N_DEV,) + b.shape, b.dtype) for b in blocks],
        in_specs=[spec] * nb, out_specs=[spec] * nb,
        scratch_shapes=[pltpu.SemaphoreType.DMA((7, nb)), pltpu.SemaphoreType.DMA((7, nb)), pltpu.SemaphoreType.DMA((nb,))],
    )(*blocks)


def _pair_exchange(parts, name):
    nb = len(parts)

    def body(*refs):
        p_refs, got_refs, (send_sems, recv_sems) = refs[:nb], refs[nb:2 * nb], refs[2 * nb:]
        x, y, c = _place()
        copies = [pltpu.make_async_remote_copy(src_ref=p_ref.at[2 * q + (1 - c)], dst_ref=got_ref.at[q],
                                               send_sem=send_sems.at[q, b], recv_sem=recv_sems.at[q, b], device_id=(x, y, 1 - c),
                                               device_id_type=MESH)
                  for q in range(4) for b, (p_ref, got_ref) in enumerate(zip(p_refs, got_refs))]
        for cp in copies:
            cp.start()
        for cp in copies:
            cp.wait_recv()
        for cp in copies:
            cp.wait_send()

    hbm = pl.BlockSpec(memory_space=pltpu.HBM)
    return _pallas(
        body, name=name, out_shape=[SDS((4,) + p.shape[1:], p.dtype) for p in parts], in_specs=[hbm] * nb, out_specs=[hbm] * nb,
        scratch_shapes=[pltpu.SemaphoreType.DMA((4, nb)), pltpu.SemaphoreType.DMA((4, nb))],
    )(*parts)


HBM_SPEC = pl.BlockSpec(memory_space=pltpu.HBM)
SEM_SPEC = pl.BlockSpec(memory_space=pltpu.SEMAPHORE)
EFFECT = pltpu.SideEffectType.DATAFLOW_SIDE_EFFECTING


def _plan_descriptors(plan, srcs, lands, send_sems, recv_sems, local_sems, arriving):
    remote, local = plan(*_place())

    def pick(si, slot):
        ref = lands[si[1]] if isinstance(si, tuple) else srcs[si]
        return ref if slot is None else ref.at[slot]

    rem = [pltpu.make_async_remote_copy(src_ref=pick(si, ss), dst_ref=lands[li].at[rs if arriving else ds],
                                        send_sem=send_sems.at[k], recv_sem=recv_sems.at[k], device_id=dev, device_id_type=MESH)
           for k, (si, ss, li, ds, dev, rs) in enumerate(remote)]
    loc = [pltpu.make_async_copy(pick(si, ss), lands[li].at[ds], local_sems.at[k])
           for k, (si, ss, li, ds) in enumerate(local)]
    return rem, loc


def _plan_counts(plan):
    remote, local = plan(0, 0, 0)
    return len(remote), max(len(local), 1)


def _exchange_start(srcs, land_shapes, plan, name, lands=None):
    ns, nl = len(srcs), len(lands if lands is not None else land_shapes)
    n_remote, n_local = _plan_counts(plan)
    if lands is None:
        lands = [pltpu.with_memory_space_constraint(lax.empty(s.shape, s.dtype), pltpu.HBM) for s in land_shapes]
    lands = list(lands)
    srcs = [pltpu.with_memory_space_constraint(s, pltpu.HBM) for s in srcs]

    def body(*refs):
        src_refs, land_refs = refs[:ns], refs[ns:ns + nl]
        send_sems, recv_sems, local_sems = refs[ns + nl:ns + nl + 3]
        token = refs[-1]
        rem, loc = _plan_descriptors(plan, src_refs, land_refs, send_sems, recv_sems, local_sems, arriving=False)
        for cp in loc + rem:
            cp.start()
        token[...] = jnp.zeros_like(token)

    outs = _pallas(
        body, name=name,
        out_shape=[pltpu.SemaphoreType.DMA((n_remote,)), pltpu.SemaphoreType.DMA((n_remote,)), pltpu.SemaphoreType.DMA((n_local,))]
        + [pltpu.HBM(a.shape, a.dtype) for a in srcs + lands] + [SDS((8, LANE), F32)],
        in_specs=[HBM_SPEC] * (ns + nl),
        out_specs=[SEM_SPEC] * 3 + [HBM_SPEC] * (ns + nl) + [pl.BlockSpec(memory_space=pltpu.VMEM)],
        input_output_aliases={i: 3 + i for i in range(ns + nl)},
        compiler_params=pltpu.CompilerParams(has_side_effects=EFFECT),
    )(*srcs, *lands)
    return (outs[:3], outs[3:3 + ns], outs[3 + ns:3 + ns + nl]), outs[-1]


def _exchange_wait(state, after, plan, name):
    sems, srcs, lands = state
    ns, nl = len(srcs), len(lands)

    def body(*refs):
        src_refs, land_refs = refs[:ns], refs[ns:ns + nl]
        send_sems, recv_sems, local_sems = refs[ns + nl:ns + nl + 3]
        rem, loc = _plan_descriptors(plan, src_refs, land_refs, send_sems, recv_sems, local_sems, arriving=True)
        for cp in rem:
            cp.wait_send()
            cp.wait_recv()
        for cp in loc:
            cp.wait()

    outs = _pallas(
        body, name=name, out_shape=[pltpu.HBM(a.shape, a.dtype) for a in list(srcs) + list(lands)],
        in_specs=[HBM_SPEC] * (ns + nl) + [SEM_SPEC] * 3 + [ANY_SPEC] * (after is not None), out_specs=[HBM_SPEC] * (ns + nl),
        input_output_aliases={i: i for i in range(ns + nl)},
        compiler_params=pltpu.CompilerParams(has_side_effects=EFFECT),
    )(*srcs, *lands, *sems, *([after] if after is not None else []))
    return outs[:ns], outs[ns:]


def _gather_plan(nb):
    def plan(x, y, c):
        me = 4 * x + 2 * y + c
        peers = [(x, y, 1 - c), (1 - x, y, c), (x, 1 - y, c), (1 - x, 1 - y, c)]
        remote = [(b, None, b, me, peer, 4 * peer[0] + 2 * peer[1] + peer[2]) for b in range(nb) for peer in peers]
        return remote, [(b, None, b, me) for b in range(nb)]
    return plan


def _relay_plan(nb):
    def plan(x, y, c):
        chips = [(1 - x, y), (x, 1 - y), (1 - x, 1 - y)]
        remote = [(("land", b), 4 * cx + 2 * cy + c, b, 4 * cx + 2 * cy + c, (x, y, 1 - c), 4 * cx + 2 * cy + (1 - c))
                  for b in range(nb) for cx, cy in chips]
        return remote, []
    return plan


def _pair_plan(nb):
    def plan(x, y, c):
        return [(b, 2 * q + (1 - c), b, q, (x, y, 1 - c), q) for b in range(nb) for q in range(4)], []
    return plan


def _chip_plan(nb):
    def plan(x, y, c):
        own = 2 * x + y
        chips = [(1 - x, y), (x, 1 - y), (1 - x, 1 - y)]
        remote = [(b, 2 * cx + cy, b, own, (cx, cy, c), 2 * cx + cy) for b in range(nb) for cx, cy in chips]
        return remote, [(b, own, b, own) for b in range(nb)]
    return plan


def _row_tile(r, cap=4608):
    return max(d for d in range(16, min(r, cap) + 1, 16) if r % d == 0)


def _pair_add(parts, got, core, name):
    _, r, cdim = parts.shape
    tr = _row_tile(r)

    def body(core_ref, p_ref, g_ref, o_ref):
        o_ref[...] = (p_ref[...].astype(F32) + g_ref[...].astype(F32)).astype(o_ref.dtype)

    return pl.pallas_call(
        body, name=name, out_shape=SDS((4, r, cdim), BF16),
        grid_spec=pltpu.PrefetchScalarGridSpec(
            num_scalar_prefetch=1, grid=(4, r // tr),
            in_specs=[pl.BlockSpec((None, tr, cdim), lambda q, i, core_ref: (2 * q + core_ref[0], i, 0)),
                      pl.BlockSpec((None, tr, cdim), lambda q, i, core_ref: (q, i, 0))],
            out_specs=pl.BlockSpec((None, tr, cdim), lambda q, i, core_ref: (q, i, 0))),
        compiler_params=_params(("parallel", "parallel")),
    )(core, parts, got)


def _chip_exchange(sums, name):
    nb = len(sums)

    def body(*refs):
        b_refs, out_refs, (send_sems, recv_sems, local_sem) = refs[:nb], refs[nb:2 * nb], refs[2 * nb:]
        x, y, c = _place()
        own = 2 * x + y
        chips = [(1 - x, y), (x, 1 - y), (1 - x, 1 - y)]

        def copies(k, chip, src_slot, dst_slot):
            return [pltpu.make_async_remote_copy(src_ref=b_ref.at[src_slot], dst_ref=out_ref.at[dst_slot],
                                                 send_sem=send_sems.at[k, b], recv_sem=recv_sems.at[k, b], device_id=(*chip, c),
                                                 device_id_type=MESH) for b, (b_ref, out_ref) in enumerate(zip(b_refs, out_refs))]

        mine = [pltpu.make_async_copy(b_ref.at[own], out_ref.at[own], local_sem.at[b])
                for b, (b_ref, out_ref) in enumerate(zip(b_refs, out_refs))]
        sends = []
        for k, chip in enumerate(chips):
            sends += copies(k, chip, 2 * chip[0] + chip[1], own)
        for cp in mine + sends:
            cp.start()
        for k, chip in enumerate(chips):
            for cp in copies(k, chip, own, 2 * chip[0] + chip[1]):
                cp.wait_recv()
        for cp in sends:
            cp.wait_send()
        for cp in mine:
            cp.wait()

    hbm = pl.BlockSpec(memory_space=pltpu.HBM)
    return _pallas(
        body, name=name, out_shape=[SDS(s.shape, s.dtype) for s in sums], in_specs=[hbm] * nb, out_specs=[hbm] * nb,
        scratch_shapes=[pltpu.SemaphoreType.DMA((3, nb)), pltpu.SemaphoreType.DMA((3, nb)), pltpu.SemaphoreType.DMA((nb,))],
    )(*sums)


def _sum_slots(parts, name):
    nj, r, cdim = parts.shape
    tr = _row_tile(r)

    def body(p_ref, o_ref):
        acc = p_ref[0].astype(F32)
        for j in range(1, nj):
            acc = acc + p_ref[j].astype(F32)
        o_ref[...] = acc

    return _pallas(
        body, name=name, out_shape=SDS((r, cdim), F32), grid=(r // tr,),
        in_specs=[pl.BlockSpec((nj, tr, cdim), lambda i: (0, i, 0))], out_specs=pl.BlockSpec((tr, cdim), lambda i: (i, 0)),
        compiler_params=_params(("parallel",)),
    )(parts)


def _adamw_update(wv, gv, mv, vv):
    nm = ADAM_B1 * mv + (1.0 - ADAM_B1) * gv
    nv = ADAM_B2 * vv + (1.0 - ADAM_B2) * (gv * gv)
    m_hat = nm / (1.0 - ADAM_B1 ** ADAM_STEP)
    v_hat = nv / (1.0 - ADAM_B2 ** ADAM_STEP)
    return -ADAM_LR * (m_hat / (jnp.sqrt(v_hat) + ADAM_EPS) + ADAM_WD * wv), nm, nv


def _adamw_summed(w, m, v, by_chip, name):
    nl, r, cdim = w.shape
    tr = _row_tile(r, 512)
    nblk = r // tr

    def body(*refs):
        chip_refs, (w_ref, m_ref, v_ref, g_ref, d_ref, nm_ref, nv_ref) = refs[:nl], refs[nl:]
        layer = pl.program_id(0)
        gv = None
        for ll, c_ref in enumerate(chip_refs):
            s = c_ref[0].astype(F32)
            for q in range(1, 4):
                s = s + c_ref[q].astype(F32)
            gv = s if gv is None else jnp.where(layer == ll, s, gv)
        g_ref[...] = gv
        d_ref[...], nm_ref[...], nv_ref[...] = _adamw_update(w_ref[...], gv, m_ref[...], v_ref[...])

    def chip_map(ll):
        return lambda l, i: (0, jnp.where(l == ll, i, jnp.where(l > ll, nblk - 1, 0)), 0)

    spec = pl.BlockSpec((None, tr, cdim), lambda l, i: (l, i, 0))
    return _pallas(
        body, name=name, grid=(nl, nblk),
        in_specs=[pl.BlockSpec((4, tr, cdim), chip_map(ll)) for ll in range(nl)] + [spec] * 3,
        out_specs=[spec] * 4, out_shape=[SDS((nl, r, cdim), F32)] * 4,
        compiler_params=_params(("arbitrary", "arbitrary")),
    )(*by_chip, w, m, v)


def _adamw(w, g, m, v, name):
    shape = w.shape
    cdim = shape[-1]
    w2, g2, m2, v2 = (a.reshape(-1, cdim) for a in (w, g, m, v))
    r = w2.shape[0]
    tr = next(d for d in (512, 352, 256, 128, 64, 32, 16, 8, r) if r % d == 0)

    def body(w_ref, g_ref, m_ref, v_ref, d_ref, nm_ref, nv_ref):
        d_ref[...], nm_ref[...], nv_ref[...] = _adamw_update(w_ref[...], g_ref[...], m_ref[...], v_ref[...])

    spec = pl.BlockSpec((tr, cdim), lambda i: (i, 0))
    outs = _pallas(
        body, name=name, grid=(r // tr,), in_specs=[spec] * 4, out_specs=[spec] * 3, out_shape=[SDS((r, cdim), F32)] * 3,
        compiler_params=_params(("parallel",)),
    )(w2, g2, m2, v2)
    return tuple(o.reshape(shape) for o in outs)


WEIGHTS = ("norm_ffn1", "ffn1_w_in", "ffn1_w_out", "norm_mix", "norm_ffn2", "ffn2_w_in", "ffn2_w_out", "ple_norm", "ple_gate_w",
           "ple_proj_w", "hyb_w_in", "conv_dw_w", "conv_dw_b", "conv_ln_g", "conv_ln_b", "ssm_conv_w", "ssm_conv_b", "ssm_dt_bias",
           "ssm_a_log", "ssm_d", "ssm_norm", "hyb_w_out", "att_w_qkv", "att_b_qkv", "att_sinks", "att_w_o", "att_b_o", "final_norm")
BIG = ("ffn1_w_in", "ffn1_w_out", "ffn2_w_in", "ffn2_w_out", "ple_gate_w", "ple_proj_w", "hyb_w_in", "hyb_w_out", "att_w_qkv",
       "att_w_o")
SMALL_SHARDED = {"conv_dw_w": 2, "ssm_conv_w": 2, "att_b_qkv": 1, "att_b_o": 1}
SMALL = tuple(n for n in WEIGHTS if n not in BIG)
TRANSPOSED = ("ffn1_w_in", "ffn2_w_in", "att_w_qkv")
PACK_ROWS = 16


def _pack(arrays, lead=0):
    pieces = []
    for a in arrays:
        flat = a.reshape(*a.shape[:lead], -1)
        size = flat.shape[-1]
        padded = -(-size // (PACK_ROWS * LANE)) * PACK_ROWS * LANE
        flat = jnp.pad(flat, [(0, 0)] * lead + [(0, padded - size)])
        pieces.append(flat.reshape(*a.shape[:lead], padded // LANE, LANE))
    return jnp.concatenate(pieces, axis=lead)


def _unpack(buf, shapes, lead=0):
    out, row = [], 0
    for shape in shapes:
        size = math.prod(shape)
        rows = -(-size // (PACK_ROWS * LANE)) * PACK_ROWS
        piece = lax.slice_in_dim(buf, row, row + rows, axis=lead)
        piece = piece.reshape(*buf.shape[:lead], rows * LANE)
        out.append(lax.slice_in_dim(piece, 0, size, axis=lead).reshape(*buf.shape[:lead], *shape))
        row += rows
    return out


def kernel(x, p, norm_ffn1, ffn1_w_in, ffn1_w_out, norm_mix, norm_ffn2, ffn2_w_in, ffn2_w_out, ple_norm, ple_gate_w, ple_proj_w, hyb_w_in, conv_dw_w, conv_dw_b, conv_ln_g, conv_ln_b, ssm_conv_w, ssm_conv_b, ssm_dt_bias, ssm_a_log, ssm_d, ssm_norm, hyb_w_out, att_w_qkv, att_b_qkv, att_sinks, att_w_o, att_b_o, final_norm, loss_target, m_norm_ffn1, m_ffn1_w_in, m_ffn1_w_out, m_norm_mix, m_norm_ffn2, m_ffn2_w_in, m_ffn2_w_out, m_ple_norm, m_ple_gate_w, m_ple_proj_w, m_hyb_w_in, m_conv_dw_w, m_conv_dw_b, m_conv_ln_g, m_conv_ln_b, m_ssm_conv_w, m_ssm_conv_b, m_ssm_dt_bias, m_ssm_a_log, m_ssm_d, m_ssm_norm, m_hyb_w_out, m_att_w_qkv, m_att_b_qkv, m_att_sinks, m_att_w_o, m_att_b_o, m_final_norm, v_norm_ffn1, v_ffn1_w_in, v_ffn1_w_out, v_norm_mix, v_norm_ffn2, v_ffn2_w_in, v_ffn2_w_out, v_ple_norm, v_ple_gate_w, v_ple_proj_w, v_hyb_w_in, v_conv_dw_w, v_conv_dw_b, v_conv_ln_g, v_conv_ln_b, v_ssm_conv_w, v_ssm_conv_b, v_ssm_dt_bias, v_ssm_a_log, v_ssm_d, v_ssm_norm, v_hyb_w_out, v_att_w_qkv, v_att_b_qkv, v_att_sinks, v_att_w_o, v_att_b_o, v_final_norm):
    args = (norm_ffn1, ffn1_w_in, ffn1_w_out, norm_mix, norm_ffn2, ffn2_w_in, ffn2_w_out, ple_norm, ple_gate_w, ple_proj_w, hyb_w_in, conv_dw_w, conv_dw_b, conv_ln_g, conv_ln_b, ssm_conv_w, ssm_conv_b, ssm_dt_bias, ssm_a_log, ssm_d, ssm_norm, hyb_w_out, att_w_qkv, att_b_qkv, att_sinks, att_w_o, att_b_o, final_norm)
    moments_m = (m_norm_ffn1, m_ffn1_w_in, m_ffn1_w_out, m_norm_mix, m_norm_ffn2, m_ffn2_w_in, m_ffn2_w_out, m_ple_norm, m_ple_gate_w, m_ple_proj_w, m_hyb_w_in, m_conv_dw_w, m_conv_dw_b, m_conv_ln_g, m_conv_ln_b, m_ssm_conv_w, m_ssm_conv_b, m_ssm_dt_bias, m_ssm_a_log, m_ssm_d, m_ssm_norm, m_hyb_w_out, m_att_w_qkv, m_att_b_qkv, m_att_sinks, m_att_w_o, m_att_b_o, m_final_norm)
    moments_v = (v_norm_ffn1, v_ffn1_w_in, v_ffn1_w_out, v_norm_mix, v_norm_ffn2, v_ffn2_w_in, v_ffn2_w_out, v_ple_norm, v_ple_gate_w, v_ple_proj_w, v_hyb_w_in, v_conv_dw_w, v_conv_dw_b, v_conv_ln_g, v_conv_ln_b, v_ssm_conv_w, v_ssm_conv_b, v_ssm_dt_bias, v_ssm_a_log, v_ssm_d, v_ssm_norm, v_hyb_w_out, v_att_w_qkv, v_att_b_qkv, v_att_sinks, v_att_w_o, v_att_b_o, v_final_norm)
    w = dict(zip(WEIGHTS, args))
    m = dict(zip(WEIGHTS, moments_m))
    v = dict(zip(WEIGHTS, moments_v))
    cx, cy, cc = _place()
    me = 4 * cx + 2 * cy + cc

    core = jnp.reshape(cc, (1,)).astype(jnp.int32)
    layer_of = lambda n, i: 1 if n.startswith("att_") else i
    keys = [[(n, i) for n in BIG for i in range(w[n].shape[0]) if layer_of(n, i) == layer] for layer in range(2)]

    first = [key for key in keys[0] if key[0].startswith("ffn1")]
    mixer = [key for key in keys[0] if key[0].startswith("hyb")]
    rest0 = [key for key in keys[0] if key not in first + mixer]
    gw, by_chip = {}, {}
    view = lambda a, n: jnp.swapaxes(a, 1, 2) if n in TRANSPOSED else a
    block = lambda n, i: view(w[n], n)[i].astype(BF16)

    def gather_later(group, name):
        blocks = [block(n, i) for n, i in group]
        plan, relay_plan = _gather_plan(len(blocks)), _relay_plan(len(blocks))
        state, _ = _exchange_start(blocks, [SDS((N_DEV,) + b.shape, BF16) for b in blocks], plan, f"{name}_start")
        stage = {}

        def relay():
            _, landed = _exchange_wait(state, None, plan, f"{name}_wait")
            stage["relay"], _ = _exchange_start([], None, relay_plan, f"{name}_relay_start", lands=landed)

        def arrived():
            gw.update(zip(group, _exchange_wait(stage["relay"], None, relay_plan, f"{name}_relay_wait")[1]))

        return relay, arrived

    def reduce_later(group, big, name):
        pair_plan, chip_plan = _pair_plan(len(group)), _chip_plan(len(group))
        parts = [big[key] for key in group]
        pair, token = _exchange_start(parts, [SDS((4,) + pt.shape[1:], BF16) for pt in parts], pair_plan, f"{name}_pair_start")
        stage = {}

        def middle(after):
            thru, got = _exchange_wait(pair, after, pair_plan, f"{name}_pair_wait")
            sums = [_pair_add(pt, gt, core, f"grads_pair_add_{n}_{i}") for pt, gt, (n, i) in zip(thru, got, group)]
            stage["chip"], chip_token = _exchange_start(sums, [SDS(s.shape, BF16) for s in sums], chip_plan, f"{name}_chip_start")
            return chip_token

        def finish(after):
            by_chip.update(zip(group, _exchange_wait(stage["chip"], after, chip_plan, f"{name}_chip_wait")[1]))

        return token, middle, finish

    _restart_chain()
    gw["ffn1_w_in", 0], gathered_small = _all_gather([block("ffn1_w_in", 0), _pack([w[n] for n in SMALL_SHARDED])],
                                                      pltpu.HBM, "gather_weights_first")
    early_relay, early_arrived = gather_later([("ffn1_w_out", 0)], "gather_weights_early")
    mixer_relay, mixer_arrived = gather_later(mixer, "gather_weights_mixer")
    rest0_relay, rest0_arrived = gather_later(rest0, "gather_weights_rest")
    layer1_relay, layer1_arrived = gather_later(keys[1], "gather_weights_l1")
    gs = dict(zip(SMALL_SHARDED, _unpack(gathered_small, [w[n].shape for n in SMALL_SHARDED], lead=1)))
    rep = {n: w[n] for n in SMALL if n not in SMALL_SHARDED}

    tables = _rope_tables(x.shape[1])
    pb = p[:, 0].astype(BF16)
    w0, s0 = {}, {}

    def first_w_out():
        early_relay()
        early_arrived()
        w0.update(_build_layer(0, gw, gs, rep, parts=("ffn1",)))
        return w0["ffn1_out"]

    h, s0["ffn1"] = _ffn_fwd(x[0], rep["norm_ffn1"][0][None], gw["ffn1_w_in", 0], first_w_out, "l0_ffn1")
    mixer_relay()
    mixer_arrived()
    w0.update(_build_layer(0, gw, gs, rep, parts=("mix",)))
    h, s0["mix"] = _hyb_fwd(h, w0, "l0_hyb", after_in=rest0_relay)
    layer1_relay()
    rest0_arrived()
    w0.update(_build_layer(0, gw, gs, rep, parts=("ffn2", "ple")))
    h, s0["ffn2"] = _ffn_fwd(h, w0["norm_ffn2"], w0["ffn2_in"], w0["ffn2_out"], "l0_ffn2")
    h, s0["ple"] = _ple_fwd(h, w0["ple_norm"], pb[0], w0["ple_gate"], w0["ple_proj"], "l0_ple")
    layer1_arrived()
    w1 = _build_layer(1, gw, gs, rep)
    h, s1 = _layer_fwd(1, h, w1, pb[1], tables)
    loss, dh, dhb, d_final = _loss_head(h, loss_target[0], final_norm[None], "loss_head")
    loss = lax.psum(loss[0, 0], ("x", "y", "c"))

    dh, dhb, head1 = _layer_bwd_head(1, dh, dhb, s1, w1, pb[1])
    dh, dhb, tail1 = _layer_bwd_tail(1, dh, dhb, s1, w1, tables)
    grads1 = {**head1, **tail1}
    l1_token, l1_middle, l1_finish = reduce_later(keys[1], _big_grads(1, grads1), "grads_l1")
    dh, dhb, grads0 = _layer_bwd_head(0, dh, dhb, s0, w0, pb[0], deps=(l1_token,))
    dh, dhb, grads0["norm_ffn2"], grads0["ffn2_in"], grads0["ffn2_out"] = _ffn_bwd(
        dh, dhb, s0["ffn2"], w0["norm_ffn2"], w0["ffn2_in"], w0["ffn2_out"], "l0_ffn2", deps=(l1_middle(dh),))
    dh, dhb, mixer_grads = _hyb_bwd(dh, dhb, s0["mix"], w0, "l0_hyb")
    grads0.update(mixer_grads)
    l0_token, l0_middle, l0_finish = reduce_later(mixer + rest0, _big_grads(0, grads0), "grads_l0")
    last = {}

    def reduce_first(dw_in, dw_out):
        token, middle, last["finish"] = reduce_later(first, _big_grads(0, dict(ffn1_in=dw_in, ffn1_out=dw_out)), "grads_first")
        middle(token)

    dx, dhb, grads0["norm_ffn1"], grads0["ffn1_in"], grads0["ffn1_out"] = _ffn_bwd(
        dh, dhb, s0["ffn1"], w0["norm_ffn1"], w0["ffn1_in"], w0["ffn1_out"], "l0_ffn1", deps=(l0_token,),
        hook=lambda dpre: (l0_middle(dpre),), weights_hook=reduce_first)
    l1_finish(dx)
    l0_finish(dx)
    last["finish"](dx)
    _, small = _collect_grads([grads0, grads1], d_final)
    small_shapes = [small[n].shape for n in SMALL]
    all_small = _all_gather([_pack([small[n] for n in SMALL])], pltpu.VMEM, "gather_small_grads")[0]
    g = dict(zip(SMALL, _unpack(_sum_slots(all_small, "small_grads_sum"), small_shapes)))
    for n, axis in SMALL_SHARDED.items():
        g[n] = lax.dynamic_slice_in_dim(g[n], me * w[n].shape[axis], w[n].shape[axis], axis=axis)

    delta, new_m, new_v = {}, {}, {}
    for n in BIG:
        outs = _adamw_summed(view(w[n], n), view(m[n], n), view(v[n], n), [by_chip[n, i] for i in range(w[n].shape[0])],
                             f"adamw_{n}")
        g[n], delta[n], new_m[n], new_v[n] = (view(o, n) for o in outs)
    packed = [_pack([d[n] for n in SMALL]) for d in (w, g, m, v)]
    shapes = [w[n].shape for n in SMALL]
    for d, buf in zip((delta, new_m, new_v), _adamw(*packed, "adamw_small")):
        d.update(zip(SMALL, _unpack(buf, shapes)))
    return (loss, dx[None], *[g[n] for n in WEIGHTS], *[delta[n] for n in WEIGHTS], *[new_m[n] for n in WEIGHTS],
            *[new_v[n] for n in WEIGHTS])
```

```python
import functools
import math

import numpy as np
import jax
import jax.numpy as jnp
from jax import lax
from jax.experimental import pallas as pl
from jax.experimental.pallas import tpu as pltpu

F32, BF16 = jnp.float32, jnp.bfloat16
HI = lax.Precision.HIGHEST
SDS = jax.ShapeDtypeStruct

N_DEV = 8
D = 1024
D_FF = 2816
FF_SHARD = 2 * D_FF // N_DEV
PLE_DIM = 256
EPS = 1e-6
CONV_W = 31
SSM_CONV = 4
SSM_HEADS = 16
SSM_XBC = 1536
CHUNK = 128
HYB_IN = 4624
HYB_PAD = 5120
DT_COL = 4608
N_PAIR = 8
ROPE_THETA = 10000.0
LANE = 128
VMEM_LIMIT = 56 * 1024 * 1024

ADAM_LR, ADAM_B1, ADAM_B2, ADAM_EPS, ADAM_WD, ADAM_STEP = 0.001, 0.9, 0.999, 1e-08, 0.01, 10


def _params(sem):
    return pltpu.CompilerParams(dimension_semantics=sem, vmem_limit_bytes=VMEM_LIMIT)


_CHAIN = []


def _restart_chain():
    _CHAIN.clear()


def _pallas(body, *, in_specs, **kw):
    def run(*args):
        n, dep = len(args), list(_CHAIN)

        def chained(*refs):
            return body(*refs[:n], *refs[n + len(dep):])

        outs = pl.pallas_call(chained, in_specs=list(in_specs) + [pl.BlockSpec(memory_space=pl.ANY)] * len(dep), **kw)(*args, *dep)
        _CHAIN[:] = [outs[-1] if isinstance(outs, (list, tuple)) else outs]
        return outs

    return run


def _mm(a, b, *, ta=False, tb=False, reduce_j=False, out_dtypes=(F32,), tm=1024, tn=1024, tk=1024,
        epi=None, extras=(), rows=(), deps=(), sums=0, name):
    ja, jb = a.shape[0], b.shape[0]
    nj = max(ja, jb)
    jo = 1 if reduce_j else nj
    m, k = (a.shape[2], a.shape[1]) if ta else (a.shape[1], a.shape[2])
    n = b.shape[1] if tb else b.shape[2]
    assert (b.shape[2] if tb else b.shape[1]) == k and ja in (1, nj) and jb in (1, nj)
    tm, tn, tk = min(tm, m), min(tn, n), min(tk, k)
    assert m % tm == 0 and n % tn == 0 and k % tk == 0, (name, m, n, k, tm, tn, tk)
    assert not sums or (tn == n and (reduce_j or nj == 1))
    nk = k // tk
    steps = nk * (nj if reduce_j else 1)
    ne, nr, no = len(extras), len(rows), len(out_dtypes)

    def a_map(i, c, j, kk):
        return (j if ja > 1 else 0, kk, i) if ta else (j if ja > 1 else 0, i, kk)

    def b_map(i, c, j, kk):
        return (j if jb > 1 else 0, c, kk) if tb else (j if jb > 1 else 0, kk, c)

    def o_map(i, c, j, kk):
        return (0 if reduce_j else j, i, c)

    dims = (((0 if ta else 1,), (1 if tb else 0,)), ((), ()))

    def body(a_ref, b_ref, *rest):
        ex, rw = rest[:ne], rest[ne:ne + nr]
        outs = rest[ne + nr + len(deps):ne + nr + len(deps) + no]
        sum_refs = rest[ne + nr + len(deps) + no:ne + nr + len(deps) + no + sums]
        first_tile = pl.program_id(0) == 0

        def product():
            return lax.dot_general(a_ref[...], b_ref[...], dims, preferred_element_type=F32)

        def finish(acc):
            res = epi(acc, *[e[...] for e in ex], *[r[...] for r in rw]) if epi else (acc,)
            for o, r in zip(outs, res):
                o[...] = r.astype(o.dtype)
            for s_ref, r in zip(sum_refs, res[no:]):
                @pl.when(first_tile)
                def _(s_ref=s_ref, r=r):
                    s_ref[...] = r

                @pl.when(jnp.logical_not(first_tile))
                def _(s_ref=s_ref, r=r):
                    s_ref[...] += r

        if steps == 1:
            finish(product())
            return
        acc_ref = rest[-1]
        kk = pl.program_id(3)
        step = pl.program_id(2) * nk + kk if reduce_j else kk

        @pl.when(step == 0)
        def _():
            acc_ref[...] = product()

        @pl.when(jnp.logical_and(step > 0, step < steps - 1))
        def _():
            acc_ref[...] += product()

        @pl.when(step == steps - 1)
        def _():
            finish(acc_ref[...] + product())

    o_spec = pl.BlockSpec((None, tm, tn), o_map)
    row_spec = pl.BlockSpec((1, tn), lambda i, c, j, kk: (0, c))
    return _pallas(
        body, name=name, grid=(m // tm, n // tn, nj, nk),
        in_specs=[pl.BlockSpec((None, tk, tm) if ta else (None, tm, tk), a_map),
                  pl.BlockSpec((None, tn, tk) if tb else (None, tk, tn), b_map)]
        + [o_spec] * ne + [row_spec] * nr + [ANY_SPEC] * len(deps),
        out_specs=[o_spec] * no + [row_spec] * sums,
        out_shape=[SDS((jo, m, n), dt) for dt in out_dtypes] + [SDS((1, n), F32)] * sums,
        scratch_shapes=[pltpu.VMEM((tm, tn), F32)] if steps > 1 else [],
        compiler_params=_params(("arbitrary" if sums else "parallel", "parallel", "arbitrary", "arbitrary")),
    )(a, b, *extras, *rows, *deps)


def _whole(p):
    return pl.BlockSpec(p.shape, lambda *_: (0,) * p.ndim)


ANY_SPEC = pl.BlockSpec(memory_space=pl.ANY)


def _rowop(fn, tiles, params, outs, *, grid, name, deps=()):
    nin = len(tiles) + len(params)

    def body(*refs):
        res = fn(*[r[...].astype(F32) for r in refs[:nin]])
        for r, o in zip(refs[nin + len(deps):], res):
            r[...] = o.astype(r.dtype)

    return _pallas(
        body, name=name, grid=grid,
        in_specs=[s for _, s in tiles] + [_whole(p) for p in params] + [ANY_SPEC] * len(deps),
        out_specs=[s for _, _, s in outs], out_shape=[SDS(sh, dt) for sh, dt, _ in outs],
        compiler_params=_params(("parallel",) * len(grid)),
    )(*[t for t, _ in tiles], *params, *deps)


def _rowop_bwd(fn, tiles, params, cots, wrt, gouts, *, grid, name, adds=(), deps=()):
    nt, npar, nc, na = len(tiles), len(params), len(cots), len(adds)
    nin = nt + npar
    flat = [i for grp in wrt for i in grp]
    n_gout = sum(len(dts) for _, dts, _ in gouts)

    def body(*refs):
        vals = [r[...].astype(F32) for r in refs[:nin]]
        cvals = [r[...].astype(F32) for r in refs[nin:nin + nc]]
        avals = [r[...].astype(F32) for r in refs[nin + nc:nin + nc + na]]
        orefs = refs[nin + nc + na + len(deps):]
        diff_idx = flat + list(range(nt, nin))

        def f(*dv):
            full = list(vals)
            for i, v in zip(diff_idx, dv):
                full[i] = v
            return fn(*full)

        _, vjp = jax.vjp(f, *[vals[i] for i in diff_idx])
        grads = vjp(tuple(cvals))
        tile_g, par_g = list(grads[:len(flat)]), grads[len(flat):]
        group_g, at = [], 0
        for grp in wrt:
            members = tile_g[at:at + len(grp)]
            at += len(grp)
            group_g.append(members[0] if len(grp) == 1 else jnp.stack(members, axis=0))
        for av in avals:
            group_g[0] = group_g[0] + av
        o = 0
        for g, (_, dts, _) in zip(group_g, gouts):
            for _ in dts:
                orefs[o][...] = g.astype(orefs[o].dtype)
                o += 1
        first = functools.reduce(jnp.logical_and, [pl.program_id(ax) == 0 for ax in range(len(grid))])
        for r, g in zip(orefs[n_gout:], par_g):
            @pl.when(first)
            def _(r=r, g=g):
                r[...] = g

            @pl.when(jnp.logical_not(first))
            def _(r=r, g=g):
                r[...] += g

    out_specs, out_shape = [], []
    for sh, dts, spec in gouts:
        for dt in dts:
            out_specs.append(spec)
            out_shape.append(SDS(sh, dt))
    for p in params:
        out_specs.append(_whole(p))
        out_shape.append(SDS(p.shape, F32))
    return _pallas(
        body, name=name, grid=grid,
        in_specs=[s for _, s in tiles] + [_whole(p) for p in params] + [s for _, s in cots] + [s for _, s in adds]
        + [ANY_SPEC] * len(deps),
        out_specs=out_specs, out_shape=out_shape,
        compiler_params=_params(("arbitrary",) * len(grid)),
    )(*[t for t, _ in tiles], *params, *[c for c, _ in cots], *[a for a, _ in adds], *deps)


def _tok(c, tm, col=0):
    return pl.BlockSpec((tm, c), lambda i, col=col: (i, col))


def _rms_fn(h, g):
    return (h * lax.rsqrt(jnp.mean(h * h, axis=-1, keepdims=True) + EPS) * g,)


def _lnswish_fn(u, g, b):
    mu = jnp.mean(u, axis=-1, keepdims=True)
    xc = u - mu
    y = xc * lax.rsqrt(jnp.mean(xc * xc, axis=-1, keepdims=True) + EPS) * g + b
    return (y * jax.nn.sigmoid(y),)


def _ple_fn(z, e):
    return (jax.nn.sigmoid(z) * e,)


def _rms(h, g, name, tm=512, deps=()):
    t = h.shape[0]
    return _rowop(_rms_fn, [(h, _tok(D, tm))], [g], [((t, D), BF16, _tok(D, tm))], grid=(t // tm,), name=name, deps=deps)[0]


def _drms_epi(dn, h, dres, g):
    _, vjp = jax.vjp(_rms_fn, h, g)
    dh, dg = vjp((dn,))
    dh = dh + dres
    return dh, dh, dg


def _mm_drms(a, b, h, g, dres, name, tk, tb=True):
    dh, dhb, dg = _mm(a, b, tb=tb, reduce_j=a.shape[0] > 1, tm=512, tk=tk, epi=_drms_epi, extras=(h[None], dres[None]),
                      rows=(g,), out_dtypes=(F32, BF16), sums=1, name=name)
    return dh[0], dhb[0], dg


def _conv_geometry(width):
    pad = 32 if width > 8 else 8
    return pad, pad - (width - 1)


def _fill_shifts(xpad_ref, sh_ref, t, shifts):
    for r in shifts:
        sh_ref[r, :, :] = xpad_ref[pl.ds(r, t + 32), :]


def _dwconv(xs, w, b, *, width, glu, silu, cb, name):
    t = xs[0][0].shape[0]
    c = w.shape[1]
    pad, off = _conv_geometry(width)
    shifts = sorted({(k + off) % 8 for k in range(width)})
    ch = 32

    def body(*refs):
        x_refs, (w_ref, b_ref, o_ref, xpad_ref, sh_ref) = refs[:len(xs)], refs[len(xs):]
        u = x_refs[0][...] * jax.nn.sigmoid(x_refs[1][...]) if glu else x_refs[0][...]
        xpad_ref[pl.ds(0, pad), :] = jnp.zeros((pad, cb), F32)
        xpad_ref[pl.ds(pad, t), :] = u
        xpad_ref[pl.ds(pad + t, 40 - pad), :] = jnp.zeros((40 - pad, cb), F32)
        _fill_shifts(xpad_ref, sh_ref, t, shifts)

        def chunk(i, carry):
            t0 = pl.multiple_of(i * ch, ch)
            acc = jnp.broadcast_to(b_ref[...], (ch, cb))
            for k in range(width):
                q, r = divmod(k + off, 8)
                acc = acc + w_ref[pl.ds(k, 1), :] * sh_ref[r, pl.ds(t0 + 8 * q, ch), :]
            o_ref[pl.ds(t0, ch), :] = acc * jax.nn.sigmoid(acc) if silu else acc
            return carry

        lax.fori_loop(0, t // ch, chunk, 0)

    return _pallas(
        body, name=name, grid=(c // cb,),
        in_specs=[pl.BlockSpec((t, cb), lambda i, o=o: (0, o + i)) for _, o in xs]
        + [pl.BlockSpec((width, cb), lambda i: (0, i)), pl.BlockSpec((1, cb), lambda i: (0, i))],
        out_specs=pl.BlockSpec((t, cb), lambda i: (0, i)), out_shape=SDS((t, c), F32),
        scratch_shapes=[pltpu.VMEM((t + 40, cb), F32), pltpu.VMEM((8, t + 32, cb), F32)],
        compiler_params=_params(("parallel",)),
    )(*[x for x, _ in xs], w, b)


def _dwconv_bwd(xs, w, b, dy, *, width, glu, silu, cb, name):
    t = xs[0][0].shape[0]
    c = w.shape[1]
    pad, off = _conv_geometry(width)
    shifts = sorted({(k + off) % 8 for k in range(width)})
    shifts_t = sorted({mm % 8 for mm in range(width)})
    ch = 32
    nx = len(xs)

    def body(*refs):
        x_refs = refs[:nx]
        w_ref, b_ref, dy_ref = refs[nx:nx + 3]
        dx_refs = refs[nx + 3:nx + 3 + nx]
        dw_ref, db_ref, xpad_ref, sh_ref, dc_ref = refs[nx + 3 + nx:]
        u = x_refs[0][...] * jax.nn.sigmoid(x_refs[1][...]) if glu else x_refs[0][...]
        xpad_ref[pl.ds(0, pad), :] = jnp.zeros((pad, cb), F32)
        xpad_ref[pl.ds(pad, t), :] = u
        xpad_ref[pl.ds(pad + t, 40 - pad), :] = jnp.zeros((40 - pad, cb), F32)
        _fill_shifts(xpad_ref, sh_ref, t, shifts)

        if silu:
            def act_chunk(i, carry):
                t0 = pl.multiple_of(i * ch, ch)
                acc = jnp.broadcast_to(b_ref[...], (ch, cb))
                for k in range(width):
                    q, r = divmod(k + off, 8)
                    acc = acc + w_ref[pl.ds(k, 1), :] * sh_ref[r, pl.ds(t0 + 8 * q, ch), :]
                sg = jax.nn.sigmoid(acc)
                dc_ref[pl.ds(t0, ch), :] = dy_ref[pl.ds(t0, ch), :] * (sg * (1.0 + acc * (1.0 - sg)))
                return carry

            lax.fori_loop(0, t // ch, act_chunk, 0)
        else:
            dc_ref[...] = dy_ref[...]

        def dw_chunk(i, accs):
            t0 = pl.multiple_of(i * ch, ch)
            new = list(accs)
            for s in range(ch // 8):
                d = dc_ref[pl.ds(t0 + 8 * s, 8), :]
                for k in range(width):
                    q, r = divmod(k + off, 8)
                    new[k] = new[k] + d * sh_ref[r, pl.ds(t0 + 8 * (q + s), 8), :]
                new[width] = new[width] + d
            return tuple(new)

        accs = lax.fori_loop(0, t // ch, dw_chunk, tuple(jnp.zeros((8, cb), F32) for _ in range(width + 1)))
        for k in range(width):
            dw_ref[pl.ds(k, 1), :] = jnp.sum(accs[k], axis=0, keepdims=True)
        db_ref[...] = jnp.sum(accs[width], axis=0, keepdims=True)

        xpad_ref[pl.ds(0, t), :] = dc_ref[...]
        xpad_ref[pl.ds(t, 40), :] = jnp.zeros((40, cb), F32)
        _fill_shifts(xpad_ref, sh_ref, t, shifts_t)

        def dx_chunk(i, carry):
            t0 = pl.multiple_of(i * ch, ch)
            acc = jnp.zeros((ch, cb), F32)
            for mm in range(width):
                q, r = divmod(mm, 8)
                acc = acc + w_ref[pl.ds(width - 1 - mm, 1), :] * sh_ref[r, pl.ds(t0 + 8 * q, ch), :]
            if glu:
                val, gate = x_refs[0][pl.ds(t0, ch), :], x_refs[1][pl.ds(t0, ch), :]
                sg = jax.nn.sigmoid(gate)
                dx_refs[0][pl.ds(t0, ch), :] = (acc * sg).astype(BF16)
                dx_refs[1][pl.ds(t0, ch), :] = (acc * val * sg * (1.0 - sg)).astype(BF16)
            else:
                dx_refs[0][pl.ds(t0, ch), :] = acc.astype(BF16)
            return carry

        lax.fori_loop(0, t // ch, dx_chunk, 0)

    col = pl.BlockSpec((t, cb), lambda i: (0, i))
    return _pallas(
        body, name=name, grid=(c // cb,),
        in_specs=[pl.BlockSpec((t, cb), lambda i, o=o: (0, o + i)) for _, o in xs]
        + [pl.BlockSpec((width, cb), lambda i: (0, i)), pl.BlockSpec((1, cb), lambda i: (0, i)), col],
        out_specs=[col] * nx + [pl.BlockSpec((width, cb), lambda i: (0, i)), pl.BlockSpec((1, cb), lambda i: (0, i))],
        out_shape=[SDS((t, c), BF16)] * nx + [SDS((width, c), F32), SDS((1, c), F32)],
        scratch_shapes=[pltpu.VMEM((t + 40, cb), F32), pltpu.VMEM((8, t + 32, cb), F32), pltpu.VMEM((t, cb), F32)],
        compiler_params=_params(("parallel",)),
    )(*[x for x, _ in xs], w, b, dy)


_DIMS = {"nn": (((1,), (0,)), ((), ())), "nt": (((1,), (1,)), ((), ())), "tn": (((0,), (0,)), ((), ()))}


def _raw_dot(a, b, mode):
    return lax.dot_general(a.astype(BF16), b.astype(BF16), _DIMS[mode], preferred_element_type=F32)


@functools.partial(jax.custom_vjp, nondiff_argnums=(2,))
def _bdot(a, b, mode):
    return _raw_dot(a, b, mode)


def _bdot_fwd(a, b, mode):
    return _raw_dot(a, b, mode), (a, b)


def _bdot_bwd(mode, res, g):
    a, b = res
    if mode == "nn":
        return _raw_dot(g, b, "nt"), _raw_dot(a, g, "tn")
    if mode == "nt":
        return _raw_dot(g, b, "nn"), _raw_dot(g, a, "tn")
    return _raw_dot(b, g, "nt"), _raw_dot(a, g, "nn")


_bdot.defvjp(_bdot_fwd, _bdot_bwd)


def _iota(shape, axis):
    return lax.broadcasted_iota(jnp.int32, shape, axis)


def _half_masks():
    left = (_iota((1, LANE), 1) < 64).astype(F32)
    return left, 1.0 - left


def _split3(a):
    a1 = a.astype(BF16)
    r1 = a - a1.astype(F32)
    a2 = r1.astype(BF16)
    return a1, a2, (r1 - a2.astype(F32)).astype(BF16)


def _exact_dot(a, e, mode):
    return sum(lax.dot_general(piece, e, _DIMS[mode], preferred_element_type=F32) for piece in _split3(a))


@jax.custom_vjp
def _spread(a, e):
    return _exact_dot(a, e, "nn")


_spread.defvjp(lambda a, e: (_exact_dot(a, e, "nn"), e), lambda e, g: (_exact_dot(g, e, "nt"), jnp.zeros_like(e)))


@jax.custom_vjp
def _running_sum(tri, a):
    return sum(lax.dot_general(tri, piece, _DIMS["nn"], preferred_element_type=F32) for piece in _split3(a))


_running_sum.defvjp(
    lambda tri, a: (sum(lax.dot_general(tri, piece, _DIMS["nn"], preferred_element_type=F32) for piece in _split3(a)), tri),
    lambda tri, g: (jnp.zeros_like(tri), sum(lax.dot_general(tri, piece, _DIMS["tn"], preferred_element_type=F32)
                                             for piece in _split3(g))))


def _ssd_chunk(state, xa, dtr, z, dtb, alog, dskf, ng):
    xs, bm, cm = xa[:, :D], xa[:, D:D + 256], xa[:, D + 256:]
    left, right = _half_masks()
    expand = (_iota((LANE, D), 1) // 64 == _iota((LANE, D), 0)).astype(BF16)
    li, si = _iota((CHUNK, CHUNK), 0), _iota((CHUNK, CHUNK), 1)
    tril = li >= si
    dt16 = jax.nn.softplus(dtr + dtb)
    adt = dt16 * (-jnp.exp(alog))
    dtf = _spread(dt16, expand)
    cs16 = _running_sum(tril.astype(BF16), adt)
    csf = _spread(cs16, expand)
    totf = jnp.sum(jnp.where(_iota((CHUNK, D), 0) == CHUNK - 1, csf, 0.0), axis=0, keepdims=True)
    cst = cs16.T
    xdt = xs * dtf
    ys, new_state = [], []
    for g in range(2):
        bg, cg = bm[:, LANE * g:LANE * (g + 1)], cm[:, LANE * g:LANE * (g + 1)]
        cb = _bdot(cg, bg, "nt")
        for q in range(4):
            pr = 4 * g + q
            decay = []
            for h in (2 * pr, 2 * pr + 1):
                col = jnp.sum(jnp.where(si == h, cs16, 0.0), axis=1, keepdims=True)
                row = jnp.sum(jnp.where(li == h, cst, 0.0), axis=0, keepdims=True)
                decay.append(cb * jnp.exp(jnp.where(tril, col - row, -jnp.inf)))
            xp = xdt[:, LANE * pr:LANE * (pr + 1)]
            y_diag = _bdot(jnp.concatenate(decay, axis=1), jnp.concatenate([xp * left, xp * right], axis=0), "nn")
            csb, tot = csf[:, LANE * pr:LANE * (pr + 1)], totf[:, LANE * pr:LANE * (pr + 1)]
            ys.append(y_diag + _bdot(cg, state[pr], "nn") * jnp.exp(csb))
            new_state.append(state[pr] * jnp.exp(tot) + _bdot(bg, xp * jnp.exp(tot - csb), "tn"))
    y = jnp.concatenate(ys, axis=1)
    y = y + dskf * xs
    y = y * (z * jax.nn.sigmoid(z))
    halves = []
    for g in range(2):
        yg = y[:, 512 * g:512 * (g + 1)]
        halves.append(yg * lax.rsqrt(jnp.mean(yg * yg, axis=-1, keepdims=True) + EPS))
    return jnp.concatenate(halves, axis=1) * ng, jnp.stack(new_state, axis=0)


def _ssd_specs(t, rev):
    nc = t // CHUNK
    ix = (lambda c: nc - 1 - c) if rev else (lambda c: c)
    return nc, ix


def _ssd_fwd(xa, proj, dtb, alog, dsk, ng, name):
    t = xa.shape[0]
    nc, ix = _ssd_specs(t, False)

    def body(xa_ref, dt_ref, z_ref, dtb_ref, alog_ref, dsk_ref, ng_ref, y_ref, st_ref, carry_ref):
        @pl.when(pl.program_id(0) == 0)
        def _():
            carry_ref[...] = jnp.zeros_like(carry_ref)

        st_ref[...] = carry_ref[...]
        y, new = _ssd_chunk(carry_ref[...], xa_ref[...], dt_ref[...], z_ref[...], dtb_ref[...], alog_ref[...],
                            dsk_ref[...], ng_ref[...])
        y_ref[...] = y.astype(BF16)
        carry_ref[...] = new

    small = [dtb, alog, dsk, ng]
    return _pallas(
        body, name=name, grid=(nc,),
        in_specs=[pl.BlockSpec((CHUNK, SSM_XBC), lambda c: (c, 0)),
                  pl.BlockSpec((CHUNK, LANE), lambda c: (c, DT_COL // LANE)),
                  pl.BlockSpec((CHUNK, D), lambda c: (c, 2))] + [_whole(p) for p in small],
        out_specs=[pl.BlockSpec((CHUNK, D), lambda c: (c, 0)), pl.BlockSpec((None, N_PAIR, LANE, LANE), lambda c: (c, 0, 0, 0))],
        out_shape=[SDS((t, D), BF16), SDS((nc, N_PAIR, LANE, LANE), F32)],
        scratch_shapes=[pltpu.VMEM((N_PAIR, LANE, LANE), F32)],
        compiler_params=_params(("arbitrary",)),
    )(xa, proj, proj, *small)


def _ssd_bwd(xa, proj, states, dy, dtb, alog, dsk, ng, name):
    t = xa.shape[0]
    nc, ix = _ssd_specs(t, True)

    def body(xa_ref, dt_ref, z_ref, st_ref, dy_ref, dtb_ref, alog_ref, dsk_ref, ng_ref,
             dxa_ref, ddt_ref, dz_ref, gdtb_ref, galog_ref, gdsk_ref, gng_ref, carry_ref):
        first = pl.program_id(0) == 0

        @pl.when(first)
        def _():
            carry_ref[...] = jnp.zeros_like(carry_ref)

        args = (st_ref[...], xa_ref[...], dt_ref[...], z_ref[...], dtb_ref[...], alog_ref[...], dsk_ref[...], ng_ref[...])
        _, vjp = jax.vjp(_ssd_chunk, *args)
        ds, dxa, ddt, dz, gdtb, galog, gdsk, gng = vjp((dy_ref[...], carry_ref[...]))
        carry_ref[...] = ds
        dxa_ref[...] = dxa
        ddt_ref[...] = ddt.astype(BF16)
        dz_ref[...] = dz.astype(BF16)
        for r, g in ((gdtb_ref, gdtb), (galog_ref, galog), (gdsk_ref, gdsk), (gng_ref, gng)):
            @pl.when(first)
            def _(r=r, g=g):
                r[...] = g

            @pl.when(jnp.logical_not(first))
            def _(r=r, g=g):
                r[...] += g

    small = [dtb, alog, dsk, ng]
    return _pallas(
        body, name=name, grid=(nc,),
        in_specs=[pl.BlockSpec((CHUNK, SSM_XBC), lambda c: (ix(c), 0)),
                  pl.BlockSpec((CHUNK, LANE), lambda c: (ix(c), DT_COL // LANE)),
                  pl.BlockSpec((CHUNK, D), lambda c: (ix(c), 2)),
                  pl.BlockSpec((None, N_PAIR, LANE, LANE), lambda c: (ix(c), 0, 0, 0)),
                  pl.BlockSpec((CHUNK, D), lambda c: (ix(c), 0))] + [_whole(p) for p in small],
        out_specs=[pl.BlockSpec((CHUNK, SSM_XBC), lambda c: (ix(c), 0)), pl.BlockSpec((CHUNK, LANE), lambda c: (ix(c), 0)),
                   pl.BlockSpec((CHUNK, D), lambda c: (ix(c), 0))] + [_whole(p) for p in small],
        out_shape=[SDS((t, SSM_XBC), F32), SDS((t, LANE), BF16), SDS((t, D), BF16)] + [SDS(p.shape, F32) for p in small],
        scratch_shapes=[pltpu.VMEM((N_PAIR, LANE, LANE), F32)],
        compiler_params=_params(("arbitrary",)),
    )(xa, proj, proj, states, dy, *small)


def _attn_block(q, kv_prev, kv_cur, cq, sq, ck, sk, sinks, rot, first_block):
    left, right = _half_masks()
    k2 = jnp.concatenate([kv_prev[:, :256], kv_cur[:, :256]], axis=0)
    v2 = jnp.concatenate([kv_prev[:, 256:], kv_cur[:, 256:]], axis=0)
    ri, ci = _iota((LANE, LANE), 0), _iota((LANE, LANE), 1)
    dup = [((ri < 64) & (ci % 64 == ri)).astype(BF16), ((ri >= 64) & (ci % 64 == ri - 64)).astype(BF16)]

    rot16 = rot.astype(BF16)

    def rope(tt, c, s):
        return tt * c + _spread(tt, rot16) * s

    kd, vd = [], []
    for j in range(4):
        sl = slice(LANE * (j // 2), LANE * (j // 2 + 1))
        kd.append(_bdot(rope(k2[:, sl], ck, sk), dup[j % 2], "nn"))
        vd.append(_bdot(v2[:, sl], dup[j % 2], "nn"))
    qi, si = _iota((2 * CHUNK, 2 * CHUNK), 0) % CHUNK, _iota((2 * CHUNK, 2 * CHUNK), 1)
    valid = (si > qi) & (si <= qi + CHUNK) & jnp.logical_or(si >= CHUNK, jnp.logical_not(first_block))
    upper = _iota((2 * CHUNK, 1), 0) < CHUNK
    lanes = _iota((1, LANE), 1)
    outs = []
    for pr in range(N_PAIR):
        qr = rope(q[:, LANE * pr:LANE * (pr + 1)], cq, sq)
        lg = _bdot(jnp.concatenate([qr * left, qr * right], axis=0), kd[pr // 2], "nt") * 0.125
        lg = jnp.where(valid, lg, -jnp.inf)
        s1 = jnp.sum(jnp.where(lanes == 2 * pr, sinks, 0.0), axis=1, keepdims=True)
        s2 = jnp.sum(jnp.where(lanes == 2 * pr + 1, sinks, 0.0), axis=1, keepdims=True)
        sink = jnp.where(upper, s1, s2)
        mx = lax.stop_gradient(jnp.maximum(jnp.max(lg, axis=-1, keepdims=True), sink))
        e = jnp.exp(lg - mx)
        probs = e / (jnp.sum(e, axis=-1, keepdims=True) + jnp.exp(sink - mx))
        o2 = _bdot(probs, vd[pr // 2], "nn")
        outs.append(o2[:CHUNK] * left + o2[CHUNK:] * right)
    return jnp.concatenate(outs, axis=1)


def _attn_fwd(qkv, cos, sin, sinks, rot, name):
    t = qkv.shape[0]
    nb = t // CHUNK

    def body(q_ref, kvp_ref, kvc_ref, cq_ref, sq_ref, cp_ref, sp_ref, sinks_ref, rot_ref, o_ref):
        ck = jnp.concatenate([cp_ref[...], cq_ref[...]], axis=0)
        sk = jnp.concatenate([sp_ref[...], sq_ref[...]], axis=0)
        o_ref[...] = _attn_block(q_ref[...], kvp_ref[...], kvc_ref[...], cq_ref[...], sq_ref[...], ck, sk,
                                 sinks_ref[...], rot_ref[...], pl.program_id(0) == 0).astype(BF16)

    prev = lambda n: jnp.maximum(n - 1, 0)
    return _pallas(
        body, name=name, grid=(nb,),
        in_specs=[pl.BlockSpec((CHUNK, D), lambda n: (n, 0)),
                  pl.BlockSpec((CHUNK, 512), lambda n: (prev(n), 2)), pl.BlockSpec((CHUNK, 512), lambda n: (n, 2)),
                  pl.BlockSpec((CHUNK, LANE), lambda n: (n, 0)), pl.BlockSpec((CHUNK, LANE), lambda n: (n, 0)),
                  pl.BlockSpec((CHUNK, LANE), lambda n: (prev(n), 0)), pl.BlockSpec((CHUNK, LANE), lambda n: (prev(n), 0)),
                  _whole(sinks), _whole(rot)],
        out_specs=pl.BlockSpec((CHUNK, D), lambda n: (n, 0)), out_shape=SDS((t, D), BF16),
        compiler_params=_params(("parallel",)),
    )(qkv, qkv, qkv, cos, sin, cos, sin, sinks, rot)


def _attn_bwd(qkv, do, cos, sin, sinks, rot, name):
    t = qkv.shape[0]
    nb = t // CHUNK

    def body(q_ref, kvp_ref, kvc_ref, do_ref, cq_ref, sq_ref, cp_ref, sp_ref, sinks_ref, rot_ref,
             dq_ref, dkv_ref, dbq_ref, dbkv_ref, dsink_ref, carry_ref):
        n = pl.program_id(0)

        @pl.when(n == 0)
        def _():
            carry_ref[...] = jnp.zeros_like(carry_ref)
            dbq_ref[...] = jnp.zeros_like(dbq_ref)
            dbkv_ref[...] = jnp.zeros_like(dbkv_ref)
            dsink_ref[...] = jnp.zeros_like(dsink_ref)

        @pl.when(n < nb)
        def _():
            ck = jnp.concatenate([cp_ref[...], cq_ref[...]], axis=0)
            sk = jnp.concatenate([sp_ref[...], sq_ref[...]], axis=0)
            f = lambda q, kvp, kvc, s: _attn_block(q, kvp, kvc, cq_ref[...], sq_ref[...], ck, sk, s, rot_ref[...], n == 0)
            _, vjp = jax.vjp(f, q_ref[...], kvp_ref[...], kvc_ref[...], sinks_ref[...])
            dq, dkvp, dkvc, ds = vjp(do_ref[...].astype(F32))
            done = carry_ref[...] + dkvp
            dq_ref[...] = dq.astype(BF16)
            dkv_ref[...] = done.astype(BF16)
            dbq_ref[...] += jnp.sum(dq, axis=0, keepdims=True)
            dsink_ref[...] += ds
            carry_ref[...] = dkvc

            @pl.when(n > 0)
            def _():
                dbkv_ref[...] += jnp.sum(done, axis=0, keepdims=True)

        @pl.when(n == nb)
        def _():
            done = carry_ref[...]
            dkv_ref[...] = done.astype(BF16)
            dbkv_ref[...] += jnp.sum(done, axis=0, keepdims=True)

    cur = lambda n: jnp.minimum(n, nb - 1)
    prev = lambda n: jnp.maximum(jnp.minimum(n, nb - 1) - 1, 0)
    fin = lambda n: jnp.maximum(n - 1, 0)
    outs = _pallas(
        body, name=name, grid=(nb + 1,),
        in_specs=[pl.BlockSpec((CHUNK, D), lambda n: (cur(n), 0)),
                  pl.BlockSpec((CHUNK, 512), lambda n: (prev(n), 2)), pl.BlockSpec((CHUNK, 512), lambda n: (cur(n), 2)),
                  pl.BlockSpec((CHUNK, D), lambda n: (cur(n), 0)),
                  pl.BlockSpec((CHUNK, LANE), lambda n: (cur(n), 0)), pl.BlockSpec((CHUNK, LANE), lambda n: (cur(n), 0)),
                  pl.BlockSpec((CHUNK, LANE), lambda n: (prev(n), 0)), pl.BlockSpec((CHUNK, LANE), lambda n: (prev(n), 0)),
                  _whole(sinks), _whole(rot)],
        out_specs=[pl.BlockSpec((CHUNK, D), lambda n: (cur(n), 0)), pl.BlockSpec((CHUNK, 512), lambda n: (fin(n), 0)),
                   pl.BlockSpec((1, D), lambda n: (0, 0)), pl.BlockSpec((1, 512), lambda n: (0, 0)), _whole(sinks)],
        out_shape=[SDS((t, D), BF16), SDS((t, 512), BF16), SDS((1, D), F32), SDS((1, 512), F32), SDS(sinks.shape, F32)],
        scratch_shapes=[pltpu.VMEM((CHUNK, 512), F32)],
        compiler_params=_params(("arbitrary",)),
    )(qkv, qkv, qkv, do, cos, sin, cos, sin, sinks, rot)
    dq, dkv, dbq, dbkv, dsinks = outs
    return jnp.concatenate([dq, dkv], axis=1), jnp.concatenate([dbq, dbkv], axis=1), dsinks


def _loss_head(h, tgt, g, name, tm=512):
    t = h.shape[0]

    def body(h_ref, t_ref, g_ref, loss_ref, dh_ref, dhb_ref, dg_ref):
        def f(hv, gv):
            err = _rms_fn(hv, gv)[0] - t_ref[...]
            return 0.5 * jnp.sum(jnp.mean(err * err, axis=-1, keepdims=True), axis=0, keepdims=True)

        loss, vjp = jax.vjp(f, h_ref[...], g_ref[...])
        dh, dg = vjp(jnp.ones((1, 1), F32))
        dh_ref[...] = dh
        dhb_ref[...] = dh.astype(BF16)
        first = pl.program_id(0) == 0

        @pl.when(first)
        def _():
            loss_ref[...] = loss
            dg_ref[...] = dg

        @pl.when(jnp.logical_not(first))
        def _():
            loss_ref[...] += loss
            dg_ref[...] += dg

    return _pallas(
        body, name=name, grid=(t // tm,),
        in_specs=[_tok(D, tm), _tok(D, tm), _whole(g)],
        out_specs=[pl.BlockSpec((1, 1), lambda i: (0, 0)), _tok(D, tm), _tok(D, tm), _whole(g)],
        out_shape=[SDS((1, 1), F32), SDS((t, D), F32), SDS((t, D), BF16), SDS(g.shape, F32)],
        compiler_params=_params(("arbitrary",)),
    )(h, tgt, g)


def _res_half(acc, res):
    return (res + 0.5 * acc,)


def _res_full(acc, res):
    return (res + acc,)


def _half(acc):
    return (0.5 * acc,)


def _ffn_in(n, w_in, name, tm=1024):
    t = n.shape[0]
    tm = min(tm, t)

    def body(n_ref, w_ref, pre_ref, act_ref):
        a = n_ref[...]
        gate = lax.dot_general(a, w_ref[0], _DIMS["nt"], preferred_element_type=F32)
        up = lax.dot_general(a, w_ref[1], _DIMS["nt"], preferred_element_type=F32)
        pre_ref[0] = gate.astype(BF16)
        pre_ref[1] = up.astype(BF16)
        act_ref[...] = (gate * jax.nn.sigmoid(gate) * up).astype(BF16)

    pair = pl.BlockSpec((2, None, tm, FF_SHARD), lambda i, j: (0, j, i, 0))
    return _pallas(
        body, name=name, grid=(t // tm, 4),
        in_specs=[pl.BlockSpec((tm, D), lambda i, j: (i, 0)), pl.BlockSpec((2, None, FF_SHARD, D), lambda i, j: (0, j, 0, 0))],
        out_specs=[pair, pl.BlockSpec((None, tm, FF_SHARD), lambda i, j: (j, i, 0))],
        out_shape=[SDS((2, 4, t, FF_SHARD), BF16), SDS((4, t, FF_SHARD), BF16)],
        compiler_params=_params(("parallel", "parallel")),
    )(n, w_in.reshape(2, 4, FF_SHARD, D))


def _ffn_dact(dhb, w_out, pre, name, tm=1024, deps=()):
    t = dhb.shape[0]
    tm = min(tm, t)

    def body(d_ref, w_ref, pre_ref, *rest):
        o_ref = rest[-1]
        dact = 0.5 * lax.dot_general(d_ref[...], w_ref[...], _DIMS["nt"], preferred_element_type=F32)
        gate, up = pre_ref[0].astype(F32), pre_ref[1].astype(F32)
        sg = jax.nn.sigmoid(gate)
        o_ref[0] = (dact * up * (sg * (1.0 + gate * (1.0 - sg)))).astype(BF16)
        o_ref[1] = (dact * (gate * sg)).astype(BF16)

    pair = pl.BlockSpec((2, None, tm, FF_SHARD), lambda i, j: (0, j, i, 0))
    return _pallas(
        body, name=name, grid=(t // tm, 4),
        in_specs=[pl.BlockSpec((tm, D), lambda i, j: (i, 0)), pl.BlockSpec((None, FF_SHARD, D), lambda i, j: (j, 0, 0)), pair]
        + [ANY_SPEC] * len(deps),
        out_specs=pair, out_shape=SDS((2, 4, t, FF_SHARD), BF16),
        compiler_params=_params(("parallel", "parallel")),
    )(dhb, w_out, pre, *deps)


def _ffn_fwd(h, g, w_in, w_out, tag, deps=()):
    n = _rms(h, g, f"{tag}_rms", deps=deps)
    pre, act = _ffn_in(n, w_in, f"{tag}_in")
    w_out = w_out() if callable(w_out) else w_out
    out = _mm(act, w_out, reduce_j=True, tk=FF_SHARD, epi=_res_half, extras=(h[None],), name=f"{tag}_out")[0][0]
    return out, (h, n, pre, act)


def _ffn_bwd(dh, dhb, saved, g, w_in, w_out, tag, deps=(), hook=None, weights_hook=None):
    h, n, pre, act = saved
    t = h.shape[0]
    dpre = _ffn_dact(dhb, w_out, pre, f"{tag}_dact", deps=deps).reshape(N_DEV, t, FF_SHARD)
    dw_out = _mm(act, dhb[None], ta=True, tm=FF_SHARD, epi=_half, out_dtypes=(BF16,), deps=hook(dpre) if hook else (),
                 name=f"{tag}_dwout")[0]
    dw_in = _mm(dpre, n[None], ta=True, tm=FF_SHARD, out_dtypes=(BF16,), name=f"{tag}_dwin")[0]
    if weights_hook:
        weights_hook(dw_in, dw_out)
    dh_in, dhb_in, dg = _mm_drms(dpre, w_in, h, g, dh, f"{tag}_dn", FF_SHARD, tb=False)
    return dh_in, dhb_in, dg, dw_in, dw_out


def _ple_fwd(h, g, pb, w_gate, w_proj, tag):
    t = h.shape[0]
    tm = 512
    n = _rms(h, g, f"{tag}_rms")
    e = _mm(pb[None], w_proj[None], name=f"{tag}_proj")[0][0]
    z = _mm(n[None], w_gate[None], name=f"{tag}_gate")[0][0]
    out = _rowop(lambda zz, ee, hh: (hh + _ple_fn(zz, ee)[0],), [(z, _tok(D, tm)), (e, _tok(D, tm)), (h, _tok(D, tm))], [],
                 [((t, D), F32, _tok(D, tm))], grid=(t // tm,), name=f"{tag}_mix")[0]
    return out, (h, n, e, z)


def _ple_bwd(dh, dhb, saved, g, pb, w_gate, tag, deps=()):
    h, n, e, z = saved
    t = h.shape[0]
    tm = 512
    dz, de = _rowop_bwd(_ple_fn, [(z, _tok(D, tm)), (e, _tok(D, tm))], [], [(dh, _tok(D, tm))], [(0,), (1,)],
                        [((t, D), (BF16,), _tok(D, tm)), ((t, D), (BF16,), _tok(D, tm))], grid=(t // tm,), name=f"{tag}_dmix",
                        deps=deps)
    dw_proj = _mm(pb[None], de[None], ta=True, out_dtypes=(BF16,), name=f"{tag}_dwproj")[0][0]
    dw_gate = _mm(n[None], dz[None], ta=True, out_dtypes=(BF16,), name=f"{tag}_dwgate")[0][0]
    dh_in, dhb_in, dg = _mm_drms(dz[None], w_gate[None], h, g, dh, f"{tag}_dn", 1024)
    return dh_in, dhb_in, dg, dw_gate, dw_proj


def _hyb_fwd(h, w, tag, after_in=None):
    t = h.shape[0]
    tm = 512
    hn = _rms(h, w["norm_mix"], f"{tag}_rms")
    proj = _mm(hn[None], w["hyb_in"][None], tn=512, name=f"{tag}_in")[0][0]
    if after_in:
        after_in()
    u1 = _dwconv([(proj, 0), (proj, D // LANE)], w["conv_w"], w["conv_b"], width=CONV_W, glu=True, silu=False, cb=LANE,
                 name=f"{tag}_conv")
    u = _rowop(_lnswish_fn, [(u1, _tok(D, tm))], [w["ln_g"], w["ln_b"]], [((t, D), BF16, _tok(D, tm))], grid=(t // tm,),
               name=f"{tag}_ln")[0]
    xa = _dwconv([(proj, 3 * D // LANE)], w["sconv_w"], w["sconv_b"], width=SSM_CONV, glu=False, silu=True, cb=LANE,
                 name=f"{tag}_sconv")
    y, states = _ssd_fwd(xa, proj, w["dt_bias"], w["a_log"], w["d_skip"], w["ssm_norm"], f"{tag}_ssd")
    mixed = jnp.stack([u, y], axis=0)
    out = _mm(mixed, w["hyb_out"], reduce_j=True, epi=_res_full, extras=(h[None],), name=f"{tag}_out")[0][0]
    return out, (h, hn, proj, u1, xa, states, mixed)


def _hyb_bwd(dh, dhb, saved, w, tag):
    h, hn, proj, u1, xa, states, mixed = saved
    t = h.shape[0]
    tm = 512
    dmix = _mm(dhb[None], w["hyb_out"], tb=True, name=f"{tag}_dmix")[0]
    dw_out = _mm(mixed, dhb[None], ta=True, out_dtypes=(BF16,), name=f"{tag}_dwout")[0]
    du1, dln_g, dln_b = _rowop_bwd(_lnswish_fn, [(u1, _tok(D, tm))], [w["ln_g"], w["ln_b"]], [(dmix[0], _tok(D, tm))], [(0,)],
                                   [((t, D), (F32,), _tok(D, tm))], grid=(t // tm,), name=f"{tag}_dln")
    dval, dgate, dconv_w, dconv_b = _dwconv_bwd([(proj, 0), (proj, D // LANE)], w["conv_w"], w["conv_b"], du1,
                                                width=CONV_W, glu=True, silu=False, cb=LANE, name=f"{tag}_dconv")
    dxa, ddt, dz, g_dtb, g_alog, g_dsk, g_ng = _ssd_bwd(xa, proj, states, dmix[1], w["dt_bias"], w["a_log"], w["d_skip"],
                                                         w["ssm_norm"], f"{tag}_dssd")
    dxbc, dsconv_w, dsconv_b = _dwconv_bwd([(proj, 3 * D // LANE)], w["sconv_w"], w["sconv_b"], dxa, width=SSM_CONV,
                                           glu=False, silu=True, cb=LANE, name=f"{tag}_dsconv")
    dproj = jnp.concatenate([dval, dgate, dz, dxbc, ddt, jnp.zeros((t, HYB_PAD - DT_COL - LANE), BF16)], axis=1)
    dh_in, dhb_in, dg = _mm_drms(dproj[None], w["hyb_in"][None], h, w["norm_mix"], dh, f"{tag}_dhn", 1024)
    dw_in = _mm(hn[None], dproj[None], ta=True, tn=512, out_dtypes=(BF16,), name=f"{tag}_dwin")[0][0]
    grads = dict(norm_mix=dg, hyb_in=dw_in, hyb_out=dw_out, conv_w=dconv_w, conv_b=dconv_b, ln_g=dln_g, ln_b=dln_b,
                 sconv_w=dsconv_w, sconv_b=dsconv_b, dt_bias=g_dtb, a_log=g_alog, d_skip=g_dsk, ssm_norm=g_ng)
    return dh_in, dhb_in, grads


def _bias_epi(acc, row):
    return (acc + row,)


def _res_bias_epi(acc, res, row):
    return (res + acc + row,)


def _att_fwd(h, w, tables, tag):
    cos, sin, rot = tables
    hn = _rms(h, w["norm_mix"], f"{tag}_rms")
    qkv = _mm(hn[None], w["qkv"][None], tb=True, tn=512, epi=_bias_epi, rows=(w["b_qkv"],), name=f"{tag}_qkv")[0][0]
    o = _attn_fwd(qkv, cos, sin, w["sinks"], rot, f"{tag}_core")
    out = _mm(o[None], w["w_o"][None], epi=_res_bias_epi, extras=(h[None],), rows=(w["b_o"],), name=f"{tag}_out")[0][0]
    return out, (h, hn, qkv, o)


def _att_bwd(dh, dhb, saved, w, tables, tag):
    cos, sin, rot = tables
    h, hn, qkv, o = saved
    t = h.shape[0]
    tm = 512
    do = _mm(dhb[None], w["w_o"][None], tb=True, out_dtypes=(BF16,), name=f"{tag}_do")[0][0]
    dw_o = _mm(o[None], dhb[None], ta=True, out_dtypes=(BF16,), name=f"{tag}_dwo")[0][0]
    db_o = _rowop_bwd(lambda xx, bb: (xx + bb,), [(dh, _tok(D, tm))], [w["b_o"]], [(dh, _tok(D, tm))], [], [],
                      grid=(t // tm,), name=f"{tag}_dbo")[0]
    dqkv, db_qkv, dsinks = _attn_bwd(qkv, do, cos, sin, w["sinks"], rot, f"{tag}_dcore")
    dh_in, dhb_in, dg = _mm_drms(dqkv[None], w["qkv"][None], h, w["norm_mix"], dh, f"{tag}_dhn", 512, tb=False)
    dw_qkv = _mm(dqkv[None], hn[None], ta=True, tm=512, out_dtypes=(BF16,), name=f"{tag}_dwqkv")[0][0]
    grads = dict(norm_mix=dg, qkv=dw_qkv, b_qkv=db_qkv, sinks=dsinks, w_o=dw_o, b_o=db_o)
    return dh_in, dhb_in, grads


def _rope_tables(t):
    inv = ROPE_THETA ** (-jnp.arange(0, 64, 2, dtype=F32) / 64)
    ang = jnp.arange(t, dtype=F32)[:, None] * inv[None, :]
    cos, sin = jnp.tile(jnp.cos(ang), (1, 4)), jnp.tile(jnp.sin(ang), (1, 4))
    rot = np.zeros((LANE, LANE), np.float32)
    for j in range(LANE):
        if j % 64 < 32:
            rot[j + 32, j] = -1.0
        else:
            rot[j - 32, j] = 1.0
    return cos, sin, jnp.asarray(rot)


def _local_step(x, p, tgt, layers, final_norm):
    _restart_chain()
    tables = _rope_tables(x.shape[0])
    pb = p.astype(BF16)
    h, saved = x, []
    for i, w in enumerate(layers):
        h, s = _layer_fwd(i, h, w, pb[i], tables)
        saved.append(s)
    loss, dh, dhb, d_final = _loss_head(h, tgt, final_norm, "loss_head")
    grads = [None] * len(layers)
    for i in reversed(range(len(layers))):
        dh, dhb, head = _layer_bwd_head(i, dh, dhb, saved[i], layers[i], pb[i])
        dh, dhb, tail = _layer_bwd_tail(i, dh, dhb, saved[i], layers[i], tables)
        grads[i] = {**head, **tail}
    return loss[0, 0], dh, grads, d_final


def _layer_fwd(i, h, w, pb, tables, deps=()):
    s = {}
    h, s["ffn1"] = _ffn_fwd(h, w["norm_ffn1"], w["ffn1_in"], w["ffn1_out"], f"l{i}_ffn1", deps=deps)
    if i % 2 == 0:
        h, s["mix"] = _hyb_fwd(h, w, f"l{i}_hyb")
    else:
        h, s["mix"] = _att_fwd(h, w, tables, f"l{i}_att")
    h, s["ffn2"] = _ffn_fwd(h, w["norm_ffn2"], w["ffn2_in"], w["ffn2_out"], f"l{i}_ffn2")
    h, s["ple"] = _ple_fwd(h, w["ple_norm"], pb, w["ple_gate"], w["ple_proj"], f"l{i}_ple")
    return h, s


def _layer_bwd_head(i, dh, dhb, s, w, pb, deps=()):
    g = {}
    dh, dhb, g["ple_norm"], g["ple_gate"], g["ple_proj"] = _ple_bwd(dh, dhb, s["ple"], w["ple_norm"], pb, w["ple_gate"],
                                                                    f"l{i}_ple", deps=deps)
    return dh, dhb, g


def _layer_bwd_tail(i, dh, dhb, s, w, tables, deps=()):
    g = {}
    dh, dhb, g["norm_ffn2"], g["ffn2_in"], g["ffn2_out"] = _ffn_bwd(dh, dhb, s["ffn2"], w["norm_ffn2"], w["ffn2_in"],
                                                                    w["ffn2_out"], f"l{i}_ffn2", deps=deps)
    if i % 2 == 0:
        dh, dhb, gm = _hyb_bwd(dh, dhb, s["mix"], w, f"l{i}_hyb")
    else:
        dh, dhb, gm = _att_bwd(dh, dhb, s["mix"], w, tables, f"l{i}_att")
    g.update(gm)
    dh, dhb, g["norm_ffn1"], g["ffn1_in"], g["ffn1_out"] = _ffn_bwd(dh, dhb, s["ffn1"], w["norm_ffn1"], w["ffn1_in"],
                                                                    w["ffn1_out"], f"l{i}_ffn1")
    return dh, dhb, g


def _cols(g):
    full = jnp.moveaxis(g, 0, -2)
    return full.reshape(*full.shape[:-2], N_DEV * g.shape[-1])


def _uncols(full):
    split = full.reshape(*full.shape[:-1], N_DEV, full.shape[-1] // N_DEV)
    return jnp.moveaxis(split, -2, 0)


def _lane_pad(v):
    return jnp.pad(v, ((0, 0), (0, LANE - v.shape[1])))


def _build_layers(gw, gs, rep):
    return [_build_layer(i, gw, gs, rep) for i in range(2)]


def _build_layer(i, gw, gs, rep, parts=("ffn1", "mix", "ffn2", "ple")):
    w = {}
    for f in ("ffn1", "ffn2"):
        if f in parts:
            w[f"norm_{f}"] = rep[f"norm_{f}"][i][None]
            w[f"{f}_in"] = gw[f"{f}_w_in", i]
            w[f"{f}_out"] = gw[f"{f}_w_out", i].reshape(4, FF_SHARD, D)
    if "ple" in parts:
        w["ple_norm"] = rep["ple_norm"][i][None]
        w["ple_gate"] = gw["ple_gate_w", i].reshape(D, D)
        w["ple_proj"] = _cols(gw["ple_proj_w", i])
    if "mix" not in parts:
        return w
    w["norm_mix"] = rep["norm_mix"][i][None]
    if i == 0:
        w["hyb_in"] = jnp.pad(_cols(gw["hyb_w_in", 0]), ((0, 0), (0, HYB_PAD - HYB_IN)))
        w["hyb_out"] = gw["hyb_w_out", 0].reshape(2, D, D)
        w["conv_w"] = _cols(gs["conv_dw_w"][:, 0])
        w["sconv_w"] = _cols(gs["ssm_conv_w"][:, 0])
        w["conv_b"], w["ln_g"], w["ln_b"] = rep["conv_dw_b"], rep["conv_ln_g"], rep["conv_ln_b"]
        w["sconv_b"], w["ssm_norm"] = rep["ssm_conv_b"], rep["ssm_norm"]
        w["dt_bias"], w["a_log"] = _lane_pad(rep["ssm_dt_bias"]), _lane_pad(rep["ssm_a_log"])
        w["d_skip"] = jnp.repeat(rep["ssm_d"], D // SSM_HEADS, axis=1)
    else:
        w["qkv"] = gw["att_w_qkv", 0].reshape(-1, D)
        w["w_o"] = gw["att_w_o", 0].reshape(D, D)
        w["b_qkv"] = gs["att_b_qkv"][:, 0].reshape(1, -1)
        w["b_o"] = gs["att_b_o"][:, 0].reshape(1, -1)
        w["sinks"] = _lane_pad(rep["att_sinks"])
    return w


def _big_grads(i, g):
    big = {}
    for f in ("ffn1", "ffn2"):
        if f"{f}_in" in g:
            big[f"{f}_w_in", i] = g[f"{f}_in"]
            big[f"{f}_w_out", i] = g[f"{f}_out"].reshape(N_DEV, D_FF // N_DEV, D)
    if "ple_gate" in g:
        big["ple_gate_w", i] = g["ple_gate"].reshape(N_DEV, D // N_DEV, D)
        big["ple_proj_w", i] = _uncols(g["ple_proj"])
    if "hyb_in" in g:
        big["hyb_w_in", 0] = _uncols(g["hyb_in"][:, :HYB_IN])
        big["hyb_w_out", 0] = g["hyb_out"].reshape(N_DEV, 2 * D // N_DEV, D)
    if "qkv" in g:
        big["att_w_qkv", 0] = g["qkv"].reshape(N_DEV, -1, D)
        big["att_w_o", 0] = g["w_o"].reshape(N_DEV, D // N_DEV, D)
    return big


def _collect_grads(grads, d_final):
    g0, g1 = grads
    big, small = {**_big_grads(0, g0), **_big_grads(1, g1)}, {}
    for f in ("ffn1", "ffn2"):
        small[f"norm_{f}"] = jnp.concatenate([g[f"norm_{f}"] for g in grads], axis=0)
    small["norm_mix"] = jnp.concatenate([g["norm_mix"] for g in grads], axis=0)
    small["ple_norm"] = jnp.concatenate([g["ple_norm"] for g in grads], axis=0)
    small["conv_dw_w"] = g0["conv_w"][None]
    small["conv_dw_b"], small["conv_ln_g"], small["conv_ln_b"] = g0["conv_b"], g0["ln_g"], g0["ln_b"]
    small["ssm_conv_w"] = g0["sconv_w"][None]
    small["ssm_conv_b"], small["ssm_norm"] = g0["sconv_b"], g0["ssm_norm"]
    small["ssm_dt_bias"], small["ssm_a_log"] = g0["dt_bias"][:, :SSM_HEADS], g0["a_log"][:, :SSM_HEADS]
    small["ssm_d"] = g0["d_skip"].reshape(1, SSM_HEADS, D // SSM_HEADS).sum(axis=-1)
    small["att_b_qkv"], small["att_b_o"] = g1["b_qkv"], g1["b_o"]
    small["att_sinks"] = g1["sinks"][:, :SSM_HEADS]
    small["final_norm"] = d_final[0]
    return big, small


MESH = pl.DeviceIdType.MESH


def _place():
    return lax.axis_index("x"), lax.axis_index("y"), lax.axis_index("c")


def _all_gather(blocks, space, name):
    nb = len(blocks)

    def body(*refs):
        x_refs, out_refs, (send_sems, recv_sems, local_sem) = refs[:nb], refs[nb:2 * nb], refs[2 * nb:]
        x, y, c = _place()
        me, sibling = (x, y, c), (x, y, 1 - c)
        chips = [(1 - x, y), (x, 1 - y), (1 - x, 1 - y)]

        def copies(k, blk, to, own=False):
            idx = 4 * blk[0] + 2 * blk[1] + blk[2]
            return [pltpu.make_async_remote_copy(src_ref=x_ref if own else out_ref.at[idx], dst_ref=out_ref.at[idx],
                                                 send_sem=send_sems.at[k, b], recv_sem=recv_sems.at[k, b], device_id=to,
                                                 device_id_type=MESH) for b, (x_ref, out_ref) in enumerate(zip(x_refs, out_refs))]

        mine = [pltpu.make_async_copy(x_ref, out_ref.at[4 * x + 2 * y + c], local_sem.at[b])
                for b, (x_ref, out_ref) in enumerate(zip(x_refs, out_refs))]
        first = copies(0, me, sibling, own=True)
        for j, chip in enumerate(chips):
            first += copies(1 + j, me, (*chip, c), own=True)
        for cp in mine + first:
            cp.start()
        passed = []
        for j, chip in enumerate(chips):
            for cp in copies(1 + j, (*chip, c), me):
                cp.wait_recv()
            onward = copies(4 + j, (*chip, c), sibling)
            for cp in onward:
                cp.start()
            passed += onward
        for cp in copies(0, sibling, me):
            cp.wait_recv()
        for j, chip in enumerate(chips):
            for cp in copies(4 + j, (*chip, 1 - c), me):
                cp.wait_recv()
        for cp in first + passed:
            cp.wait_send()
        for cp in mine:
            cp.wait()

    spec = pl.BlockSpec(memory_space=space)
    return _pallas(
        body, name=name, out_shape=[SDS((N_DEV,) + b.shape, b.dtype) for b in blocks],
        in_specs=[spec] * nb, out_specs=[spec] * nb,
        scratch_shapes=[pltpu.SemaphoreType.DMA((7, nb)), pltpu.SemaphoreType.DMA((7, nb)), pltpu.SemaphoreType.DMA((nb,))],
    )(*blocks)


HBM_SPEC = pl.BlockSpec(memory_space=pltpu.HBM)
SEM_SPEC = pl.BlockSpec(memory_space=pltpu.SEMAPHORE)
EFFECT = pltpu.SideEffectType.DATAFLOW_SIDE_EFFECTING


def _plan_descriptors(plan, srcs, lands, send_sems, recv_sems, local_sems, arriving):
    remote, local = plan(*_place())

    def pick(si, slot):
        ref = lands[si[1]] if isinstance(si, tuple) else srcs[si]
        return ref if slot is None else ref.at[slot]

    rem = [pltpu.make_async_remote_copy(src_ref=pick(si, ss), dst_ref=lands[li].at[rs if arriving else ds],
                                        send_sem=send_sems.at[k], recv_sem=recv_sems.at[k], device_id=dev, device_id_type=MESH)
           for k, (si, ss, li, ds, dev, rs) in enumerate(remote)]
    loc = [pltpu.make_async_copy(pick(si, ss), lands[li].at[ds], local_sems.at[k])
           for k, (si, ss, li, ds) in enumerate(local)]
    return rem, loc


def _plan_counts(plan):
    remote, local = plan(0, 0, 0)
    return len(remote), max(len(local), 1)


def _exchange_start(srcs, land_shapes, plan, name, lands=None):
    ns, nl = len(srcs), len(lands if lands is not None else land_shapes)
    n_remote, n_local = _plan_counts(plan)
    if lands is None:
        lands = [pltpu.with_memory_space_constraint(lax.empty(s.shape, s.dtype), pltpu.HBM) for s in land_shapes]
    lands = list(lands)
    srcs = [pltpu.with_memory_space_constraint(s, pltpu.HBM) for s in srcs]

    def body(*refs):
        src_refs, land_refs = refs[:ns], refs[ns:ns + nl]
        send_sems, recv_sems, local_sems = refs[ns + nl:ns + nl + 3]
        token = refs[-1]
        rem, loc = _plan_descriptors(plan, src_refs, land_refs, send_sems, recv_sems, local_sems, arriving=False)
        for cp in loc + rem:
            cp.start()
        token[...] = jnp.zeros_like(token)

    outs = _pallas(
        body, name=name,
        out_shape=[pltpu.SemaphoreType.DMA((n_remote,)), pltpu.SemaphoreType.DMA((n_remote,)), pltpu.SemaphoreType.DMA((n_local,))]
        + [pltpu.HBM(a.shape, a.dtype) for a in srcs + lands] + [SDS((8, LANE), F32)],
        in_specs=[HBM_SPEC] * (ns + nl),
        out_specs=[SEM_SPEC] * 3 + [HBM_SPEC] * (ns + nl) + [pl.BlockSpec(memory_space=pltpu.VMEM)],
        input_output_aliases={i: 3 + i for i in range(ns + nl)},
        compiler_params=pltpu.CompilerParams(has_side_effects=EFFECT),
    )(*srcs, *lands)
    return (outs[:3], outs[3:3 + ns], outs[3 + ns:3 + ns + nl]), outs[-1]


def _exchange_wait(state, after, plan, name):
    sems, srcs, lands = state
    ns, nl = len(srcs), len(lands)

    def body(*refs):
        src_refs, land_refs = refs[:ns], refs[ns:ns + nl]
        send_sems, recv_sems, local_sems = refs[ns + nl:ns + nl + 3]
        rem, loc = _plan_descriptors(plan, src_refs, land_refs, send_sems, recv_sems, local_sems, arriving=True)
        for cp in rem:
            cp.wait_send()
            cp.wait_recv()
        for cp in loc:
            cp.wait()

    outs = _pallas(
        body, name=name, out_shape=[pltpu.HBM(a.shape, a.dtype) for a in list(srcs) + list(lands)],
        in_specs=[HBM_SPEC] * (ns + nl) + [SEM_SPEC] * 3 + [ANY_SPEC] * (after is not None), out_specs=[HBM_SPEC] * (ns + nl),
        input_output_aliases={i: i for i in range(ns + nl)},
        compiler_params=pltpu.CompilerParams(has_side_effects=EFFECT),
    )(*srcs, *lands, *sems, *([after] if after is not None else []))
    return outs[:ns], outs[ns:]


def _gather_plan(nb):
    def plan(x, y, c):
        me = 4 * x + 2 * y + c
        peers = [(x, y, 1 - c), (1 - x, y, c), (x, 1 - y, c), (1 - x, 1 - y, c)]
        remote = [(b, None, b, me, peer, 4 * peer[0] + 2 * peer[1] + peer[2]) for b in range(nb) for peer in peers]
        return remote, [(b, None, b, me) for b in range(nb)]
    return plan


def _relay_plan(nb):
    def plan(x, y, c):
        chips = [(1 - x, y), (x, 1 - y), (1 - x, 1 - y)]
        remote = [(("land", b), 4 * cx + 2 * cy + c, b, 4 * cx + 2 * cy + c, (x, y, 1 - c), 4 * cx + 2 * cy + (1 - c))
                  for b in range(nb) for cx, cy in chips]
        return remote, []
    return plan


def _pair_plan(nb):
    def plan(x, y, c):
        return [(b, 2 * q + (1 - c), b, q, (x, y, 1 - c), q) for b in range(nb) for q in range(4)], []
    return plan


def _chip_plan(nb):
    def plan(x, y, c):
        own = 2 * x + y
        chips = [(1 - x, y), (x, 1 - y), (1 - x, 1 - y)]
        remote = [(b, 2 * cx + cy, b, own, (cx, cy, c), 2 * cx + cy) for b in range(nb) for cx, cy in chips]
        return remote, [(b, own, b, own) for b in range(nb)]
    return plan


def _row_tile(r, cap=4608):
    return max(d for d in range(16, min(r, cap) + 1, 16) if r % d == 0)


def _pair_add(parts, got, core, name):
    _, r, cdim = parts.shape
    tr = _row_tile(r)

    def body(core_ref, p_ref, g_ref, o_ref):
        o_ref[...] = (p_ref[...].astype(F32) + g_ref[...].astype(F32)).astype(o_ref.dtype)

    return pl.pallas_call(
        body, name=name, out_shape=SDS((4, r, cdim), BF16),
        grid_spec=pltpu.PrefetchScalarGridSpec(
            num_scalar_prefetch=1, grid=(4, r // tr),
            in_specs=[pl.BlockSpec((None, tr, cdim), lambda q, i, core_ref: (2 * q + core_ref[0], i, 0)),
                      pl.BlockSpec((None, tr, cdim), lambda q, i, core_ref: (q, i, 0))],
            out_specs=pl.BlockSpec((None, tr, cdim), lambda q, i, core_ref: (q, i, 0))),
        compiler_params=_params(("parallel", "parallel")),
    )(core, parts, got)


def _sum_slots(parts, name):
    nj, r, cdim = parts.shape
    tr = _row_tile(r)

    def body(p_ref, o_ref):
        acc = p_ref[0].astype(F32)
        for j in range(1, nj):
            acc = acc + p_ref[j].astype(F32)
        o_ref[...] = acc

    return _pallas(
        body, name=name, out_shape=SDS((r, cdim), F32), grid=(r // tr,),
        in_specs=[pl.BlockSpec((nj, tr, cdim), lambda i: (0, i, 0))], out_specs=pl.BlockSpec((tr, cdim), lambda i: (i, 0)),
        compiler_params=_params(("parallel",)),
    )(parts)


def _adamw_update(wv, gv, mv, vv):
    nm = ADAM_B1 * mv + (1.0 - ADAM_B1) * gv
    nv = ADAM_B2 * vv + (1.0 - ADAM_B2) * (gv * gv)
    m_hat = nm / (1.0 - ADAM_B1 ** ADAM_STEP)
    v_hat = nv / (1.0 - ADAM_B2 ** ADAM_STEP)
    return -ADAM_LR * (m_hat / (jnp.sqrt(v_hat) + ADAM_EPS) + ADAM_WD * wv), nm, nv


def _adamw_summed(w, m, v, by_chip, name):
    nl, r, cdim = w.shape
    tr = _row_tile(r, 512)
    nblk = r // tr

    def body(*refs):
        chip_refs, (w_ref, m_ref, v_ref, g_ref, d_ref, nm_ref, nv_ref) = refs[:nl], refs[nl:]
        layer = pl.program_id(0)
        gv = None
        for ll, c_ref in enumerate(chip_refs):
            s = c_ref[0].astype(F32)
            for q in range(1, 4):
                s = s + c_ref[q].astype(F32)
            gv = s if gv is None else jnp.where(layer == ll, s, gv)
        g_ref[...] = gv
        d_ref[...], nm_ref[...], nv_ref[...] = _adamw_update(w_ref[...], gv, m_ref[...], v_ref[...])

    def chip_map(ll):
        return lambda l, i: (0, jnp.where(l == ll, i, jnp.where(l > ll, nblk - 1, 0)), 0)

    spec = pl.BlockSpec((None, tr, cdim), lambda l, i: (l, i, 0))
    return _pallas(
        body, name=name, grid=(nl, nblk),
        in_specs=[pl.BlockSpec((4, tr, cdim), chip_map(ll)) for ll in range(nl)] + [spec] * 3,
        out_specs=[spec] * 4, out_shape=[SDS((nl, r, cdim), F32)] * 4,
        compiler_params=_params(("arbitrary", "arbitrary")),
    )(*by_chip, w, m, v)


def _adamw(w, g, m, v, name):
    shape = w.shape
    cdim = shape[-1]
    w2, g2, m2, v2 = (a.reshape(-1, cdim) for a in (w, g, m, v))
    r = w2.shape[0]
    tr = next(d for d in (512, 352, 256, 128, 64, 32, 16, 8, r) if r % d == 0)

    def body(w_ref, g_ref, m_ref, v_ref, d_ref, nm_ref, nv_ref):
        d_ref[...], nm_ref[...], nv_ref[...] = _adamw_update(w_ref[...], g_ref[...], m_ref[...], v_ref[...])

    spec = pl.BlockSpec((tr, cdim), lambda i: (i, 0))
    outs = _pallas(
        body, name=name, grid=(r // tr,), in_specs=[spec] * 4, out_specs=[spec] * 3, out_shape=[SDS((r, cdim), F32)] * 3,
        compiler_params=_params(("parallel",)),
    )(w2, g2, m2, v2)
    return tuple(o.reshape(shape) for o in outs)


WEIGHTS = ("norm_ffn1", "ffn1_w_in", "ffn1_w_out", "norm_mix", "norm_ffn2", "ffn2_w_in", "ffn2_w_out", "ple_norm", "ple_gate_w",
           "ple_proj_w", "hyb_w_in", "conv_dw_w", "conv_dw_b", "conv_ln_g", "conv_ln_b", "ssm_conv_w", "ssm_conv_b", "ssm_dt_bias",
           "ssm_a_log", "ssm_d", "ssm_norm", "hyb_w_out", "att_w_qkv", "att_b_qkv", "att_sinks", "att_w_o", "att_b_o", "final_norm")
BIG = ("ffn1_w_in", "ffn1_w_out", "ffn2_w_in", "ffn2_w_out", "ple_gate_w", "ple_proj_w", "hyb_w_in", "hyb_w_out", "att_w_qkv",
       "att_w_o")
SMALL_SHARDED = {"conv_dw_w": 2, "ssm_conv_w": 2, "att_b_qkv": 1, "att_b_o": 1}
SMALL = tuple(n for n in WEIGHTS if n not in BIG)
TRANSPOSED = ("ffn1_w_in", "ffn2_w_in", "att_w_qkv")
PACK_ROWS = 16


def _pack(arrays, lead=0):
    pieces = []
    for a in arrays:
        flat = a.reshape(*a.shape[:lead], -1)
        size = flat.shape[-1]
        padded = -(-size // (PACK_ROWS * LANE)) * PACK_ROWS * LANE
        flat = jnp.pad(flat, [(0, 0)] * lead + [(0, padded - size)])
        pieces.append(flat.reshape(*a.shape[:lead], padded // LANE, LANE))
    return jnp.concatenate(pieces, axis=lead)


def _unpack(buf, shapes, lead=0):
    out, row = [], 0
    for shape in shapes:
        size = math.prod(shape)
        rows = -(-size // (PACK_ROWS * LANE)) * PACK_ROWS
        piece = lax.slice_in_dim(buf, row, row + rows, axis=lead)
        piece = piece.reshape(*buf.shape[:lead], rows * LANE)
        out.append(lax.slice_in_dim(piece, 0, size, axis=lead).reshape(*buf.shape[:lead], *shape))
        row += rows
    return out


def kernel(x, p, norm_ffn1, ffn1_w_in, ffn1_w_out, norm_mix, norm_ffn2, ffn2_w_in, ffn2_w_out, ple_norm, ple_gate_w, ple_proj_w, hyb_w_in, conv_dw_w, conv_dw_b, conv_ln_g, conv_ln_b, ssm_conv_w, ssm_conv_b, ssm_dt_bias, ssm_a_log, ssm_d, ssm_norm, hyb_w_out, att_w_qkv, att_b_qkv, att_sinks, att_w_o, att_b_o, final_norm, loss_target, m_norm_ffn1, m_ffn1_w_in, m_ffn1_w_out, m_norm_mix, m_norm_ffn2, m_ffn2_w_in, m_ffn2_w_out, m_ple_norm, m_ple_gate_w, m_ple_proj_w, m_hyb_w_in, m_conv_dw_w, m_conv_dw_b, m_conv_ln_g, m_conv_ln_b, m_ssm_conv_w, m_ssm_conv_b, m_ssm_dt_bias, m_ssm_a_log, m_ssm_d, m_ssm_norm, m_hyb_w_out, m_att_w_qkv, m_att_b_qkv, m_att_sinks, m_att_w_o, m_att_b_o, m_final_norm, v_norm_ffn1, v_ffn1_w_in, v_ffn1_w_out, v_norm_mix, v_norm_ffn2, v_ffn2_w_in, v_ffn2_w_out, v_ple_norm, v_ple_gate_w, v_ple_proj_w, v_hyb_w_in, v_conv_dw_w, v_conv_dw_b, v_conv_ln_g, v_conv_ln_b, v_ssm_conv_w, v_ssm_conv_b, v_ssm_dt_bias, v_ssm_a_log, v_ssm_d, v_ssm_norm, v_hyb_w_out, v_att_w_qkv, v_att_b_qkv, v_att_sinks, v_att_w_o, v_att_b_o, v_final_norm):
    args = (norm_ffn1, ffn1_w_in, ffn1_w_out, norm_mix, norm_ffn2, ffn2_w_in, ffn2_w_out, ple_norm, ple_gate_w, ple_proj_w, hyb_w_in, conv_dw_w, conv_dw_b, conv_ln_g, conv_ln_b, ssm_conv_w, ssm_conv_b, ssm_dt_bias, ssm_a_log, ssm_d, ssm_norm, hyb_w_out, att_w_qkv, att_b_qkv, att_sinks, att_w_o, att_b_o, final_norm)
    moments_m = (m_norm_ffn1, m_ffn1_w_in, m_ffn1_w_out, m_norm_mix, m_norm_ffn2, m_ffn2_w_in, m_ffn2_w_out, m_ple_norm, m_ple_gate_w, m_ple_proj_w, m_hyb_w_in, m_conv_dw_w, m_conv_dw_b, m_conv_ln_g, m_conv_ln_b, m_ssm_conv_w, m_ssm_conv_b, m_ssm_dt_bias, m_ssm_a_log, m_ssm_d, m_ssm_norm, m_hyb_w_out, m_att_w_qkv, m_att_b_qkv, m_att_sinks, m_att_w_o, m_att_b_o, m_final_norm)
    moments_v = (v_norm_ffn1, v_ffn1_w_in, v_ffn1_w_out, v_norm_mix, v_norm_ffn2, v_ffn2_w_in, v_ffn2_w_out, v_ple_norm, v_ple_gate_w, v_ple_proj_w, v_hyb_w_in, v_conv_dw_w, v_conv_dw_b, v_conv_ln_g, v_conv_ln_b, v_ssm_conv_w, v_ssm_conv_b, v_ssm_dt_bias, v_ssm_a_log, v_ssm_d, v_ssm_norm, v_hyb_w_out, v_att_w_qkv, v_att_b_qkv, v_att_sinks, v_att_w_o, v_att_b_o, v_final_norm)
    w = dict(zip(WEIGHTS, args))
    m = dict(zip(WEIGHTS, moments_m))
    v = dict(zip(WEIGHTS, moments_v))
    cx, cy, cc = _place()
    me = 4 * cx + 2 * cy + cc

    core = jnp.reshape(cc, (1,)).astype(jnp.int32)
    layer_of = lambda n, i: 1 if n.startswith("att_") else i
    keys = [[(n, i) for n in BIG for i in range(w[n].shape[0]) if layer_of(n, i) == layer] for layer in range(2)]

    first = [key for key in keys[0] if key[0].startswith("ffn1")]
    mixer = [key for key in keys[0] if key[0].startswith("hyb")]
    rest0 = [key for key in keys[0] if key not in first + mixer]
    gw, by_chip = {}, {}
    view = lambda a, n: jnp.swapaxes(a, 1, 2) if n in TRANSPOSED else a
    block = lambda n, i: view(w[n], n)[i].astype(BF16)

    def gather_later(group, name):
        blocks = [block(n, i) for n, i in group]
        plan, relay_plan = _gather_plan(len(blocks)), _relay_plan(len(blocks))
        state, _ = _exchange_start(blocks, [SDS((N_DEV,) + b.shape, BF16) for b in blocks], plan, f"{name}_start")
        stage = {}

        def relay():
            _, landed = _exchange_wait(state, None, plan, f"{name}_wait")
            stage["relay"], _ = _exchange_start([], None, relay_plan, f"{name}_relay_start", lands=landed)

        def arrived():
            gw.update(zip(group, _exchange_wait(stage["relay"], None, relay_plan, f"{name}_relay_wait")[1]))

        return relay, arrived

    def reduce_later(group, big, name):
        pair_plan, chip_plan = _pair_plan(len(group)), _chip_plan(len(group))
        parts = [big[key] for key in group]
        pair, token = _exchange_start(parts, [SDS((4,) + pt.shape[1:], BF16) for pt in parts], pair_plan, f"{name}_pair_start")
        stage = {}

        def middle(after):
            thru, got = _exchange_wait(pair, after, pair_plan, f"{name}_pair_wait")
            sums = [_pair_add(pt, gt, core, f"grads_pair_add_{n}_{i}") for pt, gt, (n, i) in zip(thru, got, group)]
            stage["chip"], chip_token = _exchange_start(sums, [SDS(s.shape, BF16) for s in sums], chip_plan, f"{name}_chip_start")
            return chip_token

        def finish(after):
            by_chip.update(zip(group, _exchange_wait(stage["chip"], after, chip_plan, f"{name}_chip_wait")[1]))

        return token, middle, finish

    _restart_chain()
    gw["ffn1_w_in", 0], gathered_small = _all_gather([block("ffn1_w_in", 0), _pack([w[n] for n in SMALL_SHARDED])],
                                                      pltpu.HBM, "gather_weights_first")
    early_relay, early_arrived = gather_later([("ffn1_w_out", 0)], "gather_weights_early")
    mixer_relay, mixer_arrived = gather_later(mixer, "gather_weights_mixer")
    rest0_relay, rest0_arrived = gather_later(rest0, "gather_weights_rest")
    layer1_relay, layer1_arrived = gather_later(keys[1], "gather_weights_l1")
    gs = dict(zip(SMALL_SHARDED, _unpack(gathered_small, [w[n].shape for n in SMALL_SHARDED], lead=1)))
    rep = {n: w[n] for n in SMALL if n not in SMALL_SHARDED}

    tables = _rope_tables(x.shape[1])
    pb = p[:, 0].astype(BF16)
    w0, s0 = {}, {}

    def first_w_out():
        early_relay()
        early_arrived()
        w0.update(_build_layer(0, gw, gs, rep, parts=("ffn1",)))
        return w0["ffn1_out"]

    h, s0["ffn1"] = _ffn_fwd(x[0], rep["norm_ffn1"][0][None], gw["ffn1_w_in", 0], first_w_out, "l0_ffn1")
    mixer_relay()
    mixer_arrived()
    w0.update(_build_layer(0, gw, gs, rep, parts=("mix",)))
    h, s0["mix"] = _hyb_fwd(h, w0, "l0_hyb", after_in=rest0_relay)
    layer1_relay()
    rest0_arrived()
    w0.update(_build_layer(0, gw, gs, rep, parts=("ffn2", "ple")))
    h, s0["ffn2"] = _ffn_fwd(h, w0["norm_ffn2"], w0["ffn2_in"], w0["ffn2_out"], "l0_ffn2")
    h, s0["ple"] = _ple_fwd(h, w0["ple_norm"], pb[0], w0["ple_gate"], w0["ple_proj"], "l0_ple")
    layer1_arrived()
    w1 = _build_layer(1, gw, gs, rep)
    h, s1 = _layer_fwd(1, h, w1, pb[1], tables)
    loss, dh, dhb, d_final = _loss_head(h, loss_target[0], final_norm[None], "loss_head")
    loss = lax.psum(loss[0, 0], ("x", "y", "c"))

    dh, dhb, head1 = _layer_bwd_head(1, dh, dhb, s1, w1, pb[1])
    dh, dhb, tail1 = _layer_bwd_tail(1, dh, dhb, s1, w1, tables)
    grads1 = {**head1, **tail1}
    l1_token, l1_middle, l1_finish = reduce_later(keys[1], _big_grads(1, grads1), "grads_l1")
    dh, dhb, grads0 = _layer_bwd_head(0, dh, dhb, s0, w0, pb[0], deps=(l1_token,))
    dh, dhb, grads0["norm_ffn2"], grads0["ffn2_in"], grads0["ffn2_out"] = _ffn_bwd(
        dh, dhb, s0["ffn2"], w0["norm_ffn2"], w0["ffn2_in"], w0["ffn2_out"], "l0_ffn2", deps=(l1_middle(dh),))
    dh, dhb, mixer_grads = _hyb_bwd(dh, dhb, s0["mix"], w0, "l0_hyb")
    grads0.update(mixer_grads)
    l0_token, l0_middle, l0_finish = reduce_later(mixer + rest0, _big_grads(0, grads0), "grads_l0")
    last = {}

    def reduce_first(dw_in, dw_out):
        token, middle, last["finish"] = reduce_later(first, _big_grads(0, dict(ffn1_in=dw_in, ffn1_out=dw_out)), "grads_first")
        middle(token)

    dx, dhb, grads0["norm_ffn1"], grads0["ffn1_in"], grads0["ffn1_out"] = _ffn_bwd(
        dh, dhb, s0["ffn1"], w0["norm_ffn1"], w0["ffn1_in"], w0["ffn1_out"], "l0_ffn1", deps=(l0_token,),
        hook=lambda dpre: (l0_middle(dpre),), weights_hook=reduce_first)
    l1_finish(dx)
    l0_finish(dx)
    last["finish"](dx)
    _, small = _collect_grads([grads0, grads1], d_final)
    small_shapes = [small[n].shape for n in SMALL]
    all_small = _all_gather([_pack([small[n] for n in SMALL])], pltpu.VMEM, "gather_small_grads")[0]
    g = dict(zip(SMALL, _unpack(_sum_slots(all_small, "small_grads_sum"), small_shapes)))
    for n, axis in SMALL_SHARDED.items():
        g[n] = lax.dynamic_slice_in_dim(g[n], me * w[n].shape[axis], w[n].shape[axis], axis=axis)

    delta, new_m, new_v = {}, {}, {}
    for n in BIG:
        outs = _adamw_summed(view(w[n], n), view(m[n], n), view(v[n], n), [by_chip[n, i] for i in range(w[n].shape[0])],
                             f"adamw_{n}")
        g[n], delta[n], new_m[n], new_v[n] = (view(o, n) for o in outs)
    packed = [_pack([d[n] for n in SMALL]) for d in (w, g, m, v)]
    shapes = [w[n].shape for n in SMALL]
    for d, buf in zip((delta, new_m, new_v), _adamw(*packed, "adamw_small")):
        d.update(zip(SMALL, _unpack(buf, shapes)))
    return (loss, dx[None], *[g[n] for n in WEIGHTS], *[delta[n] for n in WEIGHTS], *[new_m[n] for n in WEIGHTS],
            *[new_v[n] for n in WEIGHTS])
```

```python
import functools
import math

import numpy as np
import jax
import jax.numpy as jnp
from jax import lax
from jax.experimental import pallas as pl
from jax.experimental.pallas import tpu as pltpu

F32, BF16 = jnp.float32, jnp.bfloat16
HI = lax.Precision.HIGHEST
SDS = jax.ShapeDtypeStruct

N_DEV = 8
D = 1024
D_FF = 2816
FF_SHARD = 2 * D_FF // N_DEV
PLE_DIM = 256
EPS = 1e-6
CONV_W = 31
SSM_CONV = 4
SSM_HEADS = 16
SSM_XBC = 1536
CHUNK = 128
HYB_IN = 4624
HYB_PAD = 5120
DT_COL = 4608
N_PAIR = 8
ROPE_THETA = 10000.0
LANE = 128
VMEM_LIMIT = 56 * 1024 * 1024

ADAM_LR, ADAM_B1, ADAM_B2, ADAM_EPS, ADAM_WD, ADAM_STEP = 0.001, 0.9, 0.999, 1e-08, 0.01, 10


def _params(sem):
    return pltpu.CompilerParams(dimension_semantics=sem, vmem_limit_bytes=VMEM_LIMIT)


_CHAIN = []


def _restart_chain():
    _CHAIN.clear()


def _pallas(body, *, in_specs, **kw):
    def run(*args):
        n, dep = len(args), list(_CHAIN)

        def chained(*refs):
            return body(*refs[:n], *refs[n + len(dep):])

        outs = pl.pallas_call(chained, in_specs=list(in_specs) + [pl.BlockSpec(memory_space=pl.ANY)] * len(dep), **kw)(*args, *dep)
        _CHAIN[:] = [outs[-1] if isinstance(outs, (list, tuple)) else outs]
        return outs

    return run


def _mm(a, b, *, ta=False, tb=False, reduce_j=False, out_dtypes=(F32,), tm=1024, tn=1024, tk=1024,
        epi=None, extras=(), rows=(), deps=(), sums=0, name):
    ja, jb = a.shape[0], b.shape[0]
    nj = max(ja, jb)
    jo = 1 if reduce_j else nj
    m, k = (a.shape[2], a.shape[1]) if ta else (a.shape[1], a.shape[2])
    n = b.shape[1] if tb else b.shape[2]
    assert (b.shape[2] if tb else b.shape[1]) == k and ja in (1, nj) and jb in (1, nj)
    tm, tn, tk = min(tm, m), min(tn, n), min(tk, k)
    assert m % tm == 0 and n % tn == 0 and k % tk == 0, (name, m, n, k, tm, tn, tk)
    assert not sums or (tn == n and (reduce_j or nj == 1))
    nk = k // tk
    steps = nk * (nj if reduce_j else 1)
    ne, nr, no = len(extras), len(rows), len(out_dtypes)

    def a_map(i, c, j, kk):
        return (j if ja > 1 else 0, kk, i) if ta else (j if ja > 1 else 0, i, kk)

    def b_map(i, c, j, kk):
        return (j if jb > 1 else 0, c, kk) if tb else (j if jb > 1 else 0, kk, c)

    def o_map(i, c, j, kk):
        return (0 if reduce_j else j, i, c)

    dims = (((0 if ta else 1,), (1 if tb else 0,)), ((), ()))

    def body(a_ref, b_ref, *rest):
        ex, rw = rest[:ne], rest[ne:ne + nr]
        outs = rest[ne + nr + len(deps):ne + nr + len(deps) + no]
        sum_refs = rest[ne + nr + len(deps) + no:ne + nr + len(deps) + no + sums]
        first_tile = pl.program_id(0) == 0

        def product():
            return lax.dot_general(a_ref[...], b_ref[...], dims, preferred_element_type=F32)

        def finish(acc):
            res = epi(acc, *[e[...] for e in ex], *[r[...] for r in rw]) if epi else (acc,)
            for o, r in zip(outs, res):
                o[...] = r.astype(o.dtype)
            for s_ref, r in zip(sum_refs, res[no:]):
                @pl.when(first_tile)
                def _(s_ref=s_ref, r=r):
                    s_ref[...] = r

                @pl.when(jnp.logical_not(first_tile))
                def _(s_ref=s_ref, r=r):
                    s_ref[...] += r

        if steps == 1:
            finish(product())
            return
        acc_ref = rest[-1]
        kk = pl.program_id(3)
        step = pl.program_id(2) * nk + kk if reduce_j else kk

        @pl.when(step == 0)
        def _():
            acc_ref[...] = product()

        @pl.when(jnp.logical_and(step > 0, step < steps - 1))
        def _():
            acc_ref[...] += product()

        @pl.when(step == steps - 1)
        def _():
            finish(acc_ref[...] + product())

    o_spec = pl.BlockSpec((None, tm, tn), o_map)
    row_spec = pl.BlockSpec((1, tn), lambda i, c, j, kk: (0, c))
    return _pallas(
        body, name=name, grid=(m // tm, n // tn, nj, nk),
        in_specs=[pl.BlockSpec((None, tk, tm) if ta else (None, tm, tk), a_map),
                  pl.BlockSpec((None, tn, tk) if tb else (None, tk, tn), b_map)]
        + [o_spec] * ne + [row_spec] * nr + [ANY_SPEC] * len(deps),
        out_specs=[o_spec] * no + [row_spec] * sums,
        out_shape=[SDS((jo, m, n), dt) for dt in out_dtypes] + [SDS((1, n), F32)] * sums,
        scratch_shapes=[pltpu.VMEM((tm, tn), F32)] if steps > 1 else [],
        compiler_params=_params(("arbitrary" if sums else "parallel", "parallel", "arbitrary", "arbitrary")),
    )(a, b, *extras, *rows, *deps)


def _whole(p):
    return pl.BlockSpec(p.shape, lambda *_: (0,) * p.ndim)


ANY_SPEC = pl.BlockSpec(memory_space=pl.ANY)


def _rowop(fn, tiles, params, outs, *, grid, name, deps=()):
    nin = len(tiles) + len(params)

    def body(*refs):
        res = fn(*[r[...].astype(F32) for r in refs[:nin]])
        for r, o in zip(refs[nin + len(deps):], res):
            r[...] = o.astype(r.dtype)

    return _pallas(
        body, name=name, grid=grid,
        in_specs=[s for _, s in tiles] + [_whole(p) for p in params] + [ANY_SPEC] * len(deps),
        out_specs=[s for _, _, s in outs], out_shape=[SDS(sh, dt) for sh, dt, _ in outs],
        compiler_params=_params(("parallel",) * len(grid)),
    )(*[t for t, _ in tiles], *params, *deps)


def _rowop_bwd(fn, tiles, params, cots, wrt, gouts, *, grid, name, adds=(), deps=()):
    nt, npar, nc, na = len(tiles), len(params), len(cots), len(adds)
    nin = nt + npar
    flat = [i for grp in wrt for i in grp]
    n_gout = sum(len(dts) for _, dts, _ in gouts)

    def body(*refs):
        vals = [r[...].astype(F32) for r in refs[:nin]]
        cvals = [r[...].astype(F32) for r in refs[nin:nin + nc]]
        avals = [r[...].astype(F32) for r in refs[nin + nc:nin + nc + na]]
        orefs = refs[nin + nc + na + len(deps):]
        diff_idx = flat + list(range(nt, nin))

        def f(*dv):
            full = list(vals)
            for i, v in zip(diff_idx, dv):
                full[i] = v
            return fn(*full)

        _, vjp = jax.vjp(f, *[vals[i] for i in diff_idx])
        grads = vjp(tuple(cvals))
        tile_g, par_g = list(grads[:len(flat)]), grads[len(flat):]
        group_g, at = [], 0
        for grp in wrt:
            members = tile_g[at:at + len(grp)]
            at += len(grp)
            group_g.append(members[0] if len(grp) == 1 else jnp.stack(members, axis=0))
        for av in avals:
            group_g[0] = group_g[0] + av
        o = 0
        for g, (_, dts, _) in zip(group_g, gouts):
            for _ in dts:
                orefs[o][...] = g.astype(orefs[o].dtype)
                o += 1
        first = functools.reduce(jnp.logical_and, [pl.program_id(ax) == 0 for ax in range(len(grid))])
        for r, g in zip(orefs[n_gout:], par_g):
            @pl.when(first)
            def _(r=r, g=g):
                r[...] = g

            @pl.when(jnp.logical_not(first))
            def _(r=r, g=g):
                r[...] += g

    out_specs, out_shape = [], []
    for sh, dts, spec in gouts:
        for dt in dts:
            out_specs.append(spec)
            out_shape.append(SDS(sh, dt))
    for p in params:
        out_specs.append(_whole(p))
        out_shape.append(SDS(p.shape, F32))
    return _pallas(
        body, name=name, grid=grid,
        in_specs=[s for _, s in tiles] + [_whole(p) for p in params] + [s for _, s in cots] + [s for _, s in adds]
        + [ANY_SPEC] * len(deps),
        out_specs=out_specs, out_shape=out_shape,
        compiler_params=_params(("arbitrary",) * len(grid)),
    )(*[t for t, _ in tiles], *params, *[c for c, _ in cots], *[a for a, _ in adds], *deps)


def _tok(c, tm, col=0):
    return pl.BlockSpec((tm, c), lambda i, col=col: (i, col))


def _rms_fn(h, g):
    return (h * lax.rsqrt(jnp.mean(h * h, axis=-1, keepdims=True) + EPS) * g,)


def _lnswish_fn(u, g, b):
    mu = jnp.mean(u, axis=-1, keepdims=True)
    xc = u - mu
    y = xc * lax.rsqrt(jnp.mean(xc * xc, axis=-1, keepdims=True) + EPS) * g + b
    return (y * jax.nn.sigmoid(y),)


def _ple_fn(z, e):
    return (jax.nn.sigmoid(z) * e,)


def _rms(h, g, name, tm=512, deps=()):
    t = h.shape[0]
    return _rowop(_rms_fn, [(h, _tok(D, tm))], [g], [((t, D), BF16, _tok(D, tm))], grid=(t // tm,), name=name, deps=deps)[0]


def _drms_epi(dn, h, dres, g):
    _, vjp = jax.vjp(_rms_fn, h, g)
    dh, dg = vjp((dn,))
    dh = dh + dres
    return dh, dh, dg


def _mm_drms(a, b, h, g, dres, name, tk, tb=True):
    dh, dhb, dg = _mm(a, b, tb=tb, reduce_j=a.shape[0] > 1, tm=512, tk=tk, epi=_drms_epi, extras=(h[None], dres[None]),
                      rows=(g,), out_dtypes=(F32, BF16), sums=1, name=name)
    return dh[0], dhb[0], dg


def _conv_geometry(width):
    pad = 32 if width > 8 else 8
    return pad, pad - (width - 1)


def _fill_shifts(xpad_ref, sh_ref, t, shifts):
    for r in shifts:
        sh_ref[r, :, :] = xpad_ref[pl.ds(r, t + 32), :]


def _dwconv(xs, w, b, *, width, glu, silu, cb, name):
    t = xs[0][0].shape[0]
    c = w.shape[1]
    pad, off = _conv_geometry(width)
    shifts = sorted({(k + off) % 8 for k in range(width)})
    ch = 64

    def body(*refs):
        x_refs, (w_ref, b_ref, o_ref, xpad_ref, sh_ref) = refs[:len(xs)], refs[len(xs):]
        u = x_refs[0][...] * jax.nn.sigmoid(x_refs[1][...]) if glu else x_refs[0][...]
        xpad_ref[pl.ds(0, pad), :] = jnp.zeros((pad, cb), F32)
        xpad_ref[pl.ds(pad, t), :] = u
        xpad_ref[pl.ds(pad + t, 40 - pad), :] = jnp.zeros((40 - pad, cb), F32)
        _fill_shifts(xpad_ref, sh_ref, t, shifts)

        def chunk(i, carry):
            t0 = pl.multiple_of(i * ch, ch)
            acc = jnp.broadcast_to(b_ref[...], (ch, cb))
            for k in range(width):
                q, r = divmod(k + off, 8)
                acc = acc + w_ref[pl.ds(k, 1), :] * sh_ref[r, pl.ds(t0 + 8 * q, ch), :]
            o_ref[pl.ds(t0, ch), :] = acc * jax.nn.sigmoid(acc) if silu else acc
            return carry

        lax.fori_loop(0, t // ch, chunk, 0)

    return _pallas(
        body, name=name, grid=(c // cb,),
        in_specs=[pl.BlockSpec((t, cb), lambda i, o=o: (0, o + i)) for _, o in xs]
        + [pl.BlockSpec((width, cb), lambda i: (0, i)), pl.BlockSpec((1, cb), lambda i: (0, i))],
        out_specs=pl.BlockSpec((t, cb), lambda i: (0, i)), out_shape=SDS((t, c), F32),
        scratch_shapes=[pltpu.VMEM((t + 40, cb), F32), pltpu.VMEM((8, t + 32, cb), F32)],
        compiler_params=_params(("parallel",)),
    )(*[x for x, _ in xs], w, b)


def _dwconv_bwd(xs, w, b, dy, *, width, glu, silu, cb, name):
    t = xs[0][0].shape[0]
    c = w.shape[1]
    pad, off = _conv_geometry(width)
    shifts = sorted({(k + off) % 8 for k in range(width)})
    shifts_t = sorted({mm % 8 for mm in range(width)})
    ch = 64
    nx = len(xs)

    def body(*refs):
        x_refs = refs[:nx]
        w_ref, b_ref, dy_ref = refs[nx:nx + 3]
        dx_refs = refs[nx + 3:nx + 3 + nx]
        dw_ref, db_ref, xpad_ref, sh_ref, dc_ref = refs[nx + 3 + nx:]
        u = x_refs[0][...] * jax.nn.sigmoid(x_refs[1][...]) if glu else x_refs[0][...]
        xpad_ref[pl.ds(0, pad), :] = jnp.zeros((pad, cb), F32)
        xpad_ref[pl.ds(pad, t), :] = u
        xpad_ref[pl.ds(pad + t, 40 - pad), :] = jnp.zeros((40 - pad, cb), F32)
        _fill_shifts(xpad_ref, sh_ref, t, shifts)

        if silu:
            def act_chunk(i, carry):
                t0 = pl.multiple_of(i * ch, ch)
                acc = jnp.broadcast_to(b_ref[...], (ch, cb))
                for k in range(width):
                    q, r = divmod(k + off, 8)
                    acc = acc + w_ref[pl.ds(k, 1), :] * sh_ref[r, pl.ds(t0 + 8 * q, ch), :]
                sg = jax.nn.sigmoid(acc)
                dc_ref[pl.ds(t0, ch), :] = dy_ref[pl.ds(t0, ch), :] * (sg * (1.0 + acc * (1.0 - sg)))
                return carry

            lax.fori_loop(0, t // ch, act_chunk, 0)
        else:
            dc_ref[...] = dy_ref[...]

        def dw_chunk(i, accs):
            t0 = pl.multiple_of(i * ch, ch)
            new = list(accs)
            for s in range(ch // 8):
                d = dc_ref[pl.ds(t0 + 8 * s, 8), :]
                for k in range(width):
                    q, r = divmod(k + off, 8)
                    new[k] = new[k] + d * sh_ref[r, pl.ds(t0 + 8 * (q + s), 8), :]
                new[width] = new[width] + d
            return tuple(new)

        accs = lax.fori_loop(0, t // ch, dw_chunk, tuple(jnp.zeros((8, cb), F32) for _ in range(width + 1)))
        for k in range(width):
            dw_ref[pl.ds(k, 1), :] = jnp.sum(accs[k], axis=0, keepdims=True)
        db_ref[...] = jnp.sum(accs[width], axis=0, keepdims=True)

        xpad_ref[pl.ds(0, t), :] = dc_ref[...]
        xpad_ref[pl.ds(t, 40), :] = jnp.zeros((40, cb), F32)
        _fill_shifts(xpad_ref, sh_ref, t, shifts_t)

        def dx_chunk(i, carry):
            t0 = pl.multiple_of(i * ch, ch)
            acc = jnp.zeros((ch, cb), F32)
            for mm in range(width):
                q, r = divmod(mm, 8)
                acc = acc + w_ref[pl.ds(width - 1 - mm, 1), :] * sh_ref[r, pl.ds(t0 + 8 * q, ch), :]
            if glu:
                val, gate = x_refs[0][pl.ds(t0, ch), :], x_refs[1][pl.ds(t0, ch), :]
                sg = jax.nn.sigmoid(gate)
                dx_refs[0][pl.ds(t0, ch), :] = (acc * sg).astype(BF16)
                dx_refs[1][pl.ds(t0, ch), :] = (acc * val * sg * (1.0 - sg)).astype(BF16)
            else:
                dx_refs[0][pl.ds(t0, ch), :] = acc.astype(BF16)
            return carry

        lax.fori_loop(0, t // ch, dx_chunk, 0)

    col = pl.BlockSpec((t, cb), lambda i: (0, i))
    return _pallas(
        body, name=name, grid=(c // cb,),
        in_specs=[pl.BlockSpec((t, cb), lambda i, o=o: (0, o + i)) for _, o in xs]
        + [pl.BlockSpec((width, cb), lambda i: (0, i)), pl.BlockSpec((1, cb), lambda i: (0, i)), col],
        out_specs=[col] * nx + [pl.BlockSpec((width, cb), lambda i: (0, i)), pl.BlockSpec((1, cb), lambda i: (0, i))],
        out_shape=[SDS((t, c), BF16)] * nx + [SDS((width, c), F32), SDS((1, c), F32)],
        scratch_shapes=[pltpu.VMEM((t + 40, cb), F32), pltpu.VMEM((8, t + 32, cb), F32), pltpu.VMEM((t, cb), F32)],
        compiler_params=_params(("parallel",)),
    )(*[x for x, _ in xs], w, b, dy)


_DIMS = {"nn": (((1,), (0,)), ((), ())), "nt": (((1,), (1,)), ((), ())), "tn": (((0,), (0,)), ((), ()))}


def _raw_dot(a, b, mode):
    return lax.dot_general(a.astype(BF16), b.astype(BF16), _DIMS[mode], preferred_element_type=F32)


@functools.partial(jax.custom_vjp, nondiff_argnums=(2,))
def _bdot(a, b, mode):
    return _raw_dot(a, b, mode)


def _bdot_fwd(a, b, mode):
    return _raw_dot(a, b, mode), (a, b)


def _bdot_bwd(mode, res, g):
    a, b = res
    if mode == "nn":
        return _raw_dot(g, b, "nt"), _raw_dot(a, g, "tn")
    if mode == "nt":
        return _raw_dot(g, b, "nn"), _raw_dot(g, a, "tn")
    return _raw_dot(b, g, "nt"), _raw_dot(a, g, "nn")


_bdot.defvjp(_bdot_fwd, _bdot_bwd)


def _iota(shape, axis):
    return lax.broadcasted_iota(jnp.int32, shape, axis)


def _half_masks():
    left = (_iota((1, LANE), 1) < 64).astype(F32)
    return left, 1.0 - left


def _split3(a):
    a1 = a.astype(BF16)
    r1 = a - a1.astype(F32)
    a2 = r1.astype(BF16)
    return a1, a2, (r1 - a2.astype(F32)).astype(BF16)


def _exact_dot(a, e, mode):
    return sum(lax.dot_general(piece, e, _DIMS[mode], preferred_element_type=F32) for piece in _split3(a))


@jax.custom_vjp
def _spread(a, e):
    return _exact_dot(a, e, "nn")


_spread.defvjp(lambda a, e: (_exact_dot(a, e, "nn"), e), lambda e, g: (_exact_dot(g, e, "nt"), jnp.zeros_like(e)))


@jax.custom_vjp
def _running_sum(tri, a):
    return sum(lax.dot_general(tri, piece, _DIMS["nn"], preferred_element_type=F32) for piece in _split3(a))


_running_sum.defvjp(
    lambda tri, a: (sum(lax.dot_general(tri, piece, _DIMS["nn"], preferred_element_type=F32) for piece in _split3(a)), tri),
    lambda tri, g: (jnp.zeros_like(tri), sum(lax.dot_general(tri, piece, _DIMS["tn"], preferred_element_type=F32)
                                             for piece in _split3(g))))


def _ssd_chunk(state, xa, dtr, z, dtb, alog, dskf, ng):
    xs, bm, cm = xa[:, :D], xa[:, D:D + 256], xa[:, D + 256:]
    left, right = _half_masks()
    expand = (_iota((LANE, D), 1) // 64 == _iota((LANE, D), 0)).astype(BF16)
    li, si = _iota((CHUNK, CHUNK), 0), _iota((CHUNK, CHUNK), 1)
    tril = li >= si
    dt16 = jax.nn.softplus(dtr + dtb)
    adt = dt16 * (-jnp.exp(alog))
    dtf = _spread(dt16, expand)
    cs16 = _running_sum(tril.astype(BF16), adt)
    csf = _spread(cs16, expand)
    totf = jnp.sum(jnp.where(_iota((CHUNK, D), 0) == CHUNK - 1, csf, 0.0), axis=0, keepdims=True)
    cst = cs16.T
    xdt = xs * dtf
    ys, new_state = [], []
    for g in range(2):
        bg, cg = bm[:, LANE * g:LANE * (g + 1)], cm[:, LANE * g:LANE * (g + 1)]
        cb = _bdot(cg, bg, "nt")
        for q in range(4):
            pr = 4 * g + q
            decay = []
            for h in (2 * pr, 2 * pr + 1):
                col = jnp.sum(jnp.where(si == h, cs16, 0.0), axis=1, keepdims=True)
                row = jnp.sum(jnp.where(li == h, cst, 0.0), axis=0, keepdims=True)
                decay.append(cb * jnp.exp(jnp.where(tril, col - row, -jnp.inf)))
            xp = xdt[:, LANE * pr:LANE * (pr + 1)]
            y_diag = _bdot(jnp.concatenate(decay, axis=1), jnp.concatenate([xp * left, xp * right], axis=0), "nn")
            csb, tot = csf[:, LANE * pr:LANE * (pr + 1)], totf[:, LANE * pr:LANE * (pr + 1)]
            ys.append(y_diag + _bdot(cg, state[pr], "nn") * jnp.exp(csb))
            new_state.append(state[pr] * jnp.exp(tot) + _bdot(bg, xp * jnp.exp(tot - csb), "tn"))
    y = jnp.concatenate(ys, axis=1)
    y = y + dskf * xs
    y = y * (z * jax.nn.sigmoid(z))
    halves = []
    for g in range(2):
        yg = y[:, 512 * g:512 * (g + 1)]
        halves.append(yg * lax.rsqrt(jnp.mean(yg * yg, axis=-1, keepdims=True) + EPS))
    return jnp.concatenate(halves, axis=1) * ng, jnp.stack(new_state, axis=0)


def _ssd_specs(t, rev):
    nc = t // CHUNK
    ix = (lambda c: nc - 1 - c) if rev else (lambda c: c)
    return nc, ix


def _ssd_fwd(xa, proj, dtb, alog, dsk, ng, name):
    t = xa.shape[0]
    nc, ix = _ssd_specs(t, False)

    def body(xa_ref, dt_ref, z_ref, dtb_ref, alog_ref, dsk_ref, ng_ref, y_ref, st_ref, carry_ref):
        @pl.when(pl.program_id(0) == 0)
        def _():
            carry_ref[...] = jnp.zeros_like(carry_ref)

        st_ref[...] = carry_ref[...]
        y, new = _ssd_chunk(carry_ref[...], xa_ref[...], dt_ref[...], z_ref[...], dtb_ref[...], alog_ref[...],
                            dsk_ref[...], ng_ref[...])
        y_ref[...] = y.astype(BF16)
        carry_ref[...] = new

    small = [dtb, alog, dsk, ng]
    return _pallas(
        body, name=name, grid=(nc,),
        in_specs=[pl.BlockSpec((CHUNK, SSM_XBC), lambda c: (c, 0)),
                  pl.BlockSpec((CHUNK, LANE), lambda c: (c, DT_COL // LANE)),
                  pl.BlockSpec((CHUNK, D), lambda c: (c, 2))] + [_whole(p) for p in small],
        out_specs=[pl.BlockSpec((CHUNK, D), lambda c: (c, 0)), pl.BlockSpec((None, N_PAIR, LANE, LANE), lambda c: (c, 0, 0, 0))],
        out_shape=[SDS((t, D), BF16), SDS((nc, N_PAIR, LANE, LANE), F32)],
        scratch_shapes=[pltpu.VMEM((N_PAIR, LANE, LANE), F32)],
        compiler_params=_params(("arbitrary",)),
    )(xa, proj, proj, *small)


def _ssd_bwd(xa, proj, states, dy, dtb, alog, dsk, ng, name):
    t = xa.shape[0]
    nc, ix = _ssd_specs(t, True)

    def body(xa_ref, dt_ref, z_ref, st_ref, dy_ref, dtb_ref, alog_ref, dsk_ref, ng_ref,
             dxa_ref, ddt_ref, dz_ref, gdtb_ref, galog_ref, gdsk_ref, gng_ref, carry_ref):
        first = pl.program_id(0) == 0

        @pl.when(first)
        def _():
            carry_ref[...] = jnp.zeros_like(carry_ref)

        args = (st_ref[...], xa_ref[...], dt_ref[...], z_ref[...], dtb_ref[...], alog_ref[...], dsk_ref[...], ng_ref[...])
        _, vjp = jax.vjp(_ssd_chunk, *args)
        ds, dxa, ddt, dz, gdtb, galog, gdsk, gng = vjp((dy_ref[...], carry_ref[...]))
        carry_ref[...] = ds
        dxa_ref[...] = dxa
        ddt_ref[...] = ddt.astype(BF16)
        dz_ref[...] = dz.astype(BF16)
        for r, g in ((gdtb_ref, gdtb), (galog_ref, galog), (gdsk_ref, gdsk), (gng_ref, gng)):
            @pl.when(first)
            def _(r=r, g=g):
                r[...] = g

            @pl.when(jnp.logical_not(first))
            def _(r=r, g=g):
                r[...] += g

    small = [dtb, alog, dsk, ng]
    return _pallas(
        body, name=name, grid=(nc,),
        in_specs=[pl.BlockSpec((CHUNK, SSM_XBC), lambda c: (ix(c), 0)),
                  pl.BlockSpec((CHUNK, LANE), lambda c: (ix(c), DT_COL // LANE)),
                  pl.BlockSpec((CHUNK, D), lambda c: (ix(c), 2)),
                  pl.BlockSpec((None, N_PAIR, LANE, LANE), lambda c: (ix(c), 0, 0, 0)),
                  pl.BlockSpec((CHUNK, D), lambda c: (ix(c), 0))] + [_whole(p) for p in small],
        out_specs=[pl.BlockSpec((CHUNK, SSM_XBC), lambda c: (ix(c), 0)), pl.BlockSpec((CHUNK, LANE), lambda c: (ix(c), 0)),
                   pl.BlockSpec((CHUNK, D), lambda c: (ix(c), 0))] + [_whole(p) for p in small],
        out_shape=[SDS((t, SSM_XBC), F32), SDS((t, LANE), BF16), SDS((t, D), BF16)] + [SDS(p.shape, F32) for p in small],
        scratch_shapes=[pltpu.VMEM((N_PAIR, LANE, LANE), F32)],
        compiler_params=_params(("arbitrary",)),
    )(xa, proj, proj, states, dy, *small)


def _attn_block(q, kv_prev, kv_cur, cq, sq, ck, sk, sinks, rot, first_block):
    left, right = _half_masks()
    k2 = jnp.concatenate([kv_prev[:, :256], kv_cur[:, :256]], axis=0)
    v2 = jnp.concatenate([kv_prev[:, 256:], kv_cur[:, 256:]], axis=0)
    ri, ci = _iota((LANE, LANE), 0), _iota((LANE, LANE), 1)
    dup = [((ri < 64) & (ci % 64 == ri)).astype(BF16), ((ri >= 64) & (ci % 64 == ri - 64)).astype(BF16)]

    rot16 = rot.astype(BF16)

    def rope(tt, c, s):
        return tt * c + _spread(tt, rot16) * s

    kd, vd = [], []
    for j in range(4):
        sl = slice(LANE * (j // 2), LANE * (j // 2 + 1))
        kd.append(_bdot(rope(k2[:, sl], ck, sk), dup[j % 2], "nn"))
        vd.append(_bdot(v2[:, sl], dup[j % 2], "nn"))
    qi, si = _iota((2 * CHUNK, 2 * CHUNK), 0) % CHUNK, _iota((2 * CHUNK, 2 * CHUNK), 1)
    valid = (si > qi) & (si <= qi + CHUNK) & jnp.logical_or(si >= CHUNK, jnp.logical_not(first_block))
    upper = _iota((2 * CHUNK, 1), 0) < CHUNK
    lanes = _iota((1, LANE), 1)
    outs = []
    for pr in range(N_PAIR):
        qr = rope(q[:, LANE * pr:LANE * (pr + 1)], cq, sq)
        lg = _bdot(jnp.concatenate([qr * left, qr * right], axis=0), kd[pr // 2], "nt") * 0.125
        lg = jnp.where(valid, lg, -jnp.inf)
        s1 = jnp.sum(jnp.where(lanes == 2 * pr, sinks, 0.0), axis=1, keepdims=True)
        s2 = jnp.sum(jnp.where(lanes == 2 * pr + 1, sinks, 0.0), axis=1, keepdims=True)
        sink = jnp.where(upper, s1, s2)
        mx = lax.stop_gradient(jnp.maximum(jnp.max(lg, axis=-1, keepdims=True), sink))
        e = jnp.exp(lg - mx)
        probs = e / (jnp.sum(e, axis=-1, keepdims=True) + jnp.exp(sink - mx))
        o2 = _bdot(probs, vd[pr // 2], "nn")
        outs.append(o2[:CHUNK] * left + o2[CHUNK:] * right)
    return jnp.concatenate(outs, axis=1)


def _attn_fwd(qkv, cos, sin, sinks, rot, name):
    t = qkv.shape[0]
    nb = t // CHUNK

    def body(q_ref, kvp_ref, kvc_ref, cq_ref, sq_ref, cp_ref, sp_ref, sinks_ref, rot_ref, o_ref):
        ck = jnp.concatenate([cp_ref[...], cq_ref[...]], axis=0)
        sk = jnp.concatenate([sp_ref[...], sq_ref[...]], axis=0)
        o_ref[...] = _attn_block(q_ref[...], kvp_ref[...], kvc_ref[...], cq_ref[...], sq_ref[...], ck, sk,
                                 sinks_ref[...], rot_ref[...], pl.program_id(0) == 0).astype(BF16)

    prev = lambda n: jnp.maximum(n - 1, 0)
    return _pallas(
        body, name=name, grid=(nb,),
        in_specs=[pl.BlockSpec((CHUNK, D), lambda n: (n, 0)),
                  pl.BlockSpec((CHUNK, 512), lambda n: (prev(n), 2)), pl.BlockSpec((CHUNK, 512), lambda n: (n, 2)),
                  pl.BlockSpec((CHUNK, LANE), lambda n: (n, 0)), pl.BlockSpec((CHUNK, LANE), lambda n: (n, 0)),
                  pl.BlockSpec((CHUNK, LANE), lambda n: (prev(n), 0)), pl.BlockSpec((CHUNK, LANE), lambda n: (prev(n), 0)),
                  _whole(sinks), _whole(rot)],
        out_specs=pl.BlockSpec((CHUNK, D), lambda n: (n, 0)), out_shape=SDS((t, D), BF16),
        compiler_params=_params(("parallel",)),
    )(qkv, qkv, qkv, cos, sin, cos, sin, sinks, rot)


def _attn_bwd(qkv, do, cos, sin, sinks, rot, name):
    t = qkv.shape[0]
    nb = t // CHUNK

    def body(q_ref, kvp_ref, kvc_ref, do_ref, cq_ref, sq_ref, cp_ref, sp_ref, sinks_ref, rot_ref,
             dq_ref, dkv_ref, dbq_ref, dbkv_ref, dsink_ref, carry_ref):
        n = pl.program_id(0)

        @pl.when(n == 0)
        def _():
            carry_ref[...] = jnp.zeros_like(carry_ref)
            dbq_ref[...] = jnp.zeros_like(dbq_ref)
            dbkv_ref[...] = jnp.zeros_like(dbkv_ref)
            dsink_ref[...] = jnp.zeros_like(dsink_ref)

        @pl.when(n < nb)
        def _():
            ck = jnp.concatenate([cp_ref[...], cq_ref[...]], axis=0)
            sk = jnp.concatenate([sp_ref[...], sq_ref[...]], axis=0)
            f = lambda q, kvp, kvc, s: _attn_block(q, kvp, kvc, cq_ref[...], sq_ref[...], ck, sk, s, rot_ref[...], n == 0)
            _, vjp = jax.vjp(f, q_ref[...], kvp_ref[...], kvc_ref[...], sinks_ref[...])
            dq, dkvp, dkvc, ds = vjp(do_ref[...].astype(F32))
            done = carry_ref[...] + dkvp
            dq_ref[...] = dq.astype(BF16)
            dkv_ref[...] = done.astype(BF16)
            dbq_ref[...] += jnp.sum(dq, axis=0, keepdims=True)
            dsink_ref[...] += ds
            carry_ref[...] = dkvc

            @pl.when(n > 0)
            def _():
                dbkv_ref[...] += jnp.sum(done, axis=0, keepdims=True)

        @pl.when(n == nb)
        def _():
            done = carry_ref[...]
            dkv_ref[...] = done.astype(BF16)
            dbkv_ref[...] += jnp.sum(done, axis=0, keepdims=True)

    cur = lambda n: jnp.minimum(n, nb - 1)
    prev = lambda n: jnp.maximum(jnp.minimum(n, nb - 1) - 1, 0)
    fin = lambda n: jnp.maximum(n - 1, 0)
    outs = _pallas(
        body, name=name, grid=(nb + 1,),
        in_specs=[pl.BlockSpec((CHUNK, D), lambda n: (cur(n), 0)),
                  pl.BlockSpec((CHUNK, 512), lambda n: (prev(n), 2)), pl.BlockSpec((CHUNK, 512), lambda n: (cur(n), 2)),
                  pl.BlockSpec((CHUNK, D), lambda n: (cur(n), 0)),
                  pl.BlockSpec((CHUNK, LANE), lambda n: (cur(n), 0)), pl.BlockSpec((CHUNK, LANE), lambda n: (cur(n), 0)),
                  pl.BlockSpec((CHUNK, LANE), lambda n: (prev(n), 0)), pl.BlockSpec((CHUNK, LANE), lambda n: (prev(n), 0)),
                  _whole(sinks), _whole(rot)],
        out_specs=[pl.BlockSpec((CHUNK, D), lambda n: (cur(n), 0)), pl.BlockSpec((CHUNK, 512), lambda n: (fin(n), 0)),
                   pl.BlockSpec((1, D), lambda n: (0, 0)), pl.BlockSpec((1, 512), lambda n: (0, 0)), _whole(sinks)],
        out_shape=[SDS((t, D), BF16), SDS((t, 512), BF16), SDS((1, D), F32), SDS((1, 512), F32), SDS(sinks.shape, F32)],
        scratch_shapes=[pltpu.VMEM((CHUNK, 512), F32)],
        compiler_params=_params(("arbitrary",)),
    )(qkv, qkv, qkv, do, cos, sin, cos, sin, sinks, rot)
    dq, dkv, dbq, dbkv, dsinks = outs
    return jnp.concatenate([dq, dkv], axis=1), jnp.concatenate([dbq, dbkv], axis=1), dsinks


def _loss_head(h, tgt, g, name, tm=512):
    t = h.shape[0]

    def body(h_ref, t_ref, g_ref, loss_ref, dh_ref, dhb_ref, dg_ref):
        def f(hv, gv):
            err = _rms_fn(hv, gv)[0] - t_ref[...]
            return 0.5 * jnp.sum(jnp.mean(err * err, axis=-1, keepdims=True), axis=0, keepdims=True)

        loss, vjp = jax.vjp(f, h_ref[...], g_ref[...])
        dh, dg = vjp(jnp.ones((1, 1), F32))
        dh_ref[...] = dh
        dhb_ref[...] = dh.astype(BF16)
        first = pl.program_id(0) == 0

        @pl.when(first)
        def _():
            loss_ref[...] = loss
            dg_ref[...] = dg

        @pl.when(jnp.logical_not(first))
        def _():
            loss_ref[...] += loss
            dg_ref[...] += dg

    return _pallas(
        body, name=name, grid=(t // tm,),
        in_specs=[_tok(D, tm), _tok(D, tm), _whole(g)],
        out_specs=[pl.BlockSpec((1, 1), lambda i: (0, 0)), _tok(D, tm), _tok(D, tm), _whole(g)],
        out_shape=[SDS((1, 1), F32), SDS((t, D), F32), SDS((t, D), BF16), SDS(g.shape, F32)],
        compiler_params=_params(("arbitrary",)),
    )(h, tgt, g)


def _res_half(acc, res):
    return (res + 0.5 * acc,)


def _res_full(acc, res):
    return (res + acc,)


def _half(acc):
    return (0.5 * acc,)


def _ffn_in(n, w_in, name, tm=1024):
    t = n.shape[0]
    tm = min(tm, t)

    def body(n_ref, w_ref, pre_ref, act_ref):
        a = n_ref[...]
        gate = lax.dot_general(a, w_ref[0], _DIMS["nt"], preferred_element_type=F32)
        up = lax.dot_general(a, w_ref[1], _DIMS["nt"], preferred_element_type=F32)
        pre_ref[0] = gate.astype(BF16)
        pre_ref[1] = up.astype(BF16)
        act_ref[...] = (gate * jax.nn.sigmoid(gate) * up).astype(BF16)

    pair = pl.BlockSpec((2, None, tm, FF_SHARD), lambda i, j: (0, j, i, 0))
    return _pallas(
        body, name=name, grid=(t // tm, 4),
        in_specs=[pl.BlockSpec((tm, D), lambda i, j: (i, 0)), pl.BlockSpec((2, None, FF_SHARD, D), lambda i, j: (0, j, 0, 0))],
        out_specs=[pair, pl.BlockSpec((None, tm, FF_SHARD), lambda i, j: (j, i, 0))],
        out_shape=[SDS((2, 4, t, FF_SHARD), BF16), SDS((4, t, FF_SHARD), BF16)],
        compiler_params=_params(("parallel", "parallel")),
    )(n, w_in.reshape(2, 4, FF_SHARD, D))


def _ffn_dact(dhb, w_out, pre, name, tm=1024, deps=()):
    t = dhb.shape[0]
    tm = min(tm, t)

    def body(d_ref, w_ref, pre_ref, *rest):
        o_ref = rest[-1]
        dact = 0.5 * lax.dot_general(d_ref[...], w_ref[...], _DIMS["nt"], preferred_element_type=F32)
        gate, up = pre_ref[0].astype(F32), pre_ref[1].astype(F32)
        sg = jax.nn.sigmoid(gate)
        o_ref[0] = (dact * up * (sg * (1.0 + gate * (1.0 - sg)))).astype(BF16)
        o_ref[1] = (dact * (gate * sg)).astype(BF16)

    pair = pl.BlockSpec((2, None, tm, FF_SHARD), lambda i, j: (0, j, i, 0))
    return _pallas(
        body, name=name, grid=(t // tm, 4),
        in_specs=[pl.BlockSpec((tm, D), lambda i, j: (i, 0)), pl.BlockSpec((None, FF_SHARD, D), lambda i, j: (j, 0, 0)), pair]
        + [ANY_SPEC] * len(deps),
        out_specs=pair, out_shape=SDS((2, 4, t, FF_SHARD), BF16),
        compiler_params=_params(("parallel", "parallel")),
    )(dhb, w_out, pre, *deps)


def _ffn_fwd(h, g, w_in, w_out, tag, deps=()):
    n = _rms(h, g, f"{tag}_rms", deps=deps)
    pre, act = _ffn_in(n, w_in, f"{tag}_in")
    w_out = w_out() if callable(w_out) else w_out
    out = _mm(act, w_out, reduce_j=True, tk=FF_SHARD, epi=_res_half, extras=(h[None],), name=f"{tag}_out")[0][0]
    return out, (h, n, pre, act)


def _ffn_bwd(dh, dhb, saved, g, w_in, w_out, tag, deps=(), hook=None, weights_hook=None):
    h, n, pre, act = saved
    t = h.shape[0]
    dpre = _ffn_dact(dhb, w_out, pre, f"{tag}_dact", deps=deps).reshape(N_DEV, t, FF_SHARD)
    dw_out = _mm(act, dhb[None], ta=True, tm=FF_SHARD, epi=_half, out_dtypes=(BF16,), deps=hook(dpre) if hook else (),
                 name=f"{tag}_dwout")[0]
    dw_in = _mm(dpre, n[None], ta=True, tm=FF_SHARD, out_dtypes=(BF16,), name=f"{tag}_dwin")[0]
    if weights_hook:
        weights_hook(dw_in, dw_out)
    dh_in, dhb_in, dg = _mm_drms(dpre, w_in, h, g, dh, f"{tag}_dn", FF_SHARD, tb=False)
    return dh_in, dhb_in, dg, dw_in, dw_out


def _ple_fwd(h, g, pb, w_gate, w_proj, tag):
    t = h.shape[0]
    tm = 512
    n = _rms(h, g, f"{tag}_rms")
    e = _mm(pb[None], w_proj[None], name=f"{tag}_proj")[0][0]
    z = _mm(n[None], w_gate[None], name=f"{tag}_gate")[0][0]
    out = _rowop(lambda zz, ee, hh: (hh + _ple_fn(zz, ee)[0],), [(z, _tok(D, tm)), (e, _tok(D, tm)), (h, _tok(D, tm))], [],
                 [((t, D), F32, _tok(D, tm))], grid=(t // tm,), name=f"{tag}_mix")[0]
    return out, (h, n, e, z)


def _ple_bwd(dh, dhb, saved, g, pb, w_gate, tag, deps=()):
    h, n, e, z = saved
    t = h.shape[0]
    tm = 512
    dz, de = _rowop_bwd(_ple_fn, [(z, _tok(D, tm)), (e, _tok(D, tm))], [], [(dh, _tok(D, tm))], [(0,), (1,)],
                        [((t, D), (BF16,), _tok(D, tm)), ((t, D), (BF16,), _tok(D, tm))], grid=(t // tm,), name=f"{tag}_dmix",
                        deps=deps)
    dw_proj = _mm(pb[None], de[None], ta=True, out_dtypes=(BF16,), name=f"{tag}_dwproj")[0][0]
    dw_gate = _mm(n[None], dz[None], ta=True, out_dtypes=(BF16,), name=f"{tag}_dwgate")[0][0]
    dh_in, dhb_in, dg = _mm_drms(dz[None], w_gate[None], h, g, dh, f"{tag}_dn", 1024)
    return dh_in, dhb_in, dg, dw_gate, dw_proj


def _hyb_fwd(h, w, tag, after_in=None):
    t = h.shape[0]
    tm = 512
    hn = _rms(h, w["norm_mix"], f"{tag}_rms")
    proj = _mm(hn[None], w["hyb_in"][None], tn=512, name=f"{tag}_in")[0][0]
    if after_in:
        after_in()
    u1 = _dwconv([(proj, 0), (proj, D // LANE)], w["conv_w"], w["conv_b"], width=CONV_W, glu=True, silu=False, cb=LANE,
                 name=f"{tag}_conv")
    u = _rowop(_lnswish_fn, [(u1, _tok(D, tm))], [w["ln_g"], w["ln_b"]], [((t, D), BF16, _tok(D, tm))], grid=(t // tm,),
               name=f"{tag}_ln")[0]
    xa = _dwconv([(proj, 3 * D // LANE)], w["sconv_w"], w["sconv_b"], width=SSM_CONV, glu=False, silu=True, cb=LANE,
                 name=f"{tag}_sconv")
    y, states = _ssd_fwd(xa, proj, w["dt_bias"], w["a_log"], w["d_skip"], w["ssm_norm"], f"{tag}_ssd")
    mixed = jnp.stack([u, y], axis=0)
    out = _mm(mixed, w["hyb_out"], reduce_j=True, epi=_res_full, extras=(h[None],), name=f"{tag}_out")[0][0]
    return out, (h, hn, proj, u1, xa, states, mixed)


def _hyb_bwd(dh, dhb, saved, w, tag):
    h, hn, proj, u1, xa, states, mixed = saved
    t = h.shape[0]
    tm = 512
    dmix = _mm(dhb[None], w["hyb_out"], tb=True, name=f"{tag}_dmix")[0]
    dw_out = _mm(mixed, dhb[None], ta=True, out_dtypes=(BF16,), name=f"{tag}_dwout")[0]
    du1, dln_g, dln_b = _rowop_bwd(_lnswish_fn, [(u1, _tok(D, tm))], [w["ln_g"], w["ln_b"]], [(dmix[0], _tok(D, tm))], [(0,)],
                                   [((t, D), (F32,), _tok(D, tm))], grid=(t // tm,), name=f"{tag}_dln")
    dval, dgate, dconv_w, dconv_b = _dwconv_bwd([(proj, 0), (proj, D // LANE)], w["conv_w"], w["conv_b"], du1,
                                                width=CONV_W, glu=True, silu=False, cb=LANE, name=f"{tag}_dconv")
    dxa, ddt, dz, g_dtb, g_alog, g_dsk, g_ng = _ssd_bwd(xa, proj, states, dmix[1], w["dt_bias"], w["a_log"], w["d_skip"],
                                                         w["ssm_norm"], f"{tag}_dssd")
    dxbc, dsconv_w, dsconv_b = _dwconv_bwd([(proj, 3 * D // LANE)], w["sconv_w"], w["sconv_b"], dxa, width=SSM_CONV,
                                           glu=False, silu=True, cb=LANE, name=f"{tag}_dsconv")
    dproj = jnp.concatenate([dval, dgate, dz, dxbc, ddt, jnp.zeros((t, HYB_PAD - DT_COL - LANE), BF16)], axis=1)
    dh_in, dhb_in, dg = _mm_drms(dproj[None], w["hyb_in"][None], h, w["norm_mix"], dh, f"{tag}_dhn", 1024)
    dw_in = _mm(hn[None], dproj[None], ta=True, tn=512, out_dtypes=(BF16,), name=f"{tag}_dwin")[0][0]
    grads = dict(norm_mix=dg, hyb_in=dw_in, hyb_out=dw_out, conv_w=dconv_w, conv_b=dconv_b, ln_g=dln_g, ln_b=dln_b,
                 sconv_w=dsconv_w, sconv_b=dsconv_b, dt_bias=g_dtb, a_log=g_alog, d_skip=g_dsk, ssm_norm=g_ng)
    return dh_in, dhb_in, grads


def _bias_epi(acc, row):
    return (acc + row,)


def _res_bias_epi(acc, res, row):
    return (res + acc + row,)


def _att_fwd(h, w, tables, tag):
    cos, sin, rot = tables
    hn = _rms(h, w["norm_mix"], f"{tag}_rms")
    qkv = _mm(hn[None], w["qkv"][None], tb=True, tn=512, epi=_bias_epi, rows=(w["b_qkv"],), name=f"{tag}_qkv")[0][0]
    o = _attn_fwd(qkv, cos, sin, w["sinks"], rot, f"{tag}_core")
    out = _mm(o[None], w["w_o"][None], epi=_res_bias_epi, extras=(h[None],), rows=(w["b_o"],), name=f"{tag}_out")[0][0]
    return out, (h, hn, qkv, o)


def _att_bwd(dh, dhb, saved, w, tables, tag):
    cos, sin, rot = tables
    h, hn, qkv, o = saved
    t = h.shape[0]
    tm = 512
    do = _mm(dhb[None], w["w_o"][None], tb=True, out_dtypes=(BF16,), name=f"{tag}_do")[0][0]
    dw_o = _mm(o[None], dhb[None], ta=True, out_dtypes=(BF16,), name=f"{tag}_dwo")[0][0]
    db_o = _rowop_bwd(lambda xx, bb: (xx + bb,), [(dh, _tok(D, tm))], [w["b_o"]], [(dh, _tok(D, tm))], [], [],
                      grid=(t // tm,), name=f"{tag}_dbo")[0]
    dqkv, db_qkv, dsinks = _attn_bwd(qkv, do, cos, sin, w["sinks"], rot, f"{tag}_dcore")
    dh_in, dhb_in, dg = _mm_drms(dqkv[None], w["qkv"][None], h, w["norm_mix"], dh, f"{tag}_dhn", 512, tb=False)
    dw_qkv = _mm(dqkv[None], hn[None], ta=True, tm=512, out_dtypes=(BF16,), name=f"{tag}_dwqkv")[0][0]
    grads = dict(norm_mix=dg, qkv=dw_qkv, b_qkv=db_qkv, sinks=dsinks, w_o=dw_o, b_o=db_o)
    return dh_in, dhb_in, grads


def _rope_tables(t):
    inv = ROPE_THETA ** (-jnp.arange(0, 64, 2, dtype=F32) / 64)
    ang = jnp.arange(t, dtype=F32)[:, None] * inv[None, :]
    cos, sin = jnp.tile(jnp.cos(ang), (1, 4)), jnp.tile(jnp.sin(ang), (1, 4))
    rot = np.zeros((LANE, LANE), np.float32)
    for j in range(LANE):
        if j % 64 < 32:
            rot[j + 32, j] = -1.0
        else:
            rot[j - 32, j] = 1.0
    return cos, sin, jnp.asarray(rot)


def _local_step(x, p, tgt, layers, final_norm):
    _restart_chain()
    tables = _rope_tables(x.shape[0])
    pb = p.astype(BF16)
    h, saved = x, []
    for i, w in enumerate(layers):
        h, s = _layer_fwd(i, h, w, pb[i], tables)
        saved.append(s)
    loss, dh, dhb, d_final = _loss_head(h, tgt, final_norm, "loss_head")
    grads = [None] * len(layers)
    for i in reversed(range(len(layers))):
        dh, dhb, head = _layer_bwd_head(i, dh, dhb, saved[i], layers[i], pb[i])
        dh, dhb, tail = _layer_bwd_tail(i, dh, dhb, saved[i], layers[i], tables)
        grads[i] = {**head, **tail}
    return loss[0, 0], dh, grads, d_final


def _layer_fwd(i, h, w, pb, tables, deps=()):
    s = {}
    h, s["ffn1"] = _ffn_fwd(h, w["norm_ffn1"], w["ffn1_in"], w["ffn1_out"], f"l{i}_ffn1", deps=deps)
    if i % 2 == 0:
        h, s["mix"] = _hyb_fwd(h, w, f"l{i}_hyb")
    else:
        h, s["mix"] = _att_fwd(h, w, tables, f"l{i}_att")
    h, s["ffn2"] = _ffn_fwd(h, w["norm_ffn2"], w["ffn2_in"], w["ffn2_out"], f"l{i}_ffn2")
    h, s["ple"] = _ple_fwd(h, w["ple_norm"], pb, w["ple_gate"], w["ple_proj"], f"l{i}_ple")
    return h, s


def _layer_bwd_head(i, dh, dhb, s, w, pb, deps=()):
    g = {}
    dh, dhb, g["ple_norm"], g["ple_gate"], g["ple_proj"] = _ple_bwd(dh, dhb, s["ple"], w["ple_norm"], pb, w["ple_gate"],
                                                                    f"l{i}_ple", deps=deps)
    return dh, dhb, g


def _layer_bwd_tail(i, dh, dhb, s, w, tables, deps=()):
    g = {}
    dh, dhb, g["norm_ffn2"], g["ffn2_in"], g["ffn2_out"] = _ffn_bwd(dh, dhb, s["ffn2"], w["norm_ffn2"], w["ffn2_in"],
                                                                    w["ffn2_out"], f"l{i}_ffn2", deps=deps)
    if i % 2 == 0:
        dh, dhb, gm = _hyb_bwd(dh, dhb, s["mix"], w, f"l{i}_hyb")
    else:
        dh, dhb, gm = _att_bwd(dh, dhb, s["mix"], w, tables, f"l{i}_att")
    g.update(gm)
    dh, dhb, g["norm_ffn1"], g["ffn1_in"], g["ffn1_out"] = _ffn_bwd(dh, dhb, s["ffn1"], w["norm_ffn1"], w["ffn1_in"],
                                                                    w["ffn1_out"], f"l{i}_ffn1")
    return dh, dhb, g


def _cols(g):
    full = jnp.moveaxis(g, 0, -2)
    return full.reshape(*full.shape[:-2], N_DEV * g.shape[-1])


def _uncols(full):
    split = full.reshape(*full.shape[:-1], N_DEV, full.shape[-1] // N_DEV)
    return jnp.moveaxis(split, -2, 0)


def _lane_pad(v):
    return jnp.pad(v, ((0, 0), (0, LANE - v.shape[1])))


def _build_layers(gw, gs, rep):
    return [_build_layer(i, gw, gs, rep) for i in range(2)]


def _build_layer(i, gw, gs, rep, parts=("ffn1", "mix", "ffn2", "ple")):
    w = {}
    for f in ("ffn1", "ffn2"):
        if f in parts:
            w[f"norm_{f}"] = rep[f"norm_{f}"][i][None]
            w[f"{f}_in"] = gw[f"{f}_w_in", i]
            w[f"{f}_out"] = gw[f"{f}_w_out", i].reshape(4, FF_SHARD, D)
    if "ple" in parts:
        w["ple_norm"] = rep["ple_norm"][i][None]
        w["ple_gate"] = gw["ple_gate_w", i].reshape(D, D)
        w["ple_proj"] = _cols(gw["ple_proj_w", i])
    if "mix" not in parts:
        return w
    w["norm_mix"] = rep["norm_mix"][i][None]
    if i == 0:
        w["hyb_in"] = jnp.pad(_cols(gw["hyb_w_in", 0]), ((0, 0), (0, HYB_PAD - HYB_IN)))
        w["hyb_out"] = gw["hyb_w_out", 0].reshape(2, D, D)
        w["conv_w"] = _cols(gs["conv_dw_w"][:, 0])
        w["sconv_w"] = _cols(gs["ssm_conv_w"][:, 0])
        w["conv_b"], w["ln_g"], w["ln_b"] = rep["conv_dw_b"], rep["conv_ln_g"], rep["conv_ln_b"]
        w["sconv_b"], w["ssm_norm"] = rep["ssm_conv_b"], rep["ssm_norm"]
        w["dt_bias"], w["a_log"] = _lane_pad(rep["ssm_dt_bias"]), _lane_pad(rep["ssm_a_log"])
        w["d_skip"] = jnp.repeat(rep["ssm_d"], D // SSM_HEADS, axis=1)
    else:
        w["qkv"] = gw["att_w_qkv", 0].reshape(-1, D)
        w["w_o"] = gw["att_w_o", 0].reshape(D, D)
        w["b_qkv"] = gs["att_b_qkv"][:, 0].reshape(1, -1)
        w["b_o"] = gs["att_b_o"][:, 0].reshape(1, -1)
        w["sinks"] = _lane_pad(rep["att_sinks"])
    return w


def _big_grads(i, g):
    big = {}
    for f in ("ffn1", "ffn2"):
        if f"{f}_in" in g:
            big[f"{f}_w_in", i] = g[f"{f}_in"]
            big[f"{f}_w_out", i] = g[f"{f}_out"].reshape(N_DEV, D_FF // N_DEV, D)
    if "ple_gate" in g:
        big["ple_gate_w", i] = g["ple_gate"].reshape(N_DEV, D // N_DEV, D)
        big["ple_proj_w", i] = _uncols(g["ple_proj"])
    if "hyb_in" in g:
        big["hyb_w_in", 0] = _uncols(g["hyb_in"][:, :HYB_IN])
        big["hyb_w_out", 0] = g["hyb_out"].reshape(N_DEV, 2 * D // N_DEV, D)
    if "qkv" in g:
        big["att_w_qkv", 0] = g["qkv"].reshape(N_DEV, -1, D)
        big["att_w_o", 0] = g["w_o"].reshape(N_DEV, D // N_DEV, D)
    return big


def _collect_grads(grads, d_final):
    g0, g1 = grads
    big, small = {**_big_grads(0, g0), **_big_grads(1, g1)}, {}
    for f in ("ffn1", "ffn2"):
        small[f"norm_{f}"] = jnp.concatenate([g[f"norm_{f}"] for g in grads], axis=0)
    small["norm_mix"] = jnp.concatenate([g["norm_mix"] for g in grads], axis=0)
    small["ple_norm"] = jnp.concatenate([g["ple_norm"] for g in grads], axis=0)
    small["conv_dw_w"] = g0["conv_w"][None]
    small["conv_dw_b"], small["conv_ln_g"], small["conv_ln_b"] = g0["conv_b"], g0["ln_g"], g0["ln_b"]
    small["ssm_conv_w"] = g0["sconv_w"][None]
    small["ssm_conv_b"], small["ssm_norm"] = g0["sconv_b"], g0["ssm_norm"]
    small["ssm_dt_bias"], small["ssm_a_log"] = g0["dt_bias"][:, :SSM_HEADS], g0["a_log"][:, :SSM_HEADS]
    small["ssm_d"] = g0["d_skip"].reshape(1, SSM_HEADS, D // SSM_HEADS).sum(axis=-1)
    small["att_b_qkv"], small["att_b_o"] = g1["b_qkv"], g1["b_o"]
    small["att_sinks"] = g1["sinks"][:, :SSM_HEADS]
    small["final_norm"] = d_final[0]
    return big, small


MESH = pl.DeviceIdType.MESH


def _place():
    return lax.axis_index("x"), lax.axis_index("y"), lax.axis_index("c")


def _all_gather(blocks, space, name):
    nb = len(blocks)

    def body(*refs):
        x_refs, out_refs, (send_sems, recv_sems, local_sem) = refs[:nb], refs[nb:2 * nb], refs[2 * nb:]
        x, y, c = _place()
        me, sibling = (x, y, c), (x, y, 1 - c)
        chips = [(1 - x, y), (x, 1 - y), (1 - x, 1 - y)]

        def copies(k, blk, to, own=False):
            idx = 4 * blk[0] + 2 * blk[1] + blk[2]
            return [pltpu.make_async_remote_copy(src_ref=x_ref if own else out_ref.at[idx], dst_ref=out_ref.at[idx],
                                                 send_sem=send_sems.at[k, b], recv_sem=recv_sems.at[k, b], device_id=to,
                                                 device_id_type=MESH) for b, (x_ref, out_ref) in enumerate(zip(x_refs, out_refs))]

        mine = [pltpu.make_async_copy(x_ref, out_ref.at[4 * x + 2 * y + c], local_sem.at[b])
                for b, (x_ref, out_ref) in enumerate(zip(x_refs, out_refs))]
        first = copies(0, me, sibling, own=True)
        for j, chip in enumerate(chips):
            first += copies(1 + j, me, (*chip, c), own=True)
        for cp in mine + first:
            cp.start()
        passed = []
        for j, chip in enumerate(chips):
            for cp in copies(1 + j, (*chip, c), me):
                cp.wait_recv()
            onward = copies(4 + j, (*chip, c), sibling)
            for cp in onward:
                cp.start()
            passed += onward
        for cp in copies(0, sibling, me):
            cp.wait_recv()
        for j, chip in enumerate(chips):
            for cp in copies(4 + j, (*chip, 1 - c), me):
                cp.wait_recv()
        for cp in first + passed:
            cp.wait_send()
        for cp in mine:
            cp.wait()

    spec = pl.BlockSpec(memory_space=space)
    return _pallas(
        body, name=name, out_shape=[SDS((N_DEV,) + b.shape, b.dtype) for b in blocks],
        in_specs=[spec] * nb, out_specs=[spec] * nb,
        scratch_shapes=[pltpu.SemaphoreType.DMA((7, nb)), pltpu.SemaphoreType.DMA((7, nb)), pltpu.SemaphoreType.DMA((nb,))],
    )(*blocks)


HBM_SPEC = pl.BlockSpec(memory_space=pltpu.HBM)
SEM_SPEC = pl.BlockSpec(memory_space=pltpu.SEMAPHORE)
EFFECT = pltpu.SideEffectType.DATAFLOW_SIDE_EFFECTING


def _plan_descriptors(plan, srcs, lands, send_sems, recv_sems, local_sems, arriving):
    remote, local = plan(*_place())

    def pick(si, slot):
        ref = lands[si[1]] if isinstance(si, tuple) else srcs[si]
        return ref if slot is None else ref.at[slot]

    rem = [pltpu.make_async_remote_copy(src_ref=pick(si, ss), dst_ref=lands[li].at[rs if arriving else ds],
                                        send_sem=send_sems.at[k], recv_sem=recv_sems.at[k], device_id=dev, device_id_type=MESH)
           for k, (si, ss, li, ds, dev, rs) in enumerate(remote)]
    loc = [pltpu.make_async_copy(pick(si, ss), lands[li].at[ds], local_sems.at[k])
           for k, (si, ss, li, ds) in enumerate(local)]
    return rem, loc


def _plan_counts(plan):
    remote, local = plan(0, 0, 0)
    return len(remote), max(len(local), 1)


def _exchange_start(srcs, land_shapes, plan, name, lands=None):
    ns, nl = len(srcs), len(lands if lands is not None else land_shapes)
    n_remote, n_local = _plan_counts(plan)
    if lands is None:
        lands = [pltpu.with_memory_space_constraint(lax.empty(s.shape, s.dtype), pltpu.HBM) for s in land_shapes]
    lands = list(lands)
    srcs = [pltpu.with_memory_space_constraint(s, pltpu.HBM) for s in srcs]

    def body(*refs):
        src_refs, land_refs = refs[:ns], refs[ns:ns + nl]
        send_sems, recv_sems, local_sems = refs[ns + nl:ns + nl + 3]
        token = refs[-1]
        rem, loc = _plan_descriptors(plan, src_refs, land_refs, send_sems, recv_sems, local_sems, arriving=False)
        for cp in loc + rem:
            cp.start()
        token[...] = jnp.zeros_like(token)

    outs = _pallas(
        body, name=name,
        out_shape=[pltpu.SemaphoreType.DMA((n_remote,)), pltpu.SemaphoreType.DMA((n_remote,)), pltpu.SemaphoreType.DMA((n_local,))]
        + [pltpu.HBM(a.shape, a.dtype) for a in srcs + lands] + [SDS((8, LANE), F32)],
        in_specs=[HBM_SPEC] * (ns + nl),
        out_specs=[SEM_SPEC] * 3 + [HBM_SPEC] * (ns + nl) + [pl.BlockSpec(memory_space=pltpu.VMEM)],
        input_output_aliases={i: 3 + i for i in range(ns + nl)},
        compiler_params=pltpu.CompilerParams(has_side_effects=EFFECT),
    )(*srcs, *lands)
    return (outs[:3], outs[3:3 + ns], outs[3 + ns:3 + ns + nl]), outs[-1]


def _exchange_wait(state, after, plan, name):
    sems, srcs, lands = state
    ns, nl = len(srcs), len(lands)

    def body(*refs):
        src_refs, land_refs = refs[:ns], refs[ns:ns + nl]
        send_sems, recv_sems, local_sems = refs[ns + nl:ns + nl + 3]
        rem, loc = _plan_descriptors(plan, src_refs, land_refs, send_sems, recv_sems, local_sems, arriving=True)
        for cp in rem:
            cp.wait_send()
            cp.wait_recv()
        for cp in loc:
            cp.wait()

    outs = _pallas(
        body, name=name, out_shape=[pltpu.HBM(a.shape, a.dtype) for a in list(srcs) + list(lands)],
        in_specs=[HBM_SPEC] * (ns + nl) + [SEM_SPEC] * 3 + [ANY_SPEC] * (after is not None), out_specs=[HBM_SPEC] * (ns + nl),
        input_output_aliases={i: i for i in range(ns + nl)},
        compiler_params=pltpu.CompilerParams(has_side_effects=EFFECT),
    )(*srcs, *lands, *sems, *([after] if after is not None else []))
    return outs[:ns], outs[ns:]


def _gather_plan(nb):
    def plan(x, y, c):
        me = 4 * x + 2 * y + c
        peers = [(x, y, 1 - c), (1 - x, y, c), (x, 1 - y, c), (1 - x, 1 - y, c)]
        remote = [(b, None, b, me, peer, 4 * peer[0] + 2 * peer[1] + peer[2]) for b in range(nb) for peer in peers]
        return remote, [(b, None, b, me) for b in range(nb)]
    return plan


def _relay_plan(nb):
    def plan(x, y, c):
        chips = [(1 - x, y), (x, 1 - y), (1 - x, 1 - y)]
        remote = [(("land", b), 4 * cx + 2 * cy + c, b, 4 * cx + 2 * cy + c, (x, y, 1 - c), 4 * cx + 2 * cy + (1 - c))
                  for b in range(nb) for cx, cy in chips]
        return remote, []
    return plan


def _pair_plan(nb):
    def plan(x, y, c):
        return [(b, 2 * q + (1 - c), b, q, (x, y, 1 - c), q) for b in range(nb) for q in range(4)], []
    return plan


def _chip_plan(nb):
    def plan(x, y, c):
        own = 2 * x + y
        chips = [(1 - x, y), (x, 1 - y), (1 - x, 1 - y)]
        remote = [(b, 2 * cx + cy, b, own, (cx, cy, c), 2 * cx + cy) for b in range(nb) for cx, cy in chips]
        return remote, [(b, own, b, own) for b in range(nb)]
    return plan


def _row_tile(r, cap=4608):
    return max(d for d in range(16, min(r, cap) + 1, 16) if r % d == 0)


def _pair_add(parts, got, core, name):
    _, r, cdim = parts.shape
    tr = _row_tile(r)

    def body(core_ref, p_ref, g_ref, o_ref):
        o_ref[...] = (p_ref[...].astype(F32) + g_ref[...].astype(F32)).astype(o_ref.dtype)

    return pl.pallas_call(
        body, name=name, out_shape=SDS((4, r, cdim), BF16),
        grid_spec=pltpu.PrefetchScalarGridSpec(
            num_scalar_prefetch=1, grid=(4, r // tr),
            in_specs=[pl.BlockSpec((None, tr, cdim), lambda q, i, core_ref: (2 * q + core_ref[0], i, 0)),
                      pl.BlockSpec((None, tr, cdim), lambda q, i, core_ref: (q, i, 0))],
            out_specs=pl.BlockSpec((None, tr, cdim), lambda q, i, core_ref: (q, i, 0))),
        compiler_params=_params(("parallel", "parallel")),
    )(core, parts, got)


def _sum_slots(parts, name):
    nj, r, cdim = parts.shape
    tr = _row_tile(r)

    def body(p_ref, o_ref):
        acc = p_ref[0].astype(F32)
        for j in range(1, nj):
            acc = acc + p_ref[j].astype(F32)
        o_ref[...] = acc

    return _pallas(
        body, name=name, out_shape=SDS((r, cdim), F32), grid=(r // tr,),
        in_specs=[pl.BlockSpec((nj, tr, cdim), lambda i: (0, i, 0))], out_specs=pl.BlockSpec((tr, cdim), lambda i: (i, 0)),
        compiler_params=_params(("parallel",)),
    )(parts)


def _adamw_update(wv, gv, mv, vv):
    nm = ADAM_B1 * mv + (1.0 - ADAM_B1) * gv
    nv = ADAM_B2 * vv + (1.0 - ADAM_B2) * (gv * gv)
    m_hat = nm / (1.0 - ADAM_B1 ** ADAM_STEP)
    v_hat = nv / (1.0 - ADAM_B2 ** ADAM_STEP)
    return -ADAM_LR * (m_hat / (jnp.sqrt(v_hat) + ADAM_EPS) + ADAM_WD * wv), nm, nv


def _adamw_summed(w, m, v, by_chip, name):
    nl, r, cdim = w.shape
    tr = _row_tile(r, 512)
    nblk = r // tr

    def body(*refs):
        chip_refs, (w_ref, m_ref, v_ref, g_ref, d_ref, nm_ref, nv_ref) = refs[:nl], refs[nl:]
        layer = pl.program_id(0)
        gv = None
        for ll, c_ref in enumerate(chip_refs):
            s = c_ref[0].astype(F32)
            for q in range(1, 4):
                s = s + c_ref[q].astype(F32)
            gv = s if gv is None else jnp.where(layer == ll, s, gv)
        g_ref[...] = gv
        d_ref[...], nm_ref[...], nv_ref[...] = _adamw_update(w_ref[...], gv, m_ref[...], v_ref[...])

    def chip_map(ll):
        return lambda l, i: (0, jnp.where(l == ll, i, jnp.where(l > ll, nblk - 1, 0)), 0)

    spec = pl.BlockSpec((None, tr, cdim), lambda l, i: (l, i, 0))
    return _pallas(
        body, name=name, grid=(nl, nblk),
        in_specs=[pl.BlockSpec((4, tr, cdim), chip_map(ll)) for ll in range(nl)] + [spec] * 3,
        out_specs=[spec] * 4, out_shape=[SDS((nl, r, cdim), F32)] * 4,
        compiler_params=_params(("arbitrary", "arbitrary")),
    )(*by_chip, w, m, v)


def _adamw(w, g, m, v, name):
    shape = w.shape
    cdim = shape[-1]
    w2, g2, m2, v2 = (a.reshape(-1, cdim) for a in (w, g, m, v))
    r = w2.shape[0]
    tr = next(d for d in (512, 352, 256, 128, 64, 32, 16, 8, r) if r % d == 0)

    def body(w_ref, g_ref, m_ref, v_ref, d_ref, nm_ref, nv_ref):
        d_ref[...], nm_ref[...], nv_ref[...] = _adamw_update(w_ref[...], g_ref[...], m_ref[...], v_ref[...])

    spec = pl.BlockSpec((tr, cdim), lambda i: (i, 0))
    outs = _pallas(
        body, name=name, grid=(r // tr,), in_specs=[spec] * 4, out_specs=[spec] * 3, out_shape=[SDS((r, cdim), F32)] * 3,
        compiler_params=_params(("parallel",)),
    )(w2, g2, m2, v2)
    return tuple(o.reshape(shape) for o in outs)


WEIGHTS = ("norm_ffn1", "ffn1_w_in", "ffn1_w_out", "norm_mix", "norm_ffn2", "ffn2_w_in", "ffn2_w_out", "ple_norm", "ple_gate_w",
           "ple_proj_w", "hyb_w_in", "conv_dw_w", "conv_dw_b", "conv_ln_g", "conv_ln_b", "ssm_conv_w", "ssm_conv_b", "ssm_dt_bias",
           "ssm_a_log", "ssm_d", "ssm_norm", "hyb_w_out", "att_w_qkv", "att_b_qkv", "att_sinks", "att_w_o", "att_b_o", "final_norm")
BIG = ("ffn1_w_in", "ffn1_w_out", "ffn2_w_in", "ffn2_w_out", "ple_gate_w", "ple_proj_w", "hyb_w_in", "hyb_w_out", "att_w_qkv",
       "att_w_o")
SMALL_SHARDED = {"conv_dw_w": 2, "ssm_conv_w": 2, "att_b_qkv": 1, "att_b_o": 1}
SMALL = tuple(n for n in WEIGHTS if n not in BIG)
TRANSPOSED = ("ffn1_w_in", "ffn2_w_in", "att_w_qkv")
PACK_ROWS = 16


def _pack(arrays, lead=0):
    pieces = []
    for a in arrays:
        flat = a.reshape(*a.shape[:lead], -1)
        size = flat.shape[-1]
        padded = -(-size // (PACK_ROWS * LANE)) * PACK_ROWS * LANE
        flat = jnp.pad(flat, [(0, 0)] * lead + [(0, padded - size)])
        pieces.append(flat.reshape(*a.shape[:lead], padded // LANE, LANE))
    return jnp.concatenate(pieces, axis=lead)


def _unpack(buf, shapes, lead=0):
    out, row = [], 0
    for shape in shapes:
        size = math.prod(shape)
        rows = -(-size // (PACK_ROWS * LANE)) * PACK_ROWS
        piece = lax.slice_in_dim(buf, row, row + rows, axis=lead)
        piece = piece.reshape(*buf.shape[:lead], rows * LANE)
        out.append(lax.slice_in_dim(piece, 0, size, axis=lead).reshape(*buf.shape[:lead], *shape))
        row += rows
    return out


def kernel(x, p, norm_ffn1, ffn1_w_in, ffn1_w_out, norm_mix, norm_ffn2, ffn2_w_in, ffn2_w_out, ple_norm, ple_gate_w, ple_proj_w, hyb_w_in, conv_dw_w, conv_dw_b, conv_ln_g, conv_ln_b, ssm_conv_w, ssm_conv_b, ssm_dt_bias, ssm_a_log, ssm_d, ssm_norm, hyb_w_out, att_w_qkv, att_b_qkv, att_sinks, att_w_o, att_b_o, final_norm, loss_target, m_norm_ffn1, m_ffn1_w_in, m_ffn1_w_out, m_norm_mix, m_norm_ffn2, m_ffn2_w_in, m_ffn2_w_out, m_ple_norm, m_ple_gate_w, m_ple_proj_w, m_hyb_w_in, m_conv_dw_w, m_conv_dw_b, m_conv_ln_g, m_conv_ln_b, m_ssm_conv_w, m_ssm_conv_b, m_ssm_dt_bias, m_ssm_a_log, m_ssm_d, m_ssm_norm, m_hyb_w_out, m_att_w_qkv, m_att_b_qkv, m_att_sinks, m_att_w_o, m_att_b_o, m_final_norm, v_norm_ffn1, v_ffn1_w_in, v_ffn1_w_out, v_norm_mix, v_norm_ffn2, v_ffn2_w_in, v_ffn2_w_out, v_ple_norm, v_ple_gate_w, v_ple_proj_w, v_hyb_w_in, v_conv_dw_w, v_conv_dw_b, v_conv_ln_g, v_conv_ln_b, v_ssm_conv_w, v_ssm_conv_b, v_ssm_dt_bias, v_ssm_a_log, v_ssm_d, v_ssm_norm, v_hyb_w_out, v_att_w_qkv, v_att_b_qkv, v_att_sinks, v_att_w_o, v_att_b_o, v_final_norm):
    args = (norm_ffn1, ffn1_w_in, ffn1_w_out, norm_mix, norm_ffn2, ffn2_w_in, ffn2_w_out, ple_norm, ple_gate_w, ple_proj_w, hyb_w_in, conv_dw_w, conv_dw_b, conv_ln_g, conv_ln_b, ssm_conv_w, ssm_conv_b, ssm_dt_bias, ssm_a_log, ssm_d, ssm_norm, hyb_w_out, att_w_qkv, att_b_qkv, att_sinks, att_w_o, att_b_o, final_norm)
    moments_m = (m_norm_ffn1, m_ffn1_w_in, m_ffn1_w_out, m_norm_mix, m_norm_ffn2, m_ffn2_w_in, m_ffn2_w_out, m_ple_norm, m_ple_gate_w, m_ple_proj_w, m_hyb_w_in, m_conv_dw_w, m_conv_dw_b, m_conv_ln_g, m_conv_ln_b, m_ssm_conv_w, m_ssm_conv_b, m_ssm_dt_bias, m_ssm_a_log, m_ssm_d, m_ssm_norm, m_hyb_w_out, m_att_w_qkv, m_att_b_qkv, m_att_sinks, m_att_w_o, m_att_b_o, m_final_norm)
    moments_v = (v_norm_ffn1, v_ffn1_w_in, v_ffn1_w_out, v_norm_mix, v_norm_ffn2, v_ffn2_w_in, v_ffn2_w_out, v_ple_norm, v_ple_gate_w, v_ple_proj_w, v_hyb_w_in, v_conv_dw_w, v_conv_dw_b, v_conv_ln_g, v_conv_ln_b, v_ssm_conv_w, v_ssm_conv_b, v_ssm_dt_bias, v_ssm_a_log, v_ssm_d, v_ssm_norm, v_hyb_w_out, v_att_w_qkv, v_att_b_qkv, v_att_sinks, v_att_w_o, v_att_b_o, v_final_norm)
    w = dict(zip(WEIGHTS, args))
    m = dict(zip(WEIGHTS, moments_m))
    v = dict(zip(WEIGHTS, moments_v))
    cx, cy, cc = _place()
    me = 4 * cx + 2 * cy + cc

    core = jnp.reshape(cc, (1,)).astype(jnp.int32)
    layer_of = lambda n, i: 1 if n.startswith("att_") else i
    keys = [[(n, i) for n in BIG for i in range(w[n].shape[0]) if layer_of(n, i) == layer] for layer in range(2)]

    first = [key for key in keys[0] if key[0].startswith("ffn1")]
    mixer = [key for key in keys[0] if key[0].startswith("hyb")]
    rest0 = [key for key in keys[0] if key not in first + mixer]
    gw, by_chip = {}, {}
    view = lambda a, n: jnp.swapaxes(a, 1, 2) if n in TRANSPOSED else a
    block = lambda n, i: view(w[n], n)[i].astype(BF16)

    def gather_later(group, name):
        blocks = [block(n, i) for n, i in group]
        plan, relay_plan = _gather_plan(len(blocks)), _relay_plan(len(blocks))
        state, _ = _exchange_start(blocks, [SDS((N_DEV,) + b.shape, BF16) for b in blocks], plan, f"{name}_start")
        stage = {}

        def relay():
            _, landed = _exchange_wait(state, None, plan, f"{name}_wait")
            stage["relay"], _ = _exchange_start([], None, relay_plan, f"{name}_relay_start", lands=landed)

        def arrived():
            gw.update(zip(group, _exchange_wait(stage["relay"], None, relay_plan, f"{name}_relay_wait")[1]))

        return relay, arrived

    def reduce_later(group, big, name):
        pair_plan, chip_plan = _pair_plan(len(group)), _chip_plan(len(group))
        parts = [big[key] for key in group]
        pair, token = _exchange_start(parts, [SDS((4,) + pt.shape[1:], BF16) for pt in parts], pair_plan, f"{name}_pair_start")
        stage = {}

        def middle(after):
            thru, got = _exchange_wait(pair, after, pair_plan, f"{name}_pair_wait")
            sums = [_pair_add(pt, gt, core, f"grads_pair_add_{n}_{i}") for pt, gt, (n, i) in zip(thru, got, group)]
            stage["chip"], chip_token = _exchange_start(sums, [SDS(s.shape, BF16) for s in sums], chip_plan, f"{name}_chip_start")
            return chip_token

        def finish(after):
            by_chip.update(zip(group, _exchange_wait(stage["chip"], after, chip_plan, f"{name}_chip_wait")[1]))

        return token, middle, finish

    _restart_chain()
    gw["ffn1_w_in", 0], gathered_small = _all_gather([block("ffn1_w_in", 0), _pack([w[n] for n in SMALL_SHARDED])],
                                                      pltpu.HBM, "gather_weights_first")
    early_relay, early_arrived = gather_later([("ffn1_w_out", 0)], "gather_weights_early")
    mixer_relay, mixer_arrived = gather_later(mixer, "gather_weights_mixer")
    rest0_relay, rest0_arrived = gather_later(rest0, "gather_weights_rest")
    layer1_relay, layer1_arrived = gather_later(keys[1], "gather_weights_l1")
    gs = dict(zip(SMALL_SHARDED, _unpack(gathered_small, [w[n].shape for n in SMALL_SHARDED], lead=1)))
    rep = {n: w[n] for n in SMALL if n not in SMALL_SHARDED}

    tables = _rope_tables(x.shape[1])
    pb = p[:, 0].astype(BF16)
    w0, s0 = {}, {}

    def first_w_out():
        early_relay()
        early_arrived()
        w0.update(_build_layer(0, gw, gs, rep, parts=("ffn1",)))
        return w0["ffn1_out"]

    h, s0["ffn1"] = _ffn_fwd(x[0], rep["norm_ffn1"][0][None], gw["ffn1_w_in", 0], first_w_out, "l0_ffn1")
    mixer_relay()
    mixer_arrived()
    w0.update(_build_layer(0, gw, gs, rep, parts=("mix",)))
    h, s0["mix"] = _hyb_fwd(h, w0, "l0_hyb", after_in=rest0_relay)
    layer1_relay()
    rest0_arrived()
    w0.update(_build_layer(0, gw, gs, rep, parts=("ffn2", "ple")))
    h, s0["ffn2"] = _ffn_fwd(h, w0["norm_ffn2"], w0["ffn2_in"], w0["ffn2_out"], "l0_ffn2")
    h, s0["ple"] = _ple_fwd(h, w0["ple_norm"], pb[0], w0["ple_gate"], w0["ple_proj"], "l0_ple")
    layer1_arrived()
    w1 = _build_layer(1, gw, gs, rep)
    h, s1 = _layer_fwd(1, h, w1, pb[1], tables)
    loss, dh, dhb, d_final = _loss_head(h, loss_target[0], final_norm[None], "loss_head")
    loss = lax.psum(loss[0, 0], ("x", "y", "c"))

    dh, dhb, head1 = _layer_bwd_head(1, dh, dhb, s1, w1, pb[1])
    dh, dhb, tail1 = _layer_bwd_tail(1, dh, dhb, s1, w1, tables)
    grads1 = {**head1, **tail1}
    l1_token, l1_middle, l1_finish = reduce_later(keys[1], _big_grads(1, grads1), "grads_l1")
    dh, dhb, grads0 = _layer_bwd_head(0, dh, dhb, s0, w0, pb[0], deps=(l1_token,))
    dh, dhb, grads0["norm_ffn2"], grads0["ffn2_in"], grads0["ffn2_out"] = _ffn_bwd(
        dh, dhb, s0["ffn2"], w0["norm_ffn2"], w0["ffn2_in"], w0["ffn2_out"], "l0_ffn2", deps=(l1_middle(dh),))
    dh, dhb, mixer_grads = _hyb_bwd(dh, dhb, s0["mix"], w0, "l0_hyb")
    grads0.update(mixer_grads)
    l0_token, l0_middle, l0_finish = reduce_later(mixer + rest0, _big_grads(0, grads0), "grads_l0")
    last = {}

    def reduce_first(dw_in, dw_out):
        token, middle, last["finish"] = reduce_later(first, _big_grads(0, dict(ffn1_in=dw_in, ffn1_out=dw_out)), "grads_first")
        middle(token)

    dx, dhb, grads0["norm_ffn1"], grads0["ffn1_in"], grads0["ffn1_out"] = _ffn_bwd(
        dh, dhb, s0["ffn1"], w0["norm_ffn1"], w0["ffn1_in"], w0["ffn1_out"], "l0_ffn1", deps=(l0_token,),
        hook=lambda dpre: (l0_middle(dpre),), weights_hook=reduce_first)
    l1_finish(dx)
    l0_finish(dx)
    last["finish"](dx)
    _, small = _collect_grads([grads0, grads1], d_final)
    small_shapes = [small[n].shape for n in SMALL]
    all_small = _all_gather([_pack([small[n] for n in SMALL])], pltpu.VMEM, "gather_small_grads")[0]
    g = dict(zip(SMALL, _unpack(_sum_slots(all_small, "small_grads_sum"), small_shapes)))
    for n, axis in SMALL_SHARDED.items():
        g[n] = lax.dynamic_slice_in_dim(g[n], me * w[n].shape[axis], w[n].shape[axis], axis=axis)

    delta, new_m, new_v = {}, {}, {}
    for n in BIG:
        outs = _adamw_summed(view(w[n], n), view(m[n], n), view(v[n], n), [by_chip[n, i] for i in range(w[n].shape[0])],
                             f"adamw_{n}")
        g[n], delta[n], new_m[n], new_v[n] = (view(o, n) for o in outs)
    packed = [_pack([d[n] for n in SMALL]) for d in (w, g, m, v)]
    shapes = [w[n].shape for n in SMALL]
    for d, buf in zip((delta, new_m, new_v), _adamw(*packed, "adamw_small")):
        d.update(zip(SMALL, _unpack(buf, shapes)))
    return (loss, dx[None], *[g[n] for n in WEIGHTS], *[delta[n] for n in WEIGHTS], *[new_m[n] for n in WEIGHTS],
            *[new_v[n] for n in WEIGHTS])
```

```python
import functools
import math

import numpy as np
import jax
import jax.numpy as jnp
from jax import lax
from jax.experimental import pallas as pl
from jax.experimental.pallas import tpu as pltpu

F32, BF16 = jnp.float32, jnp.bfloat16
HI = lax.Precision.HIGHEST
SDS = jax.ShapeDtypeStruct

N_DEV = 8
D = 1024
D_FF = 2816
FF_SHARD = 2 * D_FF // N_DEV
PLE_DIM = 256
EPS = 1e-6
CONV_W = 31
SSM_CONV = 4
SSM_HEADS = 16
SSM_XBC = 1536
CHUNK = 128
HYB_IN = 4624
HYB_PAD = 5120
DT_COL = 4608
N_PAIR = 8
ROPE_THETA = 10000.0
LANE = 128
VMEM_LIMIT = 56 * 1024 * 1024

ADAM_LR, ADAM_B1, ADAM_B2, ADAM_EPS, ADAM_WD, ADAM_STEP = 0.001, 0.9, 0.999, 1e-08, 0.01, 10


def _params(sem):
    return pltpu.CompilerParams(dimension_semantics=sem, vmem_limit_bytes=VMEM_LIMIT)


_CHAIN = []


def _restart_chain():
    _CHAIN.clear()


def _pallas(body, *, in_specs, **kw):
    def run(*args):
        n, dep = len(args), list(_CHAIN)

        def chained(*refs):
            return body(*refs[:n], *refs[n + len(dep):])

        outs = pl.pallas_call(chained, in_specs=list(in_specs) + [pl.BlockSpec(memory_space=pl.ANY)] * len(dep), **kw)(*args, *dep)
        _CHAIN[:] = [outs[-1] if isinstance(outs, (list, tuple)) else outs]
        return outs

    return run


def _mm(a, b, *, ta=False, tb=False, reduce_j=False, out_dtypes=(F32,), tm=1024, tn=1024, tk=1024,
        epi=None, extras=(), rows=(), deps=(), sums=0, name):
    ja, jb = a.shape[0], b.shape[0]
    nj = max(ja, jb)
    jo = 1 if reduce_j else nj
    m, k = (a.shape[2], a.shape[1]) if ta else (a.shape[1], a.shape[2])
    n = b.shape[1] if tb else b.shape[2]
    assert (b.shape[2] if tb else b.shape[1]) == k and ja in (1, nj) and jb in (1, nj)
    tm, tn, tk = min(tm, m), min(tn, n), min(tk, k)
    assert m % tm == 0 and n % tn == 0 and k % tk == 0, (name, m, n, k, tm, tn, tk)
    assert not sums or (tn == n and (reduce_j or nj == 1))
    nk = k // tk
    steps = nk * (nj if reduce_j else 1)
    ne, nr, no = len(extras), len(rows), len(out_dtypes)

    def a_map(i, c, j, kk):
        return (j if ja > 1 else 0, kk, i) if ta else (j if ja > 1 else 0, i, kk)

    def b_map(i, c, j, kk):
        return (j if jb > 1 else 0, c, kk) if tb else (j if jb > 1 else 0, kk, c)

    def o_map(i, c, j, kk):
        return (0 if reduce_j else j, i, c)

    dims = (((0 if ta else 1,), (1 if tb else 0,)), ((), ()))

    def body(a_ref, b_ref, *rest):
        ex, rw = rest[:ne], rest[ne:ne + nr]
        outs = rest[ne + nr + len(deps):ne + nr + len(deps) + no]
        sum_refs = rest[ne + nr + len(deps) + no:ne + nr + len(deps) + no + sums]
        first_tile = pl.program_id(0) == 0

        def product():
            return lax.dot_general(a_ref[...], b_ref[...], dims, preferred_element_type=F32)

        def finish(acc):
            res = epi(acc, *[e[...] for e in ex], *[r[...] for r in rw]) if epi else (acc,)
            for o, r in zip(outs, res):
                o[...] = r.astype(o.dtype)
            for s_ref, r in zip(sum_refs, res[no:]):
                @pl.when(first_tile)
                def _(s_ref=s_ref, r=r):
                    s_ref[...] = r

                @pl.when(jnp.logical_not(first_tile))
                def _(s_ref=s_ref, r=r):
                    s_ref[...] += r

        if steps == 1:
            finish(product())
            return
        acc_ref = rest[-1]
        kk = pl.program_id(3)
        step = pl.program_id(2) * nk + kk if reduce_j else kk

        @pl.when(step == 0)
        def _():
            acc_ref[...] = product()

        @pl.when(jnp.logical_and(step > 0, step < steps - 1))
        def _():
            acc_ref[...] += product()

        @pl.when(step == steps - 1)
        def _():
            finish(acc_ref[...] + product())

    o_spec = pl.BlockSpec((None, tm, tn), o_map)
    row_spec = pl.BlockSpec((1, tn), lambda i, c, j, kk: (0, c))
    return _pallas(
        body, name=name, grid=(m // tm, n // tn, nj, nk),
        in_specs=[pl.BlockSpec((None, tk, tm) if ta else (None, tm, tk), a_map),
                  pl.BlockSpec((None, tn, tk) if tb else (None, tk, tn), b_map)]
        + [o_spec] * ne + [row_spec] * nr + [ANY_SPEC] * len(deps),
        out_specs=[o_spec] * no + [row_spec] * sums,
        out_shape=[SDS((jo, m, n), dt) for dt in out_dtypes] + [SDS((1, n), F32)] * sums,
        scratch_shapes=[pltpu.VMEM((tm, tn), F32)] if steps > 1 else [],
        compiler_params=_params(("arbitrary" if sums else "parallel", "parallel", "arbitrary", "arbitrary")),
    )(a, b, *extras, *rows, *deps)


def _whole(p):
    return pl.BlockSpec(p.shape, lambda *_: (0,) * p.ndim)


ANY_SPEC = pl.BlockSpec(memory_space=pl.ANY)


def _rowop(fn, tiles, params, outs, *, grid, name, deps=()):
    nin = len(tiles) + len(params)

    def body(*refs):
        res = fn(*[r[...].astype(F32) for r in refs[:nin]])
        for r, o in zip(refs[nin + len(deps):], res):
            r[...] = o.astype(r.dtype)

    return _pallas(
        body, name=name, grid=grid,
        in_specs=[s for _, s in tiles] + [_whole(p) for p in params] + [ANY_SPEC] * len(deps),
        out_specs=[s for _, _, s in outs], out_shape=[SDS(sh, dt) for sh, dt, _ in outs],
        compiler_params=_params(("parallel",) * len(grid)),
    )(*[t for t, _ in tiles], *params, *deps)


def _rowop_bwd(fn, tiles, params, cots, wrt, gouts, *, grid, name, adds=(), deps=()):
    nt, npar, nc, na = len(tiles), len(params), len(cots), len(adds)
    nin = nt + npar
    flat = [i for grp in wrt for i in grp]
    n_gout = sum(len(dts) for _, dts, _ in gouts)

    def body(*refs):
        vals = [r[...].astype(F32) for r in refs[:nin]]
        cvals = [r[...].astype(F32) for r in refs[nin:nin + nc]]
        avals = [r[...].astype(F32) for r in refs[nin + nc:nin + nc + na]]
        orefs = refs[nin + nc + na + len(deps):]
        diff_idx = flat + list(range(nt, nin))

        def f(*dv):
            full = list(vals)
            for i, v in zip(diff_idx, dv):
                full[i] = v
            return fn(*full)

        _, vjp = jax.vjp(f, *[vals[i] for i in diff_idx])
        grads = vjp(tuple(cvals))
        tile_g, par_g = list(grads[:len(flat)]), grads[len(flat):]
        group_g, at = [], 0
        for grp in wrt:
            members = tile_g[at:at + len(grp)]
            at += len(grp)
            group_g.append(members[0] if len(grp) == 1 else jnp.stack(members, axis=0))
        for av in avals:
            group_g[0] = group_g[0] + av
        o = 0
        for g, (_, dts, _) in zip(group_g, gouts):
            for _ in dts:
                orefs[o][...] = g.astype(orefs[o].dtype)
                o += 1
        first = functools.reduce(jnp.logical_and, [pl.program_id(ax) == 0 for ax in range(len(grid))])
        for r, g in zip(orefs[n_gout:], par_g):
            @pl.when(first)
            def _(r=r, g=g):
                r[...] = g

            @pl.when(jnp.logical_not(first))
            def _(r=r, g=g):
                r[...] += g

    out_specs, out_shape = [], []
    for sh, dts, spec in gouts:
        for dt in dts:
            out_specs.append(spec)
            out_shape.append(SDS(sh, dt))
    for p in params:
        out_specs.append(_whole(p))
        out_shape.append(SDS(p.shape, F32))
    return _pallas(
        body, name=name, grid=grid,
        in_specs=[s for _, s in tiles] + [_whole(p) for p in params] + [s for _, s in cots] + [s for _, s in adds]
        + [ANY_SPEC] * len(deps),
        out_specs=out_specs, out_shape=out_shape,
        compiler_params=_params(("arbitrary",) * len(grid)),
    )(*[t for t, _ in tiles], *params, *[c for c, _ in cots], *[a for a, _ in adds], *deps)


def _tok(c, tm, col=0):
    return pl.BlockSpec((tm, c), lambda i, col=col: (i, col))


def _rms_fn(h, g):
    return (h * lax.rsqrt(jnp.mean(h * h, axis=-1, keepdims=True) + EPS) * g,)


def _lnswish_fn(u, g, b):
    mu = jnp.mean(u, axis=-1, keepdims=True)
    xc = u - mu
    y = xc * lax.rsqrt(jnp.mean(xc * xc, axis=-1, keepdims=True) + EPS) * g + b
    return (y * jax.nn.sigmoid(y),)


def _ple_fn(z, e):
    return (jax.nn.sigmoid(z) * e,)


def _rms(h, g, name, tm=512, deps=()):
    t = h.shape[0]
    return _rowop(_rms_fn, [(h, _tok(D, tm))], [g], [((t, D), BF16, _tok(D, tm))], grid=(t // tm,), name=name, deps=deps)[0]


def _drms_epi(dn, h, dres, g):
    _, vjp = jax.vjp(_rms_fn, h, g)
    dh, dg = vjp((dn,))
    dh = dh + dres
    return dh, dh, dg


def _mm_drms(a, b, h, g, dres, name, tk, tb=True):
    dh, dhb, dg = _mm(a, b, tb=tb, reduce_j=a.shape[0] > 1, tm=1024, tk=tk, epi=_drms_epi, extras=(h[None], dres[None]),
                      rows=(g,), out_dtypes=(F32, BF16), sums=1, name=name)
    return dh[0], dhb[0], dg


def _conv_geometry(width):
    pad = 32 if width > 8 else 8
    return pad, pad - (width - 1)


def _fill_shifts(xpad_ref, sh_ref, t, shifts):
    for r in shifts:
        sh_ref[r, :, :] = xpad_ref[pl.ds(r, t + 32), :]


def _dwconv(xs, w, b, *, width, glu, silu, cb, name):
    t = xs[0][0].shape[0]
    c = w.shape[1]
    pad, off = _conv_geometry(width)
    shifts = sorted({(k + off) % 8 for k in range(width)})
    ch = 64

    def body(*refs):
        x_refs, (w_ref, b_ref, o_ref, xpad_ref, sh_ref) = refs[:len(xs)], refs[len(xs):]
        u = x_refs[0][...] * jax.nn.sigmoid(x_refs[1][...]) if glu else x_refs[0][...]
        xpad_ref[pl.ds(0, pad), :] = jnp.zeros((pad, cb), F32)
        xpad_ref[pl.ds(pad, t), :] = u
        xpad_ref[pl.ds(pad + t, 40 - pad), :] = jnp.zeros((40 - pad, cb), F32)
        _fill_shifts(xpad_ref, sh_ref, t, shifts)

        def chunk(i, carry):
            t0 = pl.multiple_of(i * ch, ch)
            acc = jnp.broadcast_to(b_ref[...], (ch, cb))
            for k in range(width):
                q, r = divmod(k + off, 8)
                acc = acc + w_ref[pl.ds(k, 1), :] * sh_ref[r, pl.ds(t0 + 8 * q, ch), :]
            o_ref[pl.ds(t0, ch), :] = acc * jax.nn.sigmoid(acc) if silu else acc
            return carry

        lax.fori_loop(0, t // ch, chunk, 0)

    return _pallas(
        body, name=name, grid=(c // cb,),
        in_specs=[pl.BlockSpec((t, cb), lambda i, o=o: (0, o + i)) for _, o in xs]
        + [pl.BlockSpec((width, cb), lambda i: (0, i)), pl.BlockSpec((1, cb), lambda i: (0, i))],
        out_specs=pl.BlockSpec((t, cb), lambda i: (0, i)), out_shape=SDS((t, c), F32),
        scratch_shapes=[pltpu.VMEM((t + 40, cb), F32), pltpu.VMEM((8, t + 32, cb), F32)],
        compiler_params=_params(("parallel",)),
    )(*[x for x, _ in xs], w, b)


def _dwconv_bwd(xs, w, b, dy, *, width, glu, silu, cb, name):
    t = xs[0][0].shape[0]
    c = w.shape[1]
    pad, off = _conv_geometry(width)
    shifts = sorted({(k + off) % 8 for k in range(width)})
    shifts_t = sorted({mm % 8 for mm in range(width)})
    ch = 64
    nx = len(xs)

    def body(*refs):
        x_refs = refs[:nx]
        w_ref, b_ref, dy_ref = refs[nx:nx + 3]
        dx_refs = refs[nx + 3:nx + 3 + nx]
        dw_ref, db_ref, xpad_ref, sh_ref, dc_ref = refs[nx + 3 + nx:]
        u = x_refs[0][...] * jax.nn.sigmoid(x_refs[1][...]) if glu else x_refs[0][...]
        xpad_ref[pl.ds(0, pad), :] = jnp.zeros((pad, cb), F32)
        xpad_ref[pl.ds(pad, t), :] = u
        xpad_ref[pl.ds(pad + t, 40 - pad), :] = jnp.zeros((40 - pad, cb), F32)
        _fill_shifts(xpad_ref, sh_ref, t, shifts)

        if silu:
            def act_chunk(i, carry):
                t0 = pl.multiple_of(i * ch, ch)
                acc = jnp.broadcast_to(b_ref[...], (ch, cb))
                for k in range(width):
                    q, r = divmod(k + off, 8)
                    acc = acc + w_ref[pl.ds(k, 1), :] * sh_ref[r, pl.ds(t0 + 8 * q, ch), :]
                sg = jax.nn.sigmoid(acc)
                dc_ref[pl.ds(t0, ch), :] = dy_ref[pl.ds(t0, ch), :] * (sg * (1.0 + acc * (1.0 - sg)))
                return carry

            lax.fori_loop(0, t // ch, act_chunk, 0)
        else:
            dc_ref[...] = dy_ref[...]

        def dw_chunk(i, accs):
            t0 = pl.multiple_of(i * ch, ch)
            new = list(accs)
            for s in range(ch // 8):
                d = dc_ref[pl.ds(t0 + 8 * s, 8), :]
                for k in range(width):
                    q, r = divmod(k + off, 8)
                    new[k] = new[k] + d * sh_ref[r, pl.ds(t0 + 8 * (q + s), 8), :]
                new[width] = new[width] + d
            return tuple(new)

        accs = lax.fori_loop(0, t // ch, dw_chunk, tuple(jnp.zeros((8, cb), F32) for _ in range(width + 1)))
        for k in range(width):
            dw_ref[pl.ds(k, 1), :] = jnp.sum(accs[k], axis=0, keepdims=True)
        db_ref[...] = jnp.sum(accs[width], axis=0, keepdims=True)

        xpad_ref[pl.ds(0, t), :] = dc_ref[...]
        xpad_ref[pl.ds(t, 40), :] = jnp.zeros((40, cb), F32)
        _fill_shifts(xpad_ref, sh_ref, t, shifts_t)

        def dx_chunk(i, carry):
            t0 = pl.multiple_of(i * ch, ch)
            acc = jnp.zeros((ch, cb), F32)
            for mm in range(width):
                q, r = divmod(mm, 8)
                acc = acc + w_ref[pl.ds(width - 1 - mm, 1), :] * sh_ref[r, pl.ds(t0 + 8 * q, ch), :]
            if glu:
                val, gate = x_refs[0][pl.ds(t0, ch), :], x_refs[1][pl.ds(t0, ch), :]
                sg = jax.nn.sigmoid(gate)
                dx_refs[0][pl.ds(t0, ch), :] = (acc * sg).astype(BF16)
                dx_refs[1][pl.ds(t0, ch), :] = (acc * val * sg * (1.0 - sg)).astype(BF16)
            else:
                dx_refs[0][pl.ds(t0, ch), :] = acc.astype(BF16)
            return carry

        lax.fori_loop(0, t // ch, dx_chunk, 0)

    col = pl.BlockSpec((t, cb), lambda i: (0, i))
    return _pallas(
        body, name=name, grid=(c // cb,),
        in_specs=[pl.BlockSpec((t, cb), lambda i, o=o: (0, o + i)) for _, o in xs]
        + [pl.BlockSpec((width, cb), lambda i: (0, i)), pl.BlockSpec((1, cb), lambda i: (0, i)), col],
        out_specs=[col] * nx + [pl.BlockSpec((width, cb), lambda i: (0, i)), pl.BlockSpec((1, cb), lambda i: (0, i))],
        out_shape=[SDS((t, c), BF16)] * nx + [SDS((width, c), F32), SDS((1, c), F32)],
        scratch_shapes=[pltpu.VMEM((t + 40, cb), F32), pltpu.VMEM((8, t + 32, cb), F32), pltpu.VMEM((t, cb), F32)],
        compiler_params=_params(("parallel",)),
    )(*[x for x, _ in xs], w, b, dy)


_DIMS = {"nn": (((1,), (0,)), ((), ())), "nt": (((1,), (1,)), ((), ())), "tn": (((0,), (0,)), ((), ()))}


def _raw_dot(a, b, mode):
    return lax.dot_general(a.astype(BF16), b.astype(BF16), _DIMS[mode], preferred_element_type=F32)


@functools.partial(jax.custom_vjp, nondiff_argnums=(2,))
def _bdot(a, b, mode):
    return _raw_dot(a, b, mode)


def _bdot_fwd(a, b, mode):
    return _raw_dot(a, b, mode), (a, b)


def _bdot_bwd(mode, res, g):
    a, b = res
    if mode == "nn":
        return _raw_dot(g, b, "nt"), _raw_dot(a, g, "tn")
    if mode == "nt":
        return _raw_dot(g, b, "nn"), _raw_dot(g, a, "tn")
    return _raw_dot(b, g, "nt"), _raw_dot(a, g, "nn")


_bdot.defvjp(_bdot_fwd, _bdot_bwd)


def _iota(shape, axis):
    return lax.broadcasted_iota(jnp.int32, shape, axis)


def _half_masks():
    left = (_iota((1, LANE), 1) < 64).astype(F32)
    return left, 1.0 - left


def _split3(a):
    a1 = a.astype(BF16)
    r1 = a - a1.astype(F32)
    a2 = r1.astype(BF16)
    return a1, a2, (r1 - a2.astype(F32)).astype(BF16)


def _exact_dot(a, e, mode):
    return sum(lax.dot_general(piece, e, _DIMS[mode], preferred_element_type=F32) for piece in _split3(a))


@jax.custom_vjp
def _spread(a, e):
    return _exact_dot(a, e, "nn")


_spread.defvjp(lambda a, e: (_exact_dot(a, e, "nn"), e), lambda e, g: (_exact_dot(g, e, "nt"), jnp.zeros_like(e)))


@jax.custom_vjp
def _running_sum(tri, a):
    return sum(lax.dot_general(tri, piece, _DIMS["nn"], preferred_element_type=F32) for piece in _split3(a))


_running_sum.defvjp(
    lambda tri, a: (sum(lax.dot_general(tri, piece, _DIMS["nn"], preferred_element_type=F32) for piece in _split3(a)), tri),
    lambda tri, g: (jnp.zeros_like(tri), sum(lax.dot_general(tri, piece, _DIMS["tn"], preferred_element_type=F32)
                                             for piece in _split3(g))))


def _ssd_chunk(state, xa, dtr, z, dtb, alog, dskf, ng):
    xs, bm, cm = xa[:, :D], xa[:, D:D + 256], xa[:, D + 256:]
    left, right = _half_masks()
    expand = (_iota((LANE, D), 1) // 64 == _iota((LANE, D), 0)).astype(BF16)
    li, si = _iota((CHUNK, CHUNK), 0), _iota((CHUNK, CHUNK), 1)
    tril = li >= si
    dt16 = jax.nn.softplus(dtr + dtb)
    adt = dt16 * (-jnp.exp(alog))
    dtf = _spread(dt16, expand)
    cs16 = _running_sum(tril.astype(BF16), adt)
    csf = _spread(cs16, expand)
    totf = jnp.sum(jnp.where(_iota((CHUNK, D), 0) == CHUNK - 1, csf, 0.0), axis=0, keepdims=True)
    cst = cs16.T
    xdt = xs * dtf
    ys, new_state = [], []
    for g in range(2):
        bg, cg = bm[:, LANE * g:LANE * (g + 1)], cm[:, LANE * g:LANE * (g + 1)]
        cb = _bdot(cg, bg, "nt")
        for q in range(4):
            pr = 4 * g + q
            decay = []
            for h in (2 * pr, 2 * pr + 1):
                col = jnp.sum(jnp.where(si == h, cs16, 0.0), axis=1, keepdims=True)
                row = jnp.sum(jnp.where(li == h, cst, 0.0), axis=0, keepdims=True)
                decay.append(cb * jnp.exp(jnp.where(tril, col - row, -jnp.inf)))
            xp = xdt[:, LANE * pr:LANE * (pr + 1)]
            y_diag = _bdot(jnp.concatenate(decay, axis=1), jnp.concatenate([xp * left, xp * right], axis=0), "nn")
            csb, tot = csf[:, LANE * pr:LANE * (pr + 1)], totf[:, LANE * pr:LANE * (pr + 1)]
            ys.append(y_diag + _bdot(cg, state[pr], "nn") * jnp.exp(csb))
            new_state.append(state[pr] * jnp.exp(tot) + _bdot(bg, xp * jnp.exp(tot - csb), "tn"))
    y = jnp.concatenate(ys, axis=1)
    y = y + dskf * xs
    y = y * (z * jax.nn.sigmoid(z))
    halves = []
    for g in range(2):
        yg = y[:, 512 * g:512 * (g + 1)]
        halves.append(yg * lax.rsqrt(jnp.mean(yg * yg, axis=-1, keepdims=True) + EPS))
    return jnp.concatenate(halves, axis=1) * ng, jnp.stack(new_state, axis=0)


def _ssd_specs(t, rev):
    nc = t // CHUNK
    ix = (lambda c: nc - 1 - c) if rev else (lambda c: c)
    return nc, ix


def _ssd_fwd(xa, proj, dtb, alog, dsk, ng, name):
    t = xa.shape[0]
    nc, ix = _ssd_specs(t, False)

    def body(xa_ref, dt_ref, z_ref, dtb_ref, alog_ref, dsk_ref, ng_ref, y_ref, st_ref, carry_ref):
        @pl.when(pl.program_id(0) == 0)
        def _():
            carry_ref[...] = jnp.zeros_like(carry_ref)

        st_ref[...] = carry_ref[...]
        y, new = _ssd_chunk(carry_ref[...], xa_ref[...], dt_ref[...], z_ref[...], dtb_ref[...], alog_ref[...],
                            dsk_ref[...], ng_ref[...])
        y_ref[...] = y.astype(BF16)
        carry_ref[...] = new

    small = [dtb, alog, dsk, ng]
    return _pallas(
        body, name=name, grid=(nc,),
        in_specs=[pl.BlockSpec((CHUNK, SSM_XBC), lambda c: (c, 0)),
                  pl.BlockSpec((CHUNK, LANE), lambda c: (c, DT_COL // LANE)),
                  pl.BlockSpec((CHUNK, D), lambda c: (c, 2))] + [_whole(p) for p in small],
        out_specs=[pl.BlockSpec((CHUNK, D), lambda c: (c, 0)), pl.BlockSpec((None, N_PAIR, LANE, LANE), lambda c: (c, 0, 0, 0))],
        out_shape=[SDS((t, D), BF16), SDS((nc, N_PAIR, LANE, LANE), F32)],
        scratch_shapes=[pltpu.VMEM((N_PAIR, LANE, LANE), F32)],
        compiler_params=_params(("arbitrary",)),
    )(xa, proj, proj, *small)


def _ssd_bwd(xa, proj, states, dy, dtb, alog, dsk, ng, name):
    t = xa.shape[0]
    nc, ix = _ssd_specs(t, True)

    def body(xa_ref, dt_ref, z_ref, st_ref, dy_ref, dtb_ref, alog_ref, dsk_ref, ng_ref,
             dxa_ref, ddt_ref, dz_ref, gdtb_ref, galog_ref, gdsk_ref, gng_ref, carry_ref):
        first = pl.program_id(0) == 0

        @pl.when(first)
        def _():
            carry_ref[...] = jnp.zeros_like(carry_ref)

        args = (st_ref[...], xa_ref[...], dt_ref[...], z_ref[...], dtb_ref[...], alog_ref[...], dsk_ref[...], ng_ref[...])
        _, vjp = jax.vjp(_ssd_chunk, *args)
        ds, dxa, ddt, dz, gdtb, galog, gdsk, gng = vjp((dy_ref[...], carry_ref[...]))
        carry_ref[...] = ds
        dxa_ref[...] = dxa
        ddt_ref[...] = ddt.astype(BF16)
        dz_ref[...] = dz.astype(BF16)
        for r, g in ((gdtb_ref, gdtb), (galog_ref, galog), (gdsk_ref, gdsk), (gng_ref, gng)):
            @pl.when(first)
            def _(r=r, g=g):
                r[...] = g

            @pl.when(jnp.logical_not(first))
            def _(r=r, g=g):
                r[...] += g

    small = [dtb, alog, dsk, ng]
    return _pallas(
        body, name=name, grid=(nc,),
        in_specs=[pl.BlockSpec((CHUNK, SSM_XBC), lambda c: (ix(c), 0)),
                  pl.BlockSpec((CHUNK, LANE), lambda c: (ix(c), DT_COL // LANE)),
                  pl.BlockSpec((CHUNK, D), lambda c: (ix(c), 2)),
                  pl.BlockSpec((None, N_PAIR, LANE, LANE), lambda c: (ix(c), 0, 0, 0)),
                  pl.BlockSpec((CHUNK, D), lambda c: (ix(c), 0))] + [_whole(p) for p in small],
        out_specs=[pl.BlockSpec((CHUNK, SSM_XBC), lambda c: (ix(c), 0)), pl.BlockSpec((CHUNK, LANE), lambda c: (ix(c), 0)),
                   pl.BlockSpec((CHUNK, D), lambda c: (ix(c), 0))] + [_whole(p) for p in small],
        out_shape=[SDS((t, SSM_XBC), F32), SDS((t, LANE), BF16), SDS((t, D), BF16)] + [SDS(p.shape, F32) for p in small],
        scratch_shapes=[pltpu.VMEM((N_PAIR, LANE, LANE), F32)],
        compiler_params=_params(("arbitrary",)),
    )(xa, proj, proj, states, dy, *small)


def _attn_block(q, kv_prev, kv_cur, cq, sq, ck, sk, sinks, rot, first_block):
    left, right = _half_masks()
    k2 = jnp.concatenate([kv_prev[:, :256], kv_cur[:, :256]], axis=0)
    v2 = jnp.concatenate([kv_prev[:, 256:], kv_cur[:, 256:]], axis=0)
    ri, ci = _iota((LANE, LANE), 0), _iota((LANE, LANE), 1)
    dup = [((ri < 64) & (ci % 64 == ri)).astype(BF16), ((ri >= 64) & (ci % 64 == ri - 64)).astype(BF16)]

    rot16 = rot.astype(BF16)

    def rope(tt, c, s):
        return tt * c + _spread(tt, rot16) * s

    kd, vd = [], []
    for j in range(4):
        sl = slice(LANE * (j // 2), LANE * (j // 2 + 1))
        kd.append(_bdot(rope(k2[:, sl], ck, sk), dup[j % 2], "nn"))
        vd.append(_bdot(v2[:, sl], dup[j % 2], "nn"))
    qi, si = _iota((2 * CHUNK, 2 * CHUNK), 0) % CHUNK, _iota((2 * CHUNK, 2 * CHUNK), 1)
    valid = (si > qi) & (si <= qi + CHUNK) & jnp.logical_or(si >= CHUNK, jnp.logical_not(first_block))
    upper = _iota((2 * CHUNK, 1), 0) < CHUNK
    lanes = _iota((1, LANE), 1)
    outs = []
    for pr in range(N_PAIR):
        qr = rope(q[:, LANE * pr:LANE * (pr + 1)], cq, sq)
        lg = _bdot(jnp.concatenate([qr * left, qr * right], axis=0), kd[pr // 2], "nt") * 0.125
        lg = jnp.where(valid, lg, -jnp.inf)
        s1 = jnp.sum(jnp.where(lanes == 2 * pr, sinks, 0.0), axis=1, keepdims=True)
        s2 = jnp.sum(jnp.where(lanes == 2 * pr + 1, sinks, 0.0), axis=1, keepdims=True)
        sink = jnp.where(upper, s1, s2)
        mx = lax.stop_gradient(jnp.maximum(jnp.max(lg, axis=-1, keepdims=True), sink))
        e = jnp.exp(lg - mx)
        probs = e / (jnp.sum(e, axis=-1, keepdims=True) + jnp.exp(sink - mx))
        o2 = _bdot(probs, vd[pr // 2], "nn")
        outs.append(o2[:CHUNK] * left + o2[CHUNK:] * right)
    return jnp.concatenate(outs, axis=1)


def _attn_fwd(qkv, cos, sin, sinks, rot, name):
    t = qkv.shape[0]
    nb = t // CHUNK

    def body(q_ref, kvp_ref, kvc_ref, cq_ref, sq_ref, cp_ref, sp_ref, sinks_ref, rot_ref, o_ref):
        ck = jnp.concatenate([cp_ref[...], cq_ref[...]], axis=0)
        sk = jnp.concatenate([sp_ref[...], sq_ref[...]], axis=0)
        o_ref[...] = _attn_block(q_ref[...], kvp_ref[...], kvc_ref[...], cq_ref[...], sq_ref[...], ck, sk,
                                 sinks_ref[...], rot_ref[...], pl.program_id(0) == 0).astype(BF16)

    prev = lambda n: jnp.maximum(n - 1, 0)
    return _pallas(
        body, name=name, grid=(nb,),
        in_specs=[pl.BlockSpec((CHUNK, D), lambda n: (n, 0)),
                  pl.BlockSpec((CHUNK, 512), lambda n: (prev(n), 2)), pl.BlockSpec((CHUNK, 512), lambda n: (n, 2)),
                  pl.BlockSpec((CHUNK, LANE), lambda n: (n, 0)), pl.BlockSpec((CHUNK, LANE), lambda n: (n, 0)),
                  pl.BlockSpec((CHUNK, LANE), lambda n: (prev(n), 0)), pl.BlockSpec((CHUNK, LANE), lambda n: (prev(n), 0)),
                  _whole(sinks), _whole(rot)],
        out_specs=pl.BlockSpec((CHUNK, D), lambda n: (n, 0)), out_shape=SDS((t, D), BF16),
        compiler_params=_params(("parallel",)),
    )(qkv, qkv, qkv, cos, sin, cos, sin, sinks, rot)


def _attn_bwd(qkv, do, cos, sin, sinks, rot, name):
    t = qkv.shape[0]
    nb = t // CHUNK

    def body(q_ref, kvp_ref, kvc_ref, do_ref, cq_ref, sq_ref, cp_ref, sp_ref, sinks_ref, rot_ref,
             dq_ref, dkv_ref, dbq_ref, dbkv_ref, dsink_ref, carry_ref):
        n = pl.program_id(0)

        @pl.when(n == 0)
        def _():
            carry_ref[...] = jnp.zeros_like(carry_ref)
            dbq_ref[...] = jnp.zeros_like(dbq_ref)
            dbkv_ref[...] = jnp.zeros_like(dbkv_ref)
            dsink_ref[...] = jnp.zeros_like(dsink_ref)

        @pl.when(n < nb)
        def _():
            ck = jnp.concatenate([cp_ref[...], cq_ref[...]], axis=0)
            sk = jnp.concatenate([sp_ref[...], sq_ref[...]], axis=0)
            f = lambda q, kvp, kvc, s: _attn_block(q, kvp, kvc, cq_ref[...], sq_ref[...], ck, sk, s, rot_ref[...], n == 0)
            _, vjp = jax.vjp(f, q_ref[...], kvp_ref[...], kvc_ref[...], sinks_ref[...])
            dq, dkvp, dkvc, ds = vjp(do_ref[...].astype(F32))
            done = carry_ref[...] + dkvp
            dq_ref[...] = dq.astype(BF16)
            dkv_ref[...] = done.astype(BF16)
            dbq_ref[...] += jnp.sum(dq, axis=0, keepdims=True)
            dsink_ref[...] += ds
            carry_ref[...] = dkvc

            @pl.when(n > 0)
            def _():
                dbkv_ref[...] += jnp.sum(done, axis=0, keepdims=True)

        @pl.when(n == nb)
        def _():
            done = carry_ref[...]
            dkv_ref[...] = done.astype(BF16)
            dbkv_ref[...] += jnp.sum(done, axis=0, keepdims=True)

    cur = lambda n: jnp.minimum(n, nb - 1)
    prev = lambda n: jnp.maximum(jnp.minimum(n, nb - 1) - 1, 0)
    fin = lambda n: jnp.maximum(n - 1, 0)
    outs = _pallas(
        body, name=name, grid=(nb + 1,),
        in_specs=[pl.BlockSpec((CHUNK, D), lambda n: (cur(n), 0)),
                  pl.BlockSpec((CHUNK, 512), lambda n: (prev(n), 2)), pl.BlockSpec((CHUNK, 512), lambda n: (cur(n), 2)),
                  pl.BlockSpec((CHUNK, D), lambda n: (cur(n), 0)),
                  pl.BlockSpec((CHUNK, LANE), lambda n: (cur(n), 0)), pl.BlockSpec((CHUNK, LANE), lambda n: (cur(n), 0)),
                  pl.BlockSpec((CHUNK, LANE), lambda n: (prev(n), 0)), pl.BlockSpec((CHUNK, LANE), lambda n: (prev(n), 0)),
                  _whole(sinks), _whole(rot)],
        out_specs=[pl.BlockSpec((CHUNK, D), lambda n: (cur(n), 0)), pl.BlockSpec((CHUNK, 512), lambda n: (fin(n), 0)),
                   pl.BlockSpec((1, D), lambda n: (0, 0)), pl.BlockSpec((1, 512), lambda n: (0, 0)), _whole(sinks)],
        out_shape=[SDS((t, D), BF16), SDS((t, 512), BF16), SDS((1, D), F32), SDS((1, 512), F32), SDS(sinks.shape, F32)],
        scratch_shapes=[pltpu.VMEM((CHUNK, 512), F32)],
        compiler_params=_params(("arbitrary",)),
    )(qkv, qkv, qkv, do, cos, sin, cos, sin, sinks, rot)
    dq, dkv, dbq, dbkv, dsinks = outs
    return jnp.concatenate([dq, dkv], axis=1), jnp.concatenate([dbq, dbkv], axis=1), dsinks


def _loss_head(h, tgt, g, name, tm=512):
    t = h.shape[0]

    def body(h_ref, t_ref, g_ref, loss_ref, dh_ref, dhb_ref, dg_ref):
        def f(hv, gv):
            err = _rms_fn(hv, gv)[0] - t_ref[...]
            return 0.5 * jnp.sum(jnp.mean(err * err, axis=-1, keepdims=True), axis=0, keepdims=True)

        loss, vjp = jax.vjp(f, h_ref[...], g_ref[...])
        dh, dg = vjp(jnp.ones((1, 1), F32))
        dh_ref[...] = dh
        dhb_ref[...] = dh.astype(BF16)
        first = pl.program_id(0) == 0

        @pl.when(first)
        def _():
            loss_ref[...] = loss
            dg_ref[...] = dg

        @pl.when(jnp.logical_not(first))
        def _():
            loss_ref[...] += loss
            dg_ref[...] += dg

    return _pallas(
        body, name=name, grid=(t // tm,),
        in_specs=[_tok(D, tm), _tok(D, tm), _whole(g)],
        out_specs=[pl.BlockSpec((1, 1), lambda i: (0, 0)), _tok(D, tm), _tok(D, tm), _whole(g)],
        out_shape=[SDS((1, 1), F32), SDS((t, D), F32), SDS((t, D), BF16), SDS(g.shape, F32)],
        compiler_params=_params(("arbitrary",)),
    )(h, tgt, g)


def _res_half(acc, res):
    return (res + 0.5 * acc,)


def _res_full(acc, res):
    return (res + acc,)


def _half(acc):
    return (0.5 * acc,)


def _ffn_in(n, w_in, name, tm=1024):
    t = n.shape[0]
    tm = min(tm, t)

    def body(n_ref, w_ref, pre_ref, act_ref):
        a = n_ref[...]
        gate = lax.dot_general(a, w_ref[0], _DIMS["nt"], preferred_element_type=F32)
        up = lax.dot_general(a, w_ref[1], _DIMS["nt"], preferred_element_type=F32)
        pre_ref[0] = gate.astype(BF16)
        pre_ref[1] = up.astype(BF16)
        act_ref[...] = (gate * jax.nn.sigmoid(gate) * up).astype(BF16)

    pair = pl.BlockSpec((2, None, tm, FF_SHARD), lambda i, j: (0, j, i, 0))
    return _pallas(
        body, name=name, grid=(t // tm, 4),
        in_specs=[pl.BlockSpec((tm, D), lambda i, j: (i, 0)), pl.BlockSpec((2, None, FF_SHARD, D), lambda i, j: (0, j, 0, 0))],
        out_specs=[pair, pl.BlockSpec((None, tm, FF_SHARD), lambda i, j: (j, i, 0))],
        out_shape=[SDS((2, 4, t, FF_SHARD), BF16), SDS((4, t, FF_SHARD), BF16)],
        compiler_params=_params(("parallel", "parallel")),
    )(n, w_in.reshape(2, 4, FF_SHARD, D))


def _ffn_dact(dhb, w_out, pre, name, tm=1024, deps=()):
    t = dhb.shape[0]
    tm = min(tm, t)

    def body(d_ref, w_ref, pre_ref, *rest):
        o_ref = rest[-1]
        dact = 0.5 * lax.dot_general(d_ref[...], w_ref[...], _DIMS["nt"], preferred_element_type=F32)
        gate, up = pre_ref[0].astype(F32), pre_ref[1].astype(F32)
        sg = jax.nn.sigmoid(gate)
        o_ref[0] = (dact * up * (sg * (1.0 + gate * (1.0 - sg)))).astype(BF16)
        o_ref[1] = (dact * (gate * sg)).astype(BF16)

    pair = pl.BlockSpec((2, None, tm, FF_SHARD), lambda i, j: (0, j, i, 0))
    return _pallas(
        body, name=name, grid=(t // tm, 4),
        in_specs=[pl.BlockSpec((tm, D), lambda i, j: (i, 0)), pl.BlockSpec((None, FF_SHARD, D), lambda i, j: (j, 0, 0)), pair]
        + [ANY_SPEC] * len(deps),
        out_specs=pair, out_shape=SDS((2, 4, t, FF_SHARD), BF16),
        compiler_params=_params(("parallel", "parallel")),
    )(dhb, w_out, pre, *deps)


def _ffn_fwd(h, g, w_in, w_out, tag, deps=()):
    n = _rms(h, g, f"{tag}_rms", deps=deps)
    pre, act = _ffn_in(n, w_in, f"{tag}_in")
    w_out = w_out() if callable(w_out) else w_out
    out = _mm(act, w_out, reduce_j=True, tk=FF_SHARD, epi=_res_half, extras=(h[None],), name=f"{tag}_out")[0][0]
    return out, (h, n, pre, act)


def _ffn_bwd(dh, dhb, saved, g, w_in, w_out, tag, deps=(), hook=None, weights_hook=None):
    h, n, pre, act = saved
    t = h.shape[0]
    dpre = _ffn_dact(dhb, w_out, pre, f"{tag}_dact", deps=deps).reshape(N_DEV, t, FF_SHARD)
    dw_out = _mm(act, dhb[None], ta=True, tm=FF_SHARD, epi=_half, out_dtypes=(BF16,), deps=hook(dpre) if hook else (),
                 name=f"{tag}_dwout")[0]
    dw_in = _mm(dpre, n[None], ta=True, tm=FF_SHARD, out_dtypes=(BF16,), name=f"{tag}_dwin")[0]
    if weights_hook:
        weights_hook(dw_in, dw_out)
    dh_in, dhb_in, dg = _mm_drms(dpre, w_in, h, g, dh, f"{tag}_dn", FF_SHARD, tb=False)
    return dh_in, dhb_in, dg, dw_in, dw_out


def _ple_fwd(h, g, pb, w_gate, w_proj, tag):
    t = h.shape[0]
    tm = 512
    n = _rms(h, g, f"{tag}_rms")
    e = _mm(pb[None], w_proj[None], name=f"{tag}_proj")[0][0]
    z = _mm(n[None], w_gate[None], name=f"{tag}_gate")[0][0]
    out = _rowop(lambda zz, ee, hh: (hh + _ple_fn(zz, ee)[0],), [(z, _tok(D, tm)), (e, _tok(D, tm)), (h, _tok(D, tm))], [],
                 [((t, D), F32, _tok(D, tm))], grid=(t // tm,), name=f"{tag}_mix")[0]
    return out, (h, n, e, z)


def _ple_bwd(dh, dhb, saved, g, pb, w_gate, tag, deps=()):
    h, n, e, z = saved
    t = h.shape[0]
    tm = 512
    dz, de = _rowop_bwd(_ple_fn, [(z, _tok(D, tm)), (e, _tok(D, tm))], [], [(dh, _tok(D, tm))], [(0,), (1,)],
                        [((t, D), (BF16,), _tok(D, tm)), ((t, D), (BF16,), _tok(D, tm))], grid=(t // tm,), name=f"{tag}_dmix",
                        deps=deps)
    dw_proj = _mm(pb[None], de[None], ta=True, out_dtypes=(BF16,), name=f"{tag}_dwproj")[0][0]
    dw_gate = _mm(n[None], dz[None], ta=True, out_dtypes=(BF16,), name=f"{tag}_dwgate")[0][0]
    dh_in, dhb_in, dg = _mm_drms(dz[None], w_gate[None], h, g, dh, f"{tag}_dn", 1024)
    return dh_in, dhb_in, dg, dw_gate, dw_proj


def _hyb_fwd(h, w, tag, after_in=None):
    t = h.shape[0]
    tm = 512
    hn = _rms(h, w["norm_mix"], f"{tag}_rms")
    proj = _mm(hn[None], w["hyb_in"][None], tn=512, name=f"{tag}_in")[0][0]
    if after_in:
        after_in()
    u1 = _dwconv([(proj, 0), (proj, D // LANE)], w["conv_w"], w["conv_b"], width=CONV_W, glu=True, silu=False, cb=LANE,
                 name=f"{tag}_conv")
    u = _rowop(_lnswish_fn, [(u1, _tok(D, tm))], [w["ln_g"], w["ln_b"]], [((t, D), BF16, _tok(D, tm))], grid=(t // tm,),
               name=f"{tag}_ln")[0]
    xa = _dwconv([(proj, 3 * D // LANE)], w["sconv_w"], w["sconv_b"], width=SSM_CONV, glu=False, silu=True, cb=LANE,
                 name=f"{tag}_sconv")
    y, states = _ssd_fwd(xa, proj, w["dt_bias"], w["a_log"], w["d_skip"], w["ssm_norm"], f"{tag}_ssd")
    mixed = jnp.stack([u, y], axis=0)
    out = _mm(mixed, w["hyb_out"], reduce_j=True, epi=_res_full, extras=(h[None],), name=f"{tag}_out")[0][0]
    return out, (h, hn, proj, u1, xa, states, mixed)


def _hyb_bwd(dh, dhb, saved, w, tag):
    h, hn, proj, u1, xa, states, mixed = saved
    t = h.shape[0]
    tm = 512
    dmix = _mm(dhb[None], w["hyb_out"], tb=True, name=f"{tag}_dmix")[0]
    dw_out = _mm(mixed, dhb[None], ta=True, out_dtypes=(BF16,), name=f"{tag}_dwout")[0]
    du1, dln_g, dln_b = _rowop_bwd(_lnswish_fn, [(u1, _tok(D, tm))], [w["ln_g"], w["ln_b"]], [(dmix[0], _tok(D, tm))], [(0,)],
                                   [((t, D), (F32,), _tok(D, tm))], grid=(t // tm,), name=f"{tag}_dln")
    dval, dgate, dconv_w, dconv_b = _dwconv_bwd([(proj, 0), (proj, D // LANE)], w["conv_w"], w["conv_b"], du1,
                                                width=CONV_W, glu=True, silu=False, cb=LANE, name=f"{tag}_dconv")
    dxa, ddt, dz, g_dtb, g_alog, g_dsk, g_ng = _ssd_bwd(xa, proj, states, dmix[1], w["dt_bias"], w["a_log"], w["d_skip"],
                                                         w["ssm_norm"], f"{tag}_dssd")
    dxbc, dsconv_w, dsconv_b = _dwconv_bwd([(proj, 3 * D // LANE)], w["sconv_w"], w["sconv_b"], dxa, width=SSM_CONV,
                                           glu=False, silu=True, cb=LANE, name=f"{tag}_dsconv")
    dproj = jnp.concatenate([dval, dgate, dz, dxbc, ddt, jnp.zeros((t, HYB_PAD - DT_COL - LANE), BF16)], axis=1)
    dh_in, dhb_in, dg = _mm_drms(dproj[None], w["hyb_in"][None], h, w["norm_mix"], dh, f"{tag}_dhn", 1024)
    dw_in = _mm(hn[None], dproj[None], ta=True, tn=512, out_dtypes=(BF16,), name=f"{tag}_dwin")[0][0]
    grads = dict(norm_mix=dg, hyb_in=dw_in, hyb_out=dw_out, conv_w=dconv_w, conv_b=dconv_b, ln_g=dln_g, ln_b=dln_b,
                 sconv_w=dsconv_w, sconv_b=dsconv_b, dt_bias=g_dtb, a_log=g_alog, d_skip=g_dsk, ssm_norm=g_ng)
    return dh_in, dhb_in, grads


def _bias_epi(acc, row):
    return (acc + row,)


def _res_bias_epi(acc, res, row):
    return (res + acc + row,)


def _att_fwd(h, w, tables, tag):
    cos, sin, rot = tables
    hn = _rms(h, w["norm_mix"], f"{tag}_rms")
    qkv = _mm(hn[None], w["qkv"][None], tb=True, tn=512, epi=_bias_epi, rows=(w["b_qkv"],), name=f"{tag}_qkv")[0][0]
    o = _attn_fwd(qkv, cos, sin, w["sinks"], rot, f"{tag}_core")
    out = _mm(o[None], w["w_o"][None], epi=_res_bias_epi, extras=(h[None],), rows=(w["b_o"],), name=f"{tag}_out")[0][0]
    return out, (h, hn, qkv, o)


def _att_bwd(dh, dhb, saved, w, tables, tag):
    cos, sin, rot = tables
    h, hn, qkv, o = saved
    t = h.shape[0]
    tm = 512
    do = _mm(dhb[None], w["w_o"][None], tb=True, out_dtypes=(BF16,), name=f"{tag}_do")[0][0]
    dw_o = _mm(o[None], dhb[None], ta=True, out_dtypes=(BF16,), name=f"{tag}_dwo")[0][0]
    db_o = _rowop_bwd(lambda xx, bb: (xx + bb,), [(dh, _tok(D, tm))], [w["b_o"]], [(dh, _tok(D, tm))], [], [],
                      grid=(t // tm,), name=f"{tag}_dbo")[0]
    dqkv, db_qkv, dsinks = _attn_bwd(qkv, do, cos, sin, w["sinks"], rot, f"{tag}_dcore")
    dh_in, dhb_in, dg = _mm_drms(dqkv[None], w["qkv"][None], h, w["norm_mix"], dh, f"{tag}_dhn", 512, tb=False)
    dw_qkv = _mm(dqkv[None], hn[None], ta=True, tm=512, out_dtypes=(BF16,), name=f"{tag}_dwqkv")[0][0]
    grads = dict(norm_mix=dg, qkv=dw_qkv, b_qkv=db_qkv, sinks=dsinks, w_o=dw_o, b_o=db_o)
    return dh_in, dhb_in, grads


def _rope_tables(t):
    inv = ROPE_THETA ** (-jnp.arange(0, 64, 2, dtype=F32) / 64)
    ang = jnp.arange(t, dtype=F32)[:, None] * inv[None, :]
    cos, sin = jnp.tile(jnp.cos(ang), (1, 4)), jnp.tile(jnp.sin(ang), (1, 4))
    rot = np.zeros((LANE, LANE), np.float32)
    for j in range(LANE):
        if j % 64 < 32:
            rot[j + 32, j] = -1.0
        else:
            rot[j - 32, j] = 1.0
    return cos, sin, jnp.asarray(rot)


def _local_step(x, p, tgt, layers, final_norm):
    _restart_chain()
    tables = _rope_tables(x.shape[0])
    pb = p.astype(BF16)
    h, saved = x, []
    for i, w in enumerate(layers):
        h, s = _layer_fwd(i, h, w, pb[i], tables)
        saved.append(s)
    loss, dh, dhb, d_final = _loss_head(h, tgt, final_norm, "loss_head")
    grads = [None] * len(layers)
    for i in reversed(range(len(layers))):
        dh, dhb, head = _layer_bwd_head(i, dh, dhb, saved[i], layers[i], pb[i])
        dh, dhb, tail = _layer_bwd_tail(i, dh, dhb, saved[i], layers[i], tables)
        grads[i] = {**head, **tail}
    return loss[0, 0], dh, grads, d_final


def _layer_fwd(i, h, w, pb, tables, deps=()):
    s = {}
    h, s["ffn1"] = _ffn_fwd(h, w["norm_ffn1"], w["ffn1_in"], w["ffn1_out"], f"l{i}_ffn1", deps=deps)
    if i % 2 == 0:
        h, s["mix"] = _hyb_fwd(h, w, f"l{i}_hyb")
    else:
        h, s["mix"] = _att_fwd(h, w, tables, f"l{i}_att")
    h, s["ffn2"] = _ffn_fwd(h, w["norm_ffn2"], w["ffn2_in"], w["ffn2_out"], f"l{i}_ffn2")
    h, s["ple"] = _ple_fwd(h, w["ple_norm"], pb, w["ple_gate"], w["ple_proj"], f"l{i}_ple")
    return h, s


def _layer_bwd_head(i, dh, dhb, s, w, pb, deps=()):
    g = {}
    dh, dhb, g["ple_norm"], g["ple_gate"], g["ple_proj"] = _ple_bwd(dh, dhb, s["ple"], w["ple_norm"], pb, w["ple_gate"],
                                                                    f"l{i}_ple", deps=deps)
    return dh, dhb, g


def _layer_bwd_tail(i, dh, dhb, s, w, tables, deps=()):
    g = {}
    dh, dhb, g["norm_ffn2"], g["ffn2_in"], g["ffn2_out"] = _ffn_bwd(dh, dhb, s["ffn2"], w["norm_ffn2"], w["ffn2_in"],
                                                                    w["ffn2_out"], f"l{i}_ffn2", deps=deps)
    if i % 2 == 0:
        dh, dhb, gm = _hyb_bwd(dh, dhb, s["mix"], w, f"l{i}_hyb")
    else:
        dh, dhb, gm = _att_bwd(dh, dhb, s["mix"], w, tables, f"l{i}_att")
    g.update(gm)
    dh, dhb, g["norm_ffn1"], g["ffn1_in"], g["ffn1_out"] = _ffn_bwd(dh, dhb, s["ffn1"], w["norm_ffn1"], w["ffn1_in"],
                                                                    w["ffn1_out"], f"l{i}_ffn1")
    return dh, dhb, g


def _cols(g):
    full = jnp.moveaxis(g, 0, -2)
    return full.reshape(*full.shape[:-2], N_DEV * g.shape[-1])


def _uncols(full):
    split = full.reshape(*full.shape[:-1], N_DEV, full.shape[-1] // N_DEV)
    return jnp.moveaxis(split, -2, 0)


def _lane_pad(v):
    return jnp.pad(v, ((0, 0), (0, LANE - v.shape[1])))


def _build_layers(gw, gs, rep):
    return [_build_layer(i, gw, gs, rep) for i in range(2)]


def _build_layer(i, gw, gs, rep, parts=("ffn1", "mix", "ffn2", "ple")):
    w = {}
    for f in ("ffn1", "ffn2"):
        if f in parts:
            w[f"norm_{f}"] = rep[f"norm_{f}"][i][None]
            w[f"{f}_in"] = gw[f"{f}_w_in", i]
            w[f"{f}_out"] = gw[f"{f}_w_out", i].reshape(4, FF_SHARD, D)
    if "ple" in parts:
        w["ple_norm"] = rep["ple_norm"][i][None]
        w["ple_gate"] = gw["ple_gate_w", i].reshape(D, D)
        w["ple_proj"] = _cols(gw["ple_proj_w", i])
    if "mix" not in parts:
        return w
    w["norm_mix"] = rep["norm_mix"][i][None]
    if i == 0:
        w["hyb_in"] = jnp.pad(_cols(gw["hyb_w_in", 0]), ((0, 0), (0, HYB_PAD - HYB_IN)))
        w["hyb_out"] = gw["hyb_w_out", 0].reshape(2, D, D)
        w["conv_w"] = _cols(gs["conv_dw_w"][:, 0])
        w["sconv_w"] = _cols(gs["ssm_conv_w"][:, 0])
        w["conv_b"], w["ln_g"], w["ln_b"] = rep["conv_dw_b"], rep["conv_ln_g"], rep["conv_ln_b"]
        w["sconv_b"], w["ssm_norm"] = rep["ssm_conv_b"], rep["ssm_norm"]
        w["dt_bias"], w["a_log"] = _lane_pad(rep["ssm_dt_bias"]), _lane_pad(rep["ssm_a_log"])
        w["d_skip"] = jnp.repeat(rep["ssm_d"], D // SSM_HEADS, axis=1)
    else:
        w["qkv"] = gw["att_w_qkv", 0].reshape(-1, D)
        w["w_o"] = gw["att_w_o", 0].reshape(D, D)
        w["b_qkv"] = gs["att_b_qkv"][:, 0].reshape(1, -1)
        w["b_o"] = gs["att_b_o"][:, 0].reshape(1, -1)
        w["sinks"] = _lane_pad(rep["att_sinks"])
    return w


def _big_grads(i, g):
    big = {}
    for f in ("ffn1", "ffn2"):
        if f"{f}_in" in g:
            big[f"{f}_w_in", i] = g[f"{f}_in"]
            big[f"{f}_w_out", i] = g[f"{f}_out"].reshape(N_DEV, D_FF // N_DEV, D)
    if "ple_gate" in g:
        big["ple_gate_w", i] = g["ple_gate"].reshape(N_DEV, D // N_DEV, D)
        big["ple_proj_w", i] = _uncols(g["ple_proj"])
    if "hyb_in" in g:
        big["hyb_w_in", 0] = _uncols(g["hyb_in"][:, :HYB_IN])
        big["hyb_w_out", 0] = g["hyb_out"].reshape(N_DEV, 2 * D // N_DEV, D)
    if "qkv" in g:
        big["att_w_qkv", 0] = g["qkv"].reshape(N_DEV, -1, D)
        big["att_w_o", 0] = g["w_o"].reshape(N_DEV, D // N_DEV, D)
    return big


def _collect_grads(grads, d_final):
    g0, g1 = grads
    big, small = {**_big_grads(0, g0), **_big_grads(1, g1)}, {}
    for f in ("ffn1", "ffn2"):
        small[f"norm_{f}"] = jnp.concatenate([g[f"norm_{f}"] for g in grads], axis=0)
    small["norm_mix"] = jnp.concatenate([g["norm_mix"] for g in grads], axis=0)
    small["ple_norm"] = jnp.concatenate([g["ple_norm"] for g in grads], axis=0)
    small["conv_dw_w"] = g0["conv_w"][None]
    small["conv_dw_b"], small["conv_ln_g"], small["conv_ln_b"] = g0["conv_b"], g0["ln_g"], g0["ln_b"]
    small["ssm_conv_w"] = g0["sconv_w"][None]
    small["ssm_conv_b"], small["ssm_norm"] = g0["sconv_b"], g0["ssm_norm"]
    small["ssm_dt_bias"], small["ssm_a_log"] = g0["dt_bias"][:, :SSM_HEADS], g0["a_log"][:, :SSM_HEADS]
    small["ssm_d"] = g0["d_skip"].reshape(1, SSM_HEADS, D // SSM_HEADS).sum(axis=-1)
    small["att_b_qkv"], small["att_b_o"] = g1["b_qkv"], g1["b_o"]
    small["att_sinks"] = g1["sinks"][:, :SSM_HEADS]
    small["final_norm"] = d_final[0]
    return big, small


MESH = pl.DeviceIdType.MESH


def _place():
    return lax.axis_index("x"), lax.axis_index("y"), lax.axis_index("c")


def _all_gather(blocks, space, name):
    nb = len(blocks)

    def body(*refs):
        x_refs, out_refs, (send_sems, recv_sems, local_sem) = refs[:nb], refs[nb:2 * nb], refs[2 * nb:]
        x, y, c = _place()
        me, sibling = (x, y, c), (x, y, 1 - c)
        chips = [(1 - x, y), (x, 1 - y), (1 - x, 1 - y)]

        def copies(k, blk, to, own=False):
            idx = 4 * blk[0] + 2 * blk[1] + blk[2]
            return [pltpu.make_async_remote_copy(src_ref=x_ref if own else out_ref.at[idx], dst_ref=out_ref.at[idx],
                                                 send_sem=send_sems.at[k, b], recv_sem=recv_sems.at[k, b], device_id=to,
                                                 device_id_type=MESH) for b, (x_ref, out_ref) in enumerate(zip(x_refs, out_refs))]

        mine = [pltpu.make_async_copy(x_ref, out_ref.at[4 * x + 2 * y + c], local_sem.at[b])
                for b, (x_ref, out_ref) in enumerate(zip(x_refs, out_refs))]
        first = copies(0, me, sibling, own=True)
        for j, chip in enumerate(chips):
            first += copies(1 + j, me, (*chip, c), own=True)
        for cp in mine + first:
            cp.start()
        passed = []
        for j, chip in enumerate(chips):
            for cp in copies(1 + j, (*chip, c), me):
                cp.wait_recv()
            onward = copies(4 + j, (*chip, c), sibling)
            for cp in onward:
                cp.start()
            passed += onward
        for cp in copies(0, sibling, me):
            cp.wait_recv()
        for j, chip in enumerate(chips):
            for cp in copies(4 + j, (*chip, 1 - c), me):
                cp.wait_recv()
        for cp in first + passed:
            cp.wait_send()
        for cp in mine:
            cp.wait()

    spec = pl.BlockSpec(memory_space=space)
    return _pallas(
        body, name=name, out_shape=[SDS((N_DEV,) + b.shape, b.dtype) for b in blocks],
        in_specs=[spec] * nb, out_specs=[spec] * nb,
        scratch_shapes=[pltpu.SemaphoreType.DMA((7, nb)), pltpu.SemaphoreType.DMA((7, nb)), pltpu.SemaphoreType.DMA((nb,))],
    )(*blocks)


HBM_SPEC = pl.BlockSpec(memory_space=pltpu.HBM)
SEM_SPEC = pl.BlockSpec(memory_space=pltpu.SEMAPHORE)
EFFECT = pltpu.SideEffectType.DATAFLOW_SIDE_EFFECTING


def _plan_descriptors(plan, srcs, lands, send_sems, recv_sems, local_sems, arriving):
    remote, local = plan(*_place())

    def pick(si, slot):
        ref = lands[si[1]] if isinstance(si, tuple) else srcs[si]
        return ref if slot is None else ref.at[slot]

    rem = [pltpu.make_async_remote_copy(src_ref=pick(si, ss), dst_ref=lands[li].at[rs if arriving else ds],
                                        send_sem=send_sems.at[k], recv_sem=recv_sems.at[k], device_id=dev, device_id_type=MESH)
           for k, (si, ss, li, ds, dev, rs) in enumerate(remote)]
    loc = [pltpu.make_async_copy(pick(si, ss), lands[li].at[ds], local_sems.at[k])
           for k, (si, ss, li, ds) in enumerate(local)]
    return rem, loc


def _plan_counts(plan):
    remote, local = plan(0, 0, 0)
    return len(remote), max(len(local), 1)


def _exchange_start(srcs, land_shapes, plan, name, lands=None):
    ns, nl = len(srcs), len(lands if lands is not None else land_shapes)
    n_remote, n_local = _plan_counts(plan)
    if lands is None:
        lands = [pltpu.with_memory_space_constraint(lax.empty(s.shape, s.dtype), pltpu.HBM) for s in land_shapes]
    lands = list(lands)
    srcs = [pltpu.with_memory_space_constraint(s, pltpu.HBM) for s in srcs]

    def body(*refs):
        src_refs, land_refs = refs[:ns], refs[ns:ns + nl]
        send_sems, recv_sems, local_sems = refs[ns + nl:ns + nl + 3]
        token = refs[-1]
        rem, loc = _plan_descriptors(plan, src_refs, land_refs, send_sems, recv_sems, local_sems, arriving=False)
        for cp in loc + rem:
            cp.start()
        token[...] = jnp.zeros_like(token)

    outs = _pallas(
        body, name=name,
        out_shape=[pltpu.SemaphoreType.DMA((n_remote,)), pltpu.SemaphoreType.DMA((n_remote,)), pltpu.SemaphoreType.DMA((n_local,))]
        + [pltpu.HBM(a.shape, a.dtype) for a in srcs + lands] + [SDS((8, LANE), F32)],
        in_specs=[HBM_SPEC] * (ns + nl),
        out_specs=[SEM_SPEC] * 3 + [HBM_SPEC] * (ns + nl) + [pl.BlockSpec(memory_space=pltpu.VMEM)],
        input_output_aliases={i: 3 + i for i in range(ns + nl)},
        compiler_params=pltpu.CompilerParams(has_side_effects=EFFECT),
    )(*srcs, *lands)
    return (outs[:3], outs[3:3 + ns], outs[3 + ns:3 + ns + nl]), outs[-1]


def _exchange_wait(state, after, plan, name):
    sems, srcs, lands = state
    ns, nl = len(srcs), len(lands)

    def body(*refs):
        src_refs, land_refs = refs[:ns], refs[ns:ns + nl]
        send_sems, recv_sems, local_sems = refs[ns + nl:ns + nl + 3]
        rem, loc = _plan_descriptors(plan, src_refs, land_refs, send_sems, recv_sems, local_sems, arriving=True)
        for cp in rem:
            cp.wait_send()
            cp.wait_recv()
        for cp in loc:
            cp.wait()

    outs = _pallas(
        body, name=name, out_shape=[pltpu.HBM(a.shape, a.dtype) for a in list(srcs) + list(lands)],
        in_specs=[HBM_SPEC] * (ns + nl) + [SEM_SPEC] * 3 + [ANY_SPEC] * (after is not None), out_specs=[HBM_SPEC] * (ns + nl),
        input_output_aliases={i: i for i in range(ns + nl)},
        compiler_params=pltpu.CompilerParams(has_side_effects=EFFECT),
    )(*srcs, *lands, *sems, *([after] if after is not None else []))
    return outs[:ns], outs[ns:]


def _gather_plan(nb):
    def plan(x, y, c):
        me = 4 * x + 2 * y + c
        peers = [(x, y, 1 - c), (1 - x, y, c), (x, 1 - y, c), (1 - x, 1 - y, c)]
        remote = [(b, None, b, me, peer, 4 * peer[0] + 2 * peer[1] + peer[2]) for b in range(nb) for peer in peers]
        return remote, [(b, None, b, me) for b in range(nb)]
    return plan


def _relay_plan(nb):
    def plan(x, y, c):
        chips = [(1 - x, y), (x, 1 - y), (1 - x, 1 - y)]
        remote = [(("land", b), 4 * cx + 2 * cy + c, b, 4 * cx + 2 * cy + c, (x, y, 1 - c), 4 * cx + 2 * cy + (1 - c))
                  for b in range(nb) for cx, cy in chips]
        return remote, []
    return plan


def _pair_plan(nb):
    def plan(x, y, c):
        return [(b, 2 * q + (1 - c), b, q, (x, y, 1 - c), q) for b in range(nb) for q in range(4)], []
    return plan


def _chip_plan(nb):
    def plan(x, y, c):
        own = 2 * x + y
        chips = [(1 - x, y), (x, 1 - y), (1 - x, 1 - y)]
        remote = [(b, 2 * cx + cy, b, own, (cx, cy, c), 2 * cx + cy) for b in range(nb) for cx, cy in chips]
        return remote, [(b, own, b, own) for b in range(nb)]
    return plan


def _row_tile(r, cap=4608):
    return max(d for d in range(16, min(r, cap) + 1, 16) if r % d == 0)


def _pair_add(parts, got, core, name):
    _, r, cdim = parts.shape
    tr = _row_tile(r)

    def body(core_ref, p_ref, g_ref, o_ref):
        o_ref[...] = (p_ref[...].astype(F32) + g_ref[...].astype(F32)).astype(o_ref.dtype)

    return pl.pallas_call(
        body, name=name, out_shape=SDS((4, r, cdim), BF16),
        grid_spec=pltpu.PrefetchScalarGridSpec(
            num_scalar_prefetch=1, grid=(4, r // tr),
            in_specs=[pl.BlockSpec((None, tr, cdim), lambda q, i, core_ref: (2 * q + core_ref[0], i, 0)),
                      pl.BlockSpec((None, tr, cdim), lambda q, i, core_ref: (q, i, 0))],
            out_specs=pl.BlockSpec((None, tr, cdim), lambda q, i, core_ref: (q, i, 0))),
        compiler_params=_params(("parallel", "parallel")),
    )(core, parts, got)


def _sum_slots(parts, name):
    nj, r, cdim = parts.shape
    tr = _row_tile(r)

    def body(p_ref, o_ref):
        acc = p_ref[0].astype(F32)
        for j in range(1, nj):
            acc = acc + p_ref[j].astype(F32)
        o_ref[...] = acc

    return _pallas(
        body, name=name, out_shape=SDS((r, cdim), F32), grid=(r // tr,),
        in_specs=[pl.BlockSpec((nj, tr, cdim), lambda i: (0, i, 0))], out_specs=pl.BlockSpec((tr, cdim), lambda i: (i, 0)),
        compiler_params=_params(("parallel",)),
    )(parts)


def _adamw_update(wv, gv, mv, vv):
    nm = ADAM_B1 * mv + (1.0 - ADAM_B1) * gv
    nv = ADAM_B2 * vv + (1.0 - ADAM_B2) * (gv * gv)
    m_hat = nm / (1.0 - ADAM_B1 ** ADAM_STEP)
    v_hat = nv / (1.0 - ADAM_B2 ** ADAM_STEP)
    return -ADAM_LR * (m_hat / (jnp.sqrt(v_hat) + ADAM_EPS) + ADAM_WD * wv), nm, nv


def _adamw_summed(w, m, v, by_chip, name):
    nl, r, cdim = w.shape
    tr = _row_tile(r, 512)
    nblk = r // tr

    def body(*refs):
        chip_refs, (w_ref, m_ref, v_ref, g_ref, d_ref, nm_ref, nv_ref) = refs[:nl], refs[nl:]
        layer = pl.program_id(0)
        gv = None
        for ll, c_ref in enumerate(chip_refs):
            s = c_ref[0].astype(F32)
            for q in range(1, 4):
                s = s + c_ref[q].astype(F32)
            gv = s if gv is None else jnp.where(layer == ll, s, gv)
        g_ref[...] = gv
        d_ref[...], nm_ref[...], nv_ref[...] = _adamw_update(w_ref[...], gv, m_ref[...], v_ref[...])

    def chip_map(ll):
        return lambda l, i: (0, jnp.where(l == ll, i, jnp.where(l > ll, nblk - 1, 0)), 0)

    spec = pl.BlockSpec((None, tr, cdim), lambda l, i: (l, i, 0))
    return _pallas(
        body, name=name, grid=(nl, nblk),
        in_specs=[pl.BlockSpec((4, tr, cdim), chip_map(ll)) for ll in range(nl)] + [spec] * 3,
        out_specs=[spec] * 4, out_shape=[SDS((nl, r, cdim), F32)] * 4,
        compiler_params=_params(("arbitrary", "arbitrary")),
    )(*by_chip, w, m, v)


def _adamw(w, g, m, v, name):
    shape = w.shape
    cdim = shape[-1]
    w2, g2, m2, v2 = (a.reshape(-1, cdim) for a in (w, g, m, v))
    r = w2.shape[0]
    tr = next(d for d in (512, 352, 256, 128, 64, 32, 16, 8, r) if r % d == 0)

    def body(w_ref, g_ref, m_ref, v_ref, d_ref, nm_ref, nv_ref):
        d_ref[...], nm_ref[...], nv_ref[...] = _adamw_update(w_ref[...], g_ref[...], m_ref[...], v_ref[...])

    spec = pl.BlockSpec((tr, cdim), lambda i: (i, 0))
    outs = _pallas(
        body, name=name, grid=(r // tr,), in_specs=[spec] * 4, out_specs=[spec] * 3, out_shape=[SDS((r, cdim), F32)] * 3,
        compiler_params=_params(("parallel",)),
    )(w2, g2, m2, v2)
    return tuple(o.reshape(shape) for o in outs)


WEIGHTS = ("norm_ffn1", "ffn1_w_in", "ffn1_w_out", "norm_mix", "norm_ffn2", "ffn2_w_in", "ffn2_w_out", "ple_norm", "ple_gate_w",
           "ple_proj_w", "hyb_w_in", "conv_dw_w", "conv_dw_b", "conv_ln_g", "conv_ln_b", "ssm_conv_w", "ssm_conv_b", "ssm_dt_bias",
           "ssm_a_log", "ssm_d", "ssm_norm", "hyb_w_out", "att_w_qkv", "att_b_qkv", "att_sinks", "att_w_o", "att_b_o", "final_norm")
BIG = ("ffn1_w_in", "ffn1_w_out", "ffn2_w_in", "ffn2_w_out", "ple_gate_w", "ple_proj_w", "hyb_w_in", "hyb_w_out", "att_w_qkv",
       "att_w_o")
SMALL_SHARDED = {"conv_dw_w": 2, "ssm_conv_w": 2, "att_b_qkv": 1, "att_b_o": 1}
SMALL = tuple(n for n in WEIGHTS if n not in BIG)
TRANSPOSED = ("ffn1_w_in", "ffn2_w_in", "att_w_qkv")
PACK_ROWS = 16


def _pack(arrays, lead=0):
    pieces = []
    for a in arrays:
        flat = a.reshape(*a.shape[:lead], -1)
        size = flat.shape[-1]
        padded = -(-size // (PACK_ROWS * LANE)) * PACK_ROWS * LANE
        flat = jnp.pad(flat, [(0, 0)] * lead + [(0, padded - size)])
        pieces.append(flat.reshape(*a.shape[:lead], padded // LANE, LANE))
    return jnp.concatenate(pieces, axis=lead)


def _unpack(buf, shapes, lead=0):
    out, row = [], 0
    for shape in shapes:
        size = math.prod(shape)
        rows = -(-size // (PACK_ROWS * LANE)) * PACK_ROWS
        piece = lax.slice_in_dim(buf, row, row + rows, axis=lead)
        piece = piece.reshape(*buf.shape[:lead], rows * LANE)
        out.append(lax.slice_in_dim(piece, 0, size, axis=lead).reshape(*buf.shape[:lead], *shape))
        row += rows
    return out


def kernel(x, p, norm_ffn1, ffn1_w_in, ffn1_w_out, norm_mix, norm_ffn2, ffn2_w_in, ffn2_w_out, ple_norm, ple_gate_w, ple_proj_w, hyb_w_in, conv_dw_w, conv_dw_b, conv_ln_g, conv_ln_b, ssm_conv_w, ssm_conv_b, ssm_dt_bias, ssm_a_log, ssm_d, ssm_norm, hyb_w_out, att_w_qkv, att_b_qkv, att_sinks, att_w_o, att_b_o, final_norm, loss_target, m_norm_ffn1, m_ffn1_w_in, m_ffn1_w_out, m_norm_mix, m_norm_ffn2, m_ffn2_w_in, m_ffn2_w_out, m_ple_norm, m_ple_gate_w, m_ple_proj_w, m_hyb_w_in, m_conv_dw_w, m_conv_dw_b, m_conv_ln_g, m_conv_ln_b, m_ssm_conv_w, m_ssm_conv_b, m_ssm_dt_bias, m_ssm_a_log, m_ssm_d, m_ssm_norm, m_hyb_w_out, m_att_w_qkv, m_att_b_qkv, m_att_sinks, m_att_w_o, m_att_b_o, m_final_norm, v_norm_ffn1, v_ffn1_w_in, v_ffn1_w_out, v_norm_mix, v_norm_ffn2, v_ffn2_w_in, v_ffn2_w_out, v_ple_norm, v_ple_gate_w, v_ple_proj_w, v_hyb_w_in, v_conv_dw_w, v_conv_dw_b, v_conv_ln_g, v_conv_ln_b, v_ssm_conv_w, v_ssm_conv_b, v_ssm_dt_bias, v_ssm_a_log, v_ssm_d, v_ssm_norm, v_hyb_w_out, v_att_w_qkv, v_att_b_qkv, v_att_sinks, v_att_w_o, v_att_b_o, v_final_norm):
    args = (norm_ffn1, ffn1_w_in, ffn1_w_out, norm_mix, norm_ffn2, ffn2_w_in, ffn2_w_out, ple_norm, ple_gate_w, ple_proj_w, hyb_w_in, conv_dw_w, conv_dw_b, conv_ln_g, conv_ln_b, ssm_conv_w, ssm_conv_b, ssm_dt_bias, ssm_a_log, ssm_d, ssm_norm, hyb_w_out, att_w_qkv, att_b_qkv, att_sinks, att_w_o, att_b_o, final_norm)
    moments_m = (m_norm_ffn1, m_ffn1_w_in, m_ffn1_w_out, m_norm_mix, m_norm_ffn2, m_ffn2_w_in, m_ffn2_w_out, m_ple_norm, m_ple_gate_w, m_ple_proj_w, m_hyb_w_in, m_conv_dw_w, m_conv_dw_b, m_conv_ln_g, m_conv_ln_b, m_ssm_conv_w, m_ssm_conv_b, m_ssm_dt_bias, m_ssm_a_log, m_ssm_d, m_ssm_norm, m_hyb_w_out, m_att_w_qkv, m_att_b_qkv, m_att_sinks, m_att_w_o, m_att_b_o, m_final_norm)
    moments_v = (v_norm_ffn1, v_ffn1_w_in, v_ffn1_w_out, v_norm_mix, v_norm_ffn2, v_ffn2_w_in, v_ffn2_w_out, v_ple_norm, v_ple_gate_w, v_ple_proj_w, v_hyb_w_in, v_conv_dw_w, v_conv_dw_b, v_conv_ln_g, v_conv_ln_b, v_ssm_conv_w, v_ssm_conv_b, v_ssm_dt_bias, v_ssm_a_log, v_ssm_d, v_ssm_norm, v_hyb_w_out, v_att_w_qkv, v_att_b_qkv, v_att_sinks, v_att_w_o, v_att_b_o, v_final_norm)
    w = dict(zip(WEIGHTS, args))
    m = dict(zip(WEIGHTS, moments_m))
    v = dict(zip(WEIGHTS, moments_v))
    cx, cy, cc = _place()
    me = 4 * cx + 2 * cy + cc

    core = jnp.reshape(cc, (1,)).astype(jnp.int32)
    layer_of = lambda n, i: 1 if n.startswith("att_") else i
    keys = [[(n, i) for n in BIG for i in range(w[n].shape[0]) if layer_of(n, i) == layer] for layer in range(2)]

    first = [key for key in keys[0] if key[0].startswith("ffn1")]
    mixer = [key for key in keys[0] if key[0].startswith("hyb")]
    rest0 = [key for key in keys[0] if key not in first + mixer]
    gw, by_chip = {}, {}
    view = lambda a, n: jnp.swapaxes(a, 1, 2) if n in TRANSPOSED else a
    block = lambda n, i: view(w[n], n)[i].astype(BF16)

    def gather_later(group, name):
        blocks = [block(n, i) for n, i in group]
        plan, relay_plan = _gather_plan(len(blocks)), _relay_plan(len(blocks))
        state, _ = _exchange_start(blocks, [SDS((N_DEV,) + b.shape, BF16) for b in blocks], plan, f"{name}_start")
        stage = {}

        def relay():
            _, landed = _exchange_wait(state, None, plan, f"{name}_wait")
            stage["relay"], _ = _exchange_start([], None, relay_plan, f"{name}_relay_start", lands=landed)

        def arrived():
            gw.update(zip(group, _exchange_wait(stage["relay"], None, relay_plan, f"{name}_relay_wait")[1]))

        return relay, arrived

    def reduce_later(group, big, name):
        pair_plan, chip_plan = _pair_plan(len(group)), _chip_plan(len(group))
        parts = [big[key] for key in group]
        pair, token = _exchange_start(parts, [SDS((4,) + pt.shape[1:], BF16) for pt in parts], pair_plan, f"{name}_pair_start")
        stage = {}

        def middle(after):
            thru, got = _exchange_wait(pair, after, pair_plan, f"{name}_pair_wait")
            sums = [_pair_add(pt, gt, core, f"grads_pair_add_{n}_{i}") for pt, gt, (n, i) in zip(thru, got, group)]
            stage["chip"], chip_token = _exchange_start(sums, [SDS(s.shape, BF16) for s in sums], chip_plan, f"{name}_chip_start")
            return chip_token

        def finish(after):
            by_chip.update(zip(group, _exchange_wait(stage["chip"], after, chip_plan, f"{name}_chip_wait")[1]))

        return token, middle, finish

    _restart_chain()
    gw["ffn1_w_in", 0], gathered_small = _all_gather([block("ffn1_w_in", 0), _pack([w[n] for n in SMALL_SHARDED])],
                                                      pltpu.HBM, "gather_weights_first")
    early_relay, early_arrived = gather_later([("ffn1_w_out", 0)], "gather_weights_early")
    mixer_relay, mixer_arrived = gather_later(mixer, "gather_weights_mixer")
    rest0_relay, rest0_arrived = gather_later(rest0, "gather_weights_rest")
    layer1_relay, layer1_arrived = gather_later(keys[1], "gather_weights_l1")
    gs = dict(zip(SMALL_SHARDED, _unpack(gathered_small, [w[n].shape for n in SMALL_SHARDED], lead=1)))
    rep = {n: w[n] for n in SMALL if n not in SMALL_SHARDED}

    tables = _rope_tables(x.shape[1])
    pb = p[:, 0].astype(BF16)
    w0, s0 = {}, {}

    def first_w_out():
        early_relay()
        early_arrived()
        w0.update(_build_layer(0, gw, gs, rep, parts=("ffn1",)))
        return w0["ffn1_out"]

    h, s0["ffn1"] = _ffn_fwd(x[0], rep["norm_ffn1"][0][None], gw["ffn1_w_in", 0], first_w_out, "l0_ffn1")
    mixer_relay()
    mixer_arrived()
    w0.update(_build_layer(0, gw, gs, rep, parts=("mix",)))
    h, s0["mix"] = _hyb_fwd(h, w0, "l0_hyb", after_in=rest0_relay)
    layer1_relay()
    rest0_arrived()
    w0.update(_build_layer(0, gw, gs, rep, parts=("ffn2", "ple")))
    h, s0["ffn2"] = _ffn_fwd(h, w0["norm_ffn2"], w0["ffn2_in"], w0["ffn2_out"], "l0_ffn2")
    h, s0["ple"] = _ple_fwd(h, w0["ple_norm"], pb[0], w0["ple_gate"], w0["ple_proj"], "l0_ple")
    layer1_arrived()
    w1 = _build_layer(1, gw, gs, rep)
    h, s1 = _layer_fwd(1, h, w1, pb[1], tables)
    loss, dh, dhb, d_final = _loss_head(h, loss_target[0], final_norm[None], "loss_head")
    loss = lax.psum(loss[0, 0], ("x", "y", "c"))

    dh, dhb, head1 = _layer_bwd_head(1, dh, dhb, s1, w1, pb[1])
    dh, dhb, tail1 = _layer_bwd_tail(1, dh, dhb, s1, w1, tables)
    grads1 = {**head1, **tail1}
    l1_token, l1_middle, l1_finish = reduce_later(keys[1], _big_grads(1, grads1), "grads_l1")
    dh, dhb, grads0 = _layer_bwd_head(0, dh, dhb, s0, w0, pb[0], deps=(l1_token,))
    dh, dhb, grads0["norm_ffn2"], grads0["ffn2_in"], grads0["ffn2_out"] = _ffn_bwd(
        dh, dhb, s0["ffn2"], w0["norm_ffn2"], w0["ffn2_in"], w0["ffn2_out"], "l0_ffn2", deps=(l1_middle(dh),))
    dh, dhb, mixer_grads = _hyb_bwd(dh, dhb, s0["mix"], w0, "l0_hyb")
    grads0.update(mixer_grads)
    l0_token, l0_middle, l0_finish = reduce_later(mixer + rest0, _big_grads(0, grads0), "grads_l0")
    last = {}

    def reduce_first(dw_in, dw_out):
        token, middle, last["finish"] = reduce_later(first, _big_grads(0, dict(ffn1_in=dw_in, ffn1_out=dw_out)), "grads_first")
        middle(token)

    dx, dhb, grads0["norm_ffn1"], grads0["ffn1_in"], grads0["ffn1_out"] = _ffn_bwd(
        dh, dhb, s0["ffn1"], w0["norm_ffn1"], w0["ffn1_in"], w0["ffn1_out"], "l0_ffn1", deps=(l0_token,),
        hook=lambda dpre: (l0_middle(dpre),), weights_hook=reduce_first)
    l1_finish(dx)
    l0_finish(dx)
    last["finish"](dx)
    _, small = _collect_grads([grads0, grads1], d_final)
    small_shapes = [small[n].shape for n in SMALL]
    all_small = _all_gather([_pack([small[n] for n in SMALL])], pltpu.VMEM, "gather_small_grads")[0]
    g = dict(zip(SMALL, _unpack(_sum_slots(all_small, "small_grads_sum"), small_shapes)))
    for n, axis in SMALL_SHARDED.items():
        g[n] = lax.dynamic_slice_in_dim(g[n], me * w[n].shape[axis], w[n].shape[axis], axis=axis)

    delta, new_m, new_v = {}, {}, {}
    for n in BIG:
        outs = _adamw_summed(view(w[n], n), view(m[n], n), view(v[n], n), [by_chip[n, i] for i in range(w[n].shape[0])],
                             f"adamw_{n}")
        g[n], delta[n], new_m[n], new_v[n] = (view(o, n) for o in outs)
    packed = [_pack([d[n] for n in SMALL]) for d in (w, g, m, v)]
    shapes = [w[n].shape for n in SMALL]
    for d, buf in zip((delta, new_m, new_v), _adamw(*packed, "adamw_small")):
        d.update(zip(SMALL, _unpack(buf, shapes)))
    return (loss, dx[None], *[g[n] for n in WEIGHTS], *[delta[n] for n in WEIGHTS], *[new_m[n] for n in WEIGHTS],
            *[new_v[n] for n in WEIGHTS])
```

```python
import functools
import math

import numpy as np
import jax
import jax.numpy as jnp
from jax import lax
from jax.experimental import pallas as pl
from jax.experimental.pallas import tpu as pltpu

F32, BF16 = jnp.float32, jnp.bfloat16
HI = lax.Precision.HIGHEST
SDS = jax.ShapeDtypeStruct

N_DEV = 8
D = 1024
D_FF = 2816
FF_SHARD = 2 * D_FF // N_DEV
PLE_DIM = 256
EPS = 1e-6
CONV_W = 31
SSM_CONV = 4
SSM_HEADS = 16
SSM_XBC = 1536
CHUNK = 128
HYB_IN = 4624
HYB_PAD = 5120
DT_COL = 4608
N_PAIR = 8
ROPE_THETA = 10000.0
LANE = 128
VMEM_LIMIT = 56 * 1024 * 1024

ADAM_LR, ADAM_B1, ADAM_B2, ADAM_EPS, ADAM_WD, ADAM_STEP = 0.001, 0.9, 0.999, 1e-08, 0.01, 10


def _params(sem):
    return pltpu.CompilerParams(dimension_semantics=sem, vmem_limit_bytes=VMEM_LIMIT)


_CHAIN = []


def _restart_chain():
    _CHAIN.clear()


def _pallas(body, *, in_specs, **kw):
    def run(*args):
        n, dep = len(args), list(_CHAIN)

        def chained(*refs):
            return body(*refs[:n], *refs[n + len(dep):])

        outs = pl.pallas_call(chained, in_specs=list(in_specs) + [pl.BlockSpec(memory_space=pl.ANY)] * len(dep), **kw)(*args, *dep)
        _CHAIN[:] = [outs[-1] if isinstance(outs, (list, tuple)) else outs]
        return outs

    return run


def _mm(a, b, *, ta=False, tb=False, reduce_j=False, out_dtypes=(F32,), tm=1024, tn=1024, tk=1024,
        epi=None, extras=(), rows=(), deps=(), sums=0, name):
    ja, jb = a.shape[0], b.shape[0]
    nj = max(ja, jb)
    jo = 1 if reduce_j else nj
    m, k = (a.shape[2], a.shape[1]) if ta else (a.shape[1], a.shape[2])
    n = b.shape[1] if tb else b.shape[2]
    assert (b.shape[2] if tb else b.shape[1]) == k and ja in (1, nj) and jb in (1, nj)
    tm, tn, tk = min(tm, m), min(tn, n), min(tk, k)
    assert m % tm == 0 and n % tn == 0 and k % tk == 0, (name, m, n, k, tm, tn, tk)
    assert not sums or (tn == n and (reduce_j or nj == 1))
    nk = k // tk
    steps = nk * (nj if reduce_j else 1)
    ne, nr, no = len(extras), len(rows), len(out_dtypes)

    def a_map(i, c, j, kk):
        return (j if ja > 1 else 0, kk, i) if ta else (j if ja > 1 else 0, i, kk)

    def b_map(i, c, j, kk):
        return (j if jb > 1 else 0, c, kk) if tb else (j if jb > 1 else 0, kk, c)

    def o_map(i, c, j, kk):
        return (0 if reduce_j else j, i, c)

    dims = (((0 if ta else 1,), (1 if tb else 0,)), ((), ()))

    def body(a_ref, b_ref, *rest):
        ex, rw = rest[:ne], rest[ne:ne + nr]
        outs = rest[ne + nr + len(deps):ne + nr + len(deps) + no]
        sum_refs = rest[ne + nr + len(deps) + no:ne + nr + len(deps) + no + sums]
        first_tile = pl.program_id(0) == 0

        def product():
            return lax.dot_general(a_ref[...], b_ref[...], dims, preferred_element_type=F32)

        def finish(acc):
            res = epi(acc, *[e[...] for e in ex], *[r[...] for r in rw]) if epi else (acc,)
            for o, r in zip(outs, res):
                o[...] = r.astype(o.dtype)
            for s_ref, r in zip(sum_refs, res[no:]):
                @pl.when(first_tile)
                def _(s_ref=s_ref, r=r):
                    s_ref[...] = r

                @pl.when(jnp.logical_not(first_tile))
                def _(s_ref=s_ref, r=r):
                    s_ref[...] += r

        if steps == 1:
            finish(product())
            return
        acc_ref = rest[-1]
        kk = pl.program_id(3)
        step = pl.program_id(2) * nk + kk if reduce_j else kk

        @pl.when(step == 0)
        def _():
            acc_ref[...] = product()

        @pl.when(jnp.logical_and(step > 0, step < steps - 1))
        def _():
            acc_ref[...] += product()

        @pl.when(step == steps - 1)
        def _():
            finish(acc_ref[...] + product())

    o_spec = pl.BlockSpec((None, tm, tn), o_map)
    row_spec = pl.BlockSpec((1, tn), lambda i, c, j, kk: (0, c))
    return _pallas(
        body, name=name, grid=(m // tm, n // tn, nj, nk),
        in_specs=[pl.BlockSpec((None, tk, tm) if ta else (None, tm, tk), a_map),
                  pl.BlockSpec((None, tn, tk) if tb else (None, tk, tn), b_map)]
        + [o_spec] * ne + [row_spec] * nr + [ANY_SPEC] * len(deps),
        out_specs=[o_spec] * no + [row_spec] * sums,
        out_shape=[SDS((jo, m, n), dt) for dt in out_dtypes] + [SDS((1, n), F32)] * sums,
        scratch_shapes=[pltpu.VMEM((tm, tn), F32)] if steps > 1 else [],
        compiler_params=_params(("arbitrary" if sums else "parallel", "parallel", "arbitrary", "arbitrary")),
    )(a, b, *extras, *rows, *deps)


def _whole(p):
    return pl.BlockSpec(p.shape, lambda *_: (0,) * p.ndim)


ANY_SPEC = pl.BlockSpec(memory_space=pl.ANY)


def _rowop(fn, tiles, params, outs, *, grid, name, deps=()):
    nin = len(tiles) + len(params)

    def body(*refs):
        res = fn(*[r[...].astype(F32) for r in refs[:nin]])
        for r, o in zip(refs[nin + len(deps):], res):
            r[...] = o.astype(r.dtype)

    return _pallas(
        body, name=name, grid=grid,
        in_specs=[s for _, s in tiles] + [_whole(p) for p in params] + [ANY_SPEC] * len(deps),
        out_specs=[s for _, _, s in outs], out_shape=[SDS(sh, dt) for sh, dt, _ in outs],
        compiler_params=_params(("parallel",) * len(grid)),
    )(*[t for t, _ in tiles], *params, *deps)


def _rowop_bwd(fn, tiles, params, cots, wrt, gouts, *, grid, name, adds=(), deps=()):
    nt, npar, nc, na = len(tiles), len(params), len(cots), len(adds)
    nin = nt + npar
    flat = [i for grp in wrt for i in grp]
    n_gout = sum(len(dts) for _, dts, _ in gouts)

    def body(*refs):
        vals = [r[...].astype(F32) for r in refs[:nin]]
        cvals = [r[...].astype(F32) for r in refs[nin:nin + nc]]
        avals = [r[...].astype(F32) for r in refs[nin + nc:nin + nc + na]]
        orefs = refs[nin + nc + na + len(deps):]
        diff_idx = flat + list(range(nt, nin))

        def f(*dv):
            full = list(vals)
            for i, v in zip(diff_idx, dv):
                full[i] = v
            return fn(*full)

        _, vjp = jax.vjp(f, *[vals[i] for i in diff_idx])
        grads = vjp(tuple(cvals))
        tile_g, par_g = list(grads[:len(flat)]), grads[len(flat):]
        group_g, at = [], 0
        for grp in wrt:
            members = tile_g[at:at + len(grp)]
            at += len(grp)
            group_g.append(members[0] if len(grp) == 1 else jnp.stack(members, axis=0))
        for av in avals:
            group_g[0] = group_g[0] + av
        o = 0
        for g, (_, dts, _) in zip(group_g, gouts):
            for _ in dts:
                orefs[o][...] = g.astype(orefs[o].dtype)
                o += 1
        first = functools.reduce(jnp.logical_and, [pl.program_id(ax) == 0 for ax in range(len(grid))])
        for r, g in zip(orefs[n_gout:], par_g):
            @pl.when(first)
            def _(r=r, g=g):
                r[...] = g

            @pl.when(jnp.logical_not(first))
            def _(r=r, g=g):
                r[...] += g

    out_specs, out_shape = [], []
    for sh, dts, spec in gouts:
        for dt in dts:
            out_specs.append(spec)
            out_shape.append(SDS(sh, dt))
    for p in params:
        out_specs.append(_whole(p))
        out_shape.append(SDS(p.shape, F32))
    return _pallas(
        body, name=name, grid=grid,
        in_specs=[s for _, s in tiles] + [_whole(p) for p in params] + [s for _, s in cots] + [s for _, s in adds]
        + [ANY_SPEC] * len(deps),
        out_specs=out_specs, out_shape=out_shape,
        compiler_params=_params(("arbitrary",) * len(grid)),
    )(*[t for t, _ in tiles], *params, *[c for c, _ in cots], *[a for a, _ in adds], *deps)


def _tok(c, tm, col=0):
    return pl.BlockSpec((tm, c), lambda i, col=col: (i, col))


def _rms_fn(h, g):
    return (h * lax.rsqrt(jnp.mean(h * h, axis=-1, keepdims=True) + EPS) * g,)


def _lnswish_fn(u, g, b):
    mu = jnp.mean(u, axis=-1, keepdims=True)
    xc = u - mu
    y = xc * lax.rsqrt(jnp.mean(xc * xc, axis=-1, keepdims=True) + EPS) * g + b
    return (y * jax.nn.sigmoid(y),)


def _ple_fn(z, e):
    return (jax.nn.sigmoid(z) * e,)


def _rms(h, g, name, tm=512, deps=()):
    t = h.shape[0]
    return _rowop(_rms_fn, [(h, _tok(D, tm))], [g], [((t, D), BF16, _tok(D, tm))], grid=(t // tm,), name=name, deps=deps)[0]


def _drms_epi(dn, h, dres, g):
    _, vjp = jax.vjp(_rms_fn, h, g)
    dh, dg = vjp((dn,))
    dh = dh + dres
    return dh, dh, dg


def _mm_drms(a, b, h, g, dres, name, tk, tb=True):
    dh, dhb, dg = _mm(a, b, tb=tb, reduce_j=a.shape[0] > 1, tm=1024, tk=tk, epi=_drms_epi, extras=(h[None], dres[None]),
                      rows=(g,), out_dtypes=(F32, BF16), sums=1, name=name)
    return dh[0], dhb[0], dg


def _conv_geometry(width):
    pad = 32 if width > 8 else 8
    return pad, pad - (width - 1)


def _fill_shifts(xpad_ref, sh_ref, t, shifts):
    for r in shifts:
        sh_ref[r, :, :] = xpad_ref[pl.ds(r, t + 32), :]


def _dwconv(xs, w, b, *, width, glu, silu, cb, name):
    t = xs[0][0].shape[0]
    c = w.shape[1]
    pad, off = _conv_geometry(width)
    shifts = sorted({(k + off) % 8 for k in range(width)})
    ch = 128

    def body(*refs):
        x_refs, (w_ref, b_ref, o_ref, xpad_ref, sh_ref) = refs[:len(xs)], refs[len(xs):]
        u = x_refs[0][...] * jax.nn.sigmoid(x_refs[1][...]) if glu else x_refs[0][...]
        xpad_ref[pl.ds(0, pad), :] = jnp.zeros((pad, cb), F32)
        xpad_ref[pl.ds(pad, t), :] = u
        xpad_ref[pl.ds(pad + t, 40 - pad), :] = jnp.zeros((40 - pad, cb), F32)
        _fill_shifts(xpad_ref, sh_ref, t, shifts)

        def chunk(i, carry):
            t0 = pl.multiple_of(i * ch, ch)
            acc = jnp.broadcast_to(b_ref[...], (ch, cb))
            for k in range(width):
                q, r = divmod(k + off, 8)
                acc = acc + w_ref[pl.ds(k, 1), :] * sh_ref[r, pl.ds(t0 + 8 * q, ch), :]
            o_ref[pl.ds(t0, ch), :] = acc * jax.nn.sigmoid(acc) if silu else acc
            return carry

        lax.fori_loop(0, t // ch, chunk, 0)

    return _pallas(
        body, name=name, grid=(c // cb,),
        in_specs=[pl.BlockSpec((t, cb), lambda i, o=o: (0, o + i)) for _, o in xs]
        + [pl.BlockSpec((width, cb), lambda i: (0, i)), pl.BlockSpec((1, cb), lambda i: (0, i))],
        out_specs=pl.BlockSpec((t, cb), lambda i: (0, i)), out_shape=SDS((t, c), F32),
        scratch_shapes=[pltpu.VMEM((t + 40, cb), F32), pltpu.VMEM((8, t + 32, cb), F32)],
        compiler_params=_params(("parallel",)),
    )(*[x for x, _ in xs], w, b)


def _dwconv_bwd(xs, w, b, dy, *, width, glu, silu, cb, name):
    t = xs[0][0].shape[0]
    c = w.shape[1]
    pad, off = _conv_geometry(width)
    shifts = sorted({(k + off) % 8 for k in range(width)})
    shifts_t = sorted({mm % 8 for mm in range(width)})
    ch = 128
    nx = len(xs)

    def body(*refs):
        x_refs = refs[:nx]
        w_ref, b_ref, dy_ref = refs[nx:nx + 3]
        dx_refs = refs[nx + 3:nx + 3 + nx]
        dw_ref, db_ref, xpad_ref, sh_ref, dc_ref = refs[nx + 3 + nx:]
        u = x_refs[0][...] * jax.nn.sigmoid(x_refs[1][...]) if glu else x_refs[0][...]
        xpad_ref[pl.ds(0, pad), :] = jnp.zeros((pad, cb), F32)
        xpad_ref[pl.ds(pad, t), :] = u
        xpad_ref[pl.ds(pad + t, 40 - pad), :] = jnp.zeros((40 - pad, cb), F32)
        _fill_shifts(xpad_ref, sh_ref, t, shifts)

        if silu:
            def act_chunk(i, carry):
                t0 = pl.multiple_of(i * ch, ch)
                acc = jnp.broadcast_to(b_ref[...], (ch, cb))
                for k in range(width):
                    q, r = divmod(k + off, 8)
                    acc = acc + w_ref[pl.ds(k, 1), :] * sh_ref[r, pl.ds(t0 + 8 * q, ch), :]
                sg = jax.nn.sigmoid(acc)
                dc_ref[pl.ds(t0, ch), :] = dy_ref[pl.ds(t0, ch), :] * (sg * (1.0 + acc * (1.0 - sg)))
                return carry

            lax.fori_loop(0, t // ch, act_chunk, 0)
        else:
            dc_ref[...] = dy_ref[...]

        def dw_chunk(i, accs):
            t0 = pl.multiple_of(i * ch, ch)
            new = list(accs)
            for s in range(ch // 8):
                d = dc_ref[pl.ds(t0 + 8 * s, 8), :]
                for k in range(width):
                    q, r = divmod(k + off, 8)
                    new[k] = new[k] + d * sh_ref[r, pl.ds(t0 + 8 * (q + s), 8), :]
                new[width] = new[width] + d
            return tuple(new)

        accs = lax.fori_loop(0, t // ch, dw_chunk, tuple(jnp.zeros((8, cb), F32) for _ in range(width + 1)))
        for k in range(width):
            dw_ref[pl.ds(k, 1), :] = jnp.sum(accs[k], axis=0, keepdims=True)
        db_ref[...] = jnp.sum(accs[width], axis=0, keepdims=True)

        xpad_ref[pl.ds(0, t), :] = dc_ref[...]
        xpad_ref[pl.ds(t, 40), :] = jnp.zeros((40, cb), F32)
        _fill_shifts(xpad_ref, sh_ref, t, shifts_t)

        def dx_chunk(i, carry):
            t0 = pl.multiple_of(i * ch, ch)
            acc = jnp.zeros((ch, cb), F32)
            for mm in range(width):
                q, r = divmod(mm, 8)
                acc = acc + w_ref[pl.ds(width - 1 - mm, 1), :] * sh_ref[r, pl.ds(t0 + 8 * q, ch), :]
            if glu:
                val, gate = x_refs[0][pl.ds(t0, ch), :], x_refs[1][pl.ds(t0, ch), :]
                sg = jax.nn.sigmoid(gate)
                dx_refs[0][pl.ds(t0, ch), :] = (acc * sg).astype(BF16)
                dx_refs[1][pl.ds(t0, ch), :] = (acc * val * sg * (1.0 - sg)).astype(BF16)
            else:
                dx_refs[0][pl.ds(t0, ch), :] = acc.astype(BF16)
            return carry

        lax.fori_loop(0, t // ch, dx_chunk, 0)

    col = pl.BlockSpec((t, cb), lambda i: (0, i))
    return _pallas(
        body, name=name, grid=(c // cb,),
        in_specs=[pl.BlockSpec((t, cb), lambda i, o=o: (0, o + i)) for _, o in xs]
        + [pl.BlockSpec((width, cb), lambda i: (0, i)), pl.BlockSpec((1, cb), lambda i: (0, i)), col],
        out_specs=[col] * nx + [pl.BlockSpec((width, cb), lambda i: (0, i)), pl.BlockSpec((1, cb), lambda i: (0, i))],
        out_shape=[SDS((t, c), BF16)] * nx + [SDS((width, c), F32), SDS((1, c), F32)],
        scratch_shapes=[pltpu.VMEM((t + 40, cb), F32), pltpu.VMEM((8, t + 32, cb), F32), pltpu.VMEM((t, cb), F32)],
        compiler_params=_params(("parallel",)),
    )(*[x for x, _ in xs], w, b, dy)


_DIMS = {"nn": (((1,), (0,)), ((), ())), "nt": (((1,), (1,)), ((), ())), "tn": (((0,), (0,)), ((), ()))}


def _raw_dot(a, b, mode):
    return lax.dot_general(a.astype(BF16), b.astype(BF16), _DIMS[mode], preferred_element_type=F32)


@functools.partial(jax.custom_vjp, nondiff_argnums=(2,))
def _bdot(a, b, mode):
    return _raw_dot(a, b, mode)


def _bdot_fwd(a, b, mode):
    return _raw_dot(a, b, mode), (a, b)


def _bdot_bwd(mode, res, g):
    a, b = res
    if mode == "nn":
        return _raw_dot(g, b, "nt"), _raw_dot(a, g, "tn")
    if mode == "nt":
        return _raw_dot(g, b, "nn"), _raw_dot(g, a, "tn")
    return _raw_dot(b, g, "nt"), _raw_dot(a, g, "nn")


_bdot.defvjp(_bdot_fwd, _bdot_bwd)


def _iota(shape, axis):
    return lax.broadcasted_iota(jnp.int32, shape, axis)


def _half_masks():
    left = (_iota((1, LANE), 1) < 64).astype(F32)
    return left, 1.0 - left


def _split3(a):
    a1 = a.astype(BF16)
    r1 = a - a1.astype(F32)
    a2 = r1.astype(BF16)
    return a1, a2, (r1 - a2.astype(F32)).astype(BF16)


def _exact_dot(a, e, mode):
    return sum(lax.dot_general(piece, e, _DIMS[mode], preferred_element_type=F32) for piece in _split3(a))


@jax.custom_vjp
def _spread(a, e):
    return _exact_dot(a, e, "nn")


_spread.defvjp(lambda a, e: (_exact_dot(a, e, "nn"), e), lambda e, g: (_exact_dot(g, e, "nt"), jnp.zeros_like(e)))


@jax.custom_vjp
def _running_sum(tri, a):
    return sum(lax.dot_general(tri, piece, _DIMS["nn"], preferred_element_type=F32) for piece in _split3(a))


_running_sum.defvjp(
    lambda tri, a: (sum(lax.dot_general(tri, piece, _DIMS["nn"], preferred_element_type=F32) for piece in _split3(a)), tri),
    lambda tri, g: (jnp.zeros_like(tri), sum(lax.dot_general(tri, piece, _DIMS["tn"], preferred_element_type=F32)
                                             for piece in _split3(g))))


def _ssd_chunk(state, xa, dtr, z, dtb, alog, dskf, ng):
    xs, bm, cm = xa[:, :D], xa[:, D:D + 256], xa[:, D + 256:]
    left, right = _half_masks()
    expand = (_iota((LANE, D), 1) // 64 == _iota((LANE, D), 0)).astype(BF16)
    li, si = _iota((CHUNK, CHUNK), 0), _iota((CHUNK, CHUNK), 1)
    tril = li >= si
    dt16 = jax.nn.softplus(dtr + dtb)
    adt = dt16 * (-jnp.exp(alog))
    dtf = _spread(dt16, expand)
    cs16 = _running_sum(tril.astype(BF16), adt)
    csf = _spread(cs16, expand)
    totf = jnp.sum(jnp.where(_iota((CHUNK, D), 0) == CHUNK - 1, csf, 0.0), axis=0, keepdims=True)
    cst = cs16.T
    xdt = xs * dtf
    ys, new_state = [], []
    for g in range(2):
        bg, cg = bm[:, LANE * g:LANE * (g + 1)], cm[:, LANE * g:LANE * (g + 1)]
        cb = _bdot(cg, bg, "nt")
        for q in range(4):
            pr = 4 * g + q
            decay = []
            for h in (2 * pr, 2 * pr + 1):
                col = jnp.sum(jnp.where(si == h, cs16, 0.0), axis=1, keepdims=True)
                row = jnp.sum(jnp.where(li == h, cst, 0.0), axis=0, keepdims=True)
                decay.append(cb * jnp.exp(jnp.where(tril, col - row, -jnp.inf)))
            xp = xdt[:, LANE * pr:LANE * (pr + 1)]
            y_diag = _bdot(jnp.concatenate(decay, axis=1), jnp.concatenate([xp * left, xp * right], axis=0), "nn")
            csb, tot = csf[:, LANE * pr:LANE * (pr + 1)], totf[:, LANE * pr:LANE * (pr + 1)]
            ys.append(y_diag + _bdot(cg, state[pr], "nn") * jnp.exp(csb))
            new_state.append(state[pr] * jnp.exp(tot) + _bdot(bg, xp * jnp.exp(tot - csb), "tn"))
    y = jnp.concatenate(ys, axis=1)
    y = y + dskf * xs
    y = y * (z * jax.nn.sigmoid(z))
    halves = []
    for g in range(2):
        yg = y[:, 512 * g:512 * (g + 1)]
        halves.append(yg * lax.rsqrt(jnp.mean(yg * yg, axis=-1, keepdims=True) + EPS))
    return jnp.concatenate(halves, axis=1) * ng, jnp.stack(new_state, axis=0)


def _ssd_specs(t, rev):
    nc = t // CHUNK
    ix = (lambda c: nc - 1 - c) if rev else (lambda c: c)
    return nc, ix


def _ssd_fwd(xa, proj, dtb, alog, dsk, ng, name):
    t = xa.shape[0]
    nc, ix = _ssd_specs(t, False)

    def body(xa_ref, dt_ref, z_ref, dtb_ref, alog_ref, dsk_ref, ng_ref, y_ref, st_ref, carry_ref):
        @pl.when(pl.program_id(0) == 0)
        def _():
            carry_ref[...] = jnp.zeros_like(carry_ref)

        st_ref[...] = carry_ref[...]
        y, new = _ssd_chunk(carry_ref[...], xa_ref[...], dt_ref[...], z_ref[...], dtb_ref[...], alog_ref[...],
                            dsk_ref[...], ng_ref[...])
        y_ref[...] = y.astype(BF16)
        carry_ref[...] = new

    small = [dtb, alog, dsk, ng]
    return _pallas(
        body, name=name, grid=(nc,),
        in_specs=[pl.BlockSpec((CHUNK, SSM_XBC), lambda c: (c, 0)),
                  pl.BlockSpec((CHUNK, LANE), lambda c: (c, DT_COL // LANE)),
                  pl.BlockSpec((CHUNK, D), lambda c: (c, 2))] + [_whole(p) for p in small],
        out_specs=[pl.BlockSpec((CHUNK, D), lambda c: (c, 0)), pl.BlockSpec((None, N_PAIR, LANE, LANE), lambda c: (c, 0, 0, 0))],
        out_shape=[SDS((t, D), BF16), SDS((nc, N_PAIR, LANE, LANE), F32)],
        scratch_shapes=[pltpu.VMEM((N_PAIR, LANE, LANE), F32)],
        compiler_params=_params(("arbitrary",)),
    )(xa, proj, proj, *small)


def _ssd_bwd(xa, proj, states, dy, dtb, alog, dsk, ng, name):
    t = xa.shape[0]
    nc, ix = _ssd_specs(t, True)

    def body(xa_ref, dt_ref, z_ref, st_ref, dy_ref, dtb_ref, alog_ref, dsk_ref, ng_ref,
             dxa_ref, ddt_ref, dz_ref, gdtb_ref, galog_ref, gdsk_ref, gng_ref, carry_ref):
        first = pl.program_id(0) == 0

        @pl.when(first)
        def _():
            carry_ref[...] = jnp.zeros_like(carry_ref)

        args = (st_ref[...], xa_ref[...], dt_ref[...], z_ref[...], dtb_ref[...], alog_ref[...], dsk_ref[...], ng_ref[...])
        _, vjp = jax.vjp(_ssd_chunk, *args)
        ds, dxa, ddt, dz, gdtb, galog, gdsk, gng = vjp((dy_ref[...], carry_ref[...]))
        carry_ref[...] = ds
        dxa_ref[...] = dxa
        ddt_ref[...] = ddt.astype(BF16)
        dz_ref[...] = dz.astype(BF16)
        for r, g in ((gdtb_ref, gdtb), (galog_ref, galog), (gdsk_ref, gdsk), (gng_ref, gng)):
            @pl.when(first)
            def _(r=r, g=g):
                r[...] = g

            @pl.when(jnp.logical_not(first))
            def _(r=r, g=g):
                r[...] += g

    small = [dtb, alog, dsk, ng]
    return _pallas(
        body, name=name, grid=(nc,),
        in_specs=[pl.BlockSpec((CHUNK, SSM_XBC), lambda c: (ix(c), 0)),
                  pl.BlockSpec((CHUNK, LANE), lambda c: (ix(c), DT_COL // LANE)),
                  pl.BlockSpec((CHUNK, D), lambda c: (ix(c), 2)),
                  pl.BlockSpec((None, N_PAIR, LANE, LANE), lambda c: (ix(c), 0, 0, 0)),
                  pl.BlockSpec((CHUNK, D), lambda c: (ix(c), 0))] + [_whole(p) for p in small],
        out_specs=[pl.BlockSpec((CHUNK, SSM_XBC), lambda c: (ix(c), 0)), pl.BlockSpec((CHUNK, LANE), lambda c: (ix(c), 0)),
                   pl.BlockSpec((CHUNK, D), lambda c: (ix(c), 0))] + [_whole(p) for p in small],
        out_shape=[SDS((t, SSM_XBC), F32), SDS((t, LANE), BF16), SDS((t, D), BF16)] + [SDS(p.shape, F32) for p in small],
        scratch_shapes=[pltpu.VMEM((N_PAIR, LANE, LANE), F32)],
        compiler_params=_params(("arbitrary",)),
    )(xa, proj, proj, states, dy, *small)


def _attn_block(q, kv_prev, kv_cur, cq, sq, ck, sk, sinks, rot, first_block):
    left, right = _half_masks()
    k2 = jnp.concatenate([kv_prev[:, :256], kv_cur[:, :256]], axis=0)
    v2 = jnp.concatenate([kv_prev[:, 256:], kv_cur[:, 256:]], axis=0)
    ri, ci = _iota((LANE, LANE), 0), _iota((LANE, LANE), 1)
    dup = [((ri < 64) & (ci % 64 == ri)).astype(BF16), ((ri >= 64) & (ci % 64 == ri - 64)).astype(BF16)]

    rot16 = rot.astype(BF16)

    def rope(tt, c, s):
        return tt * c + _spread(tt, rot16) * s

    kd, vd = [], []
    for j in range(4):
        sl = slice(LANE * (j // 2), LANE * (j // 2 + 1))
        kd.append(_bdot(rope(k2[:, sl], ck, sk), dup[j % 2], "nn"))
        vd.append(_bdot(v2[:, sl], dup[j % 2], "nn"))
    qi, si = _iota((2 * CHUNK, 2 * CHUNK), 0) % CHUNK, _iota((2 * CHUNK, 2 * CHUNK), 1)
    valid = (si > qi) & (si <= qi + CHUNK) & jnp.logical_or(si >= CHUNK, jnp.logical_not(first_block))
    upper = _iota((2 * CHUNK, 1), 0) < CHUNK
    lanes = _iota((1, LANE), 1)
    outs = []
    for pr in range(N_PAIR):
        qr = rope(q[:, LANE * pr:LANE * (pr + 1)], cq, sq)
        lg = _bdot(jnp.concatenate([qr * left, qr * right], axis=0), kd[pr // 2], "nt") * 0.125
        lg = jnp.where(valid, lg, -jnp.inf)
        s1 = jnp.sum(jnp.where(lanes == 2 * pr, sinks, 0.0), axis=1, keepdims=True)
        s2 = jnp.sum(jnp.where(lanes == 2 * pr + 1, sinks, 0.0), axis=1, keepdims=True)
        sink = jnp.where(upper, s1, s2)
        mx = lax.stop_gradient(jnp.maximum(jnp.max(lg, axis=-1, keepdims=True), sink))
        e = jnp.exp(lg - mx)
        probs = e / (jnp.sum(e, axis=-1, keepdims=True) + jnp.exp(sink - mx))
        o2 = _bdot(probs, vd[pr // 2], "nn")
        outs.append(o2[:CHUNK] * left + o2[CHUNK:] * right)
    return jnp.concatenate(outs, axis=1)


def _attn_fwd(qkv, cos, sin, sinks, rot, name):
    t = qkv.shape[0]
    nb = t // CHUNK

    def body(q_ref, kvp_ref, kvc_ref, cq_ref, sq_ref, cp_ref, sp_ref, sinks_ref, rot_ref, o_ref):
        ck = jnp.concatenate([cp_ref[...], cq_ref[...]], axis=0)
        sk = jnp.concatenate([sp_ref[...], sq_ref[...]], axis=0)
        o_ref[...] = _attn_block(q_ref[...], kvp_ref[...], kvc_ref[...], cq_ref[...], sq_ref[...], ck, sk,
                                 sinks_ref[...], rot_ref[...], pl.program_id(0) == 0).astype(BF16)

    prev = lambda n: jnp.maximum(n - 1, 0)
    return _pallas(
        body, name=name, grid=(nb,),
        in_specs=[pl.BlockSpec((CHUNK, D), lambda n: (n, 0)),
                  pl.BlockSpec((CHUNK, 512), lambda n: (prev(n), 2)), pl.BlockSpec((CHUNK, 512), lambda n: (n, 2)),
                  pl.BlockSpec((CHUNK, LANE), lambda n: (n, 0)), pl.BlockSpec((CHUNK, LANE), lambda n: (n, 0)),
                  pl.BlockSpec((CHUNK, LANE), lambda n: (prev(n), 0)), pl.BlockSpec((CHUNK, LANE), lambda n: (prev(n), 0)),
                  _whole(sinks), _whole(rot)],
        out_specs=pl.BlockSpec((CHUNK, D), lambda n: (n, 0)), out_shape=SDS((t, D), BF16),
        compiler_params=_params(("parallel",)),
    )(qkv, qkv, qkv, cos, sin, cos, sin, sinks, rot)


def _attn_bwd(qkv, do, cos, sin, sinks, rot, name):
    t = qkv.shape[0]
    nb = t // CHUNK

    def body(q_ref, kvp_ref, kvc_ref, do_ref, cq_ref, sq_ref, cp_ref, sp_ref, sinks_ref, rot_ref,
             dq_ref, dkv_ref, dbq_ref, dbkv_ref, dsink_ref, carry_ref):
        n = pl.program_id(0)

        @pl.when(n == 0)
        def _():
            carry_ref[...] = jnp.zeros_like(carry_ref)
            dbq_ref[...] = jnp.zeros_like(dbq_ref)
            dbkv_ref[...] = jnp.zeros_like(dbkv_ref)
            dsink_ref[...] = jnp.zeros_like(dsink_ref)

        @pl.when(n < nb)
        def _():
            ck = jnp.concatenate([cp_ref[...], cq_ref[...]], axis=0)
            sk = jnp.concatenate([sp_ref[...], sq_ref[...]], axis=0)
            f = lambda q, kvp, kvc, s: _attn_block(q, kvp, kvc, cq_ref[...], sq_ref[...], ck, sk, s, rot_ref[...], n == 0)
            _, vjp = jax.vjp(f, q_ref[...], kvp_ref[...], kvc_ref[...], sinks_ref[...])
            dq, dkvp, dkvc, ds = vjp(do_ref[...].astype(F32))
            done = carry_ref[...] + dkvp
            dq_ref[...] = dq.astype(BF16)
            dkv_ref[...] = done.astype(BF16)
            dbq_ref[...] += jnp.sum(dq, axis=0, keepdims=True)
            dsink_ref[...] += ds
            carry_ref[...] = dkvc

            @pl.when(n > 0)
            def _():
                dbkv_ref[...] += jnp.sum(done, axis=0, keepdims=True)

        @pl.when(n == nb)
        def _():
            done = carry_ref[...]
            dkv_ref[...] = done.astype(BF16)
            dbkv_ref[...] += jnp.sum(done, axis=0, keepdims=True)

    cur = lambda n: jnp.minimum(n, nb - 1)
    prev = lambda n: jnp.maximum(jnp.minimum(n, nb - 1) - 1, 0)
    fin = lambda n: jnp.maximum(n - 1, 0)
    outs = _pallas(
        body, name=name, grid=(nb + 1,),
        in_specs=[pl.BlockSpec((CHUNK, D), lambda n: (cur(n), 0)),
                  pl.BlockSpec((CHUNK, 512), lambda n: (prev(n), 2)), pl.BlockSpec((CHUNK, 512), lambda n: (cur(n), 2)),
                  pl.BlockSpec((CHUNK, D), lambda n: (cur(n), 0)),
                  pl.BlockSpec((CHUNK, LANE), lambda n: (cur(n), 0)), pl.BlockSpec((CHUNK, LANE), lambda n: (cur(n), 0)),
                  pl.BlockSpec((CHUNK, LANE), lambda n: (prev(n), 0)), pl.BlockSpec((CHUNK, LANE), lambda n: (prev(n), 0)),
                  _whole(sinks), _whole(rot)],
        out_specs=[pl.BlockSpec((CHUNK, D), lambda n: (cur(n), 0)), pl.BlockSpec((CHUNK, 512), lambda n: (fin(n), 0)),
                   pl.BlockSpec((1, D), lambda n: (0, 0)), pl.BlockSpec((1, 512), lambda n: (0, 0)), _whole(sinks)],
        out_shape=[SDS((t, D), BF16), SDS((t, 512), BF16), SDS((1, D), F32), SDS((1, 512), F32), SDS(sinks.shape, F32)],
        scratch_shapes=[pltpu.VMEM((CHUNK, 512), F32)],
        compiler_params=_params(("arbitrary",)),
    )(qkv, qkv, qkv, do, cos, sin, cos, sin, sinks, rot)
    dq, dkv, dbq, dbkv, dsinks = outs
    return jnp.concatenate([dq, dkv], axis=1), jnp.concatenate([dbq, dbkv], axis=1), dsinks


def _loss_head(h, tgt, g, name, tm=512):
    t = h.shape[0]

    def body(h_ref, t_ref, g_ref, loss_ref, dh_ref, dhb_ref, dg_ref):
        def f(hv, gv):
            err = _rms_fn(hv, gv)[0] - t_ref[...]
            return 0.5 * jnp.sum(jnp.mean(err * err, axis=-1, keepdims=True), axis=0, keepdims=True)

        loss, vjp = jax.vjp(f, h_ref[...], g_ref[...])
        dh, dg = vjp(jnp.ones((1, 1), F32))
        dh_ref[...] = dh
        dhb_ref[...] = dh.astype(BF16)
        first = pl.program_id(0) == 0

        @pl.when(first)
        def _():
            loss_ref[...] = loss
            dg_ref[...] = dg

        @pl.when(jnp.logical_not(first))
        def _():
            loss_ref[...] += loss
            dg_ref[...] += dg

    return _pallas(
        body, name=name, grid=(t // tm,),
        in_specs=[_tok(D, tm), _tok(D, tm), _whole(g)],
        out_specs=[pl.BlockSpec((1, 1), lambda i: (0, 0)), _tok(D, tm), _tok(D, tm), _whole(g)],
        out_shape=[SDS((1, 1), F32), SDS((t, D), F32), SDS((t, D), BF16), SDS(g.shape, F32)],
        compiler_params=_params(("arbitrary",)),
    )(h, tgt, g)


def _res_half(acc, res):
    return (res + 0.5 * acc,)


def _res_full(acc, res):
    return (res + acc,)


def _half(acc):
    return (0.5 * acc,)


def _ffn_in(n, w_in, name, tm=1024):
    t = n.shape[0]
    tm = min(tm, t)

    def body(n_ref, w_ref, pre_ref, act_ref):
        a = n_ref[...]
        gate = lax.dot_general(a, w_ref[0], _DIMS["nt"], preferred_element_type=F32)
        up = lax.dot_general(a, w_ref[1], _DIMS["nt"], preferred_element_type=F32)
        pre_ref[0] = gate.astype(BF16)
        pre_ref[1] = up.astype(BF16)
        act_ref[...] = (gate * jax.nn.sigmoid(gate) * up).astype(BF16)

    pair = pl.BlockSpec((2, None, tm, FF_SHARD), lambda i, j: (0, j, i, 0))
    return _pallas(
        body, name=name, grid=(t // tm, 4),
        in_specs=[pl.BlockSpec((tm, D), lambda i, j: (i, 0)), pl.BlockSpec((2, None, FF_SHARD, D), lambda i, j: (0, j, 0, 0))],
        out_specs=[pair, pl.BlockSpec((None, tm, FF_SHARD), lambda i, j: (j, i, 0))],
        out_shape=[SDS((2, 4, t, FF_SHARD), BF16), SDS((4, t, FF_SHARD), BF16)],
        compiler_params=_params(("parallel", "parallel")),
    )(n, w_in.reshape(2, 4, FF_SHARD, D))


def _ffn_dact(dhb, w_out, pre, name, tm=1024, deps=()):
    t = dhb.shape[0]
    tm = min(tm, t)

    def body(d_ref, w_ref, pre_ref, *rest):
        o_ref = rest[-1]
        dact = 0.5 * lax.dot_general(d_ref[...], w_ref[...], _DIMS["nt"], preferred_element_type=F32)
        gate, up = pre_ref[0].astype(F32), pre_ref[1].astype(F32)
        sg = jax.nn.sigmoid(gate)
        o_ref[0] = (dact * up * (sg * (1.0 + gate * (1.0 - sg)))).astype(BF16)
        o_ref[1] = (dact * (gate * sg)).astype(BF16)

    pair = pl.BlockSpec((2, None, tm, FF_SHARD), lambda i, j: (0, j, i, 0))
    return _pallas(
        body, name=name, grid=(t // tm, 4),
        in_specs=[pl.BlockSpec((tm, D), lambda i, j: (i, 0)), pl.BlockSpec((None, FF_SHARD, D), lambda i, j: (j, 0, 0)), pair]
        + [ANY_SPEC] * len(deps),
        out_specs=pair, out_shape=SDS((2, 4, t, FF_SHARD), BF16),
        compiler_params=_params(("parallel", "parallel")),
    )(dhb, w_out, pre, *deps)


def _ffn_fwd(h, g, w_in, w_out, tag, deps=()):
    n = _rms(h, g, f"{tag}_rms", deps=deps)
    pre, act = _ffn_in(n, w_in, f"{tag}_in")
    w_out = w_out() if callable(w_out) else w_out
    out = _mm(act, w_out, reduce_j=True, tk=FF_SHARD, epi=_res_half, extras=(h[None],), name=f"{tag}_out")[0][0]
    return out, (h, n, pre, act)


def _ffn_bwd(dh, dhb, saved, g, w_in, w_out, tag, deps=(), hook=None, weights_hook=None):
    h, n, pre, act = saved
    t = h.shape[0]
    dpre = _ffn_dact(dhb, w_out, pre, f"{tag}_dact", deps=deps).reshape(N_DEV, t, FF_SHARD)
    dw_out = _mm(act, dhb[None], ta=True, tm=FF_SHARD, epi=_half, out_dtypes=(BF16,), deps=hook(dpre) if hook else (),
                 name=f"{tag}_dwout")[0]
    dw_in = _mm(dpre, n[None], ta=True, tm=FF_SHARD, tk=2048, out_dtypes=(BF16,), name=f"{tag}_dwin")[0]
    if weights_hook:
        weights_hook(dw_in, dw_out)
    dh_in, dhb_in, dg = _mm_drms(dpre, w_in, h, g, dh, f"{tag}_dn", FF_SHARD, tb=False)
    return dh_in, dhb_in, dg, dw_in, dw_out


def _ple_fwd(h, g, pb, w_gate, w_proj, tag):
    t = h.shape[0]
    tm = 512
    n = _rms(h, g, f"{tag}_rms")
    e = _mm(pb[None], w_proj[None], name=f"{tag}_proj")[0][0]
    z = _mm(n[None], w_gate[None], name=f"{tag}_gate")[0][0]
    out = _rowop(lambda zz, ee, hh: (hh + _ple_fn(zz, ee)[0],), [(z, _tok(D, tm)), (e, _tok(D, tm)), (h, _tok(D, tm))], [],
                 [((t, D), F32, _tok(D, tm))], grid=(t // tm,), name=f"{tag}_mix")[0]
    return out, (h, n, e, z)


def _ple_bwd(dh, dhb, saved, g, pb, w_gate, tag, deps=()):
    h, n, e, z = saved
    t = h.shape[0]
    tm = 512
    dz, de = _rowop_bwd(_ple_fn, [(z, _tok(D, tm)), (e, _tok(D, tm))], [], [(dh, _tok(D, tm))], [(0,), (1,)],
                        [((t, D), (BF16,), _tok(D, tm)), ((t, D), (BF16,), _tok(D, tm))], grid=(t // tm,), name=f"{tag}_dmix",
                        deps=deps)
    dw_proj = _mm(pb[None], de[None], ta=True, out_dtypes=(BF16,), name=f"{tag}_dwproj")[0][0]
    dw_gate = _mm(n[None], dz[None], ta=True, out_dtypes=(BF16,), name=f"{tag}_dwgate")[0][0]
    dh_in, dhb_in, dg = _mm_drms(dz[None], w_gate[None], h, g, dh, f"{tag}_dn", 1024)
    return dh_in, dhb_in, dg, dw_gate, dw_proj


def _hyb_fwd(h, w, tag, after_in=None):
    t = h.shape[0]
    tm = 512
    hn = _rms(h, w["norm_mix"], f"{tag}_rms")
    proj = _mm(hn[None], w["hyb_in"][None], tn=512, name=f"{tag}_in")[0][0]
    if after_in:
        after_in()
    u1 = _dwconv([(proj, 0), (proj, D // LANE)], w["conv_w"], w["conv_b"], width=CONV_W, glu=True, silu=False, cb=LANE,
                 name=f"{tag}_conv")
    u = _rowop(_lnswish_fn, [(u1, _tok(D, tm))], [w["ln_g"], w["ln_b"]], [((t, D), BF16, _tok(D, tm))], grid=(t // tm,),
               name=f"{tag}_ln")[0]
    xa = _dwconv([(proj, 3 * D // LANE)], w["sconv_w"], w["sconv_b"], width=SSM_CONV, glu=False, silu=True, cb=LANE,
                 name=f"{tag}_sconv")
    y, states = _ssd_fwd(xa, proj, w["dt_bias"], w["a_log"], w["d_skip"], w["ssm_norm"], f"{tag}_ssd")
    mixed = jnp.stack([u, y], axis=0)
    out = _mm(mixed, w["hyb_out"], reduce_j=True, epi=_res_full, extras=(h[None],), name=f"{tag}_out")[0][0]
    return out, (h, hn, proj, u1, xa, states, mixed)


def _hyb_bwd(dh, dhb, saved, w, tag):
    h, hn, proj, u1, xa, states, mixed = saved
    t = h.shape[0]
    tm = 512
    dmix = _mm(dhb[None], w["hyb_out"], tb=True, name=f"{tag}_dmix")[0]
    dw_out = _mm(mixed, dhb[None], ta=True, out_dtypes=(BF16,), name=f"{tag}_dwout")[0]
    du1, dln_g, dln_b = _rowop_bwd(_lnswish_fn, [(u1, _tok(D, tm))], [w["ln_g"], w["ln_b"]], [(dmix[0], _tok(D, tm))], [(0,)],
                                   [((t, D), (F32,), _tok(D, tm))], grid=(t // tm,), name=f"{tag}_dln")
    dval, dgate, dconv_w, dconv_b = _dwconv_bwd([(proj, 0), (proj, D // LANE)], w["conv_w"], w["conv_b"], du1,
                                                width=CONV_W, glu=True, silu=False, cb=LANE, name=f"{tag}_dconv")
    dxa, ddt, dz, g_dtb, g_alog, g_dsk, g_ng = _ssd_bwd(xa, proj, states, dmix[1], w["dt_bias"], w["a_log"], w["d_skip"],
                                                         w["ssm_norm"], f"{tag}_dssd")
    dxbc, dsconv_w, dsconv_b = _dwconv_bwd([(proj, 3 * D // LANE)], w["sconv_w"], w["sconv_b"], dxa, width=SSM_CONV,
                                           glu=False, silu=True, cb=LANE, name=f"{tag}_dsconv")
    dproj = jnp.concatenate([dval, dgate, dz, dxbc, ddt, jnp.zeros((t, HYB_PAD - DT_COL - LANE), BF16)], axis=1)
    dh_in, dhb_in, dg = _mm_drms(dproj[None], w["hyb_in"][None], h, w["norm_mix"], dh, f"{tag}_dhn", 1024)
    dw_in = _mm(hn[None], dproj[None], ta=True, tn=512, out_dtypes=(BF16,), name=f"{tag}_dwin")[0][0]
    grads = dict(norm_mix=dg, hyb_in=dw_in, hyb_out=dw_out, conv_w=dconv_w, conv_b=dconv_b, ln_g=dln_g, ln_b=dln_b,
                 sconv_w=dsconv_w, sconv_b=dsconv_b, dt_bias=g_dtb, a_log=g_alog, d_skip=g_dsk, ssm_norm=g_ng)
    return dh_in, dhb_in, grads


def _bias_epi(acc, row):
    return (acc + row,)


def _res_bias_epi(acc, res, row):
    return (res + acc + row,)


def _att_fwd(h, w, tables, tag):
    cos, sin, rot = tables
    hn = _rms(h, w["norm_mix"], f"{tag}_rms")
    qkv = _mm(hn[None], w["qkv"][None], tb=True, tn=512, epi=_bias_epi, rows=(w["b_qkv"],), name=f"{tag}_qkv")[0][0]
    o = _attn_fwd(qkv, cos, sin, w["sinks"], rot, f"{tag}_core")
    out = _mm(o[None], w["w_o"][None], epi=_res_bias_epi, extras=(h[None],), rows=(w["b_o"],), name=f"{tag}_out")[0][0]
    return out, (h, hn, qkv, o)


def _att_bwd(dh, dhb, saved, w, tables, tag):
    cos, sin, rot = tables
    h, hn, qkv, o = saved
    t = h.shape[0]
    tm = 512
    do = _mm(dhb[None], w["w_o"][None], tb=True, out_dtypes=(BF16,), name=f"{tag}_do")[0][0]
    dw_o = _mm(o[None], dhb[None], ta=True, out_dtypes=(BF16,), name=f"{tag}_dwo")[0][0]
    db_o = _rowop_bwd(lambda xx, bb: (xx + bb,), [(dh, _tok(D, tm))], [w["b_o"]], [(dh, _tok(D, tm))], [], [],
                      grid=(t // tm,), name=f"{tag}_dbo")[0]
    dqkv, db_qkv, dsinks = _attn_bwd(qkv, do, cos, sin, w["sinks"], rot, f"{tag}_dcore")
    dh_in, dhb_in, dg = _mm_drms(dqkv[None], w["qkv"][None], h, w["norm_mix"], dh, f"{tag}_dhn", 512, tb=False)
    dw_qkv = _mm(dqkv[None], hn[None], ta=True, tm=512, out_dtypes=(BF16,), name=f"{tag}_dwqkv")[0][0]
    grads = dict(norm_mix=dg, qkv=dw_qkv, b_qkv=db_qkv, sinks=dsinks, w_o=dw_o, b_o=db_o)
    return dh_in, dhb_in, grads


def _rope_tables(t):
    inv = ROPE_THETA ** (-jnp.arange(0, 64, 2, dtype=F32) / 64)
    ang = jnp.arange(t, dtype=F32)[:, None] * inv[None, :]
    cos, sin = jnp.tile(jnp.cos(ang), (1, 4)), jnp.tile(jnp.sin(ang), (1, 4))
    rot = np.zeros((LANE, LANE), np.float32)
    for j in range(LANE):
        if j % 64 < 32:
            rot[j + 32, j] = -1.0
        else:
            rot[j - 32, j] = 1.0
    return cos, sin, jnp.asarray(rot)


def _local_step(x, p, tgt, layers, final_norm):
    _restart_chain()
    tables = _rope_tables(x.shape[0])
    pb = p.astype(BF16)
    h, saved = x, []
    for i, w in enumerate(layers):
        h, s = _layer_fwd(i, h, w, pb[i], tables)
        saved.append(s)
    loss, dh, dhb, d_final = _loss_head(h, tgt, final_norm, "loss_head")
    grads = [None] * len(layers)
    for i in reversed(range(len(layers))):
        dh, dhb, head = _layer_bwd_head(i, dh, dhb, saved[i], layers[i], pb[i])
        dh, dhb, tail = _layer_bwd_tail(i, dh, dhb, saved[i], layers[i], tables)
        grads[i] = {**head, **tail}
    return loss[0, 0], dh, grads, d_final


def _layer_fwd(i, h, w, pb, tables, deps=()):
    s = {}
    h, s["ffn1"] = _ffn_fwd(h, w["norm_ffn1"], w["ffn1_in"], w["ffn1_out"], f"l{i}_ffn1", deps=deps)
    if i % 2 == 0:
        h, s["mix"] = _hyb_fwd(h, w, f"l{i}_hyb")
    else:
        h, s["mix"] = _att_fwd(h, w, tables, f"l{i}_att")
    h, s["ffn2"] = _ffn_fwd(h, w["norm_ffn2"], w["ffn2_in"], w["ffn2_out"], f"l{i}_ffn2")
    h, s["ple"] = _ple_fwd(h, w["ple_norm"], pb, w["ple_gate"], w["ple_proj"], f"l{i}_ple")
    return h, s


def _layer_bwd_head(i, dh, dhb, s, w, pb, deps=()):
    g = {}
    dh, dhb, g["ple_norm"], g["ple_gate"], g["ple_proj"] = _ple_bwd(dh, dhb, s["ple"], w["ple_norm"], pb, w["ple_gate"],
                                                                    f"l{i}_ple", deps=deps)
    return dh, dhb, g


def _layer_bwd_tail(i, dh, dhb, s, w, tables, deps=()):
    g = {}
    dh, dhb, g["norm_ffn2"], g["ffn2_in"], g["ffn2_out"] = _ffn_bwd(dh, dhb, s["ffn2"], w["norm_ffn2"], w["ffn2_in"],
                                                                    w["ffn2_out"], f"l{i}_ffn2", deps=deps)
    if i % 2 == 0:
        dh, dhb, gm = _hyb_bwd(dh, dhb, s["mix"], w, f"l{i}_hyb")
    else:
        dh, dhb, gm = _att_bwd(dh, dhb, s["mix"], w, tables, f"l{i}_att")
    g.update(gm)
    dh, dhb, g["norm_ffn1"], g["ffn1_in"], g["ffn1_out"] = _ffn_bwd(dh, dhb, s["ffn1"], w["norm_ffn1"], w["ffn1_in"],
                                                                    w["ffn1_out"], f"l{i}_ffn1")
    return dh, dhb, g


def _cols(g):
    full = jnp.moveaxis(g, 0, -2)
    return full.reshape(*full.shape[:-2], N_DEV * g.shape[-1])


def _uncols(full):
    split = full.reshape(*full.shape[:-1], N_DEV, full.shape[-1] // N_DEV)
    return jnp.moveaxis(split, -2, 0)


def _lane_pad(v):
    return jnp.pad(v, ((0, 0), (0, LANE - v.shape[1])))


def _build_layers(gw, gs, rep):
    return [_build_layer(i, gw, gs, rep) for i in range(2)]


def _build_layer(i, gw, gs, rep, parts=("ffn1", "mix", "ffn2", "ple")):
    w = {}
    for f in ("ffn1", "ffn2"):
        if f in parts:
            w[f"norm_{f}"] = rep[f"norm_{f}"][i][None]
            w[f"{f}_in"] = gw[f"{f}_w_in", i]
            w[f"{f}_out"] = gw[f"{f}_w_out", i].reshape(4, FF_SHARD, D)
    if "ple" in parts:
        w["ple_norm"] = rep["ple_norm"][i][None]
        w["ple_gate"] = gw["ple_gate_w", i].reshape(D, D)
        w["ple_proj"] = _cols(gw["ple_proj_w", i])
    if "mix" not in parts:
        return w
    w["norm_mix"] = rep["norm_mix"][i][None]
    if i == 0:
        w["hyb_in"] = jnp.pad(_cols(gw["hyb_w_in", 0]), ((0, 0), (0, HYB_PAD - HYB_IN)))
        w["hyb_out"] = gw["hyb_w_out", 0].reshape(2, D, D)
        w["conv_w"] = _cols(gs["conv_dw_w"][:, 0])
        w["sconv_w"] = _cols(gs["ssm_conv_w"][:, 0])
        w["conv_b"], w["ln_g"], w["ln_b"] = rep["conv_dw_b"], rep["conv_ln_g"], rep["conv_ln_b"]
        w["sconv_b"], w["ssm_norm"] = rep["ssm_conv_b"], rep["ssm_norm"]
        w["dt_bias"], w["a_log"] = _lane_pad(rep["ssm_dt_bias"]), _lane_pad(rep["ssm_a_log"])
        w["d_skip"] = jnp.repeat(rep["ssm_d"], D // SSM_HEADS, axis=1)
    else:
        w["qkv"] = gw["att_w_qkv", 0].reshape(-1, D)
        w["w_o"] = gw["att_w_o", 0].reshape(D, D)
        w["b_qkv"] = gs["att_b_qkv"][:, 0].reshape(1, -1)
        w["b_o"] = gs["att_b_o"][:, 0].reshape(1, -1)
        w["sinks"] = _lane_pad(rep["att_sinks"])
    return w


def _big_grads(i, g):
    big = {}
    for f in ("ffn1", "ffn2"):
        if f"{f}_in" in g:
            big[f"{f}_w_in", i] = g[f"{f}_in"]
            big[f"{f}_w_out", i] = g[f"{f}_out"].reshape(N_DEV, D_FF // N_DEV, D)
    if "ple_gate" in g:
        big["ple_gate_w", i] = g["ple_gate"].reshape(N_DEV, D // N_DEV, D)
        big["ple_proj_w", i] = _uncols(g["ple_proj"])
    if "hyb_in" in g:
        big["hyb_w_in", 0] = _uncols(g["hyb_in"][:, :HYB_IN])
        big["hyb_w_out", 0] = g["hyb_out"].reshape(N_DEV, 2 * D // N_DEV, D)
    if "qkv" in g:
        big["att_w_qkv", 0] = g["qkv"].reshape(N_DEV, -1, D)
        big["att_w_o", 0] = g["w_o"].reshape(N_DEV, D // N_DEV, D)
    return big


def _collect_grads(grads, d_final):
    g0, g1 = grads
    big, small = {**_big_grads(0, g0), **_big_grads(1, g1)}, {}
    for f in ("ffn1", "ffn2"):
        small[f"norm_{f}"] = jnp.concatenate([g[f"norm_{f}"] for g in grads], axis=0)
    small["norm_mix"] = jnp.concatenate([g["norm_mix"] for g in grads], axis=0)
    small["ple_norm"] = jnp.concatenate([g["ple_norm"] for g in grads], axis=0)
    small["conv_dw_w"] = g0["conv_w"][None]
    small["conv_dw_b"], small["conv_ln_g"], small["conv_ln_b"] = g0["conv_b"], g0["ln_g"], g0["ln_b"]
    small["ssm_conv_w"] = g0["sconv_w"][None]
    small["ssm_conv_b"], small["ssm_norm"] = g0["sconv_b"], g0["ssm_norm"]
    small["ssm_dt_bias"], small["ssm_a_log"] = g0["dt_bias"][:, :SSM_HEADS], g0["a_log"][:, :SSM_HEADS]
    small["ssm_d"] = g0["d_skip"].reshape(1, SSM_HEADS, D // SSM_HEADS).sum(axis=-1)
    small["att_b_qkv"], small["att_b_o"] = g1["b_qkv"], g1["b_o"]
    small["att_sinks"] = g1["sinks"][:, :SSM_HEADS]
    small["final_norm"] = d_final[0]
    return big, small


MESH = pl.DeviceIdType.MESH


def _place():
    return lax.axis_index("x"), lax.axis_index("y"), lax.axis_index("c")


def _all_gather(blocks, space, name):
    nb = len(blocks)

    def body(*refs):
        x_refs, out_refs, (send_sems, recv_sems, local_sem) = refs[:nb], refs[nb:2 * nb], refs[2 * nb:]
        x, y, c = _place()
        me, sibling = (x, y, c), (x, y, 1 - c)
        chips = [(1 - x, y), (x, 1 - y), (1 - x, 1 - y)]

        def copies(k, blk, to, own=False):
            idx = 4 * blk[0] + 2 * blk[1] + blk[2]
            return [pltpu.make_async_remote_copy(src_ref=x_ref if own else out_ref.at[idx], dst_ref=out_ref.at[idx],
                                                 send_sem=send_sems.at[k, b], recv_sem=recv_sems.at[k, b], device_id=to,
                                                 device_id_type=MESH) for b, (x_ref, out_ref) in enumerate(zip(x_refs, out_refs))]

        mine = [pltpu.make_async_copy(x_ref, out_ref.at[4 * x + 2 * y + c], local_sem.at[b])
                for b, (x_ref, out_ref) in enumerate(zip(x_refs, out_refs))]
        first = copies(0, me, sibling, own=True)
        for j, chip in enumerate(chips):
            first += copies(1 + j, me, (*chip, c), own=True)
        for cp in mine + first:
            cp.start()
        passed = []
        for j, chip in enumerate(chips):
            for cp in copies(1 + j, (*chip, c), me):
                cp.wait_recv()
            onward = copies(4 + j, (*chip, c), sibling)
            for cp in onward:
                cp.start()
            passed += onward
        for cp in copies(0, sibling, me):
            cp.wait_recv()
        for j, chip in enumerate(chips):
            for cp in copies(4 + j, (*chip, 1 - c), me):
                cp.wait_recv()
        for cp in first + passed:
            cp.wait_send()
        for cp in mine:
            cp.wait()

    spec = pl.BlockSpec(memory_space=space)
    return _pallas(
        body, name=name, out_shape=[SDS((N_DEV,) + b.shape, b.dtype) for b in blocks],
        in_specs=[spec] * nb, out_specs=[spec] * nb,
        scratch_shapes=[pltpu.SemaphoreType.DMA((7, nb)), pltpu.SemaphoreType.DMA((7, nb)), pltpu.SemaphoreType.DMA((nb,))],
    )(*blocks)


HBM_SPEC = pl.BlockSpec(memory_space=pltpu.HBM)
SEM_SPEC = pl.BlockSpec(memory_space=pltpu.SEMAPHORE)
EFFECT = pltpu.SideEffectType.DATAFLOW_SIDE_EFFECTING


def _plan_descriptors(plan, srcs, lands, send_sems, recv_sems, local_sems, arriving):
    remote, local = plan(*_place())

    def pick(si, slot):
        ref = lands[si[1]] if isinstance(si, tuple) else srcs[si]
        return ref if slot is None else ref.at[slot]

    rem = [pltpu.make_async_remote_copy(src_ref=pick(si, ss), dst_ref=lands[li].at[rs if arriving else ds],
                                        send_sem=send_sems.at[k], recv_sem=recv_sems.at[k], device_id=dev, device_id_type=MESH)
           for k, (si, ss, li, ds, dev, rs) in enumerate(remote)]
    loc = [pltpu.make_async_copy(pick(si, ss), lands[li].at[ds], local_sems.at[k])
           for k, (si, ss, li, ds) in enumerate(local)]
    return rem, loc


def _plan_counts(plan):
    remote, local = plan(0, 0, 0)
    return len(remote), max(len(local), 1)


def _exchange_start(srcs, land_shapes, plan, name, lands=None):
    ns, nl = len(srcs), len(lands if lands is not None else land_shapes)
    n_remote, n_local = _plan_counts(plan)
    if lands is None:
        lands = [pltpu.with_memory_space_constraint(lax.empty(s.shape, s.dtype), pltpu.HBM) for s in land_shapes]
    lands = list(lands)
    srcs = [pltpu.with_memory_space_constraint(s, pltpu.HBM) for s in srcs]

    def body(*refs):
        src_refs, land_refs = refs[:ns], refs[ns:ns + nl]
        send_sems, recv_sems, local_sems = refs[ns + nl:ns + nl + 3]
        token = refs[-1]
        rem, loc = _plan_descriptors(plan, src_refs, land_refs, send_sems, recv_sems, local_sems, arriving=False)
        for cp in loc + rem:
            cp.start()
        token[...] = jnp.zeros_like(token)

    outs = _pallas(
        body, name=name,
        out_shape=[pltpu.SemaphoreType.DMA((n_remote,)), pltpu.SemaphoreType.DMA((n_remote,)), pltpu.SemaphoreType.DMA((n_local,))]
        + [pltpu.HBM(a.shape, a.dtype) for a in srcs + lands] + [SDS((8, LANE), F32)],
        in_specs=[HBM_SPEC] * (ns + nl),
        out_specs=[SEM_SPEC] * 3 + [HBM_SPEC] * (ns + nl) + [pl.BlockSpec(memory_space=pltpu.VMEM)],
        input_output_aliases={i: 3 + i for i in range(ns + nl)},
        compiler_params=pltpu.CompilerParams(has_side_effects=EFFECT),
    )(*srcs, *lands)
    return (outs[:3], outs[3:3 + ns], outs[3 + ns:3 + ns + nl]), outs[-1]


def _exchange_wait(state, after, plan, name):
    sems, srcs, lands = state
    ns, nl = len(srcs), len(lands)

    def body(*refs):
        src_refs, land_refs = refs[:ns], refs[ns:ns + nl]
        send_sems, recv_sems, local_sems = refs[ns + nl:ns + nl + 3]
        rem, loc = _plan_descriptors(plan, src_refs, land_refs, send_sems, recv_sems, local_sems, arriving=True)
        for cp in rem:
            cp.wait_send()
            cp.wait_recv()
        for cp in loc:
            cp.wait()

    outs = _pallas(
        body, name=name, out_shape=[pltpu.HBM(a.shape, a.dtype) for a in list(srcs) + list(lands)],
        in_specs=[HBM_SPEC] * (ns + nl) + [SEM_SPEC] * 3 + [ANY_SPEC] * (after is not None), out_specs=[HBM_SPEC] * (ns + nl),
        input_output_aliases={i: i for i in range(ns + nl)},
        compiler_params=pltpu.CompilerParams(has_side_effects=EFFECT),
    )(*srcs, *lands, *sems, *([after] if after is not None else []))
    return outs[:ns], outs[ns:]


def _gather_plan(nb):
    def plan(x, y, c):
        me = 4 * x + 2 * y + c
        peers = [(x, y, 1 - c), (1 - x, y, c), (x, 1 - y, c), (1 - x, 1 - y, c)]
        remote = [(b, None, b, me, peer, 4 * peer[0] + 2 * peer[1] + peer[2]) for b in range(nb) for peer in peers]
        return remote, [(b, None, b, me) for b in range(nb)]
    return plan


def _relay_plan(nb):
    def plan(x, y, c):
        chips = [(1 - x, y), (x, 1 - y), (1 - x, 1 - y)]
        remote = [(("land", b), 4 * cx + 2 * cy + c, b, 4 * cx + 2 * cy + c, (x, y, 1 - c), 4 * cx + 2 * cy + (1 - c))
                  for b in range(nb) for cx, cy in chips]
        return remote, []
    return plan


def _pair_plan(nb):
    def plan(x, y, c):
        return [(b, 2 * q + (1 - c), b, q, (x, y, 1 - c), q) for b in range(nb) for q in range(4)], []
    return plan


def _chip_plan(nb):
    def plan(x, y, c):
        own = 2 * x + y
        chips = [(1 - x, y), (x, 1 - y), (1 - x, 1 - y)]
        remote = [(b, 2 * cx + cy, b, own, (cx, cy, c), 2 * cx + cy) for b in range(nb) for cx, cy in chips]
        return remote, [(b, own, b, own) for b in range(nb)]
    return plan


def _row_tile(r, cap=4608):
    return max(d for d in range(16, min(r, cap) + 1, 16) if r % d == 0)


def _pair_add(parts, got, core, name):
    _, r, cdim = parts.shape
    tr = _row_tile(r)

    def body(core_ref, p_ref, g_ref, o_ref):
        o_ref[...] = (p_ref[...].astype(F32) + g_ref[...].astype(F32)).astype(o_ref.dtype)

    return pl.pallas_call(
        body, name=name, out_shape=SDS((4, r, cdim), BF16),
        grid_spec=pltpu.PrefetchScalarGridSpec(
            num_scalar_prefetch=1, grid=(4, r // tr),
            in_specs=[pl.BlockSpec((None, tr, cdim), lambda q, i, core_ref: (2 * q + core_ref[0], i, 0)),
                      pl.BlockSpec((None, tr, cdim), lambda q, i, core_ref: (q, i, 0))],
            out_specs=pl.BlockSpec((None, tr, cdim), lambda q, i, core_ref: (q, i, 0))),
        compiler_params=_params(("parallel", "parallel")),
    )(core, parts, got)


def _sum_slots(parts, name):
    nj, r, cdim = parts.shape
    tr = _row_tile(r)

    def body(p_ref, o_ref):
        acc = p_ref[0].astype(F32)
        for j in range(1, nj):
            acc = acc + p_ref[j].astype(F32)
        o_ref[...] = acc

    return _pallas(
        body, name=name, out_shape=SDS((r, cdim), F32), grid=(r // tr,),
        in_specs=[pl.BlockSpec((nj, tr, cdim), lambda i: (0, i, 0))], out_specs=pl.BlockSpec((tr, cdim), lambda i: (i, 0)),
        compiler_params=_params(("parallel",)),
    )(parts)


def _adamw_update(wv, gv, mv, vv):
    nm = ADAM_B1 * mv + (1.0 - ADAM_B1) * gv
    nv = ADAM_B2 * vv + (1.0 - ADAM_B2) * (gv * gv)
    m_hat = nm / (1.0 - ADAM_B1 ** ADAM_STEP)
    v_hat = nv / (1.0 - ADAM_B2 ** ADAM_STEP)
    return -ADAM_LR * (m_hat / (jnp.sqrt(v_hat) + ADAM_EPS) + ADAM_WD * wv), nm, nv


def _adamw_summed(w, m, v, by_chip, name):
    nl, r, cdim = w.shape
    tr = _row_tile(r, 512)
    nblk = r // tr

    def body(*refs):
        chip_refs, (w_ref, m_ref, v_ref, g_ref, d_ref, nm_ref, nv_ref) = refs[:nl], refs[nl:]
        layer = pl.program_id(0)
        gv = None
        for ll, c_ref in enumerate(chip_refs):
            s = c_ref[0].astype(F32)
            for q in range(1, 4):
                s = s + c_ref[q].astype(F32)
            gv = s if gv is None else jnp.where(layer == ll, s, gv)
        g_ref[...] = gv
        d_ref[...], nm_ref[...], nv_ref[...] = _adamw_update(w_ref[...], gv, m_ref[...], v_ref[...])

    def chip_map(ll):
        return lambda l, i: (0, jnp.where(l == ll, i, jnp.where(l > ll, nblk - 1, 0)), 0)

    spec = pl.BlockSpec((None, tr, cdim), lambda l, i: (l, i, 0))
    return _pallas(
        body, name=name, grid=(nl, nblk),
        in_specs=[pl.BlockSpec((4, tr, cdim), chip_map(ll)) for ll in range(nl)] + [spec] * 3,
        out_specs=[spec] * 4, out_shape=[SDS((nl, r, cdim), F32)] * 4,
        compiler_params=_params(("arbitrary", "arbitrary")),
    )(*by_chip, w, m, v)


def _adamw(w, g, m, v, name):
    shape = w.shape
    cdim = shape[-1]
    w2, g2, m2, v2 = (a.reshape(-1, cdim) for a in (w, g, m, v))
    r = w2.shape[0]
    tr = next(d for d in (512, 352, 256, 128, 64, 32, 16, 8, r) if r % d == 0)

    def body(w_ref, g_ref, m_ref, v_ref, d_ref, nm_ref, nv_ref):
        d_ref[...], nm_ref[...], nv_ref[...] = _adamw_update(w_ref[...], g_ref[...], m_ref[...], v_ref[...])

    spec = pl.BlockSpec((tr, cdim), lambda i: (i, 0))
    outs = _pallas(
        body, name=name, grid=(r // tr,), in_specs=[spec] * 4, out_specs=[spec] * 3, out_shape=[SDS((r, cdim), F32)] * 3,
        compiler_params=_params(("parallel",)),
    )(w2, g2, m2, v2)
    return tuple(o.reshape(shape) for o in outs)


WEIGHTS = ("norm_ffn1", "ffn1_w_in", "ffn1_w_out", "norm_mix", "norm_ffn2", "ffn2_w_in", "ffn2_w_out", "ple_norm", "ple_gate_w",
           "ple_proj_w", "hyb_w_in", "conv_dw_w", "conv_dw_b", "conv_ln_g", "conv_ln_b", "ssm_conv_w", "ssm_conv_b", "ssm_dt_bias",
           "ssm_a_log", "ssm_d", "ssm_norm", "hyb_w_out", "att_w_qkv", "att_b_qkv", "att_sinks", "att_w_o", "att_b_o", "final_norm")
BIG = ("ffn1_w_in", "ffn1_w_out", "ffn2_w_in", "ffn2_w_out", "ple_gate_w", "ple_proj_w", "hyb_w_in", "hyb_w_out", "att_w_qkv",
       "att_w_o")
SMALL_SHARDED = {"conv_dw_w": 2, "ssm_conv_w": 2, "att_b_qkv": 1, "att_b_o": 1}
SMALL = tuple(n for n in WEIGHTS if n not in BIG)
TRANSPOSED = ("ffn1_w_in", "ffn2_w_in", "att_w_qkv")
PACK_ROWS = 16


def _pack(arrays, lead=0):
    pieces = []
    for a in arrays:
        flat = a.reshape(*a.shape[:lead], -1)
        size = flat.shape[-1]
        padded = -(-size // (PACK_ROWS * LANE)) * PACK_ROWS * LANE
        flat = jnp.pad(flat, [(0, 0)] * lead + [(0, padded - size)])
        pieces.append(flat.reshape(*a.shape[:lead], padded // LANE, LANE))
    return jnp.concatenate(pieces, axis=lead)


def _unpack(buf, shapes, lead=0):
    out, row = [], 0
    for shape in shapes:
        size = math.prod(shape)
        rows = -(-size // (PACK_ROWS * LANE)) * PACK_ROWS
        piece = lax.slice_in_dim(buf, row, row + rows, axis=lead)
        piece = piece.reshape(*buf.shape[:lead], rows * LANE)
        out.append(lax.slice_in_dim(piece, 0, size, axis=lead).reshape(*buf.shape[:lead], *shape))
        row += rows
    return out


def kernel(x, p, norm_ffn1, ffn1_w_in, ffn1_w_out, norm_mix, norm_ffn2, ffn2_w_in, ffn2_w_out, ple_norm, ple_gate_w, ple_proj_w, hyb_w_in, conv_dw_w, conv_dw_b, conv_ln_g, conv_ln_b, ssm_conv_w, ssm_conv_b, ssm_dt_bias, ssm_a_log, ssm_d, ssm_norm, hyb_w_out, att_w_qkv, att_b_qkv, att_sinks, att_w_o, att_b_o, final_norm, loss_target, m_norm_ffn1, m_ffn1_w_in, m_ffn1_w_out, m_norm_mix, m_norm_ffn2, m_ffn2_w_in, m_ffn2_w_out, m_ple_norm, m_ple_gate_w, m_ple_proj_w, m_hyb_w_in, m_conv_dw_w, m_conv_dw_b, m_conv_ln_g, m_conv_ln_b, m_ssm_conv_w, m_ssm_conv_b, m_ssm_dt_bias, m_ssm_a_log, m_ssm_d, m_ssm_norm, m_hyb_w_out, m_att_w_qkv, m_att_b_qkv, m_att_sinks, m_att_w_o, m_att_b_o, m_final_norm, v_norm_ffn1, v_ffn1_w_in, v_ffn1_w_out, v_norm_mix, v_norm_ffn2, v_ffn2_w_in, v_ffn2_w_out, v_ple_norm, v_ple_gate_w, v_ple_proj_w, v_hyb_w_in, v_conv_dw_w, v_conv_dw_b, v_conv_ln_g, v_conv_ln_b, v_ssm_conv_w, v_ssm_conv_b, v_ssm_dt_bias, v_ssm_a_log, v_ssm_d, v_ssm_norm, v_hyb_w_out, v_att_w_qkv, v_att_b_qkv, v_att_sinks, v_att_w_o, v_att_b_o, v_final_norm):
    args = (norm_ffn1, ffn1_w_in, ffn1_w_out, norm_mix, norm_ffn2, ffn2_w_in, ffn2_w_out, ple_norm, ple_gate_w, ple_proj_w, hyb_w_in, conv_dw_w, conv_dw_b, conv_ln_g, conv_ln_b, ssm_conv_w, ssm_conv_b, ssm_dt_bias, ssm_a_log, ssm_d, ssm_norm, hyb_w_out, att_w_qkv, att_b_qkv, att_sinks, att_w_o, att_b_o, final_norm)
    moments_m = (m_norm_ffn1, m_ffn1_w_in, m_ffn1_w_out, m_norm_mix, m_norm_ffn2, m_ffn2_w_in, m_ffn2_w_out, m_ple_norm, m_ple_gate_w, m_ple_proj_w, m_hyb_w_in, m_conv_dw_w, m_conv_dw_b, m_conv_ln_g, m_conv_ln_b, m_ssm_conv_w, m_ssm_conv_b, m_ssm_dt_bias, m_ssm_a_log, m_ssm_d, m_ssm_norm, m_hyb_w_out, m_att_w_qkv, m_att_b_qkv, m_att_sinks, m_att_w_o, m_att_b_o, m_final_norm)
    moments_v = (v_norm_ffn1, v_ffn1_w_in, v_ffn1_w_out, v_norm_mix, v_norm_ffn2, v_ffn2_w_in, v_ffn2_w_out, v_ple_norm, v_ple_gate_w, v_ple_proj_w, v_hyb_w_in, v_conv_dw_w, v_conv_dw_b, v_conv_ln_g, v_conv_ln_b, v_ssm_conv_w, v_ssm_conv_b, v_ssm_dt_bias, v_ssm_a_log, v_ssm_d, v_ssm_norm, v_hyb_w_out, v_att_w_qkv, v_att_b_qkv, v_att_sinks, v_att_w_o, v_att_b_o, v_final_norm)
    w = dict(zip(WEIGHTS, args))
    m = dict(zip(WEIGHTS, moments_m))
    v = dict(zip(WEIGHTS, moments_v))
    cx, cy, cc = _place()
    me = 4 * cx + 2 * cy + cc

    core = jnp.reshape(cc, (1,)).astype(jnp.int32)
    layer_of = lambda n, i: 1 if n.startswith("att_") else i
    keys = [[(n, i) for n in BIG for i in range(w[n].shape[0]) if layer_of(n, i) == layer] for layer in range(2)]

    first = [key for key in keys[0] if key[0].startswith("ffn1")]
    mixer = [key for key in keys[0] if key[0].startswith("hyb")]
    rest0 = [key for key in keys[0] if key not in first + mixer]
    gw, by_chip = {}, {}
    view = lambda a, n: jnp.swapaxes(a, 1, 2) if n in TRANSPOSED else a
    block = lambda n, i: view(w[n], n)[i].astype(BF16)

    def gather_later(group, name):
        blocks = [block(n, i) for n, i in group]
        plan, relay_plan = _gather_plan(len(blocks)), _relay_plan(len(blocks))
        state, _ = _exchange_start(blocks, [SDS((N_DEV,) + b.shape, BF16) for b in blocks], plan, f"{name}_start")
        stage = {}

        def relay():
            _, landed = _exchange_wait(state, None, plan, f"{name}_wait")
            stage["relay"], _ = _exchange_start([], None, relay_plan, f"{name}_relay_start", lands=landed)

        def arrived():
            gw.update(zip(group, _exchange_wait(stage["relay"], None, relay_plan, f"{name}_relay_wait")[1]))

        return relay, arrived

    def reduce_later(group, big, name):
        pair_plan, chip_plan = _pair_plan(len(group)), _chip_plan(len(group))
        parts = [big[key] for key in group]
        pair, token = _exchange_start(parts, [SDS((4,) + pt.shape[1:], BF16) for pt in parts], pair_plan, f"{name}_pair_start")
        stage = {}

        def middle(after):
            thru, got = _exchange_wait(pair, after, pair_plan, f"{name}_pair_wait")
            sums = [_pair_add(pt, gt, core, f"grads_pair_add_{n}_{i}") for pt, gt, (n, i) in zip(thru, got, group)]
            stage["chip"], chip_token = _exchange_start(sums, [SDS(s.shape, BF16) for s in sums], chip_plan, f"{name}_chip_start")
            return chip_token

        def finish(after):
            by_chip.update(zip(group, _exchange_wait(stage["chip"], after, chip_plan, f"{name}_chip_wait")[1]))

        return token, middle, finish

    _restart_chain()
    gw["ffn1_w_in", 0], gathered_small = _all_gather([block("ffn1_w_in", 0), _pack([w[n] for n in SMALL_SHARDED])],
                                                      pltpu.HBM, "gather_weights_first")
    early_relay, early_arrived = gather_later([("ffn1_w_out", 0)], "gather_weights_early")
    mixer_relay, mixer_arrived = gather_later(mixer, "gather_weights_mixer")
    rest0_relay, rest0_arrived = gather_later(rest0, "gather_weights_rest")
    layer1_relay, layer1_arrived = gather_later(keys[1], "gather_weights_l1")
    gs = dict(zip(SMALL_SHARDED, _unpack(gathered_small, [w[n].shape for n in SMALL_SHARDED], lead=1)))
    rep = {n: w[n] for n in SMALL if n not in SMALL_SHARDED}

    tables = _rope_tables(x.shape[1])
    pb = p[:, 0].astype(BF16)
    w0, s0 = {}, {}

    def first_w_out():
        early_relay()
        early_arrived()
        w0.update(_build_layer(0, gw, gs, rep, parts=("ffn1",)))
        return w0["ffn1_out"]

    h, s0["ffn1"] = _ffn_fwd(x[0], rep["norm_ffn1"][0][None], gw["ffn1_w_in", 0], first_w_out, "l0_ffn1")
    mixer_relay()
    mixer_arrived()
    w0.update(_build_layer(0, gw, gs, rep, parts=("mix",)))
    h, s0["mix"] = _hyb_fwd(h, w0, "l0_hyb", after_in=rest0_relay)
    layer1_relay()
    rest0_arrived()
    w0.update(_build_layer(0, gw, gs, rep, parts=("ffn2", "ple")))
    h, s0["ffn2"] = _ffn_fwd(h, w0["norm_ffn2"], w0["ffn2_in"], w0["ffn2_out"], "l0_ffn2")
    h, s0["ple"] = _ple_fwd(h, w0["ple_norm"], pb[0], w0["ple_gate"], w0["ple_proj"], "l0_ple")
    layer1_arrived()
    w1 = _build_layer(1, gw, gs, rep)
    h, s1 = _layer_fwd(1, h, w1, pb[1], tables)
    loss, dh, dhb, d_final = _loss_head(h, loss_target[0], final_norm[None], "loss_head")
    loss = lax.psum(loss[0, 0], ("x", "y", "c"))

    dh, dhb, head1 = _layer_bwd_head(1, dh, dhb, s1, w1, pb[1])
    dh, dhb, tail1 = _layer_bwd_tail(1, dh, dhb, s1, w1, tables)
    grads1 = {**head1, **tail1}
    l1_token, l1_middle, l1_finish = reduce_later(keys[1], _big_grads(1, grads1), "grads_l1")
    dh, dhb, grads0 = _layer_bwd_head(0, dh, dhb, s0, w0, pb[0], deps=(l1_token,))
    dh, dhb, grads0["norm_ffn2"], grads0["ffn2_in"], grads0["ffn2_out"] = _ffn_bwd(
        dh, dhb, s0["ffn2"], w0["norm_ffn2"], w0["ffn2_in"], w0["ffn2_out"], "l0_ffn2", deps=(l1_middle(dh),))
    dh, dhb, mixer_grads = _hyb_bwd(dh, dhb, s0["mix"], w0, "l0_hyb")
    grads0.update(mixer_grads)
    l0_token, l0_middle, l0_finish = reduce_later(mixer + rest0, _big_grads(0, grads0), "grads_l0")
    last = {}

    def reduce_first(dw_in, dw_out):
        token, middle, last["finish"] = reduce_later(first, _big_grads(0, dict(ffn1_in=dw_in, ffn1_out=dw_out)), "grads_first")
        middle(token)

    dx, dhb, grads0["norm_ffn1"], grads0["ffn1_in"], grads0["ffn1_out"] = _ffn_bwd(
        dh, dhb, s0["ffn1"], w0["norm_ffn1"], w0["ffn1_in"], w0["ffn1_out"], "l0_ffn1", deps=(l0_token,),
        hook=lambda dpre: (l0_middle(dpre),), weights_hook=reduce_first)
    l1_finish(dx)
    l0_finish(dx)
    last["finish"](dx)
    _, small = _collect_grads([grads0, grads1], d_final)
    small_shapes = [small[n].shape for n in SMALL]
    all_small = _all_gather([_pack([small[n] for n in SMALL])], pltpu.VMEM, "gather_small_grads")[0]
    g = dict(zip(SMALL, _unpack(_sum_slots(all_small, "small_grads_sum"), small_shapes)))
    for n, axis in SMALL_SHARDED.items():
        g[n] = lax.dynamic_slice_in_dim(g[n], me * w[n].shape[axis], w[n].shape[axis], axis=axis)

    delta, new_m, new_v = {}, {}, {}
    for n in BIG:
        outs = _adamw_summed(view(w[n], n), view(m[n], n), view(v[n], n), [by_chip[n, i] for i in range(w[n].shape[0])],
                             f"adamw_{n}")
        g[n], delta[n], new_m[n], new_v[n] = (view(o, n) for o in outs)
    packed = [_pack([d[n] for n in SMALL]) for d in (w, g, m, v)]
    shapes = [w[n].shape for n in SMALL]
    for d, buf in zip((delta, new_m, new_v), _adamw(*packed, "adamw_small")):
        d.update(zip(SMALL, _unpack(buf, shapes)))
    return (loss, dx[None], *[g[n] for n in WEIGHTS], *[delta[n] for n in WEIGHTS], *[new_m[n] for n in WEIGHTS],
            *[new_v[n] for n in WEIGHTS])
```

```python
import functools
import math

import numpy as np
import jax
import jax.numpy as jnp
from jax import lax
from jax.experimental import pallas as pl
from jax.experimental.pallas import tpu as pltpu

F32, BF16 = jnp.float32, jnp.bfloat16
HI = lax.Precision.HIGHEST
SDS = jax.ShapeDtypeStruct

N_DEV = 8
D = 1024
D_FF = 2816
FF_SHARD = 2 * D_FF // N_DEV
PLE_DIM = 256
EPS = 1e-6
CONV_W = 31
SSM_CONV = 4
SSM_HEADS = 16
SSM_XBC = 1536
CHUNK = 128
HYB_IN = 4624
HYB_PAD = 5120
DT_COL = 4608
N_PAIR = 8
ROPE_THETA = 10000.0
LANE = 128
VMEM_LIMIT = 56 * 1024 * 1024

ADAM_LR, ADAM_B1, ADAM_B2, ADAM_EPS, ADAM_WD, ADAM_STEP = 0.001, 0.9, 0.999, 1e-08, 0.01, 10


def _params(sem):
    return pltpu.CompilerParams(dimension_semantics=sem, vmem_limit_bytes=VMEM_LIMIT)


_CHAIN = []


def _restart_chain():
    _CHAIN.clear()


def _pallas(body, *, in_specs, **kw):
    def run(*args):
        n, dep = len(args), list(_CHAIN)

        def chained(*refs):
            return body(*refs[:n], *refs[n + len(dep):])

        outs = pl.pallas_call(chained, in_specs=list(in_specs) + [pl.BlockSpec(memory_space=pl.ANY)] * len(dep), **kw)(*args, *dep)
        _CHAIN[:] = [outs[-1] if isinstance(outs, (list, tuple)) else outs]
        return outs

    return run


def _mm(a, b, *, ta=False, tb=False, reduce_j=False, out_dtypes=(F32,), tm=1024, tn=1024, tk=1024,
        epi=None, extras=(), rows=(), deps=(), sums=0, name):
    ja, jb = a.shape[0], b.shape[0]
    nj = max(ja, jb)
    jo = 1 if reduce_j else nj
    m, k = (a.shape[2], a.shape[1]) if ta else (a.shape[1], a.shape[2])
    n = b.shape[1] if tb else b.shape[2]
    assert (b.shape[2] if tb else b.shape[1]) == k and ja in (1, nj) and jb in (1, nj)
    tm, tn, tk = min(tm, m), min(tn, n), min(tk, k)
    assert m % tm == 0 and n % tn == 0 and k % tk == 0, (name, m, n, k, tm, tn, tk)
    assert not sums or (tn == n and (reduce_j or nj == 1))
    nk = k // tk
    steps = nk * (nj if reduce_j else 1)
    ne, nr, no = len(extras), len(rows), len(out_dtypes)

    def a_map(i, c, j, kk):
        return (j if ja > 1 else 0, kk, i) if ta else (j if ja > 1 else 0, i, kk)

    def b_map(i, c, j, kk):
        return (j if jb > 1 else 0, c, kk) if tb else (j if jb > 1 else 0, kk, c)

    def o_map(i, c, j, kk):
        return (0 if reduce_j else j, i, c)

    dims = (((0 if ta else 1,), (1 if tb else 0,)), ((), ()))

    def body(a_ref, b_ref, *rest):
        ex, rw = rest[:ne], rest[ne:ne + nr]
        outs = rest[ne + nr + len(deps):ne + nr + len(deps) + no]
        sum_refs = rest[ne + nr + len(deps) + no:ne + nr + len(deps) + no + sums]
        first_tile = pl.program_id(0) == 0

        def product():
            return lax.dot_general(a_ref[...], b_ref[...], dims, preferred_element_type=F32)

        def finish(acc):
            res = epi(acc, *[e[...] for e in ex], *[r[...] for r in rw]) if epi else (acc,)
            for o, r in zip(outs, res):
                o[...] = r.astype(o.dtype)
            for s_ref, r in zip(sum_refs, res[no:]):
                @pl.when(first_tile)
                def _(s_ref=s_ref, r=r):
                    s_ref[...] = r

                @pl.when(jnp.logical_not(first_tile))
                def _(s_ref=s_ref, r=r):
                    s_ref[...] += r

        if steps == 1:
            finish(product())
            return
        acc_ref = rest[-1]
        kk = pl.program_id(3)
        step = pl.program_id(2) * nk + kk if reduce_j else kk

        @pl.when(step == 0)
        def _():
            acc_ref[...] = product()

        @pl.when(jnp.logical_and(step > 0, step < steps - 1))
        def _():
            acc_ref[...] += product()

        @pl.when(step == steps - 1)
        def _():
            finish(acc_ref[...] + product())

    o_spec = pl.BlockSpec((None, tm, tn), o_map)
    row_spec = pl.BlockSpec((1, tn), lambda i, c, j, kk: (0, c))
    return _pallas(
        body, name=name, grid=(m // tm, n // tn, nj, nk),
        in_specs=[pl.BlockSpec((None, tk, tm) if ta else (None, tm, tk), a_map),
                  pl.BlockSpec((None, tn, tk) if tb else (None, tk, tn), b_map)]
        + [o_spec] * ne + [row_spec] * nr + [ANY_SPEC] * len(deps),
        out_specs=[o_spec] * no + [row_spec] * sums,
        out_shape=[SDS((jo, m, n), dt) for dt in out_dtypes] + [SDS((1, n), F32)] * sums,
        scratch_shapes=[pltpu.VMEM((tm, tn), F32)] if steps > 1 else [],
        compiler_params=_params(("arbitrary" if sums else "parallel", "parallel", "arbitrary", "arbitrary")),
    )(a, b, *extras, *rows, *deps)


def _whole(p):
    return pl.BlockSpec(p.shape, lambda *_: (0,) * p.ndim)


ANY_SPEC = pl.BlockSpec(memory_space=pl.ANY)


def _rowop(fn, tiles, params, outs, *, grid, name, deps=()):
    nin = len(tiles) + len(params)

    def body(*refs):
        res = fn(*[r[...].astype(F32) for r in refs[:nin]])
        for r, o in zip(refs[nin + len(deps):], res):
            r[...] = o.astype(r.dtype)

    return _pallas(
        body, name=name, grid=grid,
        in_specs=[s for _, s in tiles] + [_whole(p) for p in params] + [ANY_SPEC] * len(deps),
        out_specs=[s for _, _, s in outs], out_shape=[SDS(sh, dt) for sh, dt, _ in outs],
        compiler_params=_params(("parallel",) * len(grid)),
    )(*[t for t, _ in tiles], *params, *deps)


def _rowop_bwd(fn, tiles, params, cots, wrt, gouts, *, grid, name, adds=(), deps=()):
    nt, npar, nc, na = len(tiles), len(params), len(cots), len(adds)
    nin = nt + npar
    flat = [i for grp in wrt for i in grp]
    n_gout = sum(len(dts) for _, dts, _ in gouts)

    def body(*refs):
        vals = [r[...].astype(F32) for r in refs[:nin]]
        cvals = [r[...].astype(F32) for r in refs[nin:nin + nc]]
        avals = [r[...].astype(F32) for r in refs[nin + nc:nin + nc + na]]
        orefs = refs[nin + nc + na + len(deps):]
        diff_idx = flat + list(range(nt, nin))

        def f(*dv):
            full = list(vals)
            for i, v in zip(diff_idx, dv):
                full[i] = v
            return fn(*full)

        _, vjp = jax.vjp(f, *[vals[i] for i in diff_idx])
        grads = vjp(tuple(cvals))
        tile_g, par_g = list(grads[:len(flat)]), grads[len(flat):]
        group_g, at = [], 0
        for grp in wrt:
            members = tile_g[at:at + len(grp)]
            at += len(grp)
            group_g.append(members[0] if len(grp) == 1 else jnp.stack(members, axis=0))
        for av in avals:
            group_g[0] = group_g[0] + av
        o = 0
        for g, (_, dts, _) in zip(group_g, gouts):
            for _ in dts:
                orefs[o][...] = g.astype(orefs[o].dtype)
                o += 1
        first = functools.reduce(jnp.logical_and, [pl.program_id(ax) == 0 for ax in range(len(grid))])
        for r, g in zip(orefs[n_gout:], par_g):
            @pl.when(first)
            def _(r=r, g=g):
                r[...] = g

            @pl.when(jnp.logical_not(first))
            def _(r=r, g=g):
                r[...] += g

    out_specs, out_shape = [], []
    for sh, dts, spec in gouts:
        for dt in dts:
            out_specs.append(spec)
            out_shape.append(SDS(sh, dt))
    for p in params:
        out_specs.append(_whole(p))
        out_shape.append(SDS(p.shape, F32))
    return _pallas(
        body, name=name, grid=grid,
        in_specs=[s for _, s in tiles] + [_whole(p) for p in params] + [s for _, s in cots] + [s for _, s in adds]
        + [ANY_SPEC] * len(deps),
        out_specs=out_specs, out_shape=out_shape,
        compiler_params=_params(("arbitrary",) * len(grid)),
    )(*[t for t, _ in tiles], *params, *[c for c, _ in cots], *[a for a, _ in adds], *deps)


def _tok(c, tm, col=0):
    return pl.BlockSpec((tm, c), lambda i, col=col: (i, col))


def _rms_fn(h, g):
    return (h * lax.rsqrt(jnp.mean(h * h, axis=-1, keepdims=True) + EPS) * g,)


def _lnswish_fn(u, g, b):
    mu = jnp.mean(u, axis=-1, keepdims=True)
    xc = u - mu
    y = xc * lax.rsqrt(jnp.mean(xc * xc, axis=-1, keepdims=True) + EPS) * g + b
    return (y * jax.nn.sigmoid(y),)


def _ple_fn(z, e):
    return (jax.nn.sigmoid(z) * e,)


def _rms(h, g, name, tm=512, deps=()):
    t = h.shape[0]
    return _rowop(_rms_fn, [(h, _tok(D, tm))], [g], [((t, D), BF16, _tok(D, tm))], grid=(t // tm,), name=name, deps=deps)[0]


def _drms_epi(dn, h, dres, g):
    _, vjp = jax.vjp(_rms_fn, h, g)
    dh, dg = vjp((dn,))
    dh = dh + dres
    return dh, dh, dg


def _mm_drms(a, b, h, g, dres, name, tk, tb=True):
    dh, dhb, dg = _mm(a, b, tb=tb, reduce_j=a.shape[0] > 1, tm=1024, tk=tk, epi=_drms_epi, extras=(h[None], dres[None]),
                      rows=(g,), out_dtypes=(F32, BF16), sums=1, name=name)
    return dh[0], dhb[0], dg


def _conv_geometry(width):
    pad = 32 if width > 8 else 8
    return pad, pad - (width - 1)


def _fill_shifts(xpad_ref, sh_ref, t, shifts):
    for r in shifts:
        sh_ref[r, :, :] = xpad_ref[pl.ds(r, t + 32), :]


def _dwconv(xs, w, b, *, width, glu, silu, cb, name):
    t = xs[0][0].shape[0]
    c = w.shape[1]
    pad, off = _conv_geometry(width)
    shifts = sorted({(k + off) % 8 for k in range(width)})
    ch = 128

    def body(*refs):
        x_refs, (w_ref, b_ref, o_ref, xpad_ref, sh_ref) = refs[:len(xs)], refs[len(xs):]
        u = x_refs[0][...] * jax.nn.sigmoid(x_refs[1][...]) if glu else x_refs[0][...]
        xpad_ref[pl.ds(0, pad), :] = jnp.zeros((pad, cb), F32)
        xpad_ref[pl.ds(pad, t), :] = u
        xpad_ref[pl.ds(pad + t, 40 - pad), :] = jnp.zeros((40 - pad, cb), F32)
        _fill_shifts(xpad_ref, sh_ref, t, shifts)

        def chunk(i, carry):
            t0 = pl.multiple_of(i * ch, ch)
            acc = jnp.broadcast_to(b_ref[...], (ch, cb))
            for k in range(width):
                q, r = divmod(k + off, 8)
                acc = acc + w_ref[pl.ds(k, 1), :] * sh_ref[r, pl.ds(t0 + 8 * q, ch), :]
            o_ref[pl.ds(t0, ch), :] = acc * jax.nn.sigmoid(acc) if silu else acc
            return carry

        lax.fori_loop(0, t // ch, chunk, 0)

    return _pallas(
        body, name=name, grid=(c // cb,),
        in_specs=[pl.BlockSpec((t, cb), lambda i, o=o: (0, o + i)) for _, o in xs]
        + [pl.BlockSpec((width, cb), lambda i: (0, i)), pl.BlockSpec((1, cb), lambda i: (0, i))],
        out_specs=pl.BlockSpec((t, cb), lambda i: (0, i)), out_shape=SDS((t, c), F32),
        scratch_shapes=[pltpu.VMEM((t + 40, cb), F32), pltpu.VMEM((8, t + 32, cb), F32)],
        compiler_params=_params(("parallel",)),
    )(*[x for x, _ in xs], w, b)


def _dwconv_bwd(xs, w, b, dy, *, width, glu, silu, cb, name):
    t = xs[0][0].shape[0]
    c = w.shape[1]
    pad, off = _conv_geometry(width)
    shifts = sorted({(k + off) % 8 for k in range(width)})
    shifts_t = sorted({mm % 8 for mm in range(width)})
    ch = 128
    nx = len(xs)

    def body(*refs):
        x_refs = refs[:nx]
        w_ref, b_ref, dy_ref = refs[nx:nx + 3]
        dx_refs = refs[nx + 3:nx + 3 + nx]
        dw_ref, db_ref, xpad_ref, sh_ref, dc_ref = refs[nx + 3 + nx:]
        u = x_refs[0][...] * jax.nn.sigmoid(x_refs[1][...]) if glu else x_refs[0][...]
        xpad_ref[pl.ds(0, pad), :] = jnp.zeros((pad, cb), F32)
        xpad_ref[pl.ds(pad, t), :] = u
        xpad_ref[pl.ds(pad + t, 40 - pad), :] = jnp.zeros((40 - pad, cb), F32)
        _fill_shifts(xpad_ref, sh_ref, t, shifts)

        if silu:
            def act_chunk(i, carry):
                t0 = pl.multiple_of(i * ch, ch)
                acc = jnp.broadcast_to(b_ref[...], (ch, cb))
                for k in range(width):
                    q, r = divmod(k + off, 8)
                    acc = acc + w_ref[pl.ds(k, 1), :] * sh_ref[r, pl.ds(t0 + 8 * q, ch), :]
                sg = jax.nn.sigmoid(acc)
                dc_ref[pl.ds(t0, ch), :] = dy_ref[pl.ds(t0, ch), :] * (sg * (1.0 + acc * (1.0 - sg)))
                return carry

            lax.fori_loop(0, t // ch, act_chunk, 0)
        else:
            dc_ref[...] = dy_ref[...]

        def dw_chunk(i, accs):
            t0 = pl.multiple_of(i * ch, ch)
            new = list(accs)
            for s in range(ch // 8):
                d = dc_ref[pl.ds(t0 + 8 * s, 8), :]
                for k in range(width):
                    q, r = divmod(k + off, 8)
                    new[k] = new[k] + d * sh_ref[r, pl.ds(t0 + 8 * (q + s), 8), :]
                new[width] = new[width] + d
            return tuple(new)

        accs = lax.fori_loop(0, t // ch, dw_chunk, tuple(jnp.zeros((8, cb), F32) for _ in range(width + 1)))
        for k in range(width):
            dw_ref[pl.ds(k, 1), :] = jnp.sum(accs[k], axis=0, keepdims=True)
        db_ref[...] = jnp.sum(accs[width], axis=0, keepdims=True)

        xpad_ref[pl.ds(0, t), :] = dc_ref[...]
        xpad_ref[pl.ds(t, 40), :] = jnp.zeros((40, cb), F32)
        _fill_shifts(xpad_ref, sh_ref, t, shifts_t)

        def dx_chunk(i, carry):
            t0 = pl.multiple_of(i * ch, ch)
            acc = jnp.zeros((ch, cb), F32)
            for mm in range(width):
                q, r = divmod(mm, 8)
                acc = acc + w_ref[pl.ds(width - 1 - mm, 1), :] * sh_ref[r, pl.ds(t0 + 8 * q, ch), :]
            if glu:
                val, gate = x_refs[0][pl.ds(t0, ch), :], x_refs[1][pl.ds(t0, ch), :]
                sg = jax.nn.sigmoid(gate)
                dx_refs[0][pl.ds(t0, ch), :] = (acc * sg).astype(BF16)
                dx_refs[1][pl.ds(t0, ch), :] = (acc * val * sg * (1.0 - sg)).astype(BF16)
            else:
                dx_refs[0][pl.ds(t0, ch), :] = acc.astype(BF16)
            return carry

        lax.fori_loop(0, t // ch, dx_chunk, 0)

    col = pl.BlockSpec((t, cb), lambda i: (0, i))
    return _pallas(
        body, name=name, grid=(c // cb,),
        in_specs=[pl.BlockSpec((t, cb), lambda i, o=o: (0, o + i)) for _, o in xs]
        + [pl.BlockSpec((width, cb), lambda i: (0, i)), pl.BlockSpec((1, cb), lambda i: (0, i)), col],
        out_specs=[col] * nx + [pl.BlockSpec((width, cb), lambda i: (0, i)), pl.BlockSpec((1, cb), lambda i: (0, i))],
        out_shape=[SDS((t, c), BF16)] * nx + [SDS((width, c), F32), SDS((1, c), F32)],
        scratch_shapes=[pltpu.VMEM((t + 40, cb), F32), pltpu.VMEM((8, t + 32, cb), F32), pltpu.VMEM((t, cb), F32)],
        compiler_params=_params(("parallel",)),
    )(*[x for x, _ in xs], w, b, dy)


_DIMS = {"nn": (((1,), (0,)), ((), ())), "nt": (((1,), (1,)), ((), ())), "tn": (((0,), (0,)), ((), ()))}


def _raw_dot(a, b, mode):
    return lax.dot_general(a.astype(BF16), b.astype(BF16), _DIMS[mode], preferred_element_type=F32)


@functools.partial(jax.custom_vjp, nondiff_argnums=(2,))
def _bdot(a, b, mode):
    return _raw_dot(a, b, mode)


def _bdot_fwd(a, b, mode):
    return _raw_dot(a, b, mode), (a, b)


def _bdot_bwd(mode, res, g):
    a, b = res
    if mode == "nn":
        return _raw_dot(g, b, "nt"), _raw_dot(a, g, "tn")
    if mode == "nt":
        return _raw_dot(g, b, "nn"), _raw_dot(g, a, "tn")
    return _raw_dot(b, g, "nt"), _raw_dot(a, g, "nn")


_bdot.defvjp(_bdot_fwd, _bdot_bwd)


def _iota(shape, axis):
    return lax.broadcasted_iota(jnp.int32, shape, axis)


def _half_masks():
    left = (_iota((1, LANE), 1) < 64).astype(F32)
    return left, 1.0 - left


def _split3(a):
    a1 = a.astype(BF16)
    r1 = a - a1.astype(F32)
    a2 = r1.astype(BF16)
    return a1, a2, (r1 - a2.astype(F32)).astype(BF16)


def _exact_dot(a, e, mode):
    return sum(lax.dot_general(piece, e, _DIMS[mode], preferred_element_type=F32) for piece in _split3(a))


@jax.custom_vjp
def _spread(a, e):
    return _exact_dot(a, e, "nn")


_spread.defvjp(lambda a, e: (_exact_dot(a, e, "nn"), e), lambda e, g: (_exact_dot(g, e, "nt"), jnp.zeros_like(e)))


@jax.custom_vjp
def _running_sum(tri, a):
    return sum(lax.dot_general(tri, piece, _DIMS["nn"], preferred_element_type=F32) for piece in _split3(a))


_running_sum.defvjp(
    lambda tri, a: (sum(lax.dot_general(tri, piece, _DIMS["nn"], preferred_element_type=F32) for piece in _split3(a)), tri),
    lambda tri, g: (jnp.zeros_like(tri), sum(lax.dot_general(tri, piece, _DIMS["tn"], preferred_element_type=F32)
                                             for piece in _split3(g))))


def _ssd_chunk(state, xa, dtr, z, dtb, alog, dskf, ng):
    xs, bm, cm = xa[:, :D], xa[:, D:D + 256], xa[:, D + 256:]
    left, right = _half_masks()
    expand = (_iota((LANE, D), 1) // 64 == _iota((LANE, D), 0)).astype(BF16)
    li, si = _iota((CHUNK, CHUNK), 0), _iota((CHUNK, CHUNK), 1)
    tril = li >= si
    dt16 = jax.nn.softplus(dtr + dtb)
    adt = dt16 * (-jnp.exp(alog))
    dtf = _spread(dt16, expand)
    cs16 = _running_sum(tril.astype(BF16), adt)
    csf = _spread(cs16, expand)
    totf = jnp.sum(jnp.where(_iota((CHUNK, D), 0) == CHUNK - 1, csf, 0.0), axis=0, keepdims=True)
    cst = cs16.T
    xdt = xs * dtf
    ys, new_state = [], []
    for g in range(2):
        bg, cg = bm[:, LANE * g:LANE * (g + 1)], cm[:, LANE * g:LANE * (g + 1)]
        cb = _bdot(cg, bg, "nt")
        for q in range(4):
            pr = 4 * g + q
            decay = []
            for h in (2 * pr, 2 * pr + 1):
                col = jnp.sum(jnp.where(si == h, cs16, 0.0), axis=1, keepdims=True)
                row = jnp.sum(jnp.where(li == h, cst, 0.0), axis=0, keepdims=True)
                decay.append(cb * jnp.exp(jnp.where(tril, col - row, -jnp.inf)))
            xp = xdt[:, LANE * pr:LANE * (pr + 1)]
            y_diag = _bdot(jnp.concatenate(decay, axis=1), jnp.concatenate([xp * left, xp * right], axis=0), "nn")
            csb, tot = csf[:, LANE * pr:LANE * (pr + 1)], totf[:, LANE * pr:LANE * (pr + 1)]
            ys.append(y_diag + _bdot(cg, state[pr], "nn") * jnp.exp(csb))
            new_state.append(state[pr] * jnp.exp(tot) + _bdot(bg, xp * jnp.exp(tot - csb), "tn"))
    y = jnp.concatenate(ys, axis=1)
    y = y + dskf * xs
    y = y * (z * jax.nn.sigmoid(z))
    halves = []
    for g in range(2):
        yg = y[:, 512 * g:512 * (g + 1)]
        halves.append(yg * lax.rsqrt(jnp.mean(yg * yg, axis=-1, keepdims=True) + EPS))
    return jnp.concatenate(halves, axis=1) * ng, jnp.stack(new_state, axis=0)


def _ssd_specs(t, rev):
    nc = t // CHUNK
    ix = (lambda c: nc - 1 - c) if rev else (lambda c: c)
    return nc, ix


def _ssd_fwd(xa, proj, dtb, alog, dsk, ng, name):
    t = xa.shape[0]
    nc, ix = _ssd_specs(t, False)

    def body(xa_ref, dt_ref, z_ref, dtb_ref, alog_ref, dsk_ref, ng_ref, y_ref, st_ref, carry_ref):
        @pl.when(pl.program_id(0) == 0)
        def _():
            carry_ref[...] = jnp.zeros_like(carry_ref)

        st_ref[...] = carry_ref[...]
        y, new = _ssd_chunk(carry_ref[...], xa_ref[...], dt_ref[...], z_ref[...], dtb_ref[...], alog_ref[...],
                            dsk_ref[...], ng_ref[...])
        y_ref[...] = y.astype(BF16)
        carry_ref[...] = new

    small = [dtb, alog, dsk, ng]
    return _pallas(
        body, name=name, grid=(nc,),
        in_specs=[pl.BlockSpec((CHUNK, SSM_XBC), lambda c: (c, 0)),
                  pl.BlockSpec((CHUNK, LANE), lambda c: (c, DT_COL // LANE)),
                  pl.BlockSpec((CHUNK, D), lambda c: (c, 2))] + [_whole(p) for p in small],
        out_specs=[pl.BlockSpec((CHUNK, D), lambda c: (c, 0)), pl.BlockSpec((None, N_PAIR, LANE, LANE), lambda c: (c, 0, 0, 0))],
        out_shape=[SDS((t, D), BF16), SDS((nc, N_PAIR, LANE, LANE), F32)],
        scratch_shapes=[pltpu.VMEM((N_PAIR, LANE, LANE), F32)],
        compiler_params=_params(("arbitrary",)),
    )(xa, proj, proj, *small)


def _ssd_bwd(xa, proj, states, dy, dtb, alog, dsk, ng, name):
    t = xa.shape[0]
    nc, ix = _ssd_specs(t, True)

    def body(xa_ref, dt_ref, z_ref, st_ref, dy_ref, dtb_ref, alog_ref, dsk_ref, ng_ref,
             dxa_ref, ddt_ref, dz_ref, gdtb_ref, galog_ref, gdsk_ref, gng_ref, carry_ref):
        first = pl.program_id(0) == 0

        @pl.when(first)
        def _():
            carry_ref[...] = jnp.zeros_like(carry_ref)

        args = (st_ref[...], xa_ref[...], dt_ref[...], z_ref[...], dtb_ref[...], alog_ref[...], dsk_ref[...], ng_ref[...])
        _, vjp = jax.vjp(_ssd_chunk, *args)
        ds, dxa, ddt, dz, gdtb, galog, gdsk, gng = vjp((dy_ref[...], carry_ref[...]))
        carry_ref[...] = ds
        dxa_ref[...] = dxa
        ddt_ref[...] = ddt.astype(BF16)
        dz_ref[...] = dz.astype(BF16)
        for r, g in ((gdtb_ref, gdtb), (galog_ref, galog), (gdsk_ref, gdsk), (gng_ref, gng)):
            @pl.when(first)
            def _(r=r, g=g):
                r[...] = g

            @pl.when(jnp.logical_not(first))
            def _(r=r, g=g):
                r[...] += g

    small = [dtb, alog, dsk, ng]
    return _pallas(
        body, name=name, grid=(nc,),
        in_specs=[pl.BlockSpec((CHUNK, SSM_XBC), lambda c: (ix(c), 0)),
                  pl.BlockSpec((CHUNK, LANE), lambda c: (ix(c), DT_COL // LANE)),
                  pl.BlockSpec((CHUNK, D), lambda c: (ix(c), 2)),
                  pl.BlockSpec((None, N_PAIR, LANE, LANE), lambda c: (ix(c), 0, 0, 0)),
                  pl.BlockSpec((CHUNK, D), lambda c: (ix(c), 0))] + [_whole(p) for p in small],
        out_specs=[pl.BlockSpec((CHUNK, SSM_XBC), lambda c: (ix(c), 0)), pl.BlockSpec((CHUNK, LANE), lambda c: (ix(c), 0)),
                   pl.BlockSpec((CHUNK, D), lambda c: (ix(c), 0))] + [_whole(p) for p in small],
        out_shape=[SDS((t, SSM_XBC), F32), SDS((t, LANE), BF16), SDS((t, D), BF16)] + [SDS(p.shape, F32) for p in small],
        scratch_shapes=[pltpu.VMEM((N_PAIR, LANE, LANE), F32)],
        compiler_params=_params(("arbitrary",)),
    )(xa, proj, proj, states, dy, *small)


def _attn_block(q, kv_prev, kv_cur, cq, sq, ck, sk, sinks, rot, first_block):
    left, right = _half_masks()
    k2 = jnp.concatenate([kv_prev[:, :256], kv_cur[:, :256]], axis=0)
    v2 = jnp.concatenate([kv_prev[:, 256:], kv_cur[:, 256:]], axis=0)
    ri, ci = _iota((LANE, LANE), 0), _iota((LANE, LANE), 1)
    dup = [((ri < 64) & (ci % 64 == ri)).astype(BF16), ((ri >= 64) & (ci % 64 == ri - 64)).astype(BF16)]

    rot16 = rot.astype(BF16)

    def rope(tt, c, s):
        return tt * c + _spread(tt, rot16) * s

    kd, vd = [], []
    for j in range(4):
        sl = slice(LANE * (j // 2), LANE * (j // 2 + 1))
        kd.append(_bdot(rope(k2[:, sl], ck, sk), dup[j % 2], "nn"))
        vd.append(_bdot(v2[:, sl], dup[j % 2], "nn"))
    qi, si = _iota((2 * CHUNK, 2 * CHUNK), 0) % CHUNK, _iota((2 * CHUNK, 2 * CHUNK), 1)
    valid = (si > qi) & (si <= qi + CHUNK) & jnp.logical_or(si >= CHUNK, jnp.logical_not(first_block))
    upper = _iota((2 * CHUNK, 1), 0) < CHUNK
    lanes = _iota((1, LANE), 1)
    outs = []
    for pr in range(N_PAIR):
        qr = rope(q[:, LANE * pr:LANE * (pr + 1)], cq, sq)
        lg = _bdot(jnp.concatenate([qr * left, qr * right], axis=0), kd[pr // 2], "nt") * 0.125
        lg = jnp.where(valid, lg, -jnp.inf)
        s1 = jnp.sum(jnp.where(lanes == 2 * pr, sinks, 0.0), axis=1, keepdims=True)
        s2 = jnp.sum(jnp.where(lanes == 2 * pr + 1, sinks, 0.0), axis=1, keepdims=True)
        sink = jnp.where(upper, s1, s2)
        mx = lax.stop_gradient(jnp.maximum(jnp.max(lg, axis=-1, keepdims=True), sink))
        e = jnp.exp(lg - mx)
        probs = e / (jnp.sum(e, axis=-1, keepdims=True) + jnp.exp(sink - mx))
        o2 = _bdot(probs, vd[pr // 2], "nn")
        outs.append(o2[:CHUNK] * left + o2[CHUNK:] * right)
    return jnp.concatenate(outs, axis=1)


def _attn_fwd(qkv, cos, sin, sinks, rot, name):
    t = qkv.shape[0]
    nb = t // CHUNK

    def body(q_ref, kvp_ref, kvc_ref, cq_ref, sq_ref, cp_ref, sp_ref, sinks_ref, rot_ref, o_ref):
        ck = jnp.concatenate([cp_ref[...], cq_ref[...]], axis=0)
        sk = jnp.concatenate([sp_ref[...], sq_ref[...]], axis=0)
        o_ref[...] = _attn_block(q_ref[...], kvp_ref[...], kvc_ref[...], cq_ref[...], sq_ref[...], ck, sk,
                                 sinks_ref[...], rot_ref[...], pl.program_id(0) == 0).astype(BF16)

    prev = lambda n: jnp.maximum(n - 1, 0)
    return _pallas(
        body, name=name, grid=(nb,),
        in_specs=[pl.BlockSpec((CHUNK, D), lambda n: (n, 0)),
                  pl.BlockSpec((CHUNK, 512), lambda n: (prev(n), 2)), pl.BlockSpec((CHUNK, 512), lambda n: (n, 2)),
                  pl.BlockSpec((CHUNK, LANE), lambda n: (n, 0)), pl.BlockSpec((CHUNK, LANE), lambda n: (n, 0)),
                  pl.BlockSpec((CHUNK, LANE), lambda n: (prev(n), 0)), pl.BlockSpec((CHUNK, LANE), lambda n: (prev(n), 0)),
                  _whole(sinks), _whole(rot)],
        out_specs=pl.BlockSpec((CHUNK, D), lambda n: (n, 0)), out_shape=SDS((t, D), BF16),
        compiler_params=_params(("parallel",)),
    )(qkv, qkv, qkv, cos, sin, cos, sin, sinks, rot)


def _attn_bwd(qkv, do, cos, sin, sinks, rot, name):
    t = qkv.shape[0]
    nb = t // CHUNK

    def body(q_ref, kvp_ref, kvc_ref, do_ref, cq_ref, sq_ref, cp_ref, sp_ref, sinks_ref, rot_ref,
             dq_ref, dkv_ref, dbq_ref, dbkv_ref, dsink_ref, carry_ref):
        n = pl.program_id(0)

        @pl.when(n == 0)
        def _():
            carry_ref[...] = jnp.zeros_like(carry_ref)
            dbq_ref[...] = jnp.zeros_like(dbq_ref)
            dbkv_ref[...] = jnp.zeros_like(dbkv_ref)
            dsink_ref[...] = jnp.zeros_like(dsink_ref)

        @pl.when(n < nb)
        def _():
            ck = jnp.concatenate([cp_ref[...], cq_ref[...]], axis=0)
            sk = jnp.concatenate([sp_ref[...], sq_ref[...]], axis=0)
            f = lambda q, kvp, kvc, s: _attn_block(q, kvp, kvc, cq_ref[...], sq_ref[...], ck, sk, s, rot_ref[...], n == 0)
            _, vjp = jax.vjp(f, q_ref[...], kvp_ref[...], kvc_ref[...], sinks_ref[...])
            dq, dkvp, dkvc, ds = vjp(do_ref[...].astype(F32))
            done = carry_ref[...] + dkvp
            dq_ref[...] = dq.astype(BF16)
            dkv_ref[...] = done.astype(BF16)
            dbq_ref[...] += jnp.sum(dq, axis=0, keepdims=True)
            dsink_ref[...] += ds
            carry_ref[...] = dkvc

            @pl.when(n > 0)
            def _():
                dbkv_ref[...] += jnp.sum(done, axis=0, keepdims=True)

        @pl.when(n == nb)
        def _():
            done = carry_ref[...]
            dkv_ref[...] = done.astype(BF16)
            dbkv_ref[...] += jnp.sum(done, axis=0, keepdims=True)

    cur = lambda n: jnp.minimum(n, nb - 1)
    prev = lambda n: jnp.maximum(jnp.minimum(n, nb - 1) - 1, 0)
    fin = lambda n: jnp.maximum(n - 1, 0)
    outs = _pallas(
        body, name=name, grid=(nb + 1,),
        in_specs=[pl.BlockSpec((CHUNK, D), lambda n: (cur(n), 0)),
                  pl.BlockSpec((CHUNK, 512), lambda n: (prev(n), 2)), pl.BlockSpec((CHUNK, 512), lambda n: (cur(n), 2)),
                  pl.BlockSpec((CHUNK, D), lambda n: (cur(n), 0)),
                  pl.BlockSpec((CHUNK, LANE), lambda n: (cur(n), 0)), pl.BlockSpec((CHUNK, LANE), lambda n: (cur(n), 0)),
                  pl.BlockSpec((CHUNK, LANE), lambda n: (prev(n), 0)), pl.BlockSpec((CHUNK, LANE), lambda n: (prev(n), 0)),
                  _whole(sinks), _whole(rot)],
        out_specs=[pl.BlockSpec((CHUNK, D), lambda n: (cur(n), 0)), pl.BlockSpec((CHUNK, 512), lambda n: (fin(n), 0)),
                   pl.BlockSpec((1, D), lambda n: (0, 0)), pl.BlockSpec((1, 512), lambda n: (0, 0)), _whole(sinks)],
        out_shape=[SDS((t, D), BF16), SDS((t, 512), BF16), SDS((1, D), F32), SDS((1, 512), F32), SDS(sinks.shape, F32)],
        scratch_shapes=[pltpu.VMEM((CHUNK, 512), F32)],
        compiler_params=_params(("arbitrary",)),
    )(qkv, qkv, qkv, do, cos, sin, cos, sin, sinks, rot)
    dq, dkv, dbq, dbkv, dsinks = outs
    return jnp.concatenate([dq, dkv], axis=1), jnp.concatenate([dbq, dbkv], axis=1), dsinks


def _loss_head(h, tgt, g, name, tm=512):
    t = h.shape[0]

    def body(h_ref, t_ref, g_ref, loss_ref, dh_ref, dhb_ref, dg_ref):
        def f(hv, gv):
            err = _rms_fn(hv, gv)[0] - t_ref[...]
            return 0.5 * jnp.sum(jnp.mean(err * err, axis=-1, keepdims=True), axis=0, keepdims=True)

        loss, vjp = jax.vjp(f, h_ref[...], g_ref[...])
        dh, dg = vjp(jnp.ones((1, 1), F32))
        dh_ref[...] = dh
        dhb_ref[...] = dh.astype(BF16)
        first = pl.program_id(0) == 0

        @pl.when(first)
        def _():
            loss_ref[...] = loss
            dg_ref[...] = dg

        @pl.when(jnp.logical_not(first))
        def _():
            loss_ref[...] += loss
            dg_ref[...] += dg

    return _pallas(
        body, name=name, grid=(t // tm,),
        in_specs=[_tok(D, tm), _tok(D, tm), _whole(g)],
        out_specs=[pl.BlockSpec((1, 1), lambda i: (0, 0)), _tok(D, tm), _tok(D, tm), _whole(g)],
        out_shape=[SDS((1, 1), F32), SDS((t, D), F32), SDS((t, D), BF16), SDS(g.shape, F32)],
        compiler_params=_params(("arbitrary",)),
    )(h, tgt, g)


def _res_half(acc, res):
    return (res + 0.5 * acc,)


def _res_full(acc, res):
    return (res + acc,)


def _half(acc):
    return (0.5 * acc,)


def _ffn_in(h, g, w_in, name, tm=1024):
    t = h.shape[0]
    tm = min(tm, t)

    def body(h_ref, g_ref, w_ref, n_ref, pre_ref, act_ref, n_scr):
        @pl.when(pl.program_id(1) == 0)
        def _():
            n_scr[...] = _rms_fn(h_ref[...], g_ref[...])[0].astype(BF16)
            n_ref[...] = n_scr[...]

        a = n_scr[...]
        gate = lax.dot_general(a, w_ref[0], _DIMS["nt"], preferred_element_type=F32)
        up = lax.dot_general(a, w_ref[1], _DIMS["nt"], preferred_element_type=F32)
        pre_ref[0] = gate.astype(BF16)
        pre_ref[1] = up.astype(BF16)
        act_ref[...] = (gate * jax.nn.sigmoid(gate) * up).astype(BF16)

    pair = pl.BlockSpec((2, None, tm, FF_SHARD), lambda i, j: (0, j, i, 0))
    rows = pl.BlockSpec((tm, D), lambda i, j: (i, 0))
    return _pallas(
        body, name=name, grid=(t // tm, 4),
        in_specs=[rows, _whole(g), pl.BlockSpec((2, None, FF_SHARD, D), lambda i, j: (0, j, 0, 0))],
        out_specs=[rows, pair, pl.BlockSpec((None, tm, FF_SHARD), lambda i, j: (j, i, 0))],
        out_shape=[SDS((t, D), BF16), SDS((2, 4, t, FF_SHARD), BF16), SDS((4, t, FF_SHARD), BF16)],
        scratch_shapes=[pltpu.VMEM((tm, D), BF16)],
        compiler_params=_params(("parallel", "arbitrary")),
    )(h, g, w_in.reshape(2, 4, FF_SHARD, D))


def _ffn_dact(dhb, w_out, pre, name, tm=1024, deps=()):
    t = dhb.shape[0]
    tm = min(tm, t)

    def body(d_ref, w_ref, pre_ref, *rest):
        o_ref = rest[-1]
        dact = 0.5 * lax.dot_general(d_ref[...], w_ref[...], _DIMS["nt"], preferred_element_type=F32)
        gate, up = pre_ref[0].astype(F32), pre_ref[1].astype(F32)
        sg = jax.nn.sigmoid(gate)
        o_ref[0] = (dact * up * (sg * (1.0 + gate * (1.0 - sg)))).astype(BF16)
        o_ref[1] = (dact * (gate * sg)).astype(BF16)

    pair = pl.BlockSpec((2, None, tm, FF_SHARD), lambda i, j: (0, j, i, 0))
    return _pallas(
        body, name=name, grid=(t // tm, 4),
        in_specs=[pl.BlockSpec((tm, D), lambda i, j: (i, 0)), pl.BlockSpec((None, FF_SHARD, D), lambda i, j: (j, 0, 0)), pair]
        + [ANY_SPEC] * len(deps),
        out_specs=pair, out_shape=SDS((2, 4, t, FF_SHARD), BF16),
        compiler_params=_params(("parallel", "parallel")),
    )(dhb, w_out, pre, *deps)


def _ffn_fwd(h, g, w_in, w_out, tag, deps=()):
    n, pre, act = _ffn_in(h, g, w_in, f"{tag}_in")
    w_out = w_out() if callable(w_out) else w_out
    out = _mm(act, w_out, reduce_j=True, tk=FF_SHARD, epi=_res_half, extras=(h[None],), name=f"{tag}_out")[0][0]
    return out, (h, n, pre, act)


def _ffn_bwd(dh, dhb, saved, g, w_in, w_out, tag, deps=(), hook=None, weights_hook=None):
    h, n, pre, act = saved
    t = h.shape[0]
    dpre = _ffn_dact(dhb, w_out, pre, f"{tag}_dact", deps=deps).reshape(N_DEV, t, FF_SHARD)
    dw_out = _mm(act, dhb[None], ta=True, tm=FF_SHARD, epi=_half, out_dtypes=(BF16,), deps=hook(dpre) if hook else (),
                 name=f"{tag}_dwout")[0]
    dw_in = _mm(dpre, n[None], ta=True, tm=FF_SHARD, tk=2048, out_dtypes=(BF16,), name=f"{tag}_dwin")[0]
    if weights_hook:
        weights_hook(dw_in, dw_out)
    dh_in, dhb_in, dg = _mm_drms(dpre, w_in, h, g, dh, f"{tag}_dn", FF_SHARD, tb=False)
    return dh_in, dhb_in, dg, dw_in, dw_out


def _ple_fwd(h, g, pb, w_gate, w_proj, tag):
    t = h.shape[0]
    tm = 512
    n = _rms(h, g, f"{tag}_rms")
    e = _mm(pb[None], w_proj[None], name=f"{tag}_proj")[0][0]
    z = _mm(n[None], w_gate[None], name=f"{tag}_gate")[0][0]
    out = _rowop(lambda zz, ee, hh: (hh + _ple_fn(zz, ee)[0],), [(z, _tok(D, tm)), (e, _tok(D, tm)), (h, _tok(D, tm))], [],
                 [((t, D), F32, _tok(D, tm))], grid=(t // tm,), name=f"{tag}_mix")[0]
    return out, (h, n, e, z)


def _ple_bwd(dh, dhb, saved, g, pb, w_gate, tag, deps=()):
    h, n, e, z = saved
    t = h.shape[0]
    tm = 512
    dz, de = _rowop_bwd(_ple_fn, [(z, _tok(D, tm)), (e, _tok(D, tm))], [], [(dh, _tok(D, tm))], [(0,), (1,)],
                        [((t, D), (BF16,), _tok(D, tm)), ((t, D), (BF16,), _tok(D, tm))], grid=(t // tm,), name=f"{tag}_dmix",
                        deps=deps)
    dw_proj = _mm(pb[None], de[None], ta=True, out_dtypes=(BF16,), name=f"{tag}_dwproj")[0][0]
    dw_gate = _mm(n[None], dz[None], ta=True, out_dtypes=(BF16,), name=f"{tag}_dwgate")[0][0]
    dh_in, dhb_in, dg = _mm_drms(dz[None], w_gate[None], h, g, dh, f"{tag}_dn", 1024)
    return dh_in, dhb_in, dg, dw_gate, dw_proj


def _hyb_fwd(h, w, tag, after_in=None):
    t = h.shape[0]
    tm = 512
    hn = _rms(h, w["norm_mix"], f"{tag}_rms")
    proj = _mm(hn[None], w["hyb_in"][None], tn=512, name=f"{tag}_in")[0][0]
    if after_in:
        after_in()
    u1 = _dwconv([(proj, 0), (proj, D // LANE)], w["conv_w"], w["conv_b"], width=CONV_W, glu=True, silu=False, cb=LANE,
                 name=f"{tag}_conv")
    u = _rowop(_lnswish_fn, [(u1, _tok(D, tm))], [w["ln_g"], w["ln_b"]], [((t, D), BF16, _tok(D, tm))], grid=(t // tm,),
               name=f"{tag}_ln")[0]
    xa = _dwconv([(proj, 3 * D // LANE)], w["sconv_w"], w["sconv_b"], width=SSM_CONV, glu=False, silu=True, cb=LANE,
                 name=f"{tag}_sconv")
    y, states = _ssd_fwd(xa, proj, w["dt_bias"], w["a_log"], w["d_skip"], w["ssm_norm"], f"{tag}_ssd")
    mixed = jnp.stack([u, y], axis=0)
    out = _mm(mixed, w["hyb_out"], reduce_j=True, epi=_res_full, extras=(h[None],), name=f"{tag}_out")[0][0]
    return out, (h, hn, proj, u1, xa, states, mixed)


def _hyb_bwd(dh, dhb, saved, w, tag):
    h, hn, proj, u1, xa, states, mixed = saved
    t = h.shape[0]
    tm = 512
    dmix = _mm(dhb[None], w["hyb_out"], tb=True, name=f"{tag}_dmix")[0]
    dw_out = _mm(mixed, dhb[None], ta=True, out_dtypes=(BF16,), name=f"{tag}_dwout")[0]
    du1, dln_g, dln_b = _rowop_bwd(_lnswish_fn, [(u1, _tok(D, tm))], [w["ln_g"], w["ln_b"]], [(dmix[0], _tok(D, tm))], [(0,)],
                                   [((t, D), (F32,), _tok(D, tm))], grid=(t // tm,), name=f"{tag}_dln")
    dval, dgate, dconv_w, dconv_b = _dwconv_bwd([(proj, 0), (proj, D // LANE)], w["conv_w"], w["conv_b"], du1,
                                                width=CONV_W, glu=True, silu=False, cb=LANE, name=f"{tag}_dconv")
    dxa, ddt, dz, g_dtb, g_alog, g_dsk, g_ng = _ssd_bwd(xa, proj, states, dmix[1], w["dt_bias"], w["a_log"], w["d_skip"],
                                                         w["ssm_norm"], f"{tag}_dssd")
    dxbc, dsconv_w, dsconv_b = _dwconv_bwd([(proj, 3 * D // LANE)], w["sconv_w"], w["sconv_b"], dxa, width=SSM_CONV,
                                           glu=False, silu=True, cb=LANE, name=f"{tag}_dsconv")
    dproj = jnp.concatenate([dval, dgate, dz, dxbc, ddt, jnp.zeros((t, HYB_PAD - DT_COL - LANE), BF16)], axis=1)
    dh_in, dhb_in, dg = _mm_drms(dproj[None], w["hyb_in"][None], h, w["norm_mix"], dh, f"{tag}_dhn", 1024)
    dw_in = _mm(hn[None], dproj[None], ta=True, tn=512, out_dtypes=(BF16,), name=f"{tag}_dwin")[0][0]
    grads = dict(norm_mix=dg, hyb_in=dw_in, hyb_out=dw_out, conv_w=dconv_w, conv_b=dconv_b, ln_g=dln_g, ln_b=dln_b,
                 sconv_w=dsconv_w, sconv_b=dsconv_b, dt_bias=g_dtb, a_log=g_alog, d_skip=g_dsk, ssm_norm=g_ng)
    return dh_in, dhb_in, grads


def _bias_epi(acc, row):
    return (acc + row,)


def _res_bias_epi(acc, res, row):
    return (res + acc + row,)


def _att_fwd(h, w, tables, tag):
    cos, sin, rot = tables
    hn = _rms(h, w["norm_mix"], f"{tag}_rms")
    qkv = _mm(hn[None], w["qkv"][None], tb=True, tn=512, epi=_bias_epi, rows=(w["b_qkv"],), name=f"{tag}_qkv")[0][0]
    o = _attn_fwd(qkv, cos, sin, w["sinks"], rot, f"{tag}_core")
    out = _mm(o[None], w["w_o"][None], epi=_res_bias_epi, extras=(h[None],), rows=(w["b_o"],), name=f"{tag}_out")[0][0]
    return out, (h, hn, qkv, o)


def _att_bwd(dh, dhb, saved, w, tables, tag):
    cos, sin, rot = tables
    h, hn, qkv, o = saved
    t = h.shape[0]
    tm = 512
    do = _mm(dhb[None], w["w_o"][None], tb=True, out_dtypes=(BF16,), name=f"{tag}_do")[0][0]
    dw_o = _mm(o[None], dhb[None], ta=True, out_dtypes=(BF16,), name=f"{tag}_dwo")[0][0]
    db_o = _rowop_bwd(lambda xx, bb: (xx + bb,), [(dh, _tok(D, tm))], [w["b_o"]], [(dh, _tok(D, tm))], [], [],
                      grid=(t // tm,), name=f"{tag}_dbo")[0]
    dqkv, db_qkv, dsinks = _attn_bwd(qkv, do, cos, sin, w["sinks"], rot, f"{tag}_dcore")
    dh_in, dhb_in, dg = _mm_drms(dqkv[None], w["qkv"][None], h, w["norm_mix"], dh, f"{tag}_dhn", 512, tb=False)
    dw_qkv = _mm(dqkv[None], hn[None], ta=True, tm=512, out_dtypes=(BF16,), name=f"{tag}_dwqkv")[0][0]
    grads = dict(norm_mix=dg, qkv=dw_qkv, b_qkv=db_qkv, sinks=dsinks, w_o=dw_o, b_o=db_o)
    return dh_in, dhb_in, grads


def _rope_tables(t):
    inv = ROPE_THETA ** (-jnp.arange(0, 64, 2, dtype=F32) / 64)
    ang = jnp.arange(t, dtype=F32)[:, None] * inv[None, :]
    cos, sin = jnp.tile(jnp.cos(ang), (1, 4)), jnp.tile(jnp.sin(ang), (1, 4))
    rot = np.zeros((LANE, LANE), np.float32)
    for j in range(LANE):
        if j % 64 < 32:
            rot[j + 32, j] = -1.0
        else:
            rot[j - 32, j] = 1.0
    return cos, sin, jnp.asarray(rot)


def _local_step(x, p, tgt, layers, final_norm):
    _restart_chain()
    tables = _rope_tables(x.shape[0])
    pb = p.astype(BF16)
    h, saved = x, []
    for i, w in enumerate(layers):
        h, s = _layer_fwd(i, h, w, pb[i], tables)
        saved.append(s)
    loss, dh, dhb, d_final = _loss_head(h, tgt, final_norm, "loss_head")
    grads = [None] * len(layers)
    for i in reversed(range(len(layers))):
        dh, dhb, head = _layer_bwd_head(i, dh, dhb, saved[i], layers[i], pb[i])
        dh, dhb, tail = _layer_bwd_tail(i, dh, dhb, saved[i], layers[i], tables)
        grads[i] = {**head, **tail}
    return loss[0, 0], dh, grads, d_final


def _layer_fwd(i, h, w, pb, tables, deps=()):
    s = {}
    h, s["ffn1"] = _ffn_fwd(h, w["norm_ffn1"], w["ffn1_in"], w["ffn1_out"], f"l{i}_ffn1", deps=deps)
    if i % 2 == 0:
        h, s["mix"] = _hyb_fwd(h, w, f"l{i}_hyb")
    else:
        h, s["mix"] = _att_fwd(h, w, tables, f"l{i}_att")
    h, s["ffn2"] = _ffn_fwd(h, w["norm_ffn2"], w["ffn2_in"], w["ffn2_out"], f"l{i}_ffn2")
    h, s["ple"] = _ple_fwd(h, w["ple_norm"], pb, w["ple_gate"], w["ple_proj"], f"l{i}_ple")
    return h, s


def _layer_bwd_head(i, dh, dhb, s, w, pb, deps=()):
    g = {}
    dh, dhb, g["ple_norm"], g["ple_gate"], g["ple_proj"] = _ple_bwd(dh, dhb, s["ple"], w["ple_norm"], pb, w["ple_gate"],
                                                                    f"l{i}_ple", deps=deps)
    return dh, dhb, g


def _layer_bwd_tail(i, dh, dhb, s, w, tables, deps=()):
    g = {}
    dh, dhb, g["norm_ffn2"], g["ffn2_in"], g["ffn2_out"] = _ffn_bwd(dh, dhb, s["ffn2"], w["norm_ffn2"], w["ffn2_in"],
                                                                    w["ffn2_out"], f"l{i}_ffn2", deps=deps)
    if i % 2 == 0:
        dh, dhb, gm = _hyb_bwd(dh, dhb, s["mix"], w, f"l{i}_hyb")
    else:
        dh, dhb, gm = _att_bwd(dh, dhb, s["mix"], w, tables, f"l{i}_att")
    g.update(gm)
    dh, dhb, g["norm_ffn1"], g["ffn1_in"], g["ffn1_out"] = _ffn_bwd(dh, dhb, s["ffn1"], w["norm_ffn1"], w["ffn1_in"],
                                                                    w["ffn1_out"], f"l{i}_ffn1")
    return dh, dhb, g


def _cols(g):
    full = jnp.moveaxis(g, 0, -2)
    return full.reshape(*full.shape[:-2], N_DEV * g.shape[-1])


def _uncols(full):
    split = full.reshape(*full.shape[:-1], N_DEV, full.shape[-1] // N_DEV)
    return jnp.moveaxis(split, -2, 0)


def _lane_pad(v):
    return jnp.pad(v, ((0, 0), (0, LANE - v.shape[1])))


def _build_layers(gw, gs, rep):
    return [_build_layer(i, gw, gs, rep) for i in range(2)]


def _build_layer(i, gw, gs, rep, parts=("ffn1", "mix", "ffn2", "ple")):
    w = {}
    for f in ("ffn1", "ffn2"):
        if f in parts:
            w[f"norm_{f}"] = rep[f"norm_{f}"][i][None]
            w[f"{f}_in"] = gw[f"{f}_w_in", i]
            w[f"{f}_out"] = gw[f"{f}_w_out", i].reshape(4, FF_SHARD, D)
    if "ple" in parts:
        w["ple_norm"] = rep["ple_norm"][i][None]
        w["ple_gate"] = gw["ple_gate_w", i].reshape(D, D)
        w["ple_proj"] = _cols(gw["ple_proj_w", i])
    if "mix" not in parts:
        return w
    w["norm_mix"] = rep["norm_mix"][i][None]
    if i == 0:
        w["hyb_in"] = jnp.pad(_cols(gw["hyb_w_in", 0]), ((0, 0), (0, HYB_PAD - HYB_IN)))
        w["hyb_out"] = gw["hyb_w_out", 0].reshape(2, D, D)
        w["conv_w"] = _cols(gs["conv_dw_w"][:, 0])
        w["sconv_w"] = _cols(gs["ssm_conv_w"][:, 0])
        w["conv_b"], w["ln_g"], w["ln_b"] = rep["conv_dw_b"], rep["conv_ln_g"], rep["conv_ln_b"]
        w["sconv_b"], w["ssm_norm"] = rep["ssm_conv_b"], rep["ssm_norm"]
        w["dt_bias"], w["a_log"] = _lane_pad(rep["ssm_dt_bias"]), _lane_pad(rep["ssm_a_log"])
        w["d_skip"] = jnp.repeat(rep["ssm_d"], D // SSM_HEADS, axis=1)
    else:
        w["qkv"] = gw["att_w_qkv", 0].reshape(-1, D)
        w["w_o"] = gw["att_w_o", 0].reshape(D, D)
        w["b_qkv"] = gs["att_b_qkv"][:, 0].reshape(1, -1)
        w["b_o"] = gs["att_b_o"][:, 0].reshape(1, -1)
        w["sinks"] = _lane_pad(rep["att_sinks"])
    return w


def _big_grads(i, g):
    big = {}
    for f in ("ffn1", "ffn2"):
        if f"{f}_in" in g:
            big[f"{f}_w_in", i] = g[f"{f}_in"]
            big[f"{f}_w_out", i] = g[f"{f}_out"].reshape(N_DEV, D_FF // N_DEV, D)
    if "ple_gate" in g:
        big["ple_gate_w", i] = g["ple_gate"].reshape(N_DEV, D // N_DEV, D)
        big["ple_proj_w", i] = _uncols(g["ple_proj"])
    if "hyb_in" in g:
        big["hyb_w_in", 0] = _uncols(g["hyb_in"][:, :HYB_IN])
        big["hyb_w_out", 0] = g["hyb_out"].reshape(N_DEV, 2 * D // N_DEV, D)
    if "qkv" in g:
        big["att_w_qkv", 0] = g["qkv"].reshape(N_DEV, -1, D)
        big["att_w_o", 0] = g["w_o"].reshape(N_DEV, D // N_DEV, D)
    return big


def _collect_grads(grads, d_final):
    g0, g1 = grads
    big, small = {**_big_grads(0, g0), **_big_grads(1, g1)}, {}
    for f in ("ffn1", "ffn2"):
        small[f"norm_{f}"] = jnp.concatenate([g[f"norm_{f}"] for g in grads], axis=0)
    small["norm_mix"] = jnp.concatenate([g["norm_mix"] for g in grads], axis=0)
    small["ple_norm"] = jnp.concatenate([g["ple_norm"] for g in grads], axis=0)
    small["conv_dw_w"] = g0["conv_w"][None]
    small["conv_dw_b"], small["conv_ln_g"], small["conv_ln_b"] = g0["conv_b"], g0["ln_g"], g0["ln_b"]
    small["ssm_conv_w"] = g0["sconv_w"][None]
    small["ssm_conv_b"], small["ssm_norm"] = g0["sconv_b"], g0["ssm_norm"]
    small["ssm_dt_bias"], small["ssm_a_log"] = g0["dt_bias"][:, :SSM_HEADS], g0["a_log"][:, :SSM_HEADS]
    small["ssm_d"] = g0["d_skip"].reshape(1, SSM_HEADS, D // SSM_HEADS).sum(axis=-1)
    small["att_b_qkv"], small["att_b_o"] = g1["b_qkv"], g1["b_o"]
    small["att_sinks"] = g1["sinks"][:, :SSM_HEADS]
    small["final_norm"] = d_final[0]
    return big, small


MESH = pl.DeviceIdType.MESH


def _place():
    return lax.axis_index("x"), lax.axis_index("y"), lax.axis_index("c")


def _all_gather(blocks, space, name):
    nb = len(blocks)

    def body(*refs):
        x_refs, out_refs, (send_sems, recv_sems, local_sem) = refs[:nb], refs[nb:2 * nb], refs[2 * nb:]
        x, y, c = _place()
        me, sibling = (x, y, c), (x, y, 1 - c)
        chips = [(1 - x, y), (x, 1 - y), (1 - x, 1 - y)]

        def copies(k, blk, to, own=False):
            idx = 4 * blk[0] + 2 * blk[1] + blk[2]
            return [pltpu.make_async_remote_copy(src_ref=x_ref if own else out_ref.at[idx], dst_ref=out_ref.at[idx],
                                                 send_sem=send_sems.at[k, b], recv_sem=recv_sems.at[k, b], device_id=to,
                                                 device_id_type=MESH) for b, (x_ref, out_ref) in enumerate(zip(x_refs, out_refs))]

        mine = [pltpu.make_async_copy(x_ref, out_ref.at[4 * x + 2 * y + c], local_sem.at[b])
                for b, (x_ref, out_ref) in enumerate(zip(x_refs, out_refs))]
        first = copies(0, me, sibling, own=True)
        for j, chip in enumerate(chips):
            first += copies(1 + j, me, (*chip, c), own=True)
        for cp in mine + first:
            cp.start()
        passed = []
        for j, chip in enumerate(chips):
            for cp in copies(1 + j, (*chip, c), me):
                cp.wait_recv()
            onward = copies(4 + j, (*chip, c), sibling)
            for cp in onward:
                cp.start()
            passed += onward
        for cp in copies(0, sibling, me):
            cp.wait_recv()
        for j, chip in enumerate(chips):
            for cp in copies(4 + j, (*chip, 1 - c), me):
                cp.wait_recv()
        for cp in first + passed:
            cp.wait_send()
        for cp in mine:
            cp.wait()

    spec = pl.BlockSpec(memory_space=space)
    return _pallas(
        body, name=name, out_shape=[SDS((N_DEV,) + b.shape, b.dtype) for b in blocks],
        in_specs=[spec] * nb, out_specs=[spec] * nb,
        scratch_shapes=[pltpu.SemaphoreType.DMA((7, nb)), pltpu.SemaphoreType.DMA((7, nb)), pltpu.SemaphoreType.DMA((nb,))],
    )(*blocks)


HBM_SPEC = pl.BlockSpec(memory_space=pltpu.HBM)
SEM_SPEC = pl.BlockSpec(memory_space=pltpu.SEMAPHORE)
EFFECT = pltpu.SideEffectType.DATAFLOW_SIDE_EFFECTING


def _plan_descriptors(plan, srcs, lands, send_sems, recv_sems, local_sems, arriving):
    remote, local = plan(*_place())

    def pick(si, slot):
        ref = lands[si[1]] if isinstance(si, tuple) else srcs[si]
        return ref if slot is None else ref.at[slot]

    rem = [pltpu.make_async_remote_copy(src_ref=pick(si, ss), dst_ref=lands[li].at[rs if arriving else ds],
                                        send_sem=send_sems.at[k], recv_sem=recv_sems.at[k], device_id=dev, device_id_type=MESH)
           for k, (si, ss, li, ds, dev, rs) in enumerate(remote)]
    loc = [pltpu.make_async_copy(pick(si, ss), lands[li].at[ds], local_sems.at[k])
           for k, (si, ss, li, ds) in enumerate(local)]
    return rem, loc


def _plan_counts(plan):
    remote, local = plan(0, 0, 0)
    return len(remote), max(len(local), 1)


def _exchange_start(srcs, land_shapes, plan, name, lands=None):
    ns, nl = len(srcs), len(lands if lands is not None else land_shapes)
    n_remote, n_local = _plan_counts(plan)
    if lands is None:
        lands = [pltpu.with_memory_space_constraint(lax.empty(s.shape, s.dtype), pltpu.HBM) for s in land_shapes]
    lands = list(lands)
    srcs = [pltpu.with_memory_space_constraint(s, pltpu.HBM) for s in srcs]

    def body(*refs):
        src_refs, land_refs = refs[:ns], refs[ns:ns + nl]
        send_sems, recv_sems, local_sems = refs[ns + nl:ns + nl + 3]
        token = refs[-1]
        rem, loc = _plan_descriptors(plan, src_refs, land_refs, send_sems, recv_sems, local_sems, arriving=False)
        for cp in loc + rem:
            cp.start()
        token[...] = jnp.zeros_like(token)

    outs = _pallas(
        body, name=name,
        out_shape=[pltpu.SemaphoreType.DMA((n_remote,)), pltpu.SemaphoreType.DMA((n_remote,)), pltpu.SemaphoreType.DMA((n_local,))]
        + [pltpu.HBM(a.shape, a.dtype) for a in srcs + lands] + [SDS((8, LANE), F32)],
        in_specs=[HBM_SPEC] * (ns + nl),
        out_specs=[SEM_SPEC] * 3 + [HBM_SPEC] * (ns + nl) + [pl.BlockSpec(memory_space=pltpu.VMEM)],
        input_output_aliases={i: 3 + i for i in range(ns + nl)},
        compiler_params=pltpu.CompilerParams(has_side_effects=EFFECT),
    )(*srcs, *lands)
    return (outs[:3], outs[3:3 + ns], outs[3 + ns:3 + ns + nl]), outs[-1]


def _exchange_wait(state, after, plan, name):
    sems, srcs, lands = state
    ns, nl = len(srcs), len(lands)

    def body(*refs):
        src_refs, land_refs = refs[:ns], refs[ns:ns + nl]
        send_sems, recv_sems, local_sems = refs[ns + nl:ns + nl + 3]
        rem, loc = _plan_descriptors(plan, src_refs, land_refs, send_sems, recv_sems, local_sems, arriving=True)
        for cp in rem:
            cp.wait_send()
            cp.wait_recv()
        for cp in loc:
            cp.wait()

    outs = _pallas(
        body, name=name, out_shape=[pltpu.HBM(a.shape, a.dtype) for a in list(srcs) + list(lands)],
        in_specs=[HBM_SPEC] * (ns + nl) + [SEM_SPEC] * 3 + [ANY_SPEC] * (after is not None), out_specs=[HBM_SPEC] * (ns + nl),
        input_output_aliases={i: i for i in range(ns + nl)},
        compiler_params=pltpu.CompilerParams(has_side_effects=EFFECT),
    )(*srcs, *lands, *sems, *([after] if after is not None else []))
    return outs[:ns], outs[ns:]


def _gather_plan(nb):
    def plan(x, y, c):
        me = 4 * x + 2 * y + c
        peers = [(x, y, 1 - c), (1 - x, y, c), (x, 1 - y, c), (1 - x, 1 - y, c)]
        remote = [(b, None, b, me, peer, 4 * peer[0] + 2 * peer[1] + peer[2]) for b in range(nb) for peer in peers]
        return remote, [(b, None, b, me) for b in range(nb)]
    return plan


def _relay_plan(nb):
    def plan(x, y, c):
        chips = [(1 - x, y), (x, 1 - y), (1 - x, 1 - y)]
        remote = [(("land", b), 4 * cx + 2 * cy + c, b, 4 * cx + 2 * cy + c, (x, y, 1 - c), 4 * cx + 2 * cy + (1 - c))
                  for b in range(nb) for cx, cy in chips]
        return remote, []
    return plan


def _pair_plan(nb):
    def plan(x, y, c):
        return [(b, 2 * q + (1 - c), b, q, (x, y, 1 - c), q) for b in range(nb) for q in range(4)], []
    return plan


def _chip_plan(nb):
    def plan(x, y, c):
        own = 2 * x + y
        chips = [(1 - x, y), (x, 1 - y), (1 - x, 1 - y)]
        remote = [(b, 2 * cx + cy, b, own, (cx, cy, c), 2 * cx + cy) for b in range(nb) for cx, cy in chips]
        return remote, [(b, own, b, own) for b in range(nb)]
    return plan


def _row_tile(r, cap=4608):
    return max(d for d in range(16, min(r, cap) + 1, 16) if r % d == 0)


def _pair_add(parts, got, core, name):
    _, r, cdim = parts.shape
    tr = _row_tile(r)

    def body(core_ref, p_ref, g_ref, o_ref):
        o_ref[...] = (p_ref[...].astype(F32) + g_ref[...].astype(F32)).astype(o_ref.dtype)

    return pl.pallas_call(
        body, name=name, out_shape=SDS((4, r, cdim), BF16),
        grid_spec=pltpu.PrefetchScalarGridSpec(
            num_scalar_prefetch=1, grid=(4, r // tr),
            in_specs=[pl.BlockSpec((None, tr, cdim), lambda q, i, core_ref: (2 * q + core_ref[0], i, 0)),
                      pl.BlockSpec((None, tr, cdim), lambda q, i, core_ref: (q, i, 0))],
            out_specs=pl.BlockSpec((None, tr, cdim), lambda q, i, core_ref: (q, i, 0))),
        compiler_params=_params(("parallel", "parallel")),
    )(core, parts, got)


def _sum_slots(parts, name):
    nj, r, cdim = parts.shape
    tr = _row_tile(r)

    def body(p_ref, o_ref):
        acc = p_ref[0].astype(F32)
        for j in range(1, nj):
            acc = acc + p_ref[j].astype(F32)
        o_ref[...] = acc

    return _pallas(
        body, name=name, out_shape=SDS((r, cdim), F32), grid=(r // tr,),
        in_specs=[pl.BlockSpec((nj, tr, cdim), lambda i: (0, i, 0))], out_specs=pl.BlockSpec((tr, cdim), lambda i: (i, 0)),
        compiler_params=_params(("parallel",)),
    )(parts)


def _adamw_update(wv, gv, mv, vv):
    nm = ADAM_B1 * mv + (1.0 - ADAM_B1) * gv
    nv = ADAM_B2 * vv + (1.0 - ADAM_B2) * (gv * gv)
    m_hat = nm / (1.0 - ADAM_B1 ** ADAM_STEP)
    v_hat = nv / (1.0 - ADAM_B2 ** ADAM_STEP)
    return -ADAM_LR * (m_hat / (jnp.sqrt(v_hat) + ADAM_EPS) + ADAM_WD * wv), nm, nv


def _adamw_summed(w, m, v, by_chip, name):
    nl, r, cdim = w.shape
    tr = _row_tile(r, 512)
    nblk = r // tr

    def body(*refs):
        chip_refs, (w_ref, m_ref, v_ref, g_ref, d_ref, nm_ref, nv_ref) = refs[:nl], refs[nl:]
        layer = pl.program_id(0)
        gv = None
        for ll, c_ref in enumerate(chip_refs):
            s = c_ref[0].astype(F32)
            for q in range(1, 4):
                s = s + c_ref[q].astype(F32)
            gv = s if gv is None else jnp.where(layer == ll, s, gv)
        g_ref[...] = gv
        d_ref[...], nm_ref[...], nv_ref[...] = _adamw_update(w_ref[...], gv, m_ref[...], v_ref[...])

    def chip_map(ll):
        return lambda l, i: (0, jnp.where(l == ll, i, jnp.where(l > ll, nblk - 1, 0)), 0)

    spec = pl.BlockSpec((None, tr, cdim), lambda l, i: (l, i, 0))
    return _pallas(
        body, name=name, grid=(nl, nblk),
        in_specs=[pl.BlockSpec((4, tr, cdim), chip_map(ll)) for ll in range(nl)] + [spec] * 3,
        out_specs=[spec] * 4, out_shape=[SDS((nl, r, cdim), F32)] * 4,
        compiler_params=_params(("arbitrary", "arbitrary")),
    )(*by_chip, w, m, v)


def _adamw(w, g, m, v, name):
    shape = w.shape
    cdim = shape[-1]
    w2, g2, m2, v2 = (a.reshape(-1, cdim) for a in (w, g, m, v))
    r = w2.shape[0]
    tr = next(d for d in (512, 352, 256, 128, 64, 32, 16, 8, r) if r % d == 0)

    def body(w_ref, g_ref, m_ref, v_ref, d_ref, nm_ref, nv_ref):
        d_ref[...], nm_ref[...], nv_ref[...] = _adamw_update(w_ref[...], g_ref[...], m_ref[...], v_ref[...])

    spec = pl.BlockSpec((tr, cdim), lambda i: (i, 0))
    outs = _pallas(
        body, name=name, grid=(r // tr,), in_specs=[spec] * 4, out_specs=[spec] * 3, out_shape=[SDS((r, cdim), F32)] * 3,
        compiler_params=_params(("parallel",)),
    )(w2, g2, m2, v2)
    return tuple(o.reshape(shape) for o in outs)


WEIGHTS = ("norm_ffn1", "ffn1_w_in", "ffn1_w_out", "norm_mix", "norm_ffn2", "ffn2_w_in", "ffn2_w_out", "ple_norm", "ple_gate_w",
           "ple_proj_w", "hyb_w_in", "conv_dw_w", "conv_dw_b", "conv_ln_g", "conv_ln_b", "ssm_conv_w", "ssm_conv_b", "ssm_dt_bias",
           "ssm_a_log", "ssm_d", "ssm_norm", "hyb_w_out", "att_w_qkv", "att_b_qkv", "att_sinks", "att_w_o", "att_b_o", "final_norm")
BIG = ("ffn1_w_in", "ffn1_w_out", "ffn2_w_in", "ffn2_w_out", "ple_gate_w", "ple_proj_w", "hyb_w_in", "hyb_w_out", "att_w_qkv",
       "att_w_o")
SMALL_SHARDED = {"conv_dw_w": 2, "ssm_conv_w": 2, "att_b_qkv": 1, "att_b_o": 1}
SMALL = tuple(n for n in WEIGHTS if n not in BIG)
TRANSPOSED = ("ffn1_w_in", "ffn2_w_in", "att_w_qkv")
PACK_ROWS = 16


def _pack(arrays, lead=0):
    pieces = []
    for a in arrays:
        flat = a.reshape(*a.shape[:lead], -1)
        size = flat.shape[-1]
        padded = -(-size // (PACK_ROWS * LANE)) * PACK_ROWS * LANE
        flat = jnp.pad(flat, [(0, 0)] * lead + [(0, padded - size)])
        pieces.append(flat.reshape(*a.shape[:lead], padded // LANE, LANE))
    return jnp.concatenate(pieces, axis=lead)


def _unpack(buf, shapes, lead=0):
    out, row = [], 0
    for shape in shapes:
        size = math.prod(shape)
        rows = -(-size // (PACK_ROWS * LANE)) * PACK_ROWS
        piece = lax.slice_in_dim(buf, row, row + rows, axis=lead)
        piece = piece.reshape(*buf.shape[:lead], rows * LANE)
        out.append(lax.slice_in_dim(piece, 0, size, axis=lead).reshape(*buf.shape[:lead], *shape))
        row += rows
    return out


def kernel(x, p, norm_ffn1, ffn1_w_in, ffn1_w_out, norm_mix, norm_ffn2, ffn2_w_in, ffn2_w_out, ple_norm, ple_gate_w, ple_proj_w, hyb_w_in, conv_dw_w, conv_dw_b, conv_ln_g, conv_ln_b, ssm_conv_w, ssm_conv_b, ssm_dt_bias, ssm_a_log, ssm_d, ssm_norm, hyb_w_out, att_w_qkv, att_b_qkv, att_sinks, att_w_o, att_b_o, final_norm, loss_target, m_norm_ffn1, m_ffn1_w_in, m_ffn1_w_out, m_norm_mix, m_norm_ffn2, m_ffn2_w_in, m_ffn2_w_out, m_ple_norm, m_ple_gate_w, m_ple_proj_w, m_hyb_w_in, m_conv_dw_w, m_conv_dw_b, m_conv_ln_g, m_conv_ln_b, m_ssm_conv_w, m_ssm_conv_b, m_ssm_dt_bias, m_ssm_a_log, m_ssm_d, m_ssm_norm, m_hyb_w_out, m_att_w_qkv, m_att_b_qkv, m_att_sinks, m_att_w_o, m_att_b_o, m_final_norm, v_norm_ffn1, v_ffn1_w_in, v_ffn1_w_out, v_norm_mix, v_norm_ffn2, v_ffn2_w_in, v_ffn2_w_out, v_ple_norm, v_ple_gate_w, v_ple_proj_w, v_hyb_w_in, v_conv_dw_w, v_conv_dw_b, v_conv_ln_g, v_conv_ln_b, v_ssm_conv_w, v_ssm_conv_b, v_ssm_dt_bias, v_ssm_a_log, v_ssm_d, v_ssm_norm, v_hyb_w_out, v_att_w_qkv, v_att_b_qkv, v_att_sinks, v_att_w_o, v_att_b_o, v_final_norm):
    args = (norm_ffn1, ffn1_w_in, ffn1_w_out, norm_mix, norm_ffn2, ffn2_w_in, ffn2_w_out, ple_norm, ple_gate_w, ple_proj_w, hyb_w_in, conv_dw_w, conv_dw_b, conv_ln_g, conv_ln_b, ssm_conv_w, ssm_conv_b, ssm_dt_bias, ssm_a_log, ssm_d, ssm_norm, hyb_w_out, att_w_qkv, att_b_qkv, att_sinks, att_w_o, att_b_o, final_norm)
    moments_m = (m_norm_ffn1, m_ffn1_w_in, m_ffn1_w_out, m_norm_mix, m_norm_ffn2, m_ffn2_w_in, m_ffn2_w_out, m_ple_norm, m_ple_gate_w, m_ple_proj_w, m_hyb_w_in, m_conv_dw_w, m_conv_dw_b, m_conv_ln_g, m_conv_ln_b, m_ssm_conv_w, m_ssm_conv_b, m_ssm_dt_bias, m_ssm_a_log, m_ssm_d, m_ssm_norm, m_hyb_w_out, m_att_w_qkv, m_att_b_qkv, m_att_sinks, m_att_w_o, m_att_b_o, m_final_norm)
    moments_v = (v_norm_ffn1, v_ffn1_w_in, v_ffn1_w_out, v_norm_mix, v_norm_ffn2, v_ffn2_w_in, v_ffn2_w_out, v_ple_norm, v_ple_gate_w, v_ple_proj_w, v_hyb_w_in, v_conv_dw_w, v_conv_dw_b, v_conv_ln_g, v_conv_ln_b, v_ssm_conv_w, v_ssm_conv_b, v_ssm_dt_bias, v_ssm_a_log, v_ssm_d, v_ssm_norm, v_hyb_w_out, v_att_w_qkv, v_att_b_qkv, v_att_sinks, v_att_w_o, v_att_b_o, v_final_norm)
    w = dict(zip(WEIGHTS, args))
    m = dict(zip(WEIGHTS, moments_m))
    v = dict(zip(WEIGHTS, moments_v))
    cx, cy, cc = _place()
    me = 4 * cx + 2 * cy + cc

    core = jnp.reshape(cc, (1,)).astype(jnp.int32)
    layer_of = lambda n, i: 1 if n.startswith("att_") else i
    keys = [[(n, i) for n in BIG for i in range(w[n].shape[0]) if layer_of(n, i) == layer] for layer in range(2)]

    first = [key for key in keys[0] if key[0].startswith("ffn1")]
    mixer = [key for key in keys[0] if key[0].startswith("hyb")]
    rest0 = [key for key in keys[0] if key not in first + mixer]
    gw, by_chip = {}, {}
    view = lambda a, n: jnp.swapaxes(a, 1, 2) if n in TRANSPOSED else a
    block = lambda n, i: view(w[n], n)[i].astype(BF16)

    def gather_later(group, name):
        blocks = [block(n, i) for n, i in group]
        plan, relay_plan = _gather_plan(len(blocks)), _relay_plan(len(blocks))
        state, _ = _exchange_start(blocks, [SDS((N_DEV,) + b.shape, BF16) for b in blocks], plan, f"{name}_start")
        stage = {}

        def relay():
            _, landed = _exchange_wait(state, None, plan, f"{name}_wait")
            stage["relay"], _ = _exchange_start([], None, relay_plan, f"{name}_relay_start", lands=landed)

        def arrived():
            gw.update(zip(group, _exchange_wait(stage["relay"], None, relay_plan, f"{name}_relay_wait")[1]))

        return relay, arrived

    def reduce_later(group, big, name):
        pair_plan, chip_plan = _pair_plan(len(group)), _chip_plan(len(group))
        parts = [big[key] for key in group]
        pair, token = _exchange_start(parts, [SDS((4,) + pt.shape[1:], BF16) for pt in parts], pair_plan, f"{name}_pair_start")
        stage = {}

        def middle(after):
            thru, got = _exchange_wait(pair, after, pair_plan, f"{name}_pair_wait")
            sums = [_pair_add(pt, gt, core, f"grads_pair_add_{n}_{i}") for pt, gt, (n, i) in zip(thru, got, group)]
            stage["chip"], chip_token = _exchange_start(sums, [SDS(s.shape, BF16) for s in sums], chip_plan, f"{name}_chip_start")
            return chip_token

        def finish(after):
            by_chip.update(zip(group, _exchange_wait(stage["chip"], after, chip_plan, f"{name}_chip_wait")[1]))

        return token, middle, finish

    _restart_chain()
    gw["ffn1_w_in", 0], gathered_small = _all_gather([block("ffn1_w_in", 0), _pack([w[n] for n in SMALL_SHARDED])],
                                                      pltpu.HBM, "gather_weights_first")
    early_relay, early_arrived = gather_later([("ffn1_w_out", 0)], "gather_weights_early")
    mixer_relay, mixer_arrived = gather_later(mixer, "gather_weights_mixer")
    rest0_relay, rest0_arrived = gather_later(rest0, "gather_weights_rest")
    layer1_relay, layer1_arrived = gather_later(keys[1], "gather_weights_l1")
    gs = dict(zip(SMALL_SHARDED, _unpack(gathered_small, [w[n].shape for n in SMALL_SHARDED], lead=1)))
    rep = {n: w[n] for n in SMALL if n not in SMALL_SHARDED}

    tables = _rope_tables(x.shape[1])
    pb = p[:, 0].astype(BF16)
    w0, s0 = {}, {}

    def first_w_out():
        early_relay()
        early_arrived()
        w0.update(_build_layer(0, gw, gs, rep, parts=("ffn1",)))
        return w0["ffn1_out"]

    h, s0["ffn1"] = _ffn_fwd(x[0], rep["norm_ffn1"][0][None], gw["ffn1_w_in", 0], first_w_out, "l0_ffn1")
    mixer_relay()
    mixer_arrived()
    w0.update(_build_layer(0, gw, gs, rep, parts=("mix",)))
    h, s0["mix"] = _hyb_fwd(h, w0, "l0_hyb", after_in=rest0_relay)
    layer1_relay()
    rest0_arrived()
    w0.update(_build_layer(0, gw, gs, rep, parts=("ffn2", "ple")))
    h, s0["ffn2"] = _ffn_fwd(h, w0["norm_ffn2"], w0["ffn2_in"], w0["ffn2_out"], "l0_ffn2")
    h, s0["ple"] = _ple_fwd(h, w0["ple_norm"], pb[0], w0["ple_gate"], w0["ple_proj"], "l0_ple")
    layer1_arrived()
    w1 = _build_layer(1, gw, gs, rep)
    h, s1 = _layer_fwd(1, h, w1, pb[1], tables)
    loss, dh, dhb, d_final = _loss_head(h, loss_target[0], final_norm[None], "loss_head")
    loss = lax.psum(loss[0, 0], ("x", "y", "c"))

    dh, dhb, head1 = _layer_bwd_head(1, dh, dhb, s1, w1, pb[1])
    dh, dhb, tail1 = _layer_bwd_tail(1, dh, dhb, s1, w1, tables)
    grads1 = {**head1, **tail1}
    l1_token, l1_middle, l1_finish = reduce_later(keys[1], _big_grads(1, grads1), "grads_l1")
    dh, dhb, grads0 = _layer_bwd_head(0, dh, dhb, s0, w0, pb[0], deps=(l1_token,))
    dh, dhb, grads0["norm_ffn2"], grads0["ffn2_in"], grads0["ffn2_out"] = _ffn_bwd(
        dh, dhb, s0["ffn2"], w0["norm_ffn2"], w0["ffn2_in"], w0["ffn2_out"], "l0_ffn2", deps=(l1_middle(dh),))
    dh, dhb, mixer_grads = _hyb_bwd(dh, dhb, s0["mix"], w0, "l0_hyb")
    grads0.update(mixer_grads)
    l0_token, l0_middle, l0_finish = reduce_later(mixer + rest0, _big_grads(0, grads0), "grads_l0")
    last = {}

    def reduce_first(dw_in, dw_out):
        token, middle, last["finish"] = reduce_later(first, _big_grads(0, dict(ffn1_in=dw_in, ffn1_out=dw_out)), "grads_first")
        middle(token)

    dx, dhb, grads0["norm_ffn1"], grads0["ffn1_in"], grads0["ffn1_out"] = _ffn_bwd(
        dh, dhb, s0["ffn1"], w0["norm_ffn1"], w0["ffn1_in"], w0["ffn1_out"], "l0_ffn1", deps=(l0_token,),
        hook=lambda dpre: (l0_middle(dpre),), weights_hook=reduce_first)
    l1_finish(dx)
    l0_finish(dx)
    last["finish"](dx)
    _, small = _collect_grads([grads0, grads1], d_final)
    small_shapes = [small[n].shape for n in SMALL]
    all_small = _all_gather([_pack([small[n] for n in SMALL])], pltpu.VMEM, "gather_small_grads")[0]
    g = dict(zip(SMALL, _unpack(_sum_slots(all_small, "small_grads_sum"), small_shapes)))
    for n, axis in SMALL_SHARDED.items():
        g[n] = lax.dynamic_slice_in_dim(g[n], me * w[n].shape[axis], w[n].shape[axis], axis=axis)

    delta, new_m, new_v = {}, {}, {}
    for n in BIG:
        outs = _adamw_summed(view(w[n], n), view(m[n], n), view(v[n], n), [by_chip[n, i] for i in range(w[n].shape[0])],
                             f"adamw_{n}")
        g[n], delta[n], new_m[n], new_v[n] = (view(o, n) for o in outs)
    packed = [_pack([d[n] for n in SMALL]) for d in (w, g, m, v)]
    shapes = [w[n].shape for n in SMALL]
    for d, buf in zip((delta, new_m, new_v), _adamw(*packed, "adamw_small")):
        d.update(zip(SMALL, _unpack(buf, shapes)))
    return (loss, dx[None], *[g[n] for n in WEIGHTS], *[delta[n] for n in WEIGHTS], *[new_m[n] for n in WEIGHTS],
            *[new_v[n] for n in WEIGHTS])
```
